```python
import jax, jax.numpy as jnp
from jax import lax
import numpy as np

D_MODEL = 1024
BATCH = 32
SEQ = 2048
DEPTH = 1

HEAD_DIM = 64
N_HEADS = D_MODEL // HEAD_DIM
N_HEADS_A = N_HEADS // 2
N_HEADS_B = N_HEADS - N_HEADS_A
WIDTH_A = N_HEADS_A * HEAD_DIM
WIDTH_B = N_HEADS_B * HEAD_DIM
DILATED_PATTERNS = ((128, 1), (512, 4), (2048, 16))
Q_BLOCK = 128
ROT_DIM = HEAD_DIM // 4
ROPE_THETA = 500000.0
D_FF = ((8 * D_MODEL // 3 + 63) // 64) * 64
N_MOD = 9
EPS = 1e-6
ATTN_SCALE = HEAD_DIM ** -0.5
NEG = -1e30
COL_SIZES = (WIDTH_A, WIDTH_A, WIDTH_A, WIDTH_B, WIDTH_B, WIDTH_B, N_HEADS_B)
COL_OFFSETS = tuple(int(o) for o in np.cumsum(COL_SIZES)[:-1])
IN_COLS = int(sum(COL_SIZES))

kernel_name = 'hybrid_dilated_fox_macaron_block'


def rmsnorm(x, g):
    xf = x.astype(jnp.float32)
    y = xf * lax.rsqrt(jnp.mean(xf * xf, axis=-1, keepdims=True) + EPS)
    return (y * g.astype(jnp.float32)).astype(x.dtype)


def partial_rotary(t, positions):
    inv_freq = ROPE_THETA ** (-jnp.arange(0, ROT_DIM, 2, dtype=jnp.float32) / ROT_DIM)
    ang = positions.astype(jnp.float32)[:, None, :, None] * inv_freq
    cos, sin = jnp.cos(ang), jnp.sin(ang)
    tf = t.astype(jnp.float32)
    x1 = tf[..., :ROT_DIM // 2]
    x2 = tf[..., ROT_DIM // 2:ROT_DIM]
    rot = jnp.concatenate([x1 * cos - x2 * sin, x2 * cos + x1 * sin, tf[..., ROT_DIM:]], axis=-1)
    return rot.astype(t.dtype)


def swiglu(h, w_gate, w_up, w_down):
    return (jax.nn.silu(h @ w_gate) * (h @ w_up)) @ w_down


def banded_causal_attention(q, k, v, w):
    L = q.shape[-2]
    lead = q.shape[:-2]
    nb = -(-L // Q_BLOCK)
    Lp = nb * Q_BLOCK
    pad = [(0, 0)] * (q.ndim - 2)
    qp = jnp.pad(q, pad + [(0, Lp - L), (0, 0)])
    kp = jnp.pad(k, pad + [(w, Lp - L), (0, 0)])
    vp = jnp.pad(v, pad + [(w, Lp - L), (0, 0)])
    span = Q_BLOCK + w
    q_blk = qp.reshape(*lead, nb, Q_BLOCK, HEAD_DIM)
    idx = jnp.arange(nb)[:, None] * Q_BLOCK + jnp.arange(span)[None, :]
    k_blk = jnp.take(kp, idx, axis=-2)
    v_blk = jnp.take(vp, idx, axis=-2)
    s = jnp.einsum('...nqd,...nkd->...nqk', q_blk, k_blk,
                   preferred_element_type=jnp.float32) * ATTN_SCALE
    dist = jnp.arange(Q_BLOCK)[:, None] + w - jnp.arange(span)[None, :]
    key_pos = idx - w
    valid = ((dist >= 0) & (dist <= w))[None] & (key_pos >= 0)[:, None, :]
    s = jnp.where(valid, s, NEG)
    lse = jax.nn.logsumexp(s, axis=-1)
    p = jnp.exp(s - lse[..., None])
    o = jnp.einsum('...nqk,...nkd->...nqd', p.astype(v.dtype), v_blk)
    o = o.reshape(*lead, Lp, HEAD_DIM)[..., :L, :]
    lse = lse.reshape(*lead, Lp)[..., :L]
    return o, lse


def dilated_mixture_attention(q, k, v):
    B, H, S, hd = q.shape
    outs, lses = [], []
    for window, d in DILATED_PATTERNS:
        w_sub = window // d
        to_cls = lambda t: t.reshape(B, H, S // d, d, hd).swapaxes(2, 3)
        o, lse = banded_causal_attention(to_cls(q), to_cls(k), to_cls(v), w_sub)
        outs.append(o.swapaxes(2, 3).reshape(B, H, S, hd))
        lses.append(lse.swapaxes(2, 3).reshape(B, H, S))
    alpha = jax.nn.softmax(jnp.stack(lses, axis=0), axis=0)
    return jnp.einsum('pbhs,pbhsd->bhsd', alpha.astype(q.dtype), jnp.stack(outs, axis=0))


def forgetting_attention(q, k, v, f_logit):
    S = q.shape[2]
    log_f = jax.nn.log_sigmoid(f_logit.astype(jnp.float32)).transpose(0, 2, 1)
    F = lax.cumsum(log_f, axis=2)
    outs = []
    for i in range(S // Q_BLOCK):
        lo, hi = i * Q_BLOCK, (i + 1) * Q_BLOCK
        s = jnp.einsum('bhqd,bhkd->bhqk', q[:, :, lo:hi], k[:, :, :hi],
                       preferred_element_type=jnp.float32) * ATTN_SCALE
        s = s + F[:, :, lo:hi, None] - F[:, :, None, :hi]
        causal = (lo + jnp.arange(Q_BLOCK))[:, None] >= jnp.arange(hi)[None, :]
        p = jax.nn.softmax(jnp.where(causal, s, NEG), axis=-1)
        outs.append(jnp.einsum('bhqk,bhkd->bhqd', p.astype(v.dtype), v[:, :, :hi]))
    return jnp.concatenate(outs, axis=2)


def hybrid_mixer(h, positions, w_in, b_forget, g_out_a, g_out_b, w_out):
    B, S, _ = h.shape
    proj = h @ w_in
    qa, ka, va, qb, kb, vb, f_logit = jnp.split(proj, COL_OFFSETS, axis=-1)
    heads = lambda t, n: t.reshape(B, S, n, HEAD_DIM).transpose(0, 2, 1, 3)
    qa = partial_rotary(heads(qa, N_HEADS_A), positions)
    ka = partial_rotary(heads(ka, N_HEADS_A), positions)
    out_a = dilated_mixture_attention(qa, ka, heads(va, N_HEADS_A))
    out_b = forgetting_attention(heads(qb, N_HEADS_B), heads(kb, N_HEADS_B), heads(vb, N_HEADS_B),
                                 f_logit + b_forget)
    flat = lambda t: t.transpose(0, 2, 1, 3).reshape(B, S, -1)
    merged = jnp.concatenate([rmsnorm(flat(out_a), g_out_a), rmsnorm(flat(out_b), g_out_b)], axis=-1)
    return merged @ w_out


def _fwd_setup_inputs(seed: int = 0) -> dict:
    key = jax.random.key(seed)
    ks = jax.random.split(key, 24)
    nrm = lambda k, shape, s: jax.random.normal(k, shape, jnp.float32) * s
    gain = lambda k, n: 1.0 + 0.05 * jax.random.normal(k, (DEPTH, n), jnp.float32)
    return {
        'x': nrm(ks[0], (BATCH, SEQ, D_MODEL), 1.0),
        'c': nrm(ks[1], (BATCH, D_MODEL), 1.0),
        'positions': jnp.broadcast_to(jnp.arange(SEQ, dtype=jnp.int32)[None, :], (BATCH, SEQ)),
        'w_ada': nrm(ks[2], (DEPTH, D_MODEL, N_MOD * D_MODEL), 0.01),
        'b_ada': nrm(ks[3], (DEPTH, N_MOD * D_MODEL), 0.02),
        'g_pre_ff1': gain(ks[4], D_MODEL),
        'g_post_ff1': gain(ks[5], D_MODEL),
        'w_ff1_gate': nrm(ks[6], (DEPTH, D_MODEL, D_FF), D_MODEL ** -0.5),
        'w_ff1_up': nrm(ks[7], (DEPTH, D_MODEL, D_FF), D_MODEL ** -0.5),
        'w_ff1_down': nrm(ks[8], (DEPTH, D_FF, D_MODEL), D_FF ** -0.5),
        'g_pre_mix': gain(ks[9], D_MODEL),
        'g_post_mix': gain(ks[10], D_MODEL),
        'w_in': nrm(ks[11], (DEPTH, D_MODEL, IN_COLS), D_MODEL ** -0.5),
        'b_forget': jax.random.uniform(ks[12], (DEPTH, N_HEADS_B), jnp.float32, 1.0, 4.0),
        'g_out_a': gain(ks[13], WIDTH_A),
        'g_out_b': gain(ks[14], WIDTH_B),
        'w_out': nrm(ks[15], (DEPTH, D_MODEL, D_MODEL), D_MODEL ** -0.5),
        'g_pre_ff2': gain(ks[16], D_MODEL),
        'g_post_ff2': gain(ks[17], D_MODEL),
        'w_ff2_gate': nrm(ks[18], (DEPTH, D_MODEL, D_FF), D_MODEL ** -0.5),
        'w_ff2_up': nrm(ks[19], (DEPTH, D_MODEL, D_FF), D_MODEL ** -0.5),
        'w_ff2_down': nrm(ks[20], (DEPTH, D_FF, D_MODEL), D_FF ** -0.5),
    }


def _fwd_reference(x, c, positions, w_ada, b_ada, g_pre_ff1, g_post_ff1, w_ff1_gate, w_ff1_up, w_ff1_down,
              g_pre_mix, g_post_mix, w_in, b_forget, g_out_a, g_out_b, w_out,
              g_pre_ff2, g_post_ff2, w_ff2_gate, w_ff2_up, w_ff2_down):
    B = x.shape[0]
    silu_c = jax.nn.silu(c)
    for l in range(DEPTH):
        mod = (silu_c @ w_ada[l] + b_ada[l]).reshape(B, N_MOD, D_MODEL)
        m = lambda i: mod[:, i][:, None, :]
        h = rmsnorm(x, g_pre_ff1[l]) * (1.0 + m(1)) + m(0)
        y = rmsnorm(swiglu(h, w_ff1_gate[l], w_ff1_up[l], w_ff1_down[l]), g_post_ff1[l])
        x = x + 0.5 * m(2) * y
        h = rmsnorm(x, g_pre_mix[l]) * (1.0 + m(4)) + m(3)
        y = rmsnorm(hybrid_mixer(h, positions, w_in[l], b_forget[l], g_out_a[l], g_out_b[l], w_out[l]),
                    g_post_mix[l])
        x = x + m(5) * y
        h = rmsnorm(x, g_pre_ff2[l]) * (1.0 + m(7)) + m(6)
        y = rmsnorm(swiglu(h, w_ff2_gate[l], w_ff2_up[l], w_ff2_down[l]), g_post_ff2[l])
        x = x + 0.5 * m(8) * y
    return x


import jax as _jax
import jax.numpy as _jnp

TWIN_FORMAT = 'train_step'
FWD_PARAMS = ['x', 'c', 'positions', 'w_ada', 'b_ada', 'g_pre_ff1', 'g_post_ff1', 'w_ff1_gate', 'w_ff1_up', 'w_ff1_down', 'g_pre_mix', 'g_post_mix', 'w_in', 'b_forget', 'g_out_a', 'g_out_b', 'w_out', 'g_pre_ff2', 'g_post_ff2', 'w_ff2_gate', 'w_ff2_up', 'w_ff2_down']
TWIN_WEIGHTS = ['w_ada', 'b_ada', 'g_pre_ff1', 'g_post_ff1', 'w_ff1_gate', 'w_ff1_up', 'w_ff1_down', 'g_pre_mix', 'g_post_mix', 'w_in', 'b_forget', 'g_out_a', 'g_out_b', 'w_out', 'g_pre_ff2', 'g_post_ff2', 'w_ff2_gate', 'w_ff2_up', 'w_ff2_down']
TWIN_DIFF_INPUT = 'x'
TWIN_INPUTS = ['x', 'c', 'positions', 'w_ada', 'b_ada', 'g_pre_ff1', 'g_post_ff1', 'w_ff1_gate', 'w_ff1_up', 'w_ff1_down', 'g_pre_mix', 'g_post_mix', 'w_in', 'b_forget', 'g_out_a', 'g_out_b', 'w_out', 'g_pre_ff2', 'g_post_ff2', 'w_ff2_gate', 'w_ff2_up', 'w_ff2_down', 'loss_target', 'm_w_ada', 'm_b_ada', 'm_g_pre_ff1', 'm_g_post_ff1', 'm_w_ff1_gate', 'm_w_ff1_up', 'm_w_ff1_down', 'm_g_pre_mix', 'm_g_post_mix', 'm_w_in', 'm_b_forget', 'm_g_out_a', 'm_g_out_b', 'm_w_out', 'm_g_pre_ff2', 'm_g_post_ff2', 'm_w_ff2_gate', 'm_w_ff2_up', 'm_w_ff2_down', 'v_w_ada', 'v_b_ada', 'v_g_pre_ff1', 'v_g_post_ff1', 'v_w_ff1_gate', 'v_w_ff1_up', 'v_w_ff1_down', 'v_g_pre_mix', 'v_g_post_mix', 'v_w_in', 'v_b_forget', 'v_g_out_a', 'v_g_out_b', 'v_w_out', 'v_g_pre_ff2', 'v_g_post_ff2', 'v_w_ff2_gate', 'v_w_ff2_up', 'v_w_ff2_down']
TWIN_OUTPUTS = ['loss', 'grad_x', 'grad_w_ada', 'grad_b_ada', 'grad_g_pre_ff1', 'grad_g_post_ff1', 'grad_w_ff1_gate', 'grad_w_ff1_up', 'grad_w_ff1_down', 'grad_g_pre_mix', 'grad_g_post_mix', 'grad_w_in', 'grad_b_forget', 'grad_g_out_a', 'grad_g_out_b', 'grad_w_out', 'grad_g_pre_ff2', 'grad_g_post_ff2', 'grad_w_ff2_gate', 'grad_w_ff2_up', 'grad_w_ff2_down', 'delta_w_ada', 'delta_b_ada', 'delta_g_pre_ff1', 'delta_g_post_ff1', 'delta_w_ff1_gate', 'delta_w_ff1_up', 'delta_w_ff1_down', 'delta_g_pre_mix', 'delta_g_post_mix', 'delta_w_in', 'delta_b_forget', 'delta_g_out_a', 'delta_g_out_b', 'delta_w_out', 'delta_g_pre_ff2', 'delta_g_post_ff2', 'delta_w_ff2_gate', 'delta_w_ff2_up', 'delta_w_ff2_down', 'new_m_w_ada', 'new_m_b_ada', 'new_m_g_pre_ff1', 'new_m_g_post_ff1', 'new_m_w_ff1_gate', 'new_m_w_ff1_up', 'new_m_w_ff1_down', 'new_m_g_pre_mix', 'new_m_g_post_mix', 'new_m_w_in', 'new_m_b_forget', 'new_m_g_out_a', 'new_m_g_out_b', 'new_m_w_out', 'new_m_g_pre_ff2', 'new_m_g_post_ff2', 'new_m_w_ff2_gate', 'new_m_w_ff2_up', 'new_m_w_ff2_down', 'new_v_w_ada', 'new_v_b_ada', 'new_v_g_pre_ff1', 'new_v_g_post_ff1', 'new_v_w_ff1_gate', 'new_v_w_ff1_up', 'new_v_w_ff1_down', 'new_v_g_pre_mix', 'new_v_g_post_mix', 'new_v_w_in', 'new_v_b_forget', 'new_v_g_out_a', 'new_v_g_out_b', 'new_v_w_out', 'new_v_g_pre_ff2', 'new_v_g_post_ff2', 'new_v_w_ff2_gate', 'new_v_w_ff2_up', 'new_v_w_ff2_down']
TWIN_LEAF_KINDS = {'loss': 'loss', 'grad_x': 'grad_x', 'grad_w_ada': 'grad_w', 'grad_b_ada': 'grad_w', 'grad_g_pre_ff1': 'grad_w', 'grad_g_post_ff1': 'grad_w', 'grad_w_ff1_gate': 'grad_w', 'grad_w_ff1_up': 'grad_w', 'grad_w_ff1_down': 'grad_w', 'grad_g_pre_mix': 'grad_w', 'grad_g_post_mix': 'grad_w', 'grad_w_in': 'grad_w', 'grad_b_forget': 'grad_w', 'grad_g_out_a': 'grad_w', 'grad_g_out_b': 'grad_w', 'grad_w_out': 'grad_w', 'grad_g_pre_ff2': 'grad_w', 'grad_g_post_ff2': 'grad_w', 'grad_w_ff2_gate': 'grad_w', 'grad_w_ff2_up': 'grad_w', 'grad_w_ff2_down': 'grad_w', 'delta_w_ada': 'delta_w', 'delta_b_ada': 'delta_w', 'delta_g_pre_ff1': 'delta_w', 'delta_g_post_ff1': 'delta_w', 'delta_w_ff1_gate': 'delta_w', 'delta_w_ff1_up': 'delta_w', 'delta_w_ff1_down': 'delta_w', 'delta_g_pre_mix': 'delta_w', 'delta_g_post_mix': 'delta_w', 'delta_w_in': 'delta_w', 'delta_b_forget': 'delta_w', 'delta_g_out_a': 'delta_w', 'delta_g_out_b': 'delta_w', 'delta_w_out': 'delta_w', 'delta_g_pre_ff2': 'delta_w', 'delta_g_post_ff2': 'delta_w', 'delta_w_ff2_gate': 'delta_w', 'delta_w_ff2_up': 'delta_w', 'delta_w_ff2_down': 'delta_w', 'new_m_w_ada': 'new_m', 'new_m_b_ada': 'new_m', 'new_m_g_pre_ff1': 'new_m', 'new_m_g_post_ff1': 'new_m', 'new_m_w_ff1_gate': 'new_m', 'new_m_w_ff1_up': 'new_m', 'new_m_w_ff1_down': 'new_m', 'new_m_g_pre_mix': 'new_m', 'new_m_g_post_mix': 'new_m', 'new_m_w_in': 'new_m', 'new_m_b_forget': 'new_m', 'new_m_g_out_a': 'new_m', 'new_m_g_out_b': 'new_m', 'new_m_w_out': 'new_m', 'new_m_g_pre_ff2': 'new_m', 'new_m_g_post_ff2': 'new_m', 'new_m_w_ff2_gate': 'new_m', 'new_m_w_ff2_up': 'new_m', 'new_m_w_ff2_down': 'new_m', 'new_v_w_ada': 'new_v', 'new_v_b_ada': 'new_v', 'new_v_g_pre_ff1': 'new_v', 'new_v_g_post_ff1': 'new_v', 'new_v_w_ff1_gate': 'new_v', 'new_v_w_ff1_up': 'new_v', 'new_v_w_ff1_down': 'new_v', 'new_v_g_pre_mix': 'new_v', 'new_v_g_post_mix': 'new_v', 'new_v_w_in': 'new_v', 'new_v_b_forget': 'new_v', 'new_v_g_out_a': 'new_v', 'new_v_g_out_b': 'new_v', 'new_v_w_out': 'new_v', 'new_v_g_pre_ff2': 'new_v', 'new_v_g_post_ff2': 'new_v', 'new_v_w_ff2_gate': 'new_v', 'new_v_w_ff2_up': 'new_v', 'new_v_w_ff2_down': 'new_v'}


def _forward(args):
    return _fwd_reference(*[args[k] for k in FWD_PARAMS])


def _output_shape():
    out = _jax.eval_shape(lambda: _forward(_fwd_setup_inputs(0)))
    return out.shape, out.dtype

N_MICROBATCH = 1
ADAM_LR = 0.001
ADAM_B1 = 0.9
ADAM_B2 = 0.999
ADAM_EPS = 1e-08
ADAM_WD = 0.01
ADAM_STEP = 10
PER_EXAMPLE_BATCH_AXIS = {'x': 0, 'c': 0, 'positions': 0, 'loss_target': 0}
SHARED_INPUTS = []
_WEIGHT_DTYPES = {'w_ada': _jnp.float32, 'b_ada': _jnp.float32, 'g_pre_ff1': _jnp.float32, 'g_post_ff1': _jnp.float32, 'w_ff1_gate': _jnp.float32, 'w_ff1_up': _jnp.float32, 'w_ff1_down': _jnp.float32, 'g_pre_mix': _jnp.float32, 'g_post_mix': _jnp.float32, 'w_in': _jnp.float32, 'b_forget': _jnp.float32, 'g_out_a': _jnp.float32, 'g_out_b': _jnp.float32, 'w_out': _jnp.float32, 'g_pre_ff2': _jnp.float32, 'g_post_ff2': _jnp.float32, 'w_ff2_gate': _jnp.float32, 'w_ff2_up': _jnp.float32, 'w_ff2_down': _jnp.float32}
MOMENT_SCALE = {'w_ada': 1.771429e+00, 'b_ada': 3.296225e+00, 'g_pre_ff1': 5.914047e-02, 'g_post_ff1': 6.766192e-01, 'w_ff1_gate': 2.478626e-02, 'w_ff1_up': 2.812611e-02, 'w_ff1_down': 4.634165e-02, 'g_pre_mix': 1.006369e-01, 'g_post_mix': 3.338251e+00, 'w_in': 3.303909e-01, 'b_forget': 8.571430e-01, 'g_out_a': 9.613925e-01, 'g_out_b': 3.761067e-01, 'w_out': 7.197352e-01, 'g_pre_ff2': 5.226930e-02, 'g_post_ff2': 7.203679e-01, 'w_ff2_gate': 2.432348e-02, 'w_ff2_up': 2.752851e-02, 'w_ff2_down': 4.515397e-02}


def _to_microbatches(a, axis):
    t = _jnp.moveaxis(a, axis, 0)
    t = t.reshape((N_MICROBATCH, t.shape[0] // N_MICROBATCH) + t.shape[1:])
    return _jnp.moveaxis(t, 1, axis + 1)


def setup_inputs(seed: int = 0) -> dict:
    inp = _fwd_setup_inputs(seed)
    key = _jax.random.fold_in(_jax.random.key(seed), 7919)
    shape, _ = _output_shape()
    out = dict(inp)
    out["loss_target"] = _jax.random.normal(_jax.random.fold_in(key, 0), shape, _jnp.float32)
    for i, name in enumerate(TWIN_WEIGHTS):
        w = inp[name].astype(_jnp.float32)
        if MOMENT_SCALE is None:
            s = _jnp.sqrt(_jnp.mean(_jnp.square(w)) + 1e-30)
        else:
            s = MOMENT_SCALE[name]
        km, kv = _jax.random.split(_jax.random.fold_in(key, i + 1))
        out[name] = w
        out["m_" + name] = s * _jax.random.normal(km, w.shape, _jnp.float32)
        out["v_" + name] = (s * s) * _jax.random.uniform(kv, w.shape, _jnp.float32, 0.5, 1.5)
    if N_MICROBATCH > 1:
        for name, axis in PER_EXAMPLE_BATCH_AXIS.items():
            out[name] = _to_microbatches(out[name], axis)
    return {'x': out['x'], 'c': out['c'], 'positions': out['positions'], 'w_ada': out['w_ada'], 'b_ada': out['b_ada'], 'g_pre_ff1': out['g_pre_ff1'], 'g_post_ff1': out['g_post_ff1'], 'w_ff1_gate': out['w_ff1_gate'], 'w_ff1_up': out['w_ff1_up'], 'w_ff1_down': out['w_ff1_down'], 'g_pre_mix': out['g_pre_mix'], 'g_post_mix': out['g_post_mix'], 'w_in': out['w_in'], 'b_forget': out['b_forget'], 'g_out_a': out['g_out_a'], 'g_out_b': out['g_out_b'], 'w_out': out['w_out'], 'g_pre_ff2': out['g_pre_ff2'], 'g_post_ff2': out['g_post_ff2'], 'w_ff2_gate': out['w_ff2_gate'], 'w_ff2_up': out['w_ff2_up'], 'w_ff2_down': out['w_ff2_down'], 'loss_target': out['loss_target'], 'm_w_ada': out['m_w_ada'], 'm_b_ada': out['m_b_ada'], 'm_g_pre_ff1': out['m_g_pre_ff1'], 'm_g_post_ff1': out['m_g_post_ff1'], 'm_w_ff1_gate': out['m_w_ff1_gate'], 'm_w_ff1_up': out['m_w_ff1_up'], 'm_w_ff1_down': out['m_w_ff1_down'], 'm_g_pre_mix': out['m_g_pre_mix'], 'm_g_post_mix': out['m_g_post_mix'], 'm_w_in': out['m_w_in'], 'm_b_forget': out['m_b_forget'], 'm_g_out_a': out['m_g_out_a'], 'm_g_out_b': out['m_g_out_b'], 'm_w_out': out['m_w_out'], 'm_g_pre_ff2': out['m_g_pre_ff2'], 'm_g_post_ff2': out['m_g_post_ff2'], 'm_w_ff2_gate': out['m_w_ff2_gate'], 'm_w_ff2_up': out['m_w_ff2_up'], 'm_w_ff2_down': out['m_w_ff2_down'], 'v_w_ada': out['v_w_ada'], 'v_b_ada': out['v_b_ada'], 'v_g_pre_ff1': out['v_g_pre_ff1'], 'v_g_post_ff1': out['v_g_post_ff1'], 'v_w_ff1_gate': out['v_w_ff1_gate'], 'v_w_ff1_up': out['v_w_ff1_up'], 'v_w_ff1_down': out['v_w_ff1_down'], 'v_g_pre_mix': out['v_g_pre_mix'], 'v_g_post_mix': out['v_g_post_mix'], 'v_w_in': out['v_w_in'], 'v_b_forget': out['v_b_forget'], 'v_g_out_a': out['v_g_out_a'], 'v_g_out_b': out['v_g_out_b'], 'v_w_out': out['v_w_out'], 'v_g_pre_ff2': out['v_g_pre_ff2'], 'v_g_post_ff2': out['v_g_post_ff2'], 'v_w_ff2_gate': out['v_w_ff2_gate'], 'v_w_ff2_up': out['v_w_ff2_up'], 'v_w_ff2_down': out['v_w_ff2_down']}


def _loss(weights, diff, rest, loss_target):
    with _jax.named_scope("forward"):
        args = {**rest, TWIN_DIFF_INPUT: diff, **{k: w.astype(_WEIGHT_DTYPES[k]) for k, w in weights.items()}}
        y = _forward(args)
    with _jax.named_scope("loss_head"):
        err = _jnp.square(y.astype(_jnp.float32) - loss_target)
        return 0.5 * _jnp.sum(_jnp.mean(err, axis=-1)) if err.ndim else 0.5 * err


def _adamw(w, g, m, v):
    m = ADAM_B1 * m + (1.0 - ADAM_B1) * g
    v = ADAM_B2 * v + (1.0 - ADAM_B2) * _jnp.square(g)
    m_hat = m / (1.0 - ADAM_B1 ** ADAM_STEP)
    v_hat = v / (1.0 - ADAM_B2 ** ADAM_STEP)
    delta = -ADAM_LR * (m_hat / (_jnp.sqrt(v_hat) + ADAM_EPS) + ADAM_WD * w)
    return delta, m, v


def reference(x, c, positions, w_ada, b_ada, g_pre_ff1, g_post_ff1, w_ff1_gate, w_ff1_up, w_ff1_down, g_pre_mix, g_post_mix, w_in, b_forget, g_out_a, g_out_b, w_out, g_pre_ff2, g_post_ff2, w_ff2_gate, w_ff2_up, w_ff2_down, loss_target, m_w_ada, m_b_ada, m_g_pre_ff1, m_g_post_ff1, m_w_ff1_gate, m_w_ff1_up, m_w_ff1_down, m_g_pre_mix, m_g_post_mix, m_w_in, m_b_forget, m_g_out_a, m_g_out_b, m_w_out, m_g_pre_ff2, m_g_post_ff2, m_w_ff2_gate, m_w_ff2_up, m_w_ff2_down, v_w_ada, v_b_ada, v_g_pre_ff1, v_g_post_ff1, v_w_ff1_gate, v_w_ff1_up, v_w_ff1_down, v_g_pre_mix, v_g_post_mix, v_w_in, v_b_forget, v_g_out_a, v_g_out_b, v_w_out, v_g_pre_ff2, v_g_post_ff2, v_w_ff2_gate, v_w_ff2_up, v_w_ff2_down):
    given = dict(x=x, c=c, positions=positions, w_ada=w_ada, b_ada=b_ada, g_pre_ff1=g_pre_ff1, g_post_ff1=g_post_ff1, w_ff1_gate=w_ff1_gate, w_ff1_up=w_ff1_up, w_ff1_down=w_ff1_down, g_pre_mix=g_pre_mix, g_post_mix=g_post_mix, w_in=w_in, b_forget=b_forget, g_out_a=g_out_a, g_out_b=g_out_b, w_out=w_out, g_pre_ff2=g_pre_ff2, g_post_ff2=g_post_ff2, w_ff2_gate=w_ff2_gate, w_ff2_up=w_ff2_up, w_ff2_down=w_ff2_down, loss_target=loss_target, m_w_ada=m_w_ada, m_b_ada=m_b_ada, m_g_pre_ff1=m_g_pre_ff1, m_g_post_ff1=m_g_post_ff1, m_w_ff1_gate=m_w_ff1_gate, m_w_ff1_up=m_w_ff1_up, m_w_ff1_down=m_w_ff1_down, m_g_pre_mix=m_g_pre_mix, m_g_post_mix=m_g_post_mix, m_w_in=m_w_in, m_b_forget=m_b_forget, m_g_out_a=m_g_out_a, m_g_out_b=m_g_out_b, m_w_out=m_w_out, m_g_pre_ff2=m_g_pre_ff2, m_g_post_ff2=m_g_post_ff2, m_w_ff2_gate=m_w_ff2_gate, m_w_ff2_up=m_w_ff2_up, m_w_ff2_down=m_w_ff2_down, v_w_ada=v_w_ada, v_b_ada=v_b_ada, v_g_pre_ff1=v_g_pre_ff1, v_g_post_ff1=v_g_post_ff1, v_w_ff1_gate=v_w_ff1_gate, v_w_ff1_up=v_w_ff1_up, v_w_ff1_down=v_w_ff1_down, v_g_pre_mix=v_g_pre_mix, v_g_post_mix=v_g_post_mix, v_w_in=v_w_in, v_b_forget=v_b_forget, v_g_out_a=v_g_out_a, v_g_out_b=v_g_out_b, v_w_out=v_w_out, v_g_pre_ff2=v_g_pre_ff2, v_g_post_ff2=v_g_post_ff2, v_w_ff2_gate=v_w_ff2_gate, v_w_ff2_up=v_w_ff2_up, v_w_ff2_down=v_w_ff2_down)
    weights = {n: given[n] for n in TWIN_WEIGHTS}
    shared = {n: given[n] for n in SHARED_INPUTS}
    per_example = {n: given[n] for n in ['x', 'c', 'positions']}
    grad_fn = _jax.value_and_grad(_loss, argnums=(0, 1))

    def one_microbatch(ex, loss_target):
        ex = dict(ex)
        diff = ex.pop(TWIN_DIFF_INPUT)
        return grad_fn(weights, diff, {**shared, **ex}, loss_target)

    if N_MICROBATCH == 1:
        loss, (grad_w, grad_x) = one_microbatch(per_example, given["loss_target"])
    else:
        def body(carry, xs):
            loss_sum, grad_sum = carry
            l_k, (gw_k, gx_k) = one_microbatch(xs[0], xs[1])
            with _jax.named_scope("update"):
                return (loss_sum + l_k, _jax.tree.map(_jnp.add, grad_sum, gw_k)), gx_k

        init = (_jnp.zeros((), _jnp.float32), _jax.tree.map(_jnp.zeros_like, weights))
        (loss, grad_w), grad_x = _jax.lax.scan(body, init, (per_example, given["loss_target"]))
    with _jax.named_scope("update"):
        delta_w, new_m, new_v = {}, {}, {}
        for n in TWIN_WEIGHTS:
            delta_w[n], new_m[n], new_v[n] = _adamw(weights[n], grad_w[n], given["m_" + n], given["v_" + n])
    return (loss, grad_x, *[grad_w[n] for n in TWIN_WEIGHTS], *[delta_w[n] for n in TWIN_WEIGHTS],
            *[new_m[n] for n in TWIN_WEIGHTS], *[new_v[n] for n in TWIN_WEIGHTS])
```

```python
import functools
import math

import jax
import jax.numpy as jnp
from jax import lax
from jax.experimental import pallas as pl
from jax.experimental.pallas import tpu as pltpu

D = 1024
SEQ = 2048
HD = 64
NH = 8
WG = NH * HD
DFF = 2752
DFF_PAD = 2816
IN_MAIN = 6 * WG
IN_COLS = IN_MAIN + NH
N_SHARD = 4
N_DEV = 8
LANE = 128
QB = 128
FB = 256
PATTERNS = ((1, 16), (4, 4), (16, 1))
ROPE_THETA = 500000.0
EPS = 1e-6
NEG = -1e30
ATTN_SCALE = HD ** -0.5
TM = 512
TM_BWD = 256
VMEM_LIMIT = 56 * 1024 * 1024

ADAM_LR, ADAM_B1, ADAM_B2, ADAM_EPS, ADAM_WD, ADAM_STEP = 0.001, 0.9, 0.999, 1e-08, 0.01, 10

F32 = jnp.float32
BF16 = jnp.bfloat16
MESH = pl.DeviceIdType.MESH
SDS = jax.ShapeDtypeStruct


def _cp(*sem):
    return pltpu.CompilerParams(dimension_semantics=sem, vmem_limit_bytes=VMEM_LIMIT)


def _dot(a, b):
    return jnp.dot(a, b, preferred_element_type=F32)


def _dot_nt(a, b):
    return lax.dot_general(a, b, (((1,), (1,)), ((), ())), preferred_element_type=F32)


def _dot_tn(a, b):
    return lax.dot_general(a, b, (((0,), (0,)), ((), ())), preferred_element_type=F32)


def _rms(xf):
    return lax.rsqrt(jnp.mean(xf * xf, axis=-1, keepdims=True) + EPS)


def _norm_mod_bwd(dh, xf, g, scale):
    r = _rms(xf)
    xh = xf * r
    dsh = jnp.sum(dh, axis=0, keepdims=True)
    dsc = jnp.sum(dh * (xh * g), axis=0, keepdims=True)
    dn = dh * (1.0 + scale)
    dg = jnp.sum(dn * xh, axis=0, keepdims=True)
    dxh = dn * g
    dx = r * (dxh - xh * jnp.mean(dxh * xh, axis=-1, keepdims=True))
    return dx, dsh, dsc, dg


def _post_bwd(dxo, y0, g, mgate, gs):
    r = _rms(y0)
    yh = y0 * r
    dmg = gs * jnp.sum(dxo * (yh * g), axis=0, keepdims=True)
    dy = (gs * mgate) * dxo
    dg = jnp.sum(dy * yh, axis=0, keepdims=True)
    dyh = dy * g
    dy0 = r * (dyh - yh * jnp.mean(dyh * yh, axis=-1, keepdims=True))
    return dy0, dmg, dg


def _mod_map(i, *_):
    return ((i * TM) // SEQ, 0, 0)


FF_TN = 1408
FF_NJ = DFF_PAD // FF_TN


def ffn_fwd(x, mod3, g_pre, g_post, wg, wu, wd, gs, name):
    T = x.shape[0]

    def body(x_ref, mod_ref, gpre_ref, gpost_ref, wg_ref, wu_ref, wd_ref, xo_ref, h_ref, gate_ref, up_ref, y0_ref, hs, acc):
        j = pl.program_id(1)

        @pl.when(j == 0)
        def _():
            xf = x_ref[...]
            h = (xf * _rms(xf) * gpre_ref[...]) * (1.0 + mod_ref[0, 1:2, :]) + mod_ref[0, 0:1, :]
            hb = h.astype(BF16)
            hs[...] = hb
            h_ref[...] = hb
            acc[...] = jnp.zeros_like(acc)

        hb = hs[...]
        gate = _dot(hb, wg_ref[...])
        up = _dot(hb, wu_ref[...])
        gate_ref[...] = gate.astype(BF16)
        up_ref[...] = up.astype(BF16)
        act = gate * jax.nn.sigmoid(gate) * up
        acc[...] += _dot(act.astype(BF16), wd_ref[...])

        @pl.when(j == FF_NJ - 1)
        def _():
            y0 = acc[...]
            y0_ref[...] = y0
            xo_ref[...] = x_ref[...] + (gs * mod_ref[0, 2:3, :]) * (y0 * _rms(y0) * gpost_ref[...])

    tok = pl.BlockSpec((TM, D), lambda i, j: (i, 0))
    vec = pl.BlockSpec((1, D), lambda i, j: (0, 0))
    hid = pl.BlockSpec((TM, FF_TN), lambda i, j: (i, j))
    return pl.pallas_call(
        body, grid=(T // TM, FF_NJ),
        in_specs=[tok, pl.BlockSpec((1, 3, D), _mod_map), vec, vec,
                  pl.BlockSpec((D, FF_TN), lambda i, j: (0, j)), pl.BlockSpec((D, FF_TN), lambda i, j: (0, j)),
                  pl.BlockSpec((FF_TN, D), lambda i, j: (j, 0))],
        out_specs=[tok, tok, hid, hid, tok],
        out_shape=[SDS((T, D), F32), SDS((T, D), BF16), SDS((T, DFF_PAD), BF16), SDS((T, DFF_PAD), BF16), SDS((T, D), F32)],
        scratch_shapes=[pltpu.VMEM((TM, D), BF16), pltpu.VMEM((TM, D), F32)],
        compiler_params=_cp("arbitrary", "arbitrary"), name=name,
    )(x, mod3, g_pre, g_post, wg, wu, wd)


def ffn_bwd(dxo, x, y0, mod3, g_pre, g_post, gate, up, wg, wu, wd, gs, name):
    T = x.shape[0]
    nb = T // SEQ
    tm = TM_BWD
    tiles_per_seq = SEQ // tm

    def body(dxo_ref, x_ref, y0_ref, mod_ref, gpre_ref, gpost_ref, gate_ref, up_ref, wg_ref, wu_ref, wd_ref,
             dx_ref, dy0_ref, act_ref, dgate_ref, dup_ref, dmod_ref, dgpre_ref, dgpost_ref, dy0s, acc):
        i = pl.program_id(0)
        j = pl.program_id(1)

        @pl.when((i == 0) & (j == 0))
        def _():
            dgpre_ref[...] = jnp.zeros_like(dgpre_ref)
            dgpost_ref[...] = jnp.zeros_like(dgpost_ref)

        @pl.when((i % tiles_per_seq == 0) & (j == 0))
        def _():
            dmod_ref[...] = jnp.zeros_like(dmod_ref)

        @pl.when(j == 0)
        def _():
            dy0, dmg, dg = _post_bwd(dxo_ref[...], y0_ref[...], gpost_ref[...], mod_ref[0, 2:3, :], gs)
            dmod_ref[0, 2:3, :] += dmg
            dgpost_ref[...] += dg
            db = dy0.astype(BF16)
            dy0s[...] = db
            dy0_ref[...] = db
            acc[...] = jnp.zeros_like(acc)

        dact = _dot_nt(dy0s[...], wd_ref[...])
        g = gate_ref[...].astype(F32)
        u = up_ref[...].astype(F32)
        sig = jax.nn.sigmoid(g)
        sl = g * sig
        dgate = (dact * u * (sig * (1.0 + g * (1.0 - sig)))).astype(BF16)
        dup = (dact * sl).astype(BF16)
        act_ref[...] = (sl * u).astype(BF16)
        dgate_ref[...] = dgate
        dup_ref[...] = dup
        acc[...] += _dot_nt(dgate, wg_ref[...]) + _dot_nt(dup, wu_ref[...])

        @pl.when(j == FF_NJ - 1)
        def _():
            dx, dsh, dsc, dg = _norm_mod_bwd(acc[...], x_ref[...], gpre_ref[...], mod_ref[0, 1:2, :])
            dx_ref[...] = dxo_ref[...] + dx
            dmod_ref[0, 0:1, :] += dsh
            dmod_ref[0, 1:2, :] += dsc
            dgpre_ref[...] += dg

    tok = pl.BlockSpec((tm, D), lambda i, j: (i, 0))
    vec = pl.BlockSpec((1, D), lambda i, j: (0, 0))
    hid = pl.BlockSpec((tm, FF_TN), lambda i, j: (i, j))
    modspec = pl.BlockSpec((1, 3, D), lambda i, j: ((i * tm) // SEQ, 0, 0))
    return pl.pallas_call(
        body, grid=(T // tm, FF_NJ),
        in_specs=[tok, tok, tok, modspec, vec, vec, hid, hid,
                  pl.BlockSpec((D, FF_TN), lambda i, j: (0, j)), pl.BlockSpec((D, FF_TN), lambda i, j: (0, j)),
                  pl.BlockSpec((FF_TN, D), lambda i, j: (j, 0))],
        out_specs=[tok, tok, hid, hid, hid, modspec, vec, vec],
        out_shape=[SDS((T, D), F32), SDS((T, D), BF16), SDS((T, DFF_PAD), BF16), SDS((T, DFF_PAD), BF16),
                   SDS((T, DFF_PAD), BF16), SDS((nb, 3, D), F32), SDS((1, D), F32), SDS((1, D), F32)],
        scratch_shapes=[pltpu.VMEM((tm, D), BF16), pltpu.VMEM((tm, D), F32)],
        compiler_params=_cp("arbitrary", "arbitrary"), name=name,
    )(dxo, x, y0, mod3, g_pre, g_post, gate, up, wg, wu, wd)


def matmul_tn(a, b, tm, tn, tk, name):
    T, M = a.shape
    N = b.shape[1]
    nk = T // tk

    def body(a_ref, b_ref, o_ref):
        @pl.when(pl.program_id(2) == 0)
        def _():
            o_ref[...] = jnp.zeros_like(o_ref)

        o_ref[...] += _dot_tn(a_ref[...], b_ref[...])

    return pl.pallas_call(
        body, grid=(M // tm, N // tn, nk),
        in_specs=[pl.BlockSpec((tk, tm), lambda i, j, k: (k, i)), pl.BlockSpec((tk, tn), lambda i, j, k: (k, j))],
        out_specs=pl.BlockSpec((tm, tn), lambda i, j, k: (i, j)),
        out_shape=SDS((M, N), F32),
        compiler_params=_cp("arbitrary", "arbitrary", "arbitrary"), name=name,
    )(a, b)


def loss_grad(y, tgt, name):
    T = y.shape[0]

    def body(y_ref, t_ref, dy_ref, l_ref):
        @pl.when(pl.program_id(0) == 0)
        def _():
            l_ref[...] = jnp.zeros_like(l_ref)

        e = y_ref[...] - t_ref[...]
        dy_ref[...] = e * (1.0 / D)
        l_ref[...] += jnp.sum(e * e) * (0.5 / D)

    tok = pl.BlockSpec((TM, D), lambda i: (i, 0))
    return pl.pallas_call(
        body, grid=(T // TM,), in_specs=[tok, tok],
        out_specs=[tok, pl.BlockSpec((8, LANE), lambda i: (0, 0))],
        out_shape=[SDS((T, D), F32), SDS((8, LANE), F32)],
        compiler_params=_cp("arbitrary"), name=name,
    )(y, tgt)


def rope_tables(pos_col, name):
    T = pos_col.shape[0]
    tm = 1024

    def body(p_ref, c_ref, s1_ref, s2_ref):
        lane = lax.broadcasted_iota(jnp.int32, (1, LANE), 1)
        l64 = lane % HD
        inv_freq = jnp.exp((l64 % 8).astype(F32) * (-math.log(ROPE_THETA) / 8.0))
        ang = p_ref[...].astype(F32) * inv_freq
        cs = jnp.cos(ang)
        sn = jnp.sin(ang)
        c_ref[...] = jnp.where(l64 < 16, cs, 1.0)
        s1_ref[...] = jnp.where(l64 < 8, -sn, 0.0)
        s2_ref[...] = jnp.where((l64 >= 8) & (l64 < 16), sn, 0.0)

    tab = pl.BlockSpec((tm, LANE), lambda i: (i, 0))
    return pl.pallas_call(
        body, grid=(T // tm,), in_specs=[pl.BlockSpec((tm, 1), lambda i: (i, 0))], out_specs=[tab, tab, tab],
        out_shape=[SDS((T, LANE), F32)] * 3, compiler_params=_cp("arbitrary"), name=name,
    )(pos_col)


def mixer_proj(x, mod3, g_pre, w_main, w_f, rc, rs1, rs2, name):
    T = x.shape[0]
    tn = 1024

    def body(x_ref, mod_ref, g_ref, w_ref, wf_ref, c_ref, s1_ref, s2_ref, h_ref, p_ref, f_ref, hs):
        j = pl.program_id(1)

        @pl.when(j == 0)
        def _():
            xf = x_ref[...]
            h = (xf * _rms(xf) * g_ref[...]) * (1.0 + mod_ref[0, 1:2, :]) + mod_ref[0, 0:1, :]
            hb = h.astype(BF16)
            hs[...] = hb
            h_ref[...] = hb
            f_ref[...] = _dot(hb, wf_ref[...])

        pr = _dot(hs[...], w_ref[...])

        @pl.when(j == 0)
        def _():
            c, s1, s2 = c_ref[...], s1_ref[...], s2_ref[...]
            for k in range(tn // LANE):
                t = pr[:, k * LANE:(k + 1) * LANE]
                rot = t * c + pltpu.roll(t, LANE - 8, 1) * s1 + pltpu.roll(t, 8, 1) * s2
                p_ref[:, k * LANE:(k + 1) * LANE] = rot.astype(BF16)

        @pl.when(j > 0)
        def _():
            p_ref[...] = pr.astype(BF16)

    tok = pl.BlockSpec((TM, D), lambda i, j: (i, 0))
    vec = pl.BlockSpec((1, D), lambda i, j: (0, 0))
    tab = pl.BlockSpec((TM, LANE), lambda i, j: (i, 0))
    return pl.pallas_call(
        body, grid=(T // TM, IN_MAIN // tn),
        in_specs=[tok, pl.BlockSpec((1, 3, D), _mod_map), vec, pl.BlockSpec((D, tn), lambda i, j: (0, j)),
                  pl.BlockSpec((D, LANE), lambda i, j: (0, 0)), tab, tab, tab],
        out_specs=[tok, pl.BlockSpec((TM, tn), lambda i, j: (i, j)), tab],
        out_shape=[SDS((T, D), BF16), SDS((T, IN_MAIN), BF16), SDS((T, LANE), F32)],
        scratch_shapes=[pltpu.VMEM((TM, D), BF16)],
        compiler_params=_cp("arbitrary", "arbitrary"), name=name,
    )(x, mod3, g_pre, w_main, w_f, rc, rs1, rs2)


def _band_mask(i, nbc):
    r = lax.broadcasted_iota(jnp.int32, (QB, 2 * QB), 0)
    c = lax.broadcasted_iota(jnp.int32, (QB, 2 * QB), 1)
    return (c >= r) & (c <= r + QB) & ((c >= QB) | (i % nbc != 0))


def band_fwd(q, kp, vp, nbc, name):
    B, H, S, _ = q.shape

    def body(q_ref, k_ref, v_ref, o_ref, lse_ref):
        i = pl.program_id(2)
        st = pl.multiple_of(i * QB, QB)
        k = k_ref[0, 0, pl.ds(st, 2 * QB), :]
        v = v_ref[0, 0, pl.ds(st, 2 * QB), :]
        s = _dot_nt(q_ref[0, 0], k) * ATTN_SCALE
        s = jnp.where(_band_mask(i, nbc), s, NEG)
        m = jnp.max(s, axis=-1, keepdims=True)
        p = jnp.exp(s - m)
        l = jnp.sum(p, axis=-1, keepdims=True)
        o_ref[0, 0] = _dot(p.astype(BF16), v) / l
        lse_ref[0, 0] = m + jnp.log(l)

    qs = pl.BlockSpec((1, 1, QB, HD), lambda b, h, i: (b, h, i, 0))
    ks = pl.BlockSpec((1, 1, S + QB, HD), lambda b, h, i: (b, h, 0, 0))
    return pl.pallas_call(
        body, grid=(B, H, S // QB), in_specs=[qs, ks, ks],
        out_specs=[qs, pl.BlockSpec((1, 1, QB, 1), lambda b, h, i: (b, h, i, 0))],
        out_shape=[SDS((B, H, S, HD), F32), SDS((B, H, S, 1), F32)],
        compiler_params=_cp("arbitrary", "arbitrary", "arbitrary"), name=name,
    )(q, kp, vp)


def band_bwd(q, kp, vp, do, lse, dvec, nbc, name):
    B, H, S, _ = q.shape

    def body(q_ref, k_ref, v_ref, do_ref, l_ref, d_ref, dq_ref, dk_ref, dv_ref):
        i = pl.program_id(2)

        @pl.when(i == 0)
        def _():
            dk_ref[...] = jnp.zeros_like(dk_ref)
            dv_ref[...] = jnp.zeros_like(dv_ref)

        st = pl.multiple_of(i * QB, QB)
        k = k_ref[0, 0, pl.ds(st, 2 * QB), :]
        v = v_ref[0, 0, pl.ds(st, 2 * QB), :]
        qi = q_ref[0, 0]
        doi = do_ref[0, 0]
        s = _dot_nt(qi, k) * ATTN_SCALE
        s = jnp.where(_band_mask(i, nbc), s, NEG)
        p = jnp.exp(s - l_ref[0, 0])
        dp = _dot_nt(doi, v)
        ds = (p * (dp - d_ref[0, 0]) * ATTN_SCALE).astype(BF16)
        dq_ref[0, 0] = _dot(ds, k)
        dk_ref[0, 0, pl.ds(st, 2 * QB), :] += _dot_tn(ds, qi)
        dv_ref[0, 0, pl.ds(st, 2 * QB), :] += _dot_tn(p.astype(BF16), doi)

    qs = pl.BlockSpec((1, 1, QB, HD), lambda b, h, i: (b, h, i, 0))
    ks = pl.BlockSpec((1, 1, S + QB, HD), lambda b, h, i: (b, h, 0, 0))
    col = pl.BlockSpec((1, 1, QB, 1), lambda b, h, i: (b, h, i, 0))
    return pl.pallas_call(
        body, grid=(B, H, S // QB), in_specs=[qs, ks, ks, qs, col, col],
        out_specs=[qs, ks, ks],
        out_shape=[SDS((B, H, S, HD), F32), SDS((B, H, S + QB, HD), F32), SDS((B, H, S + QB, HD), F32)],
        compiler_params=_cp("arbitrary", "arbitrary", "arbitrary"), name=name,
    )(q, kp, vp, do, lse, dvec)


def band_merge(os_, lses, name):
    B, H, S, _ = os_[0].shape

    def body(o1, o2, o3, l1, l2, l3, out_ref, lse_ref):
        a, b, c = l1[0, 0], l2[0, 0], l3[0, 0]
        m = jnp.maximum(jnp.maximum(a, b), c)
        L = m + jnp.log(jnp.exp(a - m) + jnp.exp(b - m) + jnp.exp(c - m))
        out_ref[0, 0] = jnp.exp(a - L) * o1[0, 0] + jnp.exp(b - L) * o2[0, 0] + jnp.exp(c - L) * o3[0, 0]
        lse_ref[0, 0] = L

    full = pl.BlockSpec((1, 1, S, HD), lambda b, h: (b, h, 0, 0))
    col = pl.BlockSpec((1, 1, S, 1), lambda b, h: (b, h, 0, 0))
    return pl.pallas_call(
        body, grid=(B, H), in_specs=[full] * 3 + [col] * 3, out_specs=[full, col],
        out_shape=[SDS((B, H, S, HD), F32), SDS((B, H, S, 1), F32)],
        compiler_params=_cp("arbitrary", "arbitrary"), name=name,
    )(*os_, *lses)


def rowdot(a, b, name):
    B, H, S, _ = a.shape

    def body(a_ref, b_ref, o_ref):
        o_ref[0, 0] = jnp.sum(a_ref[0, 0] * b_ref[0, 0], axis=-1, keepdims=True)

    full = pl.BlockSpec((1, 1, S, HD), lambda b, h: (b, h, 0, 0))
    return pl.pallas_call(
        body, grid=(B, H), in_specs=[full, full], out_specs=pl.BlockSpec((1, 1, S, 1), lambda b, h: (b, h, 0, 0)),
        out_shape=SDS((B, H, S, 1), F32), compiler_params=_cp("arbitrary", "arbitrary"), name=name,
    )(a, b)


def _causal_mask():
    r = lax.broadcasted_iota(jnp.int32, (FB, FB), 0)
    c = lax.broadcasted_iota(jnp.int32, (FB, FB), 1)
    return r >= c


def fox_fwd(q, k, v, fcol, frow, name):
    B, H, n = q.shape[:3]

    def body(q_ref, k_ref, v_ref, fc_ref, fr_ref, o_ref, lse_ref):
        i = pl.program_id(2)
        qi = q_ref[0, 0, 0]
        fq = fc_ref[0, 0, 0]

        def step(j, carry, masked):
            m, l, acc = carry
            s = _dot_nt(qi, k_ref[0, 0, j]) * ATTN_SCALE + fq - fr_ref[0, 0, j]
            if masked:
                s = jnp.where(_causal_mask(), s, NEG)
            m2 = jnp.maximum(m, jnp.max(s, axis=-1, keepdims=True))
            a = jnp.exp(m - m2)
            p = jnp.exp(s - m2)
            return m2, a * l + jnp.sum(p, axis=-1, keepdims=True), a * acc + _dot(p.astype(BF16), v_ref[0, 0, j])

        init = (jnp.full((FB, 1), NEG, F32), jnp.zeros((FB, 1), F32), jnp.zeros((FB, HD), F32))
        carry = lax.fori_loop(0, i, lambda j, cr: step(j, cr, False), init)
        m, l, acc = step(i, carry, True)
        o_ref[0, 0, 0] = acc / l
        lse_ref[0, 0, 0] = m + jnp.log(l)

    blk = pl.BlockSpec((1, 1, 1, FB, HD), lambda b, h, i: (b, h, i, 0, 0))
    full = pl.BlockSpec((1, 1, n, FB, HD), lambda b, h, i: (b, h, 0, 0, 0))
    colb = pl.BlockSpec((1, 1, 1, FB, 1), lambda b, h, i: (b, h, i, 0, 0))
    rowf = pl.BlockSpec((1, 1, n, 1, FB), lambda b, h, i: (b, h, 0, 0, 0))
    return pl.pallas_call(
        body, grid=(B, H, n), in_specs=[blk, full, full, colb, rowf], out_specs=[blk, colb],
        out_shape=[SDS((B, H, n, FB, HD), F32), SDS((B, H, n, FB, 1), F32)],
        compiler_params=_cp("arbitrary", "arbitrary", "arbitrary"), name=name,
    )(q, k, v, fcol, frow)


def fox_bwd(q, k, v, do, lse, dvec, fcol, frow, name):
    B, H, n = q.shape[:3]

    def body(q_ref, k_ref, v_ref, do_ref, l_ref, d_ref, fc_ref, fr_ref, dq_ref, dk_ref, dv_ref, dfq_ref, dfk_ref):
        j = pl.program_id(2)

        @pl.when(j == 0)
        def _():
            dq_ref[...] = jnp.zeros_like(dq_ref)
            dfq_ref[...] = jnp.zeros_like(dfq_ref)

        kj = k_ref[0, 0, 0]
        vj = v_ref[0, 0, 0]
        fk = fr_ref[0, 0, 0]

        def step(i, carry, masked):
            dk, dv, dfk = carry
            qi = q_ref[0, 0, i]
            doi = do_ref[0, 0, i]
            s = _dot_nt(qi, kj) * ATTN_SCALE + fc_ref[0, 0, i] - fk
            if masked:
                s = jnp.where(_causal_mask(), s, NEG)
            p = jnp.exp(s - l_ref[0, 0, i])
            ds = p * (_dot_nt(doi, vj) - d_ref[0, 0, i])
            dsb = (ds * ATTN_SCALE).astype(BF16)
            dq_ref[0, 0, i] += _dot(dsb, kj)
            dfq_ref[0, 0, i] += jnp.sum(ds, axis=-1, keepdims=True)
            return (dk + _dot_tn(dsb, qi), dv + _dot_tn(p.astype(BF16), doi), dfk - jnp.sum(ds, axis=0, keepdims=True))

        init = (jnp.zeros((FB, HD), F32), jnp.zeros((FB, HD), F32), jnp.zeros((1, FB), F32))
        carry = step(j, init, True)
        dk, dv, dfk = lax.fori_loop(j + 1, n, lambda i, cr: step(i, cr, False), carry)
        dk_ref[0, 0, 0] = dk
        dv_ref[0, 0, 0] = dv
        dfk_ref[0, 0, 0] = dfk

    blk = pl.BlockSpec((1, 1, 1, FB, HD), lambda b, h, j: (b, h, j, 0, 0))
    full = pl.BlockSpec((1, 1, n, FB, HD), lambda b, h, j: (b, h, 0, 0, 0))
    colf = pl.BlockSpec((1, 1, n, FB, 1), lambda b, h, j: (b, h, 0, 0, 0))
    rowb = pl.BlockSpec((1, 1, 1, 1, FB), lambda b, h, j: (b, h, j, 0, 0))
    return pl.pallas_call(
        body, grid=(B, H, n), in_specs=[full, blk, blk, full, colf, colf, colf, rowb],
        out_specs=[full, blk, blk, colf, rowb],
        out_shape=[SDS((B, H, n, FB, HD), F32), SDS((B, H, n, FB, HD), F32), SDS((B, H, n, FB, HD), F32),
                   SDS((B, H, n, FB, 1), F32), SDS((B, H, n, 1, FB), F32)],
        compiler_params=_cp("arbitrary", "arbitrary", "arbitrary"), name=name,
    )(q, k, v, do, lse, dvec, fcol, frow)


def _tri(lower):
    r = lax.broadcasted_iota(jnp.int32, (LANE, LANE), 0)
    c = lax.broadcasted_iota(jnp.int32, (LANE, LANE), 1)
    return ((r >= c) if lower else (r <= c)).astype(F32)


def _tri_dot(t, xblk):
    return jnp.dot(t, xblk, precision=lax.Precision.HIGHEST, preferred_element_type=F32)


def forget_cumsum(flog, bias, name):
    B, S, _ = flog.shape

    def body(f_ref, b_ref, o_ref):
        t = _tri(True)
        carry = jnp.zeros((1, LANE), F32)
        for blk in range(S // LANE):
            z = f_ref[0, blk * LANE:(blk + 1) * LANE, :] + b_ref[...]
            lf = jnp.minimum(z, 0.0) - jnp.log(1.0 + jnp.exp(-jnp.abs(z)))
            cs = _tri_dot(t, lf) + carry
            o_ref[0, blk * LANE:(blk + 1) * LANE, :] = cs
            carry = cs[LANE - 1:LANE, :]

    spec = pl.BlockSpec((1, S, LANE), lambda b: (b, 0, 0))
    return pl.pallas_call(
        body, grid=(B,), in_specs=[spec, pl.BlockSpec((1, LANE), lambda b: (0, 0))], out_specs=spec,
        out_shape=SDS((B, S, LANE), F32), compiler_params=_cp("arbitrary"), name=name,
    )(flog, bias)


def forget_cumsum_bwd(dF, flog, bias, name):
    B, S, _ = flog.shape

    def body(d_ref, f_ref, b_ref, o_ref, db_ref):
        @pl.when(pl.program_id(0) == 0)
        def _():
            db_ref[...] = jnp.zeros_like(db_ref)

        t = _tri(False)
        carry = jnp.zeros((1, LANE), F32)
        tot = jnp.zeros((1, LANE), F32)
        for blk in reversed(range(S // LANE)):
            sl = slice(blk * LANE, (blk + 1) * LANE)
            rc = _tri_dot(t, d_ref[0, sl, :]) + carry
            carry = rc[0:1, :]
            z = f_ref[0, sl, :] + b_ref[...]
            dz = rc * jax.nn.sigmoid(-z)
            o_ref[0, sl, :] = dz
            tot = tot + jnp.sum(dz, axis=0, keepdims=True)
        db_ref[...] += tot

    spec = pl.BlockSpec((1, S, LANE), lambda b: (b, 0, 0))
    vec = pl.BlockSpec((1, LANE), lambda b: (0, 0))
    return pl.pallas_call(
        body, grid=(B,), in_specs=[spec, spec, vec], out_specs=[spec, vec],
        out_shape=[SDS((B, S, LANE), F32), SDS((1, LANE), F32)], compiler_params=_cp("arbitrary"), name=name,
    )(dF, flog, bias)


def mixer_out_fwd(oa, ob, goa, gob, w_out, g_post, x, mod3, name):
    T = x.shape[0]

    def body(oa_ref, ob_ref, goa_ref, gob_ref, w_ref, gp_ref, x_ref, mod_ref, xo_ref, mg_ref, y0_ref):
        a = oa_ref[...]
        b = ob_ref[...]
        mg = jnp.concatenate([a * _rms(a) * goa_ref[...], b * _rms(b) * gob_ref[...]], axis=-1).astype(BF16)
        mg_ref[...] = mg
        y0 = _dot(mg, w_ref[...])
        y0_ref[...] = y0
        xo_ref[...] = x_ref[...] + mod_ref[0, 2:3, :] * (y0 * _rms(y0) * gp_ref[...])

    tok = pl.BlockSpec((TM, D), lambda i: (i, 0))
    half = pl.BlockSpec((TM, WG), lambda i: (i, 0))
    hv = pl.BlockSpec((1, WG), lambda i: (0, 0))
    return pl.pallas_call(
        body, grid=(T // TM,),
        in_specs=[half, half, hv, hv, pl.BlockSpec((D, D), lambda i: (0, 0)), pl.BlockSpec((1, D), lambda i: (0, 0)), tok,
                  pl.BlockSpec((1, 3, D), _mod_map)],
        out_specs=[tok, tok, tok], out_shape=[SDS((T, D), F32), SDS((T, D), BF16), SDS((T, D), F32)],
        compiler_params=_cp("arbitrary"), name=name,
    )(oa, ob, goa, gob, w_out, g_post, x, mod3)


def mixer_out_bwd(dxo, y0, mod3, g_post, w_out, oa, ob, goa, gob, name):
    T = dxo.shape[0]
    nb = T // SEQ
    tiles_per_seq = SEQ // TM

    def body(dxo_ref, y0_ref, mod_ref, gp_ref, w_ref, oa_ref, ob_ref, goa_ref, gob_ref,
             dy0_ref, doa_ref, dob_ref, dmg_ref, dgp_ref, dgoa_ref, dgob_ref):
        i = pl.program_id(0)

        @pl.when(i == 0)
        def _():
            dgp_ref[...] = jnp.zeros_like(dgp_ref)
            dgoa_ref[...] = jnp.zeros_like(dgoa_ref)
            dgob_ref[...] = jnp.zeros_like(dgob_ref)

        @pl.when(i % tiles_per_seq == 0)
        def _():
            dmg_ref[...] = jnp.zeros_like(dmg_ref)

        dy0, dmg, dg = _post_bwd(dxo_ref[...], y0_ref[...], gp_ref[...], mod_ref[0, 2:3, :], 1.0)
        dmg_ref[0] += dmg
        dgp_ref[...] += dg
        db = dy0.astype(BF16)
        dy0_ref[...] = db
        dm = _dot_nt(db, w_ref[...])
        for o_ref, g_ref, do_ref, dg_ref, sl in ((oa_ref, goa_ref, doa_ref, dgoa_ref, slice(0, WG)),
                                                  (ob_ref, gob_ref, dob_ref, dgob_ref, slice(WG, 2 * WG))):
            o = o_ref[...]
            r = _rms(o)
            oh = o * r
            d = dm[:, sl]
            dg_ref[...] += jnp.sum(d * oh, axis=0, keepdims=True)
            dh = d * g_ref[...]
            do_ref[...] = r * (dh - oh * jnp.mean(dh * oh, axis=-1, keepdims=True))

    tok = pl.BlockSpec((TM, D), lambda i: (i, 0))
    half = pl.BlockSpec((TM, WG), lambda i: (i, 0))
    hv = pl.BlockSpec((1, WG), lambda i: (0, 0))
    vec = pl.BlockSpec((1, D), lambda i: (0, 0))
    return pl.pallas_call(
        body, grid=(T // TM,),
        in_specs=[tok, tok, pl.BlockSpec((1, 3, D), _mod_map), vec, pl.BlockSpec((D, D), lambda i: (0, 0)), half, half, hv, hv],
        out_specs=[tok, half, half, pl.BlockSpec((1, 1, D), _mod_map), vec, hv, hv],
        out_shape=[SDS((T, D), BF16), SDS((T, WG), F32), SDS((T, WG), F32), SDS((nb, 1, D), F32), SDS((1, D), F32),
                   SDS((1, WG), F32), SDS((1, WG), F32)],
        compiler_params=_cp("arbitrary"), name=name,
    )(dxo, y0, mod3, g_post, w_out, oa, ob, goa, gob)


def proj_grad_assemble(dqs, dks, dvs, dqb, dkb, dvb, rc, rs1, rs2, name):
    T = dqb.shape[0]
    tm = 256

    def body(*refs):
        ins, (c_ref, s1_ref, s2_ref, o_ref) = refs[:12], refs[12:]
        c, s1, s2 = c_ref[...], s1_ref[...], s2_ref[...]
        for grp in range(2):
            t = ins[3 * grp][...] + ins[3 * grp + 1][...] + ins[3 * grp + 2][...]
            for k in range(WG // LANE):
                d = t[:, k * LANE:(k + 1) * LANE]
                un = d * c + pltpu.roll(d * s1, 8, 1) + pltpu.roll(d * s2, LANE - 8, 1)
                o_ref[:, grp * WG + k * LANE:grp * WG + (k + 1) * LANE] = un.astype(BF16)
        o_ref[:, 2 * WG:3 * WG] = (ins[6][...] + ins[7][...] + ins[8][...]).astype(BF16)
        for g in range(3):
            o_ref[:, (3 + g) * WG:(4 + g) * WG] = ins[9 + g][...].astype(BF16)

    half = pl.BlockSpec((tm, WG), lambda i: (i, 0))
    tab = pl.BlockSpec((tm, LANE), lambda i: (i, 0))
    return pl.pallas_call(
        body, grid=(T // tm,), in_specs=[half] * 12 + [tab] * 3, out_specs=pl.BlockSpec((tm, IN_MAIN), lambda i: (i, 0)),
        out_shape=SDS((T, IN_MAIN), BF16), compiler_params=_cp("arbitrary"), name=name,
    )(*dqs, *dks, *dvs, dqb, dkb, dvb, rc, rs1, rs2)


def mixer_proj_bwd(dproj, dflog, dxo, x, mod3, g_pre, w_main, w_f, name):
    T = x.shape[0]
    nb = T // SEQ
    tiles_per_seq = SEQ // TM
    tk = 1024
    nj = IN_MAIN // tk

    def body(dp_ref, df_ref, dxo_ref, x_ref, mod_ref, g_ref, w_ref, wf_ref, dx_ref, dmod_ref, dg_ref, acc):
        i = pl.program_id(0)
        j = pl.program_id(1)

        @pl.when((i == 0) & (j == 0))
        def _():
            dg_ref[...] = jnp.zeros_like(dg_ref)

        @pl.when((i % tiles_per_seq == 0) & (j == 0))
        def _():
            dmod_ref[...] = jnp.zeros_like(dmod_ref)

        @pl.when(j == 0)
        def _():
            acc[...] = _dot_nt(df_ref[...].astype(BF16), wf_ref[...])

        acc[...] += _dot_nt(dp_ref[...], w_ref[...])

        @pl.when(j == nj - 1)
        def _():
            dx, dsh, dsc, dg = _norm_mod_bwd(acc[...], x_ref[...], g_ref[...], mod_ref[0, 1:2, :])
            dx_ref[...] = dxo_ref[...] + dx
            dmod_ref[0, 0:1, :] += dsh
            dmod_ref[0, 1:2, :] += dsc
            dg_ref[...] += dg

    tok = pl.BlockSpec((TM, D), lambda i, j: (i, 0))
    vec = pl.BlockSpec((1, D), lambda i, j: (0, 0))
    return pl.pallas_call(
        body, grid=(T // TM, nj),
        in_specs=[pl.BlockSpec((TM, tk), lambda i, j: (i, j)), pl.BlockSpec((TM, LANE), lambda i, j: (i, 0)), tok, tok,
                  pl.BlockSpec((1, 3, D), _mod_map), vec, pl.BlockSpec((D, tk), lambda i, j: (0, j)),
                  pl.BlockSpec((D, LANE), lambda i, j: (0, 0))],
        out_specs=[tok, pl.BlockSpec((1, 2, D), _mod_map), vec],
        out_shape=[SDS((T, D), F32), SDS((nb, 2, D), F32), SDS((1, D), F32)],
        scratch_shapes=[pltpu.VMEM((TM, D), F32)],
        compiler_params=_cp("arbitrary", "arbitrary"), name=name,
    )(dproj, dflog, dxo, x, mod3, g_pre, w_main, w_f)


def ada_fwd(c_all, w, b, name):
    n = w.shape[1]
    tn = n // 2

    def body(c_ref, w_ref, b_ref, o_ref):
        cv = c_ref[...]
        o_ref[...] = _dot((cv * jax.nn.sigmoid(cv)).astype(BF16), w_ref[...].astype(BF16)) + b_ref[...]

    R = c_all.shape[0]
    return pl.pallas_call(
        body, grid=(2,),
        in_specs=[pl.BlockSpec((R, D), lambda j: (0, 0)), pl.BlockSpec((D, tn), lambda j: (0, j)), pl.BlockSpec((1, tn), lambda j: (0, j))],
        out_specs=pl.BlockSpec((R, tn), lambda j: (0, j)), out_shape=SDS((R, n), F32),
        compiler_params=_cp("arbitrary"), name=name,
    )(c_all, w, b)


def ada_bwd(c_all, dmod, name):
    R, n = dmod.shape
    tn = n // 2

    def body(c_ref, d_ref, o_ref):
        cv = c_ref[...]
        o_ref[...] = _dot_tn((cv * jax.nn.sigmoid(cv)).astype(BF16), d_ref[...].astype(BF16))

    return pl.pallas_call(
        body, grid=(2,), in_specs=[pl.BlockSpec((R, D), lambda j: (0, 0)), pl.BlockSpec((R, tn), lambda j: (0, j))],
        out_specs=pl.BlockSpec((D, tn), lambda j: (0, j)), out_shape=SDS((D, n), F32),
        compiler_params=_cp("arbitrary"), name=name,
    )(c_all, dmod)


def _adam_math(w, g, m, v):
    m2 = ADAM_B1 * m + (1.0 - ADAM_B1) * g
    v2 = ADAM_B2 * v + (1.0 - ADAM_B2) * (g * g)
    m_hat = m2 / (1.0 - ADAM_B1 ** ADAM_STEP)
    v_hat = v2 / (1.0 - ADAM_B2 ** ADAM_STEP)
    delta = -ADAM_LR * (m_hat / (jnp.sqrt(v_hat) + ADAM_EPS) + ADAM_WD * w)
    return delta, m2, v2


def adam_update(w, g, m, v, tr, name):
    R, C = w.shape

    def body(w_ref, g_ref, m_ref, v_ref, d_ref, mo_ref, vo_ref):
        d_ref[...], mo_ref[...], vo_ref[...] = _adam_math(w_ref[...], g_ref[...], m_ref[...], v_ref[...])

    spec = pl.BlockSpec((tr, C), lambda i: (i, 0))
    return pl.pallas_call(
        body, grid=(R // tr,), in_specs=[spec] * 4, out_specs=[spec] * 3, out_shape=[SDS((R, C), F32)] * 3,
        compiler_params=_cp("arbitrary"), name=name,
    )(w, g, m, v)


def vec_adam(parts, w, m, v, name):
    P, C = parts.shape

    def body(p_ref, w_ref, m_ref, v_ref, g_ref, d_ref, mo_ref, vo_ref):
        g = jnp.sum(p_ref[...], axis=0, keepdims=True)
        g_ref[...] = g
        d_ref[...], mo_ref[...], vo_ref[...] = _adam_math(w_ref[...], g, m_ref[...], v_ref[...])

    return pl.pallas_call(body, out_shape=[SDS((1, C), F32)] * 4, compiler_params=_cp(), name=name)(parts, w, m, v)


HBM = pl.BlockSpec(memory_space=pltpu.HBM)
VMEM = pl.BlockSpec(memory_space=pltpu.VMEM)


def _place():
    x, y, c = lax.axis_index("x"), lax.axis_index("y"), lax.axis_index("c")
    return x, y, c, [(1 - x, y), (x, 1 - y), (1 - x, 1 - y)]


def all_gather8(xs, name):
    R, C = xs.shape

    def body(x_ref, out_ref, send_sems, recv_sems, local_sem):
        x, y, c, chips = _place()
        me, sibling = (x, y, c), (x, y, 1 - c)

        def slot(px, py, pc):
            return out_ref.at[4 * px + 2 * py + pc]

        def copy(k, block, to, src=None):
            return pltpu.make_async_remote_copy(
                src_ref=slot(*block) if src is None else src, dst_ref=slot(*block),
                send_sem=send_sems.at[k], recv_sem=recv_sems.at[k], device_id=to, device_id_type=MESH)

        mine = pltpu.make_async_copy(x_ref, slot(*me), local_sem)
        mine.start()
        first = [copy(0, me, sibling, src=x_ref)]
        first += [copy(1 + j, me, (*chip, c), src=x_ref) for j, chip in enumerate(chips)]
        for cp in first:
            cp.start()
        passed = [copy(4 + j, (*chip, c), sibling) for j, chip in enumerate(chips)]
        for j, chip in enumerate(chips):
            copy(1 + j, (*chip, c), me).wait_recv()
            passed[j].start()
        copy(0, sibling, me).wait_recv()
        for j, chip in enumerate(chips):
            copy(4 + j, (*chip, 1 - c), me).wait_recv()
        for cp in first + passed:
            cp.wait_send()
        mine.wait()

    return pl.pallas_call(
        body, out_shape=SDS((N_DEV, R, C), xs.dtype), in_specs=[VMEM], out_specs=VMEM,
        scratch_shapes=[pltpu.SemaphoreType.DMA((7,)), pltpu.SemaphoreType.DMA((7,)), pltpu.SemaphoreType.DMA],
        compiler_params=pltpu.CompilerParams(vmem_limit_bytes=VMEM_LIMIT), name=name,
    )(xs)


def all_gather_shards(ws, splits, name):
    n = len(ws)

    def body(*refs):
        w_refs, o_refs = refs[:n], refs[n:2 * n]
        send_sems, recv_sems, local_sems = refs[2 * n:]
        x, y, c, chips = _place()
        sibling = (x, y, 1 - c)
        me_s = 2 * x + y

        def half(ref, k, cc):
            lo, hi = (0, splits[k]) if cc == 0 else (splits[k], ws[k].shape[0])
            return ref.at[pl.ds(lo, hi - lo)]

        def rcopy(src, dst, k, s, to):
            return pltpu.make_async_remote_copy(src_ref=src, dst_ref=dst, send_sem=send_sems.at[k, s],
                                                recv_sem=recv_sems.at[k, s], device_id=to, device_id_type=MESH)

        for cc in (0, 1):
            @pl.when(c == cc)
            def _():
                started, local = [], []
                for k in range(n):
                    cp = pltpu.make_async_copy(w_refs[k], o_refs[k].at[me_s], local_sems.at[k])
                    cp.start()
                    local.append(cp)
                    for j, chip in enumerate(chips):
                        s = rcopy(half(w_refs[k], k, cc), half(o_refs[k].at[me_s], k, cc), k, j, (*chip, c))
                        s.start()
                        started.append(s)
                for k in range(n):
                    for j, chip in enumerate(chips):
                        land = half(o_refs[k].at[2 * chip[0] + chip[1]], k, cc)
                        rcopy(land, land, k, j, (*chip, c)).wait_recv()
                        f = rcopy(land, land, k, 3 + j, sibling)
                        f.start()
                        started.append(f)
                for k in range(n):
                    for j, chip in enumerate(chips):
                        other = half(o_refs[k].at[2 * chip[0] + chip[1]], k, 1 - cc)
                        rcopy(other, other, k, 3 + j, sibling).wait_recv()
                for s in started:
                    s.wait_send()
                for cp in local:
                    cp.wait()

    return pl.pallas_call(
        body, out_shape=[SDS((N_SHARD,) + w.shape, w.dtype) for w in ws], in_specs=[HBM] * n, out_specs=[HBM] * n,
        scratch_shapes=[pltpu.SemaphoreType.DMA((n, 6)), pltpu.SemaphoreType.DMA((n, 6)), pltpu.SemaphoreType.DMA((n,))],
        name=name,
    )(*ws)


def sibling_send_half(gs, name):
    n = len(gs)

    def body(*refs):
        g_refs, o_refs = refs[:n], refs[n:2 * n]
        send_sems, recv_sems = refs[2 * n:]
        x, y, c, _ = _place()
        cps = []
        for k in range(n):
            hr = gs[k].shape[1] // 2
            src = g_refs[k].at[:, pl.ds(pl.multiple_of((1 - c) * hr, 8), hr)]
            cp = pltpu.make_async_remote_copy(src_ref=src, dst_ref=o_refs[k], send_sem=send_sems.at[k], recv_sem=recv_sems.at[k],
                                              device_id=(x, y, 1 - c), device_id_type=MESH)
            cp.start()
            cps.append(cp)
        for cp in cps:
            cp.wait()

    return pl.pallas_call(
        body, out_shape=[SDS((N_SHARD, g.shape[1] // 2, g.shape[2]), g.dtype) for g in gs], in_specs=[HBM] * n, out_specs=[HBM] * n,
        scratch_shapes=[pltpu.SemaphoreType.DMA((n,)), pltpu.SemaphoreType.DMA((n,))], name=name,
    )(*gs)


def chip_scatter(hs, name):
    n = len(hs)

    def body(*refs):
        h_refs, o_refs = refs[:n], refs[n:2 * n]
        send_sems, recv_sems = refs[2 * n:]
        x, y, c, chips = _place()
        cps = []
        for k in range(n):
            for j, chip in enumerate(chips):
                cp = pltpu.make_async_remote_copy(
                    src_ref=h_refs[k].at[2 * chip[0] + chip[1]], dst_ref=o_refs[k].at[j], send_sem=send_sems.at[k, j],
                    recv_sem=recv_sems.at[k, j], device_id=(*chip, c), device_id_type=MESH)
                cp.start()
                cps.append(cp)
        for cp in cps:
            cp.wait()

    return pl.pallas_call(
        body, out_shape=[SDS((3,) + h.shape[1:], h.dtype) for h in hs], in_specs=[HBM] * n, out_specs=[HBM] * n,
        scratch_shapes=[pltpu.SemaphoreType.DMA((n, 3)), pltpu.SemaphoreType.DMA((n, 3))], name=name,
    )(*hs)


def sibling_share_half(ghs, name):
    n = len(ghs)

    def body(*refs):
        g_refs, o_refs = refs[:n], refs[n:2 * n]
        send_sems, recv_sems, local_sems = refs[2 * n:]
        x, y, c, _ = _place()
        cps = []
        for k in range(n):
            hr = ghs[k].shape[0]
            dst = o_refs[k].at[pl.ds(pl.multiple_of(c * hr, 8), hr)]
            lc = pltpu.make_async_copy(g_refs[k], dst, local_sems.at[k])
            lc.start()
            cp = pltpu.make_async_remote_copy(src_ref=g_refs[k], dst_ref=dst, send_sem=send_sems.at[k], recv_sem=recv_sems.at[k],
                                              device_id=(x, y, 1 - c), device_id_type=MESH)
            cp.start()
            cps += [lc, cp]
        for cp in cps:
            cp.wait()

    return pl.pallas_call(
        body, out_shape=[SDS((2 * g.shape[0], g.shape[1]), g.dtype) for g in ghs], in_specs=[HBM] * n, out_specs=[HBM] * n,
        scratch_shapes=[pltpu.SemaphoreType.DMA((n,)), pltpu.SemaphoreType.DMA((n,)), pltpu.SemaphoreType.DMA((n,))],
        name=name,
    )(*ghs)


def pair_sum(g, ra, cidx, name):
    _, r, cols = g.shape
    hr = r // 2

    def body(c_ref, g_ref, a_ref, o_ref):
        o_ref[...] = (g_ref[...] + a_ref[...]).astype(BF16)

    return pl.pallas_call(
        body,
        grid_spec=pltpu.PrefetchScalarGridSpec(
            num_scalar_prefetch=1, grid=(N_SHARD,),
            in_specs=[pl.BlockSpec((1, hr, cols), lambda s, c_ref: (s, c_ref[0], 0)),
                      pl.BlockSpec((1, hr, cols), lambda s, c_ref: (s, 0, 0))],
            out_specs=pl.BlockSpec((1, hr, cols), lambda s, c_ref: (s, 0, 0))),
        out_shape=SDS((N_SHARD, hr, cols), BF16), compiler_params=_cp("arbitrary"), name=name,
    )(cidx, g, ra)


def chip_sum(h, rb, sidx, name):
    _, hr, cols = h.shape

    def body(s_ref, h_ref, r_ref, o_ref):
        o_ref[...] = ((h_ref[0].astype(F32) + r_ref[0].astype(F32)) + r_ref[1].astype(F32)) + r_ref[2].astype(F32)

    return pl.pallas_call(
        body,
        grid_spec=pltpu.PrefetchScalarGridSpec(
            num_scalar_prefetch=1, grid=(1,),
            in_specs=[pl.BlockSpec((1, hr, cols), lambda i, s_ref: (s_ref[0], 0, 0)),
                      pl.BlockSpec((3, hr, cols), lambda i, s_ref: (0, 0, 0))],
            out_specs=pl.BlockSpec((hr, cols), lambda i, s_ref: (0, 0))),
        out_shape=SDS((hr, cols), F32), compiler_params=_cp("arbitrary"), name=name,
    )(sidx, h, rb)


def _to_classes(t, d):
    B, S, H, w = t.shape
    return t.reshape(B, S // d, d, H, w).transpose(0, 3, 2, 1, 4).reshape(B, H, S, w)


def _classes_to_heads(t, d):
    B, H, S, w = t.shape
    return t.reshape(B, H, d, S // d, w).transpose(0, 1, 3, 2, 4).reshape(B, H, S, w)


def _heads_to_classes(t, d):
    B, H, S, w = t.shape
    return t.reshape(B, H, S // d, d, w).transpose(0, 1, 3, 2, 4).reshape(B, H, S, w)


def _classes_to_tokens(t, d):
    B, H, S, w = t.shape
    return t.reshape(B, H, d, S // d, w).transpose(0, 3, 2, 1, 4).reshape(B * S, H * w)


def _pad_front(t):
    return jnp.pad(t, ((0, 0), (0, 0), (QB, 0), (0, 0)))


def _blocked(t):
    B, H, S, w = t.shape
    return t.reshape(B, H, S // FB, FB, w)


def _shard_cols(g, n_valid):
    r = g.shape[0]
    return g[:, :n_valid].reshape(r, N_SHARD, n_valid // N_SHARD).transpose(1, 0, 2)


def _unshard_cols(o, pad_to):
    _, r, n = o.shape
    full = o.transpose(1, 0, 2).reshape(r, N_SHARD * n)
    return jnp.pad(full, ((0, 0), (0, pad_to - N_SHARD * n)))


def mixer_fwd(x1, mod3, g_pre, w_main, w_f, b_forget_pad, goa, gob, w_out, g_post, tabs, nb):
    hmix, proj, flog = mixer_proj(x1, mod3, g_pre, w_main, w_f, *tabs, name="mixer_proj")
    p6 = proj.reshape(nb, SEQ, 6, NH, HD)
    res = dict(hmix=hmix, flog=flog)
    os_, lses, qkv = [], [], []
    for n, (d, nbc) in enumerate(PATTERNS):
        q = _to_classes(p6[:, :, 0], d)
        kp = _pad_front(_to_classes(p6[:, :, 1], d))
        vp = _pad_front(_to_classes(p6[:, :, 2], d))
        o, lse = band_fwd(q, kp, vp, nbc, name=f"band_fwd{n}")
        qkv.append((q, kp, vp))
        os_.append(_classes_to_heads(o, d))
        lses.append(_classes_to_heads(lse, d))
    out_a, lse_a = band_merge(os_, lses, name="band_merge")
    qb, kb, vb = (_blocked(p6[:, :, 3 + g].transpose(0, 2, 1, 3)) for g in range(3))
    F = forget_cumsum(flog.reshape(nb, SEQ, LANE), b_forget_pad, name="forget_cumsum")
    Fh = F[:, :, :NH].transpose(0, 2, 1)
    fcol = Fh.reshape(nb, NH, SEQ // FB, FB, 1)
    frow = Fh.reshape(nb, NH, SEQ // FB, 1, FB)
    out_b, lse_b = fox_fwd(qb, kb, vb, fcol, frow, name="fox_fwd")
    oa_tok = out_a.transpose(0, 2, 1, 3).reshape(nb * SEQ, WG)
    ob_tok = out_b.reshape(nb, NH, SEQ, HD).transpose(0, 2, 1, 3).reshape(nb * SEQ, WG)
    x2, merged, y0m = mixer_out_fwd(oa_tok, ob_tok, goa, gob, w_out, g_post, x1, mod3, name="mixer_out_fwd")
    res.update(qkv=qkv, out_a=out_a, lse_a=lse_a, qb=qb, kb=kb, vb=vb, fcol=fcol, frow=frow, out_b=out_b, lse_b=lse_b,
               oa_tok=oa_tok, ob_tok=ob_tok, merged=merged, y0m=y0m)
    return x2, res


def mixer_bwd(dx2, x1, mod3, g_pre, w_main, w_f, b_forget_pad, goa, gob, w_out, g_post, tabs, res, nb):
    T = nb * SEQ
    dy0m, doa, dob, dmgate, dg_post, dgoa, dgob = mixer_out_bwd(
        dx2, res["y0m"], mod3, g_post, w_out, res["oa_tok"], res["ob_tok"], goa, gob, name="mixer_out_bwd")
    doa_h = doa.reshape(nb, SEQ, NH, HD).transpose(0, 2, 1, 3)
    dvec_a = rowdot(doa_h, res["out_a"], name="rowdot_a")
    dqs, dks, dvs = [], [], []
    for n, (d, nbc) in enumerate(PATTERNS):
        q, kp, vp = res["qkv"][n]
        do_p = _heads_to_classes(doa_h, d).astype(BF16)
        dq, dkp, dvp = band_bwd(q, kp, vp, do_p, _heads_to_classes(res["lse_a"], d), _heads_to_classes(dvec_a, d), nbc,
                                name=f"band_bwd{n}")
        dqs.append(_classes_to_tokens(dq, d))
        dks.append(_classes_to_tokens(dkp[:, :, QB:], d))
        dvs.append(_classes_to_tokens(dvp[:, :, QB:], d))
    dob_h = _blocked(dob.reshape(nb, SEQ, NH, HD).transpose(0, 2, 1, 3))
    dvec_b = rowdot(dob_h.reshape(nb, NH, SEQ, HD), res["out_b"].reshape(nb, NH, SEQ, HD), name="rowdot_b")
    dqb, dkb, dvb, dfq, dfk = fox_bwd(res["qb"], res["kb"], res["vb"], dob_h.astype(BF16), res["lse_b"],
                                      dvec_b.reshape(nb, NH, SEQ // FB, FB, 1), res["fcol"], res["frow"], name="fox_bwd")
    to_tok = lambda t: t.reshape(nb, NH, SEQ, HD).transpose(0, 2, 1, 3).reshape(T, WG)
    dF = (dfq.reshape(nb, NH, SEQ) + dfk.reshape(nb, NH, SEQ)).transpose(0, 2, 1)
    dF = jnp.pad(dF, ((0, 0), (0, 0), (0, LANE - NH)))
    dflog, dbf = forget_cumsum_bwd(dF, res["flog"].reshape(nb, SEQ, LANE), b_forget_pad, name="forget_cumsum_bwd")
    dflog = dflog.reshape(T, LANE)
    dproj = proj_grad_assemble(dqs, dks, dvs, to_tok(dqb), to_tok(dkb), to_tok(dvb), *tabs, name="proj_grad_assemble")
    dx1, dmod2, dg_pre = mixer_proj_bwd(dproj, dflog, dx2, x1, mod3, g_pre, w_main, w_f, name="mixer_proj_bwd")
    g_main = matmul_tn(res["hmix"], dproj, D, 1024, 1024, name="grad_w_in")
    g_f = matmul_tn(res["hmix"], dflog.astype(BF16), D, LANE, 1024, name="grad_w_forget")
    g_out = matmul_tn(res["merged"], dy0m, D, D, 1024, name="grad_w_out")
    dmod3 = jnp.concatenate([dmod2, dmgate], axis=1)
    return dx1, dmod3, dict(g_pre=dg_pre, g_post=dg_post, goa=dgoa, gob=dgob, b_forget=dbf[:, :NH],
                            w_in=jnp.concatenate([g_main, g_f[:, :NH]], axis=1), w_out=g_out)


def ffn_grads(h, dy0, act, dgate, dup, pre):
    g_gate = matmul_tn(h, dgate, D, FF_TN, 1024, name=pre + "_grad_gate")
    g_up = matmul_tn(h, dup, D, FF_TN, 1024, name=pre + "_grad_up")
    g_down = matmul_tn(act, dy0, FF_TN, D, 1024, name=pre + "_grad_down")
    return g_gate, g_up, g_down


def local_step(x0, tgt, pos_col, mod, wfull, p):
    T = x0.shape[0]
    nb = T // SEQ
    mod_ff1, mod_mix, mod_ff2 = mod[:, 0:3], mod[:, 3:6], mod[:, 6:9]
    tabs = rope_tables(pos_col, name="rope_tables")
    bf_pad = jnp.pad(p["b_forget"], ((0, 0), (0, LANE - NH)))

    x1, h1, gate1, up1, y01 = ffn_fwd(x0, mod_ff1, p["g_pre_ff1"], p["g_post_ff1"], wfull["w_ff1_gate"], wfull["w_ff1_up"],
                                      wfull["w_ff1_down"], 0.5, name="ff1_fwd")
    x2, res = mixer_fwd(x1, mod_mix, p["g_pre_mix"], wfull["w_main"], wfull["w_f"], bf_pad, p["g_out_a"], p["g_out_b"],
                        wfull["w_out"], p["g_post_mix"], tabs, nb)
    x3, h2, gate2, up2, y02 = ffn_fwd(x2, mod_ff2, p["g_pre_ff2"], p["g_post_ff2"], wfull["w_ff2_gate"], wfull["w_ff2_up"],
                                      wfull["w_ff2_down"], 0.5, name="ff2_fwd")

    dx3, loss_part = loss_grad(x3, tgt, name="loss_grad")
    dx2, dy02, act2, dgate2, dup2, dmod_ff2, dgpre2, dgpost2 = ffn_bwd(
        dx3, x2, y02, mod_ff2, p["g_pre_ff2"], p["g_post_ff2"], gate2, up2, wfull["w_ff2_gate"], wfull["w_ff2_up"],
        wfull["w_ff2_down"], 0.5, name="ff2_bwd")
    gw = {}
    gw["w_ff2_gate"], gw["w_ff2_up"], gw["w_ff2_down"] = ffn_grads(h2, dy02, act2, dgate2, dup2, "ff2")
    dx1, dmod_mix, gmix = mixer_bwd(dx2, x1, mod_mix, p["g_pre_mix"], wfull["w_main"], wfull["w_f"], bf_pad, p["g_out_a"],
                                    p["g_out_b"], wfull["w_out"], p["g_post_mix"], tabs, res, nb)
    gw["w_in"], gw["w_out"] = gmix["w_in"], gmix["w_out"]
    dx0, dy01, act1, dgate1, dup1, dmod_ff1, dgpre1, dgpost1 = ffn_bwd(
        dx1, x0, y01, mod_ff1, p["g_pre_ff1"], p["g_post_ff1"], gate1, up1, wfull["w_ff1_gate"], wfull["w_ff1_up"],
        wfull["w_ff1_down"], 0.5, name="ff1_bwd")
    gw["w_ff1_gate"], gw["w_ff1_up"], gw["w_ff1_down"] = ffn_grads(h1, dy01, act1, dgate1, dup1, "ff1")
    dmod = jnp.concatenate([dmod_ff1, dmod_mix, dmod_ff2], axis=1).reshape(nb, 9 * D)
    small = dict(g_pre_ff1=dgpre1, g_post_ff1=dgpost1, g_pre_mix=gmix["g_pre"], g_post_mix=gmix["g_post"], g_pre_ff2=dgpre2,
                 g_post_ff2=dgpost2, g_out_a=gmix["goa"], g_out_b=gmix["gob"], b_forget=gmix["b_forget"])
    return loss_part, dx0, dmod, gw, small


def kernel(x, c, positions, w_ada, b_ada, g_pre_ff1, g_post_ff1, w_ff1_gate, w_ff1_up, w_ff1_down, g_pre_mix, g_post_mix, w_in, b_forget, g_out_a, g_out_b, w_out, g_pre_ff2, g_post_ff2, w_ff2_gate, w_ff2_up, w_ff2_down, loss_target, m_w_ada, m_b_ada, m_g_pre_ff1, m_g_post_ff1, m_w_ff1_gate, m_w_ff1_up, m_w_ff1_down, m_g_pre_mix, m_g_post_mix, m_w_in, m_b_forget, m_g_out_a, m_g_out_b, m_w_out, m_g_pre_ff2, m_g_post_ff2, m_w_ff2_gate, m_w_ff2_up, m_w_ff2_down, v_w_ada, v_b_ada, v_g_pre_ff1, v_g_post_ff1, v_w_ff1_gate, v_w_ff1_up, v_w_ff1_down, v_g_pre_mix, v_g_post_mix, v_w_in, v_b_forget, v_g_out_a, v_g_out_b, v_w_out, v_g_pre_ff2, v_g_post_ff2, v_w_ff2_gate, v_w_ff2_up, v_w_ff2_down):
    args = dict(locals())
    nb = x.shape[0]
    T = nb * SEQ
    ax, ay, ac = lax.axis_index("x"), lax.axis_index("y"), lax.axis_index("c")
    shard = 2 * ax + ay
    cidx = jnp.reshape(ac, (1,)).astype(jnp.int32)
    sidx = jnp.reshape(shard, (1,)).astype(jnp.int32)

    big = ["w_ff1_gate", "w_ff1_up", "w_ff1_down", "w_in", "w_out", "w_ff2_gate", "w_ff2_up", "w_ff2_down"]
    vecs = ["g_pre_ff1", "g_post_ff1", "g_pre_mix", "g_post_mix", "g_pre_ff2", "g_post_ff2"]

    shards_bf = [args[n][0].astype(BF16) for n in big]
    splits = [512, 512, 352, 512, 128, 512, 512, 352]
    gathered = all_gather_shards(shards_bf, splits, name="all_gather_weights")
    wfull = {}
    for n, o in zip(big, gathered):
        if n.endswith("gate") or n.endswith("up"):
            wfull[n] = _unshard_cols(o, DFF_PAD)
        elif n.endswith("down"):
            wfull[n] = jnp.pad(o.reshape(DFF, D), ((0, DFF_PAD - DFF), (0, 0)))
        elif n == "w_in":
            full = _unshard_cols(o, IN_COLS)
            wfull["w_main"] = full[:, :IN_MAIN]
            wfull["w_f"] = jnp.pad(full[:, IN_MAIN:], ((0, 0), (0, LANE - NH)))
        else:
            wfull[n] = o.reshape(D, D)

    ncol = w_ada.shape[2]
    c_all = all_gather8(c, name="all_gather_c").reshape(N_DEV * nb, D)
    b_loc = lax.dynamic_slice(b_ada, (0, shard * ncol), (1, ncol))
    mod_loc = ada_fwd(c_all, w_ada[0], b_loc, name="ada_fwd")
    mod_g = all_gather8(mod_loc, name="all_gather_mod")
    row0 = (4 * ax + 2 * ay + ac) * nb
    mod_rows = lax.dynamic_slice(mod_g, (0, row0, 0), (N_DEV, nb, ncol))
    mod = jnp.concatenate([mod_rows[2 * s] for s in range(N_SHARD)], axis=-1).reshape(nb, 9, D)

    small_in = dict(g_pre_ff1=g_pre_ff1, g_post_ff1=g_post_ff1, g_pre_mix=g_pre_mix, g_post_mix=g_post_mix, g_pre_ff2=g_pre_ff2,
                    g_post_ff2=g_post_ff2, g_out_a=g_out_a, g_out_b=g_out_b, b_forget=b_forget)
    loss_part, dx0, dmod, gw, small = local_step(x.reshape(T, D), loss_target.reshape(T, D), positions.reshape(T, 1), mod, wfull,
                                                 small_in)

    dmod_all = all_gather8(dmod, name="all_gather_dmod").reshape(N_DEV * nb, 9 * D)
    dmod_loc = lax.dynamic_slice(dmod_all, (0, shard * ncol), (N_DEV * nb, ncol))
    g_w_ada = ada_bwd(c_all, dmod_loc, name="ada_bwd")

    def shard_blocked(n, g):
        if n.endswith("gate") or n.endswith("up"):
            return _shard_cols(g, DFF)
        if n.endswith("down"):
            return g[:DFF].reshape(N_SHARD, DFF // N_SHARD, D)
        if n == "w_in":
            return _shard_cols(g, IN_COLS)
        return g.reshape(N_SHARD, D // N_SHARD, D)

    gsb = [shard_blocked(n, gw[n]) for n in big]
    ras = sibling_send_half(gsb, name="grad_sibling_send")
    hs = [pair_sum(g, ra, cidx, name=f"grad_pair_sum_{n}") for n, g, ra in zip(big, gsb, ras)]
    rbs = chip_scatter(hs, name="grad_chip_scatter")
    ghs = [chip_sum(h, rb, sidx, name=f"grad_chip_sum_{n}") for n, h, rb in zip(big, hs, rbs)]
    gfull = dict(zip(big, sibling_share_half(ghs, name="grad_sibling_share")))
    gfull["w_ada"] = g_w_ada

    row6 = jnp.concatenate([small["g_out_a"], small["g_out_b"]], axis=1)
    row7 = jnp.concatenate([small["b_forget"], loss_part[0:1, 0:1], jnp.zeros((1, D - NH - 1), F32)], axis=1)
    pack = jnp.concatenate([small[n] for n in vecs] + [row6, row7], axis=0)
    packed = all_gather8(pack, name="all_gather_small").reshape(N_DEV, 8 * D)

    def pack_state(pre):
        r6 = jnp.concatenate([args[pre + "g_out_a"], args[pre + "g_out_b"]], axis=1)
        r7 = jnp.pad(args[pre + "b_forget"], ((0, 0), (0, D - NH)))
        return jnp.concatenate([args[pre + n] for n in vecs] + [r6, r7], axis=0).reshape(1, 8 * D)

    sg, sd, sm, sv = (t.reshape(8, D) for t in vec_adam(packed, pack_state(""), pack_state("m_"), pack_state("v_"), name="adam_small"))

    def unpack(t):
        out = {n: t[i:i + 1] for i, n in enumerate(vecs)}
        out["g_out_a"], out["g_out_b"], out["b_forget"] = t[6:7, :WG], t[6:7, WG:], t[7:8, :NH]
        return out

    outs = dict(grad=unpack(sg), delta=unpack(sd), new_m=unpack(sm), new_v=unpack(sv))
    loss = sg[7, NH]
    outs["grad"]["b_ada"], outs["delta"]["b_ada"], outs["new_m"]["b_ada"], outs["new_v"]["b_ada"] = vec_adam(
        dmod_all, b_ada, m_b_ada, v_b_ada, name="adam_b_ada")

    for n in big + ["w_ada"]:
        g = gfull[n]
        rows = g.shape[0]
        tr = 128 if rows % 128 == 0 else 344
        d, m2, v2 = adam_update(args[n][0], g, args["m_" + n][0], args["v_" + n][0], tr, name="adam_" + n)
        outs["grad"][n], outs["delta"][n], outs["new_m"][n], outs["new_v"][n] = g[None], d[None], m2[None], v2[None]

    order = ["w_ada", "b_ada", "g_pre_ff1", "g_post_ff1", "w_ff1_gate", "w_ff1_up", "w_ff1_down", "g_pre_mix", "g_post_mix", "w_in",
             "b_forget", "g_out_a", "g_out_b", "w_out", "g_pre_ff2", "g_post_ff2", "w_ff2_gate", "w_ff2_up", "w_ff2_down"]
    result = [loss, dx0.reshape(nb, SEQ, D)]
    for kind in ("grad", "delta", "new_m", "new_v"):
        result += [outs[kind][n] for n in order]
    return tuple(result)
```

```python
import functools
import math

import jax
import jax.numpy as jnp
from jax import lax
from jax.experimental import pallas as pl
from jax.experimental.pallas import tpu as pltpu

D = 1024
SEQ = 2048
HD = 64
NH = 8
WG = NH * HD
DFF = 2752
DFF_PAD = 2816
IN_MAIN = 6 * WG
IN_COLS = IN_MAIN + NH
N_SHARD = 4
N_DEV = 8
LANE = 128
QB = 128
FB = 256
PATTERNS = ((1, 16), (4, 4), (16, 1))
ROPE_THETA = 500000.0
EPS = 1e-6
NEG = -1e30
ATTN_SCALE = HD ** -0.5
TM = 512
TM_BWD = 256
VMEM_LIMIT = 56 * 1024 * 1024

ADAM_LR, ADAM_B1, ADAM_B2, ADAM_EPS, ADAM_WD, ADAM_STEP = 0.001, 0.9, 0.999, 1e-08, 0.01, 10

F32 = jnp.float32
BF16 = jnp.bfloat16
MESH = pl.DeviceIdType.MESH
SDS = jax.ShapeDtypeStruct


def _cp(*sem):
    return pltpu.CompilerParams(dimension_semantics=sem, vmem_limit_bytes=VMEM_LIMIT)


def _dot(a, b):
    return jnp.dot(a, b, preferred_element_type=F32)


def _dot_nt(a, b):
    return lax.dot_general(a, b, (((1,), (1,)), ((), ())), preferred_element_type=F32)


def _dot_tn(a, b):
    return lax.dot_general(a, b, (((0,), (0,)), ((), ())), preferred_element_type=F32)


def _rms(xf):
    return lax.rsqrt(jnp.mean(xf * xf, axis=-1, keepdims=True) + EPS)


def _norm_mod_bwd(dh, xf, g, scale):
    r = _rms(xf)
    xh = xf * r
    dsh = jnp.sum(dh, axis=0, keepdims=True)
    dsc = jnp.sum(dh * (xh * g), axis=0, keepdims=True)
    dn = dh * (1.0 + scale)
    dg = jnp.sum(dn * xh, axis=0, keepdims=True)
    dxh = dn * g
    dx = r * (dxh - xh * jnp.mean(dxh * xh, axis=-1, keepdims=True))
    return dx, dsh, dsc, dg


def _post_bwd(dxo, y0, g, mgate, gs):
    r = _rms(y0)
    yh = y0 * r
    dmg = gs * jnp.sum(dxo * (yh * g), axis=0, keepdims=True)
    dy = (gs * mgate) * dxo
    dg = jnp.sum(dy * yh, axis=0, keepdims=True)
    dyh = dy * g
    dy0 = r * (dyh - yh * jnp.mean(dyh * yh, axis=-1, keepdims=True))
    return dy0, dmg, dg


def _mod_map(i, *_):
    return ((i * TM) // SEQ, 0, 0)


FF_TN = 1408
FF_NJ = DFF_PAD // FF_TN


def ffn_fwd(x, mod3, g_pre, g_post, wg, wu, wd, gs, name):
    T = x.shape[0]

    def body(x_ref, mod_ref, gpre_ref, gpost_ref, wg_ref, wu_ref, wd_ref, xo_ref, h_ref, gate_ref, up_ref, y0_ref, hs, acc):
        j = pl.program_id(1)

        @pl.when(j == 0)
        def _():
            xf = x_ref[...]
            h = (xf * _rms(xf) * gpre_ref[...]) * (1.0 + mod_ref[0, 1:2, :]) + mod_ref[0, 0:1, :]
            hb = h.astype(BF16)
            hs[...] = hb
            h_ref[...] = hb
            acc[...] = jnp.zeros_like(acc)

        hb = hs[...]
        gate = _dot(hb, wg_ref[...])
        up = _dot(hb, wu_ref[...])
        gate_ref[...] = gate.astype(BF16)
        up_ref[...] = up.astype(BF16)
        act = gate * jax.nn.sigmoid(gate) * up
        acc[...] += _dot(act.astype(BF16), wd_ref[...])

        @pl.when(j == FF_NJ - 1)
        def _():
            y0 = acc[...]
            y0_ref[...] = y0
            xo_ref[...] = x_ref[...] + (gs * mod_ref[0, 2:3, :]) * (y0 * _rms(y0) * gpost_ref[...])

    tok = pl.BlockSpec((TM, D), lambda i, j: (i, 0))
    vec = pl.BlockSpec((1, D), lambda i, j: (0, 0))
    hid = pl.BlockSpec((TM, FF_TN), lambda i, j: (i, j))
    return pl.pallas_call(
        body, grid=(T // TM, FF_NJ),
        in_specs=[tok, pl.BlockSpec((1, 3, D), _mod_map), vec, vec,
                  pl.BlockSpec((D, FF_TN), lambda i, j: (0, j)), pl.BlockSpec((D, FF_TN), lambda i, j: (0, j)),
                  pl.BlockSpec((FF_TN, D), lambda i, j: (j, 0))],
        out_specs=[tok, tok, hid, hid, tok],
        out_shape=[SDS((T, D), F32), SDS((T, D), BF16), SDS((T, DFF_PAD), BF16), SDS((T, DFF_PAD), BF16), SDS((T, D), F32)],
        scratch_shapes=[pltpu.VMEM((TM, D), BF16), pltpu.VMEM((TM, D), F32)],
        compiler_params=_cp("arbitrary", "arbitrary"), name=name,
    )(x, mod3, g_pre, g_post, wg, wu, wd)


def ffn_bwd(dxo, x, y0, mod3, g_pre, g_post, gate, up, wg, wu, wd, gs, name):
    T = x.shape[0]
    nb = T // SEQ
    tm = TM_BWD
    tiles_per_seq = SEQ // tm

    def body(dxo_ref, x_ref, y0_ref, mod_ref, gpre_ref, gpost_ref, gate_ref, up_ref, wg_ref, wu_ref, wd_ref,
             dx_ref, dy0_ref, act_ref, dgate_ref, dup_ref, dmod_ref, dgpre_ref, dgpost_ref, dy0s, acc):
        i = pl.program_id(0)
        j = pl.program_id(1)

        @pl.when((i == 0) & (j == 0))
        def _():
            dgpre_ref[...] = jnp.zeros_like(dgpre_ref)
            dgpost_ref[...] = jnp.zeros_like(dgpost_ref)

        @pl.when((i % tiles_per_seq == 0) & (j == 0))
        def _():
            dmod_ref[...] = jnp.zeros_like(dmod_ref)

        @pl.when(j == 0)
        def _():
            dy0, dmg, dg = _post_bwd(dxo_ref[...], y0_ref[...], gpost_ref[...], mod_ref[0, 2:3, :], gs)
            dmod_ref[0, 2:3, :] += dmg
            dgpost_ref[...] += dg
            db = dy0.astype(BF16)
            dy0s[...] = db
            dy0_ref[...] = db
            acc[...] = jnp.zeros_like(acc)

        dact = _dot_nt(dy0s[...], wd_ref[...])
        g = gate_ref[...].astype(F32)
        u = up_ref[...].astype(F32)
        sig = jax.nn.sigmoid(g)
        sl = g * sig
        dgate = (dact * u * (sig * (1.0 + g * (1.0 - sig)))).astype(BF16)
        dup = (dact * sl).astype(BF16)
        act_ref[...] = (sl * u).astype(BF16)
        dgate_ref[...] = dgate
        dup_ref[...] = dup
        acc[...] += _dot_nt(dgate, wg_ref[...]) + _dot_nt(dup, wu_ref[...])

        @pl.when(j == FF_NJ - 1)
        def _():
            dx, dsh, dsc, dg = _norm_mod_bwd(acc[...], x_ref[...], gpre_ref[...], mod_ref[0, 1:2, :])
            dx_ref[...] = dxo_ref[...] + dx
            dmod_ref[0, 0:1, :] += dsh
            dmod_ref[0, 1:2, :] += dsc
            dgpre_ref[...] += dg

    tok = pl.BlockSpec((tm, D), lambda i, j: (i, 0))
    vec = pl.BlockSpec((1, D), lambda i, j: (0, 0))
    hid = pl.BlockSpec((tm, FF_TN), lambda i, j: (i, j))
    modspec = pl.BlockSpec((1, 3, D), lambda i, j: ((i * tm) // SEQ, 0, 0))
    return pl.pallas_call(
        body, grid=(T // tm, FF_NJ),
        in_specs=[tok, tok, tok, modspec, vec, vec, hid, hid,
                  pl.BlockSpec((D, FF_TN), lambda i, j: (0, j)), pl.BlockSpec((D, FF_TN), lambda i, j: (0, j)),
                  pl.BlockSpec((FF_TN, D), lambda i, j: (j, 0))],
        out_specs=[tok, tok, hid, hid, hid, modspec, vec, vec],
        out_shape=[SDS((T, D), F32), SDS((T, D), BF16), SDS((T, DFF_PAD), BF16), SDS((T, DFF_PAD), BF16),
                   SDS((T, DFF_PAD), BF16), SDS((nb, 3, D), F32), SDS((1, D), F32), SDS((1, D), F32)],
        scratch_shapes=[pltpu.VMEM((tm, D), BF16), pltpu.VMEM((tm, D), F32)],
        compiler_params=_cp("arbitrary", "arbitrary"), name=name,
    )(dxo, x, y0, mod3, g_pre, g_post, gate, up, wg, wu, wd)


def matmul_tn(a, b, tm, tn, tk, name):
    T, M = a.shape
    N = b.shape[1]
    nk = T // tk

    def body(a_ref, b_ref, o_ref):
        @pl.when(pl.program_id(2) == 0)
        def _():
            o_ref[...] = jnp.zeros_like(o_ref)

        o_ref[...] += _dot_tn(a_ref[...], b_ref[...])

    return pl.pallas_call(
        body, grid=(M // tm, N // tn, nk),
        in_specs=[pl.BlockSpec((tk, tm), lambda i, j, k: (k, i)), pl.BlockSpec((tk, tn), lambda i, j, k: (k, j))],
        out_specs=pl.BlockSpec((tm, tn), lambda i, j, k: (i, j)),
        out_shape=SDS((M, N), F32),
        compiler_params=_cp("arbitrary", "arbitrary", "arbitrary"), name=name,
    )(a, b)


def loss_grad(y, tgt, name):
    T = y.shape[0]

    def body(y_ref, t_ref, dy_ref, l_ref):
        @pl.when(pl.program_id(0) == 0)
        def _():
            l_ref[...] = jnp.zeros_like(l_ref)

        e = y_ref[...] - t_ref[...]
        dy_ref[...] = e * (1.0 / D)
        l_ref[...] += jnp.sum(e * e) * (0.5 / D)

    tok = pl.BlockSpec((TM, D), lambda i: (i, 0))
    return pl.pallas_call(
        body, grid=(T // TM,), in_specs=[tok, tok],
        out_specs=[tok, pl.BlockSpec((8, LANE), lambda i: (0, 0))],
        out_shape=[SDS((T, D), F32), SDS((8, LANE), F32)],
        compiler_params=_cp("arbitrary"), name=name,
    )(y, tgt)


def rope_tables(pos_col, name):
    T = pos_col.shape[0]
    tm = 1024

    def body(p_ref, c_ref, s1_ref, s2_ref):
        lane = lax.broadcasted_iota(jnp.int32, (1, LANE), 1)
        l64 = lane % HD
        inv_freq = jnp.exp((l64 % 8).astype(F32) * (-math.log(ROPE_THETA) / 8.0))
        ang = p_ref[...].astype(F32) * inv_freq
        cs = jnp.cos(ang)
        sn = jnp.sin(ang)
        c_ref[...] = jnp.where(l64 < 16, cs, 1.0)
        s1_ref[...] = jnp.where(l64 < 8, -sn, 0.0)
        s2_ref[...] = jnp.where((l64 >= 8) & (l64 < 16), sn, 0.0)

    tab = pl.BlockSpec((tm, LANE), lambda i: (i, 0))
    return pl.pallas_call(
        body, grid=(T // tm,), in_specs=[pl.BlockSpec((tm, 1), lambda i: (i, 0))], out_specs=[tab, tab, tab],
        out_shape=[SDS((T, LANE), F32)] * 3, compiler_params=_cp("arbitrary"), name=name,
    )(pos_col)


def mixer_proj(x, mod3, g_pre, w_main, w_f, rc, rs1, rs2, name):
    T = x.shape[0]
    tn = 1024

    def body(x_ref, mod_ref, g_ref, w_ref, wf_ref, c_ref, s1_ref, s2_ref, h_ref, pa_ref, pb_ref, f_ref, hs):
        j = pl.program_id(1)

        @pl.when(j == 0)
        def _():
            xf = x_ref[...]
            h = (xf * _rms(xf) * g_ref[...]) * (1.0 + mod_ref[0, 1:2, :]) + mod_ref[0, 0:1, :]
            hb = h.astype(BF16)
            hs[...] = hb
            h_ref[...] = hb
            f_ref[...] = _dot(hb, wf_ref[...])

        pr = _dot(hs[...], w_ref[...])

        @pl.when(j == 0)
        def _():
            c, s1, s2 = c_ref[...], s1_ref[...], s2_ref[...]
            for k in range(tn // LANE):
                t = pr[:, k * LANE:(k + 1) * LANE]
                pa_ref[:, k * LANE:(k + 1) * LANE] = t * c + pltpu.roll(t, LANE - 8, 1) * s1 + pltpu.roll(t, 8, 1) * s2

        @pl.when(j == 1)
        def _():
            pa_ref[:, 2 * WG:3 * WG] = pr[:, :WG]
            pb_ref[:, 0:WG] = pr[:, WG:].astype(BF16)

        @pl.when(j == 2)
        def _():
            pb_ref[:, WG:3 * WG] = pr.astype(BF16)

    tok = pl.BlockSpec((TM, D), lambda i, j: (i, 0))
    vec = pl.BlockSpec((1, D), lambda i, j: (0, 0))
    tab = pl.BlockSpec((TM, LANE), lambda i, j: (i, 0))
    grp = pl.BlockSpec((TM, 3 * WG), lambda i, j: (i, 0))
    return pl.pallas_call(
        body, grid=(T // TM, IN_MAIN // tn),
        in_specs=[tok, pl.BlockSpec((1, 3, D), _mod_map), vec, pl.BlockSpec((D, tn), lambda i, j: (0, j)),
                  pl.BlockSpec((D, LANE), lambda i, j: (0, 0)), tab, tab, tab],
        out_specs=[tok, grp, grp, tab],
        out_shape=[SDS((T, D), BF16), SDS((T, 3 * WG), F32), SDS((T, 3 * WG), BF16), SDS((T, LANE), F32)],
        scratch_shapes=[pltpu.VMEM((TM, D), BF16)],
        compiler_params=_cp("arbitrary", "arbitrary"), name=name,
    )(x, mod3, g_pre, w_main, w_f, rc, rs1, rs2)


def _head_lanes():
    return lax.broadcasted_iota(jnp.int32, (1, LANE), 1) < HD


def _pair(m0, a, b):
    return jnp.where(m0, a, b)


def _band_rows(i, d, nbc):
    if nbc == 1:
        return i, i, 0
    r, mb = i // nbc, i % nbc
    return r + mb * (QB * d), r + jnp.maximum(mb - 1, 0) * (QB * d), jnp.where(mb > 0, QB, 0)


def _rows(start, size, d):
    return pl.ds(pl.multiple_of(start, QB), size) if d == 1 else pl.ds(start, size, stride=d)


def _band_valid(span, off):
    rq = lax.broadcasted_iota(jnp.int32, (QB, span), 0)
    rel = lax.broadcasted_iota(jnp.int32, (QB, span), 1) - off
    return (rel <= rq) & (rel >= rq - QB)


def band_fwd(pa, name):
    T = pa.shape[0]
    B = T // SEQ
    NP = WG // LANE

    def body(q_ref, k_ref, v_ref, out_ref, lse_ref, o_s, l_s):
        m0 = _head_lanes()
        for pidx, (d, nbc) in enumerate(PATTERNS):
            span = QB if nbc == 1 else 2 * QB

            def blk(i, carry, pidx=pidx, d=d, nbc=nbc, span=span):
                qs, ks, off = _band_rows(i, d, nbc)
                q = q_ref[_rows(qs, QB, d), :]
                k = k_ref[_rows(ks, span, d), :].astype(BF16)
                v = v_ref[_rows(ks, span, d), :].astype(BF16)
                valid = _band_valid(span, off)
                os_, ls_ = [], []
                for qh in (jnp.where(m0, q, 0.0), jnp.where(m0, 0.0, q)):
                    s = jnp.where(valid, _dot_nt(qh.astype(BF16), k) * ATTN_SCALE, NEG)
                    m = jnp.max(s, axis=-1, keepdims=True)
                    p = jnp.exp(s - m)
                    l = jnp.sum(p, axis=-1, keepdims=True)
                    os_.append(_dot(p.astype(BF16), v) / l)
                    ls_.append(m + jnp.log(l))
                o_s[pidx, _rows(qs, QB, d), :] = _pair(m0, os_[0], os_[1])
                l_s[pidx, _rows(qs, QB, d), :] = _pair(m0, ls_[0], ls_[1]) + jnp.zeros((QB, LANE), F32)
                return carry

            lax.fori_loop(0, SEQ // QB, blk, 0)
        for c in range(SEQ // FB):
            sl = slice(c * FB, (c + 1) * FB)
            a, b, e = l_s[0, sl, :], l_s[1, sl, :], l_s[2, sl, :]
            m = jnp.maximum(jnp.maximum(a, b), e)
            L = m + jnp.log(jnp.exp(a - m) + jnp.exp(b - m) + jnp.exp(e - m))
            out_ref[sl, :] = jnp.exp(a - L) * o_s[0, sl, :] + jnp.exp(b - L) * o_s[1, sl, :] + jnp.exp(e - L) * o_s[2, sl, :]
            lse_ref[sl, :] = L

    blk_of = lambda g: pl.BlockSpec((SEQ, LANE), lambda b, hp, g=g: (b, g * NP + hp))
    return pl.pallas_call(
        body, grid=(B, NP), in_specs=[blk_of(0), blk_of(1), blk_of(2)], out_specs=[blk_of(0), blk_of(0)],
        out_shape=[SDS((T, WG), F32), SDS((T, WG), F32)],
        scratch_shapes=[pltpu.VMEM((3, SEQ, LANE), F32), pltpu.VMEM((3, SEQ, LANE), F32)],
        compiler_params=_cp("arbitrary", "arbitrary"), name=name,
    )(pa, pa, pa)


def _pair_rowsum(m0, prod):
    s0 = jnp.sum(jnp.where(m0, prod, 0.0), axis=-1, keepdims=True)
    return _pair(m0, s0, jnp.sum(prod, axis=-1, keepdims=True) - s0)


def band_bwd(pa, do, out, lse, name):
    T = pa.shape[0]
    B = T // SEQ
    NP = WG // LANE

    def body(q_ref, k_ref, v_ref, do_ref, out_ref, l_ref, dq_ref, dk_ref, dv_ref, d_s):
        m0 = _head_lanes()
        dq_ref[...] = jnp.zeros_like(dq_ref)
        dk_ref[...] = jnp.zeros_like(dk_ref)
        dv_ref[...] = jnp.zeros_like(dv_ref)
        for c in range(SEQ // FB):
            sl = slice(c * FB, (c + 1) * FB)
            d_s[sl, :] = _pair_rowsum(m0, do_ref[sl, :] * out_ref[sl, :])
        for d, nbc in PATTERNS:
            span = QB if nbc == 1 else 2 * QB

            def blk(i, carry, d=d, nbc=nbc, span=span):
                qs, ks, off = _band_rows(i, d, nbc)
                qrow, krow = _rows(qs, QB, d), _rows(ks, span, d)
                q = q_ref[qrow, :]
                k = k_ref[krow, :].astype(BF16)
                v = v_ref[krow, :].astype(BF16)
                do_q = do_ref[qrow, :]
                lq = l_ref[qrow, :]
                dq_ = d_s[qrow, :]
                valid = _band_valid(span, off)
                qb = q.astype(BF16)
                dob = do_q.astype(BF16)
                parts = []
                for h in range(2):
                    mh = m0 if h == 0 else jnp.logical_not(m0)
                    col = slice(h * HD, h * HD + 1)
                    s = jnp.where(valid, _dot_nt(jnp.where(mh, q, 0.0).astype(BF16), k) * ATTN_SCALE, NEG)
                    p = jnp.exp(s - lq[:, col])
                    dp = _dot_nt(jnp.where(mh, do_q, 0.0).astype(BF16), v)
                    ds = (p * (dp - dq_[:, col]) * ATTN_SCALE).astype(BF16)
                    parts.append((_dot(ds, k), _dot_tn(ds, qb), _dot_tn(p.astype(BF16), dob)))
                dq_ref[qrow, :] += _pair(m0, parts[0][0], parts[1][0])
                dk_ref[krow, :] += _pair(m0, parts[0][1], parts[1][1])
                dv_ref[krow, :] += _pair(m0, parts[0][2], parts[1][2])
                return carry

            lax.fori_loop(0, SEQ // QB, blk, 0)

    blk_of = lambda g: pl.BlockSpec((SEQ, LANE), lambda b, hp, g=g: (b, g * NP + hp))
    return pl.pallas_call(
        body, grid=(B, NP), in_specs=[blk_of(0), blk_of(1), blk_of(2), blk_of(0), blk_of(0), blk_of(0)],
        out_specs=[blk_of(0)] * 3, out_shape=[SDS((T, WG), F32)] * 3,
        scratch_shapes=[pltpu.VMEM((SEQ, LANE), F32)],
        compiler_params=_cp("arbitrary", "arbitrary"), name=name,
    )(pa, pa, pa, do, out, lse)


def _causal_mask():
    r = lax.broadcasted_iota(jnp.int32, (FB, FB), 0)
    c = lax.broadcasted_iota(jnp.int32, (FB, FB), 1)
    return r >= c


def fox_fwd(pb, fcol, frow, name):
    T = pb.shape[0]
    B = T // SEQ
    NP = WG // LANE
    n = SEQ // FB

    def body(q_ref, k_ref, v_ref, fc_ref, fr_ref, o_ref, lse_ref):
        i = pl.program_id(2)
        m0 = _head_lanes()
        q = q_ref[...]
        zero = jnp.zeros_like(q)
        qh = (jnp.where(m0, q, zero), jnp.where(m0, zero, q))
        fq = (fc_ref[0, 0], fc_ref[0, 1])

        def step(j, carry, masked):
            rows = pl.ds(pl.multiple_of(j * FB, FB), FB)
            kj = k_ref[rows, :]
            vj = v_ref[rows, :]
            new = []
            for h in range(2):
                m, l, acc = carry[h]
                s = _dot_nt(qh[h], kj) * ATTN_SCALE + fq[h] - fr_ref[0, h, j]
                if masked:
                    s = jnp.where(_causal_mask(), s, NEG)
                m2 = jnp.maximum(m, jnp.max(s, axis=-1, keepdims=True))
                a = jnp.exp(m - m2)
                p = jnp.exp(s - m2)
                new.append((m2, a * l + jnp.sum(p, axis=-1, keepdims=True), a * acc + _dot(p.astype(BF16), vj)))
            return tuple(new)

        one = (jnp.full((FB, 1), NEG, F32), jnp.zeros((FB, 1), F32), jnp.zeros((FB, LANE), F32))
        carry = lax.fori_loop(0, i, lambda j, cr: step(j, cr, False), (one, one))
        (ma, la, acca), (mb, lb, accb) = step(i, carry, True)
        o_ref[...] = _pair(m0, acca / la, accb / lb)
        lse_ref[...] = _pair(m0, ma + jnp.log(la), mb + jnp.log(lb))

    qblk = pl.BlockSpec((FB, LANE), lambda b, hp, i: (b * n + i, hp))
    full = lambda g: pl.BlockSpec((SEQ, LANE), lambda b, hp, i, g=g: (b, g * NP + hp))
    return pl.pallas_call(
        body, grid=(B, NP, n),
        in_specs=[qblk, full(1), full(2), pl.BlockSpec((1, 2, FB, 1), lambda b, hp, i: (b, hp, i, 0)),
                  pl.BlockSpec((1, 2, n, 1, FB), lambda b, hp, i: (b, hp, 0, 0, 0))],
        out_specs=[qblk, qblk], out_shape=[SDS((T, WG), F32), SDS((T, WG), F32)],
        compiler_params=_cp("arbitrary", "arbitrary", "arbitrary"), name=name,
    )(pb, pb, pb, fcol, frow)


def fox_bwd(pb, do, out, lse, fcol, frow, name):
    T = pb.shape[0]
    B = T // SEQ
    NP = WG // LANE
    n = SEQ // FB

    def body(q_ref, k_ref, v_ref, do_ref, out_ref, l_ref, fc_ref, fr_ref, dq_ref, dk_ref, dv_ref, dfq_ref, dfk_ref, d_s):
        j = pl.program_id(2)
        m0 = _head_lanes()
        masks = (m0, jnp.logical_not(m0))

        @pl.when(j == 0)
        def _():
            dq_ref[...] = jnp.zeros_like(dq_ref)
            dfq_ref[...] = jnp.zeros_like(dfq_ref)
            for c in range(n):
                sl = slice(c * FB, (c + 1) * FB)
                d_s[sl, :] = _pair_rowsum(m0, do_ref[sl, :] * out_ref[sl, :])

        kj = k_ref[...]
        vj = v_ref[...]
        fk = (fr_ref[0, 0, 0], fr_ref[0, 1, 0])

        def step(i, carry, masked):
            rows = pl.ds(pl.multiple_of(i * FB, FB), FB)
            qi = q_ref[rows, :]
            doi = do_ref[rows, :]
            li = l_ref[rows, :]
            di = d_s[rows, :]
            dob = doi.astype(BF16)
            zero = jnp.zeros_like(qi)
            new, dqs = [], []
            for h in range(2):
                dk, dv, dfk = carry[h]
                col = slice(h * HD, h * HD + 1)
                s = _dot_nt(jnp.where(masks[h], qi, zero), kj) * ATTN_SCALE + fc_ref[0, h, rows, :] - fk[h]
                if masked:
                    s = jnp.where(_causal_mask(), s, NEG)
                p = jnp.exp(s - li[:, col])
                ds = p * (_dot_nt(jnp.where(masks[h], doi, 0.0).astype(BF16), vj) - di[:, col])
                dsb = (ds * ATTN_SCALE).astype(BF16)
                dqs.append(_dot(dsb, kj))
                dfq_ref[0, h, rows, :] += jnp.sum(ds, axis=-1, keepdims=True)
                new.append((dk + _dot_tn(dsb, qi), dv + _dot_tn(p.astype(BF16), dob), dfk - jnp.sum(ds, axis=0, keepdims=True)))
            dq_ref[rows, :] += _pair(m0, dqs[0], dqs[1])
            return tuple(new)

        one = (jnp.zeros((FB, LANE), F32), jnp.zeros((FB, LANE), F32), jnp.zeros((1, FB), F32))
        carry = step(j, (one, one), True)
        (dka, dva, dfka), (dkb, dvb, dfkb) = lax.fori_loop(j + 1, n, lambda i, cr: step(i, cr, False), carry)
        dk_ref[...] = _pair(m0, dka, dkb)
        dv_ref[...] = _pair(m0, dva, dvb)
        dfk_ref[0, 0, 0] = dfka
        dfk_ref[0, 1, 0] = dfkb

    kblk = lambda g: pl.BlockSpec((FB, LANE), lambda b, hp, j, g=g: (b * n + j, g * NP + hp))
    full = pl.BlockSpec((SEQ, LANE), lambda b, hp, j: (b, hp))
    colf = pl.BlockSpec((1, 2, SEQ, 1), lambda b, hp, j: (b, hp, 0, 0))
    rowb = pl.BlockSpec((1, 2, 1, 1, FB), lambda b, hp, j: (b, hp, j, 0, 0))
    return pl.pallas_call(
        body, grid=(B, NP, n), in_specs=[full, kblk(1), kblk(2), full, full, full, colf, rowb],
        out_specs=[full, kblk(0), kblk(0), colf, rowb],
        out_shape=[SDS((T, WG), F32), SDS((T, WG), F32), SDS((T, WG), F32), SDS((B, NH, SEQ, 1), F32),
                   SDS((B, NH, n, 1, FB), F32)],
        scratch_shapes=[pltpu.VMEM((SEQ, LANE), F32)],
        compiler_params=_cp("arbitrary", "arbitrary", "arbitrary"), name=name,
    )(pb, pb, pb, do, out, lse, fcol, frow)


def _tri(lower):
    r = lax.broadcasted_iota(jnp.int32, (LANE, LANE), 0)
    c = lax.broadcasted_iota(jnp.int32, (LANE, LANE), 1)
    return ((r >= c) if lower else (r <= c)).astype(F32)


def _tri_dot(t, xblk):
    return jnp.dot(t, xblk, precision=lax.Precision.HIGHEST, preferred_element_type=F32)


def forget_cumsum(flog, bias, name):
    B, S, _ = flog.shape

    def body(f_ref, b_ref, o_ref):
        t = _tri(True)
        carry = jnp.zeros((1, LANE), F32)
        for blk in range(S // LANE):
            z = f_ref[0, blk * LANE:(blk + 1) * LANE, :] + b_ref[...]
            lf = jnp.minimum(z, 0.0) - jnp.log(1.0 + jnp.exp(-jnp.abs(z)))
            cs = _tri_dot(t, lf) + carry
            o_ref[0, blk * LANE:(blk + 1) * LANE, :] = cs
            carry = cs[LANE - 1:LANE, :]

    spec = pl.BlockSpec((1, S, LANE), lambda b: (b, 0, 0))
    return pl.pallas_call(
        body, grid=(B,), in_specs=[spec, pl.BlockSpec((1, LANE), lambda b: (0, 0))], out_specs=spec,
        out_shape=SDS((B, S, LANE), F32), compiler_params=_cp("arbitrary"), name=name,
    )(flog, bias)


def forget_cumsum_bwd(dF, flog, bias, name):
    B, S, _ = flog.shape

    def body(d_ref, f_ref, b_ref, o_ref, db_ref):
        @pl.when(pl.program_id(0) == 0)
        def _():
            db_ref[...] = jnp.zeros_like(db_ref)

        t = _tri(False)
        carry = jnp.zeros((1, LANE), F32)
        tot = jnp.zeros((1, LANE), F32)
        for blk in reversed(range(S // LANE)):
            sl = slice(blk * LANE, (blk + 1) * LANE)
            rc = _tri_dot(t, d_ref[0, sl, :]) + carry
            carry = rc[0:1, :]
            z = f_ref[0, sl, :] + b_ref[...]
            dz = rc * jax.nn.sigmoid(-z)
            o_ref[0, sl, :] = dz
            tot = tot + jnp.sum(dz, axis=0, keepdims=True)
        db_ref[...] += tot

    spec = pl.BlockSpec((1, S, LANE), lambda b: (b, 0, 0))
    vec = pl.BlockSpec((1, LANE), lambda b: (0, 0))
    return pl.pallas_call(
        body, grid=(B,), in_specs=[spec, spec, vec], out_specs=[spec, vec],
        out_shape=[SDS((B, S, LANE), F32), SDS((1, LANE), F32)], compiler_params=_cp("arbitrary"), name=name,
    )(dF, flog, bias)


def mixer_out_fwd(oa, ob, goa, gob, w_out, g_post, x, mod3, name):
    T = x.shape[0]

    def body(oa_ref, ob_ref, goa_ref, gob_ref, w_ref, gp_ref, x_ref, mod_ref, xo_ref, mg_ref, y0_ref):
        a = oa_ref[...]
        b = ob_ref[...]
        mg = jnp.concatenate([a * _rms(a) * goa_ref[...], b * _rms(b) * gob_ref[...]], axis=-1).astype(BF16)
        mg_ref[...] = mg
        y0 = _dot(mg, w_ref[...])
        y0_ref[...] = y0
        xo_ref[...] = x_ref[...] + mod_ref[0, 2:3, :] * (y0 * _rms(y0) * gp_ref[...])

    tok = pl.BlockSpec((TM, D), lambda i: (i, 0))
    half = pl.BlockSpec((TM, WG), lambda i: (i, 0))
    hv = pl.BlockSpec((1, WG), lambda i: (0, 0))
    return pl.pallas_call(
        body, grid=(T // TM,),
        in_specs=[half, half, hv, hv, pl.BlockSpec((D, D), lambda i: (0, 0)), pl.BlockSpec((1, D), lambda i: (0, 0)), tok,
                  pl.BlockSpec((1, 3, D), _mod_map)],
        out_specs=[tok, tok, tok], out_shape=[SDS((T, D), F32), SDS((T, D), BF16), SDS((T, D), F32)],
        compiler_params=_cp("arbitrary"), name=name,
    )(oa, ob, goa, gob, w_out, g_post, x, mod3)


def mixer_out_bwd(dxo, y0, mod3, g_post, w_out, oa, ob, goa, gob, name):
    T = dxo.shape[0]
    nb = T // SEQ
    tiles_per_seq = SEQ // TM

    def body(dxo_ref, y0_ref, mod_ref, gp_ref, w_ref, oa_ref, ob_ref, goa_ref, gob_ref,
             dy0_ref, doa_ref, dob_ref, dmg_ref, dgp_ref, dgoa_ref, dgob_ref):
        i = pl.program_id(0)

        @pl.when(i == 0)
        def _():
            dgp_ref[...] = jnp.zeros_like(dgp_ref)
            dgoa_ref[...] = jnp.zeros_like(dgoa_ref)
            dgob_ref[...] = jnp.zeros_like(dgob_ref)

        @pl.when(i % tiles_per_seq == 0)
        def _():
            dmg_ref[...] = jnp.zeros_like(dmg_ref)

        dy0, dmg, dg = _post_bwd(dxo_ref[...], y0_ref[...], gp_ref[...], mod_ref[0, 2:3, :], 1.0)
        dmg_ref[0] += dmg
        dgp_ref[...] += dg
        db = dy0.astype(BF16)
        dy0_ref[...] = db
        dm = _dot_nt(db, w_ref[...])
        for o_ref, g_ref, do_ref, dg_ref, sl in ((oa_ref, goa_ref, doa_ref, dgoa_ref, slice(0, WG)),
                                                  (ob_ref, gob_ref, dob_ref, dgob_ref, slice(WG, 2 * WG))):
            o = o_ref[...]
            r = _rms(o)
            oh = o * r
            d = dm[:, sl]
            dg_ref[...] += jnp.sum(d * oh, axis=0, keepdims=True)
            dh = d * g_ref[...]
            do_ref[...] = r * (dh - oh * jnp.mean(dh * oh, axis=-1, keepdims=True))

    tok = pl.BlockSpec((TM, D), lambda i: (i, 0))
    half = pl.BlockSpec((TM, WG), lambda i: (i, 0))
    hv = pl.BlockSpec((1, WG), lambda i: (0, 0))
    vec = pl.BlockSpec((1, D), lambda i: (0, 0))
    return pl.pallas_call(
        body, grid=(T // TM,),
        in_specs=[tok, tok, pl.BlockSpec((1, 3, D), _mod_map), vec, pl.BlockSpec((D, D), lambda i: (0, 0)), half, half, hv, hv],
        out_specs=[tok, half, half, pl.BlockSpec((1, 1, D), _mod_map), vec, hv, hv],
        out_shape=[SDS((T, D), BF16), SDS((T, WG), F32), SDS((T, WG), F32), SDS((nb, 1, D), F32), SDS((1, D), F32),
                   SDS((1, WG), F32), SDS((1, WG), F32)],
        compiler_params=_cp("arbitrary"), name=name,
    )(dxo, y0, mod3, g_post, w_out, oa, ob, goa, gob)


def proj_grad_assemble(grads, rc, rs1, rs2, name):
    T = grads[0].shape[0]

    def body(*refs):
        ins, (c_ref, s1_ref, s2_ref, o_ref) = refs[:6], refs[6:]
        c, s1, s2 = c_ref[...], s1_ref[...], s2_ref[...]
        for grp in range(2):
            for k in range(WG // LANE):
                d = ins[grp][:, k * LANE:(k + 1) * LANE]
                un = d * c + pltpu.roll(d * s1, 8, 1) + pltpu.roll(d * s2, LANE - 8, 1)
                o_ref[:, grp * WG + k * LANE:grp * WG + (k + 1) * LANE] = un.astype(BF16)
        for g in range(2, 6):
            o_ref[:, g * WG:(g + 1) * WG] = ins[g][...].astype(BF16)

    half = pl.BlockSpec((TM, WG), lambda i: (i, 0))
    tab = pl.BlockSpec((TM, LANE), lambda i: (i, 0))
    return pl.pallas_call(
        body, grid=(T // TM,), in_specs=[half] * 6 + [tab] * 3, out_specs=pl.BlockSpec((TM, IN_MAIN), lambda i: (i, 0)),
        out_shape=SDS((T, IN_MAIN), BF16), compiler_params=_cp("arbitrary"), name=name,
    )(*grads, rc, rs1, rs2)


def mixer_proj_bwd(dproj, dflog, dxo, x, mod3, g_pre, w_main, w_f, name):
    T = x.shape[0]
    nb = T // SEQ
    tiles_per_seq = SEQ // TM
    tk = 1024
    nj = IN_MAIN // tk

    def body(dp_ref, df_ref, dxo_ref, x_ref, mod_ref, g_ref, w_ref, wf_ref, dx_ref, dmod_ref, dg_ref, acc):
        i = pl.program_id(0)
        j = pl.program_id(1)

        @pl.when((i == 0) & (j == 0))
        def _():
            dg_ref[...] = jnp.zeros_like(dg_ref)

        @pl.when((i % tiles_per_seq == 0) & (j == 0))
        def _():
            dmod_ref[...] = jnp.zeros_like(dmod_ref)

        @pl.when(j == 0)
        def _():
            acc[...] = _dot_nt(df_ref[...].astype(BF16), wf_ref[...])

        acc[...] += _dot_nt(dp_ref[...], w_ref[...])

        @pl.when(j == nj - 1)
        def _():
            dx, dsh, dsc, dg = _norm_mod_bwd(acc[...], x_ref[...], g_ref[...], mod_ref[0, 1:2, :])
            dx_ref[...] = dxo_ref[...] + dx
            dmod_ref[0, 0:1, :] += dsh
            dmod_ref[0, 1:2, :] += dsc
            dg_ref[...] += dg

    tok = pl.BlockSpec((TM, D), lambda i, j: (i, 0))
    vec = pl.BlockSpec((1, D), lambda i, j: (0, 0))
    return pl.pallas_call(
        body, grid=(T // TM, nj),
        in_specs=[pl.BlockSpec((TM, tk), lambda i, j: (i, j)), pl.BlockSpec((TM, LANE), lambda i, j: (i, 0)), tok, tok,
                  pl.BlockSpec((1, 3, D), _mod_map), vec, pl.BlockSpec((D, tk), lambda i, j: (0, j)),
                  pl.BlockSpec((D, LANE), lambda i, j: (0, 0))],
        out_specs=[tok, pl.BlockSpec((1, 2, D), _mod_map), vec],
        out_shape=[SDS((T, D), F32), SDS((nb, 2, D), F32), SDS((1, D), F32)],
        scratch_shapes=[pltpu.VMEM((TM, D), F32)],
        compiler_params=_cp("arbitrary", "arbitrary"), name=name,
    )(dproj, dflog, dxo, x, mod3, g_pre, w_main, w_f)


def ada_fwd(c_all, w, b, name):
    n = w.shape[1]
    tn = n // 2

    def body(c_ref, w_ref, b_ref, o_ref):
        cv = c_ref[...]
        o_ref[...] = _dot((cv * jax.nn.sigmoid(cv)).astype(BF16), w_ref[...].astype(BF16)) + b_ref[...]

    R = c_all.shape[0]
    return pl.pallas_call(
        body, grid=(2,),
        in_specs=[pl.BlockSpec((R, D), lambda j: (0, 0)), pl.BlockSpec((D, tn), lambda j: (0, j)), pl.BlockSpec((1, tn), lambda j: (0, j))],
        out_specs=pl.BlockSpec((R, tn), lambda j: (0, j)), out_shape=SDS((R, n), F32),
        compiler_params=_cp("arbitrary"), name=name,
    )(c_all, w, b)


def ada_bwd(c_all, dmod, name):
    R, n = dmod.shape
    tn = n // 2

    def body(c_ref, d_ref, o_ref):
        cv = c_ref[...]
        o_ref[...] = _dot_tn((cv * jax.nn.sigmoid(cv)).astype(BF16), d_ref[...].astype(BF16))

    return pl.pallas_call(
        body, grid=(2,), in_specs=[pl.BlockSpec((R, D), lambda j: (0, 0)), pl.BlockSpec((R, tn), lambda j: (0, j))],
        out_specs=pl.BlockSpec((D, tn), lambda j: (0, j)), out_shape=SDS((D, n), F32),
        compiler_params=_cp("arbitrary"), name=name,
    )(c_all, dmod)


def _adam_math(w, g, m, v):
    m2 = ADAM_B1 * m + (1.0 - ADAM_B1) * g
    v2 = ADAM_B2 * v + (1.0 - ADAM_B2) * (g * g)
    m_hat = m2 / (1.0 - ADAM_B1 ** ADAM_STEP)
    v_hat = v2 / (1.0 - ADAM_B2 ** ADAM_STEP)
    delta = -ADAM_LR * (m_hat / (jnp.sqrt(v_hat) + ADAM_EPS) + ADAM_WD * w)
    return delta, m2, v2


def adam_update(w, g, m, v, tr, name):
    R, C = w.shape

    def body(w_ref, g_ref, m_ref, v_ref, d_ref, mo_ref, vo_ref):
        d_ref[...], mo_ref[...], vo_ref[...] = _adam_math(w_ref[...], g_ref[...], m_ref[...], v_ref[...])

    spec = pl.BlockSpec((tr, C), lambda i: (i, 0))
    return pl.pallas_call(
        body, grid=(R // tr,), in_specs=[spec] * 4, out_specs=[spec] * 3, out_shape=[SDS((R, C), F32)] * 3,
        compiler_params=_cp("arbitrary"), name=name,
    )(w, g, m, v)


def vec_adam(parts, w, m, v, name):
    P, C = parts.shape

    def body(p_ref, w_ref, m_ref, v_ref, g_ref, d_ref, mo_ref, vo_ref):
        g = jnp.sum(p_ref[...], axis=0, keepdims=True)
        g_ref[...] = g
        d_ref[...], mo_ref[...], vo_ref[...] = _adam_math(w_ref[...], g, m_ref[...], v_ref[...])

    return pl.pallas_call(body, out_shape=[SDS((1, C), F32)] * 4, compiler_params=_cp(), name=name)(parts, w, m, v)


HBM = pl.BlockSpec(memory_space=pltpu.HBM)
VMEM = pl.BlockSpec(memory_space=pltpu.VMEM)


def _place():
    x, y, c = lax.axis_index("x"), lax.axis_index("y"), lax.axis_index("c")
    return x, y, c, [(1 - x, y), (x, 1 - y), (1 - x, 1 - y)]


def all_gather8(xs, name):
    R, C = xs.shape

    def body(x_ref, out_ref, send_sems, recv_sems, local_sem):
        x, y, c, chips = _place()
        me, sibling = (x, y, c), (x, y, 1 - c)

        def slot(px, py, pc):
            return out_ref.at[4 * px + 2 * py + pc]

        def copy(k, block, to, src=None):
            return pltpu.make_async_remote_copy(
                src_ref=slot(*block) if src is None else src, dst_ref=slot(*block),
                send_sem=send_sems.at[k], recv_sem=recv_sems.at[k], device_id=to, device_id_type=MESH)

        mine = pltpu.make_async_copy(x_ref, slot(*me), local_sem)
        mine.start()
        first = [copy(0, me, sibling, src=x_ref)]
        first += [copy(1 + j, me, (*chip, c), src=x_ref) for j, chip in enumerate(chips)]
        for cp in first:
            cp.start()
        passed = [copy(4 + j, (*chip, c), sibling) for j, chip in enumerate(chips)]
        for j, chip in enumerate(chips):
            copy(1 + j, (*chip, c), me).wait_recv()
            passed[j].start()
        copy(0, sibling, me).wait_recv()
        for j, chip in enumerate(chips):
            copy(4 + j, (*chip, 1 - c), me).wait_recv()
        for cp in first + passed:
            cp.wait_send()
        mine.wait()

    return pl.pallas_call(
        body, out_shape=SDS((N_DEV, R, C), xs.dtype), in_specs=[VMEM], out_specs=VMEM,
        scratch_shapes=[pltpu.SemaphoreType.DMA((7,)), pltpu.SemaphoreType.DMA((7,)), pltpu.SemaphoreType.DMA],
        compiler_params=pltpu.CompilerParams(vmem_limit_bytes=VMEM_LIMIT), name=name,
    )(xs)


def all_gather_shards(ws, splits, name):
    n = len(ws)

    def body(*refs):
        w_refs, o_refs = refs[:n], refs[n:2 * n]
        send_sems, recv_sems, local_sems = refs[2 * n:]
        x, y, c, chips = _place()
        sibling = (x, y, 1 - c)
        me_s = 2 * x + y

        def half(ref, k, cc):
            lo, hi = (0, splits[k]) if cc == 0 else (splits[k], ws[k].shape[0])
            return ref.at[pl.ds(lo, hi - lo)]

        def rcopy(src, dst, k, s, to):
            return pltpu.make_async_remote_copy(src_ref=src, dst_ref=dst, send_sem=send_sems.at[k, s],
                                                recv_sem=recv_sems.at[k, s], device_id=to, device_id_type=MESH)

        for cc in (0, 1):
            @pl.when(c == cc)
            def _():
                started, local = [], []
                for k in range(n):
                    cp = pltpu.make_async_copy(w_refs[k], o_refs[k].at[me_s], local_sems.at[k])
                    cp.start()
                    local.append(cp)
                    for j, chip in enumerate(chips):
                        s = rcopy(half(w_refs[k], k, cc), half(o_refs[k].at[me_s], k, cc), k, j, (*chip, c))
                        s.start()
                        started.append(s)
                for k in range(n):
                    for j, chip in enumerate(chips):
                        land = half(o_refs[k].at[2 * chip[0] + chip[1]], k, cc)
                        rcopy(land, land, k, j, (*chip, c)).wait_recv()
                        f = rcopy(land, land, k, 3 + j, sibling)
                        f.start()
                        started.append(f)
                for k in range(n):
                    for j, chip in enumerate(chips):
                        other = half(o_refs[k].at[2 * chip[0] + chip[1]], k, 1 - cc)
                        rcopy(other, other, k, 3 + j, sibling).wait_recv()
                for s in started:
                    s.wait_send()
                for cp in local:
                    cp.wait()

    return pl.pallas_call(
        body, out_shape=[SDS((N_SHARD,) + w.shape, w.dtype) for w in ws], in_specs=[HBM] * n, out_specs=[HBM] * n,
        scratch_shapes=[pltpu.SemaphoreType.DMA((n, 6)), pltpu.SemaphoreType.DMA((n, 6)), pltpu.SemaphoreType.DMA((n,))],
        name=name,
    )(*ws)


def sibling_send_half(gs, name):
    n = len(gs)

    def body(*refs):
        g_refs, o_refs = refs[:n], refs[n:2 * n]
        send_sems, recv_sems = refs[2 * n:]
        x, y, c, _ = _place()
        cps = []
        for k in range(n):
            hr = gs[k].shape[1] // 2
            src = g_refs[k].at[:, pl.ds(pl.multiple_of((1 - c) * hr, 8), hr)]
            cp = pltpu.make_async_remote_copy(src_ref=src, dst_ref=o_refs[k], send_sem=send_sems.at[k], recv_sem=recv_sems.at[k],
                                              device_id=(x, y, 1 - c), device_id_type=MESH)
            cp.start()
            cps.append(cp)
        for cp in cps:
            cp.wait()

    return pl.pallas_call(
        body, out_shape=[SDS((N_SHARD, g.shape[1] // 2, g.shape[2]), g.dtype) for g in gs], in_specs=[HBM] * n, out_specs=[HBM] * n,
        scratch_shapes=[pltpu.SemaphoreType.DMA((n,)), pltpu.SemaphoreType.DMA((n,))], name=name,
    )(*gs)


def chip_scatter(hs, name):
    n = len(hs)

    def body(*refs):
        h_refs, o_refs = refs[:n], refs[n:2 * n]
        send_sems, recv_sems = refs[2 * n:]
        x, y, c, chips = _place()
        cps = []
        for k in range(n):
            for j, chip in enumerate(chips):
                cp = pltpu.make_async_remote_copy(
                    src_ref=h_refs[k].at[2 * chip[0] + chip[1]], dst_ref=o_refs[k].at[j], send_sem=send_sems.at[k, j],
                    recv_sem=recv_sems.at[k, j], device_id=(*chip, c), device_id_type=MESH)
                cp.start()
                cps.append(cp)
        for cp in cps:
            cp.wait()

    return pl.pallas_call(
        body, out_shape=[SDS((3,) + h.shape[1:], h.dtype) for h in hs], in_specs=[HBM] * n, out_specs=[HBM] * n,
        scratch_shapes=[pltpu.SemaphoreType.DMA((n, 3)), pltpu.SemaphoreType.DMA((n, 3))], name=name,
    )(*hs)


def sibling_share_half(ghs, name):
    n = len(ghs)

    def body(*refs):
        g_refs, o_refs = refs[:n], refs[n:2 * n]
        send_sems, recv_sems, local_sems = refs[2 * n:]
        x, y, c, _ = _place()
        cps = []
        for k in range(n):
            hr = ghs[k].shape[0]
            dst = o_refs[k].at[pl.ds(pl.multiple_of(c * hr, 8), hr)]
            lc = pltpu.make_async_copy(g_refs[k], dst, local_sems.at[k])
            lc.start()
            cp = pltpu.make_async_remote_copy(src_ref=g_refs[k], dst_ref=dst, send_sem=send_sems.at[k], recv_sem=recv_sems.at[k],
                                              device_id=(x, y, 1 - c), device_id_type=MESH)
            cp.start()
            cps += [lc, cp]
        for cp in cps:
            cp.wait()

    return pl.pallas_call(
        body, out_shape=[SDS((2 * g.shape[0], g.shape[1]), g.dtype) for g in ghs], in_specs=[HBM] * n, out_specs=[HBM] * n,
        scratch_shapes=[pltpu.SemaphoreType.DMA((n,)), pltpu.SemaphoreType.DMA((n,)), pltpu.SemaphoreType.DMA((n,))],
        name=name,
    )(*ghs)


def pair_sum(g, ra, cidx, name):
    _, r, cols = g.shape
    hr = r // 2

    def body(c_ref, g_ref, a_ref, o_ref):
        o_ref[...] = (g_ref[...] + a_ref[...]).astype(BF16)

    return pl.pallas_call(
        body,
        grid_spec=pltpu.PrefetchScalarGridSpec(
            num_scalar_prefetch=1, grid=(N_SHARD,),
            in_specs=[pl.BlockSpec((1, hr, cols), lambda s, c_ref: (s, c_ref[0], 0)),
                      pl.BlockSpec((1, hr, cols), lambda s, c_ref: (s, 0, 0))],
            out_specs=pl.BlockSpec((1, hr, cols), lambda s, c_ref: (s, 0, 0))),
        out_shape=SDS((N_SHARD, hr, cols), BF16), compiler_params=_cp("arbitrary"), name=name,
    )(cidx, g, ra)


def chip_sum(h, rb, sidx, name):
    _, hr, cols = h.shape

    def body(s_ref, h_ref, r_ref, o_ref):
        o_ref[...] = ((h_ref[0].astype(F32) + r_ref[0].astype(F32)) + r_ref[1].astype(F32)) + r_ref[2].astype(F32)

    return pl.pallas_call(
        body,
        grid_spec=pltpu.PrefetchScalarGridSpec(
            num_scalar_prefetch=1, grid=(1,),
            in_specs=[pl.BlockSpec((1, hr, cols), lambda i, s_ref: (s_ref[0], 0, 0)),
                      pl.BlockSpec((3, hr, cols), lambda i, s_ref: (0, 0, 0))],
            out_specs=pl.BlockSpec((hr, cols), lambda i, s_ref: (0, 0))),
        out_shape=SDS((hr, cols), F32), compiler_params=_cp("arbitrary"), name=name,
    )(sidx, h, rb)


def _shard_cols(g, n_valid):
    r = g.shape[0]
    return g[:, :n_valid].reshape(r, N_SHARD, n_valid // N_SHARD).transpose(1, 0, 2)


def _unshard_cols(o, pad_to):
    _, r, n = o.shape
    full = o.transpose(1, 0, 2).reshape(r, N_SHARD * n)
    return jnp.pad(full, ((0, 0), (0, pad_to - N_SHARD * n)))


def mixer_fwd(x1, mod3, g_pre, w_main, w_f, b_forget_pad, goa, gob, w_out, g_post, tabs, nb):
    hmix, pa, pb, flog = mixer_proj(x1, mod3, g_pre, w_main, w_f, *tabs, name="mixer_proj")
    out_a, lse_a = band_fwd(pa, name="band_fwd")
    F = forget_cumsum(flog.reshape(nb, SEQ, LANE), b_forget_pad, name="forget_cumsum")
    Fh = F[:, :, :NH].transpose(0, 2, 1)
    fcol = Fh.reshape(nb, NH, SEQ, 1)
    frow = Fh.reshape(nb, NH, SEQ // FB, 1, FB)
    out_b, lse_b = fox_fwd(pb, fcol, frow, name="fox_fwd")
    x2, merged, y0m = mixer_out_fwd(out_a, out_b, goa, gob, w_out, g_post, x1, mod3, name="mixer_out_fwd")
    res = dict(hmix=hmix, flog=flog, pa=pa, pb=pb, out_a=out_a, lse_a=lse_a, fcol=fcol, frow=frow, out_b=out_b, lse_b=lse_b,
               merged=merged, y0m=y0m)
    return x2, res


def mixer_bwd(dx2, x1, mod3, g_pre, w_main, w_f, b_forget_pad, goa, gob, w_out, g_post, tabs, res, nb):
    T = nb * SEQ
    dy0m, doa, dob, dmgate, dg_post, dgoa, dgob = mixer_out_bwd(
        dx2, res["y0m"], mod3, g_post, w_out, res["out_a"], res["out_b"], goa, gob, name="mixer_out_bwd")
    dqa, dka, dva = band_bwd(res["pa"], doa, res["out_a"], res["lse_a"], name="band_bwd")
    dqb, dkb, dvb, dfq, dfk = fox_bwd(res["pb"], dob, res["out_b"], res["lse_b"], res["fcol"], res["frow"], name="fox_bwd")
    dF = (dfq.reshape(nb, NH, SEQ) + dfk.reshape(nb, NH, SEQ)).transpose(0, 2, 1)
    dF = jnp.pad(dF, ((0, 0), (0, 0), (0, LANE - NH)))
    dflog, dbf = forget_cumsum_bwd(dF, res["flog"].reshape(nb, SEQ, LANE), b_forget_pad, name="forget_cumsum_bwd")
    dflog = dflog.reshape(T, LANE)
    dproj = proj_grad_assemble((dqa, dka, dva, dqb, dkb, dvb), *tabs, name="proj_grad_assemble")
    dx1, dmod2, dg_pre = mixer_proj_bwd(dproj, dflog, dx2, x1, mod3, g_pre, w_main, w_f, name="mixer_proj_bwd")
    g_main = matmul_tn(res["hmix"], dproj, D, 1024, 1024, name="grad_w_in")
    g_f = matmul_tn(res["hmix"], dflog.astype(BF16), D, LANE, 1024, name="grad_w_forget")
    g_out = matmul_tn(res["merged"], dy0m, D, D, 1024, name="grad_w_out")
    dmod3 = jnp.concatenate([dmod2, dmgate], axis=1)
    return dx1, dmod3, dict(g_pre=dg_pre, g_post=dg_post, goa=dgoa, gob=dgob, b_forget=dbf[:, :NH],
                            w_in=jnp.concatenate([g_main, g_f[:, :NH]], axis=1), w_out=g_out)


def ffn_grads(h, dy0, act, dgate, dup, pre):
    g_gate = matmul_tn(h, dgate, D, FF_TN, 1024, name=pre + "_grad_gate")
    g_up = matmul_tn(h, dup, D, FF_TN, 1024, name=pre + "_grad_up")
    g_down = matmul_tn(act, dy0, FF_TN, D, 1024, name=pre + "_grad_down")
    return g_gate, g_up, g_down


def local_step(x0, tgt, pos_col, mod, wfull, p):
    T = x0.shape[0]
    nb = T // SEQ
    mod_ff1, mod_mix, mod_ff2 = mod[:, 0:3], mod[:, 3:6], mod[:, 6:9]
    tabs = rope_tables(pos_col, name="rope_tables")
    bf_pad = jnp.pad(p["b_forget"], ((0, 0), (0, LANE - NH)))

    x1, h1, gate1, up1, y01 = ffn_fwd(x0, mod_ff1, p["g_pre_ff1"], p["g_post_ff1"], wfull["w_ff1_gate"], wfull["w_ff1_up"],
                                      wfull["w_ff1_down"], 0.5, name="ff1_fwd")
    x2, res = mixer_fwd(x1, mod_mix, p["g_pre_mix"], wfull["w_main"], wfull["w_f"], bf_pad, p["g_out_a"], p["g_out_b"],
                        wfull["w_out"], p["g_post_mix"], tabs, nb)
    x3, h2, gate2, up2, y02 = ffn_fwd(x2, mod_ff2, p["g_pre_ff2"], p["g_post_ff2"], wfull["w_ff2_gate"], wfull["w_ff2_up"],
                                      wfull["w_ff2_down"], 0.5, name="ff2_fwd")

    dx3, loss_part = loss_grad(x3, tgt, name="loss_grad")
    dx2, dy02, act2, dgate2, dup2, dmod_ff2, dgpre2, dgpost2 = ffn_bwd(
        dx3, x2, y02, mod_ff2, p["g_pre_ff2"], p["g_post_ff2"], gate2, up2, wfull["w_ff2_gate"], wfull["w_ff2_up"],
        wfull["w_ff2_down"], 0.5, name="ff2_bwd")
    gw = {}
    gw["w_ff2_gate"], gw["w_ff2_up"], gw["w_ff2_down"] = ffn_grads(h2, dy02, act2, dgate2, dup2, "ff2")
    dx1, dmod_mix, gmix = mixer_bwd(dx2, x1, mod_mix, p["g_pre_mix"], wfull["w_main"], wfull["w_f"], bf_pad, p["g_out_a"],
                                    p["g_out_b"], wfull["w_out"], p["g_post_mix"], tabs, res, nb)
    gw["w_in"], gw["w_out"] = gmix["w_in"], gmix["w_out"]
    dx0, dy01, act1, dgate1, dup1, dmod_ff1, dgpre1, dgpost1 = ffn_bwd(
        dx1, x0, y01, mod_ff1, p["g_pre_ff1"], p["g_post_ff1"], gate1, up1, wfull["w_ff1_gate"], wfull["w_ff1_up"],
        wfull["w_ff1_down"], 0.5, name="ff1_bwd")
    gw["w_ff1_gate"], gw["w_ff1_up"], gw["w_ff1_down"] = ffn_grads(h1, dy01, act1, dgate1, dup1, "ff1")
    dmod = jnp.concatenate([dmod_ff1, dmod_mix, dmod_ff2], axis=1).reshape(nb, 9 * D)
    small = dict(g_pre_ff1=dgpre1, g_post_ff1=dgpost1, g_pre_mix=gmix["g_pre"], g_post_mix=gmix["g_post"], g_pre_ff2=dgpre2,
                 g_post_ff2=dgpost2, g_out_a=gmix["goa"], g_out_b=gmix["gob"], b_forget=gmix["b_forget"])
    return loss_part, dx0, dmod, gw, small


def kernel(x, c, positions, w_ada, b_ada, g_pre_ff1, g_post_ff1, w_ff1_gate, w_ff1_up, w_ff1_down, g_pre_mix, g_post_mix, w_in, b_forget, g_out_a, g_out_b, w_out, g_pre_ff2, g_post_ff2, w_ff2_gate, w_ff2_up, w_ff2_down, loss_target, m_w_ada, m_b_ada, m_g_pre_ff1, m_g_post_ff1, m_w_ff1_gate, m_w_ff1_up, m_w_ff1_down, m_g_pre_mix, m_g_post_mix, m_w_in, m_b_forget, m_g_out_a, m_g_out_b, m_w_out, m_g_pre_ff2, m_g_post_ff2, m_w_ff2_gate, m_w_ff2_up, m_w_ff2_down, v_w_ada, v_b_ada, v_g_pre_ff1, v_g_post_ff1, v_w_ff1_gate, v_w_ff1_up, v_w_ff1_down, v_g_pre_mix, v_g_post_mix, v_w_in, v_b_forget, v_g_out_a, v_g_out_b, v_w_out, v_g_pre_ff2, v_g_post_ff2, v_w_ff2_gate, v_w_ff2_up, v_w_ff2_down):
    args = dict(locals())
    nb = x.shape[0]
    T = nb * SEQ
    ax, ay, ac = lax.axis_index("x"), lax.axis_index("y"), lax.axis_index("c")
    shard = 2 * ax + ay
    cidx = jnp.reshape(ac, (1,)).astype(jnp.int32)
    sidx = jnp.reshape(shard, (1,)).astype(jnp.int32)

    big = ["w_ff1_gate", "w_ff1_up", "w_ff1_down", "w_in", "w_out", "w_ff2_gate", "w_ff2_up", "w_ff2_down"]
    vecs = ["g_pre_ff1", "g_post_ff1", "g_pre_mix", "g_post_mix", "g_pre_ff2", "g_post_ff2"]

    shards_bf = [args[n][0].astype(BF16) for n in big]
    splits = [512, 512, 352, 512, 128, 512, 512, 352]
    gathered = all_gather_shards(shards_bf, splits, name="all_gather_weights")
    wfull = {}
    for n, o in zip(big, gathered):
        if n.endswith("gate") or n.endswith("up"):
            wfull[n] = _unshard_cols(o, DFF_PAD)
        elif n.endswith("down"):
            wfull[n] = jnp.pad(o.reshape(DFF, D), ((0, DFF_PAD - DFF), (0, 0)))
        elif n == "w_in":
            full = _unshard_cols(o, IN_COLS)
            wfull["w_main"] = full[:, :IN_MAIN]
            wfull["w_f"] = jnp.pad(full[:, IN_MAIN:], ((0, 0), (0, LANE - NH)))
        else:
            wfull[n] = o.reshape(D, D)

    ncol = w_ada.shape[2]
    c_all = all_gather8(c, name="all_gather_c").reshape(N_DEV * nb, D)
    b_loc = lax.dynamic_slice(b_ada, (0, shard * ncol), (1, ncol))
    mod_loc = ada_fwd(c_all, w_ada[0], b_loc, name="ada_fwd")
    mod_g = all_gather8(mod_loc, name="all_gather_mod")
    row0 = (4 * ax + 2 * ay + ac) * nb
    mod_rows = lax.dynamic_slice(mod_g, (0, row0, 0), (N_DEV, nb, ncol))
    mod = jnp.concatenate([mod_rows[2 * s] for s in range(N_SHARD)], axis=-1).reshape(nb, 9, D)

    small_in = dict(g_pre_ff1=g_pre_ff1, g_post_ff1=g_post_ff1, g_pre_mix=g_pre_mix, g_post_mix=g_post_mix, g_pre_ff2=g_pre_ff2,
                    g_post_ff2=g_post_ff2, g_out_a=g_out_a, g_out_b=g_out_b, b_forget=b_forget)
    loss_part, dx0, dmod, gw, small = local_step(x.reshape(T, D), loss_target.reshape(T, D), positions.reshape(T, 1), mod, wfull,
                                                 small_in)

    dmod_all = all_gather8(dmod, name="all_gather_dmod").reshape(N_DEV * nb, 9 * D)
    dmod_loc = lax.dynamic_slice(dmod_all, (0, shard * ncol), (N_DEV * nb, ncol))
    g_w_ada = ada_bwd(c_all, dmod_loc, name="ada_bwd")

    def shard_blocked(n, g):
        if n.endswith("gate") or n.endswith("up"):
            return _shard_cols(g, DFF)
        if n.endswith("down"):
            return g[:DFF].reshape(N_SHARD, DFF // N_SHARD, D)
        if n == "w_in":
            return _shard_cols(g, IN_COLS)
        return g.reshape(N_SHARD, D // N_SHARD, D)

    gsb = [shard_blocked(n, gw[n]) for n in big]
    ras = sibling_send_half(gsb, name="grad_sibling_send")
    hs = [pair_sum(g, ra, cidx, name=f"grad_pair_sum_{n}") for n, g, ra in zip(big, gsb, ras)]
    rbs = chip_scatter(hs, name="grad_chip_scatter")
    ghs = [chip_sum(h, rb, sidx, name=f"grad_chip_sum_{n}") for n, h, rb in zip(big, hs, rbs)]
    gfull = dict(zip(big, sibling_share_half(ghs, name="grad_sibling_share")))
    gfull["w_ada"] = g_w_ada

    row6 = jnp.concatenate([small["g_out_a"], small["g_out_b"]], axis=1)
    row7 = jnp.concatenate([small["b_forget"], loss_part[0:1, 0:1], jnp.zeros((1, D - NH - 1), F32)], axis=1)
    pack = jnp.concatenate([small[n] for n in vecs] + [row6, row7], axis=0)
    packed = all_gather8(pack, name="all_gather_small").reshape(N_DEV, 8 * D)

    def pack_state(pre):
        r6 = jnp.concatenate([args[pre + "g_out_a"], args[pre + "g_out_b"]], axis=1)
        r7 = jnp.pad(args[pre + "b_forget"], ((0, 0), (0, D - NH)))
        return jnp.concatenate([args[pre + n] for n in vecs] + [r6, r7], axis=0).reshape(1, 8 * D)

    sg, sd, sm, sv = (t.reshape(8, D) for t in vec_adam(packed, pack_state(""), pack_state("m_"), pack_state("v_"), name="adam_small"))

    def unpack(t):
        out = {n: t[i:i + 1] for i, n in enumerate(vecs)}
        out["g_out_a"], out["g_out_b"], out["b_forget"] = t[6:7, :WG], t[6:7, WG:], t[7:8, :NH]
        return out

    outs = dict(grad=unpack(sg), delta=unpack(sd), new_m=unpack(sm), new_v=unpack(sv))
    loss = sg[7, NH]
    outs["grad"]["b_ada"], outs["delta"]["b_ada"], outs["new_m"]["b_ada"], outs["new_v"]["b_ada"] = vec_adam(
        dmod_all, b_ada, m_b_ada, v_b_ada, name="adam_b_ada")

    for n in big + ["w_ada"]:
        g = gfull[n]
        rows = g.shape[0]
        tr = 128 if rows % 128 == 0 else 344
        d, m2, v2 = adam_update(args[n][0], g, args["m_" + n][0], args["v_" + n][0], tr, name="adam_" + n)
        outs["grad"][n], outs["delta"][n], outs["new_m"][n], outs["new_v"][n] = g[None], d[None], m2[None], v2[None]

    order = ["w_ada", "b_ada", "g_pre_ff1", "g_post_ff1", "w_ff1_gate", "w_ff1_up", "w_ff1_down", "g_pre_mix", "g_post_mix", "w_in",
             "b_forget", "g_out_a", "g_out_b", "w_out", "g_pre_ff2", "g_post_ff2", "w_ff2_gate", "w_ff2_up", "w_ff2_down"]
    result = [loss, dx0.reshape(nb, SEQ, D)]
    for kind in ("grad", "delta", "new_m", "new_v"):
        result += [outs[kind][n] for n in order]
    return tuple(result)
```

```python
import functools
import math

import jax
import jax.numpy as jnp
from jax import lax
from jax.experimental import pallas as pl
from jax.experimental.pallas import tpu as pltpu

D = 1024
SEQ = 2048
HD = 64
NH = 8
WG = NH * HD
DFF = 2752
DFF_PAD = 2816
IN_MAIN = 6 * WG
IN_COLS = IN_MAIN + NH
N_SHARD = 4
N_DEV = 8
LANE = 128
QB = 128
FB = 256
FT = 512
BAND_UNROLL = 4
BAND_UNROLL_BWD = 4
PATTERNS = ((1, 16), (4, 4), (16, 1))
ROPE_THETA = 500000.0
EPS = 1e-6
NEG = -1e30
ATTN_SCALE = HD ** -0.5
TM = 512
TM_BWD = 256
VMEM_LIMIT = 56 * 1024 * 1024

ADAM_LR, ADAM_B1, ADAM_B2, ADAM_EPS, ADAM_WD, ADAM_STEP = 0.001, 0.9, 0.999, 1e-08, 0.01, 10

F32 = jnp.float32
BF16 = jnp.bfloat16
MESH = pl.DeviceIdType.MESH
SDS = jax.ShapeDtypeStruct


def _cp(*sem):
    return pltpu.CompilerParams(dimension_semantics=sem, vmem_limit_bytes=VMEM_LIMIT)


def _dot(a, b):
    return jnp.dot(a, b, preferred_element_type=F32)


def _dot_nt(a, b):
    return lax.dot_general(a, b, (((1,), (1,)), ((), ())), preferred_element_type=F32)


def _dot_tn(a, b):
    return lax.dot_general(a, b, (((0,), (0,)), ((), ())), preferred_element_type=F32)


def _rms(xf):
    return lax.rsqrt(jnp.mean(xf * xf, axis=-1, keepdims=True) + EPS)


def _norm_mod_bwd(dh, xf, g, scale):
    r = _rms(xf)
    xh = xf * r
    dsh = jnp.sum(dh, axis=0, keepdims=True)
    dsc = jnp.sum(dh * (xh * g), axis=0, keepdims=True)
    dn = dh * (1.0 + scale)
    dg = jnp.sum(dn * xh, axis=0, keepdims=True)
    dxh = dn * g
    dx = r * (dxh - xh * jnp.mean(dxh * xh, axis=-1, keepdims=True))
    return dx, dsh, dsc, dg


def _post_bwd(dxo, y0, g, mgate, gs):
    r = _rms(y0)
    yh = y0 * r
    dmg = gs * jnp.sum(dxo * (yh * g), axis=0, keepdims=True)
    dy = (gs * mgate) * dxo
    dg = jnp.sum(dy * yh, axis=0, keepdims=True)
    dyh = dy * g
    dy0 = r * (dyh - yh * jnp.mean(dyh * yh, axis=-1, keepdims=True))
    return dy0, dmg, dg


def _mod_map(i, *_):
    return ((i * TM) // SEQ, 0, 0)


FF_TN = 1408
FF_NJ = DFF_PAD // FF_TN


def ffn_fwd(x, mod3, g_pre, g_post, wg, wu, wd, gs, name):
    T = x.shape[0]

    def body(x_ref, mod_ref, gpre_ref, gpost_ref, wg_ref, wu_ref, wd_ref, xo_ref, h_ref, gate_ref, up_ref, y0_ref, hs, acc):
        j = pl.program_id(1)

        @pl.when(j == 0)
        def _():
            xf = x_ref[...]
            h = (xf * _rms(xf) * gpre_ref[...]) * (1.0 + mod_ref[0, 1:2, :]) + mod_ref[0, 0:1, :]
            hb = h.astype(BF16)
            hs[...] = hb
            h_ref[...] = hb
            acc[...] = jnp.zeros_like(acc)

        hb = hs[...]
        gate = _dot(hb, wg_ref[...])
        up = _dot(hb, wu_ref[...])
        gate_ref[...] = gate.astype(BF16)
        up_ref[...] = up.astype(BF16)
        act = gate * jax.nn.sigmoid(gate) * up
        acc[...] += _dot(act.astype(BF16), wd_ref[...])

        @pl.when(j == FF_NJ - 1)
        def _():
            y0 = acc[...]
            y0_ref[...] = y0
            xo_ref[...] = x_ref[...] + (gs * mod_ref[0, 2:3, :]) * (y0 * _rms(y0) * gpost_ref[...])

    tok = pl.BlockSpec((TM, D), lambda i, j: (i, 0))
    vec = pl.BlockSpec((1, D), lambda i, j: (0, 0))
    hid = pl.BlockSpec((TM, FF_TN), lambda i, j: (i, j))
    return pl.pallas_call(
        body, grid=(T // TM, FF_NJ),
        in_specs=[tok, pl.BlockSpec((1, 3, D), _mod_map), vec, vec,
                  pl.BlockSpec((D, FF_TN), lambda i, j: (0, j)), pl.BlockSpec((D, FF_TN), lambda i, j: (0, j)),
                  pl.BlockSpec((FF_TN, D), lambda i, j: (j, 0))],
        out_specs=[tok, tok, hid, hid, tok],
        out_shape=[SDS((T, D), F32), SDS((T, D), BF16), SDS((T, DFF_PAD), BF16), SDS((T, DFF_PAD), BF16), SDS((T, D), F32)],
        scratch_shapes=[pltpu.VMEM((TM, D), BF16), pltpu.VMEM((TM, D), F32)],
        compiler_params=_cp("arbitrary", "arbitrary"), name=name,
    )(x, mod3, g_pre, g_post, wg, wu, wd)


def ffn_bwd(dxo, x, y0, mod3, g_pre, g_post, gate, up, wg, wu, wd, gs, name):
    T = x.shape[0]
    nb = T // SEQ
    tm = TM_BWD
    tiles_per_seq = SEQ // tm

    def body(dxo_ref, x_ref, y0_ref, mod_ref, gpre_ref, gpost_ref, gate_ref, up_ref, wg_ref, wu_ref, wd_ref,
             dx_ref, dy0_ref, act_ref, dgate_ref, dup_ref, dmod_ref, dgpre_ref, dgpost_ref, dy0s, acc):
        i = pl.program_id(0)
        j = pl.program_id(1)

        @pl.when((i == 0) & (j == 0))
        def _():
            dgpre_ref[...] = jnp.zeros_like(dgpre_ref)
            dgpost_ref[...] = jnp.zeros_like(dgpost_ref)

        @pl.when((i % tiles_per_seq == 0) & (j == 0))
        def _():
            dmod_ref[...] = jnp.zeros_like(dmod_ref)

        @pl.when(j == 0)
        def _():
            dy0, dmg, dg = _post_bwd(dxo_ref[...], y0_ref[...], gpost_ref[...], mod_ref[0, 2:3, :], gs)
            dmod_ref[0, 2:3, :] += dmg
            dgpost_ref[...] += dg
            db = dy0.astype(BF16)
            dy0s[...] = db
            dy0_ref[...] = db
            acc[...] = jnp.zeros_like(acc)

        dact = _dot_nt(dy0s[...], wd_ref[...])
        g = gate_ref[...].astype(F32)
        u = up_ref[...].astype(F32)
        sig = jax.nn.sigmoid(g)
        sl = g * sig
        dgate = (dact * u * (sig * (1.0 + g * (1.0 - sig)))).astype(BF16)
        dup = (dact * sl).astype(BF16)
        act_ref[...] = (sl * u).astype(BF16)
        dgate_ref[...] = dgate
        dup_ref[...] = dup
        acc[...] += _dot_nt(dgate, wg_ref[...]) + _dot_nt(dup, wu_ref[...])

        @pl.when(j == FF_NJ - 1)
        def _():
            dx, dsh, dsc, dg = _norm_mod_bwd(acc[...], x_ref[...], gpre_ref[...], mod_ref[0, 1:2, :])
            dx_ref[...] = dxo_ref[...] + dx
            dmod_ref[0, 0:1, :] += dsh
            dmod_ref[0, 1:2, :] += dsc
            dgpre_ref[...] += dg

    tok = pl.BlockSpec((tm, D), lambda i, j: (i, 0))
    vec = pl.BlockSpec((1, D), lambda i, j: (0, 0))
    hid = pl.BlockSpec((tm, FF_TN), lambda i, j: (i, j))
    modspec = pl.BlockSpec((1, 3, D), lambda i, j: ((i * tm) // SEQ, 0, 0))
    return pl.pallas_call(
        body, grid=(T // tm, FF_NJ),
        in_specs=[tok, tok, tok, modspec, vec, vec, hid, hid,
                  pl.BlockSpec((D, FF_TN), lambda i, j: (0, j)), pl.BlockSpec((D, FF_TN), lambda i, j: (0, j)),
                  pl.BlockSpec((FF_TN, D), lambda i, j: (j, 0))],
        out_specs=[tok, tok, hid, hid, hid, modspec, vec, vec],
        out_shape=[SDS((T, D), F32), SDS((T, D), BF16), SDS((T, DFF_PAD), BF16), SDS((T, DFF_PAD), BF16),
                   SDS((T, DFF_PAD), BF16), SDS((nb, 3, D), F32), SDS((1, D), F32), SDS((1, D), F32)],
        scratch_shapes=[pltpu.VMEM((tm, D), BF16), pltpu.VMEM((tm, D), F32)],
        compiler_params=_cp("arbitrary", "arbitrary"), name=name,
    )(dxo, x, y0, mod3, g_pre, g_post, gate, up, wg, wu, wd)


def matmul_tn(a, b, tm, tn, tk, name):
    T, M = a.shape
    N = b.shape[1]
    nk = T // tk

    def body(a_ref, b_ref, o_ref):
        @pl.when(pl.program_id(2) == 0)
        def _():
            o_ref[...] = jnp.zeros_like(o_ref)

        o_ref[...] += _dot_tn(a_ref[...], b_ref[...])

    return pl.pallas_call(
        body, grid=(M // tm, N // tn, nk),
        in_specs=[pl.BlockSpec((tk, tm), lambda i, j, k: (k, i)), pl.BlockSpec((tk, tn), lambda i, j, k: (k, j))],
        out_specs=pl.BlockSpec((tm, tn), lambda i, j, k: (i, j)),
        out_shape=SDS((M, N), F32),
        compiler_params=_cp("arbitrary", "arbitrary", "arbitrary"), name=name,
    )(a, b)


def loss_grad(y, tgt, name):
    T = y.shape[0]

    def body(y_ref, t_ref, dy_ref, l_ref):
        @pl.when(pl.program_id(0) == 0)
        def _():
            l_ref[...] = jnp.zeros_like(l_ref)

        e = y_ref[...] - t_ref[...]
        dy_ref[...] = e * (1.0 / D)
        l_ref[...] += jnp.sum(e * e) * (0.5 / D)

    tok = pl.BlockSpec((TM, D), lambda i: (i, 0))
    return pl.pallas_call(
        body, grid=(T // TM,), in_specs=[tok, tok],
        out_specs=[tok, pl.BlockSpec((8, LANE), lambda i: (0, 0))],
        out_shape=[SDS((T, D), F32), SDS((8, LANE), F32)],
        compiler_params=_cp("arbitrary"), name=name,
    )(y, tgt)


def rope_tables(pos_col, name):
    T = pos_col.shape[0]
    tm = 1024

    def body(p_ref, c_ref, s1_ref, s2_ref):
        lane = lax.broadcasted_iota(jnp.int32, (1, LANE), 1)
        l64 = lane % HD
        inv_freq = jnp.exp((l64 % 8).astype(F32) * (-math.log(ROPE_THETA) / 8.0))
        ang = p_ref[...].astype(F32) * inv_freq
        cs = jnp.cos(ang)
        sn = jnp.sin(ang)
        c_ref[...] = jnp.where(l64 < 16, cs, 1.0)
        s1_ref[...] = jnp.where(l64 < 8, -sn, 0.0)
        s2_ref[...] = jnp.where((l64 >= 8) & (l64 < 16), sn, 0.0)

    tab = pl.BlockSpec((tm, LANE), lambda i: (i, 0))
    return pl.pallas_call(
        body, grid=(T // tm,), in_specs=[pl.BlockSpec((tm, 1), lambda i: (i, 0))], out_specs=[tab, tab, tab],
        out_shape=[SDS((T, LANE), F32)] * 3, compiler_params=_cp("arbitrary"), name=name,
    )(pos_col)


def mixer_proj(x, mod3, g_pre, w_main, w_f, rc, rs1, rs2, name):
    T = x.shape[0]
    tn = 1024

    def body(x_ref, mod_ref, g_ref, w_ref, wf_ref, c_ref, s1_ref, s2_ref, h_ref, pa_ref, pb_ref, f_ref, hs):
        j = pl.program_id(1)

        @pl.when(j == 0)
        def _():
            xf = x_ref[...]
            h = (xf * _rms(xf) * g_ref[...]) * (1.0 + mod_ref[0, 1:2, :]) + mod_ref[0, 0:1, :]
            hb = h.astype(BF16)
            hs[...] = hb
            h_ref[...] = hb
            f_ref[...] = _dot(hb, wf_ref[...])

        pr = _dot(hs[...], w_ref[...])

        @pl.when(j == 0)
        def _():
            c, s1, s2 = c_ref[...], s1_ref[...], s2_ref[...]
            for k in range(tn // LANE):
                t = pr[:, k * LANE:(k + 1) * LANE]
                pa_ref[:, k * LANE:(k + 1) * LANE] = t * c + pltpu.roll(t, LANE - 8, 1) * s1 + pltpu.roll(t, 8, 1) * s2

        @pl.when(j == 1)
        def _():
            pa_ref[:, 2 * WG:3 * WG] = pr[:, :WG]
            pb_ref[:, 0:WG] = pr[:, WG:].astype(BF16)

        @pl.when(j == 2)
        def _():
            pb_ref[:, WG:3 * WG] = pr.astype(BF16)

    tok = pl.BlockSpec((TM, D), lambda i, j: (i, 0))
    vec = pl.BlockSpec((1, D), lambda i, j: (0, 0))
    tab = pl.BlockSpec((TM, LANE), lambda i, j: (i, 0))
    grp = pl.BlockSpec((TM, 3 * WG), lambda i, j: (i, 0))
    return pl.pallas_call(
        body, grid=(T // TM, IN_MAIN // tn),
        in_specs=[tok, pl.BlockSpec((1, 3, D), _mod_map), vec, pl.BlockSpec((D, tn), lambda i, j: (0, j)),
                  pl.BlockSpec((D, LANE), lambda i, j: (0, 0)), tab, tab, tab],
        out_specs=[tok, grp, grp, tab],
        out_shape=[SDS((T, D), BF16), SDS((T, 3 * WG), F32), SDS((T, 3 * WG), BF16), SDS((T, LANE), F32)],
        scratch_shapes=[pltpu.VMEM((TM, D), BF16)],
        compiler_params=_cp("arbitrary", "arbitrary"), name=name,
    )(x, mod3, g_pre, w_main, w_f, rc, rs1, rs2)


def _head_lanes():
    return lax.broadcasted_iota(jnp.int32, (1, LANE), 1) < HD


def _pair(m0, a, b):
    return jnp.where(m0, a, b)


def _band_rows(i, d, nbc):
    if nbc == 1:
        return i, i, 0
    r, mb = i // nbc, i % nbc
    return r + mb * (QB * d), r + jnp.maximum(mb - 1, 0) * (QB * d), jnp.where(mb > 0, QB, 0)


def _rows(start, size, d):
    return pl.ds(pl.multiple_of(start, QB), size) if d == 1 else pl.ds(start, size, stride=d)


def _band_valid(span, off):
    rq = lax.broadcasted_iota(jnp.int32, (QB, span), 0)
    rel = lax.broadcasted_iota(jnp.int32, (QB, span), 1) - off
    return (rel <= rq) & (rel >= rq - QB)


def band_fwd(pa, name):
    T = pa.shape[0]
    B = T // SEQ
    NP = WG // LANE

    def body(q_ref, k_ref, v_ref, out_ref, lse_ref, o_s, l_s):
        m0 = _head_lanes()
        for pidx, (d, nbc) in enumerate(PATTERNS):
            span = QB if nbc == 1 else 2 * QB

            def blk(it, carry, pidx=pidx, d=d, nbc=nbc, span=span):
                ld = []
                for u in range(BAND_UNROLL):
                    qs, ks, off = _band_rows(it * BAND_UNROLL + u, d, nbc)
                    q = q_ref[_rows(qs, QB, d), :] * ATTN_SCALE
                    ld.append((qs, q, k_ref[_rows(ks, span, d), :].astype(BF16), v_ref[_rows(ks, span, d), :].astype(BF16),
                               _band_valid(span, off)))
                ss = [[jnp.where(valid, _dot_nt(jnp.where(mh, q, 0.0).astype(BF16), k), NEG) for mh in (m0, jnp.logical_not(m0))]
                      for _, q, k, _, valid in ld]
                ps = []
                for pair in ss:
                    row = []
                    for s in pair:
                        m = jnp.max(s, axis=-1, keepdims=True)
                        p = jnp.exp(s - m)
                        row.append((p.astype(BF16), jnp.sum(p, axis=-1, keepdims=True), m))
                    ps.append(row)
                pv = [[_dot(p, ld[u][3]) for p, _, _ in ps[u]] for u in range(BAND_UNROLL)]
                for u in range(BAND_UNROLL):
                    rows = _rows(ld[u][0], QB, d)
                    (_, l0, mx0), (_, l1, mx1) = ps[u]
                    o_s[pidx, rows, :] = _pair(m0, pv[u][0] / l0, pv[u][1] / l1)
                    l_s[pidx, rows, :] = _pair(m0, mx0 + jnp.log(l0), mx1 + jnp.log(l1))
                return carry

            lax.fori_loop(0, SEQ // QB // BAND_UNROLL, blk, 0)
        for c in range(SEQ // FB):
            sl = slice(c * FB, (c + 1) * FB)
            a, b, e = l_s[0, sl, :], l_s[1, sl, :], l_s[2, sl, :]
            m = jnp.maximum(jnp.maximum(a, b), e)
            L = m + jnp.log(jnp.exp(a - m) + jnp.exp(b - m) + jnp.exp(e - m))
            out_ref[sl, :] = jnp.exp(a - L) * o_s[0, sl, :] + jnp.exp(b - L) * o_s[1, sl, :] + jnp.exp(e - L) * o_s[2, sl, :]
            lse_ref[sl, :] = L

    blk_of = lambda g: pl.BlockSpec((SEQ, LANE), lambda b, hp, g=g: (b, g * NP + hp))
    return pl.pallas_call(
        body, grid=(B, NP), in_specs=[blk_of(0), blk_of(1), blk_of(2)], out_specs=[blk_of(0), blk_of(0)],
        out_shape=[SDS((T, WG), F32), SDS((T, WG), F32)],
        scratch_shapes=[pltpu.VMEM((3, SEQ, LANE), F32), pltpu.VMEM((3, SEQ, LANE), F32)],
        compiler_params=_cp("arbitrary", "arbitrary"), name=name,
    )(pa, pa, pa)


def _pair_rowsum(m0, prod):
    s0 = jnp.sum(jnp.where(m0, prod, 0.0), axis=-1, keepdims=True)
    return _pair(m0, s0, jnp.sum(prod, axis=-1, keepdims=True) - s0)


def band_bwd(pa, do, out, lse, name):
    T = pa.shape[0]
    B = T // SEQ
    NP = WG // LANE

    def body(q_ref, k_ref, v_ref, do_ref, out_ref, l_ref, dq_ref, dk_ref, dv_ref, d_s):
        m0 = _head_lanes()
        dq_ref[...] = jnp.zeros_like(dq_ref)
        dk_ref[...] = jnp.zeros_like(dk_ref)
        dv_ref[...] = jnp.zeros_like(dv_ref)
        for c in range(SEQ // FB):
            sl = slice(c * FB, (c + 1) * FB)
            d_s[sl, :] = _pair_rowsum(m0, do_ref[sl, :] * out_ref[sl, :])
        for d, nbc in PATTERNS:
            span = QB if nbc == 1 else 2 * QB

            def blk(it, carry, d=d, nbc=nbc, span=span):
                masks = (m0, jnp.logical_not(m0))
                ld = []
                for u in range(BAND_UNROLL_BWD):
                    qs, ks, off = _band_rows(it * BAND_UNROLL_BWD + u, d, nbc)
                    qrow, krow = _rows(qs, QB, d), _rows(ks, span, d)
                    ld.append(dict(qrow=qrow, krow=krow, q=q_ref[qrow, :] * ATTN_SCALE, k=k_ref[krow, :].astype(BF16),
                                   v=v_ref[krow, :].astype(BF16), do=do_ref[qrow, :], l=l_ref[qrow, :], dv=d_s[qrow, :],
                                   valid=_band_valid(span, off)))
                for t in ld:
                    t["qm"] = [jnp.where(mh, t["q"], 0.0).astype(BF16) for mh in masks]
                    t["dom"] = [jnp.where(mh, t["do"], 0.0).astype(BF16) for mh in masks]
                sd = [[(jnp.where(t["valid"], _dot_nt(t["qm"][h], t["k"]), NEG), _dot_nt(t["dom"][h], t["v"])) for h in range(2)]
                      for t in ld]
                pd = []
                for t, pair in zip(ld, sd):
                    row = []
                    for h, (s, dp) in enumerate(pair):
                        col = slice(h * HD, h * HD + 1)
                        p = jnp.exp(s - t["l"][:, col])
                        row.append((p.astype(BF16), (p * (dp - t["dv"][:, col])).astype(BF16)))
                    pd.append(row)
                gr = [(_dot(row[0][1], t["k"]), _dot(row[1][1], t["k"]),
                       _dot_tn(jnp.concatenate([row[0][1], row[1][1]], axis=0), jnp.concatenate(t["qm"], axis=0)),
                       _dot_tn(jnp.concatenate([row[0][0], row[1][0]], axis=0), jnp.concatenate(t["dom"], axis=0)))
                      for t, row in zip(ld, pd)]
                for t, (dq0, dq1, dk, dv) in zip(ld, gr):
                    dq_ref[t["qrow"], :] += _pair(m0, dq0, dq1) * ATTN_SCALE
                    dk_ref[t["krow"], :] += dk
                    dv_ref[t["krow"], :] += dv
                return carry

            lax.fori_loop(0, SEQ // QB // BAND_UNROLL_BWD, blk, 0)

    blk_of = lambda g: pl.BlockSpec((SEQ, LANE), lambda b, hp, g=g: (b, g * NP + hp))
    return pl.pallas_call(
        body, grid=(B, NP), in_specs=[blk_of(0), blk_of(1), blk_of(2), blk_of(0), blk_of(0), blk_of(0)],
        out_specs=[blk_of(0)] * 3, out_shape=[SDS((T, WG), F32)] * 3,
        scratch_shapes=[pltpu.VMEM((SEQ, LANE), F32)],
        compiler_params=_cp("arbitrary", "arbitrary"), name=name,
    )(pa, pa, pa, do, out, lse)


def _tile_causal(nq, nk, q0, k0):
    r = lax.broadcasted_iota(jnp.int32, (nq, nk), 0)
    c = lax.broadcasted_iota(jnp.int32, (nq, nk), 1)
    return r + (q0 - k0) >= c


def fox_fwd(pb, fcol, frow, name):
    T = pb.shape[0]
    B = T // SEQ
    NP = WG // LANE
    n = SEQ // FB

    def body(q_ref, k_ref, v_ref, fc_ref, fr_ref, o_ref, lse_ref):
        i = pl.program_id(2)
        m0 = _head_lanes()
        q = q_ref[...] * ATTN_SCALE
        zero = jnp.zeros_like(q)
        qh = (jnp.where(m0, q, zero), jnp.where(m0, zero, q))
        fq = (fc_ref[0, 0], fc_ref[0, 1])

        def step(t, carry, masked):
            rows = pl.ds(pl.multiple_of(t * FT, FT), FT)
            kt = k_ref[rows, :]
            vt = v_ref[rows, :]
            ss = [_dot_nt(qh[h], kt) + fq[h] - fr_ref[0, h, t] for h in range(2)]
            if masked:
                ok = _tile_causal(FB, FT, i * FB, t * FT)
                ss = [jnp.where(ok, s, NEG) for s in ss]
            st = []
            for h in range(2):
                m, l, _ = carry[h]
                m2 = jnp.maximum(m, jnp.max(ss[h], axis=-1, keepdims=True))
                a = jnp.exp(m - m2)
                p = jnp.exp(ss[h] - m2)
                st.append((m2, a, a * l + jnp.sum(p, axis=-1, keepdims=True), p.astype(BF16)))
            pv = [_dot(st[h][3], vt) for h in range(2)]
            return tuple((st[h][0], st[h][2], st[h][1] * carry[h][2] + pv[h]) for h in range(2))

        one = (jnp.full((FB, 1), NEG, F32), jnp.zeros((FB, 1), F32), jnp.zeros((FB, LANE), F32))
        last = (i * FB) // FT
        carry = lax.fori_loop(0, last, lambda t, cr: step(t, cr, False), (one, one))
        (ma, la, acca), (mb, lb, accb) = step(last, carry, True)
        o_ref[...] = _pair(m0, acca / la, accb / lb)
        lse_ref[...] = _pair(m0, ma + jnp.log(la), mb + jnp.log(lb))

    qblk = pl.BlockSpec((FB, LANE), lambda b, hp, i: (b * n + i, hp))
    full = lambda g: pl.BlockSpec((SEQ, LANE), lambda b, hp, i, g=g: (b, g * NP + hp))
    return pl.pallas_call(
        body, grid=(B, NP, n),
        in_specs=[qblk, full(1), full(2), pl.BlockSpec((1, 2, FB, 1), lambda b, hp, i: (b, hp, i, 0)),
                  pl.BlockSpec((1, 2, SEQ // FT, 1, FT), lambda b, hp, i: (b, hp, 0, 0, 0))],
        out_specs=[qblk, qblk], out_shape=[SDS((T, WG), F32), SDS((T, WG), F32)],
        compiler_params=_cp("arbitrary", "arbitrary", "arbitrary"), name=name,
    )(pb, pb, pb, fcol, frow)


def fox_bwd(pb, do, out, lse, fcol, frow, name):
    T = pb.shape[0]
    B = T // SEQ
    NP = WG // LANE
    n = SEQ // FB

    def body(q_ref, k_ref, v_ref, do_ref, out_ref, l_ref, fc_ref, fr_ref, dq_ref, dk_ref, dv_ref, dfq_ref, dfk_ref, d_s):
        j = pl.program_id(2)
        m0 = _head_lanes()
        masks = (m0, jnp.logical_not(m0))

        @pl.when(j == 0)
        def _():
            dq_ref[...] = jnp.zeros_like(dq_ref)
            dfq_ref[...] = jnp.zeros_like(dfq_ref)
            for c in range(n):
                sl = slice(c * FB, (c + 1) * FB)
                d_s[sl, :] = _pair_rowsum(m0, do_ref[sl, :] * out_ref[sl, :])

        kj = k_ref[...]
        vj = v_ref[...]
        fk = (fr_ref[0, 0, 0], fr_ref[0, 1, 0])

        def step(t, carry, masked):
            rows = pl.ds(pl.multiple_of(t * FT, FT), FT)
            qt = q_ref[rows, :] * ATTN_SCALE
            dot_ = do_ref[rows, :]
            lt = l_ref[rows, :]
            dt = d_s[rows, :]
            zero = jnp.zeros_like(qt)
            qm = [jnp.where(mh, qt, zero) for mh in masks]
            dom = [jnp.where(mh, dot_, 0.0).astype(BF16) for mh in masks]
            ss = [_dot_nt(qm[h], kj) + fc_ref[0, h, rows, :] - fk[h] for h in range(2)]
            dps = [_dot_nt(dom[h], vj) for h in range(2)]
            if masked:
                ok = _tile_causal(FT, FB, t * FT, j * FB)
                ss = [jnp.where(ok, s, NEG) for s in ss]
            pds = []
            for h in range(2):
                col = slice(h * HD, h * HD + 1)
                p = jnp.exp(ss[h] - lt[:, col])
                ds = p * (dps[h] - dt[:, col])
                dfq_ref[0, h, rows, :] += jnp.sum(ds, axis=-1, keepdims=True)
                pds.append((p.astype(BF16), ds.astype(BF16), jnp.sum(ds, axis=0, keepdims=True)))
            dqs = [_dot(pds[h][1], kj) for h in range(2)]
            dks = [_dot_tn(pds[h][1], qm[h]) for h in range(2)]
            dvs = [_dot_tn(pds[h][0], dom[h]) for h in range(2)]
            dq_ref[rows, :] += _pair(m0, dqs[0], dqs[1]) * ATTN_SCALE
            return tuple((carry[h][0] + dks[h], carry[h][1] + dvs[h], carry[h][2] - pds[h][2]) for h in range(2))

        one = (jnp.zeros((FB, LANE), F32), jnp.zeros((FB, LANE), F32), jnp.zeros((1, FB), F32))
        first = (j * FB) // FT
        carry = step(first, (one, one), True)
        (dka, dva, dfka), (dkb, dvb, dfkb) = lax.fori_loop(first + 1, SEQ // FT, lambda t, cr: step(t, cr, False), carry)
        dk_ref[...] = _pair(m0, dka, dkb)
        dv_ref[...] = _pair(m0, dva, dvb)
        dfk_ref[0, 0, 0] = dfka
        dfk_ref[0, 1, 0] = dfkb

    kblk = lambda g: pl.BlockSpec((FB, LANE), lambda b, hp, j, g=g: (b * n + j, g * NP + hp))
    full = pl.BlockSpec((SEQ, LANE), lambda b, hp, j: (b, hp))
    colf = pl.BlockSpec((1, 2, SEQ, 1), lambda b, hp, j: (b, hp, 0, 0))
    rowb = pl.BlockSpec((1, 2, 1, 1, FB), lambda b, hp, j: (b, hp, j, 0, 0))
    return pl.pallas_call(
        body, grid=(B, NP, n), in_specs=[full, kblk(1), kblk(2), full, full, full, colf, rowb],
        out_specs=[full, kblk(0), kblk(0), colf, rowb],
        out_shape=[SDS((T, WG), F32), SDS((T, WG), F32), SDS((T, WG), F32), SDS((B, NH, SEQ, 1), F32),
                   SDS((B, NH, n, 1, FB), F32)],
        scratch_shapes=[pltpu.VMEM((SEQ, LANE), F32)],
        compiler_params=_cp("arbitrary", "arbitrary", "arbitrary"), name=name,
    )(pb, pb, pb, do, out, lse, fcol, frow)


def _tri(lower):
    r = lax.broadcasted_iota(jnp.int32, (LANE, LANE), 0)
    c = lax.broadcasted_iota(jnp.int32, (LANE, LANE), 1)
    return ((r >= c) if lower else (r <= c)).astype(F32)


def _tri_dot(t, xblk):
    return jnp.dot(t, xblk, precision=lax.Precision.HIGHEST, preferred_element_type=F32)


def forget_cumsum(flog, bias, name):
    B, S, _ = flog.shape

    def body(f_ref, b_ref, o_ref):
        t = _tri(True)
        carry = jnp.zeros((1, LANE), F32)
        for blk in range(S // LANE):
            z = f_ref[0, blk * LANE:(blk + 1) * LANE, :] + b_ref[...]
            lf = jnp.minimum(z, 0.0) - jnp.log(1.0 + jnp.exp(-jnp.abs(z)))
            cs = _tri_dot(t, lf) + carry
            o_ref[0, blk * LANE:(blk + 1) * LANE, :] = cs
            carry = cs[LANE - 1:LANE, :]

    spec = pl.BlockSpec((1, S, LANE), lambda b: (b, 0, 0))
    return pl.pallas_call(
        body, grid=(B,), in_specs=[spec, pl.BlockSpec((1, LANE), lambda b: (0, 0))], out_specs=spec,
        out_shape=SDS((B, S, LANE), F32), compiler_params=_cp("arbitrary"), name=name,
    )(flog, bias)


def forget_cumsum_bwd(dF, flog, bias, name):
    B, S, _ = flog.shape

    def body(d_ref, f_ref, b_ref, o_ref, db_ref):
        @pl.when(pl.program_id(0) == 0)
        def _():
            db_ref[...] = jnp.zeros_like(db_ref)

        t = _tri(False)
        carry = jnp.zeros((1, LANE), F32)
        tot = jnp.zeros((1, LANE), F32)
        for blk in reversed(range(S // LANE)):
            sl = slice(blk * LANE, (blk + 1) * LANE)
            rc = _tri_dot(t, d_ref[0, sl, :]) + carry
            carry = rc[0:1, :]
            z = f_ref[0, sl, :] + b_ref[...]
            dz = rc * jax.nn.sigmoid(-z)
            o_ref[0, sl, :] = dz
            tot = tot + jnp.sum(dz, axis=0, keepdims=True)
        db_ref[...] += tot

    spec = pl.BlockSpec((1, S, LANE), lambda b: (b, 0, 0))
    vec = pl.BlockSpec((1, LANE), lambda b: (0, 0))
    return pl.pallas_call(
        body, grid=(B,), in_specs=[spec, spec, vec], out_specs=[spec, vec],
        out_shape=[SDS((B, S, LANE), F32), SDS((1, LANE), F32)], compiler_params=_cp("arbitrary"), name=name,
    )(dF, flog, bias)


def mixer_out_fwd(oa, ob, goa, gob, w_out, g_post, x, mod3, name):
    T = x.shape[0]

    def body(oa_ref, ob_ref, goa_ref, gob_ref, w_ref, gp_ref, x_ref, mod_ref, xo_ref, mg_ref, y0_ref):
        a = oa_ref[...]
        b = ob_ref[...]
        mg = jnp.concatenate([a * _rms(a) * goa_ref[...], b * _rms(b) * gob_ref[...]], axis=-1).astype(BF16)
        mg_ref[...] = mg
        y0 = _dot(mg, w_ref[...])
        y0_ref[...] = y0
        xo_ref[...] = x_ref[...] + mod_ref[0, 2:3, :] * (y0 * _rms(y0) * gp_ref[...])

    tok = pl.BlockSpec((TM, D), lambda i: (i, 0))
    half = pl.BlockSpec((TM, WG), lambda i: (i, 0))
    hv = pl.BlockSpec((1, WG), lambda i: (0, 0))
    return pl.pallas_call(
        body, grid=(T // TM,),
        in_specs=[half, half, hv, hv, pl.BlockSpec((D, D), lambda i: (0, 0)), pl.BlockSpec((1, D), lambda i: (0, 0)), tok,
                  pl.BlockSpec((1, 3, D), _mod_map)],
        out_specs=[tok, tok, tok], out_shape=[SDS((T, D), F32), SDS((T, D), BF16), SDS((T, D), F32)],
        compiler_params=_cp("arbitrary"), name=name,
    )(oa, ob, goa, gob, w_out, g_post, x, mod3)


def mixer_out_bwd(dxo, y0, mod3, g_post, w_out, oa, ob, goa, gob, name):
    T = dxo.shape[0]
    nb = T // SEQ
    tiles_per_seq = SEQ // TM

    def body(dxo_ref, y0_ref, mod_ref, gp_ref, w_ref, oa_ref, ob_ref, goa_ref, gob_ref,
             dy0_ref, doa_ref, dob_ref, dmg_ref, dgp_ref, dgoa_ref, dgob_ref):
        i = pl.program_id(0)

        @pl.when(i == 0)
        def _():
            dgp_ref[...] = jnp.zeros_like(dgp_ref)
            dgoa_ref[...] = jnp.zeros_like(dgoa_ref)
            dgob_ref[...] = jnp.zeros_like(dgob_ref)

        @pl.when(i % tiles_per_seq == 0)
        def _():
            dmg_ref[...] = jnp.zeros_like(dmg_ref)

        dy0, dmg, dg = _post_bwd(dxo_ref[...], y0_ref[...], gp_ref[...], mod_ref[0, 2:3, :], 1.0)
        dmg_ref[0] += dmg
        dgp_ref[...] += dg
        db = dy0.astype(BF16)
        dy0_ref[...] = db
        dm = _dot_nt(db, w_ref[...])
        for o_ref, g_ref, do_ref, dg_ref, sl in ((oa_ref, goa_ref, doa_ref, dgoa_ref, slice(0, WG)),
                                                  (ob_ref, gob_ref, dob_ref, dgob_ref, slice(WG, 2 * WG))):
            o = o_ref[...]
            r = _rms(o)
            oh = o * r
            d = dm[:, sl]
            dg_ref[...] += jnp.sum(d * oh, axis=0, keepdims=True)
            dh = d * g_ref[...]
            do_ref[...] = r * (dh - oh * jnp.mean(dh * oh, axis=-1, keepdims=True))

    tok = pl.BlockSpec((TM, D), lambda i: (i, 0))
    half = pl.BlockSpec((TM, WG), lambda i: (i, 0))
    hv = pl.BlockSpec((1, WG), lambda i: (0, 0))
    vec = pl.BlockSpec((1, D), lambda i: (0, 0))
    return pl.pallas_call(
        body, grid=(T // TM,),
        in_specs=[tok, tok, pl.BlockSpec((1, 3, D), _mod_map), vec, pl.BlockSpec((D, D), lambda i: (0, 0)), half, half, hv, hv],
        out_specs=[tok, half, half, pl.BlockSpec((1, 1, D), _mod_map), vec, hv, hv],
        out_shape=[SDS((T, D), BF16), SDS((T, WG), F32), SDS((T, WG), F32), SDS((nb, 1, D), F32), SDS((1, D), F32),
                   SDS((1, WG), F32), SDS((1, WG), F32)],
        compiler_params=_cp("arbitrary"), name=name,
    )(dxo, y0, mod3, g_post, w_out, oa, ob, goa, gob)


def proj_grad_assemble(grads, rc, rs1, rs2, name):
    T = grads[0].shape[0]

    def body(*refs):
        ins, (c_ref, s1_ref, s2_ref, o_ref) = refs[:6], refs[6:]
        c, s1, s2 = c_ref[...], s1_ref[...], s2_ref[...]
        for grp in range(2):
            for k in range(WG // LANE):
                d = ins[grp][:, k * LANE:(k + 1) * LANE]
                un = d * c + pltpu.roll(d * s1, 8, 1) + pltpu.roll(d * s2, LANE - 8, 1)
                o_ref[:, grp * WG + k * LANE:grp * WG + (k + 1) * LANE] = un.astype(BF16)
        for g in range(2, 6):
            o_ref[:, g * WG:(g + 1) * WG] = ins[g][...].astype(BF16)

    half = pl.BlockSpec((TM, WG), lambda i: (i, 0))
    tab = pl.BlockSpec((TM, LANE), lambda i: (i, 0))
    return pl.pallas_call(
        body, grid=(T // TM,), in_specs=[half] * 6 + [tab] * 3, out_specs=pl.BlockSpec((TM, IN_MAIN), lambda i: (i, 0)),
        out_shape=SDS((T, IN_MAIN), BF16), compiler_params=_cp("arbitrary"), name=name,
    )(*grads, rc, rs1, rs2)


def mixer_proj_bwd(dproj, dflog, dxo, x, mod3, g_pre, w_main, w_f, name):
    T = x.shape[0]
    nb = T // SEQ
    tiles_per_seq = SEQ // TM
    tk = 1024
    nj = IN_MAIN // tk

    def body(dp_ref, df_ref, dxo_ref, x_ref, mod_ref, g_ref, w_ref, wf_ref, dx_ref, dmod_ref, dg_ref, acc):
        i = pl.program_id(0)
        j = pl.program_id(1)

        @pl.when((i == 0) & (j == 0))
        def _():
            dg_ref[...] = jnp.zeros_like(dg_ref)

        @pl.when((i % tiles_per_seq == 0) & (j == 0))
        def _():
            dmod_ref[...] = jnp.zeros_like(dmod_ref)

        @pl.when(j == 0)
        def _():
            acc[...] = _dot_nt(df_ref[...].astype(BF16), wf_ref[...])

        acc[...] += _dot_nt(dp_ref[...], w_ref[...])

        @pl.when(j == nj - 1)
        def _():
            dx, dsh, dsc, dg = _norm_mod_bwd(acc[...], x_ref[...], g_ref[...], mod_ref[0, 1:2, :])
            dx_ref[...] = dxo_ref[...] + dx
            dmod_ref[0, 0:1, :] += dsh
            dmod_ref[0, 1:2, :] += dsc
            dg_ref[...] += dg

    tok = pl.BlockSpec((TM, D), lambda i, j: (i, 0))
    vec = pl.BlockSpec((1, D), lambda i, j: (0, 0))
    return pl.pallas_call(
        body, grid=(T // TM, nj),
        in_specs=[pl.BlockSpec((TM, tk), lambda i, j: (i, j)), pl.BlockSpec((TM, LANE), lambda i, j: (i, 0)), tok, tok,
                  pl.BlockSpec((1, 3, D), _mod_map), vec, pl.BlockSpec((D, tk), lambda i, j: (0, j)),
                  pl.BlockSpec((D, LANE), lambda i, j: (0, 0))],
        out_specs=[tok, pl.BlockSpec((1, 2, D), _mod_map), vec],
        out_shape=[SDS((T, D), F32), SDS((nb, 2, D), F32), SDS((1, D), F32)],
        scratch_shapes=[pltpu.VMEM((TM, D), F32)],
        compiler_params=_cp("arbitrary", "arbitrary"), name=name,
    )(dproj, dflog, dxo, x, mod3, g_pre, w_main, w_f)


def ada_fwd(c_all, w, b, name):
    n = w.shape[1]
    tn = n // 2

    def body(c_ref, w_ref, b_ref, o_ref):
        cv = c_ref[...]
        o_ref[...] = _dot((cv * jax.nn.sigmoid(cv)).astype(BF16), w_ref[...].astype(BF16)) + b_ref[...]

    R = c_all.shape[0]
    return pl.pallas_call(
        body, grid=(2,),
        in_specs=[pl.BlockSpec((R, D), lambda j: (0, 0)), pl.BlockSpec((D, tn), lambda j: (0, j)), pl.BlockSpec((1, tn), lambda j: (0, j))],
        out_specs=pl.BlockSpec((R, tn), lambda j: (0, j)), out_shape=SDS((R, n), F32),
        compiler_params=_cp("arbitrary"), name=name,
    )(c_all, w, b)


def ada_bwd(c_all, dmod, name):
    R, n = dmod.shape
    tn = n // 2

    def body(c_ref, d_ref, o_ref):
        cv = c_ref[...]
        o_ref[...] = _dot_tn((cv * jax.nn.sigmoid(cv)).astype(BF16), d_ref[...].astype(BF16))

    return pl.pallas_call(
        body, grid=(2,), in_specs=[pl.BlockSpec((R, D), lambda j: (0, 0)), pl.BlockSpec((R, tn), lambda j: (0, j))],
        out_specs=pl.BlockSpec((D, tn), lambda j: (0, j)), out_shape=SDS((D, n), F32),
        compiler_params=_cp("arbitrary"), name=name,
    )(c_all, dmod)


def _adam_math(w, g, m, v):
    m2 = ADAM_B1 * m + (1.0 - ADAM_B1) * g
    v2 = ADAM_B2 * v + (1.0 - ADAM_B2) * (g * g)
    m_hat = m2 / (1.0 - ADAM_B1 ** ADAM_STEP)
    v_hat = v2 / (1.0 - ADAM_B2 ** ADAM_STEP)
    delta = -ADAM_LR * (m_hat / (jnp.sqrt(v_hat) + ADAM_EPS) + ADAM_WD * w)
    return delta, m2, v2


def adam_update(w, g, m, v, tr, name):
    R, C = w.shape

    def body(w_ref, g_ref, m_ref, v_ref, d_ref, mo_ref, vo_ref):
        d_ref[...], mo_ref[...], vo_ref[...] = _adam_math(w_ref[...], g_ref[...], m_ref[...], v_ref[...])

    spec = pl.BlockSpec((tr, C), lambda i: (i, 0))
    return pl.pallas_call(
        body, grid=(R // tr,), in_specs=[spec] * 4, out_specs=[spec] * 3, out_shape=[SDS((R, C), F32)] * 3,
        compiler_params=_cp("arbitrary"), name=name,
    )(w, g, m, v)


def vec_adam(parts, w, m, v, name):
    P, C = parts.shape

    def body(p_ref, w_ref, m_ref, v_ref, g_ref, d_ref, mo_ref, vo_ref):
        g = jnp.sum(p_ref[...], axis=0, keepdims=True)
        g_ref[...] = g
        d_ref[...], mo_ref[...], vo_ref[...] = _adam_math(w_ref[...], g, m_ref[...], v_ref[...])

    return pl.pallas_call(body, out_shape=[SDS((1, C), F32)] * 4, compiler_params=_cp(), name=name)(parts, w, m, v)


HBM = pl.BlockSpec(memory_space=pltpu.HBM)
VMEM = pl.BlockSpec(memory_space=pltpu.VMEM)


def _place():
    x, y, c = lax.axis_index("x"), lax.axis_index("y"), lax.axis_index("c")
    return x, y, c, [(1 - x, y), (x, 1 - y), (1 - x, 1 - y)]


def all_gather8(xs, name):
    R, C = xs.shape

    def body(x_ref, out_ref, send_sems, recv_sems, local_sem):
        x, y, c, chips = _place()
        me, sibling = (x, y, c), (x, y, 1 - c)

        def slot(px, py, pc):
            return out_ref.at[4 * px + 2 * py + pc]

        def copy(k, block, to, src=None):
            return pltpu.make_async_remote_copy(
                src_ref=slot(*block) if src is None else src, dst_ref=slot(*block),
                send_sem=send_sems.at[k], recv_sem=recv_sems.at[k], device_id=to, device_id_type=MESH)

        mine = pltpu.make_async_copy(x_ref, slot(*me), local_sem)
        mine.start()
        first = [copy(0, me, sibling, src=x_ref)]
        first += [copy(1 + j, me, (*chip, c), src=x_ref) for j, chip in enumerate(chips)]
        for cp in first:
            cp.start()
        passed = [copy(4 + j, (*chip, c), sibling) for j, chip in enumerate(chips)]
        for j, chip in enumerate(chips):
            copy(1 + j, (*chip, c), me).wait_recv()
            passed[j].start()
        copy(0, sibling, me).wait_recv()
        for j, chip in enumerate(chips):
            copy(4 + j, (*chip, 1 - c), me).wait_recv()
        for cp in first + passed:
            cp.wait_send()
        mine.wait()

    return pl.pallas_call(
        body, out_shape=SDS((N_DEV, R, C), xs.dtype), in_specs=[VMEM], out_specs=VMEM,
        scratch_shapes=[pltpu.SemaphoreType.DMA((7,)), pltpu.SemaphoreType.DMA((7,)), pltpu.SemaphoreType.DMA],
        compiler_params=pltpu.CompilerParams(vmem_limit_bytes=VMEM_LIMIT), name=name,
    )(xs)


def all_gather_shards(ws, splits, name):
    n = len(ws)

    def body(*refs):
        w_refs, o_refs = refs[:n], refs[n:2 * n]
        send_sems, recv_sems, local_sems = refs[2 * n:]
        x, y, c, chips = _place()
        sibling = (x, y, 1 - c)
        me_s = 2 * x + y

        def half(ref, k, cc):
            lo, hi = (0, splits[k]) if cc == 0 else (splits[k], ws[k].shape[0])
            return ref.at[pl.ds(lo, hi - lo)]

        def rcopy(src, dst, k, s, to):
            return pltpu.make_async_remote_copy(src_ref=src, dst_ref=dst, send_sem=send_sems.at[k, s],
                                                recv_sem=recv_sems.at[k, s], device_id=to, device_id_type=MESH)

        for cc in (0, 1):
            @pl.when(c == cc)
            def _():
                started, local = [], []
                for k in range(n):
                    cp = pltpu.make_async_copy(w_refs[k], o_refs[k].at[me_s], local_sems.at[k])
                    cp.start()
                    local.append(cp)
                    for j, chip in enumerate(chips):
                        s = rcopy(half(w_refs[k], k, cc), half(o_refs[k].at[me_s], k, cc), k, j, (*chip, c))
                        s.start()
                        started.append(s)
                for k in range(n):
                    for j, chip in enumerate(chips):
                        land = half(o_refs[k].at[2 * chip[0] + chip[1]], k, cc)
                        rcopy(land, land, k, j, (*chip, c)).wait_recv()
                        f = rcopy(land, land, k, 3 + j, sibling)
                        f.start()
                        started.append(f)
                for k in range(n):
                    for j, chip in enumerate(chips):
                        other = half(o_refs[k].at[2 * chip[0] + chip[1]], k, 1 - cc)
                        rcopy(other, other, k, 3 + j, sibling).wait_recv()
                for s in started:
                    s.wait_send()
                for cp in local:
                    cp.wait()

    return pl.pallas_call(
        body, out_shape=[SDS((N_SHARD,) + w.shape, w.dtype) for w in ws], in_specs=[HBM] * n, out_specs=[HBM] * n,
        scratch_shapes=[pltpu.SemaphoreType.DMA((n, 6)), pltpu.SemaphoreType.DMA((n, 6)), pltpu.SemaphoreType.DMA((n,))],
        name=name,
    )(*ws)


def sibling_send_half(gs, name):
    n = len(gs)

    def body(*refs):
        g_refs, o_refs = refs[:n], refs[n:2 * n]
        send_sems, recv_sems = refs[2 * n:]
        x, y, c, _ = _place()
        cps = []
        for k in range(n):
            hr = gs[k].shape[1] // 2
            src = g_refs[k].at[:, pl.ds(pl.multiple_of((1 - c) * hr, 8), hr)]
            cp = pltpu.make_async_remote_copy(src_ref=src, dst_ref=o_refs[k], send_sem=send_sems.at[k], recv_sem=recv_sems.at[k],
                                              device_id=(x, y, 1 - c), device_id_type=MESH)
            cp.start()
            cps.append(cp)
        for cp in cps:
            cp.wait()

    return pl.pallas_call(
        body, out_shape=[SDS((N_SHARD, g.shape[1] // 2, g.shape[2]), g.dtype) for g in gs], in_specs=[HBM] * n, out_specs=[HBM] * n,
        scratch_shapes=[pltpu.SemaphoreType.DMA((n,)), pltpu.SemaphoreType.DMA((n,))], name=name,
    )(*gs)


def chip_scatter(hs, name):
    n = len(hs)

    def body(*refs):
        h_refs, o_refs = refs[:n], refs[n:2 * n]
        send_sems, recv_sems = refs[2 * n:]
        x, y, c, chips = _place()
        cps = []
        for k in range(n):
            for j, chip in enumerate(chips):
                cp = pltpu.make_async_remote_copy(
                    src_ref=h_refs[k].at[2 * chip[0] + chip[1]], dst_ref=o_refs[k].at[j], send_sem=send_sems.at[k, j],
                    recv_sem=recv_sems.at[k, j], device_id=(*chip, c), device_id_type=MESH)
                cp.start()
                cps.append(cp)
        for cp in cps:
            cp.wait()

    return pl.pallas_call(
        body, out_shape=[SDS((3,) + h.shape[1:], h.dtype) for h in hs], in_specs=[HBM] * n, out_specs=[HBM] * n,
        scratch_shapes=[pltpu.SemaphoreType.DMA((n, 3)), pltpu.SemaphoreType.DMA((n, 3))], name=name,
    )(*hs)


def sibling_share_half(ghs, name):
    n = len(ghs)

    def body(*refs):
        g_refs, o_refs = refs[:n], refs[n:2 * n]
        send_sems, recv_sems, local_sems = refs[2 * n:]
        x, y, c, _ = _place()
        cps = []
        for k in range(n):
            hr = ghs[k].shape[0]
            dst = o_refs[k].at[pl.ds(pl.multiple_of(c * hr, 8), hr)]
            lc = pltpu.make_async_copy(g_refs[k], dst, local_sems.at[k])
            lc.start()
            cp = pltpu.make_async_remote_copy(src_ref=g_refs[k], dst_ref=dst, send_sem=send_sems.at[k], recv_sem=recv_sems.at[k],
                                              device_id=(x, y, 1 - c), device_id_type=MESH)
            cp.start()
            cps += [lc, cp]
        for cp in cps:
            cp.wait()

    return pl.pallas_call(
        body, out_shape=[SDS((2 * g.shape[0], g.shape[1]), g.dtype) for g in ghs], in_specs=[HBM] * n, out_specs=[HBM] * n,
        scratch_shapes=[pltpu.SemaphoreType.DMA((n,)), pltpu.SemaphoreType.DMA((n,)), pltpu.SemaphoreType.DMA((n,))],
        name=name,
    )(*ghs)


def pair_sum(g, ra, cidx, name):
    _, r, cols = g.shape
    hr = r // 2

    def body(c_ref, g_ref, a_ref, o_ref):
        o_ref[...] = (g_ref[...] + a_ref[...]).astype(BF16)

    return pl.pallas_call(
        body,
        grid_spec=pltpu.PrefetchScalarGridSpec(
            num_scalar_prefetch=1, grid=(N_SHARD,),
            in_specs=[pl.BlockSpec((1, hr, cols), lambda s, c_ref: (s, c_ref[0], 0)),
                      pl.BlockSpec((1, hr, cols), lambda s, c_ref: (s, 0, 0))],
            out_specs=pl.BlockSpec((1, hr, cols), lambda s, c_ref: (s, 0, 0))),
        out_shape=SDS((N_SHARD, hr, cols), BF16), compiler_params=_cp("arbitrary"), name=name,
    )(cidx, g, ra)


def chip_sum(h, rb, sidx, name):
    _, hr, cols = h.shape

    def body(s_ref, h_ref, r_ref, o_ref):
        o_ref[...] = ((h_ref[0].astype(F32) + r_ref[0].astype(F32)) + r_ref[1].astype(F32)) + r_ref[2].astype(F32)

    return pl.pallas_call(
        body,
        grid_spec=pltpu.PrefetchScalarGridSpec(
            num_scalar_prefetch=1, grid=(1,),
            in_specs=[pl.BlockSpec((1, hr, cols), lambda i, s_ref: (s_ref[0], 0, 0)),
                      pl.BlockSpec((3, hr, cols), lambda i, s_ref: (0, 0, 0))],
            out_specs=pl.BlockSpec((hr, cols), lambda i, s_ref: (0, 0))),
        out_shape=SDS((hr, cols), F32), compiler_params=_cp("arbitrary"), name=name,
    )(sidx, h, rb)


def _shard_cols(g, n_valid):
    r = g.shape[0]
    return g[:, :n_valid].reshape(r, N_SHARD, n_valid // N_SHARD).transpose(1, 0, 2)


def _unshard_cols(o, pad_to):
    _, r, n = o.shape
    full = o.transpose(1, 0, 2).reshape(r, N_SHARD * n)
    return jnp.pad(full, ((0, 0), (0, pad_to - N_SHARD * n)))


def mixer_fwd(x1, mod3, g_pre, w_main, w_f, b_forget_pad, goa, gob, w_out, g_post, tabs, nb):
    hmix, pa, pb, flog = mixer_proj(x1, mod3, g_pre, w_main, w_f, *tabs, name="mixer_proj")
    out_a, lse_a = band_fwd(pa, name="band_fwd")
    F = forget_cumsum(flog.reshape(nb, SEQ, LANE), b_forget_pad, name="forget_cumsum")
    Fh = F[:, :, :NH].transpose(0, 2, 1)
    fcol = Fh.reshape(nb, NH, SEQ, 1)
    frow = Fh.reshape(nb, NH, SEQ // FB, 1, FB)
    out_b, lse_b = fox_fwd(pb, fcol, Fh.reshape(nb, NH, SEQ // FT, 1, FT), name="fox_fwd")
    x2, merged, y0m = mixer_out_fwd(out_a, out_b, goa, gob, w_out, g_post, x1, mod3, name="mixer_out_fwd")
    res = dict(hmix=hmix, flog=flog, pa=pa, pb=pb, out_a=out_a, lse_a=lse_a, fcol=fcol, frow=frow, out_b=out_b, lse_b=lse_b,
               merged=merged, y0m=y0m)
    return x2, res


def mixer_bwd(dx2, x1, mod3, g_pre, w_main, w_f, b_forget_pad, goa, gob, w_out, g_post, tabs, res, nb):
    T = nb * SEQ
    dy0m, doa, dob, dmgate, dg_post, dgoa, dgob = mixer_out_bwd(
        dx2, res["y0m"], mod3, g_post, w_out, res["out_a"], res["out_b"], goa, gob, name="mixer_out_bwd")
    dqa, dka, dva = band_bwd(res["pa"], doa, res["out_a"], res["lse_a"], name="band_bwd")
    dqb, dkb, dvb, dfq, dfk = fox_bwd(res["pb"], dob, res["out_b"], res["lse_b"], res["fcol"], res["frow"], name="fox_bwd")
    dF = (dfq.reshape(nb, NH, SEQ) + dfk.reshape(nb, NH, SEQ)).transpose(0, 2, 1)
    dF = jnp.pad(dF, ((0, 0), (0, 0), (0, LANE - NH)))
    dflog, dbf = forget_cumsum_bwd(dF, res["flog"].reshape(nb, SEQ, LANE), b_forget_pad, name="forget_cumsum_bwd")
    dflog = dflog.reshape(T, LANE)
    dproj = proj_grad_assemble((dqa, dka, dva, dqb, dkb, dvb), *tabs, name="proj_grad_assemble")
    dx1, dmod2, dg_pre = mixer_proj_bwd(dproj, dflog, dx2, x1, mod3, g_pre, w_main, w_f, name="mixer_proj_bwd")
    g_main = matmul_tn(res["hmix"], dproj, D, 1024, 1024, name="grad_w_in")
    g_f = matmul_tn(res["hmix"], dflog.astype(BF16), D, LANE, 1024, name="grad_w_forget")
    g_out = matmul_tn(res["merged"], dy0m, D, D, 1024, name="grad_w_out")
    dmod3 = jnp.concatenate([dmod2, dmgate], axis=1)
    return dx1, dmod3, dict(g_pre=dg_pre, g_post=dg_post, goa=dgoa, gob=dgob, b_forget=dbf[:, :NH],
                            w_in=jnp.concatenate([g_main, g_f[:, :NH]], axis=1), w_out=g_out)


def ffn_grads(h, dy0, act, dgate, dup, pre):
    g_gate = matmul_tn(h, dgate, D, FF_TN, 1024, name=pre + "_grad_gate")
    g_up = matmul_tn(h, dup, D, FF_TN, 1024, name=pre + "_grad_up")
    g_down = matmul_tn(act, dy0, FF_TN, D, 1024, name=pre + "_grad_down")
    return g_gate, g_up, g_down


def local_step(x0, tgt, pos_col, mod, wfull, p):
    T = x0.shape[0]
    nb = T // SEQ
    mod_ff1, mod_mix, mod_ff2 = mod[:, 0:3], mod[:, 3:6], mod[:, 6:9]
    tabs = rope_tables(pos_col, name="rope_tables")
    bf_pad = jnp.pad(p["b_forget"], ((0, 0), (0, LANE - NH)))

    x1, h1, gate1, up1, y01 = ffn_fwd(x0, mod_ff1, p["g_pre_ff1"], p["g_post_ff1"], wfull["w_ff1_gate"], wfull["w_ff1_up"],
                                      wfull["w_ff1_down"], 0.5, name="ff1_fwd")
    x2, res = mixer_fwd(x1, mod_mix, p["g_pre_mix"], wfull["w_main"], wfull["w_f"], bf_pad, p["g_out_a"], p["g_out_b"],
                        wfull["w_out"], p["g_post_mix"], tabs, nb)
    x3, h2, gate2, up2, y02 = ffn_fwd(x2, mod_ff2, p["g_pre_ff2"], p["g_post_ff2"], wfull["w_ff2_gate"], wfull["w_ff2_up"],
                                      wfull["w_ff2_down"], 0.5, name="ff2_fwd")

    dx3, loss_part = loss_grad(x3, tgt, name="loss_grad")
    dx2, dy02, act2, dgate2, dup2, dmod_ff2, dgpre2, dgpost2 = ffn_bwd(
        dx3, x2, y02, mod_ff2, p["g_pre_ff2"], p["g_post_ff2"], gate2, up2, wfull["w_ff2_gate"], wfull["w_ff2_up"],
        wfull["w_ff2_down"], 0.5, name="ff2_bwd")
    gw = {}
    gw["w_ff2_gate"], gw["w_ff2_up"], gw["w_ff2_down"] = ffn_grads(h2, dy02, act2, dgate2, dup2, "ff2")
    dx1, dmod_mix, gmix = mixer_bwd(dx2, x1, mod_mix, p["g_pre_mix"], wfull["w_main"], wfull["w_f"], bf_pad, p["g_out_a"],
                                    p["g_out_b"], wfull["w_out"], p["g_post_mix"], tabs, res, nb)
    gw["w_in"], gw["w_out"] = gmix["w_in"], gmix["w_out"]
    dx0, dy01, act1, dgate1, dup1, dmod_ff1, dgpre1, dgpost1 = ffn_bwd(
        dx1, x0, y01, mod_ff1, p["g_pre_ff1"], p["g_post_ff1"], gate1, up1, wfull["w_ff1_gate"], wfull["w_ff1_up"],
        wfull["w_ff1_down"], 0.5, name="ff1_bwd")
    gw["w_ff1_gate"], gw["w_ff1_up"], gw["w_ff1_down"] = ffn_grads(h1, dy01, act1, dgate1, dup1, "ff1")
    dmod = jnp.concatenate([dmod_ff1, dmod_mix, dmod_ff2], axis=1).reshape(nb, 9 * D)
    small = dict(g_pre_ff1=dgpre1, g_post_ff1=dgpost1, g_pre_mix=gmix["g_pre"], g_post_mix=gmix["g_post"], g_pre_ff2=dgpre2,
                 g_post_ff2=dgpost2, g_out_a=gmix["goa"], g_out_b=gmix["gob"], b_forget=gmix["b_forget"])
    return loss_part, dx0, dmod, gw, small


def kernel(x, c, positions, w_ada, b_ada, g_pre_ff1, g_post_ff1, w_ff1_gate, w_ff1_up, w_ff1_down, g_pre_mix, g_post_mix, w_in, b_forget, g_out_a, g_out_b, w_out, g_pre_ff2, g_post_ff2, w_ff2_gate, w_ff2_up, w_ff2_down, loss_target, m_w_ada, m_b_ada, m_g_pre_ff1, m_g_post_ff1, m_w_ff1_gate, m_w_ff1_up, m_w_ff1_down, m_g_pre_mix, m_g_post_mix, m_w_in, m_b_forget, m_g_out_a, m_g_out_b, m_w_out, m_g_pre_ff2, m_g_post_ff2, m_w_ff2_gate, m_w_ff2_up, m_w_ff2_down, v_w_ada, v_b_ada, v_g_pre_ff1, v_g_post_ff1, v_w_ff1_gate, v_w_ff1_up, v_w_ff1_down, v_g_pre_mix, v_g_post_mix, v_w_in, v_b_forget, v_g_out_a, v_g_out_b, v_w_out, v_g_pre_ff2, v_g_post_ff2, v_w_ff2_gate, v_w_ff2_up, v_w_ff2_down):
    args = dict(locals())
    nb = x.shape[0]
    T = nb * SEQ
    ax, ay, ac = lax.axis_index("x"), lax.axis_index("y"), lax.axis_index("c")
    shard = 2 * ax + ay
    cidx = jnp.reshape(ac, (1,)).astype(jnp.int32)
    sidx = jnp.reshape(shard, (1,)).astype(jnp.int32)

    big = ["w_ff1_gate", "w_ff1_up", "w_ff1_down", "w_in", "w_out", "w_ff2_gate", "w_ff2_up", "w_ff2_down"]
    vecs = ["g_pre_ff1", "g_post_ff1", "g_pre_mix", "g_post_mix", "g_pre_ff2", "g_post_ff2"]

    shards_bf = [args[n][0].astype(BF16) for n in big]
    splits = [512, 512, 352, 512, 128, 512, 512, 352]
    gathered = all_gather_shards(shards_bf, splits, name="all_gather_weights")
    wfull = {}
    for n, o in zip(big, gathered):
        if n.endswith("gate") or n.endswith("up"):
            wfull[n] = _unshard_cols(o, DFF_PAD)
        elif n.endswith("down"):
            wfull[n] = jnp.pad(o.reshape(DFF, D), ((0, DFF_PAD - DFF), (0, 0)))
        elif n == "w_in":
            full = _unshard_cols(o, IN_COLS)
            wfull["w_main"] = full[:, :IN_MAIN]
            wfull["w_f"] = jnp.pad(full[:, IN_MAIN:], ((0, 0), (0, LANE - NH)))
        else:
            wfull[n] = o.reshape(D, D)

    ncol = w_ada.shape[2]
    c_all = all_gather8(c, name="all_gather_c").reshape(N_DEV * nb, D)
    b_loc = lax.dynamic_slice(b_ada, (0, shard * ncol), (1, ncol))
    mod_loc = ada_fwd(c_all, w_ada[0], b_loc, name="ada_fwd")
    mod_g = all_gather8(mod_loc, name="all_gather_mod")
    row0 = (4 * ax + 2 * ay + ac) * nb
    mod_rows = lax.dynamic_slice(mod_g, (0, row0, 0), (N_DEV, nb, ncol))
    mod = jnp.concatenate([mod_rows[2 * s] for s in range(N_SHARD)], axis=-1).reshape(nb, 9, D)

    small_in = dict(g_pre_ff1=g_pre_ff1, g_post_ff1=g_post_ff1, g_pre_mix=g_pre_mix, g_post_mix=g_post_mix, g_pre_ff2=g_pre_ff2,
                    g_post_ff2=g_post_ff2, g_out_a=g_out_a, g_out_b=g_out_b, b_forget=b_forget)
    loss_part, dx0, dmod, gw, small = local_step(x.reshape(T, D), loss_target.reshape(T, D), positions.reshape(T, 1), mod, wfull,
                                                 small_in)

    dmod_all = all_gather8(dmod, name="all_gather_dmod").reshape(N_DEV * nb, 9 * D)
    dmod_loc = lax.dynamic_slice(dmod_all, (0, shard * ncol), (N_DEV * nb, ncol))
    g_w_ada = ada_bwd(c_all, dmod_loc, name="ada_bwd")

    def shard_blocked(n, g):
        if n.endswith("gate") or n.endswith("up"):
            return _shard_cols(g, DFF)
        if n.endswith("down"):
            return g[:DFF].reshape(N_SHARD, DFF // N_SHARD, D)
        if n == "w_in":
            return _shard_cols(g, IN_COLS)
        return g.reshape(N_SHARD, D // N_SHARD, D)

    gsb = [shard_blocked(n, gw[n]) for n in big]
    ras = sibling_send_half(gsb, name="grad_sibling_send")
    hs = [pair_sum(g, ra, cidx, name=f"grad_pair_sum_{n}") for n, g, ra in zip(big, gsb, ras)]
    rbs = chip_scatter(hs, name="grad_chip_scatter")
    ghs = [chip_sum(h, rb, sidx, name=f"grad_chip_sum_{n}") for n, h, rb in zip(big, hs, rbs)]
    gfull = dict(zip(big, sibling_share_half(ghs, name="grad_sibling_share")))
    gfull["w_ada"] = g_w_ada

    row6 = jnp.concatenate([small["g_out_a"], small["g_out_b"]], axis=1)
    row7 = jnp.concatenate([small["b_forget"], loss_part[0:1, 0:1], jnp.zeros((1, D - NH - 1), F32)], axis=1)
    pack = jnp.concatenate([small[n] for n in vecs] + [row6, row7], axis=0)
    packed = all_gather8(pack, name="all_gather_small").reshape(N_DEV, 8 * D)

    def pack_state(pre):
        r6 = jnp.concatenate([args[pre + "g_out_a"], args[pre + "g_out_b"]], axis=1)
        r7 = jnp.pad(args[pre + "b_forget"], ((0, 0), (0, D - NH)))
        return jnp.concatenate([args[pre + n] for n in vecs] + [r6, r7], axis=0).reshape(1, 8 * D)

    sg, sd, sm, sv = (t.reshape(8, D) for t in vec_adam(packed, pack_state(""), pack_state("m_"), pack_state("v_"), name="adam_small"))

    def unpack(t):
        out = {n: t[i:i + 1] for i, n in enumerate(vecs)}
        out["g_out_a"], out["g_out_b"], out["b_forget"] = t[6:7, :WG], t[6:7, WG:], t[7:8, :NH]
        return out

    outs = dict(grad=unpack(sg), delta=unpack(sd), new_m=unpack(sm), new_v=unpack(sv))
    loss = sg[7, NH]
    outs["grad"]["b_ada"], outs["delta"]["b_ada"], outs["new_m"]["b_ada"], outs["new_v"]["b_ada"] = vec_adam(
        dmod_all, b_ada, m_b_ada, v_b_ada, name="adam_b_ada")

    for n in big + ["w_ada"]:
        g = gfull[n]
        rows = g.shape[0]
        tr = 128 if rows % 128 == 0 else 344
        d, m2, v2 = adam_update(args[n][0], g, args["m_" + n][0], args["v_" + n][0], tr, name="adam_" + n)
        outs["grad"][n], outs["delta"][n], outs["new_m"][n], outs["new_v"][n] = g[None], d[None], m2[None], v2[None]

    order = ["w_ada", "b_ada", "g_pre_ff1", "g_post_ff1", "w_ff1_gate", "w_ff1_up", "w_ff1_down", "g_pre_mix", "g_post_mix", "w_in",
             "b_forget", "g_out_a", "g_out_b", "w_out", "g_pre_ff2", "g_post_ff2", "w_ff2_gate", "w_ff2_up", "w_ff2_down"]
    result = [loss, dx0.reshape(nb, SEQ, D)]
    for kind in ("grad", "delta", "new_m", "new_v"):
        result += [outs[kind][n] for n in order]
    return tuple(result)
```

```python
import functools
import math

import jax
import jax.numpy as jnp
from jax import lax
from jax.experimental import pallas as pl
from jax.experimental.pallas import tpu as pltpu

D = 1024
SEQ = 2048
HD = 64
NH = 8
WG = NH * HD
DFF = 2752
DFF_PAD = 2816
IN_MAIN = 6 * WG
IN_COLS = IN_MAIN + NH
N_SHARD = 4
N_DEV = 8
LANE = 128
QB = 128
FB = 256
FT = 512
BAND_UNROLL = 4
BAND_UNROLL_BWD = 4
PATTERNS = ((1, 16), (4, 4), (16, 1))
ROPE_THETA = 500000.0
EPS = 1e-6
NEG = -1e30
ATTN_SCALE = HD ** -0.5
TM = 512
TM_BWD = 256
VMEM_LIMIT = 56 * 1024 * 1024

ADAM_LR, ADAM_B1, ADAM_B2, ADAM_EPS, ADAM_WD, ADAM_STEP = 0.001, 0.9, 0.999, 1e-08, 0.01, 10

F32 = jnp.float32
BF16 = jnp.bfloat16
MESH = pl.DeviceIdType.MESH
SDS = jax.ShapeDtypeStruct


def _cp(*sem):
    return pltpu.CompilerParams(dimension_semantics=sem, vmem_limit_bytes=VMEM_LIMIT)


def _dot(a, b):
    return jnp.dot(a, b, preferred_element_type=F32)


def _dot_nt(a, b):
    return lax.dot_general(a, b, (((1,), (1,)), ((), ())), preferred_element_type=F32)


def _dot_tn(a, b):
    return lax.dot_general(a, b, (((0,), (0,)), ((), ())), preferred_element_type=F32)


def _rms(xf):
    return lax.rsqrt(jnp.mean(xf * xf, axis=-1, keepdims=True) + EPS)


def _norm_mod_bwd(dh, xf, g, scale):
    r = _rms(xf)
    xh = xf * r
    dsh = jnp.sum(dh, axis=0, keepdims=True)
    dsc = jnp.sum(dh * (xh * g), axis=0, keepdims=True)
    dn = dh * (1.0 + scale)
    dg = jnp.sum(dn * xh, axis=0, keepdims=True)
    dxh = dn * g
    dx = r * (dxh - xh * jnp.mean(dxh * xh, axis=-1, keepdims=True))
    return dx, dsh, dsc, dg


def _post_bwd(dxo, y0, g, mgate, gs):
    r = _rms(y0)
    yh = y0 * r
    dmg = gs * jnp.sum(dxo * (yh * g), axis=0, keepdims=True)
    dy = (gs * mgate) * dxo
    dg = jnp.sum(dy * yh, axis=0, keepdims=True)
    dyh = dy * g
    dy0 = r * (dyh - yh * jnp.mean(dyh * yh, axis=-1, keepdims=True))
    return dy0, dmg, dg


def _mod_map(i, *_):
    return ((i * TM) // SEQ, 0, 0)


FF_TN = 1408
FF_NJ = DFF_PAD // FF_TN


def ffn_fwd(x, mod3, g_pre, g_post, wg, wu, wd, gs, name, gather=None):
    T = x.shape[0]
    ng = 0 if gather is None else len(gather[0])
    plan = None if gather is None else ShardGather([w.shape for w in gather[0]], gather[1])

    def body(*refs):
        x_ref, mod_ref, gpre_ref, gpost_ref, wg_ref, wu_ref, wd_ref = refs[:7]
        xo_ref, h_ref, gate_ref, up_ref, y0_ref = refs[7 + ng:12 + ng]
        hs, acc = refs[12 + 2 * ng:14 + 2 * ng]
        i = pl.program_id(0)
        j = pl.program_id(1)
        if plan is not None:
            comm = (refs[7:7 + ng], refs[12 + ng:12 + 2 * ng], refs[14 + 2 * ng:])
            pl.when((i == 0) & (j == 0))(lambda: plan.start(*comm))

        @pl.when(j == 0)
        def _():
            xf = x_ref[...]
            h = (xf * _rms(xf) * gpre_ref[...]) * (1.0 + mod_ref[0, 1:2, :]) + mod_ref[0, 0:1, :]
            hb = h.astype(BF16)
            hs[...] = hb
            h_ref[...] = hb
            acc[...] = jnp.zeros_like(acc)

        hb = hs[...]
        gate = _dot(hb, wg_ref[...])
        up = _dot(hb, wu_ref[...])
        gate_ref[...] = gate.astype(BF16)
        up_ref[...] = up.astype(BF16)
        act = gate * jax.nn.sigmoid(gate) * up
        acc[...] += _dot(act.astype(BF16), wd_ref[...])

        @pl.when(j == FF_NJ - 1)
        def _():
            y0 = acc[...]
            y0_ref[...] = y0
            xo_ref[...] = x_ref[...] + (gs * mod_ref[0, 2:3, :]) * (y0 * _rms(y0) * gpost_ref[...])

        if plan is not None:
            pl.when((i == T // TM - 1) & (j == FF_NJ - 1))(lambda: plan.finish(*comm))

    tok = pl.BlockSpec((TM, D), lambda i, j: (i, 0))
    vec = pl.BlockSpec((1, D), lambda i, j: (0, 0))
    hid = pl.BlockSpec((TM, FF_TN), lambda i, j: (i, j))
    outs = pl.pallas_call(
        body, grid=(T // TM, FF_NJ),
        in_specs=[tok, pl.BlockSpec((1, 3, D), _mod_map), vec, vec,
                  pl.BlockSpec((D, FF_TN), lambda i, j: (0, j)), pl.BlockSpec((D, FF_TN), lambda i, j: (0, j)),
                  pl.BlockSpec((FF_TN, D), lambda i, j: (j, 0))] + [HBM] * ng,
        out_specs=[tok, tok, hid, hid, tok] + [HBM] * ng,
        out_shape=[SDS((T, D), F32), SDS((T, D), BF16), SDS((T, DFF_PAD), BF16), SDS((T, DFF_PAD), BF16), SDS((T, D), F32)]
        + ([] if plan is None else plan.out_shapes(BF16)),
        scratch_shapes=[pltpu.VMEM((TM, D), BF16), pltpu.VMEM((TM, D), F32)] + ([] if plan is None else plan.scratch()),
        compiler_params=_cp("arbitrary", "arbitrary"), name=name,
    )(x, mod3, g_pre, g_post, wg, wu, wd, *([] if gather is None else gather[0]))
    return outs[:5], outs[5:]


def ffn_bwd(dxo, x, y0, mod3, g_pre, g_post, gate, up, wg, wu, wd, gs, name, scatter=None):
    T = x.shape[0]
    nb = T // SEQ
    tm = TM_BWD
    tiles_per_seq = SEQ // tm
    ns = 0 if scatter is None else len(scatter)

    def body(*refs):
        dxo_ref, x_ref, y0_ref, mod_ref, gpre_ref, gpost_ref, gate_ref, up_ref, wg_ref, wu_ref, wd_ref = refs[:11]
        dx_ref, dy0_ref, act_ref, dgate_ref, dup_ref, dmod_ref, dgpre_ref, dgpost_ref = refs[11 + ns:19 + ns]
        dy0s, acc = refs[19 + 2 * ns:21 + 2 * ns]
        i = pl.program_id(0)
        j = pl.program_id(1)
        if ns:
            comm = (refs[11:11 + ns], refs[19 + ns:19 + 2 * ns], *refs[21 + 2 * ns:])

            @pl.when((i == 0) & (j == 0))
            def _():
                for cp in _scatter_copies(*comm):
                    cp.start()

        @pl.when((i == 0) & (j == 0))
        def _():
            dgpre_ref[...] = jnp.zeros_like(dgpre_ref)
            dgpost_ref[...] = jnp.zeros_like(dgpost_ref)

        @pl.when((i % tiles_per_seq == 0) & (j == 0))
        def _():
            dmod_ref[...] = jnp.zeros_like(dmod_ref)

        @pl.when(j == 0)
        def _():
            dy0, dmg, dg = _post_bwd(dxo_ref[...], y0_ref[...], gpost_ref[...], mod_ref[0, 2:3, :], gs)
            dmod_ref[0, 2:3, :] += dmg
            dgpost_ref[...] += dg
            db = dy0.astype(BF16)
            dy0s[...] = db
            dy0_ref[...] = db
            acc[...] = jnp.zeros_like(acc)

        dact = _dot_nt(dy0s[...], wd_ref[...])
        g = gate_ref[...].astype(F32)
        u = up_ref[...].astype(F32)
        sig = jax.nn.sigmoid(g)
        sl = g * sig
        dgate = (dact * u * (sig * (1.0 + g * (1.0 - sig)))).astype(BF16)
        dup = (dact * sl).astype(BF16)
        act_ref[...] = (sl * u).astype(BF16)
        dgate_ref[...] = dgate
        dup_ref[...] = dup
        acc[...] += _dot_nt(dgate, wg_ref[...]) + _dot_nt(dup, wu_ref[...])

        @pl.when(j == FF_NJ - 1)
        def _():
            dx, dsh, dsc, dg = _norm_mod_bwd(acc[...], x_ref[...], gpre_ref[...], mod_ref[0, 1:2, :])
            dx_ref[...] = dxo_ref[...] + dx
            dmod_ref[0, 0:1, :] += dsh
            dmod_ref[0, 1:2, :] += dsc
            dgpre_ref[...] += dg

        if ns:
            @pl.when((i == T // tm - 1) & (j == FF_NJ - 1))
            def _():
                for cp in _scatter_copies(*comm):
                    cp.wait()

    tok = pl.BlockSpec((tm, D), lambda i, j: (i, 0))
    vec = pl.BlockSpec((1, D), lambda i, j: (0, 0))
    hid = pl.BlockSpec((tm, FF_TN), lambda i, j: (i, j))
    modspec = pl.BlockSpec((1, 3, D), lambda i, j: ((i * tm) // SEQ, 0, 0))
    outs = pl.pallas_call(
        body, grid=(T // tm, FF_NJ),
        in_specs=[tok, tok, tok, modspec, vec, vec, hid, hid,
                  pl.BlockSpec((D, FF_TN), lambda i, j: (0, j)), pl.BlockSpec((D, FF_TN), lambda i, j: (0, j)),
                  pl.BlockSpec((FF_TN, D), lambda i, j: (j, 0))] + [HBM] * ns,
        out_specs=[tok, tok, hid, hid, hid, modspec, vec, vec] + [HBM] * ns,
        out_shape=[SDS((T, D), F32), SDS((T, D), BF16), SDS((T, DFF_PAD), BF16), SDS((T, DFF_PAD), BF16),
                   SDS((T, DFF_PAD), BF16), SDS((nb, 3, D), F32), SDS((1, D), F32), SDS((1, D), F32)]
        + [SDS((3,) + h.shape[1:], h.dtype) for h in (scatter or [])],
        scratch_shapes=[pltpu.VMEM((tm, D), BF16), pltpu.VMEM((tm, D), F32)]
        + ([pltpu.SemaphoreType.DMA((ns, 3)), pltpu.SemaphoreType.DMA((ns, 3))] if ns else []),
        compiler_params=_cp("arbitrary", "arbitrary"), name=name,
    )(dxo, x, y0, mod3, g_pre, g_post, gate, up, wg, wu, wd, *(scatter or []))
    return outs[:8], outs[8:]


def matmul_tn(a, b, tm, tn, tk, name):
    T, M = a.shape
    N = b.shape[1]
    nk = T // tk

    def body(a_ref, b_ref, o_ref):
        @pl.when(pl.program_id(2) == 0)
        def _():
            o_ref[...] = jnp.zeros_like(o_ref)

        o_ref[...] += _dot_tn(a_ref[...], b_ref[...])

    return pl.pallas_call(
        body, grid=(M // tm, N // tn, nk),
        in_specs=[pl.BlockSpec((tk, tm), lambda i, j, k: (k, i)), pl.BlockSpec((tk, tn), lambda i, j, k: (k, j))],
        out_specs=pl.BlockSpec((tm, tn), lambda i, j, k: (i, j)),
        out_shape=SDS((M, N), F32),
        compiler_params=_cp("arbitrary", "arbitrary", "arbitrary"), name=name,
    )(a, b)


def loss_grad(y, tgt, name):
    T = y.shape[0]

    def body(y_ref, t_ref, dy_ref, l_ref):
        @pl.when(pl.program_id(0) == 0)
        def _():
            l_ref[...] = jnp.zeros_like(l_ref)

        e = y_ref[...] - t_ref[...]
        dy_ref[...] = e * (1.0 / D)
        l_ref[...] += jnp.sum(e * e) * (0.5 / D)

    tok = pl.BlockSpec((TM, D), lambda i: (i, 0))
    return pl.pallas_call(
        body, grid=(T // TM,), in_specs=[tok, tok],
        out_specs=[tok, pl.BlockSpec((8, LANE), lambda i: (0, 0))],
        out_shape=[SDS((T, D), F32), SDS((8, LANE), F32)],
        compiler_params=_cp("arbitrary"), name=name,
    )(y, tgt)


def rope_tables(pos_col, name):
    T = pos_col.shape[0]
    tm = 1024

    def body(p_ref, c_ref, s1_ref, s2_ref):
        lane = lax.broadcasted_iota(jnp.int32, (1, LANE), 1)
        l64 = lane % HD
        inv_freq = jnp.exp((l64 % 8).astype(F32) * (-math.log(ROPE_THETA) / 8.0))
        ang = p_ref[...].astype(F32) * inv_freq
        cs = jnp.cos(ang)
        sn = jnp.sin(ang)
        c_ref[...] = jnp.where(l64 < 16, cs, 1.0)
        s1_ref[...] = jnp.where(l64 < 8, -sn, 0.0)
        s2_ref[...] = jnp.where((l64 >= 8) & (l64 < 16), sn, 0.0)

    tab = pl.BlockSpec((tm, LANE), lambda i: (i, 0))
    return pl.pallas_call(
        body, grid=(T // tm,), in_specs=[pl.BlockSpec((tm, 1), lambda i: (i, 0))], out_specs=[tab, tab, tab],
        out_shape=[SDS((T, LANE), F32)] * 3, compiler_params=_cp("arbitrary"), name=name,
    )(pos_col)


def mixer_proj(x, mod3, g_pre, w_main, w_f, rc, rs1, rs2, name):
    T = x.shape[0]
    tn = 1024

    def body(x_ref, mod_ref, g_ref, w_ref, wf_ref, c_ref, s1_ref, s2_ref, h_ref, pa_ref, pb_ref, f_ref, hs):
        j = pl.program_id(1)

        @pl.when(j == 0)
        def _():
            xf = x_ref[...]
            h = (xf * _rms(xf) * g_ref[...]) * (1.0 + mod_ref[0, 1:2, :]) + mod_ref[0, 0:1, :]
            hb = h.astype(BF16)
            hs[...] = hb
            h_ref[...] = hb
            f_ref[...] = _dot(hb, wf_ref[...])

        pr = _dot(hs[...], w_ref[...])

        @pl.when(j == 0)
        def _():
            c, s1, s2 = c_ref[...], s1_ref[...], s2_ref[...]
            for k in range(tn // LANE):
                t = pr[:, k * LANE:(k + 1) * LANE]
                pa_ref[:, k * LANE:(k + 1) * LANE] = t * c + pltpu.roll(t, LANE - 8, 1) * s1 + pltpu.roll(t, 8, 1) * s2

        @pl.when(j == 1)
        def _():
            pa_ref[:, 2 * WG:3 * WG] = pr[:, :WG]
            pb_ref[:, 0:WG] = pr[:, WG:].astype(BF16)

        @pl.when(j == 2)
        def _():
            pb_ref[:, WG:3 * WG] = pr.astype(BF16)

    tok = pl.BlockSpec((TM, D), lambda i, j: (i, 0))
    vec = pl.BlockSpec((1, D), lambda i, j: (0, 0))
    tab = pl.BlockSpec((TM, LANE), lambda i, j: (i, 0))
    grp = pl.BlockSpec((TM, 3 * WG), lambda i, j: (i, 0))
    return pl.pallas_call(
        body, grid=(T // TM, IN_MAIN // tn),
        in_specs=[tok, pl.BlockSpec((1, 3, D), _mod_map), vec, pl.BlockSpec((D, tn), lambda i, j: (0, j)),
                  pl.BlockSpec((D, LANE), lambda i, j: (0, 0)), tab, tab, tab],
        out_specs=[tok, grp, grp, tab],
        out_shape=[SDS((T, D), BF16), SDS((T, 3 * WG), F32), SDS((T, 3 * WG), BF16), SDS((T, LANE), F32)],
        scratch_shapes=[pltpu.VMEM((TM, D), BF16)],
        compiler_params=_cp("arbitrary", "arbitrary"), name=name,
    )(x, mod3, g_pre, w_main, w_f, rc, rs1, rs2)


def _head_lanes():
    return lax.broadcasted_iota(jnp.int32, (1, LANE), 1) < HD


def _pair(m0, a, b):
    return jnp.where(m0, a, b)


def _band_rows(i, d, nbc):
    if nbc == 1:
        return i, i, 0
    r, mb = i // nbc, i % nbc
    return r + mb * (QB * d), r + jnp.maximum(mb - 1, 0) * (QB * d), jnp.where(mb > 0, QB, 0)


def _rows(start, size, d):
    return pl.ds(pl.multiple_of(start, QB), size) if d == 1 else pl.ds(start, size, stride=d)


def _band_valid(span, off):
    rq = lax.broadcasted_iota(jnp.int32, (QB, span), 0)
    rel = lax.broadcasted_iota(jnp.int32, (QB, span), 1) - off
    return (rel <= rq) & (rel >= rq - QB)


def band_fwd(pa, name):
    T = pa.shape[0]
    B = T // SEQ
    NP = WG // LANE

    def body(q_ref, k_ref, v_ref, out_ref, lse_ref, o_s, l_s):
        m0 = _head_lanes()
        for pidx, (d, nbc) in enumerate(PATTERNS):
            span = QB if nbc == 1 else 2 * QB

            def blk(it, carry, pidx=pidx, d=d, nbc=nbc, span=span):
                ld = []
                for u in range(BAND_UNROLL):
                    qs, ks, off = _band_rows(it * BAND_UNROLL + u, d, nbc)
                    q = q_ref[_rows(qs, QB, d), :] * ATTN_SCALE
                    ld.append((qs, q, k_ref[_rows(ks, span, d), :].astype(BF16), v_ref[_rows(ks, span, d), :].astype(BF16),
                               _band_valid(span, off)))
                ss = [[jnp.where(valid, _dot_nt(jnp.where(mh, q, 0.0).astype(BF16), k), NEG) for mh in (m0, jnp.logical_not(m0))]
                      for _, q, k, _, valid in ld]
                ps = []
                for pair in ss:
                    row = []
                    for s in pair:
                        m = jnp.max(s, axis=-1, keepdims=True)
                        p = jnp.exp(s - m)
                        row.append((p.astype(BF16), jnp.sum(p, axis=-1, keepdims=True), m))
                    ps.append(row)
                pv = [[_dot(p, ld[u][3]) for p, _, _ in ps[u]] for u in range(BAND_UNROLL)]
                for u in range(BAND_UNROLL):
                    rows = _rows(ld[u][0], QB, d)
                    (_, l0, mx0), (_, l1, mx1) = ps[u]
                    o_s[pidx, rows, :] = _pair(m0, pv[u][0] / l0, pv[u][1] / l1)
                    l_s[pidx, rows, :] = _pair(m0, mx0 + jnp.log(l0), mx1 + jnp.log(l1))
                return carry

            lax.fori_loop(0, SEQ // QB // BAND_UNROLL, blk, 0)
        for c in range(SEQ // FB):
            sl = slice(c * FB, (c + 1) * FB)
            a, b, e = l_s[0, sl, :], l_s[1, sl, :], l_s[2, sl, :]
            m = jnp.maximum(jnp.maximum(a, b), e)
            L = m + jnp.log(jnp.exp(a - m) + jnp.exp(b - m) + jnp.exp(e - m))
            out_ref[sl, :] = jnp.exp(a - L) * o_s[0, sl, :] + jnp.exp(b - L) * o_s[1, sl, :] + jnp.exp(e - L) * o_s[2, sl, :]
            lse_ref[sl, :] = L

    blk_of = lambda g: pl.BlockSpec((SEQ, LANE), lambda b, hp, g=g: (b, g * NP + hp))
    return pl.pallas_call(
        body, grid=(B, NP), in_specs=[blk_of(0), blk_of(1), blk_of(2)], out_specs=[blk_of(0), blk_of(0)],
        out_shape=[SDS((T, WG), F32), SDS((T, WG), F32)],
        scratch_shapes=[pltpu.VMEM((3, SEQ, LANE), F32), pltpu.VMEM((3, SEQ, LANE), F32)],
        compiler_params=_cp("arbitrary", "arbitrary"), name=name,
    )(pa, pa, pa)


def _pair_rowsum(m0, prod):
    s0 = jnp.sum(jnp.where(m0, prod, 0.0), axis=-1, keepdims=True)
    return _pair(m0, s0, jnp.sum(prod, axis=-1, keepdims=True) - s0)


def band_bwd(pa, do, out, lse, name):
    T = pa.shape[0]
    B = T // SEQ
    NP = WG // LANE

    def body(q_ref, k_ref, v_ref, do_ref, out_ref, l_ref, dq_ref, dk_ref, dv_ref, d_s):
        m0 = _head_lanes()
        dq_ref[...] = jnp.zeros_like(dq_ref)
        dk_ref[...] = jnp.zeros_like(dk_ref)
        dv_ref[...] = jnp.zeros_like(dv_ref)
        for c in range(SEQ // FB):
            sl = slice(c * FB, (c + 1) * FB)
            d_s[sl, :] = _pair_rowsum(m0, do_ref[sl, :] * out_ref[sl, :])
        for d, nbc in PATTERNS:
            span = QB if nbc == 1 else 2 * QB

            def blk(it, carry, d=d, nbc=nbc, span=span):
                masks = (m0, jnp.logical_not(m0))
                ld = []
                for u in range(BAND_UNROLL_BWD):
                    qs, ks, off = _band_rows(it * BAND_UNROLL_BWD + u, d, nbc)
                    qrow, krow = _rows(qs, QB, d), _rows(ks, span, d)
                    ld.append(dict(qrow=qrow, krow=krow, q=q_ref[qrow, :] * ATTN_SCALE, k=k_ref[krow, :].astype(BF16),
                                   v=v_ref[krow, :].astype(BF16), do=do_ref[qrow, :], l=l_ref[qrow, :], dv=d_s[qrow, :],
                                   valid=_band_valid(span, off)))
                for t in ld:
                    t["qm"] = [jnp.where(mh, t["q"], 0.0).astype(BF16) for mh in masks]
                    t["dom"] = [jnp.where(mh, t["do"], 0.0).astype(BF16) for mh in masks]
                sd = [[(jnp.where(t["valid"], _dot_nt(t["qm"][h], t["k"]), NEG), _dot_nt(t["dom"][h], t["v"])) for h in range(2)]
                      for t in ld]
                pd = []
                for t, pair in zip(ld, sd):
                    row = []
                    for h, (s, dp) in enumerate(pair):
                        col = slice(h * HD, h * HD + 1)
                        p = jnp.exp(s - t["l"][:, col])
                        row.append((p.astype(BF16), (p * (dp - t["dv"][:, col])).astype(BF16)))
                    pd.append(row)
                gr = [(_dot(row[0][1], t["k"]), _dot(row[1][1], t["k"]),
                       _dot_tn(jnp.concatenate([row[0][1], row[1][1]], axis=0), jnp.concatenate(t["qm"], axis=0)),
                       _dot_tn(jnp.concatenate([row[0][0], row[1][0]], axis=0), jnp.concatenate(t["dom"], axis=0)))
                      for t, row in zip(ld, pd)]
                for t, (dq0, dq1, dk, dv) in zip(ld, gr):
                    dq_ref[t["qrow"], :] += _pair(m0, dq0, dq1) * ATTN_SCALE
                    dk_ref[t["krow"], :] += dk
                    dv_ref[t["krow"], :] += dv
                return carry

            lax.fori_loop(0, SEQ // QB // BAND_UNROLL_BWD, blk, 0)

    blk_of = lambda g: pl.BlockSpec((SEQ, LANE), lambda b, hp, g=g: (b, g * NP + hp))
    return pl.pallas_call(
        body, grid=(B, NP), in_specs=[blk_of(0), blk_of(1), blk_of(2), blk_of(0), blk_of(0), blk_of(0)],
        out_specs=[blk_of(0)] * 3, out_shape=[SDS((T, WG), F32)] * 3,
        scratch_shapes=[pltpu.VMEM((SEQ, LANE), F32)],
        compiler_params=_cp("arbitrary", "arbitrary"), name=name,
    )(pa, pa, pa, do, out, lse)


def _tile_causal(nq, nk, q0, k0):
    r = lax.broadcasted_iota(jnp.int32, (nq, nk), 0)
    c = lax.broadcasted_iota(jnp.int32, (nq, nk), 1)
    return r + (q0 - k0) >= c


def fox_fwd(pb, fcol, frow, name):
    T = pb.shape[0]
    B = T // SEQ
    NP = WG // LANE
    n = SEQ // FB

    def body(q_ref, k_ref, v_ref, fc_ref, fr_ref, o_ref, lse_ref):
        i = pl.program_id(2)
        m0 = _head_lanes()
        q = q_ref[...] * ATTN_SCALE
        zero = jnp.zeros_like(q)
        qh = (jnp.where(m0, q, zero), jnp.where(m0, zero, q))
        fq = (fc_ref[0, 0], fc_ref[0, 1])

        def step(t, carry, masked):
            rows = pl.ds(pl.multiple_of(t * FT, FT), FT)
            kt = k_ref[rows, :]
            vt = v_ref[rows, :]
            ss = [_dot_nt(qh[h], kt) + fq[h] - fr_ref[0, h, t] for h in range(2)]
            if masked:
                ok = _tile_causal(FB, FT, i * FB, t * FT)
                ss = [jnp.where(ok, s, NEG) for s in ss]
            st = []
            for h in range(2):
                m, l, _ = carry[h]
                m2 = jnp.maximum(m, jnp.max(ss[h], axis=-1, keepdims=True))
                a = jnp.exp(m - m2)
                p = jnp.exp(ss[h] - m2)
                st.append((m2, a, a * l + jnp.sum(p, axis=-1, keepdims=True), p.astype(BF16)))
            pv = [_dot(st[h][3], vt) for h in range(2)]
            return tuple((st[h][0], st[h][2], st[h][1] * carry[h][2] + pv[h]) for h in range(2))

        one = (jnp.full((FB, 1), NEG, F32), jnp.zeros((FB, 1), F32), jnp.zeros((FB, LANE), F32))
        last = (i * FB) // FT
        carry = lax.fori_loop(0, last, lambda t, cr: step(t, cr, False), (one, one))
        (ma, la, acca), (mb, lb, accb) = step(last, carry, True)
        o_ref[...] = _pair(m0, acca / la, accb / lb)
        lse_ref[...] = _pair(m0, ma + jnp.log(la), mb + jnp.log(lb))

    qblk = pl.BlockSpec((FB, LANE), lambda b, hp, i: (b * n + i, hp))
    full = lambda g: pl.BlockSpec((SEQ, LANE), lambda b, hp, i, g=g: (b, g * NP + hp))
    return pl.pallas_call(
        body, grid=(B, NP, n),
        in_specs=[qblk, full(1), full(2), pl.BlockSpec((1, 2, FB, 1), lambda b, hp, i: (b, hp, i, 0)),
                  pl.BlockSpec((1, 2, SEQ // FT, 1, FT), lambda b, hp, i: (b, hp, 0, 0, 0))],
        out_specs=[qblk, qblk], out_shape=[SDS((T, WG), F32), SDS((T, WG), F32)],
        compiler_params=_cp("arbitrary", "arbitrary", "arbitrary"), name=name,
    )(pb, pb, pb, fcol, frow)


def fox_bwd(pb, do, lrow, drow, fcol, frow, name):
    T = pb.shape[0]
    B = T // SEQ
    NP = WG // LANE
    n = SEQ // FB

    def body(q_ref, k_ref, v_ref, do_ref, l_ref, d_ref, fc_ref, fr_ref, dq_ref, dk_ref, dv_ref, dfq_ref, dfk_ref):
        j = pl.program_id(2)
        m0 = _head_lanes()
        masks = (m0, jnp.logical_not(m0))

        @pl.when(j == 0)
        def _():
            dq_ref[...] = jnp.zeros_like(dq_ref)
            dfq_ref[...] = jnp.zeros_like(dfq_ref)

        kj = k_ref[...]
        vj = v_ref[...]
        fk = (fc_ref[0, 0], fc_ref[0, 1])

        def step(t, carry, masked):
            rows = pl.ds(pl.multiple_of(t * FT, FT), FT)
            qt = q_ref[rows, :] * ATTN_SCALE
            dot_ = do_ref[rows, :]
            zero = jnp.zeros_like(qt)
            qm = [jnp.where(mh, qt, zero) for mh in masks]
            dom = [jnp.where(mh, dot_, 0.0).astype(BF16) for mh in masks]
            ss = [_dot_nt(kj, qm[h]) + fr_ref[0, h, t] - fk[h] for h in range(2)]
            dps = [_dot_nt(vj, dom[h]) for h in range(2)]
            if masked:
                key = lax.broadcasted_iota(jnp.int32, (FB, FT), 0)
                qry = lax.broadcasted_iota(jnp.int32, (FB, FT), 1)
                ok = qry + (t * FT - j * FB) >= key
                ss = [jnp.where(ok, s, NEG) for s in ss]
            pds = []
            for h in range(2):
                p = jnp.exp(ss[h] - l_ref[0, h, t])
                ds = p * (dps[h] - d_ref[0, h, t])
                dfq_ref[0, h, t] += jnp.sum(ds, axis=0, keepdims=True)
                pds.append((p.astype(BF16), ds.astype(BF16), jnp.sum(ds, axis=-1, keepdims=True)))
            dks = [_dot(pds[h][1], qm[h]) for h in range(2)]
            dvs = [_dot(pds[h][0], dom[h]) for h in range(2)]
            dqs = [_dot_tn(pds[h][1], kj) for h in range(2)]
            dq_ref[rows, :] += _pair(m0, dqs[0], dqs[1]) * ATTN_SCALE
            return tuple((carry[h][0] + dks[h], carry[h][1] + dvs[h], carry[h][2] - pds[h][2]) for h in range(2))

        one = (jnp.zeros((FB, LANE), F32), jnp.zeros((FB, LANE), F32), jnp.zeros((FB, 1), F32))
        first = (j * FB) // FT
        carry = step(first, (one, one), True)
        (dka, dva, dfka), (dkb, dvb, dfkb) = lax.fori_loop(first + 1, SEQ // FT, lambda t, cr: step(t, cr, False), carry)
        dk_ref[...] = _pair(m0, dka, dkb)
        dv_ref[...] = _pair(m0, dva, dvb)
        dfk_ref[0, 0] = dfka
        dfk_ref[0, 1] = dfkb

    kblk = lambda g: pl.BlockSpec((FB, LANE), lambda b, hp, j, g=g: (b * n + j, g * NP + hp))
    full = pl.BlockSpec((SEQ, LANE), lambda b, hp, j: (b, hp))
    rowf = pl.BlockSpec((1, 2, SEQ // FT, 1, FT), lambda b, hp, j: (b, hp, 0, 0, 0))
    colb = pl.BlockSpec((1, 2, FB, 1), lambda b, hp, j: (b, hp, j, 0))
    return pl.pallas_call(
        body, grid=(B, NP, n), in_specs=[full, kblk(1), kblk(2), full, rowf, rowf, colb, rowf],
        out_specs=[full, kblk(0), kblk(0), rowf, colb],
        out_shape=[SDS((T, WG), F32), SDS((T, WG), F32), SDS((T, WG), F32), SDS((B, NH, SEQ // FT, 1, FT), F32),
                   SDS((B, NH, SEQ, 1), F32)],
        compiler_params=_cp("arbitrary", "arbitrary", "arbitrary"), name=name,
    )(pb, pb, pb, do, lrow, drow, fcol, frow)


def _tri(lower):
    r = lax.broadcasted_iota(jnp.int32, (LANE, LANE), 0)
    c = lax.broadcasted_iota(jnp.int32, (LANE, LANE), 1)
    return ((r >= c) if lower else (r <= c)).astype(F32)


def _tri_dot(t, xblk):
    return jnp.dot(t, xblk, precision=lax.Precision.HIGHEST, preferred_element_type=F32)


def forget_cumsum(flog, bias, name):
    B, S, _ = flog.shape

    def body(f_ref, b_ref, o_ref):
        t = _tri(True)
        carry = jnp.zeros((1, LANE), F32)
        for blk in range(S // LANE):
            z = f_ref[0, blk * LANE:(blk + 1) * LANE, :] + b_ref[...]
            lf = jnp.minimum(z, 0.0) - jnp.log(1.0 + jnp.exp(-jnp.abs(z)))
            cs = _tri_dot(t, lf) + carry
            o_ref[0, blk * LANE:(blk + 1) * LANE, :] = cs
            carry = cs[LANE - 1:LANE, :]

    spec = pl.BlockSpec((1, S, LANE), lambda b: (b, 0, 0))
    return pl.pallas_call(
        body, grid=(B,), in_specs=[spec, pl.BlockSpec((1, LANE), lambda b: (0, 0))], out_specs=spec,
        out_shape=SDS((B, S, LANE), F32), compiler_params=_cp("arbitrary"), name=name,
    )(flog, bias)


def forget_cumsum_bwd(dF, flog, bias, name):
    B, S, _ = flog.shape

    def body(d_ref, f_ref, b_ref, o_ref, db_ref):
        @pl.when(pl.program_id(0) == 0)
        def _():
            db_ref[...] = jnp.zeros_like(db_ref)

        t = _tri(False)
        carry = jnp.zeros((1, LANE), F32)
        tot = jnp.zeros((1, LANE), F32)
        for blk in reversed(range(S // LANE)):
            sl = slice(blk * LANE, (blk + 1) * LANE)
            rc = _tri_dot(t, d_ref[0, sl, :]) + carry
            carry = rc[0:1, :]
            z = f_ref[0, sl, :] + b_ref[...]
            dz = rc * jax.nn.sigmoid(-z)
            o_ref[0, sl, :] = dz
            tot = tot + jnp.sum(dz, axis=0, keepdims=True)
        db_ref[...] += tot

    spec = pl.BlockSpec((1, S, LANE), lambda b: (b, 0, 0))
    vec = pl.BlockSpec((1, LANE), lambda b: (0, 0))
    return pl.pallas_call(
        body, grid=(B,), in_specs=[spec, spec, vec], out_specs=[spec, vec],
        out_shape=[SDS((B, S, LANE), F32), SDS((1, LANE), F32)], compiler_params=_cp("arbitrary"), name=name,
    )(dF, flog, bias)


def mixer_out_fwd(oa, ob, goa, gob, w_out, g_post, x, mod3, name):
    T = x.shape[0]

    def body(oa_ref, ob_ref, goa_ref, gob_ref, w_ref, gp_ref, x_ref, mod_ref, xo_ref, mg_ref, y0_ref):
        a = oa_ref[...]
        b = ob_ref[...]
        mg = jnp.concatenate([a * _rms(a) * goa_ref[...], b * _rms(b) * gob_ref[...]], axis=-1).astype(BF16)
        mg_ref[...] = mg
        y0 = _dot(mg, w_ref[...])
        y0_ref[...] = y0
        xo_ref[...] = x_ref[...] + mod_ref[0, 2:3, :] * (y0 * _rms(y0) * gp_ref[...])

    tok = pl.BlockSpec((TM, D), lambda i: (i, 0))
    half = pl.BlockSpec((TM, WG), lambda i: (i, 0))
    hv = pl.BlockSpec((1, WG), lambda i: (0, 0))
    return pl.pallas_call(
        body, grid=(T // TM,),
        in_specs=[half, half, hv, hv, pl.BlockSpec((D, D), lambda i: (0, 0)), pl.BlockSpec((1, D), lambda i: (0, 0)), tok,
                  pl.BlockSpec((1, 3, D), _mod_map)],
        out_specs=[tok, tok, tok], out_shape=[SDS((T, D), F32), SDS((T, D), BF16), SDS((T, D), F32)],
        compiler_params=_cp("arbitrary"), name=name,
    )(oa, ob, goa, gob, w_out, g_post, x, mod3)


def mixer_out_bwd(dxo, y0, mod3, g_post, w_out, oa, ob, goa, gob, name):
    T = dxo.shape[0]
    nb = T // SEQ
    tiles_per_seq = SEQ // TM

    def body(dxo_ref, y0_ref, mod_ref, gp_ref, w_ref, oa_ref, ob_ref, goa_ref, gob_ref,
             dy0_ref, doa_ref, dob_ref, dmg_ref, dgp_ref, dgoa_ref, dgob_ref, dvb_ref):
        i = pl.program_id(0)

        @pl.when(i == 0)
        def _():
            dgp_ref[...] = jnp.zeros_like(dgp_ref)
            dgoa_ref[...] = jnp.zeros_like(dgoa_ref)
            dgob_ref[...] = jnp.zeros_like(dgob_ref)

        @pl.when(i % tiles_per_seq == 0)
        def _():
            dmg_ref[...] = jnp.zeros_like(dmg_ref)

        dy0, dmg, dg = _post_bwd(dxo_ref[...], y0_ref[...], gp_ref[...], mod_ref[0, 2:3, :], 1.0)
        dmg_ref[0] += dmg
        dgp_ref[...] += dg
        db = dy0.astype(BF16)
        dy0_ref[...] = db
        dm = _dot_nt(db, w_ref[...])
        for o_ref, g_ref, do_ref, dg_ref, sl in ((oa_ref, goa_ref, doa_ref, dgoa_ref, slice(0, WG)),
                                                  (ob_ref, gob_ref, dob_ref, dgob_ref, slice(WG, 2 * WG))):
            o = o_ref[...]
            r = _rms(o)
            oh = o * r
            d = dm[:, sl]
            dg_ref[...] += jnp.sum(d * oh, axis=0, keepdims=True)
            dh = d * g_ref[...]
            do = r * (dh - oh * jnp.mean(dh * oh, axis=-1, keepdims=True))
            do_ref[...] = do
        ind = (lax.broadcasted_iota(jnp.int32, (WG, LANE), 0) // HD == lax.broadcasted_iota(jnp.int32, (WG, LANE), 1)).astype(F32)
        dvb_ref[...] = jnp.dot(do * o, ind, precision=lax.Precision.HIGHEST, preferred_element_type=F32)

    tok = pl.BlockSpec((TM, D), lambda i: (i, 0))
    half = pl.BlockSpec((TM, WG), lambda i: (i, 0))
    hv = pl.BlockSpec((1, WG), lambda i: (0, 0))
    vec = pl.BlockSpec((1, D), lambda i: (0, 0))
    return pl.pallas_call(
        body, grid=(T // TM,),
        in_specs=[tok, tok, pl.BlockSpec((1, 3, D), _mod_map), vec, pl.BlockSpec((D, D), lambda i: (0, 0)), half, half, hv, hv],
        out_specs=[tok, half, half, pl.BlockSpec((1, 1, D), _mod_map), vec, hv, hv, pl.BlockSpec((TM, LANE), lambda i: (i, 0))],
        out_shape=[SDS((T, D), BF16), SDS((T, WG), F32), SDS((T, WG), F32), SDS((nb, 1, D), F32), SDS((1, D), F32),
                   SDS((1, WG), F32), SDS((1, WG), F32), SDS((T, LANE), F32)],
        compiler_params=_cp("arbitrary"), name=name,
    )(dxo, y0, mod3, g_post, w_out, oa, ob, goa, gob)


def proj_grad_assemble(grads, rc, rs1, rs2, name):
    T = grads[0].shape[0]

    def body(*refs):
        ins, (c_ref, s1_ref, s2_ref, o_ref) = refs[:6], refs[6:]
        c, s1, s2 = c_ref[...], s1_ref[...], s2_ref[...]
        for grp in range(2):
            for k in range(WG // LANE):
                d = ins[grp][:, k * LANE:(k + 1) * LANE]
                un = d * c + pltpu.roll(d * s1, 8, 1) + pltpu.roll(d * s2, LANE - 8, 1)
                o_ref[:, grp * WG + k * LANE:grp * WG + (k + 1) * LANE] = un.astype(BF16)
        for g in range(2, 6):
            o_ref[:, g * WG:(g + 1) * WG] = ins[g][...].astype(BF16)

    half = pl.BlockSpec((TM, WG), lambda i: (i, 0))
    tab = pl.BlockSpec((TM, LANE), lambda i: (i, 0))
    return pl.pallas_call(
        body, grid=(T // TM,), in_specs=[half] * 6 + [tab] * 3, out_specs=pl.BlockSpec((TM, IN_MAIN), lambda i: (i, 0)),
        out_shape=SDS((T, IN_MAIN), BF16), compiler_params=_cp("arbitrary"), name=name,
    )(*grads, rc, rs1, rs2)


def mixer_proj_bwd(dproj, dflog, dxo, x, mod3, g_pre, w_main, w_f, name):
    T = x.shape[0]
    nb = T // SEQ
    tiles_per_seq = SEQ // TM
    tk = 1024
    nj = IN_MAIN // tk

    def body(dp_ref, df_ref, dxo_ref, x_ref, mod_ref, g_ref, w_ref, wf_ref, dx_ref, dmod_ref, dg_ref, acc):
        i = pl.program_id(0)
        j = pl.program_id(1)

        @pl.when((i == 0) & (j == 0))
        def _():
            dg_ref[...] = jnp.zeros_like(dg_ref)

        @pl.when((i % tiles_per_seq == 0) & (j == 0))
        def _():
            dmod_ref[...] = jnp.zeros_like(dmod_ref)

        @pl.when(j == 0)
        def _():
            acc[...] = _dot_nt(df_ref[...].astype(BF16), wf_ref[...])

        acc[...] += _dot_nt(dp_ref[...], w_ref[...])

        @pl.when(j == nj - 1)
        def _():
            dx, dsh, dsc, dg = _norm_mod_bwd(acc[...], x_ref[...], g_ref[...], mod_ref[0, 1:2, :])
            dx_ref[...] = dxo_ref[...] + dx
            dmod_ref[0, 0:1, :] += dsh
            dmod_ref[0, 1:2, :] += dsc
            dg_ref[...] += dg

    tok = pl.BlockSpec((TM, D), lambda i, j: (i, 0))
    vec = pl.BlockSpec((1, D), lambda i, j: (0, 0))
    return pl.pallas_call(
        body, grid=(T // TM, nj),
        in_specs=[pl.BlockSpec((TM, tk), lambda i, j: (i, j)), pl.BlockSpec((TM, LANE), lambda i, j: (i, 0)), tok, tok,
                  pl.BlockSpec((1, 3, D), _mod_map), vec, pl.BlockSpec((D, tk), lambda i, j: (0, j)),
                  pl.BlockSpec((D, LANE), lambda i, j: (0, 0))],
        out_specs=[tok, pl.BlockSpec((1, 2, D), _mod_map), vec],
        out_shape=[SDS((T, D), F32), SDS((nb, 2, D), F32), SDS((1, D), F32)],
        scratch_shapes=[pltpu.VMEM((TM, D), F32)],
        compiler_params=_cp("arbitrary", "arbitrary"), name=name,
    )(dproj, dflog, dxo, x, mod3, g_pre, w_main, w_f)


def ada_fwd(c_all, w, b, name):
    n = w.shape[1]
    tn = n // 2

    def body(c_ref, w_ref, b_ref, o_ref):
        cv = c_ref[...]
        o_ref[...] = _dot((cv * jax.nn.sigmoid(cv)).astype(BF16), w_ref[...].astype(BF16)) + b_ref[...]

    R = c_all.shape[0]
    return pl.pallas_call(
        body, grid=(2,),
        in_specs=[pl.BlockSpec((R, D), lambda j: (0, 0)), pl.BlockSpec((D, tn), lambda j: (0, j)), pl.BlockSpec((1, tn), lambda j: (0, j))],
        out_specs=pl.BlockSpec((R, tn), lambda j: (0, j)), out_shape=SDS((R, n), F32),
        compiler_params=_cp("arbitrary"), name=name,
    )(c_all, w, b)


def ada_bwd(c_all, dmod, name):
    R, n = dmod.shape
    tn = n // 2

    def body(c_ref, d_ref, o_ref):
        cv = c_ref[...]
        o_ref[...] = _dot_tn((cv * jax.nn.sigmoid(cv)).astype(BF16), d_ref[...].astype(BF16))

    return pl.pallas_call(
        body, grid=(2,), in_specs=[pl.BlockSpec((R, D), lambda j: (0, 0)), pl.BlockSpec((R, tn), lambda j: (0, j))],
        out_specs=pl.BlockSpec((D, tn), lambda j: (0, j)), out_shape=SDS((D, n), F32),
        compiler_params=_cp("arbitrary"), name=name,
    )(c_all, dmod)


def _adam_math(w, g, m, v):
    m2 = ADAM_B1 * m + (1.0 - ADAM_B1) * g
    v2 = ADAM_B2 * v + (1.0 - ADAM_B2) * (g * g)
    m_hat = m2 / (1.0 - ADAM_B1 ** ADAM_STEP)
    v_hat = v2 / (1.0 - ADAM_B2 ** ADAM_STEP)
    delta = -ADAM_LR * (m_hat / (jnp.sqrt(v_hat) + ADAM_EPS) + ADAM_WD * w)
    return delta, m2, v2


def adam_update(w, g, m, v, tr, name):
    R, C = w.shape

    def body(w_ref, g_ref, m_ref, v_ref, d_ref, mo_ref, vo_ref):
        d_ref[...], mo_ref[...], vo_ref[...] = _adam_math(w_ref[...], g_ref[...], m_ref[...], v_ref[...])

    spec = pl.BlockSpec((tr, C), lambda i: (i, 0))
    return pl.pallas_call(
        body, grid=(R // tr,), in_specs=[spec] * 4, out_specs=[spec] * 3, out_shape=[SDS((R, C), F32)] * 3,
        compiler_params=_cp("arbitrary"), name=name,
    )(w, g, m, v)


def vec_adam(parts, w, m, v, name):
    P, C = parts.shape

    def body(p_ref, w_ref, m_ref, v_ref, g_ref, d_ref, mo_ref, vo_ref):
        g = jnp.sum(p_ref[...], axis=0, keepdims=True)
        g_ref[...] = g
        d_ref[...], mo_ref[...], vo_ref[...] = _adam_math(w_ref[...], g, m_ref[...], v_ref[...])

    return pl.pallas_call(body, out_shape=[SDS((1, C), F32)] * 4, compiler_params=_cp(), name=name)(parts, w, m, v)


HBM = pl.BlockSpec(memory_space=pltpu.HBM)
VMEM = pl.BlockSpec(memory_space=pltpu.VMEM)


def _place():
    x, y, c = lax.axis_index("x"), lax.axis_index("y"), lax.axis_index("c")
    return x, y, c, [(1 - x, y), (x, 1 - y), (1 - x, 1 - y)]


def all_gather8(xs, name):
    R, C = xs.shape

    def body(x_ref, out_ref, send_sems, recv_sems, local_sem):
        x, y, c, chips = _place()
        me, sibling = (x, y, c), (x, y, 1 - c)

        def slot(px, py, pc):
            return out_ref.at[4 * px + 2 * py + pc]

        def copy(k, block, to, src=None):
            return pltpu.make_async_remote_copy(
                src_ref=slot(*block) if src is None else src, dst_ref=slot(*block),
                send_sem=send_sems.at[k], recv_sem=recv_sems.at[k], device_id=to, device_id_type=MESH)

        mine = pltpu.make_async_copy(x_ref, slot(*me), local_sem)
        mine.start()
        first = [copy(0, me, sibling, src=x_ref)]
        first += [copy(1 + j, me, (*chip, c), src=x_ref) for j, chip in enumerate(chips)]
        for cp in first:
            cp.start()
        passed = [copy(4 + j, (*chip, c), sibling) for j, chip in enumerate(chips)]
        for j, chip in enumerate(chips):
            copy(1 + j, (*chip, c), me).wait_recv()
            passed[j].start()
        copy(0, sibling, me).wait_recv()
        for j, chip in enumerate(chips):
            copy(4 + j, (*chip, 1 - c), me).wait_recv()
        for cp in first + passed:
            cp.wait_send()
        mine.wait()

    return pl.pallas_call(
        body, out_shape=SDS((N_DEV, R, C), xs.dtype), in_specs=[VMEM], out_specs=VMEM,
        scratch_shapes=[pltpu.SemaphoreType.DMA((7,)), pltpu.SemaphoreType.DMA((7,)), pltpu.SemaphoreType.DMA],
        compiler_params=pltpu.CompilerParams(vmem_limit_bytes=VMEM_LIMIT), name=name,
    )(xs)


class ShardGather:
    def __init__(self, shapes, splits):
        self.shapes, self.splits, self.n = shapes, splits, len(shapes)

    def scratch(self):
        n = self.n
        return [pltpu.SemaphoreType.DMA((n, 6)), pltpu.SemaphoreType.DMA((n, 6)), pltpu.SemaphoreType.DMA((n,))]

    def out_shapes(self, dtype):
        return [SDS((N_SHARD,) + tuple(s), dtype) for s in self.shapes]

    def _half(self, ref, k, cc):
        lo, hi = (0, self.splits[k]) if cc == 0 else (self.splits[k], self.shapes[k][0])
        return ref.at[pl.ds(lo, hi - lo)]

    def _phase(self, w_refs, o_refs, sems, finish):
        send_sems, recv_sems, local_sems = sems
        x, y, c, chips = _place()
        sibling = (x, y, 1 - c)
        me_s = 2 * x + y

        def rcopy(src, dst, k, s, to):
            return pltpu.make_async_remote_copy(src_ref=src, dst_ref=dst, send_sem=send_sems.at[k, s],
                                                recv_sem=recv_sems.at[k, s], device_id=to, device_id_type=MESH)

        for cc in (0, 1):
            @pl.when(c == cc)
            def _():
                local = [pltpu.make_async_copy(w_refs[k], o_refs[k].at[me_s], local_sems.at[k]) for k in range(self.n)]
                first = [rcopy(self._half(w_refs[k], k, cc), self._half(o_refs[k].at[me_s], k, cc), k, j, (*chip, c))
                         for k in range(self.n) for j, chip in enumerate(chips)]
                if not finish:
                    for cp in local + first:
                        cp.start()
                    return
                passed = []
                for k in range(self.n):
                    for j, chip in enumerate(chips):
                        land = self._half(o_refs[k].at[2 * chip[0] + chip[1]], k, cc)
                        rcopy(land, land, k, j, (*chip, c)).wait_recv()
                        f = rcopy(land, land, k, 3 + j, sibling)
                        f.start()
                        passed.append(f)
                for k in range(self.n):
                    for j, chip in enumerate(chips):
                        other = self._half(o_refs[k].at[2 * chip[0] + chip[1]], k, 1 - cc)
                        rcopy(other, other, k, 3 + j, sibling).wait_recv()
                for s in first + passed:
                    s.wait_send()
                for cp in local:
                    cp.wait()

    def start(self, w_refs, o_refs, sems):
        self._phase(w_refs, o_refs, sems, False)

    def finish(self, w_refs, o_refs, sems):
        self._phase(w_refs, o_refs, sems, True)


def all_gather_shards(ws, splits, name):
    n = len(ws)
    plan = ShardGather([w.shape for w in ws], splits)

    def body(*refs):
        plan.start(refs[:n], refs[n:2 * n], refs[2 * n:])
        plan.finish(refs[:n], refs[n:2 * n], refs[2 * n:])

    return pl.pallas_call(
        body, out_shape=plan.out_shapes(ws[0].dtype), in_specs=[HBM] * n, out_specs=[HBM] * n,
        scratch_shapes=plan.scratch(), name=name,
    )(*ws)


def sibling_send_half(gs, name):
    n = len(gs)

    def body(*refs):
        g_refs, o_refs = refs[:n], refs[n:2 * n]
        send_sems, recv_sems = refs[2 * n:]
        x, y, c, _ = _place()
        cps = []
        for k in range(n):
            hr = gs[k].shape[1] // 2
            src = g_refs[k].at[:, pl.ds(pl.multiple_of((1 - c) * hr, 8), hr)]
            cp = pltpu.make_async_remote_copy(src_ref=src, dst_ref=o_refs[k], send_sem=send_sems.at[k], recv_sem=recv_sems.at[k],
                                              device_id=(x, y, 1 - c), device_id_type=MESH)
            cp.start()
            cps.append(cp)
        for cp in cps:
            cp.wait()

    return pl.pallas_call(
        body, out_shape=[SDS((N_SHARD, g.shape[1] // 2, g.shape[2]), g.dtype) for g in gs], in_specs=[HBM] * n, out_specs=[HBM] * n,
        scratch_shapes=[pltpu.SemaphoreType.DMA((n,)), pltpu.SemaphoreType.DMA((n,))], name=name,
    )(*gs)


def _scatter_copies(h_refs, o_refs, send_sems, recv_sems):
    _, _, c, chips = _place()
    return [pltpu.make_async_remote_copy(
        src_ref=h_refs[k].at[2 * chip[0] + chip[1]], dst_ref=o_refs[k].at[j], send_sem=send_sems.at[k, j],
        recv_sem=recv_sems.at[k, j], device_id=(*chip, c), device_id_type=MESH)
        for k in range(len(h_refs)) for j, chip in enumerate(chips)]


def chip_scatter(hs, name):
    n = len(hs)

    def body(*refs):
        cps = _scatter_copies(refs[:n], refs[n:2 * n], *refs[2 * n:])
        for cp in cps:
            cp.start()
        for cp in cps:
            cp.wait()

    return pl.pallas_call(
        body, out_shape=[SDS((3,) + h.shape[1:], h.dtype) for h in hs], in_specs=[HBM] * n, out_specs=[HBM] * n,
        scratch_shapes=[pltpu.SemaphoreType.DMA((n, 3)), pltpu.SemaphoreType.DMA((n, 3))], name=name,
    )(*hs)


def sibling_swap(ghs, name):
    n = len(ghs)

    def body(*refs):
        g_refs, o_refs = refs[:n], refs[n:2 * n]
        send_sems, recv_sems = refs[2 * n:]
        x, y, c, _ = _place()
        cps = []
        for k in range(n):
            cp = pltpu.make_async_remote_copy(src_ref=g_refs[k], dst_ref=o_refs[k], send_sem=send_sems.at[k],
                                              recv_sem=recv_sems.at[k], device_id=(x, y, 1 - c), device_id_type=MESH)
            cp.start()
            cps.append(cp)
        for cp in cps:
            cp.wait()

    return pl.pallas_call(
        body, out_shape=[SDS(g.shape, g.dtype) for g in ghs], in_specs=[HBM] * n, out_specs=[HBM] * n,
        scratch_shapes=[pltpu.SemaphoreType.DMA((n,)), pltpu.SemaphoreType.DMA((n,))], name=name,
    )(*ghs)


def pair_sum(g, ra, cidx, name):
    _, r, cols = g.shape
    hr = r // 2

    def body(c_ref, g_ref, a_ref, o_ref):
        o_ref[...] = (g_ref[...] + a_ref[...]).astype(BF16)

    return pl.pallas_call(
        body,
        grid_spec=pltpu.PrefetchScalarGridSpec(
            num_scalar_prefetch=1, grid=(N_SHARD,),
            in_specs=[pl.BlockSpec((1, hr, cols), lambda s, c_ref: (s, c_ref[0], 0)),
                      pl.BlockSpec((1, hr, cols), lambda s, c_ref: (s, 0, 0))],
            out_specs=pl.BlockSpec((1, hr, cols), lambda s, c_ref: (s, 0, 0))),
        out_shape=SDS((N_SHARD, hr, cols), BF16), compiler_params=_cp("arbitrary"), name=name,
    )(cidx, g, ra)


def chip_sum(h, rb, sidx, name):
    _, hr, cols = h.shape

    def body(s_ref, h_ref, r_ref, o_ref):
        o_ref[...] = ((h_ref[0].astype(F32) + r_ref[0].astype(F32)) + r_ref[1].astype(F32)) + r_ref[2].astype(F32)

    return pl.pallas_call(
        body,
        grid_spec=pltpu.PrefetchScalarGridSpec(
            num_scalar_prefetch=1, grid=(1,),
            in_specs=[pl.BlockSpec((1, hr, cols), lambda i, s_ref: (s_ref[0], 0, 0)),
                      pl.BlockSpec((3, hr, cols), lambda i, s_ref: (0, 0, 0))],
            out_specs=pl.BlockSpec((hr, cols), lambda i, s_ref: (0, 0))),
        out_shape=SDS((hr, cols), F32), compiler_params=_cp("arbitrary"), name=name,
    )(sidx, h, rb)


def _shard_cols(g, n_valid):
    r = g.shape[0]
    return g[:, :n_valid].reshape(r, N_SHARD, n_valid // N_SHARD).transpose(1, 0, 2)


def _unshard_cols(o, pad_to):
    _, r, n = o.shape
    full = o.transpose(1, 0, 2).reshape(r, N_SHARD * n)
    return jnp.pad(full, ((0, 0), (0, pad_to - N_SHARD * n)))


def _rows_of_tiles(t):
    B, H, S = t.shape
    return t.reshape(B, H, S // FT, 1, FT)


def mixer_fwd(x1, mod3, g_pre, w_main, w_f, b_forget_pad, goa, gob, w_out, g_post, tabs, nb):
    hmix, pa, pb, flog = mixer_proj(x1, mod3, g_pre, w_main, w_f, *tabs, name="mixer_proj")
    out_a, lse_a = band_fwd(pa, name="band_fwd")
    F = forget_cumsum(flog.reshape(nb, SEQ, LANE), b_forget_pad, name="forget_cumsum")
    Fh = F[:, :, :NH].transpose(0, 2, 1)
    fcol = Fh.reshape(nb, NH, SEQ, 1)
    frow = _rows_of_tiles(Fh)
    out_b, lse_b = fox_fwd(pb, fcol, frow, name="fox_fwd")
    x2, merged, y0m = mixer_out_fwd(out_a, out_b, goa, gob, w_out, g_post, x1, mod3, name="mixer_out_fwd")
    res = dict(hmix=hmix, flog=flog, pa=pa, pb=pb, out_a=out_a, lse_a=lse_a, fcol=fcol, frow=frow, out_b=out_b, lse_b=lse_b,
               merged=merged, y0m=y0m)
    return x2, res


def mixer_bwd(dx2, x1, mod3, g_pre, w_main, w_f, b_forget_pad, goa, gob, w_out, g_post, tabs, res, nb):
    T = nb * SEQ
    dy0m, doa, dob, dmgate, dg_post, dgoa, dgob, dvec_b = mixer_out_bwd(
        dx2, res["y0m"], mod3, g_post, w_out, res["out_a"], res["out_b"], goa, gob, name="mixer_out_bwd")
    dqa, dka, dva = band_bwd(res["pa"], doa, res["out_a"], res["lse_a"], name="band_bwd")
    lrow = _rows_of_tiles(res["lse_b"].reshape(nb, SEQ, NH, HD)[:, :, :, 0].transpose(0, 2, 1))
    drow = _rows_of_tiles(dvec_b[:, :NH].reshape(nb, SEQ, NH).transpose(0, 2, 1))
    dqb, dkb, dvb, dfq, dfk = fox_bwd(res["pb"], dob, lrow, drow, res["fcol"], res["frow"], name="fox_bwd")
    dF = (dfq.reshape(nb, NH, SEQ) + dfk.reshape(nb, NH, SEQ)).transpose(0, 2, 1)
    dF = jnp.pad(dF, ((0, 0), (0, 0), (0, LANE - NH)))
    dflog, dbf = forget_cumsum_bwd(dF, res["flog"].reshape(nb, SEQ, LANE), b_forget_pad, name="forget_cumsum_bwd")
    dflog = dflog.reshape(T, LANE)
    dproj = proj_grad_assemble((dqa, dka, dva, dqb, dkb, dvb), *tabs, name="proj_grad_assemble")
    dx1, dmod2, dg_pre = mixer_proj_bwd(dproj, dflog, dx2, x1, mod3, g_pre, w_main, w_f, name="mixer_proj_bwd")
    g_main = matmul_tn(res["hmix"], dproj, D, 1024, 1024, name="grad_w_in")
    g_f = matmul_tn(res["hmix"], dflog.astype(BF16), D, LANE, 1024, name="grad_w_forget")
    g_out = matmul_tn(res["merged"], dy0m, D, D, 1024, name="grad_w_out")
    dmod3 = jnp.concatenate([dmod2, dmgate], axis=1)
    return dx1, dmod3, dict(g_pre=dg_pre, g_post=dg_post, goa=dgoa, gob=dgob, b_forget=dbf[:, :NH],
                            w_in=jnp.concatenate([g_main, g_f[:, :NH]], axis=1), w_out=g_out)


def ffn_grads(h, dy0, act, dgate, dup, pre):
    g_gate = matmul_tn(h, dgate, D, FF_TN, 1024, name=pre + "_grad_gate")
    g_up = matmul_tn(h, dup, D, FF_TN, 1024, name=pre + "_grad_up")
    g_down = matmul_tn(act, dy0, FF_TN, D, 1024, name=pre + "_grad_down")
    return g_gate, g_up, g_down


def local_step(x0, tgt, pos_col, mod, wfull, p, late_weights=None, early_grads=None):
    T = x0.shape[0]
    nb = T // SEQ
    mod_ff1, mod_mix, mod_ff2 = mod[:, 0:3], mod[:, 3:6], mod[:, 6:9]
    tabs = rope_tables(pos_col, name="rope_tables")
    bf_pad = jnp.pad(p["b_forget"], ((0, 0), (0, LANE - NH)))

    (x1, h1, gate1, up1, y01), gathered = ffn_fwd(
        x0, mod_ff1, p["g_pre_ff1"], p["g_post_ff1"], wfull["w_ff1_gate"], wfull["w_ff1_up"], wfull["w_ff1_down"], 0.5,
        name="ff1_fwd", gather=None if late_weights is None else late_weights[:2])
    if late_weights is not None:
        wfull = {**wfull, **late_weights[2](gathered)}
    x2, res = mixer_fwd(x1, mod_mix, p["g_pre_mix"], wfull["w_main"], wfull["w_f"], bf_pad, p["g_out_a"], p["g_out_b"],
                        wfull["w_out"], p["g_post_mix"], tabs, nb)
    (x3, h2, gate2, up2, y02), _ = ffn_fwd(x2, mod_ff2, p["g_pre_ff2"], p["g_post_ff2"], wfull["w_ff2_gate"],
                                           wfull["w_ff2_up"], wfull["w_ff2_down"], 0.5, name="ff2_fwd")

    dx3, loss_part = loss_grad(x3, tgt, name="loss_grad")
    (dx2, dy02, act2, dgate2, dup2, dmod_ff2, dgpre2, dgpost2), _ = ffn_bwd(
        dx3, x2, y02, mod_ff2, p["g_pre_ff2"], p["g_post_ff2"], gate2, up2, wfull["w_ff2_gate"], wfull["w_ff2_up"],
        wfull["w_ff2_down"], 0.5, name="ff2_bwd")
    gw = {}
    gw["w_ff2_gate"], gw["w_ff2_up"], gw["w_ff2_down"] = ffn_grads(h2, dy02, act2, dgate2, dup2, "ff2")
    dx1, dmod_mix, gmix = mixer_bwd(dx2, x1, mod_mix, p["g_pre_mix"], wfull["w_main"], wfull["w_f"], bf_pad, p["g_out_a"],
                                    p["g_out_b"], wfull["w_out"], p["g_post_mix"], tabs, res, nb)
    gw["w_in"], gw["w_out"] = gmix["w_in"], gmix["w_out"]
    (dx0, dy01, act1, dgate1, dup1, dmod_ff1, dgpre1, dgpost1), scattered = ffn_bwd(
        dx1, x0, y01, mod_ff1, p["g_pre_ff1"], p["g_post_ff1"], gate1, up1, wfull["w_ff1_gate"], wfull["w_ff1_up"],
        wfull["w_ff1_down"], 0.5, name="ff1_bwd", scatter=None if early_grads is None else early_grads(gw))
    gw["w_ff1_gate"], gw["w_ff1_up"], gw["w_ff1_down"] = ffn_grads(h1, dy01, act1, dgate1, dup1, "ff1")
    dmod = jnp.concatenate([dmod_ff1, dmod_mix, dmod_ff2], axis=1).reshape(nb, 9 * D)
    small = dict(g_pre_ff1=dgpre1, g_post_ff1=dgpost1, g_pre_mix=gmix["g_pre"], g_post_mix=gmix["g_post"], g_pre_ff2=dgpre2,
                 g_post_ff2=dgpost2, g_out_a=gmix["goa"], g_out_b=gmix["gob"], b_forget=gmix["b_forget"])
    return loss_part, dx0, dmod, gw, small, scattered


def kernel(x, c, positions, w_ada, b_ada, g_pre_ff1, g_post_ff1, w_ff1_gate, w_ff1_up, w_ff1_down, g_pre_mix, g_post_mix, w_in, b_forget, g_out_a, g_out_b, w_out, g_pre_ff2, g_post_ff2, w_ff2_gate, w_ff2_up, w_ff2_down, loss_target, m_w_ada, m_b_ada, m_g_pre_ff1, m_g_post_ff1, m_w_ff1_gate, m_w_ff1_up, m_w_ff1_down, m_g_pre_mix, m_g_post_mix, m_w_in, m_b_forget, m_g_out_a, m_g_out_b, m_w_out, m_g_pre_ff2, m_g_post_ff2, m_w_ff2_gate, m_w_ff2_up, m_w_ff2_down, v_w_ada, v_b_ada, v_g_pre_ff1, v_g_post_ff1, v_w_ff1_gate, v_w_ff1_up, v_w_ff1_down, v_g_pre_mix, v_g_post_mix, v_w_in, v_b_forget, v_g_out_a, v_g_out_b, v_w_out, v_g_pre_ff2, v_g_post_ff2, v_w_ff2_gate, v_w_ff2_up, v_w_ff2_down):
    args = dict(locals())
    nb = x.shape[0]
    T = nb * SEQ
    ax, ay, ac = lax.axis_index("x"), lax.axis_index("y"), lax.axis_index("c")
    shard = 2 * ax + ay
    cidx = jnp.reshape(ac, (1,)).astype(jnp.int32)
    sidx = jnp.reshape(shard, (1,)).astype(jnp.int32)

    big = ["w_ff1_gate", "w_ff1_up", "w_ff1_down", "w_in", "w_out", "w_ff2_gate", "w_ff2_up", "w_ff2_down"]
    vecs = ["g_pre_ff1", "g_post_ff1", "g_pre_mix", "g_post_mix", "g_pre_ff2", "g_post_ff2"]

    first, late = big[:3], big[3:]
    splits = dict(zip(big, [512, 512, 352, 512, 128, 512, 512, 352]))

    def assemble(names, gathered):
        out = {}
        for n, o in zip(names, gathered):
            if n.endswith("gate") or n.endswith("up"):
                out[n] = _unshard_cols(o, DFF_PAD)
            elif n.endswith("down"):
                out[n] = jnp.pad(o.reshape(DFF, D), ((0, DFF_PAD - DFF), (0, 0)))
            elif n == "w_in":
                full = _unshard_cols(o, IN_COLS)
                out["w_main"] = full[:, :IN_MAIN]
                out["w_f"] = jnp.pad(full[:, IN_MAIN:], ((0, 0), (0, LANE - NH)))
            else:
                out[n] = o.reshape(D, D)
        return out

    wfull = assemble(first, all_gather_shards([args[n][0].astype(BF16) for n in first], [splits[n] for n in first],
                                              name="all_gather_weights"))
    late_weights = ([args[n][0].astype(BF16) for n in late], [splits[n] for n in late], functools.partial(assemble, late))

    ncol = w_ada.shape[2]
    c_all = all_gather8(c, name="all_gather_c").reshape(N_DEV * nb, D)
    b_loc = lax.dynamic_slice(b_ada, (0, shard * ncol), (1, ncol))
    mod_loc = ada_fwd(c_all, w_ada[0], b_loc, name="ada_fwd")
    mod_g = all_gather8(mod_loc, name="all_gather_mod")
    row0 = (4 * ax + 2 * ay + ac) * nb
    mod_rows = lax.dynamic_slice(mod_g, (0, row0, 0), (N_DEV, nb, ncol))
    mod = jnp.concatenate([mod_rows[2 * s] for s in range(N_SHARD)], axis=-1).reshape(nb, 9, D)

    small_in = dict(g_pre_ff1=g_pre_ff1, g_post_ff1=g_post_ff1, g_pre_mix=g_pre_mix, g_post_mix=g_post_mix, g_pre_ff2=g_pre_ff2,
                    g_post_ff2=g_post_ff2, g_out_a=g_out_a, g_out_b=g_out_b, b_forget=b_forget)
    def shard_blocked(n, g):
        if n.endswith("gate") or n.endswith("up"):
            return _shard_cols(g, DFF)
        if n.endswith("down"):
            return g[:DFF].reshape(N_SHARD, DFF // N_SHARD, D)
        if n == "w_in":
            return _shard_cols(g, IN_COLS)
        return g.reshape(N_SHARD, D // N_SHARD, D)

    def chip_sums(names, gw, tag):
        gsb = [shard_blocked(n, gw[n]) for n in names]
        ras = sibling_send_half(gsb, name="grad_sibling_send_" + tag)
        return [pair_sum(g, ra, cidx, name=f"grad_pair_sum_{n}") for n, g, ra in zip(names, gsb, ras)]

    hs = {}

    def early_grads(gw):
        hs.update(zip(late, chip_sums(late, gw, "late")))
        return [hs[n] for n in late]

    loss_part, dx0, dmod, gw, small, rbs_late = local_step(
        x.reshape(T, D), loss_target.reshape(T, D), positions.reshape(T, 1), mod, wfull, small_in, late_weights, early_grads)

    dmod_all = all_gather8(dmod, name="all_gather_dmod").reshape(N_DEV * nb, 9 * D)
    dmod_loc = lax.dynamic_slice(dmod_all, (0, shard * ncol), (N_DEV * nb, ncol))
    g_w_ada = ada_bwd(c_all, dmod_loc, name="ada_bwd")

    hs.update(zip(first, chip_sums(first, gw, "first")))
    rbs = dict(zip(late, rbs_late))
    rbs.update(zip(first, chip_scatter([hs[n] for n in first], name="grad_chip_scatter")))
    ghs = [chip_sum(hs[n], rbs[n], sidx, name=f"grad_chip_sum_{n}") for n in big]
    theirs = sibling_swap(ghs, name="grad_sibling_swap")
    gfull = {n: jnp.concatenate([jnp.where(ac == 0, mine, other), jnp.where(ac == 0, other, mine)], axis=0)
             for n, mine, other in zip(big, ghs, theirs)}
    gfull["w_ada"] = g_w_ada

    row6 = jnp.concatenate([small["g_out_a"], small["g_out_b"]], axis=1)
    row7 = jnp.concatenate([small["b_forget"], loss_part[0:1, 0:1], jnp.zeros((1, D - NH - 1), F32)], axis=1)
    pack = jnp.concatenate([small[n] for n in vecs] + [row6, row7], axis=0)
    packed = all_gather8(pack, name="all_gather_small").reshape(N_DEV, 8 * D)

    def pack_state(pre):
        r6 = jnp.concatenate([args[pre + "g_out_a"], args[pre + "g_out_b"]], axis=1)
        r7 = jnp.pad(args[pre + "b_forget"], ((0, 0), (0, D - NH)))
        return jnp.concatenate([args[pre + n] for n in vecs] + [r6, r7], axis=0).reshape(1, 8 * D)

    sg, sd, sm, sv = (t.reshape(8, D) for t in vec_adam(packed, pack_state(""), pack_state("m_"), pack_state("v_"), name="adam_small"))

    def unpack(t):
        out = {n: t[i:i + 1] for i, n in enumerate(vecs)}
        out["g_out_a"], out["g_out_b"], out["b_forget"] = t[6:7, :WG], t[6:7, WG:], t[7:8, :NH]
        return out

    outs = dict(grad=unpack(sg), delta=unpack(sd), new_m=unpack(sm), new_v=unpack(sv))
    loss = sg[7, NH]
    outs["grad"]["b_ada"], outs["delta"]["b_ada"], outs["new_m"]["b_ada"], outs["new_v"]["b_ada"] = vec_adam(
        dmod_all, b_ada, m_b_ada, v_b_ada, name="adam_b_ada")

    for n in big + ["w_ada"]:
        g = gfull[n]
        rows = g.shape[0]
        tr = 128 if rows % 128 == 0 else 344
        d, m2, v2 = adam_update(args[n][0], g, args["m_" + n][0], args["v_" + n][0], tr, name="adam_" + n)
        outs["grad"][n], outs["delta"][n], outs["new_m"][n], outs["new_v"][n] = g[None], d[None], m2[None], v2[None]

    order = ["w_ada", "b_ada", "g_pre_ff1", "g_post_ff1", "w_ff1_gate", "w_ff1_up", "w_ff1_down", "g_pre_mix", "g_post_mix", "w_in",
             "b_forget", "g_out_a", "g_out_b", "w_out", "g_pre_ff2", "g_post_ff2", "w_ff2_gate", "w_ff2_up", "w_ff2_down"]
    result = [loss, dx0.reshape(nb, SEQ, D)]
    for kind in ("grad", "delta", "new_m", "new_v"):
        result += [outs[kind][n] for n in order]
    return tuple(result)
```

```python
import functools
import math

import jax
import jax.numpy as jnp
from jax import lax
from jax.experimental import pallas as pl
from jax.experimental.pallas import tpu as pltpu

D = 1024
SEQ = 2048
HD = 64
NH = 8
WG = NH * HD
DFF = 2752
DFF_PAD = 2816
IN_MAIN = 6 * WG
IN_COLS = IN_MAIN + NH
N_SHARD = 4
N_DEV = 8
LANE = 128
QB = 128
FB = 256
FT = 512
BAND_UNROLL = 4
BAND_UNROLL_BWD = 4
PATTERNS = ((1, 16), (4, 4), (16, 1))
ROPE_THETA = 500000.0
EPS = 1e-6
NEG = -1e30
ATTN_SCALE = HD ** -0.5
TM = 512
TM_BWD = 256
VMEM_LIMIT = 56 * 1024 * 1024

ADAM_LR, ADAM_B1, ADAM_B2, ADAM_EPS, ADAM_WD, ADAM_STEP = 0.001, 0.9, 0.999, 1e-08, 0.01, 10

F32 = jnp.float32
BF16 = jnp.bfloat16
MESH = pl.DeviceIdType.MESH
SDS = jax.ShapeDtypeStruct


def _cp(*sem):
    return pltpu.CompilerParams(dimension_semantics=sem, vmem_limit_bytes=VMEM_LIMIT)


def _dot(a, b):
    return jnp.dot(a, b, preferred_element_type=F32)


def _dot_nt(a, b):
    return lax.dot_general(a, b, (((1,), (1,)), ((), ())), preferred_element_type=F32)


def _dot_tn(a, b):
    return lax.dot_general(a, b, (((0,), (0,)), ((), ())), preferred_element_type=F32)


def _rms(xf):
    return lax.rsqrt(jnp.mean(xf * xf, axis=-1, keepdims=True) + EPS)


def _norm_mod_bwd(dh, xf, g, scale):
    r = _rms(xf)
    xh = xf * r
    dsh = jnp.sum(dh, axis=0, keepdims=True)
    dsc = jnp.sum(dh * (xh * g), axis=0, keepdims=True)
    dn = dh * (1.0 + scale)
    dg = jnp.sum(dn * xh, axis=0, keepdims=True)
    dxh = dn * g
    dx = r * (dxh - xh * jnp.mean(dxh * xh, axis=-1, keepdims=True))
    return dx, dsh, dsc, dg


def _post_bwd(dxo, y0, g, mgate, gs):
    r = _rms(y0)
    yh = y0 * r
    dmg = gs * jnp.sum(dxo * (yh * g), axis=0, keepdims=True)
    dy = (gs * mgate) * dxo
    dg = jnp.sum(dy * yh, axis=0, keepdims=True)
    dyh = dy * g
    dy0 = r * (dyh - yh * jnp.mean(dyh * yh, axis=-1, keepdims=True))
    return dy0, dmg, dg


def _mod_map(i, *_):
    return ((i * TM) // SEQ, 0, 0)


FF_TN = 1408
FF_NJ = DFF_PAD // FF_TN


def _resident_scratch():
    return [pltpu.VMEM((FF_NJ, D, FF_TN), BF16), pltpu.VMEM((FF_NJ, D, FF_TN), BF16), pltpu.VMEM((DFF_PAD, D), BF16),
            pltpu.SemaphoreType.DMA((3,))]


def _load_resident(first_step, srcs, dsts, sems):
    @pl.when(first_step)
    def _():
        cps = [pltpu.make_async_copy(s, d, sems.at[k]) for k, (s, d) in enumerate(zip(srcs, dsts))]
        for cp in cps:
            cp.start()
        for cp in cps:
            cp.wait()


def _tiles_of_cols(w):
    return w.reshape(D, FF_NJ, FF_TN).transpose(1, 0, 2)


def ffn_fwd(x, mod3, g_pre, g_post, wg, wu, wd, gs, name, gather=None):
    T = x.shape[0]
    ng = 0 if gather is None else len(gather[0])
    plan = None if gather is None else ShardGather([w.shape for w in gather[0]], gather[1])

    def body(*refs):
        x_ref, mod_ref, gpre_ref, gpost_ref = refs[:4]
        xo_ref, h_ref, gate_ref, up_ref, y0_ref = refs[7 + ng:12 + ng]
        hs, acc, wg_ref, wu_ref, wd_ref, wsem = refs[12 + 2 * ng:18 + 2 * ng]
        i = pl.program_id(0)
        j = pl.program_id(1)
        if plan is not None:
            comm = (refs[7:7 + ng], refs[12 + ng:12 + 2 * ng], refs[18 + 2 * ng:])
            pl.when((i == 0) & (j == 0))(lambda: plan.start(*comm))
        _load_resident((i == 0) & (j == 0), refs[4:7], (wg_ref, wu_ref, wd_ref), wsem)

        @pl.when(j == 0)
        def _():
            xf = x_ref[...]
            h = (xf * _rms(xf) * gpre_ref[...]) * (1.0 + mod_ref[0, 1:2, :]) + mod_ref[0, 0:1, :]
            hb = h.astype(BF16)
            hs[...] = hb
            h_ref[...] = hb
            acc[...] = jnp.zeros_like(acc)

        hb = hs[...]
        gate = _dot(hb, wg_ref[j])
        up = _dot(hb, wu_ref[j])
        gate_ref[...] = gate.astype(BF16)
        up_ref[...] = up.astype(BF16)
        act = gate * jax.nn.sigmoid(gate) * up
        acc[...] += _dot(act.astype(BF16), wd_ref[pl.ds(pl.multiple_of(j * FF_TN, FF_TN), FF_TN), :])

        @pl.when(j == FF_NJ - 1)
        def _():
            y0 = acc[...]
            y0_ref[...] = y0
            xo_ref[...] = x_ref[...] + (gs * mod_ref[0, 2:3, :]) * (y0 * _rms(y0) * gpost_ref[...])

        if plan is not None:
            pl.when((i == T // TM - 1) & (j == FF_NJ - 1))(lambda: plan.finish(*comm))

    tok = pl.BlockSpec((TM, D), lambda i, j: (i, 0))
    vec = pl.BlockSpec((1, D), lambda i, j: (0, 0))
    hid = pl.BlockSpec((TM, FF_TN), lambda i, j: (i, j))
    outs = pl.pallas_call(
        body, grid=(T // TM, FF_NJ),
        in_specs=[tok, pl.BlockSpec((1, 3, D), _mod_map), vec, vec, HBM, HBM, HBM] + [HBM] * ng,
        out_specs=[tok, tok, hid, hid, tok] + [HBM] * ng,
        out_shape=[SDS((T, D), F32), SDS((T, D), BF16), SDS((T, DFF_PAD), BF16), SDS((T, DFF_PAD), BF16), SDS((T, D), F32)]
        + ([] if plan is None else plan.out_shapes(BF16)),
        scratch_shapes=[pltpu.VMEM((TM, D), BF16), pltpu.VMEM((TM, D), F32)] + _resident_scratch()
        + ([] if plan is None else plan.scratch()),
        compiler_params=_cp("arbitrary", "arbitrary"), name=name,
    )(x, mod3, g_pre, g_post, wg, wu, wd, *([] if gather is None else gather[0]))
    return outs[:5], outs[5:]


def ffn_bwd(dxo, x, y0, mod3, g_pre, g_post, gate, up, wg, wu, wd, gs, name, scatter=None):
    T = x.shape[0]
    nb = T // SEQ
    tm = TM_BWD
    tiles_per_seq = SEQ // tm
    ns = 0 if scatter is None else len(scatter)

    def body(*refs):
        dxo_ref, x_ref, y0_ref, mod_ref, gpre_ref, gpost_ref, gate_ref, up_ref = refs[:8]
        dx_ref, dy0_ref, act_ref, dgate_ref, dup_ref, dmod_ref, dgpre_ref, dgpost_ref = refs[11 + ns:19 + ns]
        dy0s, acc, wg_ref, wu_ref, wd_ref, wsem = refs[19 + 2 * ns:25 + 2 * ns]
        i = pl.program_id(0)
        j = pl.program_id(1)
        _load_resident((i == 0) & (j == 0), refs[8:11], (wg_ref, wu_ref, wd_ref), wsem)
        if ns:
            comm = (refs[11:11 + ns], refs[19 + ns:19 + 2 * ns], *refs[25 + 2 * ns:])

            @pl.when((i == 0) & (j == 0))
            def _():
                for cp in _scatter_copies(*comm):
                    cp.start()

        @pl.when((i == 0) & (j == 0))
        def _():
            dgpre_ref[...] = jnp.zeros_like(dgpre_ref)
            dgpost_ref[...] = jnp.zeros_like(dgpost_ref)

        @pl.when((i % tiles_per_seq == 0) & (j == 0))
        def _():
            dmod_ref[...] = jnp.zeros_like(dmod_ref)

        @pl.when(j == 0)
        def _():
            dy0, dmg, dg = _post_bwd(dxo_ref[...], y0_ref[...], gpost_ref[...], mod_ref[0, 2:3, :], gs)
            dmod_ref[0, 2:3, :] += dmg
            dgpost_ref[...] += dg
            db = dy0.astype(BF16)
            dy0s[...] = db
            dy0_ref[...] = db
            acc[...] = jnp.zeros_like(acc)

        dact = _dot_nt(dy0s[...], wd_ref[pl.ds(pl.multiple_of(j * FF_TN, FF_TN), FF_TN), :])
        g = gate_ref[...].astype(F32)
        u = up_ref[...].astype(F32)
        sig = jax.nn.sigmoid(g)
        sl = g * sig
        dgate = (dact * u * (sig * (1.0 + g * (1.0 - sig)))).astype(BF16)
        dup = (dact * sl).astype(BF16)
        act_ref[...] = (sl * u).astype(BF16)
        dgate_ref[...] = dgate
        dup_ref[...] = dup
        acc[...] += _dot_nt(dgate, wg_ref[j]) + _dot_nt(dup, wu_ref[j])

        @pl.when(j == FF_NJ - 1)
        def _():
            dx, dsh, dsc, dg = _norm_mod_bwd(acc[...], x_ref[...], gpre_ref[...], mod_ref[0, 1:2, :])
            dx_ref[...] = dxo_ref[...] + dx
            dmod_ref[0, 0:1, :] += dsh
            dmod_ref[0, 1:2, :] += dsc
            dgpre_ref[...] += dg

        if ns:
            @pl.when((i == T // tm - 1) & (j == FF_NJ - 1))
            def _():
                for cp in _scatter_copies(*comm):
                    cp.wait()

    tok = pl.BlockSpec((tm, D), lambda i, j: (i, 0))
    vec = pl.BlockSpec((1, D), lambda i, j: (0, 0))
    hid = pl.BlockSpec((tm, FF_TN), lambda i, j: (i, j))
    modspec = pl.BlockSpec((1, 3, D), lambda i, j: ((i * tm) // SEQ, 0, 0))
    outs = pl.pallas_call(
        body, grid=(T // tm, FF_NJ),
        in_specs=[tok, tok, tok, modspec, vec, vec, hid, hid, HBM, HBM, HBM] + [HBM] * ns,
        out_specs=[tok, tok, hid, hid, hid, modspec, vec, vec] + [HBM] * ns,
        out_shape=[SDS((T, D), F32), SDS((T, D), BF16), SDS((T, DFF_PAD), BF16), SDS((T, DFF_PAD), BF16),
                   SDS((T, DFF_PAD), BF16), SDS((nb, 3, D), F32), SDS((1, D), F32), SDS((1, D), F32)]
        + [SDS((3,) + h.shape[1:], h.dtype) for h in (scatter or [])],
        scratch_shapes=[pltpu.VMEM((tm, D), BF16), pltpu.VMEM((tm, D), F32)] + _resident_scratch()
        + ([pltpu.SemaphoreType.DMA((ns, 3)), pltpu.SemaphoreType.DMA((ns, 3))] if ns else []),
        compiler_params=_cp("arbitrary", "arbitrary"), name=name,
    )(dxo, x, y0, mod3, g_pre, g_post, gate, up, wg, wu, wd, *(scatter or []))
    return outs[:8], outs[8:]


def matmul_tn(a, b, tm, tn, tk, name):
    T, M = a.shape
    N = b.shape[1]
    nk = T // tk

    def body(a_ref, b_ref, o_ref):
        @pl.when(pl.program_id(2) == 0)
        def _():
            o_ref[...] = jnp.zeros_like(o_ref)

        o_ref[...] += _dot_tn(a_ref[...], b_ref[...])

    return pl.pallas_call(
        body, grid=(M // tm, N // tn, nk),
        in_specs=[pl.BlockSpec((tk, tm), lambda i, j, k: (k, i)), pl.BlockSpec((tk, tn), lambda i, j, k: (k, j))],
        out_specs=pl.BlockSpec((tm, tn), lambda i, j, k: (i, j)),
        out_shape=SDS((M, N), F32),
        compiler_params=_cp("arbitrary", "arbitrary", "arbitrary"), name=name,
    )(a, b)


def loss_grad(y, tgt, name):
    T = y.shape[0]

    def body(y_ref, t_ref, dy_ref, l_ref):
        @pl.when(pl.program_id(0) == 0)
        def _():
            l_ref[...] = jnp.zeros_like(l_ref)

        e = y_ref[...] - t_ref[...]
        dy_ref[...] = e * (1.0 / D)
        l_ref[...] += jnp.sum(e * e) * (0.5 / D)

    tok = pl.BlockSpec((TM, D), lambda i: (i, 0))
    return pl.pallas_call(
        body, grid=(T // TM,), in_specs=[tok, tok],
        out_specs=[tok, pl.BlockSpec((8, LANE), lambda i: (0, 0))],
        out_shape=[SDS((T, D), F32), SDS((8, LANE), F32)],
        compiler_params=_cp("arbitrary"), name=name,
    )(y, tgt)


def rope_tables(pos_col, name):
    T = pos_col.shape[0]
    tm = 1024

    def body(p_ref, c_ref, s1_ref, s2_ref):
        lane = lax.broadcasted_iota(jnp.int32, (1, LANE), 1)
        l64 = lane % HD
        inv_freq = jnp.exp((l64 % 8).astype(F32) * (-math.log(ROPE_THETA) / 8.0))
        ang = p_ref[...].astype(F32) * inv_freq
        cs = jnp.cos(ang)
        sn = jnp.sin(ang)
        c_ref[...] = jnp.where(l64 < 16, cs, 1.0)
        s1_ref[...] = jnp.where(l64 < 8, -sn, 0.0)
        s2_ref[...] = jnp.where((l64 >= 8) & (l64 < 16), sn, 0.0)

    tab = pl.BlockSpec((tm, LANE), lambda i: (i, 0))
    return pl.pallas_call(
        body, grid=(T // tm,), in_specs=[pl.BlockSpec((tm, 1), lambda i: (i, 0))], out_specs=[tab, tab, tab],
        out_shape=[SDS((T, LANE), F32)] * 3, compiler_params=_cp("arbitrary"), name=name,
    )(pos_col)


def mixer_proj(x, mod3, g_pre, w_main, w_f, rc, rs1, rs2, name):
    T = x.shape[0]

    def body(x_ref, mod_ref, g_ref, w_ref, wf_ref, c_ref, s1_ref, s2_ref, h_ref, pa_ref, pb_ref, f_ref):
        xf = x_ref[...]
        h = (xf * _rms(xf) * g_ref[...]) * (1.0 + mod_ref[0, 1:2, :]) + mod_ref[0, 0:1, :]
        hb = h.astype(BF16)
        h_ref[...] = hb
        f_ref[...] = _dot(hb, wf_ref[...])
        c, s1, s2 = c_ref[...], s1_ref[...], s2_ref[...]
        for grp in range(2):
            pr = _dot(hb, w_ref[:, grp * WG:(grp + 1) * WG])
            for k in range(WG // LANE):
                t = pr[:, k * LANE:(k + 1) * LANE]
                pa_ref[:, grp * WG + k * LANE:grp * WG + (k + 1) * LANE] = (
                    t * c + pltpu.roll(t, LANE - 8, 1) * s1 + pltpu.roll(t, 8, 1) * s2)
        pa_ref[:, 2 * WG:3 * WG] = _dot(hb, w_ref[:, 2 * WG:3 * WG])
        for grp in range(3):
            pb_ref[:, grp * WG:(grp + 1) * WG] = _dot(hb, w_ref[:, (3 + grp) * WG:(4 + grp) * WG]).astype(BF16)

    tok = pl.BlockSpec((TM, D), lambda i: (i, 0))
    vec = pl.BlockSpec((1, D), lambda i: (0, 0))
    tab = pl.BlockSpec((TM, LANE), lambda i: (i, 0))
    grp3 = pl.BlockSpec((TM, 3 * WG), lambda i: (i, 0))
    return pl.pallas_call(
        body, grid=(T // TM,),
        in_specs=[tok, pl.BlockSpec((1, 3, D), _mod_map), vec, pl.BlockSpec((D, IN_MAIN), lambda i: (0, 0)),
                  pl.BlockSpec((D, LANE), lambda i: (0, 0)), tab, tab, tab],
        out_specs=[tok, grp3, grp3, tab],
        out_shape=[SDS((T, D), BF16), SDS((T, 3 * WG), F32), SDS((T, 3 * WG), BF16), SDS((T, LANE), F32)],
        compiler_params=_cp("arbitrary"), name=name,
    )(x, mod3, g_pre, w_main, w_f, rc, rs1, rs2)


def _head_lanes():
    return lax.broadcasted_iota(jnp.int32, (1, LANE), 1) < HD


def _pair(m0, a, b):
    return jnp.where(m0, a, b)


def _band_rows(i, d, nbc):
    if nbc == 1:
        return i, i, 0
    r, mb = i // nbc, i % nbc
    return r + mb * (QB * d), r + jnp.maximum(mb - 1, 0) * (QB * d), jnp.where(mb > 0, QB, 0)


def _rows(start, size, d):
    return pl.ds(pl.multiple_of(start, QB), size) if d == 1 else pl.ds(start, size, stride=d)


def _band_valid(span, off):
    rq = lax.broadcasted_iota(jnp.int32, (QB, span), 0)
    rel = lax.broadcasted_iota(jnp.int32, (QB, span), 1) - off
    return (rel <= rq) & (rel >= rq - QB)


def band_fwd(pa, name):
    T = pa.shape[0]
    B = T // SEQ
    NP = WG // LANE

    def body(q_ref, k_ref, v_ref, out_ref, lse_ref, o_s, l_s):
        m0 = _head_lanes()
        for pidx, (d, nbc) in enumerate(PATTERNS):
            span = QB if nbc == 1 else 2 * QB

            def blk(it, carry, pidx=pidx, d=d, nbc=nbc, span=span):
                ld = []
                for u in range(BAND_UNROLL):
                    qs, ks, off = _band_rows(it * BAND_UNROLL + u, d, nbc)
                    q = q_ref[_rows(qs, QB, d), :] * ATTN_SCALE
                    ld.append((qs, q, k_ref[_rows(ks, span, d), :].astype(BF16), v_ref[_rows(ks, span, d), :].astype(BF16),
                               _band_valid(span, off)))
                ss = [[jnp.where(valid, _dot_nt(jnp.where(mh, q, 0.0).astype(BF16), k), NEG) for mh in (m0, jnp.logical_not(m0))]
                      for _, q, k, _, valid in ld]
                ps = []
                for pair in ss:
                    row = []
                    for s in pair:
                        m = jnp.max(s, axis=-1, keepdims=True)
                        p = jnp.exp(s - m)
                        row.append((p.astype(BF16), jnp.sum(p, axis=-1, keepdims=True), m))
                    ps.append(row)
                pv = [[_dot(p, ld[u][3]) for p, _, _ in ps[u]] for u in range(BAND_UNROLL)]
                for u in range(BAND_UNROLL):
                    rows = _rows(ld[u][0], QB, d)
                    (_, l0, mx0), (_, l1, mx1) = ps[u]
                    o_s[pidx, rows, :] = _pair(m0, pv[u][0] / l0, pv[u][1] / l1)
                    l_s[pidx, rows, :] = _pair(m0, mx0 + jnp.log(l0), mx1 + jnp.log(l1))
                return carry

            lax.fori_loop(0, SEQ // QB // BAND_UNROLL, blk, 0)
        for c in range(SEQ // FB):
            sl = slice(c * FB, (c + 1) * FB)
            a, b, e = l_s[0, sl, :], l_s[1, sl, :], l_s[2, sl, :]
            m = jnp.maximum(jnp.maximum(a, b), e)
            L = m + jnp.log(jnp.exp(a - m) + jnp.exp(b - m) + jnp.exp(e - m))
            out_ref[sl, :] = jnp.exp(a - L) * o_s[0, sl, :] + jnp.exp(b - L) * o_s[1, sl, :] + jnp.exp(e - L) * o_s[2, sl, :]
            lse_ref[sl, :] = L

    blk_of = lambda g: pl.BlockSpec((SEQ, LANE), lambda b, hp, g=g: (b, g * NP + hp))
    return pl.pallas_call(
        body, grid=(B, NP), in_specs=[blk_of(0), blk_of(1), blk_of(2)], out_specs=[blk_of(0), blk_of(0)],
        out_shape=[SDS((T, WG), F32), SDS((T, WG), F32)],
        scratch_shapes=[pltpu.VMEM((3, SEQ, LANE), F32), pltpu.VMEM((3, SEQ, LANE), F32)],
        compiler_params=_cp("arbitrary", "arbitrary"), name=name,
    )(pa, pa, pa)


def _pair_rowsum(m0, prod):
    s0 = jnp.sum(jnp.where(m0, prod, 0.0), axis=-1, keepdims=True)
    return _pair(m0, s0, jnp.sum(prod, axis=-1, keepdims=True) - s0)


def band_bwd(pa, do, out, lse, name):
    T = pa.shape[0]
    B = T // SEQ
    NP = WG // LANE

    def body(q_ref, k_ref, v_ref, do_ref, out_ref, l_ref, dq_ref, dk_ref, dv_ref, d_s):
        m0 = _head_lanes()
        dq_ref[...] = jnp.zeros_like(dq_ref)
        dk_ref[...] = jnp.zeros_like(dk_ref)
        dv_ref[...] = jnp.zeros_like(dv_ref)
        for c in range(SEQ // FB):
            sl = slice(c * FB, (c + 1) * FB)
            d_s[sl, :] = _pair_rowsum(m0, do_ref[sl, :] * out_ref[sl, :])
        for d, nbc in PATTERNS:
            span = QB if nbc == 1 else 2 * QB

            def blk(it, carry, d=d, nbc=nbc, span=span):
                masks = (m0, jnp.logical_not(m0))
                ld = []
                for u in range(BAND_UNROLL_BWD):
                    qs, ks, off = _band_rows(it * BAND_UNROLL_BWD + u, d, nbc)
                    qrow, krow = _rows(qs, QB, d), _rows(ks, span, d)
                    ld.append(dict(qrow=qrow, krow=krow, q=q_ref[qrow, :] * ATTN_SCALE, k=k_ref[krow, :].astype(BF16),
                                   v=v_ref[krow, :].astype(BF16), do=do_ref[qrow, :], l=l_ref[qrow, :], dv=d_s[qrow, :],
                                   valid=_band_valid(span, off)))
                for t in ld:
                    t["qm"] = [jnp.where(mh, t["q"], 0.0).astype(BF16) for mh in masks]
                    t["dom"] = [jnp.where(mh, t["do"], 0.0).astype(BF16) for mh in masks]
                sd = [[(jnp.where(t["valid"], _dot_nt(t["qm"][h], t["k"]), NEG), _dot_nt(t["dom"][h], t["v"])) for h in range(2)]
                      for t in ld]
                pd = []
                for t, pair in zip(ld, sd):
                    row = []
                    for h, (s, dp) in enumerate(pair):
                        col = slice(h * HD, h * HD + 1)
                        p = jnp.exp(s - t["l"][:, col])
                        row.append((p.astype(BF16), (p * (dp - t["dv"][:, col])).astype(BF16)))
                    pd.append(row)
                gr = [(_dot(row[0][1], t["k"]), _dot(row[1][1], t["k"]),
                       _dot_tn(jnp.concatenate([row[0][1], row[1][1]], axis=0), jnp.concatenate(t["qm"], axis=0)),
                       _dot_tn(jnp.concatenate([row[0][0], row[1][0]], axis=0), jnp.concatenate(t["dom"], axis=0)))
                      for t, row in zip(ld, pd)]
                for t, (dq0, dq1, dk, dv) in zip(ld, gr):
                    dq_ref[t["qrow"], :] += _pair(m0, dq0, dq1) * ATTN_SCALE
                    dk_ref[t["krow"], :] += dk
                    dv_ref[t["krow"], :] += dv
                return carry

            lax.fori_loop(0, SEQ // QB // BAND_UNROLL_BWD, blk, 0)

    blk_of = lambda g: pl.BlockSpec((SEQ, LANE), lambda b, hp, g=g: (b, g * NP + hp))
    return pl.pallas_call(
        body, grid=(B, NP), in_specs=[blk_of(0), blk_of(1), blk_of(2), blk_of(0), blk_of(0), blk_of(0)],
        out_specs=[blk_of(0)] * 3, out_shape=[SDS((T, WG), F32)] * 3,
        scratch_shapes=[pltpu.VMEM((SEQ, LANE), F32)],
        compiler_params=_cp("arbitrary", "arbitrary"), name=name,
    )(pa, pa, pa, do, out, lse)


def _tile_causal(nq, nk, q0, k0):
    r = lax.broadcasted_iota(jnp.int32, (nq, nk), 0)
    c = lax.broadcasted_iota(jnp.int32, (nq, nk), 1)
    return r + (q0 - k0) >= c


def _row_to_col(row):
    n = row.shape[1]
    return jnp.transpose(jnp.broadcast_to(row, (LANE, n)))[:, 0:1]


def _col_to_row(col):
    n = col.shape[0]
    return jnp.transpose(jnp.broadcast_to(col, (n, LANE)))[0:1, :]


def fox_fwd(pb, fblk, frow, name):
    T = pb.shape[0]
    B = T // SEQ
    NP = WG // LANE
    n = SEQ // FB

    def body(q_ref, k_ref, v_ref, fc_ref, fr_ref, o_ref, lse_ref):
        i = pl.program_id(2)
        m0 = _head_lanes()
        q = q_ref[...] * ATTN_SCALE
        zero = jnp.zeros_like(q)
        qh = (jnp.where(m0, q, zero), jnp.where(m0, zero, q))
        fq = (_row_to_col(fc_ref[0, 0, 0]), _row_to_col(fc_ref[0, 1, 0]))

        def step(t, carry, masked):
            rows = pl.ds(pl.multiple_of(t * FT, FT), FT)
            kt = k_ref[rows, :]
            vt = v_ref[rows, :]
            ss = [_dot_nt(qh[h], kt) + fq[h] - fr_ref[0, h, t] for h in range(2)]
            if masked:
                ok = _tile_causal(FB, FT, i * FB, t * FT)
                ss = [jnp.where(ok, s, NEG) for s in ss]
            st = []
            for h in range(2):
                m, l, _ = carry[h]
                m2 = jnp.maximum(m, jnp.max(ss[h], axis=-1, keepdims=True))
                a = jnp.exp(m - m2)
                p = jnp.exp(ss[h] - m2)
                st.append((m2, a, a * l + jnp.sum(p, axis=-1, keepdims=True), p.astype(BF16)))
            pv = [_dot(st[h][3], vt) for h in range(2)]
            return tuple((st[h][0], st[h][2], st[h][1] * carry[h][2] + pv[h]) for h in range(2))

        one = (jnp.full((FB, 1), NEG, F32), jnp.zeros((FB, 1), F32), jnp.zeros((FB, LANE), F32))
        last = (i * FB) // FT
        carry = lax.fori_loop(0, last, lambda t, cr: step(t, cr, False), (one, one))
        (ma, la, acca), (mb, lb, accb) = step(last, carry, True)
        o_ref[...] = _pair(m0, acca / la, accb / lb)
        lse_ref[0, 0, 0] = _col_to_row(ma + jnp.log(la))
        lse_ref[0, 1, 0] = _col_to_row(mb + jnp.log(lb))

    qblk = pl.BlockSpec((FB, LANE), lambda b, hp, i: (b * n + i, hp))
    full = lambda g: pl.BlockSpec((SEQ, LANE), lambda b, hp, i, g=g: (b, g * NP + hp))
    rowb = pl.BlockSpec((1, 2, 1, 1, FB), lambda b, hp, i: (b, hp, i, 0, 0))
    return pl.pallas_call(
        body, grid=(B, NP, n),
        in_specs=[qblk, full(1), full(2), rowb, pl.BlockSpec((1, 2, SEQ // FT, 1, FT), lambda b, hp, i: (b, hp, 0, 0, 0))],
        out_specs=[qblk, rowb], out_shape=[SDS((T, WG), F32), SDS((B, NH, n, 1, FB), F32)],
        compiler_params=_cp("arbitrary", "arbitrary", "arbitrary"), name=name,
    )(pb, pb, pb, fblk, frow)


def fox_bwd(pb, do, lrow, drow, fblk, frow, name):
    T = pb.shape[0]
    B = T // SEQ
    NP = WG // LANE
    n = SEQ // FB

    def body(q_ref, k_ref, v_ref, do_ref, l_ref, d_ref, fc_ref, fr_ref, dq_ref, dk_ref, dv_ref, dfq_ref, dfk_ref):
        j = pl.program_id(2)
        m0 = _head_lanes()
        masks = (m0, jnp.logical_not(m0))

        @pl.when(j == 0)
        def _():
            dq_ref[...] = jnp.zeros_like(dq_ref)
            dfq_ref[...] = jnp.zeros_like(dfq_ref)

        kj = k_ref[...]
        vj = v_ref[...]
        fk = (_row_to_col(fc_ref[0, 0, 0]), _row_to_col(fc_ref[0, 1, 0]))

        def step(t, carry, masked):
            rows = pl.ds(pl.multiple_of(t * FT, FT), FT)
            qt = q_ref[rows, :] * ATTN_SCALE
            dot_ = do_ref[rows, :]
            zero = jnp.zeros_like(qt)
            qm = [jnp.where(mh, qt, zero) for mh in masks]
            dom = [jnp.where(mh, dot_, 0.0).astype(BF16) for mh in masks]
            ss = [_dot_nt(kj, qm[h]) + fr_ref[0, h, t] - fk[h] for h in range(2)]
            dps = [_dot_nt(vj, dom[h]) for h in range(2)]
            if masked:
                key = lax.broadcasted_iota(jnp.int32, (FB, FT), 0)
                qry = lax.broadcasted_iota(jnp.int32, (FB, FT), 1)
                ok = qry + (t * FT - j * FB) >= key
                ss = [jnp.where(ok, s, NEG) for s in ss]
            pds = []
            for h in range(2):
                p = jnp.exp(ss[h] - l_ref[0, h, t])
                ds = p * (dps[h] - d_ref[0, h, t])
                dfq_ref[0, h, t] += jnp.sum(ds, axis=0, keepdims=True)
                pds.append((p.astype(BF16), ds.astype(BF16), jnp.sum(ds, axis=-1, keepdims=True)))
            dks = [_dot(pds[h][1], qm[h]) for h in range(2)]
            dvs = [_dot(pds[h][0], dom[h]) for h in range(2)]
            dqs = [_dot_tn(pds[h][1], kj) for h in range(2)]
            dq_ref[rows, :] += _pair(m0, dqs[0], dqs[1]) * ATTN_SCALE
            return tuple((carry[h][0] + dks[h], carry[h][1] + dvs[h], carry[h][2] - pds[h][2]) for h in range(2))

        one = (jnp.zeros((FB, LANE), F32), jnp.zeros((FB, LANE), F32), jnp.zeros((FB, 1), F32))
        first = (j * FB) // FT
        carry = step(first, (one, one), True)
        (dka, dva, dfka), (dkb, dvb, dfkb) = lax.fori_loop(first + 1, SEQ // FT, lambda t, cr: step(t, cr, False), carry)
        dk_ref[...] = _pair(m0, dka, dkb)
        dv_ref[...] = _pair(m0, dva, dvb)
        dfk_ref[0, 0, 0] = _col_to_row(dfka)
        dfk_ref[0, 1, 0] = _col_to_row(dfkb)

    kblk = lambda g: pl.BlockSpec((FB, LANE), lambda b, hp, j, g=g: (b * n + j, g * NP + hp))
    full = pl.BlockSpec((SEQ, LANE), lambda b, hp, j: (b, hp))
    rowf = pl.BlockSpec((1, 2, SEQ // FT, 1, FT), lambda b, hp, j: (b, hp, 0, 0, 0))
    rowb = pl.BlockSpec((1, 2, 1, 1, FB), lambda b, hp, j: (b, hp, j, 0, 0))
    return pl.pallas_call(
        body, grid=(B, NP, n), in_specs=[full, kblk(1), kblk(2), full, rowf, rowf, rowb, rowf],
        out_specs=[full, kblk(0), kblk(0), rowf, rowb],
        out_shape=[SDS((T, WG), F32), SDS((T, WG), F32), SDS((T, WG), F32), SDS((B, NH, SEQ // FT, 1, FT), F32),
                   SDS((B, NH, n, 1, FB), F32)],
        compiler_params=_cp("arbitrary", "arbitrary", "arbitrary"), name=name,
    )(pb, pb, pb, do, lrow, drow, fblk, frow)


def _tri(lower):
    r = lax.broadcasted_iota(jnp.int32, (LANE, LANE), 0)
    c = lax.broadcasted_iota(jnp.int32, (LANE, LANE), 1)
    return ((r >= c) if lower else (r <= c)).astype(F32)


def _tri_dot(t, xblk):
    return jnp.dot(t, xblk, precision=lax.Precision.HIGHEST, preferred_element_type=F32)


def forget_cumsum(flog, bias, name):
    B, S, _ = flog.shape

    def body(f_ref, b_ref, o_ref):
        t = _tri(True)
        carry = jnp.zeros((1, LANE), F32)
        for blk in range(S // LANE):
            z = f_ref[0, blk * LANE:(blk + 1) * LANE, :] + b_ref[...]
            lf = jnp.minimum(z, 0.0) - jnp.log(1.0 + jnp.exp(-jnp.abs(z)))
            cs = _tri_dot(t, lf) + carry
            o_ref[0, blk * LANE:(blk + 1) * LANE, :] = cs
            carry = cs[LANE - 1:LANE, :]

    spec = pl.BlockSpec((1, S, LANE), lambda b: (b, 0, 0))
    return pl.pallas_call(
        body, grid=(B,), in_specs=[spec, pl.BlockSpec((1, LANE), lambda b: (0, 0))], out_specs=spec,
        out_shape=SDS((B, S, LANE), F32), compiler_params=_cp("arbitrary"), name=name,
    )(flog, bias)


def forget_cumsum_bwd(dF, flog, bias, name):
    B, S, _ = flog.shape

    def body(d_ref, f_ref, b_ref, o_ref, db_ref):
        @pl.when(pl.program_id(0) == 0)
        def _():
            db_ref[...] = jnp.zeros_like(db_ref)

        t = _tri(False)
        carry = jnp.zeros((1, LANE), F32)
        tot = jnp.zeros((1, LANE), F32)
        for blk in reversed(range(S // LANE)):
            sl = slice(blk * LANE, (blk + 1) * LANE)
            rc = _tri_dot(t, d_ref[0, sl, :]) + carry
            carry = rc[0:1, :]
            z = f_ref[0, sl, :] + b_ref[...]
            dz = rc * jax.nn.sigmoid(-z)
            o_ref[0, sl, :] = dz
            tot = tot + jnp.sum(dz, axis=0, keepdims=True)
        db_ref[...] += tot

    spec = pl.BlockSpec((1, S, LANE), lambda b: (b, 0, 0))
    vec = pl.BlockSpec((1, LANE), lambda b: (0, 0))
    return pl.pallas_call(
        body, grid=(B,), in_specs=[spec, spec, vec], out_specs=[spec, vec],
        out_shape=[SDS((B, S, LANE), F32), SDS((1, LANE), F32)], compiler_params=_cp("arbitrary"), name=name,
    )(dF, flog, bias)


def mixer_out_fwd(oa, ob, goa, gob, w_out, g_post, x, mod3, name):
    T = x.shape[0]

    def body(oa_ref, ob_ref, goa_ref, gob_ref, w_ref, gp_ref, x_ref, mod_ref, xo_ref, mg_ref, y0_ref):
        a = oa_ref[...]
        b = ob_ref[...]
        mg = jnp.concatenate([a * _rms(a) * goa_ref[...], b * _rms(b) * gob_ref[...]], axis=-1).astype(BF16)
        mg_ref[...] = mg
        y0 = _dot(mg, w_ref[...])
        y0_ref[...] = y0
        xo_ref[...] = x_ref[...] + mod_ref[0, 2:3, :] * (y0 * _rms(y0) * gp_ref[...])

    tok = pl.BlockSpec((TM, D), lambda i: (i, 0))
    half = pl.BlockSpec((TM, WG), lambda i: (i, 0))
    hv = pl.BlockSpec((1, WG), lambda i: (0, 0))
    return pl.pallas_call(
        body, grid=(T // TM,),
        in_specs=[half, half, hv, hv, pl.BlockSpec((D, D), lambda i: (0, 0)), pl.BlockSpec((1, D), lambda i: (0, 0)), tok,
                  pl.BlockSpec((1, 3, D), _mod_map)],
        out_specs=[tok, tok, tok], out_shape=[SDS((T, D), F32), SDS((T, D), BF16), SDS((T, D), F32)],
        compiler_params=_cp("arbitrary"), name=name,
    )(oa, ob, goa, gob, w_out, g_post, x, mod3)


def mixer_out_bwd(dxo, y0, mod3, g_post, w_out, oa, ob, goa, gob, name):
    T = dxo.shape[0]
    nb = T // SEQ
    tiles_per_seq = SEQ // TM

    def body(dxo_ref, y0_ref, mod_ref, gp_ref, w_ref, oa_ref, ob_ref, goa_ref, gob_ref,
             dy0_ref, doa_ref, dob_ref, dmg_ref, dgp_ref, dgoa_ref, dgob_ref, dvb_ref):
        i = pl.program_id(0)

        @pl.when(i == 0)
        def _():
            dgp_ref[...] = jnp.zeros_like(dgp_ref)
            dgoa_ref[...] = jnp.zeros_like(dgoa_ref)
            dgob_ref[...] = jnp.zeros_like(dgob_ref)

        @pl.when(i % tiles_per_seq == 0)
        def _():
            dmg_ref[...] = jnp.zeros_like(dmg_ref)

        dy0, dmg, dg = _post_bwd(dxo_ref[...], y0_ref[...], gp_ref[...], mod_ref[0, 2:3, :], 1.0)
        dmg_ref[0] += dmg
        dgp_ref[...] += dg
        db = dy0.astype(BF16)
        dy0_ref[...] = db
        dm = _dot_nt(db, w_ref[...])
        for o_ref, g_ref, do_ref, dg_ref, sl in ((oa_ref, goa_ref, doa_ref, dgoa_ref, slice(0, WG)),
                                                  (ob_ref, gob_ref, dob_ref, dgob_ref, slice(WG, 2 * WG))):
            o = o_ref[...]
            r = _rms(o)
            oh = o * r
            d = dm[:, sl]
            dg_ref[...] += jnp.sum(d * oh, axis=0, keepdims=True)
            dh = d * g_ref[...]
            do = r * (dh - oh * jnp.mean(dh * oh, axis=-1, keepdims=True))
            do_ref[...] = do
        ind = (lax.broadcasted_iota(jnp.int32, (WG, LANE), 0) // HD == lax.broadcasted_iota(jnp.int32, (WG, LANE), 1)).astype(BF16)
        prod = do * o
        hi = prod.astype(BF16)
        dvb_ref[...] = _dot(hi, ind) + _dot((prod - hi.astype(F32)).astype(BF16), ind)

    tok = pl.BlockSpec((TM, D), lambda i: (i, 0))
    half = pl.BlockSpec((TM, WG), lambda i: (i, 0))
    hv = pl.BlockSpec((1, WG), lambda i: (0, 0))
    vec = pl.BlockSpec((1, D), lambda i: (0, 0))
    return pl.pallas_call(
        body, grid=(T // TM,),
        in_specs=[tok, tok, pl.BlockSpec((1, 3, D), _mod_map), vec, pl.BlockSpec((D, D), lambda i: (0, 0)), half, half, hv, hv],
        out_specs=[tok, half, half, pl.BlockSpec((1, 1, D), _mod_map), vec, hv, hv, pl.BlockSpec((TM, LANE), lambda i: (i, 0))],
        out_shape=[SDS((T, D), BF16), SDS((T, WG), F32), SDS((T, WG), F32), SDS((nb, 1, D), F32), SDS((1, D), F32),
                   SDS((1, WG), F32), SDS((1, WG), F32), SDS((T, LANE), F32)],
        compiler_params=_cp("arbitrary"), name=name,
    )(dxo, y0, mod3, g_post, w_out, oa, ob, goa, gob)


def proj_grad_assemble(grads, rc, rs1, rs2, name):
    T = grads[0].shape[0]

    def body(*refs):
        ins, (c_ref, s1_ref, s2_ref, o_ref) = refs[:6], refs[6:]
        c, s1, s2 = c_ref[...], s1_ref[...], s2_ref[...]
        for grp in range(2):
            for k in range(WG // LANE):
                d = ins[grp][:, k * LANE:(k + 1) * LANE]
                un = d * c + pltpu.roll(d * s1, 8, 1) + pltpu.roll(d * s2, LANE - 8, 1)
                o_ref[:, grp * WG + k * LANE:grp * WG + (k + 1) * LANE] = un.astype(BF16)
        for g in range(2, 6):
            o_ref[:, g * WG:(g + 1) * WG] = ins[g][...].astype(BF16)

    half = pl.BlockSpec((TM, WG), lambda i: (i, 0))
    tab = pl.BlockSpec((TM, LANE), lambda i: (i, 0))
    return pl.pallas_call(
        body, grid=(T // TM,), in_specs=[half] * 6 + [tab] * 3, out_specs=pl.BlockSpec((TM, IN_MAIN), lambda i: (i, 0)),
        out_shape=SDS((T, IN_MAIN), BF16), compiler_params=_cp("arbitrary"), name=name,
    )(*grads, rc, rs1, rs2)


def mixer_proj_bwd(dproj, dflog, dxo, x, mod3, g_pre, w_main, w_f, name):
    T = x.shape[0]
    nb = T // SEQ
    tiles_per_seq = SEQ // TM

    def body(dp_ref, df_ref, dxo_ref, x_ref, mod_ref, g_ref, w_ref, wf_ref, dx_ref, dmod_ref, dg_ref):
        i = pl.program_id(0)

        @pl.when(i == 0)
        def _():
            dg_ref[...] = jnp.zeros_like(dg_ref)

        @pl.when(i % tiles_per_seq == 0)
        def _():
            dmod_ref[...] = jnp.zeros_like(dmod_ref)

        dh = _dot_nt(dp_ref[...], w_ref[...]) + _dot_nt(df_ref[...].astype(BF16), wf_ref[...])
        dx, dsh, dsc, dg = _norm_mod_bwd(dh, x_ref[...], g_ref[...], mod_ref[0, 1:2, :])
        dx_ref[...] = dxo_ref[...] + dx
        dmod_ref[0, 0:1, :] += dsh
        dmod_ref[0, 1:2, :] += dsc
        dg_ref[...] += dg

    tok = pl.BlockSpec((TM, D), lambda i: (i, 0))
    vec = pl.BlockSpec((1, D), lambda i: (0, 0))
    return pl.pallas_call(
        body, grid=(T // TM,),
        in_specs=[pl.BlockSpec((TM, IN_MAIN), lambda i: (i, 0)), pl.BlockSpec((TM, LANE), lambda i: (i, 0)), tok, tok,
                  pl.BlockSpec((1, 3, D), _mod_map), vec, pl.BlockSpec((D, IN_MAIN), lambda i: (0, 0)),
                  pl.BlockSpec((D, LANE), lambda i: (0, 0))],
        out_specs=[tok, pl.BlockSpec((1, 2, D), _mod_map), vec],
        out_shape=[SDS((T, D), F32), SDS((nb, 2, D), F32), SDS((1, D), F32)],
        compiler_params=_cp("arbitrary"), name=name,
    )(dproj, dflog, dxo, x, mod3, g_pre, w_main, w_f)


def ada_fwd(c_all, w, b, name):
    n = w.shape[1]
    tn = n // 2

    def body(c_ref, w_ref, b_ref, o_ref):
        cv = c_ref[...]
        o_ref[...] = _dot((cv * jax.nn.sigmoid(cv)).astype(BF16), w_ref[...].astype(BF16)) + b_ref[...]

    R = c_all.shape[0]
    return pl.pallas_call(
        body, grid=(2,),
        in_specs=[pl.BlockSpec((R, D), lambda j: (0, 0)), pl.BlockSpec((D, tn), lambda j: (0, j)), pl.BlockSpec((1, tn), lambda j: (0, j))],
        out_specs=pl.BlockSpec((R, tn), lambda j: (0, j)), out_shape=SDS((R, n), F32),
        compiler_params=_cp("arbitrary"), name=name,
    )(c_all, w, b)


def ada_bwd(c_all, dmod, name):
    R, n = dmod.shape
    tn = n // 2

    def body(c_ref, d_ref, o_ref):
        cv = c_ref[...]
        o_ref[...] = _dot_tn((cv * jax.nn.sigmoid(cv)).astype(BF16), d_ref[...].astype(BF16))

    return pl.pallas_call(
        body, grid=(2,), in_specs=[pl.BlockSpec((R, D), lambda j: (0, 0)), pl.BlockSpec((R, tn), lambda j: (0, j))],
        out_specs=pl.BlockSpec((D, tn), lambda j: (0, j)), out_shape=SDS((D, n), F32),
        compiler_params=_cp("arbitrary"), name=name,
    )(c_all, dmod)


def _adam_math(w, g, m, v):
    m2 = ADAM_B1 * m + (1.0 - ADAM_B1) * g
    v2 = ADAM_B2 * v + (1.0 - ADAM_B2) * (g * g)
    m_hat = m2 / (1.0 - ADAM_B1 ** ADAM_STEP)
    v_hat = v2 / (1.0 - ADAM_B2 ** ADAM_STEP)
    delta = -ADAM_LR * (m_hat / (jnp.sqrt(v_hat) + ADAM_EPS) + ADAM_WD * w)
    return delta, m2, v2


def adam_update(w, g, m, v, tr, name):
    R, C = w.shape

    def body(w_ref, g_ref, m_ref, v_ref, d_ref, mo_ref, vo_ref):
        d_ref[...], mo_ref[...], vo_ref[...] = _adam_math(w_ref[...], g_ref[...], m_ref[...], v_ref[...])

    spec = pl.BlockSpec((tr, C), lambda i: (i, 0))
    return pl.pallas_call(
        body, grid=(R // tr,), in_specs=[spec] * 4, out_specs=[spec] * 3, out_shape=[SDS((R, C), F32)] * 3,
        compiler_params=_cp("arbitrary"), name=name,
    )(w, g, m, v)


def vec_adam(parts, w, m, v, name):
    P, C = parts.shape

    def body(p_ref, w_ref, m_ref, v_ref, g_ref, d_ref, mo_ref, vo_ref):
        g = jnp.sum(p_ref[...], axis=0, keepdims=True)
        g_ref[...] = g
        d_ref[...], mo_ref[...], vo_ref[...] = _adam_math(w_ref[...], g, m_ref[...], v_ref[...])

    return pl.pallas_call(body, out_shape=[SDS((1, C), F32)] * 4, compiler_params=_cp(), name=name)(parts, w, m, v)


HBM = pl.BlockSpec(memory_space=pltpu.HBM)
VMEM = pl.BlockSpec(memory_space=pltpu.VMEM)


def _place():
    x, y, c = lax.axis_index("x"), lax.axis_index("y"), lax.axis_index("c")
    return x, y, c, [(1 - x, y), (x, 1 - y), (1 - x, 1 - y)]


def all_gather8(xs, name):
    R, C = xs.shape

    def body(x_ref, out_ref, send_sems, recv_sems, local_sem):
        x, y, c, chips = _place()
        me, sibling = (x, y, c), (x, y, 1 - c)

        def slot(px, py, pc):
            return out_ref.at[4 * px + 2 * py + pc]

        def copy(k, block, to, src=None):
            return pltpu.make_async_remote_copy(
                src_ref=slot(*block) if src is None else src, dst_ref=slot(*block),
                send_sem=send_sems.at[k], recv_sem=recv_sems.at[k], device_id=to, device_id_type=MESH)

        mine = pltpu.make_async_copy(x_ref, slot(*me), local_sem)
        mine.start()
        first = [copy(0, me, sibling, src=x_ref)]
        first += [copy(1 + j, me, (*chip, c), src=x_ref) for j, chip in enumerate(chips)]
        for cp in first:
            cp.start()
        passed = [copy(4 + j, (*chip, c), sibling) for j, chip in enumerate(chips)]
        for j, chip in enumerate(chips):
            copy(1 + j, (*chip, c), me).wait_recv()
            passed[j].start()
        copy(0, sibling, me).wait_recv()
        for j, chip in enumerate(chips):
            copy(4 + j, (*chip, 1 - c), me).wait_recv()
        for cp in first + passed:
            cp.wait_send()
        mine.wait()

    return pl.pallas_call(
        body, out_shape=SDS((N_DEV, R, C), xs.dtype), in_specs=[VMEM], out_specs=VMEM,
        scratch_shapes=[pltpu.SemaphoreType.DMA((7,)), pltpu.SemaphoreType.DMA((7,)), pltpu.SemaphoreType.DMA],
        compiler_params=pltpu.CompilerParams(vmem_limit_bytes=VMEM_LIMIT), name=name,
    )(xs)


class ShardGather:
    def __init__(self, shapes, splits):
        self.shapes, self.splits, self.n = shapes, splits, len(shapes)

    def scratch(self):
        n = self.n
        return [pltpu.SemaphoreType.DMA((n, 6)), pltpu.SemaphoreType.DMA((n, 6)), pltpu.SemaphoreType.DMA((n,))]

    def out_shapes(self, dtype):
        return [SDS((N_SHARD,) + tuple(s), dtype) for s in self.shapes]

    def _half(self, ref, k, cc):
        lo, hi = (0, self.splits[k]) if cc == 0 else (self.splits[k], self.shapes[k][0])
        return ref.at[pl.ds(lo, hi - lo)]

    def _phase(self, w_refs, o_refs, sems, finish):
        send_sems, recv_sems, local_sems = sems
        x, y, c, chips = _place()
        sibling = (x, y, 1 - c)
        me_s = 2 * x + y

        def rcopy(src, dst, k, s, to):
            return pltpu.make_async_remote_copy(src_ref=src, dst_ref=dst, send_sem=send_sems.at[k, s],
                                                recv_sem=recv_sems.at[k, s], device_id=to, device_id_type=MESH)

        for cc in (0, 1):
            @pl.when(c == cc)
            def _():
                local = [pltpu.make_async_copy(w_refs[k], o_refs[k].at[me_s], local_sems.at[k]) for k in range(self.n)]
                first = [rcopy(self._half(w_refs[k], k, cc), self._half(o_refs[k].at[me_s], k, cc), k, j, (*chip, c))
                         for k in range(self.n) for j, chip in enumerate(chips)]
                if not finish:
                    for cp in local + first:
                        cp.start()
                    return
                passed = []
                for k in range(self.n):
                    for j, chip in enumerate(chips):
                        land = self._half(o_refs[k].at[2 * chip[0] + chip[1]], k, cc)
                        rcopy(land, land, k, j, (*chip, c)).wait_recv()
                        f = rcopy(land, land, k, 3 + j, sibling)
                        f.start()
                        passed.append(f)
                for k in range(self.n):
                    for j, chip in enumerate(chips):
                        other = self._half(o_refs[k].at[2 * chip[0] + chip[1]], k, 1 - cc)
                        rcopy(other, other, k, 3 + j, sibling).wait_recv()
                for s in first + passed:
                    s.wait_send()
                for cp in local:
                    cp.wait()

    def start(self, w_refs, o_refs, sems):
        self._phase(w_refs, o_refs, sems, False)

    def finish(self, w_refs, o_refs, sems):
        self._phase(w_refs, o_refs, sems, True)


def all_gather_shards(ws, splits, name):
    n = len(ws)
    plan = ShardGather([w.shape for w in ws], splits)

    def body(*refs):
        plan.start(refs[:n], refs[n:2 * n], refs[2 * n:])
        plan.finish(refs[:n], refs[n:2 * n], refs[2 * n:])

    return pl.pallas_call(
        body, out_shape=plan.out_shapes(ws[0].dtype), in_specs=[HBM] * n, out_specs=[HBM] * n,
        scratch_shapes=plan.scratch(), name=name,
    )(*ws)


def sibling_send_half(gs, name):
    n = len(gs)

    def body(*refs):
        g_refs, o_refs = refs[:n], refs[n:2 * n]
        send_sems, recv_sems = refs[2 * n:]
        x, y, c, _ = _place()
        cps = []
        for k in range(n):
            hr = gs[k].shape[1] // 2
            src = g_refs[k].at[:, pl.ds(pl.multiple_of((1 - c) * hr, 8), hr)]
            cp = pltpu.make_async_remote_copy(src_ref=src, dst_ref=o_refs[k], send_sem=send_sems.at[k], recv_sem=recv_sems.at[k],
                                              device_id=(x, y, 1 - c), device_id_type=MESH)
            cp.start()
            cps.append(cp)
        for cp in cps:
            cp.wait()

    return pl.pallas_call(
        body, out_shape=[SDS((N_SHARD, g.shape[1] // 2, g.shape[2]), g.dtype) for g in gs], in_specs=[HBM] * n, out_specs=[HBM] * n,
        scratch_shapes=[pltpu.SemaphoreType.DMA((n,)), pltpu.SemaphoreType.DMA((n,))], name=name,
    )(*gs)


def _scatter_copies(h_refs, o_refs, send_sems, recv_sems):
    _, _, c, chips = _place()
    return [pltpu.make_async_remote_copy(
        src_ref=h_refs[k].at[2 * chip[0] + chip[1]], dst_ref=o_refs[k].at[j], send_sem=send_sems.at[k, j],
        recv_sem=recv_sems.at[k, j], device_id=(*chip, c), device_id_type=MESH)
        for k in range(len(h_refs)) for j, chip in enumerate(chips)]


def chip_scatter(hs, name):
    n = len(hs)

    def body(*refs):
        cps = _scatter_copies(refs[:n], refs[n:2 * n], *refs[2 * n:])
        for cp in cps:
            cp.start()
        for cp in cps:
            cp.wait()

    return pl.pallas_call(
        body, out_shape=[SDS((3,) + h.shape[1:], h.dtype) for h in hs], in_specs=[HBM] * n, out_specs=[HBM] * n,
        scratch_shapes=[pltpu.SemaphoreType.DMA((n, 3)), pltpu.SemaphoreType.DMA((n, 3))], name=name,
    )(*hs)


def sibling_swap(ghs, name):
    n = len(ghs)

    def body(*refs):
        g_refs, o_refs = refs[:n], refs[n:2 * n]
        send_sems, recv_sems = refs[2 * n:]
        x, y, c, _ = _place()
        cps = []
        for k in range(n):
            cp = pltpu.make_async_remote_copy(src_ref=g_refs[k], dst_ref=o_refs[k], send_sem=send_sems.at[k],
                                              recv_sem=recv_sems.at[k], device_id=(x, y, 1 - c), device_id_type=MESH)
            cp.start()
            cps.append(cp)
        for cp in cps:
            cp.wait()

    return pl.pallas_call(
        body, out_shape=[SDS(g.shape, g.dtype) for g in ghs], in_specs=[HBM] * n, out_specs=[HBM] * n,
        scratch_shapes=[pltpu.SemaphoreType.DMA((n,)), pltpu.SemaphoreType.DMA((n,))], name=name,
    )(*ghs)


def pair_sum(g, ra, cidx, name):
    _, r, cols = g.shape
    hr = r // 2

    def body(c_ref, g_ref, a_ref, o_ref):
        o_ref[...] = (g_ref[...] + a_ref[...]).astype(BF16)

    return pl.pallas_call(
        body,
        grid_spec=pltpu.PrefetchScalarGridSpec(
            num_scalar_prefetch=1, grid=(N_SHARD,),
            in_specs=[pl.BlockSpec((1, hr, cols), lambda s, c_ref: (s, c_ref[0], 0)),
                      pl.BlockSpec((1, hr, cols), lambda s, c_ref: (s, 0, 0))],
            out_specs=pl.BlockSpec((1, hr, cols), lambda s, c_ref: (s, 0, 0))),
        out_shape=SDS((N_SHARD, hr, cols), BF16), compiler_params=_cp("arbitrary"), name=name,
    )(cidx, g, ra)


def chip_sum(h, rb, sidx, name):
    _, hr, cols = h.shape

    def body(s_ref, h_ref, r_ref, o_ref):
        o_ref[...] = ((h_ref[0].astype(F32) + r_ref[0].astype(F32)) + r_ref[1].astype(F32)) + r_ref[2].astype(F32)

    return pl.pallas_call(
        body,
        grid_spec=pltpu.PrefetchScalarGridSpec(
            num_scalar_prefetch=1, grid=(1,),
            in_specs=[pl.BlockSpec((1, hr, cols), lambda i, s_ref: (s_ref[0], 0, 0)),
                      pl.BlockSpec((3, hr, cols), lambda i, s_ref: (0, 0, 0))],
            out_specs=pl.BlockSpec((hr, cols), lambda i, s_ref: (0, 0))),
        out_shape=SDS((hr, cols), F32), compiler_params=_cp("arbitrary"), name=name,
    )(sidx, h, rb)


def _shard_cols(g, n_valid):
    r = g.shape[0]
    return g[:, :n_valid].reshape(r, N_SHARD, n_valid // N_SHARD).transpose(1, 0, 2)


def _unshard_cols(o, pad_to):
    _, r, n = o.shape
    full = o.transpose(1, 0, 2).reshape(r, N_SHARD * n)
    return jnp.pad(full, ((0, 0), (0, pad_to - N_SHARD * n)))


def _rows_of_tiles(t):
    B, H, S = t.shape
    return t.reshape(B, H, S // FT, 1, FT)


def mixer_fwd(x1, mod3, g_pre, w_main, w_f, b_forget_pad, goa, gob, w_out, g_post, tabs, nb):
    hmix, pa, pb, flog = mixer_proj(x1, mod3, g_pre, w_main, w_f, *tabs, name="mixer_proj")
    out_a, lse_a = band_fwd(pa, name="band_fwd")
    F = forget_cumsum(flog.reshape(nb, SEQ, LANE), b_forget_pad, name="forget_cumsum")
    Fh = F[:, :, :NH].transpose(0, 2, 1)
    fblk = Fh.reshape(nb, NH, SEQ // FB, 1, FB)
    frow = _rows_of_tiles(Fh)
    out_b, lse_b = fox_fwd(pb, fblk, frow, name="fox_fwd")
    x2, merged, y0m = mixer_out_fwd(out_a, out_b, goa, gob, w_out, g_post, x1, mod3, name="mixer_out_fwd")
    res = dict(hmix=hmix, flog=flog, pa=pa, pb=pb, out_a=out_a, lse_a=lse_a, fblk=fblk, frow=frow, out_b=out_b,
               lrow=_rows_of_tiles(lse_b.reshape(nb, NH, SEQ)), merged=merged, y0m=y0m)
    return x2, res


def mixer_bwd(dx2, x1, mod3, g_pre, w_main, w_f, b_forget_pad, goa, gob, w_out, g_post, tabs, res, nb):
    T = nb * SEQ
    dy0m, doa, dob, dmgate, dg_post, dgoa, dgob, dvec_b = mixer_out_bwd(
        dx2, res["y0m"], mod3, g_post, w_out, res["out_a"], res["out_b"], goa, gob, name="mixer_out_bwd")
    dqa, dka, dva = band_bwd(res["pa"], doa, res["out_a"], res["lse_a"], name="band_bwd")
    drow = _rows_of_tiles(dvec_b[:, :NH].reshape(nb, SEQ, NH).transpose(0, 2, 1))
    dqb, dkb, dvb, dfq, dfk = fox_bwd(res["pb"], dob, res["lrow"], drow, res["fblk"], res["frow"], name="fox_bwd")
    dF = (dfq.reshape(nb, NH, SEQ) + dfk.reshape(nb, NH, SEQ)).transpose(0, 2, 1)
    dF = jnp.pad(dF, ((0, 0), (0, 0), (0, LANE - NH)))
    dflog, dbf = forget_cumsum_bwd(dF, res["flog"].reshape(nb, SEQ, LANE), b_forget_pad, name="forget_cumsum_bwd")
    dflog = dflog.reshape(T, LANE)
    dproj = proj_grad_assemble((dqa, dka, dva, dqb, dkb, dvb), *tabs, name="proj_grad_assemble")
    dx1, dmod2, dg_pre = mixer_proj_bwd(dproj, dflog, dx2, x1, mod3, g_pre, w_main, w_f, name="mixer_proj_bwd")
    g_main = matmul_tn(res["hmix"], dproj, D, 1024, 1024, name="grad_w_in")
    g_f = matmul_tn(res["hmix"], dflog.astype(BF16), D, LANE, 1024, name="grad_w_forget")
    g_out = matmul_tn(res["merged"], dy0m, D, D, 1024, name="grad_w_out")
    dmod3 = jnp.concatenate([dmod2, dmgate], axis=1)
    return dx1, dmod3, dict(g_pre=dg_pre, g_post=dg_post, goa=dgoa, gob=dgob, b_forget=dbf[:, :NH],
                            w_in=jnp.concatenate([g_main, g_f[:, :NH]], axis=1), w_out=g_out)


def ffn_grads(h, dy0, act, dgate, dup, pre):
    g_gate = matmul_tn(h, dgate, D, FF_TN, 1024, name=pre + "_grad_gate")
    g_up = matmul_tn(h, dup, D, FF_TN, 1024, name=pre + "_grad_up")
    g_down = matmul_tn(act, dy0, FF_TN, D, 1024, name=pre + "_grad_down")
    return g_gate, g_up, g_down


def local_step(x0, tgt, pos_col, mod, wfull, p, late_weights=None, early_grads=None):
    T = x0.shape[0]
    nb = T // SEQ
    mod_ff1, mod_mix, mod_ff2 = mod[:, 0:3], mod[:, 3:6], mod[:, 6:9]
    tabs = rope_tables(pos_col, name="rope_tables")
    bf_pad = jnp.pad(p["b_forget"], ((0, 0), (0, LANE - NH)))

    (x1, h1, gate1, up1, y01), gathered = ffn_fwd(
        x0, mod_ff1, p["g_pre_ff1"], p["g_post_ff1"], wfull["w_ff1_gate"], wfull["w_ff1_up"], wfull["w_ff1_down"], 0.5,
        name="ff1_fwd", gather=None if late_weights is None else late_weights[:2])
    if late_weights is not None:
        wfull = {**wfull, **late_weights[2](gathered)}
    x2, res = mixer_fwd(x1, mod_mix, p["g_pre_mix"], wfull["w_main"], wfull["w_f"], bf_pad, p["g_out_a"], p["g_out_b"],
                        wfull["w_out"], p["g_post_mix"], tabs, nb)
    (x3, h2, gate2, up2, y02), _ = ffn_fwd(x2, mod_ff2, p["g_pre_ff2"], p["g_post_ff2"], wfull["w_ff2_gate"],
                                           wfull["w_ff2_up"], wfull["w_ff2_down"], 0.5, name="ff2_fwd")

    dx3, loss_part = loss_grad(x3, tgt, name="loss_grad")
    (dx2, dy02, act2, dgate2, dup2, dmod_ff2, dgpre2, dgpost2), _ = ffn_bwd(
        dx3, x2, y02, mod_ff2, p["g_pre_ff2"], p["g_post_ff2"], gate2, up2, wfull["w_ff2_gate"], wfull["w_ff2_up"],
        wfull["w_ff2_down"], 0.5, name="ff2_bwd")
    gw = {}
    gw["w_ff2_gate"], gw["w_ff2_up"], gw["w_ff2_down"] = ffn_grads(h2, dy02, act2, dgate2, dup2, "ff2")
    dx1, dmod_mix, gmix = mixer_bwd(dx2, x1, mod_mix, p["g_pre_mix"], wfull["w_main"], wfull["w_f"], bf_pad, p["g_out_a"],
                                    p["g_out_b"], wfull["w_out"], p["g_post_mix"], tabs, res, nb)
    gw["w_in"], gw["w_out"] = gmix["w_in"], gmix["w_out"]
    (dx0, dy01, act1, dgate1, dup1, dmod_ff1, dgpre1, dgpost1), scattered = ffn_bwd(
        dx1, x0, y01, mod_ff1, p["g_pre_ff1"], p["g_post_ff1"], gate1, up1, wfull["w_ff1_gate"], wfull["w_ff1_up"],
        wfull["w_ff1_down"], 0.5, name="ff1_bwd", scatter=None if early_grads is None else early_grads(gw))
    gw["w_ff1_gate"], gw["w_ff1_up"], gw["w_ff1_down"] = ffn_grads(h1, dy01, act1, dgate1, dup1, "ff1")
    dmod = jnp.concatenate([dmod_ff1, dmod_mix, dmod_ff2], axis=1).reshape(nb, 9 * D)
    small = dict(g_pre_ff1=dgpre1, g_post_ff1=dgpost1, g_pre_mix=gmix["g_pre"], g_post_mix=gmix["g_post"], g_pre_ff2=dgpre2,
                 g_post_ff2=dgpost2, g_out_a=gmix["goa"], g_out_b=gmix["gob"], b_forget=gmix["b_forget"])
    return loss_part, dx0, dmod, gw, small, scattered


def kernel(x, c, positions, w_ada, b_ada, g_pre_ff1, g_post_ff1, w_ff1_gate, w_ff1_up, w_ff1_down, g_pre_mix, g_post_mix, w_in, b_forget, g_out_a, g_out_b, w_out, g_pre_ff2, g_post_ff2, w_ff2_gate, w_ff2_up, w_ff2_down, loss_target, m_w_ada, m_b_ada, m_g_pre_ff1, m_g_post_ff1, m_w_ff1_gate, m_w_ff1_up, m_w_ff1_down, m_g_pre_mix, m_g_post_mix, m_w_in, m_b_forget, m_g_out_a, m_g_out_b, m_w_out, m_g_pre_ff2, m_g_post_ff2, m_w_ff2_gate, m_w_ff2_up, m_w_ff2_down, v_w_ada, v_b_ada, v_g_pre_ff1, v_g_post_ff1, v_w_ff1_gate, v_w_ff1_up, v_w_ff1_down, v_g_pre_mix, v_g_post_mix, v_w_in, v_b_forget, v_g_out_a, v_g_out_b, v_w_out, v_g_pre_ff2, v_g_post_ff2, v_w_ff2_gate, v_w_ff2_up, v_w_ff2_down):
    args = dict(locals())
    nb = x.shape[0]
    T = nb * SEQ
    ax, ay, ac = lax.axis_index("x"), lax.axis_index("y"), lax.axis_index("c")
    shard = 2 * ax + ay
    cidx = jnp.reshape(ac, (1,)).astype(jnp.int32)
    sidx = jnp.reshape(shard, (1,)).astype(jnp.int32)

    big = ["w_ff1_gate", "w_ff1_up", "w_ff1_down", "w_in", "w_out", "w_ff2_gate", "w_ff2_up", "w_ff2_down"]
    vecs = ["g_pre_ff1", "g_post_ff1", "g_pre_mix", "g_post_mix", "g_pre_ff2", "g_post_ff2"]

    first, late = big[:3], big[3:]
    splits = dict(zip(big, [512, 512, 352, 512, 128, 512, 512, 352]))

    def assemble(names, gathered):
        out = {}
        for n, o in zip(names, gathered):
            if n.endswith("gate") or n.endswith("up"):
                out[n] = _tiles_of_cols(_unshard_cols(o, DFF_PAD))
            elif n.endswith("down"):
                out[n] = jnp.pad(o.reshape(DFF, D), ((0, DFF_PAD - DFF), (0, 0)))
            elif n == "w_in":
                full = _unshard_cols(o, IN_COLS)
                out["w_main"] = full[:, :IN_MAIN]
                out["w_f"] = jnp.pad(full[:, IN_MAIN:], ((0, 0), (0, LANE - NH)))
            else:
                out[n] = o.reshape(D, D)
        return out

    wfull = assemble(first, all_gather_shards([args[n][0].astype(BF16) for n in first], [splits[n] for n in first],
                                              name="all_gather_weights"))
    late_weights = ([args[n][0].astype(BF16) for n in late], [splits[n] for n in late], functools.partial(assemble, late))

    ncol = w_ada.shape[2]
    c_all = all_gather8(c, name="all_gather_c").reshape(N_DEV * nb, D)
    b_loc = lax.dynamic_slice(b_ada, (0, shard * ncol), (1, ncol))
    mod_loc = ada_fwd(c_all, w_ada[0], b_loc, name="ada_fwd")
    mod_g = all_gather8(mod_loc, name="all_gather_mod")
    row0 = (4 * ax + 2 * ay + ac) * nb
    mod_rows = lax.dynamic_slice(mod_g, (0, row0, 0), (N_DEV, nb, ncol))
    mod = jnp.concatenate([mod_rows[2 * s] for s in range(N_SHARD)], axis=-1).reshape(nb, 9, D)

    small_in = dict(g_pre_ff1=g_pre_ff1, g_post_ff1=g_post_ff1, g_pre_mix=g_pre_mix, g_post_mix=g_post_mix, g_pre_ff2=g_pre_ff2,
                    g_post_ff2=g_post_ff2, g_out_a=g_out_a, g_out_b=g_out_b, b_forget=b_forget)
    def shard_blocked(n, g):
        if n.endswith("gate") or n.endswith("up"):
            return _shard_cols(g, DFF)
        if n.endswith("down"):
            return g[:DFF].reshape(N_SHARD, DFF // N_SHARD, D)
        if n == "w_in":
            return _shard_cols(g, IN_COLS)
        return g.reshape(N_SHARD, D // N_SHARD, D)

    def chip_sums(names, gw, tag):
        gsb = [shard_blocked(n, gw[n]) for n in names]
        ras = sibling_send_half(gsb, name="grad_sibling_send_" + tag)
        return [pair_sum(g, ra, cidx, name=f"grad_pair_sum_{n}") for n, g, ra in zip(names, gsb, ras)]

    hs = {}

    def early_grads(gw):
        hs.update(zip(late, chip_sums(late, gw, "late")))
        return [hs[n] for n in late]

    loss_part, dx0, dmod, gw, small, rbs_late = local_step(
        x.reshape(T, D), loss_target.reshape(T, D), positions.reshape(T, 1), mod, wfull, small_in, late_weights, early_grads)

    dmod_all = all_gather8(dmod, name="all_gather_dmod").reshape(N_DEV * nb, 9 * D)
    dmod_loc = lax.dynamic_slice(dmod_all, (0, shard * ncol), (N_DEV * nb, ncol))
    g_w_ada = ada_bwd(c_all, dmod_loc, name="ada_bwd")

    hs.update(zip(first, chip_sums(first, gw, "first")))
    rbs = dict(zip(late, rbs_late))
    rbs.update(zip(first, chip_scatter([hs[n] for n in first], name="grad_chip_scatter")))
    ghs = [chip_sum(hs[n], rbs[n], sidx, name=f"grad_chip_sum_{n}") for n in big]
    theirs = sibling_swap(ghs, name="grad_sibling_swap")
    gfull = {n: jnp.concatenate([jnp.where(ac == 0, mine, other), jnp.where(ac == 0, other, mine)], axis=0)
             for n, mine, other in zip(big, ghs, theirs)}
    gfull["w_ada"] = g_w_ada

    row6 = jnp.concatenate([small["g_out_a"], small["g_out_b"]], axis=1)
    row7 = jnp.concatenate([small["b_forget"], loss_part[0:1, 0:1], jnp.zeros((1, D - NH - 1), F32)], axis=1)
    pack = jnp.concatenate([small[n] for n in vecs] + [row6, row7], axis=0)
    packed = all_gather8(pack, name="all_gather_small").reshape(N_DEV, 8 * D)

    def pack_state(pre):
        r6 = jnp.concatenate([args[pre + "g_out_a"], args[pre + "g_out_b"]], axis=1)
        r7 = jnp.pad(args[pre + "b_forget"], ((0, 0), (0, D - NH)))
        return jnp.concatenate([args[pre + n] for n in vecs] + [r6, r7], axis=0).reshape(1, 8 * D)

    sg, sd, sm, sv = (t.reshape(8, D) for t in vec_adam(packed, pack_state(""), pack_state("m_"), pack_state("v_"), name="adam_small"))

    def unpack(t):
        out = {n: t[i:i + 1] for i, n in enumerate(vecs)}
        out["g_out_a"], out["g_out_b"], out["b_forget"] = t[6:7, :WG], t[6:7, WG:], t[7:8, :NH]
        return out

    outs = dict(grad=unpack(sg), delta=unpack(sd), new_m=unpack(sm), new_v=unpack(sv))
    loss = sg[7, NH]
    outs["grad"]["b_ada"], outs["delta"]["b_ada"], outs["new_m"]["b_ada"], outs["new_v"]["b_ada"] = vec_adam(
        dmod_all, b_ada, m_b_ada, v_b_ada, name="adam_b_ada")

    for n in big + ["w_ada"]:
        g = gfull[n]
        rows = g.shape[0]
        tr = 128 if rows % 128 == 0 else 344
        d, m2, v2 = adam_update(args[n][0], g, args["m_" + n][0], args["v_" + n][0], tr, name="adam_" + n)
        outs["grad"][n], outs["delta"][n], outs["new_m"][n], outs["new_v"][n] = g[None], d[None], m2[None], v2[None]

    order = ["w_ada", "b_ada", "g_pre_ff1", "g_post_ff1", "w_ff1_gate", "w_ff1_up", "w_ff1_down", "g_pre_mix", "g_post_mix", "w_in",
             "b_forget", "g_out_a", "g_out_b", "w_out", "g_pre_ff2", "g_post_ff2", "w_ff2_gate", "w_ff2_up", "w_ff2_down"]
    result = [loss, dx0.reshape(nb, SEQ, D)]
    for kind in ("grad", "delta", "new_m", "new_v"):
        result += [outs[kind][n] for n in order]
    return tuple(result)
```

```python
import functools
import math

import jax
import jax.numpy as jnp
from jax import lax
from jax.experimental import pallas as pl
from jax.experimental.pallas import tpu as pltpu

D = 1024
SEQ = 2048
HD = 64
NH = 8
WG = NH * HD
DFF = 2752
DFF_PAD = 2816
IN_MAIN = 6 * WG
IN_COLS = IN_MAIN + NH
N_SHARD = 4
N_DEV = 8
LANE = 128
QB = 128
FB = 256
FT = 512
BAND_UNROLL = 4
BAND_UNROLL_BWD = 4
PATTERNS = ((1, 16), (4, 4), (16, 1))
ROPE_THETA = 500000.0
EPS = 1e-6
NEG = -1e30
ATTN_SCALE = HD ** -0.5
TM = 512
TM_FFN = 512
TM_BWD = 256
VMEM_LIMIT = 56 * 1024 * 1024

ADAM_LR, ADAM_B1, ADAM_B2, ADAM_EPS, ADAM_WD, ADAM_STEP = 0.001, 0.9, 0.999, 1e-08, 0.01, 10

F32 = jnp.float32
BF16 = jnp.bfloat16
MESH = pl.DeviceIdType.MESH
SDS = jax.ShapeDtypeStruct


def _cp(*sem):
    return pltpu.CompilerParams(dimension_semantics=sem, vmem_limit_bytes=VMEM_LIMIT)


def _dot(a, b):
    return jnp.dot(a, b, preferred_element_type=F32)


def _dot_nt(a, b):
    return lax.dot_general(a, b, (((1,), (1,)), ((), ())), preferred_element_type=F32)


def _dot_tn(a, b):
    return lax.dot_general(a, b, (((0,), (0,)), ((), ())), preferred_element_type=F32)


def _rms(xf):
    return lax.rsqrt(jnp.mean(xf * xf, axis=-1, keepdims=True) + EPS)


def _norm_mod_bwd(dh, xf, g, scale):
    r = _rms(xf)
    xh = xf * r
    dsh = jnp.sum(dh, axis=0, keepdims=True)
    dsc = jnp.sum(dh * (xh * g), axis=0, keepdims=True)
    dn = dh * (1.0 + scale)
    dg = jnp.sum(dn * xh, axis=0, keepdims=True)
    dxh = dn * g
    dx = r * (dxh - xh * jnp.mean(dxh * xh, axis=-1, keepdims=True))
    return dx, dsh, dsc, dg


def _post_bwd(dxo, y0, g, mgate, gs):
    r = _rms(y0)
    yh = y0 * r
    dmg = gs * jnp.sum(dxo * (yh * g), axis=0, keepdims=True)
    dy = (gs * mgate) * dxo
    dg = jnp.sum(dy * yh, axis=0, keepdims=True)
    dyh = dy * g
    dy0 = r * (dyh - yh * jnp.mean(dyh * yh, axis=-1, keepdims=True))
    return dy0, dmg, dg


def _mod_map(i, *_):
    return ((i * TM) // SEQ, 0, 0)


FF_TN = 1408
FF_TILES = ((0, 768), (768, 1536), (1536, 2304), (2304, 2816))


def _resident_scratch():
    return [pltpu.VMEM((D, DFF_PAD), BF16), pltpu.VMEM((D, DFF_PAD), BF16), pltpu.VMEM((DFF_PAD, D), BF16),
            pltpu.SemaphoreType.DMA((3,))]


def _load_resident(first_step, srcs, dsts, sems):
    @pl.when(first_step)
    def _():
        cps = [pltpu.make_async_copy(s, d, sems.at[k]) for k, (s, d) in enumerate(zip(srcs, dsts))]
        for cp in cps:
            cp.start()
        for cp in cps:
            cp.wait()


def ffn_fwd(x, mod3, g_pre, g_post, wg, wu, wd, gs, name, gather=None):
    T = x.shape[0]
    tm = TM_FFN
    ng = 0 if gather is None else len(gather[0])
    plan = None if gather is None else ShardGather([w.shape for w in gather[0]], gather[1])

    def body(*refs):
        x_ref, mod_ref, gpre_ref, gpost_ref = refs[:4]
        xo_ref, h_ref, gate_ref, up_ref, y0_ref = refs[7 + ng:12 + ng]
        wg_ref, wu_ref, wd_ref, wsem = refs[12 + 2 * ng:16 + 2 * ng]
        i = pl.program_id(0)
        if plan is not None:
            comm = (refs[7:7 + ng], refs[12 + ng:12 + 2 * ng], refs[16 + 2 * ng:])
            pl.when(i == 0)(lambda: plan.start(*comm))
        _load_resident(i == 0, refs[4:7], (wg_ref, wu_ref, wd_ref), wsem)

        xf = x_ref[...]
        hb = ((xf * _rms(xf) * gpre_ref[...]) * (1.0 + mod_ref[0, 1:2, :]) + mod_ref[0, 0:1, :]).astype(BF16)
        h_ref[...] = hb
        y0 = None
        for lo, hi in FF_TILES:
            gate = _dot(hb, wg_ref[:, lo:hi])
            up = _dot(hb, wu_ref[:, lo:hi])
            gate_ref[:, lo:hi] = gate.astype(BF16)
            up_ref[:, lo:hi] = up.astype(BF16)
            part = _dot((gate * jax.nn.sigmoid(gate) * up).astype(BF16), wd_ref[lo:hi, :])
            y0 = part if y0 is None else y0 + part
        y0_ref[...] = y0
        xo_ref[...] = xf + (gs * mod_ref[0, 2:3, :]) * (y0 * _rms(y0) * gpost_ref[...])

        if plan is not None:
            pl.when(i == T // tm - 1)(lambda: plan.finish(*comm))

    tok = pl.BlockSpec((tm, D), lambda i: (i, 0))
    vec = pl.BlockSpec((1, D), lambda i: (0, 0))
    hid = pl.BlockSpec((tm, DFF_PAD), lambda i: (i, 0))
    outs = pl.pallas_call(
        body, grid=(T // tm,),
        in_specs=[tok, pl.BlockSpec((1, 3, D), lambda i: ((i * tm) // SEQ, 0, 0)), vec, vec, HBM, HBM, HBM] + [HBM] * ng,
        out_specs=[tok, tok, hid, hid, tok] + [HBM] * ng,
        out_shape=[SDS((T, D), F32), SDS((T, D), BF16), SDS((T, DFF_PAD), BF16), SDS((T, DFF_PAD), BF16), SDS((T, D), F32)]
        + ([] if plan is None else plan.out_shapes(BF16)),
        scratch_shapes=_resident_scratch() + ([] if plan is None else plan.scratch()),
        compiler_params=_cp("arbitrary"), name=name,
    )(x, mod3, g_pre, g_post, wg, wu, wd, *([] if gather is None else gather[0]))
    return outs[:5], outs[5:]


def ffn_bwd(dxo, x, y0, mod3, g_pre, g_post, gate, up, wg, wu, wd, gs, name, scatter=None):
    T = x.shape[0]
    nb = T // SEQ
    tm = TM_BWD
    tiles_per_seq = SEQ // tm
    ns = 0 if scatter is None else len(scatter)

    def body(*refs):
        dxo_ref, x_ref, y0_ref, mod_ref, gpre_ref, gpost_ref, gate_ref, up_ref = refs[:8]
        dx_ref, dy0_ref, act_ref, dgate_ref, dup_ref, dmod_ref, dgpre_ref, dgpost_ref = refs[11 + ns:19 + ns]
        wg_ref, wu_ref, wd_ref, wsem = refs[19 + 2 * ns:23 + 2 * ns]
        i = pl.program_id(0)
        _load_resident(i == 0, refs[8:11], (wg_ref, wu_ref, wd_ref), wsem)
        if ns:
            comm = (refs[11:11 + ns], refs[19 + ns:19 + 2 * ns], *refs[23 + 2 * ns:])

            @pl.when(i == 0)
            def _():
                for cp in _scatter_copies(*comm):
                    cp.start()

        @pl.when(i == 0)
        def _():
            dgpre_ref[...] = jnp.zeros_like(dgpre_ref)
            dgpost_ref[...] = jnp.zeros_like(dgpost_ref)

        @pl.when(i % tiles_per_seq == 0)
        def _():
            dmod_ref[...] = jnp.zeros_like(dmod_ref)

        dxo = dxo_ref[...]
        dy0, dmg, dg = _post_bwd(dxo, y0_ref[...], gpost_ref[...], mod_ref[0, 2:3, :], gs)
        dmod_ref[0, 2:3, :] += dmg
        dgpost_ref[...] += dg
        db = dy0.astype(BF16)
        dy0_ref[...] = db
        dh = None
        for lo, hi in FF_TILES:
            dact = _dot_nt(db, wd_ref[lo:hi, :])
            g = gate_ref[:, lo:hi].astype(F32)
            u = up_ref[:, lo:hi].astype(F32)
            sig = jax.nn.sigmoid(g)
            sl = g * sig
            dgate = (dact * u * (sig * (1.0 + g * (1.0 - sig)))).astype(BF16)
            dup = (dact * sl).astype(BF16)
            act_ref[:, lo:hi] = (sl * u).astype(BF16)
            dgate_ref[:, lo:hi] = dgate
            dup_ref[:, lo:hi] = dup
            part = _dot_nt(dgate, wg_ref[:, lo:hi]) + _dot_nt(dup, wu_ref[:, lo:hi])
            dh = part if dh is None else dh + part
        dx, dsh, dsc, dg = _norm_mod_bwd(dh, x_ref[...], gpre_ref[...], mod_ref[0, 1:2, :])
        dx_ref[...] = dxo + dx
        dmod_ref[0, 0:1, :] += dsh
        dmod_ref[0, 1:2, :] += dsc
        dgpre_ref[...] += dg

        if ns:
            @pl.when(i == T // tm - 1)
            def _():
                for cp in _scatter_copies(*comm):
                    cp.wait()

    tok = pl.BlockSpec((tm, D), lambda i: (i, 0))
    vec = pl.BlockSpec((1, D), lambda i: (0, 0))
    hid = pl.BlockSpec((tm, DFF_PAD), lambda i: (i, 0))
    modspec = pl.BlockSpec((1, 3, D), lambda i: ((i * tm) // SEQ, 0, 0))
    outs = pl.pallas_call(
        body, grid=(T // tm,),
        in_specs=[tok, tok, tok, modspec, vec, vec, hid, hid, HBM, HBM, HBM] + [HBM] * ns,
        out_specs=[tok, tok, hid, hid, hid, modspec, vec, vec] + [HBM] * ns,
        out_shape=[SDS((T, D), F32), SDS((T, D), BF16), SDS((T, DFF_PAD), BF16), SDS((T, DFF_PAD), BF16),
                   SDS((T, DFF_PAD), BF16), SDS((nb, 3, D), F32), SDS((1, D), F32), SDS((1, D), F32)]
        + [SDS((3,) + h.shape[1:], h.dtype) for h in (scatter or [])],
        scratch_shapes=_resident_scratch()
        + ([pltpu.SemaphoreType.DMA((ns, 3)), pltpu.SemaphoreType.DMA((ns, 3))] if ns else []),
        compiler_params=_cp("arbitrary"), name=name,
    )(dxo, x, y0, mod3, g_pre, g_post, gate, up, wg, wu, wd, *(scatter or []))
    return outs[:8], outs[8:]


def matmul_tn(a, b, tm, tn, tk, name):
    T, M = a.shape
    N = b.shape[1]
    nk = T // tk

    def body(a_ref, b_ref, o_ref):
        @pl.when(pl.program_id(2) == 0)
        def _():
            o_ref[...] = jnp.zeros_like(o_ref)

        o_ref[...] += _dot_tn(a_ref[...], b_ref[...])

    return pl.pallas_call(
        body, grid=(M // tm, N // tn, nk),
        in_specs=[pl.BlockSpec((tk, tm), lambda i, j, k: (k, i)), pl.BlockSpec((tk, tn), lambda i, j, k: (k, j))],
        out_specs=pl.BlockSpec((tm, tn), lambda i, j, k: (i, j)),
        out_shape=SDS((M, N), F32),
        compiler_params=_cp("arbitrary", "arbitrary", "arbitrary"), name=name,
    )(a, b)


def loss_grad(y, tgt, name):
    T = y.shape[0]

    def body(y_ref, t_ref, dy_ref, l_ref):
        @pl.when(pl.program_id(0) == 0)
        def _():
            l_ref[...] = jnp.zeros_like(l_ref)

        e = y_ref[...] - t_ref[...]
        dy_ref[...] = e * (1.0 / D)
        l_ref[...] += jnp.sum(e * e) * (0.5 / D)

    tok = pl.BlockSpec((TM, D), lambda i: (i, 0))
    return pl.pallas_call(
        body, grid=(T // TM,), in_specs=[tok, tok],
        out_specs=[tok, pl.BlockSpec((8, LANE), lambda i: (0, 0))],
        out_shape=[SDS((T, D), F32), SDS((8, LANE), F32)],
        compiler_params=_cp("arbitrary"), name=name,
    )(y, tgt)


def rope_tables(pos_col, name):
    T = pos_col.shape[0]
    tm = 1024

    def body(p_ref, c_ref, s1_ref, s2_ref):
        lane = lax.broadcasted_iota(jnp.int32, (1, LANE), 1)
        l64 = lane % HD
        inv_freq = jnp.exp((l64 % 8).astype(F32) * (-math.log(ROPE_THETA) / 8.0))
        ang = p_ref[...].astype(F32) * inv_freq
        cs = jnp.cos(ang)
        sn = jnp.sin(ang)
        c_ref[...] = jnp.where(l64 < 16, cs, 1.0)
        s1_ref[...] = jnp.where(l64 < 8, -sn, 0.0)
        s2_ref[...] = jnp.where((l64 >= 8) & (l64 < 16), sn, 0.0)

    tab = pl.BlockSpec((tm, LANE), lambda i: (i, 0))
    return pl.pallas_call(
        body, grid=(T // tm,), in_specs=[pl.BlockSpec((tm, 1), lambda i: (i, 0))], out_specs=[tab, tab, tab],
        out_shape=[SDS((T, LANE), F32)] * 3, compiler_params=_cp("arbitrary"), name=name,
    )(pos_col)


def mixer_proj(x, mod3, g_pre, w_main, w_f, rc, rs1, rs2, name):
    T = x.shape[0]

    def body(x_ref, mod_ref, g_ref, w_ref, wf_ref, c_ref, s1_ref, s2_ref, h_ref, pa_ref, pb_ref, f_ref):
        xf = x_ref[...]
        h = (xf * _rms(xf) * g_ref[...]) * (1.0 + mod_ref[0, 1:2, :]) + mod_ref[0, 0:1, :]
        hb = h.astype(BF16)
        h_ref[...] = hb
        f_ref[...] = _dot(hb, wf_ref[...])
        c, s1, s2 = c_ref[...], s1_ref[...], s2_ref[...]
        for grp in range(2):
            pr = _dot(hb, w_ref[:, grp * WG:(grp + 1) * WG])
            for k in range(WG // LANE):
                t = pr[:, k * LANE:(k + 1) * LANE]
                pa_ref[:, grp * WG + k * LANE:grp * WG + (k + 1) * LANE] = (
                    t * c + pltpu.roll(t, LANE - 8, 1) * s1 + pltpu.roll(t, 8, 1) * s2)
        pa_ref[:, 2 * WG:3 * WG] = _dot(hb, w_ref[:, 2 * WG:3 * WG])
        for grp in range(3):
            pb_ref[:, grp * WG:(grp + 1) * WG] = _dot(hb, w_ref[:, (3 + grp) * WG:(4 + grp) * WG]).astype(BF16)

    tok = pl.BlockSpec((TM, D), lambda i: (i, 0))
    vec = pl.BlockSpec((1, D), lambda i: (0, 0))
    tab = pl.BlockSpec((TM, LANE), lambda i: (i, 0))
    grp3 = pl.BlockSpec((TM, 3 * WG), lambda i: (i, 0))
    return pl.pallas_call(
        body, grid=(T // TM,),
        in_specs=[tok, pl.BlockSpec((1, 3, D), _mod_map), vec, pl.BlockSpec((D, IN_MAIN), lambda i: (0, 0)),
                  pl.BlockSpec((D, LANE), lambda i: (0, 0)), tab, tab, tab],
        out_specs=[tok, grp3, grp3, tab],
        out_shape=[SDS((T, D), BF16), SDS((T, 3 * WG), F32), SDS((T, 3 * WG), BF16), SDS((T, LANE), F32)],
        compiler_params=_cp("arbitrary"), name=name,
    )(x, mod3, g_pre, w_main, w_f, rc, rs1, rs2)


def _head_lanes():
    return lax.broadcasted_iota(jnp.int32, (1, LANE), 1) < HD


def _pair(m0, a, b):
    return jnp.where(m0, a, b)


def _band_rows(i, d, nbc):
    if nbc == 1:
        return i, i, 0
    r, mb = i // nbc, i % nbc
    return r + mb * (QB * d), r + jnp.maximum(mb - 1, 0) * (QB * d), jnp.where(mb > 0, QB, 0)


def _rows(start, size, d):
    return pl.ds(pl.multiple_of(start, QB), size) if d == 1 else pl.ds(start, size, stride=d)


def _band_valid(span, off):
    rq = lax.broadcasted_iota(jnp.int32, (QB, span), 0)
    rel = lax.broadcasted_iota(jnp.int32, (QB, span), 1) - off
    return (rel <= rq) & (rel >= rq - QB)


def band_fwd(pa, name):
    T = pa.shape[0]
    B = T // SEQ
    NP = WG // LANE

    def body(q_ref, k_ref, v_ref, out_ref, lse_ref, o_s, l_s):
        m0 = _head_lanes()
        for pidx, (d, nbc) in enumerate(PATTERNS):
            span = QB if nbc == 1 else 2 * QB

            def blk(it, carry, pidx=pidx, d=d, nbc=nbc, span=span):
                ld = []
                for u in range(BAND_UNROLL):
                    qs, ks, off = _band_rows(it * BAND_UNROLL + u, d, nbc)
                    q = q_ref[_rows(qs, QB, d), :] * ATTN_SCALE
                    ld.append((qs, q, k_ref[_rows(ks, span, d), :].astype(BF16), v_ref[_rows(ks, span, d), :].astype(BF16),
                               _band_valid(span, off)))
                ss = [[jnp.where(valid, _dot_nt(jnp.where(mh, q, 0.0).astype(BF16), k), NEG) for mh in (m0, jnp.logical_not(m0))]
                      for _, q, k, _, valid in ld]
                ps = []
                for pair in ss:
                    row = []
                    for s in pair:
                        m = jnp.max(s, axis=-1, keepdims=True)
                        p = jnp.exp(s - m)
                        row.append((p.astype(BF16), jnp.sum(p, axis=-1, keepdims=True), m))
                    ps.append(row)
                pv = [[_dot(p, ld[u][3]) for p, _, _ in ps[u]] for u in range(BAND_UNROLL)]
                for u in range(BAND_UNROLL):
                    rows = _rows(ld[u][0], QB, d)
                    (_, l0, mx0), (_, l1, mx1) = ps[u]
                    o_s[pidx, rows, :] = _pair(m0, pv[u][0] / l0, pv[u][1] / l1)
                    l_s[pidx, rows, :] = _pair(m0, mx0 + jnp.log(l0), mx1 + jnp.log(l1))
                return carry

            lax.fori_loop(0, SEQ // QB // BAND_UNROLL, blk, 0)
        for c in range(SEQ // FB):
            sl = slice(c * FB, (c + 1) * FB)
            a, b, e = l_s[0, sl, :], l_s[1, sl, :], l_s[2, sl, :]
            m = jnp.maximum(jnp.maximum(a, b), e)
            L = m + jnp.log(jnp.exp(a - m) + jnp.exp(b - m) + jnp.exp(e - m))
            out_ref[sl, :] = jnp.exp(a - L) * o_s[0, sl, :] + jnp.exp(b - L) * o_s[1, sl, :] + jnp.exp(e - L) * o_s[2, sl, :]
            lse_ref[sl, :] = L

    blk_of = lambda g: pl.BlockSpec((SEQ, LANE), lambda b, hp, g=g: (b, g * NP + hp))
    return pl.pallas_call(
        body, grid=(B, NP), in_specs=[blk_of(0), blk_of(1), blk_of(2)], out_specs=[blk_of(0), blk_of(0)],
        out_shape=[SDS((T, WG), F32), SDS((T, WG), F32)],
        scratch_shapes=[pltpu.VMEM((3, SEQ, LANE), F32), pltpu.VMEM((3, SEQ, LANE), F32)],
        compiler_params=_cp("arbitrary", "arbitrary"), name=name,
    )(pa, pa, pa)


def _pair_rowsum(m0, prod):
    s0 = jnp.sum(jnp.where(m0, prod, 0.0), axis=-1, keepdims=True)
    return _pair(m0, s0, jnp.sum(prod, axis=-1, keepdims=True) - s0)


def band_bwd(pa, do, out, lse, name):
    T = pa.shape[0]
    B = T // SEQ
    NP = WG // LANE

    def body(q_ref, k_ref, v_ref, do_ref, out_ref, l_ref, dq_ref, dk_ref, dv_ref, d_s):
        m0 = _head_lanes()
        dq_ref[...] = jnp.zeros_like(dq_ref)
        dk_ref[...] = jnp.zeros_like(dk_ref)
        dv_ref[...] = jnp.zeros_like(dv_ref)
        for c in range(SEQ // FB):
            sl = slice(c * FB, (c + 1) * FB)
            d_s[sl, :] = _pair_rowsum(m0, do_ref[sl, :] * out_ref[sl, :])
        for d, nbc in PATTERNS:
            span = QB if nbc == 1 else 2 * QB

            def blk(it, carry, d=d, nbc=nbc, span=span):
                masks = (m0, jnp.logical_not(m0))
                ld = []
                for u in range(BAND_UNROLL_BWD):
                    qs, ks, off = _band_rows(it * BAND_UNROLL_BWD + u, d, nbc)
                    qrow, krow = _rows(qs, QB, d), _rows(ks, span, d)
                    ld.append(dict(qrow=qrow, krow=krow, q=q_ref[qrow, :] * ATTN_SCALE, k=k_ref[krow, :].astype(BF16),
                                   v=v_ref[krow, :].astype(BF16), do=do_ref[qrow, :], l=l_ref[qrow, :], dv=d_s[qrow, :],
                                   valid=_band_valid(span, off)))
                for t in ld:
                    t["qm"] = [jnp.where(mh, t["q"], 0.0).astype(BF16) for mh in masks]
                    t["dom"] = [jnp.where(mh, t["do"], 0.0).astype(BF16) for mh in masks]
                sd = [[(jnp.where(t["valid"], _dot_nt(t["qm"][h], t["k"]), NEG), _dot_nt(t["dom"][h], t["v"])) for h in range(2)]
                      for t in ld]
                pd = []
                for t, pair in zip(ld, sd):
                    row = []
                    for h, (s, dp) in enumerate(pair):
                        col = slice(h * HD, h * HD + 1)
                        p = jnp.exp(s - t["l"][:, col])
                        row.append((p.astype(BF16), (p * (dp - t["dv"][:, col])).astype(BF16)))
                    pd.append(row)
                gr = [(_dot(row[0][1], t["k"]), _dot(row[1][1], t["k"]),
                       _dot_tn(jnp.concatenate([row[0][1], row[1][1]], axis=0), jnp.concatenate(t["qm"], axis=0)),
                       _dot_tn(jnp.concatenate([row[0][0], row[1][0]], axis=0), jnp.concatenate(t["dom"], axis=0)))
                      for t, row in zip(ld, pd)]
                for t, (dq0, dq1, dk, dv) in zip(ld, gr):
                    dq_ref[t["qrow"], :] += _pair(m0, dq0, dq1) * ATTN_SCALE
                    dk_ref[t["krow"], :] += dk
                    dv_ref[t["krow"], :] += dv
                return carry

            lax.fori_loop(0, SEQ // QB // BAND_UNROLL_BWD, blk, 0)

    blk_of = lambda g: pl.BlockSpec((SEQ, LANE), lambda b, hp, g=g: (b, g * NP + hp))
    return pl.pallas_call(
        body, grid=(B, NP), in_specs=[blk_of(0), blk_of(1), blk_of(2), blk_of(0), blk_of(0), blk_of(0)],
        out_specs=[blk_of(0)] * 3, out_shape=[SDS((T, WG), F32)] * 3,
        scratch_shapes=[pltpu.VMEM((SEQ, LANE), F32)],
        compiler_params=_cp("arbitrary", "arbitrary"), name=name,
    )(pa, pa, pa, do, out, lse)


def _tile_causal(nq, nk, q0, k0):
    r = lax.broadcasted_iota(jnp.int32, (nq, nk), 0)
    c = lax.broadcasted_iota(jnp.int32, (nq, nk), 1)
    return r + (q0 - k0) >= c


def _row_to_col(row):
    n = row.shape[1]
    return jnp.transpose(jnp.broadcast_to(row, (LANE, n)))[:, 0:1]


def _col_to_row(col):
    n = col.shape[0]
    return jnp.transpose(jnp.broadcast_to(col, (n, LANE)))[0:1, :]


def fox_fwd(pb, fblk, frow, name):
    T = pb.shape[0]
    B = T // SEQ
    NP = WG // LANE
    n = SEQ // FB

    def body(q_ref, k_ref, v_ref, fc_ref, fr_ref, o_ref, lse_ref):
        i = pl.program_id(2)
        m0 = _head_lanes()
        q = q_ref[...] * ATTN_SCALE
        zero = jnp.zeros_like(q)
        qh = (jnp.where(m0, q, zero), jnp.where(m0, zero, q))
        fq = (_row_to_col(fc_ref[0, 0, 0]), _row_to_col(fc_ref[0, 1, 0]))

        def step(t, carry, masked):
            rows = pl.ds(pl.multiple_of(t * FT, FT), FT)
            kt = k_ref[rows, :]
            vt = v_ref[rows, :]
            ss = [_dot_nt(qh[h], kt) + fq[h] - fr_ref[0, h, t] for h in range(2)]
            if masked:
                ok = _tile_causal(FB, FT, i * FB, t * FT)
                ss = [jnp.where(ok, s, NEG) for s in ss]
            st = []
            for h in range(2):
                m, l, _ = carry[h]
                m2 = jnp.maximum(m, jnp.max(ss[h], axis=-1, keepdims=True))
                a = jnp.exp(m - m2)
                p = jnp.exp(ss[h] - m2)
                st.append((m2, a, a * l + jnp.sum(p, axis=-1, keepdims=True), p.astype(BF16)))
            pv = [_dot(st[h][3], vt) for h in range(2)]
            return tuple((st[h][0], st[h][2], st[h][1] * carry[h][2] + pv[h]) for h in range(2))

        one = (jnp.full((FB, 1), NEG, F32), jnp.zeros((FB, 1), F32), jnp.zeros((FB, LANE), F32))
        last = (i * FB) // FT
        carry = lax.fori_loop(0, last, lambda t, cr: step(t, cr, False), (one, one))
        (ma, la, acca), (mb, lb, accb) = step(last, carry, True)
        o_ref[...] = _pair(m0, acca / la, accb / lb)
        lse_ref[0, 0, 0] = _col_to_row(ma + jnp.log(la))
        lse_ref[0, 1, 0] = _col_to_row(mb + jnp.log(lb))

    qblk = pl.BlockSpec((FB, LANE), lambda b, hp, i: (b * n + i, hp))
    full = lambda g: pl.BlockSpec((SEQ, LANE), lambda b, hp, i, g=g: (b, g * NP + hp))
    rowb = pl.BlockSpec((1, 2, 1, 1, FB), lambda b, hp, i: (b, hp, i, 0, 0))
    return pl.pallas_call(
        body, grid=(B, NP, n),
        in_specs=[qblk, full(1), full(2), rowb, pl.BlockSpec((1, 2, SEQ // FT, 1, FT), lambda b, hp, i: (b, hp, 0, 0, 0))],
        out_specs=[qblk, rowb], out_shape=[SDS((T, WG), F32), SDS((B, NH, n, 1, FB), F32)],
        compiler_params=_cp("arbitrary", "arbitrary", "arbitrary"), name=name,
    )(pb, pb, pb, fblk, frow)


def fox_bwd(pb, do, lrow, drow, fblk, frow, name):
    T = pb.shape[0]
    B = T // SEQ
    NP = WG // LANE
    n = SEQ // FB

    def body(q_ref, k_ref, v_ref, do_ref, l_ref, d_ref, fc_ref, fr_ref, dq_ref, dk_ref, dv_ref, dfq_ref, dfk_ref):
        j = pl.program_id(2)
        m0 = _head_lanes()
        masks = (m0, jnp.logical_not(m0))

        @pl.when(j == 0)
        def _():
            dq_ref[...] = jnp.zeros_like(dq_ref)
            dfq_ref[...] = jnp.zeros_like(dfq_ref)

        kj = k_ref[...]
        vj = v_ref[...]
        fk = (_row_to_col(fc_ref[0, 0, 0]), _row_to_col(fc_ref[0, 1, 0]))

        def step(t, carry, masked):
            rows = pl.ds(pl.multiple_of(t * FT, FT), FT)
            qt = q_ref[rows, :] * ATTN_SCALE
            dot_ = do_ref[rows, :]
            zero = jnp.zeros_like(qt)
            qm = [jnp.where(mh, qt, zero) for mh in masks]
            dom = [jnp.where(mh, dot_, 0.0).astype(BF16) for mh in masks]
            ss = [_dot_nt(kj, qm[h]) + fr_ref[0, h, t] - fk[h] for h in range(2)]
            dps = [_dot_nt(vj, dom[h]) for h in range(2)]
            if masked:
                key = lax.broadcasted_iota(jnp.int32, (FB, FT), 0)
                qry = lax.broadcasted_iota(jnp.int32, (FB, FT), 1)
                ok = qry + (t * FT - j * FB) >= key
                ss = [jnp.where(ok, s, NEG) for s in ss]
            pds = []
            for h in range(2):
                p = jnp.exp(ss[h] - l_ref[0, h, t])
                ds = p * (dps[h] - d_ref[0, h, t])
                dfq_ref[0, h, t] += jnp.sum(ds, axis=0, keepdims=True)
                pds.append((p.astype(BF16), ds.astype(BF16), jnp.sum(ds, axis=-1, keepdims=True)))
            dks = [_dot(pds[h][1], qm[h]) for h in range(2)]
            dvs = [_dot(pds[h][0], dom[h]) for h in range(2)]
            dqs = [_dot_tn(pds[h][1], kj) for h in range(2)]
            dq_ref[rows, :] += _pair(m0, dqs[0], dqs[1]) * ATTN_SCALE
            return tuple((carry[h][0] + dks[h], carry[h][1] + dvs[h], carry[h][2] - pds[h][2]) for h in range(2))

        one = (jnp.zeros((FB, LANE), F32), jnp.zeros((FB, LANE), F32), jnp.zeros((FB, 1), F32))
        first = (j * FB) // FT
        carry = step(first, (one, one), True)
        (dka, dva, dfka), (dkb, dvb, dfkb) = lax.fori_loop(first + 1, SEQ // FT, lambda t, cr: step(t, cr, False), carry)
        dk_ref[...] = _pair(m0, dka, dkb)
        dv_ref[...] = _pair(m0, dva, dvb)
        dfk_ref[0, 0, 0] = _col_to_row(dfka)
        dfk_ref[0, 1, 0] = _col_to_row(dfkb)

    kblk = lambda g: pl.BlockSpec((FB, LANE), lambda b, hp, j, g=g: (b * n + j, g * NP + hp))
    full = pl.BlockSpec((SEQ, LANE), lambda b, hp, j: (b, hp))
    rowf = pl.BlockSpec((1, 2, SEQ // FT, 1, FT), lambda b, hp, j: (b, hp, 0, 0, 0))
    rowb = pl.BlockSpec((1, 2, 1, 1, FB), lambda b, hp, j: (b, hp, j, 0, 0))
    return pl.pallas_call(
        body, grid=(B, NP, n), in_specs=[full, kblk(1), kblk(2), full, rowf, rowf, rowb, rowf],
        out_specs=[full, kblk(0), kblk(0), rowf, rowb],
        out_shape=[SDS((T, WG), F32), SDS((T, WG), F32), SDS((T, WG), F32), SDS((B, NH, SEQ // FT, 1, FT), F32),
                   SDS((B, NH, n, 1, FB), F32)],
        compiler_params=_cp("arbitrary", "arbitrary", "arbitrary"), name=name,
    )(pb, pb, pb, do, lrow, drow, fblk, frow)


def _tri(lower):
    r = lax.broadcasted_iota(jnp.int32, (LANE, LANE), 0)
    c = lax.broadcasted_iota(jnp.int32, (LANE, LANE), 1)
    return ((r >= c) if lower else (r <= c)).astype(F32)


def _tri_dot(t, xblk):
    return jnp.dot(t, xblk, precision=lax.Precision.HIGHEST, preferred_element_type=F32)


def forget_cumsum(flog, bias, name):
    B, S, _ = flog.shape

    def body(f_ref, b_ref, o_ref):
        t = _tri(True)
        carry = jnp.zeros((1, LANE), F32)
        for blk in range(S // LANE):
            z = f_ref[0, blk * LANE:(blk + 1) * LANE, :] + b_ref[...]
            lf = jnp.minimum(z, 0.0) - jnp.log(1.0 + jnp.exp(-jnp.abs(z)))
            cs = _tri_dot(t, lf) + carry
            o_ref[0, blk * LANE:(blk + 1) * LANE, :] = cs
            carry = cs[LANE - 1:LANE, :]

    spec = pl.BlockSpec((1, S, LANE), lambda b: (b, 0, 0))
    return pl.pallas_call(
        body, grid=(B,), in_specs=[spec, pl.BlockSpec((1, LANE), lambda b: (0, 0))], out_specs=spec,
        out_shape=SDS((B, S, LANE), F32), compiler_params=_cp("arbitrary"), name=name,
    )(flog, bias)


def forget_cumsum_bwd(dF, flog, bias, name):
    B, S, _ = flog.shape

    def body(d_ref, f_ref, b_ref, o_ref, db_ref):
        @pl.when(pl.program_id(0) == 0)
        def _():
            db_ref[...] = jnp.zeros_like(db_ref)

        t = _tri(False)
        carry = jnp.zeros((1, LANE), F32)
        tot = jnp.zeros((1, LANE), F32)
        for blk in reversed(range(S // LANE)):
            sl = slice(blk * LANE, (blk + 1) * LANE)
            rc = _tri_dot(t, d_ref[0, sl, :]) + carry
            carry = rc[0:1, :]
            z = f_ref[0, sl, :] + b_ref[...]
            dz = rc * jax.nn.sigmoid(-z)
            o_ref[0, sl, :] = dz
            tot = tot + jnp.sum(dz, axis=0, keepdims=True)
        db_ref[...] += tot

    spec = pl.BlockSpec((1, S, LANE), lambda b: (b, 0, 0))
    vec = pl.BlockSpec((1, LANE), lambda b: (0, 0))
    return pl.pallas_call(
        body, grid=(B,), in_specs=[spec, spec, vec], out_specs=[spec, vec],
        out_shape=[SDS((B, S, LANE), F32), SDS((1, LANE), F32)], compiler_params=_cp("arbitrary"), name=name,
    )(dF, flog, bias)


def mixer_out_fwd(oa, ob, goa, gob, w_out, g_post, x, mod3, name):
    T = x.shape[0]

    def body(oa_ref, ob_ref, goa_ref, gob_ref, w_ref, gp_ref, x_ref, mod_ref, xo_ref, mg_ref, y0_ref):
        a = oa_ref[...]
        b = ob_ref[...]
        mg = jnp.concatenate([a * _rms(a) * goa_ref[...], b * _rms(b) * gob_ref[...]], axis=-1).astype(BF16)
        mg_ref[...] = mg
        y0 = _dot(mg, w_ref[...])
        y0_ref[...] = y0
        xo_ref[...] = x_ref[...] + mod_ref[0, 2:3, :] * (y0 * _rms(y0) * gp_ref[...])

    tok = pl.BlockSpec((TM, D), lambda i: (i, 0))
    half = pl.BlockSpec((TM, WG), lambda i: (i, 0))
    hv = pl.BlockSpec((1, WG), lambda i: (0, 0))
    return pl.pallas_call(
        body, grid=(T // TM,),
        in_specs=[half, half, hv, hv, pl.BlockSpec((D, D), lambda i: (0, 0)), pl.BlockSpec((1, D), lambda i: (0, 0)), tok,
                  pl.BlockSpec((1, 3, D), _mod_map)],
        out_specs=[tok, tok, tok], out_shape=[SDS((T, D), F32), SDS((T, D), BF16), SDS((T, D), F32)],
        compiler_params=_cp("arbitrary"), name=name,
    )(oa, ob, goa, gob, w_out, g_post, x, mod3)


def mixer_out_bwd(dxo, y0, mod3, g_post, w_out, oa, ob, goa, gob, name):
    T = dxo.shape[0]
    nb = T // SEQ
    tiles_per_seq = SEQ // TM

    def body(dxo_ref, y0_ref, mod_ref, gp_ref, w_ref, oa_ref, ob_ref, goa_ref, gob_ref,
             dy0_ref, doa_ref, dob_ref, dmg_ref, dgp_ref, dgoa_ref, dgob_ref, dvb_ref):
        i = pl.program_id(0)

        @pl.when(i == 0)
        def _():
            dgp_ref[...] = jnp.zeros_like(dgp_ref)
            dgoa_ref[...] = jnp.zeros_like(dgoa_ref)
            dgob_ref[...] = jnp.zeros_like(dgob_ref)

        @pl.when(i % tiles_per_seq == 0)
        def _():
            dmg_ref[...] = jnp.zeros_like(dmg_ref)

        dy0, dmg, dg = _post_bwd(dxo_ref[...], y0_ref[...], gp_ref[...], mod_ref[0, 2:3, :], 1.0)
        dmg_ref[0] += dmg
        dgp_ref[...] += dg
        db = dy0.astype(BF16)
        dy0_ref[...] = db
        dm = _dot_nt(db, w_ref[...])
        for o_ref, g_ref, do_ref, dg_ref, sl in ((oa_ref, goa_ref, doa_ref, dgoa_ref, slice(0, WG)),
                                                  (ob_ref, gob_ref, dob_ref, dgob_ref, slice(WG, 2 * WG))):
            o = o_ref[...]
            r = _rms(o)
            oh = o * r
            d = dm[:, sl]
            dg_ref[...] += jnp.sum(d * oh, axis=0, keepdims=True)
            dh = d * g_ref[...]
            do = r * (dh - oh * jnp.mean(dh * oh, axis=-1, keepdims=True))
            do_ref[...] = do
        ind = (lax.broadcasted_iota(jnp.int32, (WG, LANE), 0) // HD == lax.broadcasted_iota(jnp.int32, (WG, LANE), 1)).astype(BF16)
        prod = do * o
        hi = prod.astype(BF16)
        dvb_ref[...] = _dot(hi, ind) + _dot((prod - hi.astype(F32)).astype(BF16), ind)

    tok = pl.BlockSpec((TM, D), lambda i: (i, 0))
    half = pl.BlockSpec((TM, WG), lambda i: (i, 0))
    hv = pl.BlockSpec((1, WG), lambda i: (0, 0))
    vec = pl.BlockSpec((1, D), lambda i: (0, 0))
    return pl.pallas_call(
        body, grid=(T // TM,),
        in_specs=[tok, tok, pl.BlockSpec((1, 3, D), _mod_map), vec, pl.BlockSpec((D, D), lambda i: (0, 0)), half, half, hv, hv],
        out_specs=[tok, half, half, pl.BlockSpec((1, 1, D), _mod_map), vec, hv, hv, pl.BlockSpec((TM, LANE), lambda i: (i, 0))],
        out_shape=[SDS((T, D), BF16), SDS((T, WG), F32), SDS((T, WG), F32), SDS((nb, 1, D), F32), SDS((1, D), F32),
                   SDS((1, WG), F32), SDS((1, WG), F32), SDS((T, LANE), F32)],
        compiler_params=_cp("arbitrary"), name=name,
    )(dxo, y0, mod3, g_post, w_out, oa, ob, goa, gob)


def proj_grad_assemble(grads, rc, rs1, rs2, name):
    T = grads[0].shape[0]

    def body(*refs):
        ins, (c_ref, s1_ref, s2_ref, o_ref) = refs[:6], refs[6:]
        c, s1, s2 = c_ref[...], s1_ref[...], s2_ref[...]
        for grp in range(2):
            for k in range(WG // LANE):
                d = ins[grp][:, k * LANE:(k + 1) * LANE]
                un = d * c + pltpu.roll(d * s1, 8, 1) + pltpu.roll(d * s2, LANE - 8, 1)
                o_ref[:, grp * WG + k * LANE:grp * WG + (k + 1) * LANE] = un.astype(BF16)
        for g in range(2, 6):
            o_ref[:, g * WG:(g + 1) * WG] = ins[g][...].astype(BF16)

    half = pl.BlockSpec((TM, WG), lambda i: (i, 0))
    tab = pl.BlockSpec((TM, LANE), lambda i: (i, 0))
    return pl.pallas_call(
        body, grid=(T // TM,), in_specs=[half] * 6 + [tab] * 3, out_specs=pl.BlockSpec((TM, IN_MAIN), lambda i: (i, 0)),
        out_shape=SDS((T, IN_MAIN), BF16), compiler_params=_cp("arbitrary"), name=name,
    )(*grads, rc, rs1, rs2)


def mixer_proj_bwd(dproj, dflog, dxo, x, mod3, g_pre, w_main, w_f, name):
    T = x.shape[0]
    nb = T // SEQ
    tiles_per_seq = SEQ // TM

    def body(dp_ref, df_ref, dxo_ref, x_ref, mod_ref, g_ref, w_ref, wf_ref, dx_ref, dmod_ref, dg_ref):
        i = pl.program_id(0)

        @pl.when(i == 0)
        def _():
            dg_ref[...] = jnp.zeros_like(dg_ref)

        @pl.when(i % tiles_per_seq == 0)
        def _():
            dmod_ref[...] = jnp.zeros_like(dmod_ref)

        dh = _dot_nt(dp_ref[...], w_ref[...]) + _dot_nt(df_ref[...].astype(BF16), wf_ref[...])
        dx, dsh, dsc, dg = _norm_mod_bwd(dh, x_ref[...], g_ref[...], mod_ref[0, 1:2, :])
        dx_ref[...] = dxo_ref[...] + dx
        dmod_ref[0, 0:1, :] += dsh
        dmod_ref[0, 1:2, :] += dsc
        dg_ref[...] += dg

    tok = pl.BlockSpec((TM, D), lambda i: (i, 0))
    vec = pl.BlockSpec((1, D), lambda i: (0, 0))
    return pl.pallas_call(
        body, grid=(T // TM,),
        in_specs=[pl.BlockSpec((TM, IN_MAIN), lambda i: (i, 0)), pl.BlockSpec((TM, LANE), lambda i: (i, 0)), tok, tok,
                  pl.BlockSpec((1, 3, D), _mod_map), vec, pl.BlockSpec((D, IN_MAIN), lambda i: (0, 0)),
                  pl.BlockSpec((D, LANE), lambda i: (0, 0))],
        out_specs=[tok, pl.BlockSpec((1, 2, D), _mod_map), vec],
        out_shape=[SDS((T, D), F32), SDS((nb, 2, D), F32), SDS((1, D), F32)],
        compiler_params=_cp("arbitrary"), name=name,
    )(dproj, dflog, dxo, x, mod3, g_pre, w_main, w_f)


def ada_fwd(c_all, w, b, name):
    n = w.shape[1]
    tn = n // 2

    def body(c_ref, w_ref, b_ref, o_ref):
        cv = c_ref[...]
        o_ref[...] = _dot((cv * jax.nn.sigmoid(cv)).astype(BF16), w_ref[...].astype(BF16)) + b_ref[...]

    R = c_all.shape[0]
    return pl.pallas_call(
        body, grid=(2,),
        in_specs=[pl.BlockSpec((R, D), lambda j: (0, 0)), pl.BlockSpec((D, tn), lambda j: (0, j)), pl.BlockSpec((1, tn), lambda j: (0, j))],
        out_specs=pl.BlockSpec((R, tn), lambda j: (0, j)), out_shape=SDS((R, n), F32),
        compiler_params=_cp("arbitrary"), name=name,
    )(c_all, w, b)


def ada_bwd(c_all, dmod, name):
    R, n = dmod.shape
    tn = n // 2

    def body(c_ref, d_ref, o_ref):
        cv = c_ref[...]
        o_ref[...] = _dot_tn((cv * jax.nn.sigmoid(cv)).astype(BF16), d_ref[...].astype(BF16))

    return pl.pallas_call(
        body, grid=(2,), in_specs=[pl.BlockSpec((R, D), lambda j: (0, 0)), pl.BlockSpec((R, tn), lambda j: (0, j))],
        out_specs=pl.BlockSpec((D, tn), lambda j: (0, j)), out_shape=SDS((D, n), F32),
        compiler_params=_cp("arbitrary"), name=name,
    )(c_all, dmod)


def _adam_math(w, g, m, v):
    m2 = ADAM_B1 * m + (1.0 - ADAM_B1) * g
    v2 = ADAM_B2 * v + (1.0 - ADAM_B2) * (g * g)
    m_hat = m2 / (1.0 - ADAM_B1 ** ADAM_STEP)
    v_hat = v2 / (1.0 - ADAM_B2 ** ADAM_STEP)
    delta = -ADAM_LR * (m_hat / (jnp.sqrt(v_hat) + ADAM_EPS) + ADAM_WD * w)
    return delta, m2, v2


def adam_update(w, g, m, v, tr, name):
    R, C = w.shape

    def body(w_ref, g_ref, m_ref, v_ref, d_ref, mo_ref, vo_ref):
        d_ref[...], mo_ref[...], vo_ref[...] = _adam_math(w_ref[...], g_ref[...], m_ref[...], v_ref[...])

    spec = pl.BlockSpec((tr, C), lambda i: (i, 0))
    return pl.pallas_call(
        body, grid=(R // tr,), in_specs=[spec] * 4, out_specs=[spec] * 3, out_shape=[SDS((R, C), F32)] * 3,
        compiler_params=_cp("arbitrary"), name=name,
    )(w, g, m, v)


def vec_adam(parts, w, m, v, name):
    P, C = parts.shape

    def body(p_ref, w_ref, m_ref, v_ref, g_ref, d_ref, mo_ref, vo_ref):
        g = jnp.sum(p_ref[...], axis=0, keepdims=True)
        g_ref[...] = g
        d_ref[...], mo_ref[...], vo_ref[...] = _adam_math(w_ref[...], g, m_ref[...], v_ref[...])

    return pl.pallas_call(body, out_shape=[SDS((1, C), F32)] * 4, compiler_params=_cp(), name=name)(parts, w, m, v)


HBM = pl.BlockSpec(memory_space=pltpu.HBM)
VMEM = pl.BlockSpec(memory_space=pltpu.VMEM)


def _place():
    x, y, c = lax.axis_index("x"), lax.axis_index("y"), lax.axis_index("c")
    return x, y, c, [(1 - x, y), (x, 1 - y), (1 - x, 1 - y)]


def all_gather8(xs, name):
    R, C = xs.shape

    def body(x_ref, out_ref, send_sems, recv_sems, local_sem):
        x, y, c, chips = _place()
        me, sibling = (x, y, c), (x, y, 1 - c)

        def slot(px, py, pc):
            return out_ref.at[4 * px + 2 * py + pc]

        def copy(k, block, to, src=None):
            return pltpu.make_async_remote_copy(
                src_ref=slot(*block) if src is None else src, dst_ref=slot(*block),
                send_sem=send_sems.at[k], recv_sem=recv_sems.at[k], device_id=to, device_id_type=MESH)

        mine = pltpu.make_async_copy(x_ref, slot(*me), local_sem)
        mine.start()
        first = [copy(0, me, sibling, src=x_ref)]
        first += [copy(1 + j, me, (*chip, c), src=x_ref) for j, chip in enumerate(chips)]
        for cp in first:
            cp.start()
        passed = [copy(4 + j, (*chip, c), sibling) for j, chip in enumerate(chips)]
        for j, chip in enumerate(chips):
            copy(1 + j, (*chip, c), me).wait_recv()
            passed[j].start()
        copy(0, sibling, me).wait_recv()
        for j, chip in enumerate(chips):
            copy(4 + j, (*chip, 1 - c), me).wait_recv()
        for cp in first + passed:
            cp.wait_send()
        mine.wait()

    return pl.pallas_call(
        body, out_shape=SDS((N_DEV, R, C), xs.dtype), in_specs=[VMEM], out_specs=VMEM,
        scratch_shapes=[pltpu.SemaphoreType.DMA((7,)), pltpu.SemaphoreType.DMA((7,)), pltpu.SemaphoreType.DMA],
        compiler_params=pltpu.CompilerParams(vmem_limit_bytes=VMEM_LIMIT), name=name,
    )(xs)


class ShardGather:
    def __init__(self, shapes, splits):
        self.shapes, self.splits, self.n = shapes, splits, len(shapes)

    def scratch(self):
        n = self.n
        return [pltpu.SemaphoreType.DMA((n, 6)), pltpu.SemaphoreType.DMA((n, 6)), pltpu.SemaphoreType.DMA((n,))]

    def out_shapes(self, dtype):
        return [SDS((N_SHARD,) + tuple(s), dtype) for s in self.shapes]

    def _half(self, ref, k, cc):
        lo, hi = (0, self.splits[k]) if cc == 0 else (self.splits[k], self.shapes[k][0])
        return ref.at[pl.ds(lo, hi - lo)]

    def _phase(self, w_refs, o_refs, sems, finish):
        send_sems, recv_sems, local_sems = sems
        x, y, c, chips = _place()
        sibling = (x, y, 1 - c)
        me_s = 2 * x + y

        def rcopy(src, dst, k, s, to):
            return pltpu.make_async_remote_copy(src_ref=src, dst_ref=dst, send_sem=send_sems.at[k, s],
                                                recv_sem=recv_sems.at[k, s], device_id=to, device_id_type=MESH)

        for cc in (0, 1):
            @pl.when(c == cc)
            def _():
                local = [pltpu.make_async_copy(w_refs[k], o_refs[k].at[me_s], local_sems.at[k]) for k in range(self.n)]
                first = [rcopy(self._half(w_refs[k], k, cc), self._half(o_refs[k].at[me_s], k, cc), k, j, (*chip, c))
                         for k in range(self.n) for j, chip in enumerate(chips)]
                if not finish:
                    for cp in local + first:
                        cp.start()
                    return
                passed = []
                for k in range(self.n):
                    for j, chip in enumerate(chips):
                        land = self._half(o_refs[k].at[2 * chip[0] + chip[1]], k, cc)
                        rcopy(land, land, k, j, (*chip, c)).wait_recv()
                        f = rcopy(land, land, k, 3 + j, sibling)
                        f.start()
                        passed.append(f)
                for k in range(self.n):
                    for j, chip in enumerate(chips):
                        other = self._half(o_refs[k].at[2 * chip[0] + chip[1]], k, 1 - cc)
                        rcopy(other, other, k, 3 + j, sibling).wait_recv()
                for s in first + passed:
                    s.wait_send()
                for cp in local:
                    cp.wait()

    def start(self, w_refs, o_refs, sems):
        self._phase(w_refs, o_refs, sems, False)

    def finish(self, w_refs, o_refs, sems):
        self._phase(w_refs, o_refs, sems, True)


def all_gather_shards(ws, splits, name):
    n = len(ws)
    plan = ShardGather([w.shape for w in ws], splits)

    def body(*refs):
        plan.start(refs[:n], refs[n:2 * n], refs[2 * n:])
        plan.finish(refs[:n], refs[n:2 * n], refs[2 * n:])

    return pl.pallas_call(
        body, out_shape=plan.out_shapes(ws[0].dtype), in_specs=[HBM] * n, out_specs=[HBM] * n,
        scratch_shapes=plan.scratch(), name=name,
    )(*ws)


def sibling_send_half(gs, name):
    n = len(gs)

    def body(*refs):
        g_refs, o_refs = refs[:n], refs[n:2 * n]
        send_sems, recv_sems = refs[2 * n:]
        x, y, c, _ = _place()
        cps = []
        for k in range(n):
            hr = gs[k].shape[1] // 2
            src = g_refs[k].at[:, pl.ds(pl.multiple_of((1 - c) * hr, 8), hr)]
            cp = pltpu.make_async_remote_copy(src_ref=src, dst_ref=o_refs[k], send_sem=send_sems.at[k], recv_sem=recv_sems.at[k],
                                              device_id=(x, y, 1 - c), device_id_type=MESH)
            cp.start()
            cps.append(cp)
        for cp in cps:
            cp.wait()

    return pl.pallas_call(
        body, out_shape=[SDS((N_SHARD, g.shape[1] // 2, g.shape[2]), g.dtype) for g in gs], in_specs=[HBM] * n, out_specs=[HBM] * n,
        scratch_shapes=[pltpu.SemaphoreType.DMA((n,)), pltpu.SemaphoreType.DMA((n,))], name=name,
    )(*gs)


def _scatter_copies(h_refs, o_refs, send_sems, recv_sems):
    _, _, c, chips = _place()
    return [pltpu.make_async_remote_copy(
        src_ref=h_refs[k].at[2 * chip[0] + chip[1]], dst_ref=o_refs[k].at[j], send_sem=send_sems.at[k, j],
        recv_sem=recv_sems.at[k, j], device_id=(*chip, c), device_id_type=MESH)
        for k in range(len(h_refs)) for j, chip in enumerate(chips)]


def chip_scatter(hs, name):
    n = len(hs)

    def body(*refs):
        cps = _scatter_copies(refs[:n], refs[n:2 * n], *refs[2 * n:])
        for cp in cps:
            cp.start()
        for cp in cps:
            cp.wait()

    return pl.pallas_call(
        body, out_shape=[SDS((3,) + h.shape[1:], h.dtype) for h in hs], in_specs=[HBM] * n, out_specs=[HBM] * n,
        scratch_shapes=[pltpu.SemaphoreType.DMA((n, 3)), pltpu.SemaphoreType.DMA((n, 3))], name=name,
    )(*hs)


def sibling_swap(ghs, name):
    n = len(ghs)

    def body(*refs):
        g_refs, o_refs = refs[:n], refs[n:2 * n]
        send_sems, recv_sems = refs[2 * n:]
        x, y, c, _ = _place()
        cps = []
        for k in range(n):
            cp = pltpu.make_async_remote_copy(src_ref=g_refs[k], dst_ref=o_refs[k], send_sem=send_sems.at[k],
                                              recv_sem=recv_sems.at[k], device_id=(x, y, 1 - c), device_id_type=MESH)
            cp.start()
            cps.append(cp)
        for cp in cps:
            cp.wait()

    return pl.pallas_call(
        body, out_shape=[SDS(g.shape, g.dtype) for g in ghs], in_specs=[HBM] * n, out_specs=[HBM] * n,
        scratch_shapes=[pltpu.SemaphoreType.DMA((n,)), pltpu.SemaphoreType.DMA((n,))], name=name,
    )(*ghs)


def pair_sum(g, ra, cidx, name):
    _, r, cols = g.shape
    hr = r // 2

    def body(c_ref, g_ref, a_ref, o_ref):
        o_ref[...] = (g_ref[...] + a_ref[...]).astype(BF16)

    return pl.pallas_call(
        body,
        grid_spec=pltpu.PrefetchScalarGridSpec(
            num_scalar_prefetch=1, grid=(N_SHARD,),
            in_specs=[pl.BlockSpec((1, hr, cols), lambda s, c_ref: (s, c_ref[0], 0)),
                      pl.BlockSpec((1, hr, cols), lambda s, c_ref: (s, 0, 0))],
            out_specs=pl.BlockSpec((1, hr, cols), lambda s, c_ref: (s, 0, 0))),
        out_shape=SDS((N_SHARD, hr, cols), BF16), compiler_params=_cp("arbitrary"), name=name,
    )(cidx, g, ra)


def chip_sum(h, rb, sidx, name):
    _, hr, cols = h.shape

    def body(s_ref, h_ref, r_ref, o_ref):
        o_ref[...] = ((h_ref[0].astype(F32) + r_ref[0].astype(F32)) + r_ref[1].astype(F32)) + r_ref[2].astype(F32)

    return pl.pallas_call(
        body,
        grid_spec=pltpu.PrefetchScalarGridSpec(
            num_scalar_prefetch=1, grid=(1,),
            in_specs=[pl.BlockSpec((1, hr, cols), lambda i, s_ref: (s_ref[0], 0, 0)),
                      pl.BlockSpec((3, hr, cols), lambda i, s_ref: (0, 0, 0))],
            out_specs=pl.BlockSpec((hr, cols), lambda i, s_ref: (0, 0))),
        out_shape=SDS((hr, cols), F32), compiler_params=_cp("arbitrary"), name=name,
    )(sidx, h, rb)


def _shard_cols(g, n_valid):
    r = g.shape[0]
    return g[:, :n_valid].reshape(r, N_SHARD, n_valid // N_SHARD).transpose(1, 0, 2)


def _unshard_cols(o, pad_to):
    _, r, n = o.shape
    full = o.transpose(1, 0, 2).reshape(r, N_SHARD * n)
    return jnp.pad(full, ((0, 0), (0, pad_to - N_SHARD * n)))


def _rows_of_tiles(t):
    B, H, S = t.shape
    return t.reshape(B, H, S // FT, 1, FT)


def mixer_fwd(x1, mod3, g_pre, w_main, w_f, b_forget_pad, goa, gob, w_out, g_post, tabs, nb):
    hmix, pa, pb, flog = mixer_proj(x1, mod3, g_pre, w_main, w_f, *tabs, name="mixer_proj")
    out_a, lse_a = band_fwd(pa, name="band_fwd")
    F = forget_cumsum(flog.reshape(nb, SEQ, LANE), b_forget_pad, name="forget_cumsum")
    Fh = F[:, :, :NH].transpose(0, 2, 1)
    fblk = Fh.reshape(nb, NH, SEQ // FB, 1, FB)
    frow = _rows_of_tiles(Fh)
    out_b, lse_b = fox_fwd(pb, fblk, frow, name="fox_fwd")
    x2, merged, y0m = mixer_out_fwd(out_a, out_b, goa, gob, w_out, g_post, x1, mod3, name="mixer_out_fwd")
    res = dict(hmix=hmix, flog=flog, pa=pa, pb=pb, out_a=out_a, lse_a=lse_a, fblk=fblk, frow=frow, out_b=out_b,
               lrow=_rows_of_tiles(lse_b.reshape(nb, NH, SEQ)), merged=merged, y0m=y0m)
    return x2, res


def mixer_bwd(dx2, x1, mod3, g_pre, w_main, w_f, b_forget_pad, goa, gob, w_out, g_post, tabs, res, nb):
    T = nb * SEQ
    dy0m, doa, dob, dmgate, dg_post, dgoa, dgob, dvec_b = mixer_out_bwd(
        dx2, res["y0m"], mod3, g_post, w_out, res["out_a"], res["out_b"], goa, gob, name="mixer_out_bwd")
    dqa, dka, dva = band_bwd(res["pa"], doa, res["out_a"], res["lse_a"], name="band_bwd")
    drow = _rows_of_tiles(dvec_b[:, :NH].reshape(nb, SEQ, NH).transpose(0, 2, 1))
    dqb, dkb, dvb, dfq, dfk = fox_bwd(res["pb"], dob, res["lrow"], drow, res["fblk"], res["frow"], name="fox_bwd")
    dF = (dfq.reshape(nb, NH, SEQ) + dfk.reshape(nb, NH, SEQ)).transpose(0, 2, 1)
    dF = jnp.pad(dF, ((0, 0), (0, 0), (0, LANE - NH)))
    dflog, dbf = forget_cumsum_bwd(dF, res["flog"].reshape(nb, SEQ, LANE), b_forget_pad, name="forget_cumsum_bwd")
    dflog = dflog.reshape(T, LANE)
    dproj = proj_grad_assemble((dqa, dka, dva, dqb, dkb, dvb), *tabs, name="proj_grad_assemble")
    dx1, dmod2, dg_pre = mixer_proj_bwd(dproj, dflog, dx2, x1, mod3, g_pre, w_main, w_f, name="mixer_proj_bwd")
    g_main = matmul_tn(res["hmix"], dproj, D, 1024, 1024, name="grad_w_in")
    g_f = matmul_tn(res["hmix"], dflog.astype(BF16), D, LANE, 1024, name="grad_w_forget")
    g_out = matmul_tn(res["merged"], dy0m, D, D, 1024, name="grad_w_out")
    dmod3 = jnp.concatenate([dmod2, dmgate], axis=1)
    return dx1, dmod3, dict(g_pre=dg_pre, g_post=dg_post, goa=dgoa, gob=dgob, b_forget=dbf[:, :NH],
                            w_in=jnp.concatenate([g_main, g_f[:, :NH]], axis=1), w_out=g_out)


def ffn_grads(h, dy0, act, dgate, dup, pre):
    g_gate = matmul_tn(h, dgate, D, FF_TN, 1024, name=pre + "_grad_gate")
    g_up = matmul_tn(h, dup, D, FF_TN, 1024, name=pre + "_grad_up")
    g_down = matmul_tn(act, dy0, FF_TN, D, 1024, name=pre + "_grad_down")
    return g_gate, g_up, g_down


def local_step(x0, tgt, pos_col, mod, wfull, p, late_weights=None, early_grads=None):
    T = x0.shape[0]
    nb = T // SEQ
    mod_ff1, mod_mix, mod_ff2 = mod[:, 0:3], mod[:, 3:6], mod[:, 6:9]
    tabs = rope_tables(pos_col, name="rope_tables")
    bf_pad = jnp.pad(p["b_forget"], ((0, 0), (0, LANE - NH)))

    (x1, h1, gate1, up1, y01), gathered = ffn_fwd(
        x0, mod_ff1, p["g_pre_ff1"], p["g_post_ff1"], wfull["w_ff1_gate"], wfull["w_ff1_up"], wfull["w_ff1_down"], 0.5,
        name="ff1_fwd", gather=None if late_weights is None else late_weights[:2])
    if late_weights is not None:
        wfull = {**wfull, **late_weights[2](gathered)}
    x2, res = mixer_fwd(x1, mod_mix, p["g_pre_mix"], wfull["w_main"], wfull["w_f"], bf_pad, p["g_out_a"], p["g_out_b"],
                        wfull["w_out"], p["g_post_mix"], tabs, nb)
    (x3, h2, gate2, up2, y02), _ = ffn_fwd(x2, mod_ff2, p["g_pre_ff2"], p["g_post_ff2"], wfull["w_ff2_gate"],
                                           wfull["w_ff2_up"], wfull["w_ff2_down"], 0.5, name="ff2_fwd")

    dx3, loss_part = loss_grad(x3, tgt, name="loss_grad")
    (dx2, dy02, act2, dgate2, dup2, dmod_ff2, dgpre2, dgpost2), _ = ffn_bwd(
        dx3, x2, y02, mod_ff2, p["g_pre_ff2"], p["g_post_ff2"], gate2, up2, wfull["w_ff2_gate"], wfull["w_ff2_up"],
        wfull["w_ff2_down"], 0.5, name="ff2_bwd")
    gw = {}
    gw["w_ff2_gate"], gw["w_ff2_up"], gw["w_ff2_down"] = ffn_grads(h2, dy02, act2, dgate2, dup2, "ff2")
    dx1, dmod_mix, gmix = mixer_bwd(dx2, x1, mod_mix, p["g_pre_mix"], wfull["w_main"], wfull["w_f"], bf_pad, p["g_out_a"],
                                    p["g_out_b"], wfull["w_out"], p["g_post_mix"], tabs, res, nb)
    gw["w_in"], gw["w_out"] = gmix["w_in"], gmix["w_out"]
    (dx0, dy01, act1, dgate1, dup1, dmod_ff1, dgpre1, dgpost1), scattered = ffn_bwd(
        dx1, x0, y01, mod_ff1, p["g_pre_ff1"], p["g_post_ff1"], gate1, up1, wfull["w_ff1_gate"], wfull["w_ff1_up"],
        wfull["w_ff1_down"], 0.5, name="ff1_bwd", scatter=None if early_grads is None else early_grads(gw))
    gw["w_ff1_gate"], gw["w_ff1_up"], gw["w_ff1_down"] = ffn_grads(h1, dy01, act1, dgate1, dup1, "ff1")
    dmod = jnp.concatenate([dmod_ff1, dmod_mix, dmod_ff2], axis=1).reshape(nb, 9 * D)
    small = dict(g_pre_ff1=dgpre1, g_post_ff1=dgpost1, g_pre_mix=gmix["g_pre"], g_post_mix=gmix["g_post"], g_pre_ff2=dgpre2,
                 g_post_ff2=dgpost2, g_out_a=gmix["goa"], g_out_b=gmix["gob"], b_forget=gmix["b_forget"])
    return loss_part, dx0, dmod, gw, small, scattered


def kernel(x, c, positions, w_ada, b_ada, g_pre_ff1, g_post_ff1, w_ff1_gate, w_ff1_up, w_ff1_down, g_pre_mix, g_post_mix, w_in, b_forget, g_out_a, g_out_b, w_out, g_pre_ff2, g_post_ff2, w_ff2_gate, w_ff2_up, w_ff2_down, loss_target, m_w_ada, m_b_ada, m_g_pre_ff1, m_g_post_ff1, m_w_ff1_gate, m_w_ff1_up, m_w_ff1_down, m_g_pre_mix, m_g_post_mix, m_w_in, m_b_forget, m_g_out_a, m_g_out_b, m_w_out, m_g_pre_ff2, m_g_post_ff2, m_w_ff2_gate, m_w_ff2_up, m_w_ff2_down, v_w_ada, v_b_ada, v_g_pre_ff1, v_g_post_ff1, v_w_ff1_gate, v_w_ff1_up, v_w_ff1_down, v_g_pre_mix, v_g_post_mix, v_w_in, v_b_forget, v_g_out_a, v_g_out_b, v_w_out, v_g_pre_ff2, v_g_post_ff2, v_w_ff2_gate, v_w_ff2_up, v_w_ff2_down):
    args = dict(locals())
    nb = x.shape[0]
    T = nb * SEQ
    ax, ay, ac = lax.axis_index("x"), lax.axis_index("y"), lax.axis_index("c")
    shard = 2 * ax + ay
    cidx = jnp.reshape(ac, (1,)).astype(jnp.int32)
    sidx = jnp.reshape(shard, (1,)).astype(jnp.int32)

    big = ["w_ff1_gate", "w_ff1_up", "w_ff1_down", "w_in", "w_out", "w_ff2_gate", "w_ff2_up", "w_ff2_down"]
    vecs = ["g_pre_ff1", "g_post_ff1", "g_pre_mix", "g_post_mix", "g_pre_ff2", "g_post_ff2"]

    first, late = big[:3], big[3:]
    splits = dict(zip(big, [512, 512, 352, 512, 128, 512, 512, 352]))

    def assemble(names, gathered):
        out = {}
        for n, o in zip(names, gathered):
            if n.endswith("gate") or n.endswith("up"):
                out[n] = _unshard_cols(o, DFF_PAD)
            elif n.endswith("down"):
                out[n] = jnp.pad(o.reshape(DFF, D), ((0, DFF_PAD - DFF), (0, 0)))
            elif n == "w_in":
                full = _unshard_cols(o, IN_COLS)
                out["w_main"] = full[:, :IN_MAIN]
                out["w_f"] = jnp.pad(full[:, IN_MAIN:], ((0, 0), (0, LANE - NH)))
            else:
                out[n] = o.reshape(D, D)
        return out

    wfull = assemble(first, all_gather_shards([args[n][0].astype(BF16) for n in first], [splits[n] for n in first],
                                              name="all_gather_weights"))
    late_weights = ([args[n][0].astype(BF16) for n in late], [splits[n] for n in late], functools.partial(assemble, late))

    ncol = w_ada.shape[2]
    c_all = all_gather8(c, name="all_gather_c").reshape(N_DEV * nb, D)
    b_loc = lax.dynamic_slice(b_ada, (0, shard * ncol), (1, ncol))
    mod_loc = ada_fwd(c_all, w_ada[0], b_loc, name="ada_fwd")
    mod_g = all_gather8(mod_loc, name="all_gather_mod")
    row0 = (4 * ax + 2 * ay + ac) * nb
    mod_rows = lax.dynamic_slice(mod_g, (0, row0, 0), (N_DEV, nb, ncol))
    mod = jnp.concatenate([mod_rows[2 * s] for s in range(N_SHARD)], axis=-1).reshape(nb, 9, D)

    small_in = dict(g_pre_ff1=g_pre_ff1, g_post_ff1=g_post_ff1, g_pre_mix=g_pre_mix, g_post_mix=g_post_mix, g_pre_ff2=g_pre_ff2,
                    g_post_ff2=g_post_ff2, g_out_a=g_out_a, g_out_b=g_out_b, b_forget=b_forget)
    def shard_blocked(n, g):
        if n.endswith("gate") or n.endswith("up"):
            return _shard_cols(g, DFF)
        if n.endswith("down"):
            return g[:DFF].reshape(N_SHARD, DFF // N_SHARD, D)
        if n == "w_in":
            return _shard_cols(g, IN_COLS)
        return g.reshape(N_SHARD, D // N_SHARD, D)

    def chip_sums(names, gw, tag):
        gsb = [shard_blocked(n, gw[n]) for n in names]
        ras = sibling_send_half(gsb, name="grad_sibling_send_" + tag)
        return [pair_sum(g, ra, cidx, name=f"grad_pair_sum_{n}") for n, g, ra in zip(names, gsb, ras)]

    hs = {}

    def early_grads(gw):
        hs.update(zip(late, chip_sums(late, gw, "late")))
        return [hs[n] for n in late]

    loss_part, dx0, dmod, gw, small, rbs_late = local_step(
        x.reshape(T, D), loss_target.reshape(T, D), positions.reshape(T, 1), mod, wfull, small_in, late_weights, early_grads)

    dmod_all = all_gather8(dmod, name="all_gather_dmod").reshape(N_DEV * nb, 9 * D)
    dmod_loc = lax.dynamic_slice(dmod_all, (0, shard * ncol), (N_DEV * nb, ncol))
    g_w_ada = ada_bwd(c_all, dmod_loc, name="ada_bwd")

    hs.update(zip(first, chip_sums(first, gw, "first")))
    rbs = dict(zip(late, rbs_late))
    rbs.update(zip(first, chip_scatter([hs[n] for n in first], name="grad_chip_scatter")))
    ghs = [chip_sum(hs[n], rbs[n], sidx, name=f"grad_chip_sum_{n}") for n in big]
    theirs = sibling_swap(ghs, name="grad_sibling_swap")
    gfull = {n: jnp.concatenate([jnp.where(ac == 0, mine, other), jnp.where(ac == 0, other, mine)], axis=0)
             for n, mine, other in zip(big, ghs, theirs)}
    gfull["w_ada"] = g_w_ada

    row6 = jnp.concatenate([small["g_out_a"], small["g_out_b"]], axis=1)
    row7 = jnp.concatenate([small["b_forget"], loss_part[0:1, 0:1], jnp.zeros((1, D - NH - 1), F32)], axis=1)
    pack = jnp.concatenate([small[n] for n in vecs] + [row6, row7], axis=0)
    packed = all_gather8(pack, name="all_gather_small").reshape(N_DEV, 8 * D)

    def pack_state(pre):
        r6 = jnp.concatenate([args[pre + "g_out_a"], args[pre + "g_out_b"]], axis=1)
        r7 = jnp.pad(args[pre + "b_forget"], ((0, 0), (0, D - NH)))
        return jnp.concatenate([args[pre + n] for n in vecs] + [r6, r7], axis=0).reshape(1, 8 * D)

    sg, sd, sm, sv = (t.reshape(8, D) for t in vec_adam(packed, pack_state(""), pack_state("m_"), pack_state("v_"), name="adam_small"))

    def unpack(t):
        out = {n: t[i:i + 1] for i, n in enumerate(vecs)}
        out["g_out_a"], out["g_out_b"], out["b_forget"] = t[6:7, :WG], t[6:7, WG:], t[7:8, :NH]
        return out

    outs = dict(grad=unpack(sg), delta=unpack(sd), new_m=unpack(sm), new_v=unpack(sv))
    loss = sg[7, NH]
    outs["grad"]["b_ada"], outs["delta"]["b_ada"], outs["new_m"]["b_ada"], outs["new_v"]["b_ada"] = vec_adam(
        dmod_all, b_ada, m_b_ada, v_b_ada, name="adam_b_ada")

    for n in big + ["w_ada"]:
        g = gfull[n]
        rows = g.shape[0]
        tr = 128 if rows % 128 == 0 else 344
        d, m2, v2 = adam_update(args[n][0], g, args["m_" + n][0], args["v_" + n][0], tr, name="adam_" + n)
        outs["grad"][n], outs["delta"][n], outs["new_m"][n], outs["new_v"][n] = g[None], d[None], m2[None], v2[None]

    order = ["w_ada", "b_ada", "g_pre_ff1", "g_post_ff1", "w_ff1_gate", "w_ff1_up", "w_ff1_down", "g_pre_mix", "g_post_mix", "w_in",
             "b_forget", "g_out_a", "g_out_b", "w_out", "g_pre_ff2", "g_post_ff2", "w_ff2_gate", "w_ff2_up", "w_ff2_down"]
    result = [loss, dx0.reshape(nb, SEQ, D)]
    for kind in ("grad", "delta", "new_m", "new_v"):
        result += [outs[kind][n] for n in order]
    return tuple(result)
```

```python
import functools
import math

import jax
import jax.numpy as jnp
from jax import lax
from jax.experimental import pallas as pl
from jax.experimental.pallas import tpu as pltpu

D = 1024
SEQ = 2048
HD = 64
NH = 8
WG = NH * HD
DFF = 2752
DFF_PAD = 2816
IN_MAIN = 6 * WG
IN_COLS = IN_MAIN + NH
N_SHARD = 4
N_DEV = 8
LANE = 128
QB = 128
FB = 256
FT = 512
FOX_PAIRS = 2
FOX_PAIRS_BWD = 1
BAND_UNROLL = 4
BAND_UNROLL_BWD = 4
PATTERNS = ((1, 16), (4, 4), (16, 1))
ROPE_THETA = 500000.0
EPS = 1e-6
NEG = -1e30
ATTN_SCALE = HD ** -0.5
TM = 512
TM_FFN = 512
TM_BWD = 256
VMEM_LIMIT = 56 * 1024 * 1024

ADAM_LR, ADAM_B1, ADAM_B2, ADAM_EPS, ADAM_WD, ADAM_STEP = 0.001, 0.9, 0.999, 1e-08, 0.01, 10

F32 = jnp.float32
BF16 = jnp.bfloat16
MESH = pl.DeviceIdType.MESH
SDS = jax.ShapeDtypeStruct


def _cp(*sem):
    return pltpu.CompilerParams(dimension_semantics=sem, vmem_limit_bytes=VMEM_LIMIT)


def _dot(a, b):
    return jnp.dot(a, b, preferred_element_type=F32)


def _dot_nt(a, b):
    return lax.dot_general(a, b, (((1,), (1,)), ((), ())), preferred_element_type=F32)


def _dot_tn(a, b):
    return lax.dot_general(a, b, (((0,), (0,)), ((), ())), preferred_element_type=F32)


def _rms(xf):
    return lax.rsqrt(jnp.mean(xf * xf, axis=-1, keepdims=True) + EPS)


def _norm_mod_bwd(dh, xf, g, scale):
    r = _rms(xf)
    xh = xf * r
    dsh = jnp.sum(dh, axis=0, keepdims=True)
    dsc = jnp.sum(dh * (xh * g), axis=0, keepdims=True)
    dn = dh * (1.0 + scale)
    dg = jnp.sum(dn * xh, axis=0, keepdims=True)
    dxh = dn * g
    dx = r * (dxh - xh * jnp.mean(dxh * xh, axis=-1, keepdims=True))
    return dx, dsh, dsc, dg


def _post_bwd(dxo, y0, g, mgate, gs):
    r = _rms(y0)
    yh = y0 * r
    dmg = gs * jnp.sum(dxo * (yh * g), axis=0, keepdims=True)
    dy = (gs * mgate) * dxo
    dg = jnp.sum(dy * yh, axis=0, keepdims=True)
    dyh = dy * g
    dy0 = r * (dyh - yh * jnp.mean(dyh * yh, axis=-1, keepdims=True))
    return dy0, dmg, dg


def _mod_map(i, *_):
    return ((i * TM) // SEQ, 0, 0)


FF_TN = 1408
FF_TILES = ((0, 768), (768, 1536), (1536, 2304), (2304, 2816))


def _resident_scratch():
    return [pltpu.VMEM((D, DFF_PAD), BF16), pltpu.VMEM((D, DFF_PAD), BF16), pltpu.VMEM((DFF_PAD, D), BF16),
            pltpu.SemaphoreType.DMA((3,))]


def _load_resident(first_step, srcs, dsts, sems):
    @pl.when(first_step)
    def _():
        cps = [pltpu.make_async_copy(s, d, sems.at[k]) for k, (s, d) in enumerate(zip(srcs, dsts))]
        for cp in cps:
            cp.start()
        for cp in cps:
            cp.wait()


def ffn_fwd(x, mod3, g_pre, g_post, wg, wu, wd, gs, name, gather=None):
    T = x.shape[0]
    tm = TM_FFN
    ng = 0 if gather is None else len(gather[0])
    plan = None if gather is None else ShardGather([w.shape for w in gather[0]], gather[1])

    def body(*refs):
        x_ref, mod_ref, gpre_ref, gpost_ref = refs[:4]
        xo_ref, h_ref, gate_ref, up_ref, y0_ref = refs[7 + ng:12 + ng]
        wg_ref, wu_ref, wd_ref, wsem = refs[12 + 2 * ng:16 + 2 * ng]
        i = pl.program_id(0)
        if plan is not None:
            comm = (refs[7:7 + ng], refs[12 + ng:12 + 2 * ng], refs[16 + 2 * ng:])
            pl.when(i == 0)(lambda: plan.start(*comm))
        _load_resident(i == 0, refs[4:7], (wg_ref, wu_ref, wd_ref), wsem)

        xf = x_ref[...]
        hb = ((xf * _rms(xf) * gpre_ref[...]) * (1.0 + mod_ref[0, 1:2, :]) + mod_ref[0, 0:1, :]).astype(BF16)
        h_ref[...] = hb
        y0 = None
        for lo, hi in FF_TILES:
            gate = _dot(hb, wg_ref[:, lo:hi])
            up = _dot(hb, wu_ref[:, lo:hi])
            gate_ref[:, lo:hi] = gate.astype(BF16)
            up_ref[:, lo:hi] = up.astype(BF16)
            part = _dot((gate * jax.nn.sigmoid(gate) * up).astype(BF16), wd_ref[lo:hi, :])
            y0 = part if y0 is None else y0 + part
        y0_ref[...] = y0
        xo_ref[...] = xf + (gs * mod_ref[0, 2:3, :]) * (y0 * _rms(y0) * gpost_ref[...])

        if plan is not None:
            pl.when(i == T // tm - 1)(lambda: plan.finish(*comm))

    tok = pl.BlockSpec((tm, D), lambda i: (i, 0))
    vec = pl.BlockSpec((1, D), lambda i: (0, 0))
    hid = pl.BlockSpec((tm, DFF_PAD), lambda i: (i, 0))
    outs = pl.pallas_call(
        body, grid=(T // tm,),
        in_specs=[tok, pl.BlockSpec((1, 3, D), lambda i: ((i * tm) // SEQ, 0, 0)), vec, vec, HBM, HBM, HBM] + [HBM] * ng,
        out_specs=[tok, tok, hid, hid, tok] + [HBM] * ng,
        out_shape=[SDS((T, D), F32), SDS((T, D), BF16), SDS((T, DFF_PAD), BF16), SDS((T, DFF_PAD), BF16), SDS((T, D), F32)]
        + ([] if plan is None else plan.out_shapes(BF16)),
        scratch_shapes=_resident_scratch() + ([] if plan is None else plan.scratch()),
        compiler_params=_cp("arbitrary"), name=name,
    )(x, mod3, g_pre, g_post, wg, wu, wd, *([] if gather is None else gather[0]))
    return outs[:5], outs[5:]


def ffn_bwd(dxo, x, y0, mod3, g_pre, g_post, gate, up, wg, wu, wd, gs, name, scatter=None):
    T = x.shape[0]
    nb = T // SEQ
    tm = TM_BWD
    tiles_per_seq = SEQ // tm
    ns = 0 if scatter is None else len(scatter)

    def body(*refs):
        dxo_ref, x_ref, y0_ref, mod_ref, gpre_ref, gpost_ref, gate_ref, up_ref = refs[:8]
        dx_ref, dy0_ref, act_ref, dgate_ref, dup_ref, dmod_ref, dgpre_ref, dgpost_ref = refs[11 + ns:19 + ns]
        wg_ref, wu_ref, wd_ref, wsem = refs[19 + 2 * ns:23 + 2 * ns]
        i = pl.program_id(0)
        _load_resident(i == 0, refs[8:11], (wg_ref, wu_ref, wd_ref), wsem)
        if ns:
            comm = (refs[11:11 + ns], refs[19 + ns:19 + 2 * ns], *refs[23 + 2 * ns:])

            @pl.when(i == 0)
            def _():
                for cp in _scatter_copies(*comm):
                    cp.start()

        @pl.when(i == 0)
        def _():
            dgpre_ref[...] = jnp.zeros_like(dgpre_ref)
            dgpost_ref[...] = jnp.zeros_like(dgpost_ref)

        @pl.when(i % tiles_per_seq == 0)
        def _():
            dmod_ref[...] = jnp.zeros_like(dmod_ref)

        dxo = dxo_ref[...]
        dy0, dmg, dg = _post_bwd(dxo, y0_ref[...], gpost_ref[...], mod_ref[0, 2:3, :], gs)
        dmod_ref[0, 2:3, :] += dmg
        dgpost_ref[...] += dg
        db = dy0.astype(BF16)
        dy0_ref[...] = db
        dh = None
        for lo, hi in FF_TILES:
            dact = _dot_nt(db, wd_ref[lo:hi, :])
            g = gate_ref[:, lo:hi].astype(F32)
            u = up_ref[:, lo:hi].astype(F32)
            sig = jax.nn.sigmoid(g)
            sl = g * sig
            dgate = (dact * u * (sig * (1.0 + g * (1.0 - sig)))).astype(BF16)
            dup = (dact * sl).astype(BF16)
            act_ref[:, lo:hi] = (sl * u).astype(BF16)
            dgate_ref[:, lo:hi] = dgate
            dup_ref[:, lo:hi] = dup
            part = _dot_nt(dgate, wg_ref[:, lo:hi]) + _dot_nt(dup, wu_ref[:, lo:hi])
            dh = part if dh is None else dh + part
        dx, dsh, dsc, dg = _norm_mod_bwd(dh, x_ref[...], gpre_ref[...], mod_ref[0, 1:2, :])
        dx_ref[...] = dxo + dx
        dmod_ref[0, 0:1, :] += dsh
        dmod_ref[0, 1:2, :] += dsc
        dgpre_ref[...] += dg

        if ns:
            @pl.when(i == T // tm - 1)
            def _():
                for cp in _scatter_copies(*comm):
                    cp.wait()

    tok = pl.BlockSpec((tm, D), lambda i: (i, 0))
    vec = pl.BlockSpec((1, D), lambda i: (0, 0))
    hid = pl.BlockSpec((tm, DFF_PAD), lambda i: (i, 0))
    modspec = pl.BlockSpec((1, 3, D), lambda i: ((i * tm) // SEQ, 0, 0))
    outs = pl.pallas_call(
        body, grid=(T // tm,),
        in_specs=[tok, tok, tok, modspec, vec, vec, hid, hid, HBM, HBM, HBM] + [HBM] * ns,
        out_specs=[tok, tok, hid, hid, hid, modspec, vec, vec] + [HBM] * ns,
        out_shape=[SDS((T, D), F32), SDS((T, D), BF16), SDS((T, DFF_PAD), BF16), SDS((T, DFF_PAD), BF16),
                   SDS((T, DFF_PAD), BF16), SDS((nb, 3, D), F32), SDS((1, D), F32), SDS((1, D), F32)]
        + [SDS((3,) + h.shape[1:], h.dtype) for h in (scatter or [])],
        scratch_shapes=_resident_scratch()
        + ([pltpu.SemaphoreType.DMA((ns, 3)), pltpu.SemaphoreType.DMA((ns, 3))] if ns else []),
        compiler_params=_cp("arbitrary"), name=name,
    )(dxo, x, y0, mod3, g_pre, g_post, gate, up, wg, wu, wd, *(scatter or []))
    return outs[:8], outs[8:]


def matmul_tn(a, b, tm, tn, tk, name):
    T, M = a.shape
    N = b.shape[1]
    nk = T // tk

    def body(a_ref, b_ref, o_ref):
        @pl.when(pl.program_id(2) == 0)
        def _():
            o_ref[...] = jnp.zeros_like(o_ref)

        o_ref[...] += _dot_tn(a_ref[...], b_ref[...])

    return pl.pallas_call(
        body, grid=(M // tm, N // tn, nk),
        in_specs=[pl.BlockSpec((tk, tm), lambda i, j, k: (k, i)), pl.BlockSpec((tk, tn), lambda i, j, k: (k, j))],
        out_specs=pl.BlockSpec((tm, tn), lambda i, j, k: (i, j)),
        out_shape=SDS((M, N), F32),
        compiler_params=_cp("arbitrary", "arbitrary", "arbitrary"), name=name,
    )(a, b)


def loss_grad(y, tgt, name):
    T = y.shape[0]

    def body(y_ref, t_ref, dy_ref, l_ref):
        @pl.when(pl.program_id(0) == 0)
        def _():
            l_ref[...] = jnp.zeros_like(l_ref)

        e = y_ref[...] - t_ref[...]
        dy_ref[...] = e * (1.0 / D)
        l_ref[...] += jnp.sum(e * e) * (0.5 / D)

    tok = pl.BlockSpec((TM, D), lambda i: (i, 0))
    return pl.pallas_call(
        body, grid=(T // TM,), in_specs=[tok, tok],
        out_specs=[tok, pl.BlockSpec((8, LANE), lambda i: (0, 0))],
        out_shape=[SDS((T, D), F32), SDS((8, LANE), F32)],
        compiler_params=_cp("arbitrary"), name=name,
    )(y, tgt)


def rope_tables(pos_col, name):
    T = pos_col.shape[0]
    tm = 1024

    def body(p_ref, c_ref, s1_ref, s2_ref):
        lane = lax.broadcasted_iota(jnp.int32, (1, LANE), 1)
        l64 = lane % HD
        inv_freq = jnp.exp((l64 % 8).astype(F32) * (-math.log(ROPE_THETA) / 8.0))
        ang = p_ref[...].astype(F32) * inv_freq
        cs = jnp.cos(ang)
        sn = jnp.sin(ang)
        c_ref[...] = jnp.where(l64 < 16, cs, 1.0)
        s1_ref[...] = jnp.where(l64 < 8, -sn, 0.0)
        s2_ref[...] = jnp.where((l64 >= 8) & (l64 < 16), sn, 0.0)

    tab = pl.BlockSpec((tm, LANE), lambda i: (i, 0))
    return pl.pallas_call(
        body, grid=(T // tm,), in_specs=[pl.BlockSpec((tm, 1), lambda i: (i, 0))], out_specs=[tab, tab, tab],
        out_shape=[SDS((T, LANE), F32)] * 3, compiler_params=_cp("arbitrary"), name=name,
    )(pos_col)


def mixer_proj(x, mod3, g_pre, w_main, w_f, rc, rs1, rs2, name):
    T = x.shape[0]

    def body(x_ref, mod_ref, g_ref, w_ref, wf_ref, c_ref, s1_ref, s2_ref, h_ref, pa_ref, pb_ref, f_ref):
        xf = x_ref[...]
        h = (xf * _rms(xf) * g_ref[...]) * (1.0 + mod_ref[0, 1:2, :]) + mod_ref[0, 0:1, :]
        hb = h.astype(BF16)
        h_ref[...] = hb
        f_ref[...] = _dot(hb, wf_ref[...])
        c, s1, s2 = c_ref[...], s1_ref[...], s2_ref[...]
        for grp in range(2):
            pr = _dot(hb, w_ref[:, grp * WG:(grp + 1) * WG])
            for k in range(WG // LANE):
                t = pr[:, k * LANE:(k + 1) * LANE]
                pa_ref[:, grp * WG + k * LANE:grp * WG + (k + 1) * LANE] = (
                    t * c + pltpu.roll(t, LANE - 8, 1) * s1 + pltpu.roll(t, 8, 1) * s2)
        pa_ref[:, 2 * WG:3 * WG] = _dot(hb, w_ref[:, 2 * WG:3 * WG])
        for grp in range(3):
            pb_ref[:, grp * WG:(grp + 1) * WG] = _dot(hb, w_ref[:, (3 + grp) * WG:(4 + grp) * WG]).astype(BF16)

    tok = pl.BlockSpec((TM, D), lambda i: (i, 0))
    vec = pl.BlockSpec((1, D), lambda i: (0, 0))
    tab = pl.BlockSpec((TM, LANE), lambda i: (i, 0))
    grp3 = pl.BlockSpec((TM, 3 * WG), lambda i: (i, 0))
    return pl.pallas_call(
        body, grid=(T // TM,),
        in_specs=[tok, pl.BlockSpec((1, 3, D), _mod_map), vec, pl.BlockSpec((D, IN_MAIN), lambda i: (0, 0)),
                  pl.BlockSpec((D, LANE), lambda i: (0, 0)), tab, tab, tab],
        out_specs=[tok, grp3, grp3, tab],
        out_shape=[SDS((T, D), BF16), SDS((T, 3 * WG), F32), SDS((T, 3 * WG), BF16), SDS((T, LANE), F32)],
        compiler_params=_cp("arbitrary"), name=name,
    )(x, mod3, g_pre, w_main, w_f, rc, rs1, rs2)


def _head_lanes():
    return lax.broadcasted_iota(jnp.int32, (1, LANE), 1) < HD


def _pair(m0, a, b):
    return jnp.where(m0, a, b)


def _band_rows(i, d, nbc):
    if nbc == 1:
        return i, i, 0
    r, mb = i // nbc, i % nbc
    return r + mb * (QB * d), r + jnp.maximum(mb - 1, 0) * (QB * d), jnp.where(mb > 0, QB, 0)


def _rows(start, size, d):
    return pl.ds(pl.multiple_of(start, QB), size) if d == 1 else pl.ds(start, size, stride=d)


def _band_valid(span, off):
    rq = lax.broadcasted_iota(jnp.int32, (QB, span), 0)
    rel = lax.broadcasted_iota(jnp.int32, (QB, span), 1) - off
    return (rel <= rq) & (rel >= rq - QB)


def band_fwd(pa, name):
    T = pa.shape[0]
    B = T // SEQ
    NP = WG // LANE

    def body(q_ref, k_ref, v_ref, out_ref, lse_ref, o_s, l_s):
        m0 = _head_lanes()
        for pidx, (d, nbc) in enumerate(PATTERNS):
            span = QB if nbc == 1 else 2 * QB

            def blk(it, carry, pidx=pidx, d=d, nbc=nbc, span=span):
                ld = []
                for u in range(BAND_UNROLL):
                    qs, ks, off = _band_rows(it * BAND_UNROLL + u, d, nbc)
                    q = q_ref[_rows(qs, QB, d), :] * ATTN_SCALE
                    ld.append((qs, q, k_ref[_rows(ks, span, d), :].astype(BF16), v_ref[_rows(ks, span, d), :].astype(BF16),
                               _band_valid(span, off)))
                ss = [[jnp.where(valid, _dot_nt(jnp.where(mh, q, 0.0).astype(BF16), k), NEG) for mh in (m0, jnp.logical_not(m0))]
                      for _, q, k, _, valid in ld]
                ps = []
                for pair in ss:
                    row = []
                    for s in pair:
                        m = jnp.max(s, axis=-1, keepdims=True)
                        p = jnp.exp(s - m)
                        row.append((p.astype(BF16), jnp.sum(p, axis=-1, keepdims=True), m))
                    ps.append(row)
                pv = [[_dot(p, ld[u][3]) for p, _, _ in ps[u]] for u in range(BAND_UNROLL)]
                for u in range(BAND_UNROLL):
                    rows = _rows(ld[u][0], QB, d)
                    (_, l0, mx0), (_, l1, mx1) = ps[u]
                    o_s[pidx, rows, :] = _pair(m0, pv[u][0] / l0, pv[u][1] / l1)
                    l_s[pidx, rows, :] = _pair(m0, mx0 + jnp.log(l0), mx1 + jnp.log(l1))
                return carry

            lax.fori_loop(0, SEQ // QB // BAND_UNROLL, blk, 0)
        for c in range(SEQ // FB):
            sl = slice(c * FB, (c + 1) * FB)
            a, b, e = l_s[0, sl, :], l_s[1, sl, :], l_s[2, sl, :]
            m = jnp.maximum(jnp.maximum(a, b), e)
            L = m + jnp.log(jnp.exp(a - m) + jnp.exp(b - m) + jnp.exp(e - m))
            out_ref[sl, :] = jnp.exp(a - L) * o_s[0, sl, :] + jnp.exp(b - L) * o_s[1, sl, :] + jnp.exp(e - L) * o_s[2, sl, :]
            lse_ref[sl, :] = L

    blk_of = lambda g: pl.BlockSpec((SEQ, LANE), lambda b, hp, g=g: (b, g * NP + hp))
    return pl.pallas_call(
        body, grid=(B, NP), in_specs=[blk_of(0), blk_of(1), blk_of(2)], out_specs=[blk_of(0), blk_of(0)],
        out_shape=[SDS((T, WG), F32), SDS((T, WG), F32)],
        scratch_shapes=[pltpu.VMEM((3, SEQ, LANE), F32), pltpu.VMEM((3, SEQ, LANE), F32)],
        compiler_params=_cp("arbitrary", "arbitrary"), name=name,
    )(pa, pa, pa)


def _pair_rowsum(m0, prod):
    s0 = jnp.sum(jnp.where(m0, prod, 0.0), axis=-1, keepdims=True)
    return _pair(m0, s0, jnp.sum(prod, axis=-1, keepdims=True) - s0)


def band_bwd(pa, do, out, lse, name):
    T = pa.shape[0]
    B = T // SEQ
    NP = WG // LANE

    def body(q_ref, k_ref, v_ref, do_ref, out_ref, l_ref, dq_ref, dk_ref, dv_ref, d_s):
        m0 = _head_lanes()
        dq_ref[...] = jnp.zeros_like(dq_ref)
        dk_ref[...] = jnp.zeros_like(dk_ref)
        dv_ref[...] = jnp.zeros_like(dv_ref)
        for c in range(SEQ // FB):
            sl = slice(c * FB, (c + 1) * FB)
            d_s[sl, :] = _pair_rowsum(m0, do_ref[sl, :] * out_ref[sl, :])
        for d, nbc in PATTERNS:
            span = QB if nbc == 1 else 2 * QB

            def blk(it, carry, d=d, nbc=nbc, span=span):
                masks = (m0, jnp.logical_not(m0))
                ld = []
                for u in range(BAND_UNROLL_BWD):
                    qs, ks, off = _band_rows(it * BAND_UNROLL_BWD + u, d, nbc)
                    qrow, krow = _rows(qs, QB, d), _rows(ks, span, d)
                    ld.append(dict(qrow=qrow, krow=krow, q=q_ref[qrow, :] * ATTN_SCALE, k=k_ref[krow, :].astype(BF16),
                                   v=v_ref[krow, :].astype(BF16), do=do_ref[qrow, :], l=l_ref[qrow, :], dv=d_s[qrow, :],
                                   valid=_band_valid(span, off)))
                for t in ld:
                    t["qm"] = [jnp.where(mh, t["q"], 0.0).astype(BF16) for mh in masks]
                    t["dom"] = [jnp.where(mh, t["do"], 0.0).astype(BF16) for mh in masks]
                sd = [[(jnp.where(t["valid"], _dot_nt(t["qm"][h], t["k"]), NEG), _dot_nt(t["dom"][h], t["v"])) for h in range(2)]
                      for t in ld]
                pd = []
                for t, pair in zip(ld, sd):
                    row = []
                    for h, (s, dp) in enumerate(pair):
                        col = slice(h * HD, h * HD + 1)
                        p = jnp.exp(s - t["l"][:, col])
                        row.append((p.astype(BF16), (p * (dp - t["dv"][:, col])).astype(BF16)))
                    pd.append(row)
                gr = [(_dot(row[0][1], t["k"]), _dot(row[1][1], t["k"]),
                       _dot_tn(jnp.concatenate([row[0][1], row[1][1]], axis=0), jnp.concatenate(t["qm"], axis=0)),
                       _dot_tn(jnp.concatenate([row[0][0], row[1][0]], axis=0), jnp.concatenate(t["dom"], axis=0)))
                      for t, row in zip(ld, pd)]
                for t, (dq0, dq1, dk, dv) in zip(ld, gr):
                    dq_ref[t["qrow"], :] += _pair(m0, dq0, dq1) * ATTN_SCALE
                    dk_ref[t["krow"], :] += dk
                    dv_ref[t["krow"], :] += dv
                return carry

            lax.fori_loop(0, SEQ // QB // BAND_UNROLL_BWD, blk, 0)

    blk_of = lambda g: pl.BlockSpec((SEQ, LANE), lambda b, hp, g=g: (b, g * NP + hp))
    return pl.pallas_call(
        body, grid=(B, NP), in_specs=[blk_of(0), blk_of(1), blk_of(2), blk_of(0), blk_of(0), blk_of(0)],
        out_specs=[blk_of(0)] * 3, out_shape=[SDS((T, WG), F32)] * 3,
        scratch_shapes=[pltpu.VMEM((SEQ, LANE), F32)],
        compiler_params=_cp("arbitrary", "arbitrary"), name=name,
    )(pa, pa, pa, do, out, lse)


def _tile_causal(nq, nk, q0, k0):
    r = lax.broadcasted_iota(jnp.int32, (nq, nk), 0)
    c = lax.broadcasted_iota(jnp.int32, (nq, nk), 1)
    return r + (q0 - k0) >= c


def _row_to_col(row):
    n = row.shape[1]
    return jnp.transpose(jnp.broadcast_to(row, (LANE, n)))[:, 0:1]


def _col_to_row(col):
    n = col.shape[0]
    return jnp.transpose(jnp.broadcast_to(col, (n, LANE)))[0:1, :]


def fox_fwd(pb, fblk, frow, name):
    T = pb.shape[0]
    B = T // SEQ
    NG = WG // (LANE * FOX_PAIRS)
    NHS = 2 * FOX_PAIRS
    W = LANE * FOX_PAIRS
    n = SEQ // FB

    def body(q_ref, k_ref, v_ref, fc_ref, fr_ref, o_ref, lse_ref):
        i = pl.program_id(2)
        m0 = _head_lanes()
        masks = (m0, jnp.logical_not(m0))
        heads = [(hh, slice((hh // 2) * LANE, (hh // 2 + 1) * LANE), masks[hh % 2]) for hh in range(NHS)]
        qh, fq = [], []
        for hh, lanes, mh in heads:
            q = q_ref[:, lanes] * ATTN_SCALE
            qh.append(jnp.where(mh, q, jnp.zeros_like(q)))
            fq.append(_row_to_col(fc_ref[0, hh, 0]))

        def step(t, carry, masked):
            rows = pl.ds(pl.multiple_of(t * FT, FT), FT)
            ss = [_dot_nt(qh[hh], k_ref[rows, lanes]) + fq[hh] - fr_ref[0, hh, t] for hh, lanes, _ in heads]
            if masked:
                ok = _tile_causal(FB, FT, i * FB, t * FT)
                ss = [jnp.where(ok, s, NEG) for s in ss]
            st = []
            for hh, _, _ in heads:
                m2 = jnp.maximum(carry[hh][0], jnp.max(ss[hh], axis=-1, keepdims=True))
                st.append((m2, jnp.exp(carry[hh][0] - m2), jnp.exp(ss[hh] - m2).astype(BF16)))
            pv = []
            for hh, lanes, mh in heads:
                vt = v_ref[rows, lanes]
                pv.append(_dot(st[hh][2], jnp.where(mh, vt, jnp.ones_like(vt))))
            return tuple((st[hh][0], st[hh][1] * carry[hh][1] + pv[hh]) for hh in range(NHS))

        one = (jnp.full((FB, 1), NEG, F32), jnp.zeros((FB, LANE), F32))
        last = (i * FB) // FT
        carry = lax.fori_loop(0, last, lambda t, cr: step(t, cr, False), (one,) * NHS)
        carry = step(last, carry, True)
        for pr in range(FOX_PAIRS):
            (ma, acca), (mb, accb) = carry[2 * pr], carry[2 * pr + 1]
            la, lb = acca[:, HD:HD + 1], accb[:, 0:1]
            o_ref[:, pr * LANE:(pr + 1) * LANE] = _pair(m0, acca / la, accb / lb)
            lse_ref[0, 2 * pr, 0] = _col_to_row(ma + jnp.log(la))
            lse_ref[0, 2 * pr + 1, 0] = _col_to_row(mb + jnp.log(lb))

    qblk = pl.BlockSpec((FB, W), lambda b, g, i: (b * n + i, g))
    full = lambda grp: pl.BlockSpec((SEQ, W), lambda b, g, i, grp=grp: (b, grp * NG + g))
    rowb = pl.BlockSpec((1, NHS, 1, 1, FB), lambda b, g, i: (b, g, i, 0, 0))
    return pl.pallas_call(
        body, grid=(B, NG, n),
        in_specs=[qblk, full(1), full(2), rowb, pl.BlockSpec((1, NHS, SEQ // FT, 1, FT), lambda b, g, i: (b, g, 0, 0, 0))],
        out_specs=[qblk, rowb], out_shape=[SDS((T, WG), F32), SDS((B, NH, n, 1, FB), F32)],
        compiler_params=_cp("arbitrary", "arbitrary", "arbitrary"), name=name,
    )(pb, pb, pb, fblk, frow)


def fox_bwd(pb, do, lrow, drow, fblk, frow, name):
    T = pb.shape[0]
    B = T // SEQ
    PAIRS = FOX_PAIRS_BWD
    NG = WG // (LANE * PAIRS)
    NHS = 2 * PAIRS
    W = LANE * PAIRS
    n = SEQ // FB

    def body(q_ref, k_ref, v_ref, do_ref, l_ref, d_ref, fc_ref, fr_ref, dq_ref, dk_ref, dv_ref, dfq_ref, dfk_ref):
        j = pl.program_id(2)
        m0 = _head_lanes()
        masks = (m0, jnp.logical_not(m0))
        heads = [(hh, slice((hh // 2) * LANE, (hh // 2 + 1) * LANE), masks[hh % 2]) for hh in range(NHS)]

        @pl.when(j == 0)
        def _():
            dq_ref[...] = jnp.zeros_like(dq_ref)
            dfq_ref[...] = jnp.zeros_like(dfq_ref)

        kj = [k_ref[:, lanes] for _, lanes, _ in heads]
        vj = [v_ref[:, lanes] for _, lanes, _ in heads]
        fk = [_row_to_col(fc_ref[0, hh, 0]) for hh in range(NHS)]

        def step(t, carry, masked):
            rows = pl.ds(pl.multiple_of(t * FT, FT), FT)
            qm, dom = [], []
            for _, lanes, mh in heads:
                qt = q_ref[rows, lanes] * ATTN_SCALE
                qm.append(jnp.where(mh, qt, jnp.zeros_like(qt)))
                dom.append(jnp.where(mh, do_ref[rows, lanes], 0.0).astype(BF16))
            ss = [_dot_nt(kj[hh], qm[hh]) + fr_ref[0, hh, t] - fk[hh] for hh in range(NHS)]
            dps = [_dot_nt(vj[hh], dom[hh]) for hh in range(NHS)]
            if masked:
                key = lax.broadcasted_iota(jnp.int32, (FB, FT), 0)
                qry = lax.broadcasted_iota(jnp.int32, (FB, FT), 1)
                ok = qry + (t * FT - j * FB) >= key
                ss = [jnp.where(ok, s, NEG) for s in ss]
            pds = []
            for hh in range(NHS):
                p = jnp.exp(ss[hh] - l_ref[0, hh, t])
                ds = p * (dps[hh] - d_ref[0, hh, t])
                dfq_ref[0, hh, t] += jnp.sum(ds, axis=0, keepdims=True)
                pds.append((p.astype(BF16), ds.astype(BF16), jnp.sum(ds, axis=-1, keepdims=True)))
            dks = [_dot(pds[hh][1], qm[hh]) for hh in range(NHS)]
            dvs = [_dot(pds[hh][0], dom[hh]) for hh in range(NHS)]
            dqs = [_dot_tn(pds[hh][1], kj[hh]) for hh in range(NHS)]
            for pr in range(PAIRS):
                dq_ref[rows, pr * LANE:(pr + 1) * LANE] += _pair(m0, dqs[2 * pr], dqs[2 * pr + 1]) * ATTN_SCALE
            return tuple((carry[hh][0] + dks[hh], carry[hh][1] + dvs[hh], carry[hh][2] - pds[hh][2]) for hh in range(NHS))

        one = (jnp.zeros((FB, LANE), F32), jnp.zeros((FB, LANE), F32), jnp.zeros((FB, 1), F32))
        first = (j * FB) // FT
        carry = step(first, (one,) * NHS, True)
        carry = lax.fori_loop(first + 1, SEQ // FT, lambda t, cr: step(t, cr, False), carry)
        for pr in range(PAIRS):
            (dka, dva, dfka), (dkb, dvb, dfkb) = carry[2 * pr], carry[2 * pr + 1]
            dk_ref[:, pr * LANE:(pr + 1) * LANE] = _pair(m0, dka, dkb)
            dv_ref[:, pr * LANE:(pr + 1) * LANE] = _pair(m0, dva, dvb)
            dfk_ref[0, 2 * pr, 0] = _col_to_row(dfka)
            dfk_ref[0, 2 * pr + 1, 0] = _col_to_row(dfkb)

    kblk = lambda grp: pl.BlockSpec((FB, W), lambda b, g, j, grp=grp: (b * n + j, grp * NG + g))
    full = pl.BlockSpec((SEQ, W), lambda b, g, j: (b, g))
    rowf = pl.BlockSpec((1, NHS, SEQ // FT, 1, FT), lambda b, g, j: (b, g, 0, 0, 0))
    rowb = pl.BlockSpec((1, NHS, 1, 1, FB), lambda b, g, j: (b, g, j, 0, 0))
    return pl.pallas_call(
        body, grid=(B, NG, n), in_specs=[full, kblk(1), kblk(2), full, rowf, rowf, rowb, rowf],
        out_specs=[full, kblk(0), kblk(0), rowf, rowb],
        out_shape=[SDS((T, WG), F32), SDS((T, WG), F32), SDS((T, WG), F32), SDS((B, NH, SEQ // FT, 1, FT), F32),
                   SDS((B, NH, n, 1, FB), F32)],
        compiler_params=_cp("arbitrary", "arbitrary", "arbitrary"), name=name,
    )(pb, pb, pb, do, lrow, drow, fblk, frow)


def _tri(lower):
    r = lax.broadcasted_iota(jnp.int32, (LANE, LANE), 0)
    c = lax.broadcasted_iota(jnp.int32, (LANE, LANE), 1)
    return ((r >= c) if lower else (r <= c)).astype(F32)


def _tri_dot(t, xblk):
    return jnp.dot(t, xblk, precision=lax.Precision.HIGHEST, preferred_element_type=F32)


def forget_cumsum(flog, bias, name):
    B, S, _ = flog.shape

    def body(f_ref, b_ref, o_ref):
        t = _tri(True)
        carry = jnp.zeros((1, LANE), F32)
        for blk in range(S // LANE):
            z = f_ref[0, blk * LANE:(blk + 1) * LANE, :] + b_ref[...]
            lf = jnp.minimum(z, 0.0) - jnp.log(1.0 + jnp.exp(-jnp.abs(z)))
            cs = _tri_dot(t, lf) + carry
            o_ref[0, blk * LANE:(blk + 1) * LANE, :] = cs
            carry = cs[LANE - 1:LANE, :]

    spec = pl.BlockSpec((1, S, LANE), lambda b: (b, 0, 0))
    return pl.pallas_call(
        body, grid=(B,), in_specs=[spec, pl.BlockSpec((1, LANE), lambda b: (0, 0))], out_specs=spec,
        out_shape=SDS((B, S, LANE), F32), compiler_params=_cp("arbitrary"), name=name,
    )(flog, bias)


def forget_cumsum_bwd(dF, flog, bias, name):
    B, S, _ = flog.shape

    def body(d_ref, f_ref, b_ref, o_ref, db_ref):
        @pl.when(pl.program_id(0) == 0)
        def _():
            db_ref[...] = jnp.zeros_like(db_ref)

        t = _tri(False)
        carry = jnp.zeros((1, LANE), F32)
        tot = jnp.zeros((1, LANE), F32)
        for blk in reversed(range(S // LANE)):
            sl = slice(blk * LANE, (blk + 1) * LANE)
            rc = _tri_dot(t, d_ref[0, sl, :]) + carry
            carry = rc[0:1, :]
            z = f_ref[0, sl, :] + b_ref[...]
            dz = rc * jax.nn.sigmoid(-z)
            o_ref[0, sl, :] = dz
            tot = tot + jnp.sum(dz, axis=0, keepdims=True)
        db_ref[...] += tot

    spec = pl.BlockSpec((1, S, LANE), lambda b: (b, 0, 0))
    vec = pl.BlockSpec((1, LANE), lambda b: (0, 0))
    return pl.pallas_call(
        body, grid=(B,), in_specs=[spec, spec, vec], out_specs=[spec, vec],
        out_shape=[SDS((B, S, LANE), F32), SDS((1, LANE), F32)], compiler_params=_cp("arbitrary"), name=name,
    )(dF, flog, bias)


def mixer_out_fwd(oa, ob, goa, gob, w_out, g_post, x, mod3, name):
    T = x.shape[0]

    def body(oa_ref, ob_ref, goa_ref, gob_ref, w_ref, gp_ref, x_ref, mod_ref, xo_ref, mg_ref, y0_ref):
        a = oa_ref[...]
        b = ob_ref[...]
        mg = jnp.concatenate([a * _rms(a) * goa_ref[...], b * _rms(b) * gob_ref[...]], axis=-1).astype(BF16)
        mg_ref[...] = mg
        y0 = _dot(mg, w_ref[...])
        y0_ref[...] = y0
        xo_ref[...] = x_ref[...] + mod_ref[0, 2:3, :] * (y0 * _rms(y0) * gp_ref[...])

    tok = pl.BlockSpec((TM, D), lambda i: (i, 0))
    half = pl.BlockSpec((TM, WG), lambda i: (i, 0))
    hv = pl.BlockSpec((1, WG), lambda i: (0, 0))
    return pl.pallas_call(
        body, grid=(T // TM,),
        in_specs=[half, half, hv, hv, pl.BlockSpec((D, D), lambda i: (0, 0)), pl.BlockSpec((1, D), lambda i: (0, 0)), tok,
                  pl.BlockSpec((1, 3, D), _mod_map)],
        out_specs=[tok, tok, tok], out_shape=[SDS((T, D), F32), SDS((T, D), BF16), SDS((T, D), F32)],
        compiler_params=_cp("arbitrary"), name=name,
    )(oa, ob, goa, gob, w_out, g_post, x, mod3)


def mixer_out_bwd(dxo, y0, mod3, g_post, w_out, oa, ob, goa, gob, name):
    T = dxo.shape[0]
    nb = T // SEQ
    tiles_per_seq = SEQ // TM

    def body(dxo_ref, y0_ref, mod_ref, gp_ref, w_ref, oa_ref, ob_ref, goa_ref, gob_ref,
             dy0_ref, doa_ref, dob_ref, dmg_ref, dgp_ref, dgoa_ref, dgob_ref, dvb_ref):
        i = pl.program_id(0)

        @pl.when(i == 0)
        def _():
            dgp_ref[...] = jnp.zeros_like(dgp_ref)
            dgoa_ref[...] = jnp.zeros_like(dgoa_ref)
            dgob_ref[...] = jnp.zeros_like(dgob_ref)

        @pl.when(i % tiles_per_seq == 0)
        def _():
            dmg_ref[...] = jnp.zeros_like(dmg_ref)

        dy0, dmg, dg = _post_bwd(dxo_ref[...], y0_ref[...], gp_ref[...], mod_ref[0, 2:3, :], 1.0)
        dmg_ref[0] += dmg
        dgp_ref[...] += dg
        db = dy0.astype(BF16)
        dy0_ref[...] = db
        dm = _dot_nt(db, w_ref[...])
        for o_ref, g_ref, do_ref, dg_ref, sl in ((oa_ref, goa_ref, doa_ref, dgoa_ref, slice(0, WG)),
                                                  (ob_ref, gob_ref, dob_ref, dgob_ref, slice(WG, 2 * WG))):
            o = o_ref[...]
            r = _rms(o)
            oh = o * r
            d = dm[:, sl]
            dg_ref[...] += jnp.sum(d * oh, axis=0, keepdims=True)
            dh = d * g_ref[...]
            do = r * (dh - oh * jnp.mean(dh * oh, axis=-1, keepdims=True))
            do_ref[...] = do
        ind = (lax.broadcasted_iota(jnp.int32, (WG, LANE), 0) // HD == lax.broadcasted_iota(jnp.int32, (WG, LANE), 1)).astype(BF16)
        prod = do * o
        hi = prod.astype(BF16)
        dvb_ref[...] = _dot(hi, ind) + _dot((prod - hi.astype(F32)).astype(BF16), ind)

    tok = pl.BlockSpec((TM, D), lambda i: (i, 0))
    half = pl.BlockSpec((TM, WG), lambda i: (i, 0))
    hv = pl.BlockSpec((1, WG), lambda i: (0, 0))
    vec = pl.BlockSpec((1, D), lambda i: (0, 0))
    return pl.pallas_call(
        body, grid=(T // TM,),
        in_specs=[tok, tok, pl.BlockSpec((1, 3, D), _mod_map), vec, pl.BlockSpec((D, D), lambda i: (0, 0)), half, half, hv, hv],
        out_specs=[tok, half, half, pl.BlockSpec((1, 1, D), _mod_map), vec, hv, hv, pl.BlockSpec((TM, LANE), lambda i: (i, 0))],
        out_shape=[SDS((T, D), BF16), SDS((T, WG), F32), SDS((T, WG), F32), SDS((nb, 1, D), F32), SDS((1, D), F32),
                   SDS((1, WG), F32), SDS((1, WG), F32), SDS((T, LANE), F32)],
        compiler_params=_cp("arbitrary"), name=name,
    )(dxo, y0, mod3, g_post, w_out, oa, ob, goa, gob)


def proj_grad_assemble(grads, rc, rs1, rs2, name):
    T = grads[0].shape[0]

    def body(*refs):
        ins, (c_ref, s1_ref, s2_ref, o_ref) = refs[:6], refs[6:]
        c, s1, s2 = c_ref[...], s1_ref[...], s2_ref[...]
        for grp in range(2):
            for k in range(WG // LANE):
                d = ins[grp][:, k * LANE:(k + 1) * LANE]
                un = d * c + pltpu.roll(d * s1, 8, 1) + pltpu.roll(d * s2, LANE - 8, 1)
                o_ref[:, grp * WG + k * LANE:grp * WG + (k + 1) * LANE] = un.astype(BF16)
        for g in range(2, 6):
            o_ref[:, g * WG:(g + 1) * WG] = ins[g][...].astype(BF16)

    half = pl.BlockSpec((TM, WG), lambda i: (i, 0))
    tab = pl.BlockSpec((TM, LANE), lambda i: (i, 0))
    return pl.pallas_call(
        body, grid=(T // TM,), in_specs=[half] * 6 + [tab] * 3, out_specs=pl.BlockSpec((TM, IN_MAIN), lambda i: (i, 0)),
        out_shape=SDS((T, IN_MAIN), BF16), compiler_params=_cp("arbitrary"), name=name,
    )(*grads, rc, rs1, rs2)


def mixer_proj_bwd(dproj, dflog, dxo, x, mod3, g_pre, w_main, w_f, name):
    T = x.shape[0]
    nb = T // SEQ
    tiles_per_seq = SEQ // TM

    def body(dp_ref, df_ref, dxo_ref, x_ref, mod_ref, g_ref, w_ref, wf_ref, dx_ref, dmod_ref, dg_ref):
        i = pl.program_id(0)

        @pl.when(i == 0)
        def _():
            dg_ref[...] = jnp.zeros_like(dg_ref)

        @pl.when(i % tiles_per_seq == 0)
        def _():
            dmod_ref[...] = jnp.zeros_like(dmod_ref)

        dh = _dot_nt(dp_ref[...], w_ref[...]) + _dot_nt(df_ref[...].astype(BF16), wf_ref[...])
        dx, dsh, dsc, dg = _norm_mod_bwd(dh, x_ref[...], g_ref[...], mod_ref[0, 1:2, :])
        dx_ref[...] = dxo_ref[...] + dx
        dmod_ref[0, 0:1, :] += dsh
        dmod_ref[0, 1:2, :] += dsc
        dg_ref[...] += dg

    tok = pl.BlockSpec((TM, D), lambda i: (i, 0))
    vec = pl.BlockSpec((1, D), lambda i: (0, 0))
    return pl.pallas_call(
        body, grid=(T // TM,),
        in_specs=[pl.BlockSpec((TM, IN_MAIN), lambda i: (i, 0)), pl.BlockSpec((TM, LANE), lambda i: (i, 0)), tok, tok,
                  pl.BlockSpec((1, 3, D), _mod_map), vec, pl.BlockSpec((D, IN_MAIN), lambda i: (0, 0)),
                  pl.BlockSpec((D, LANE), lambda i: (0, 0))],
        out_specs=[tok, pl.BlockSpec((1, 2, D), _mod_map), vec],
        out_shape=[SDS((T, D), F32), SDS((nb, 2, D), F32), SDS((1, D), F32)],
        compiler_params=_cp("arbitrary"), name=name,
    )(dproj, dflog, dxo, x, mod3, g_pre, w_main, w_f)


def ada_fwd(c_all, w, b, name):
    n = w.shape[1]
    tn = n // 2

    def body(c_ref, w_ref, b_ref, o_ref):
        cv = c_ref[...]
        o_ref[...] = _dot((cv * jax.nn.sigmoid(cv)).astype(BF16), w_ref[...].astype(BF16)) + b_ref[...]

    R = c_all.shape[0]
    return pl.pallas_call(
        body, grid=(2,),
        in_specs=[pl.BlockSpec((R, D), lambda j: (0, 0)), pl.BlockSpec((D, tn), lambda j: (0, j)), pl.BlockSpec((1, tn), lambda j: (0, j))],
        out_specs=pl.BlockSpec((R, tn), lambda j: (0, j)), out_shape=SDS((R, n), F32),
        compiler_params=_cp("arbitrary"), name=name,
    )(c_all, w, b)


def ada_bwd(c_all, dmod, name):
    R, n = dmod.shape
    tn = n // 2

    def body(c_ref, d_ref, o_ref):
        cv = c_ref[...]
        o_ref[...] = _dot_tn((cv * jax.nn.sigmoid(cv)).astype(BF16), d_ref[...].astype(BF16))

    return pl.pallas_call(
        body, grid=(2,), in_specs=[pl.BlockSpec((R, D), lambda j: (0, 0)), pl.BlockSpec((R, tn), lambda j: (0, j))],
        out_specs=pl.BlockSpec((D, tn), lambda j: (0, j)), out_shape=SDS((D, n), F32),
        compiler_params=_cp("arbitrary"), name=name,
    )(c_all, dmod)


def _adam_math(w, g, m, v):
    m2 = ADAM_B1 * m + (1.0 - ADAM_B1) * g
    v2 = ADAM_B2 * v + (1.0 - ADAM_B2) * (g * g)
    m_hat = m2 / (1.0 - ADAM_B1 ** ADAM_STEP)
    v_hat = v2 / (1.0 - ADAM_B2 ** ADAM_STEP)
    delta = -ADAM_LR * (m_hat / (jnp.sqrt(v_hat) + ADAM_EPS) + ADAM_WD * w)
    return delta, m2, v2


def adam_update(w, g, m, v, tr, name):
    _, R, C = w.shape

    def body(w_ref, g_ref, m_ref, v_ref, d_ref, mo_ref, vo_ref):
        d_ref[0], mo_ref[0], vo_ref[0] = _adam_math(w_ref[0], g_ref[...], m_ref[0], v_ref[0])

    spec = pl.BlockSpec((1, tr, C), lambda i: (0, i, 0))
    gspec = pl.BlockSpec((tr, C), lambda i: (i, 0))
    return pl.pallas_call(
        body, grid=(R // tr,), in_specs=[spec, gspec, spec, spec], out_specs=[spec] * 3, out_shape=[SDS((1, R, C), F32)] * 3,
        compiler_params=_cp("arbitrary"), name=name,
    )(w, g, m, v)


def adam_update_halves(w, mine, other, m, v, cidx, tr, name):
    _, R, C = w.shape
    nh = R // 2 // tr

    def body(c_ref, w_ref, a_ref, b_ref, m_ref, v_ref, g_ref, d_ref, mo_ref, vo_ref):
        first_half = pl.program_id(0) < nh
        g = jnp.where(first_half == (c_ref[0] == 0), a_ref[...], b_ref[...])
        g_ref[0] = g
        d_ref[0], mo_ref[0], vo_ref[0] = _adam_math(w_ref[0], g, m_ref[0], v_ref[0])

    spec = pl.BlockSpec((1, tr, C), lambda i, c_ref: (0, i, 0))
    hspec = pl.BlockSpec((tr, C), lambda i, c_ref: (i % nh, 0))
    return pl.pallas_call(
        body,
        grid_spec=pltpu.PrefetchScalarGridSpec(num_scalar_prefetch=1, grid=(R // tr,), in_specs=[spec, hspec, hspec, spec, spec],
                                               out_specs=[spec] * 4),
        out_shape=[SDS((1, R, C), F32)] * 4, compiler_params=_cp("arbitrary"), name=name,
    )(cidx, w, mine, other, m, v)


def vec_adam(parts, w, m, v, name):
    P, C = parts.shape

    def body(p_ref, w_ref, m_ref, v_ref, g_ref, d_ref, mo_ref, vo_ref):
        g = jnp.sum(p_ref[...], axis=0, keepdims=True)
        g_ref[...] = g
        d_ref[...], mo_ref[...], vo_ref[...] = _adam_math(w_ref[...], g, m_ref[...], v_ref[...])

    return pl.pallas_call(body, out_shape=[SDS((1, C), F32)] * 4, compiler_params=_cp(), name=name)(parts, w, m, v)


HBM = pl.BlockSpec(memory_space=pltpu.HBM)
VMEM = pl.BlockSpec(memory_space=pltpu.VMEM)


def _place():
    x, y, c = lax.axis_index("x"), lax.axis_index("y"), lax.axis_index("c")
    return x, y, c, [(1 - x, y), (x, 1 - y), (1 - x, 1 - y)]


def all_gather8(xs, name):
    R, C = xs.shape

    def body(x_ref, out_ref, send_sems, recv_sems, local_sem):
        x, y, c, chips = _place()
        me, sibling = (x, y, c), (x, y, 1 - c)

        def slot(px, py, pc):
            return out_ref.at[4 * px + 2 * py + pc]

        def copy(k, block, to, src=None):
            return pltpu.make_async_remote_copy(
                src_ref=slot(*block) if src is None else src, dst_ref=slot(*block),
                send_sem=send_sems.at[k], recv_sem=recv_sems.at[k], device_id=to, device_id_type=MESH)

        mine = pltpu.make_async_copy(x_ref, slot(*me), local_sem)
        mine.start()
        first = [copy(0, me, sibling, src=x_ref)]
        first += [copy(1 + j, me, (*chip, c), src=x_ref) for j, chip in enumerate(chips)]
        for cp in first:
            cp.start()
        passed = [copy(4 + j, (*chip, c), sibling) for j, chip in enumerate(chips)]
        for j, chip in enumerate(chips):
            copy(1 + j, (*chip, c), me).wait_recv()
            passed[j].start()
        copy(0, sibling, me).wait_recv()
        for j, chip in enumerate(chips):
            copy(4 + j, (*chip, 1 - c), me).wait_recv()
        for cp in first + passed:
            cp.wait_send()
        mine.wait()

    return pl.pallas_call(
        body, out_shape=SDS((N_DEV, R, C), xs.dtype), in_specs=[VMEM], out_specs=VMEM,
        scratch_shapes=[pltpu.SemaphoreType.DMA((7,)), pltpu.SemaphoreType.DMA((7,)), pltpu.SemaphoreType.DMA],
        compiler_params=pltpu.CompilerParams(vmem_limit_bytes=VMEM_LIMIT), name=name,
    )(xs)


class ShardGather:
    def __init__(self, shapes, splits):
        self.shapes, self.splits, self.n = shapes, splits, len(shapes)

    def scratch(self):
        n = self.n
        return [pltpu.SemaphoreType.DMA((n, 6)), pltpu.SemaphoreType.DMA((n, 6)), pltpu.SemaphoreType.DMA((n,))]

    def out_shapes(self, dtype):
        return [SDS((N_SHARD,) + tuple(s), dtype) for s in self.shapes]

    def _half(self, ref, k, cc):
        lo, hi = (0, self.splits[k]) if cc == 0 else (self.splits[k], self.shapes[k][0])
        return ref.at[pl.ds(lo, hi - lo)]

    def _phase(self, w_refs, o_refs, sems, finish):
        send_sems, recv_sems, local_sems = sems
        x, y, c, chips = _place()
        sibling = (x, y, 1 - c)
        me_s = 2 * x + y

        def rcopy(src, dst, k, s, to):
            return pltpu.make_async_remote_copy(src_ref=src, dst_ref=dst, send_sem=send_sems.at[k, s],
                                                recv_sem=recv_sems.at[k, s], device_id=to, device_id_type=MESH)

        for cc in (0, 1):
            @pl.when(c == cc)
            def _():
                local = [pltpu.make_async_copy(w_refs[k], o_refs[k].at[me_s], local_sems.at[k]) for k in range(self.n)]
                first = [rcopy(self._half(w_refs[k], k, cc), self._half(o_refs[k].at[me_s], k, cc), k, j, (*chip, c))
                         for k in range(self.n) for j, chip in enumerate(chips)]
                if not finish:
                    for cp in local + first:
                        cp.start()
                    return
                passed = []
                for k in range(self.n):
                    for j, chip in enumerate(chips):
                        land = self._half(o_refs[k].at[2 * chip[0] + chip[1]], k, cc)
                        rcopy(land, land, k, j, (*chip, c)).wait_recv()
                        f = rcopy(land, land, k, 3 + j, sibling)
                        f.start()
                        passed.append(f)
                for k in range(self.n):
                    for j, chip in enumerate(chips):
                        other = self._half(o_refs[k].at[2 * chip[0] + chip[1]], k, 1 - cc)
                        rcopy(other, other, k, 3 + j, sibling).wait_recv()
                for s in first + passed:
                    s.wait_send()
                for cp in local:
                    cp.wait()

    def start(self, w_refs, o_refs, sems):
        self._phase(w_refs, o_refs, sems, False)

    def finish(self, w_refs, o_refs, sems):
        self._phase(w_refs, o_refs, sems, True)


def all_gather_shards(ws, splits, name):
    n = len(ws)
    plan = ShardGather([w.shape for w in ws], splits)

    def body(*refs):
        plan.start(refs[:n], refs[n:2 * n], refs[2 * n:])
        plan.finish(refs[:n], refs[n:2 * n], refs[2 * n:])

    return pl.pallas_call(
        body, out_shape=plan.out_shapes(ws[0].dtype), in_specs=[HBM] * n, out_specs=[HBM] * n,
        scratch_shapes=plan.scratch(), name=name,
    )(*ws)


def sibling_send_half(gs, name):
    n = len(gs)

    def body(*refs):
        g_refs, o_refs = refs[:n], refs[n:2 * n]
        send_sems, recv_sems = refs[2 * n:]
        x, y, c, _ = _place()
        cps = []
        for k in range(n):
            hr = gs[k].shape[1] // 2
            src = g_refs[k].at[:, pl.ds(pl.multiple_of((1 - c) * hr, 8), hr)]
            cp = pltpu.make_async_remote_copy(src_ref=src, dst_ref=o_refs[k], send_sem=send_sems.at[k], recv_sem=recv_sems.at[k],
                                              device_id=(x, y, 1 - c), device_id_type=MESH)
            cp.start()
            cps.append(cp)
        for cp in cps:
            cp.wait()

    return pl.pallas_call(
        body, out_shape=[SDS((N_SHARD, g.shape[1] // 2, g.shape[2]), g.dtype) for g in gs], in_specs=[HBM] * n, out_specs=[HBM] * n,
        scratch_shapes=[pltpu.SemaphoreType.DMA((n,)), pltpu.SemaphoreType.DMA((n,))], name=name,
    )(*gs)


def _scatter_copies(h_refs, o_refs, send_sems, recv_sems):
    _, _, c, chips = _place()
    return [pltpu.make_async_remote_copy(
        src_ref=h_refs[k].at[2 * chip[0] + chip[1]], dst_ref=o_refs[k].at[j], send_sem=send_sems.at[k, j],
        recv_sem=recv_sems.at[k, j], device_id=(*chip, c), device_id_type=MESH)
        for k in range(len(h_refs)) for j, chip in enumerate(chips)]


def chip_scatter(hs, name):
    n = len(hs)

    def body(*refs):
        cps = _scatter_copies(refs[:n], refs[n:2 * n], *refs[2 * n:])
        for cp in cps:
            cp.start()
        for cp in cps:
            cp.wait()

    return pl.pallas_call(
        body, out_shape=[SDS((3,) + h.shape[1:], h.dtype) for h in hs], in_specs=[HBM] * n, out_specs=[HBM] * n,
        scratch_shapes=[pltpu.SemaphoreType.DMA((n, 3)), pltpu.SemaphoreType.DMA((n, 3))], name=name,
    )(*hs)


def sibling_swap(ghs, name):
    n = len(ghs)

    def body(*refs):
        g_refs, o_refs = refs[:n], refs[n:2 * n]
        send_sems, recv_sems = refs[2 * n:]
        x, y, c, _ = _place()
        cps = []
        for k in range(n):
            cp = pltpu.make_async_remote_copy(src_ref=g_refs[k], dst_ref=o_refs[k], send_sem=send_sems.at[k],
                                              recv_sem=recv_sems.at[k], device_id=(x, y, 1 - c), device_id_type=MESH)
            cp.start()
            cps.append(cp)
        for cp in cps:
            cp.wait()

    return pl.pallas_call(
        body, out_shape=[SDS(g.shape, g.dtype) for g in ghs], in_specs=[HBM] * n, out_specs=[HBM] * n,
        scratch_shapes=[pltpu.SemaphoreType.DMA((n,)), pltpu.SemaphoreType.DMA((n,))], name=name,
    )(*ghs)


def pair_sum(g, ra, cidx, name):
    _, r, cols = g.shape
    hr = r // 2

    def body(c_ref, g_ref, a_ref, o_ref):
        o_ref[...] = (g_ref[...] + a_ref[...]).astype(BF16)

    return pl.pallas_call(
        body,
        grid_spec=pltpu.PrefetchScalarGridSpec(
            num_scalar_prefetch=1, grid=(N_SHARD,),
            in_specs=[pl.BlockSpec((1, hr, cols), lambda s, c_ref: (s, c_ref[0], 0)),
                      pl.BlockSpec((1, hr, cols), lambda s, c_ref: (s, 0, 0))],
            out_specs=pl.BlockSpec((1, hr, cols), lambda s, c_ref: (s, 0, 0))),
        out_shape=SDS((N_SHARD, hr, cols), BF16), compiler_params=_cp("arbitrary"), name=name,
    )(cidx, g, ra)


def chip_sum(h, rb, sidx, name):
    _, hr, cols = h.shape

    def body(s_ref, h_ref, r_ref, o_ref):
        o_ref[...] = ((h_ref[0].astype(F32) + r_ref[0].astype(F32)) + r_ref[1].astype(F32)) + r_ref[2].astype(F32)

    return pl.pallas_call(
        body,
        grid_spec=pltpu.PrefetchScalarGridSpec(
            num_scalar_prefetch=1, grid=(1,),
            in_specs=[pl.BlockSpec((1, hr, cols), lambda i, s_ref: (s_ref[0], 0, 0)),
                      pl.BlockSpec((3, hr, cols), lambda i, s_ref: (0, 0, 0))],
            out_specs=pl.BlockSpec((hr, cols), lambda i, s_ref: (0, 0))),
        out_shape=SDS((hr, cols), F32), compiler_params=_cp("arbitrary"), name=name,
    )(sidx, h, rb)


def _shard_cols(g, n_valid):
    r = g.shape[0]
    return g[:, :n_valid].reshape(r, N_SHARD, n_valid // N_SHARD).transpose(1, 0, 2)


def _unshard_cols(o, pad_to):
    _, r, n = o.shape
    full = o.transpose(1, 0, 2).reshape(r, N_SHARD * n)
    return jnp.pad(full, ((0, 0), (0, pad_to - N_SHARD * n)))


def _rows_of_tiles(t):
    B, H, S = t.shape
    return t.reshape(B, H, S // FT, 1, FT)


def mixer_fwd(x1, mod3, g_pre, w_main, w_f, b_forget_pad, goa, gob, w_out, g_post, tabs, nb):
    hmix, pa, pb, flog = mixer_proj(x1, mod3, g_pre, w_main, w_f, *tabs, name="mixer_proj")
    out_a, lse_a = band_fwd(pa, name="band_fwd")
    F = forget_cumsum(flog.reshape(nb, SEQ, LANE), b_forget_pad, name="forget_cumsum")
    Fh = F[:, :, :NH].transpose(0, 2, 1)
    fblk = Fh.reshape(nb, NH, SEQ // FB, 1, FB)
    frow = _rows_of_tiles(Fh)
    out_b, lse_b = fox_fwd(pb, fblk, frow, name="fox_fwd")
    x2, merged, y0m = mixer_out_fwd(out_a, out_b, goa, gob, w_out, g_post, x1, mod3, name="mixer_out_fwd")
    res = dict(hmix=hmix, flog=flog, pa=pa, pb=pb, out_a=out_a, lse_a=lse_a, fblk=fblk, frow=frow, out_b=out_b,
               lrow=_rows_of_tiles(lse_b.reshape(nb, NH, SEQ)), merged=merged, y0m=y0m)
    return x2, res


def mixer_bwd(dx2, x1, mod3, g_pre, w_main, w_f, b_forget_pad, goa, gob, w_out, g_post, tabs, res, nb):
    T = nb * SEQ
    dy0m, doa, dob, dmgate, dg_post, dgoa, dgob, dvec_b = mixer_out_bwd(
        dx2, res["y0m"], mod3, g_post, w_out, res["out_a"], res["out_b"], goa, gob, name="mixer_out_bwd")
    dqa, dka, dva = band_bwd(res["pa"], doa, res["out_a"], res["lse_a"], name="band_bwd")
    drow = _rows_of_tiles(dvec_b[:, :NH].reshape(nb, SEQ, NH).transpose(0, 2, 1))
    dqb, dkb, dvb, dfq, dfk = fox_bwd(res["pb"], dob, res["lrow"], drow, res["fblk"], res["frow"], name="fox_bwd")
    dF = (dfq.reshape(nb, NH, SEQ) + dfk.reshape(nb, NH, SEQ)).transpose(0, 2, 1)
    dF = jnp.pad(dF, ((0, 0), (0, 0), (0, LANE - NH)))
    dflog, dbf = forget_cumsum_bwd(dF, res["flog"].reshape(nb, SEQ, LANE), b_forget_pad, name="forget_cumsum_bwd")
    dflog = dflog.reshape(T, LANE)
    dproj = proj_grad_assemble((dqa, dka, dva, dqb, dkb, dvb), *tabs, name="proj_grad_assemble")
    dx1, dmod2, dg_pre = mixer_proj_bwd(dproj, dflog, dx2, x1, mod3, g_pre, w_main, w_f, name="mixer_proj_bwd")
    g_main = matmul_tn(res["hmix"], dproj, D, 1024, 1024, name="grad_w_in")
    g_f = matmul_tn(res["hmix"], dflog.astype(BF16), D, LANE, 1024, name="grad_w_forget")
    g_out = matmul_tn(res["merged"], dy0m, D, D, 1024, name="grad_w_out")
    dmod3 = jnp.concatenate([dmod2, dmgate], axis=1)
    return dx1, dmod3, dict(g_pre=dg_pre, g_post=dg_post, goa=dgoa, gob=dgob, b_forget=dbf[:, :NH],
                            w_in=jnp.concatenate([g_main, g_f[:, :NH]], axis=1), w_out=g_out)


def ffn_grads(h, dy0, act, dgate, dup, pre):
    g_gate = matmul_tn(h, dgate, D, FF_TN, 1024, name=pre + "_grad_gate")
    g_up = matmul_tn(h, dup, D, FF_TN, 1024, name=pre + "_grad_up")
    g_down = matmul_tn(act, dy0, FF_TN, D, 1024, name=pre + "_grad_down")
    return g_gate, g_up, g_down


def local_step(x0, tgt, pos_col, mod, wfull, p, late_weights=None, early_grads=None):
    T = x0.shape[0]
    nb = T // SEQ
    mod_ff1, mod_mix, mod_ff2 = mod[:, 0:3], mod[:, 3:6], mod[:, 6:9]
    tabs = rope_tables(pos_col, name="rope_tables")
    bf_pad = jnp.pad(p["b_forget"], ((0, 0), (0, LANE - NH)))

    (x1, h1, gate1, up1, y01), gathered = ffn_fwd(
        x0, mod_ff1, p["g_pre_ff1"], p["g_post_ff1"], wfull["w_ff1_gate"], wfull["w_ff1_up"], wfull["w_ff1_down"], 0.5,
        name="ff1_fwd", gather=None if late_weights is None else late_weights[:2])
    if late_weights is not None:
        wfull = {**wfull, **late_weights[2](gathered)}
    x2, res = mixer_fwd(x1, mod_mix, p["g_pre_mix"], wfull["w_main"], wfull["w_f"], bf_pad, p["g_out_a"], p["g_out_b"],
                        wfull["w_out"], p["g_post_mix"], tabs, nb)
    (x3, h2, gate2, up2, y02), _ = ffn_fwd(x2, mod_ff2, p["g_pre_ff2"], p["g_post_ff2"], wfull["w_ff2_gate"],
                                           wfull["w_ff2_up"], wfull["w_ff2_down"], 0.5, name="ff2_fwd")

    dx3, loss_part = loss_grad(x3, tgt, name="loss_grad")
    (dx2, dy02, act2, dgate2, dup2, dmod_ff2, dgpre2, dgpost2), _ = ffn_bwd(
        dx3, x2, y02, mod_ff2, p["g_pre_ff2"], p["g_post_ff2"], gate2, up2, wfull["w_ff2_gate"], wfull["w_ff2_up"],
        wfull["w_ff2_down"], 0.5, name="ff2_bwd")
    gw = {}
    gw["w_ff2_gate"], gw["w_ff2_up"], gw["w_ff2_down"] = ffn_grads(h2, dy02, act2, dgate2, dup2, "ff2")
    dx1, dmod_mix, gmix = mixer_bwd(dx2, x1, mod_mix, p["g_pre_mix"], wfull["w_main"], wfull["w_f"], bf_pad, p["g_out_a"],
                                    p["g_out_b"], wfull["w_out"], p["g_post_mix"], tabs, res, nb)
    gw["w_in"], gw["w_out"] = gmix["w_in"], gmix["w_out"]
    (dx0, dy01, act1, dgate1, dup1, dmod_ff1, dgpre1, dgpost1), scattered = ffn_bwd(
        dx1, x0, y01, mod_ff1, p["g_pre_ff1"], p["g_post_ff1"], gate1, up1, wfull["w_ff1_gate"], wfull["w_ff1_up"],
        wfull["w_ff1_down"], 0.5, name="ff1_bwd", scatter=None if early_grads is None else early_grads(gw))
    gw["w_ff1_gate"], gw["w_ff1_up"], gw["w_ff1_down"] = ffn_grads(h1, dy01, act1, dgate1, dup1, "ff1")
    dmod = jnp.concatenate([dmod_ff1, dmod_mix, dmod_ff2], axis=1).reshape(nb, 9 * D)
    small = dict(g_pre_ff1=dgpre1, g_post_ff1=dgpost1, g_pre_mix=gmix["g_pre"], g_post_mix=gmix["g_post"], g_pre_ff2=dgpre2,
                 g_post_ff2=dgpost2, g_out_a=gmix["goa"], g_out_b=gmix["gob"], b_forget=gmix["b_forget"])
    return loss_part, dx0, dmod, gw, small, scattered


def kernel(x, c, positions, w_ada, b_ada, g_pre_ff1, g_post_ff1, w_ff1_gate, w_ff1_up, w_ff1_down, g_pre_mix, g_post_mix, w_in, b_forget, g_out_a, g_out_b, w_out, g_pre_ff2, g_post_ff2, w_ff2_gate, w_ff2_up, w_ff2_down, loss_target, m_w_ada, m_b_ada, m_g_pre_ff1, m_g_post_ff1, m_w_ff1_gate, m_w_ff1_up, m_w_ff1_down, m_g_pre_mix, m_g_post_mix, m_w_in, m_b_forget, m_g_out_a, m_g_out_b, m_w_out, m_g_pre_ff2, m_g_post_ff2, m_w_ff2_gate, m_w_ff2_up, m_w_ff2_down, v_w_ada, v_b_ada, v_g_pre_ff1, v_g_post_ff1, v_w_ff1_gate, v_w_ff1_up, v_w_ff1_down, v_g_pre_mix, v_g_post_mix, v_w_in, v_b_forget, v_g_out_a, v_g_out_b, v_w_out, v_g_pre_ff2, v_g_post_ff2, v_w_ff2_gate, v_w_ff2_up, v_w_ff2_down):
    args = dict(locals())
    nb = x.shape[0]
    T = nb * SEQ
    ax, ay, ac = lax.axis_index("x"), lax.axis_index("y"), lax.axis_index("c")
    shard = 2 * ax + ay
    cidx = jnp.reshape(ac, (1,)).astype(jnp.int32)
    sidx = jnp.reshape(shard, (1,)).astype(jnp.int32)

    big = ["w_ff1_gate", "w_ff1_up", "w_ff1_down", "w_in", "w_out", "w_ff2_gate", "w_ff2_up", "w_ff2_down"]
    vecs = ["g_pre_ff1", "g_post_ff1", "g_pre_mix", "g_post_mix", "g_pre_ff2", "g_post_ff2"]

    first, late = big[:3], big[3:]
    splits = dict(zip(big, [512, 512, 352, 512, 128, 512, 512, 352]))

    def assemble(names, gathered):
        out = {}
        for n, o in zip(names, gathered):
            if n.endswith("gate") or n.endswith("up"):
                out[n] = _unshard_cols(o, DFF_PAD)
            elif n.endswith("down"):
                out[n] = jnp.pad(o.reshape(DFF, D), ((0, DFF_PAD - DFF), (0, 0)))
            elif n == "w_in":
                full = _unshard_cols(o, IN_COLS)
                out["w_main"] = full[:, :IN_MAIN]
                out["w_f"] = jnp.pad(full[:, IN_MAIN:], ((0, 0), (0, LANE - NH)))
            else:
                out[n] = o.reshape(D, D)
        return out

    wfull = assemble(first, all_gather_shards([args[n][0].astype(BF16) for n in first], [splits[n] for n in first],
                                              name="all_gather_weights"))
    late_weights = ([args[n][0].astype(BF16) for n in late], [splits[n] for n in late], functools.partial(assemble, late))

    ncol = w_ada.shape[2]
    c_all = all_gather8(c, name="all_gather_c").reshape(N_DEV * nb, D)
    b_loc = lax.dynamic_slice(b_ada, (0, shard * ncol), (1, ncol))
    mod_loc = ada_fwd(c_all, w_ada[0], b_loc, name="ada_fwd")
    mod_g = all_gather8(mod_loc, name="all_gather_mod")
    row0 = (4 * ax + 2 * ay + ac) * nb
    mod_rows = lax.dynamic_slice(mod_g, (0, row0, 0), (N_DEV, nb, ncol))
    mod = jnp.concatenate([mod_rows[2 * s] for s in range(N_SHARD)], axis=-1).reshape(nb, 9, D)

    small_in = dict(g_pre_ff1=g_pre_ff1, g_post_ff1=g_post_ff1, g_pre_mix=g_pre_mix, g_post_mix=g_post_mix, g_pre_ff2=g_pre_ff2,
                    g_post_ff2=g_post_ff2, g_out_a=g_out_a, g_out_b=g_out_b, b_forget=b_forget)
    def shard_blocked(n, g):
        if n.endswith("gate") or n.endswith("up"):
            return _shard_cols(g, DFF)
        if n.endswith("down"):
            return g[:DFF].reshape(N_SHARD, DFF // N_SHARD, D)
        if n == "w_in":
            return _shard_cols(g, IN_COLS)
        return g.reshape(N_SHARD, D // N_SHARD, D)

    def chip_sums(names, gw, tag):
        gsb = [shard_blocked(n, gw[n]) for n in names]
        ras = sibling_send_half(gsb, name="grad_sibling_send_" + tag)
        return [pair_sum(g, ra, cidx, name=f"grad_pair_sum_{n}") for n, g, ra in zip(names, gsb, ras)]

    hs = {}

    def early_grads(gw):
        hs.update(zip(late, chip_sums(late, gw, "late")))
        return [hs[n] for n in late]

    loss_part, dx0, dmod, gw, small, rbs_late = local_step(
        x.reshape(T, D), loss_target.reshape(T, D), positions.reshape(T, 1), mod, wfull, small_in, late_weights, early_grads)

    dmod_all = all_gather8(dmod, name="all_gather_dmod").reshape(N_DEV * nb, 9 * D)
    dmod_loc = lax.dynamic_slice(dmod_all, (0, shard * ncol), (N_DEV * nb, ncol))
    g_w_ada = ada_bwd(c_all, dmod_loc, name="ada_bwd")

    hs.update(zip(first, chip_sums(first, gw, "first")))
    rbs = dict(zip(late, rbs_late))
    rbs.update(zip(first, chip_scatter([hs[n] for n in first], name="grad_chip_scatter")))
    ghs = [chip_sum(hs[n], rbs[n], sidx, name=f"grad_chip_sum_{n}") for n in big]
    theirs = sibling_swap(ghs, name="grad_sibling_swap")

    row6 = jnp.concatenate([small["g_out_a"], small["g_out_b"]], axis=1)
    row7 = jnp.concatenate([small["b_forget"], loss_part[0:1, 0:1], jnp.zeros((1, D - NH - 1), F32)], axis=1)
    pack = jnp.concatenate([small[n] for n in vecs] + [row6, row7], axis=0)
    packed = all_gather8(pack, name="all_gather_small").reshape(N_DEV, 8 * D)

    def pack_state(pre):
        r6 = jnp.concatenate([args[pre + "g_out_a"], args[pre + "g_out_b"]], axis=1)
        r7 = jnp.pad(args[pre + "b_forget"], ((0, 0), (0, D - NH)))
        return jnp.concatenate([args[pre + n] for n in vecs] + [r6, r7], axis=0).reshape(1, 8 * D)

    sg, sd, sm, sv = (t.reshape(8, D) for t in vec_adam(packed, pack_state(""), pack_state("m_"), pack_state("v_"), name="adam_small"))

    def unpack(t):
        out = {n: t[i:i + 1] for i, n in enumerate(vecs)}
        out["g_out_a"], out["g_out_b"], out["b_forget"] = t[6:7, :WG], t[6:7, WG:], t[7:8, :NH]
        return out

    outs = dict(grad=unpack(sg), delta=unpack(sd), new_m=unpack(sm), new_v=unpack(sv))
    loss = sg[7, NH]
    outs["grad"]["b_ada"], outs["delta"]["b_ada"], outs["new_m"]["b_ada"], outs["new_v"]["b_ada"] = vec_adam(
        dmod_all, b_ada, m_b_ada, v_b_ada, name="adam_b_ada")

    for n, mine, other in zip(big, ghs, theirs):
        tr = 128 if mine.shape[0] % 128 == 0 else mine.shape[0]
        outs["grad"][n], outs["delta"][n], outs["new_m"][n], outs["new_v"][n] = adam_update_halves(
            args[n], mine, other, args["m_" + n], args["v_" + n], cidx, tr, name="adam_" + n)
    outs["delta"]["w_ada"], outs["new_m"]["w_ada"], outs["new_v"]["w_ada"] = adam_update(
        w_ada, g_w_ada, m_w_ada, v_w_ada, 128, name="adam_w_ada")
    outs["grad"]["w_ada"] = g_w_ada[None]

    order = ["w_ada", "b_ada", "g_pre_ff1", "g_post_ff1", "w_ff1_gate", "w_ff1_up", "w_ff1_down", "g_pre_mix", "g_post_mix", "w_in",
             "b_forget", "g_out_a", "g_out_b", "w_out", "g_pre_ff2", "g_post_ff2", "w_ff2_gate", "w_ff2_up", "w_ff2_down"]
    result = [loss, dx0.reshape(nb, SEQ, D)]
    for kind in ("grad", "delta", "new_m", "new_v"):
        result += [outs[kind][n] for n in order]
    return tuple(result)
```

```python
import functools
import math

import jax
import jax.numpy as jnp
from jax import lax
from jax.experimental import pallas as pl
from jax.experimental.pallas import tpu as pltpu

D = 1024
SEQ = 2048
HD = 64
NH = 8
WG = NH * HD
DFF = 2752
DFF_PAD = 2816
IN_MAIN = 6 * WG
IN_COLS = IN_MAIN + NH
N_SHARD = 4
N_DEV = 8
LANE = 128
QB = 128
FB = 256
FT = 512
FOX_PAIRS = 2
FOX_PAIRS_BWD = 1
BAND_UNROLL = 4
BAND_UNROLL_BWD = 4
PATTERNS = ((1, 16), (4, 4), (16, 1))
ROPE_THETA = 500000.0
EPS = 1e-6
NEG = -1e30
ATTN_SCALE = HD ** -0.5
TM = 512
TM_FFN = 512
TM_BWD = 256
VMEM_LIMIT = 56 * 1024 * 1024

ADAM_LR, ADAM_B1, ADAM_B2, ADAM_EPS, ADAM_WD, ADAM_STEP = 0.001, 0.9, 0.999, 1e-08, 0.01, 10

F32 = jnp.float32
BF16 = jnp.bfloat16
MESH = pl.DeviceIdType.MESH
SDS = jax.ShapeDtypeStruct


def _cp(*sem):
    return pltpu.CompilerParams(dimension_semantics=sem, vmem_limit_bytes=VMEM_LIMIT)


def _dot(a, b):
    return jnp.dot(a, b, preferred_element_type=F32)


def _dot_nt(a, b):
    return lax.dot_general(a, b, (((1,), (1,)), ((), ())), preferred_element_type=F32)


def _dot_tn(a, b):
    return lax.dot_general(a, b, (((0,), (0,)), ((), ())), preferred_element_type=F32)


def _rms(xf):
    return lax.rsqrt(jnp.mean(xf * xf, axis=-1, keepdims=True) + EPS)


def _norm_mod_bwd(dh, xf, g, scale):
    r = _rms(xf)
    xh = xf * r
    dsh = jnp.sum(dh, axis=0, keepdims=True)
    dsc = jnp.sum(dh * (xh * g), axis=0, keepdims=True)
    dn = dh * (1.0 + scale)
    dg = jnp.sum(dn * xh, axis=0, keepdims=True)
    dxh = dn * g
    dx = r * (dxh - xh * jnp.mean(dxh * xh, axis=-1, keepdims=True))
    return dx, dsh, dsc, dg


def _post_bwd(dxo, y0, g, mgate, gs):
    r = _rms(y0)
    yh = y0 * r
    dmg = gs * jnp.sum(dxo * (yh * g), axis=0, keepdims=True)
    dy = (gs * mgate) * dxo
    dg = jnp.sum(dy * yh, axis=0, keepdims=True)
    dyh = dy * g
    dy0 = r * (dyh - yh * jnp.mean(dyh * yh, axis=-1, keepdims=True))
    return dy0, dmg, dg


def _mod_map(i, *_):
    return ((i * TM) // SEQ, 0, 0)


FF_TN = 1408
FF_TILES = ((0, 768), (768, 1536), (1536, 2304), (2304, 2816))


def _resident_scratch():
    return [pltpu.VMEM((D, DFF_PAD), BF16), pltpu.VMEM((D, DFF_PAD), BF16), pltpu.VMEM((DFF_PAD, D), BF16),
            pltpu.SemaphoreType.DMA((3,))]


def _load_resident(first_step, srcs, dsts, sems):
    @pl.when(first_step)
    def _():
        cps = [pltpu.make_async_copy(s, d, sems.at[k]) for k, (s, d) in enumerate(zip(srcs, dsts))]
        for cp in cps:
            cp.start()
        for cp in cps:
            cp.wait()


def ffn_fwd(x, mod3, g_pre, g_post, wg, wu, wd, gs, name, gather=None):
    T = x.shape[0]
    tm = TM_FFN
    ng = 0 if gather is None else len(gather[0])
    plan = None if gather is None else ShardGather([w.shape for w in gather[0]], gather[1])

    def body(*refs):
        x_ref, mod_ref, gpre_ref, gpost_ref = refs[:4]
        xo_ref, h_ref, gate_ref, up_ref, y0_ref = refs[7 + ng:12 + ng]
        wg_ref, wu_ref, wd_ref, wsem = refs[12 + 2 * ng:16 + 2 * ng]
        i = pl.program_id(0)
        if plan is not None:
            comm = (refs[7:7 + ng], refs[12 + ng:12 + 2 * ng], refs[16 + 2 * ng:])
            pl.when(i == 0)(lambda: plan.start(*comm))
        _load_resident(i == 0, refs[4:7], (wg_ref, wu_ref, wd_ref), wsem)

        xf = x_ref[...]
        hb = ((xf * _rms(xf) * gpre_ref[...]) * (1.0 + mod_ref[0, 1:2, :]) + mod_ref[0, 0:1, :]).astype(BF16)
        h_ref[...] = hb
        y0 = None
        for lo, hi in FF_TILES:
            gate = _dot(hb, wg_ref[:, lo:hi])
            up = _dot(hb, wu_ref[:, lo:hi])
            gate_ref[:, lo:hi] = gate.astype(BF16)
            up_ref[:, lo:hi] = up.astype(BF16)
            part = _dot((gate * jax.nn.sigmoid(gate) * up).astype(BF16), wd_ref[lo:hi, :])
            y0 = part if y0 is None else y0 + part
        y0_ref[...] = y0
        xo_ref[...] = xf + (gs * mod_ref[0, 2:3, :]) * (y0 * _rms(y0) * gpost_ref[...])

        if plan is not None:
            pl.when(i == T // tm - 1)(lambda: plan.finish(*comm))

    tok = pl.BlockSpec((tm, D), lambda i: (i, 0))
    vec = pl.BlockSpec((1, D), lambda i: (0, 0))
    hid = pl.BlockSpec((tm, DFF_PAD), lambda i: (i, 0))
    outs = pl.pallas_call(
        body, grid=(T // tm,),
        in_specs=[tok, pl.BlockSpec((1, 3, D), lambda i: ((i * tm) // SEQ, 0, 0)), vec, vec, HBM, HBM, HBM] + [HBM] * ng,
        out_specs=[tok, tok, hid, hid, tok] + [HBM] * ng,
        out_shape=[SDS((T, D), F32), SDS((T, D), BF16), SDS((T, DFF_PAD), BF16), SDS((T, DFF_PAD), BF16), SDS((T, D), F32)]
        + ([] if plan is None else plan.out_shapes(BF16)),
        scratch_shapes=_resident_scratch() + ([] if plan is None else plan.scratch()),
        compiler_params=_cp("arbitrary"), name=name,
    )(x, mod3, g_pre, g_post, wg, wu, wd, *([] if gather is None else gather[0]))
    return outs[:5], outs[5:]


def ffn_bwd(dxo, x, y0, mod3, g_pre, g_post, gate, up, wg, wu, wd, gs, name, scatter=None, target=None):
    assert scatter is None or target is None
    T = x.shape[0]
    nb = T // SEQ
    tm = TM_BWD
    tiles_per_seq = SEQ // tm
    ns = 0 if scatter is None else len(scatter)
    ne = ns + (target is not None)

    def body(*refs):
        dxo_ref, x_ref, y0_ref, mod_ref, gpre_ref, gpost_ref, gate_ref, up_ref = refs[:8]
        dx_ref, dy0_ref, act_ref, dgate_ref, dup_ref, dmod_ref, dgpre_ref, dgpost_ref = refs[11 + ne:19 + ne]
        wg_ref, wu_ref, wd_ref, wsem = refs[19 + 2 * ne:23 + 2 * ne]
        i = pl.program_id(0)
        _load_resident(i == 0, refs[8:11], (wg_ref, wu_ref, wd_ref), wsem)
        if ns:
            comm = (refs[11:11 + ns], refs[19 + ns:19 + 2 * ns], *refs[23 + 2 * ns:])

            @pl.when(i == 0)
            def _():
                for cp in _scatter_copies(*comm):
                    cp.start()

        @pl.when(i == 0)
        def _():
            dgpre_ref[...] = jnp.zeros_like(dgpre_ref)
            dgpost_ref[...] = jnp.zeros_like(dgpost_ref)

        @pl.when(i % tiles_per_seq == 0)
        def _():
            dmod_ref[...] = jnp.zeros_like(dmod_ref)

        dxo = dxo_ref[...]
        if target is not None:
            loss_ref = refs[19 + ne]

            @pl.when(i == 0)
            def _():
                loss_ref[...] = jnp.zeros_like(loss_ref)

            err = dxo - refs[11][...]
            loss_ref[...] += jnp.sum(err * err) * (0.5 / D)
            dxo = err * (1.0 / D)
        dy0, dmg, dg = _post_bwd(dxo, y0_ref[...], gpost_ref[...], mod_ref[0, 2:3, :], gs)
        dmod_ref[0, 2:3, :] += dmg
        dgpost_ref[...] += dg
        db = dy0.astype(BF16)
        dy0_ref[...] = db
        dh = None
        for lo, hi in FF_TILES:
            dact = _dot_nt(db, wd_ref[lo:hi, :])
            g = gate_ref[:, lo:hi].astype(F32)
            u = up_ref[:, lo:hi].astype(F32)
            sig = jax.nn.sigmoid(g)
            sl = g * sig
            dgate = (dact * u * (sig * (1.0 + g * (1.0 - sig)))).astype(BF16)
            dup = (dact * sl).astype(BF16)
            act_ref[:, lo:hi] = (sl * u).astype(BF16)
            dgate_ref[:, lo:hi] = dgate
            dup_ref[:, lo:hi] = dup
            part = _dot_nt(dgate, wg_ref[:, lo:hi]) + _dot_nt(dup, wu_ref[:, lo:hi])
            dh = part if dh is None else dh + part
        dx, dsh, dsc, dg = _norm_mod_bwd(dh, x_ref[...], gpre_ref[...], mod_ref[0, 1:2, :])
        dx_ref[...] = dxo + dx
        dmod_ref[0, 0:1, :] += dsh
        dmod_ref[0, 1:2, :] += dsc
        dgpre_ref[...] += dg

        if ns:
            @pl.when(i == T // tm - 1)
            def _():
                for cp in _scatter_copies(*comm):
                    cp.wait()

    tok = pl.BlockSpec((tm, D), lambda i: (i, 0))
    vec = pl.BlockSpec((1, D), lambda i: (0, 0))
    hid = pl.BlockSpec((tm, DFF_PAD), lambda i: (i, 0))
    modspec = pl.BlockSpec((1, 3, D), lambda i: ((i * tm) // SEQ, 0, 0))
    outs = pl.pallas_call(
        body, grid=(T // tm,),
        in_specs=[tok, tok, tok, modspec, vec, vec, hid, hid, HBM, HBM, HBM] + [HBM] * ns + [tok] * (ne - ns),
        out_specs=[tok, tok, hid, hid, hid, modspec, vec, vec] + [HBM] * ns
        + [pl.BlockSpec((8, LANE), lambda i: (0, 0))] * (ne - ns),
        out_shape=[SDS((T, D), F32), SDS((T, D), BF16), SDS((T, DFF_PAD), BF16), SDS((T, DFF_PAD), BF16),
                   SDS((T, DFF_PAD), BF16), SDS((nb, 3, D), F32), SDS((1, D), F32), SDS((1, D), F32)]
        + [SDS((3,) + h.shape[1:], h.dtype) for h in (scatter or [])] + [SDS((8, LANE), F32)] * (ne - ns),
        scratch_shapes=_resident_scratch()
        + ([pltpu.SemaphoreType.DMA((ns, 3)), pltpu.SemaphoreType.DMA((ns, 3))] if ns else []),
        compiler_params=_cp("arbitrary"), name=name,
    )(dxo, x, y0, mod3, g_pre, g_post, gate, up, wg, wu, wd, *(scatter or []), *([] if target is None else [target]))
    return outs[:8], outs[8:]


def matmul_tn(a, b, tm, tn, tk, name, scatter=None):
    T, M = a.shape
    N = b.shape[1]
    grid = (M // tm, N // tn, T // tk)
    ns = 0 if scatter is None else len(scatter)

    def body(*refs):
        a_ref, b_ref = refs[:2]
        o_ref = refs[2 + ns]
        ids = [pl.program_id(ax) for ax in range(3)]
        if ns:
            comm = (refs[2:2 + ns], refs[3 + ns:3 + 2 * ns], *refs[3 + 2 * ns:])

            @pl.when((ids[0] == 0) & (ids[1] == 0) & (ids[2] == 0))
            def _():
                for cp in _scatter_copies(*comm):
                    cp.start()

        @pl.when(ids[2] == 0)
        def _():
            o_ref[...] = jnp.zeros_like(o_ref)

        o_ref[...] += _dot_tn(a_ref[...], b_ref[...])

        if ns:
            @pl.when((ids[0] == grid[0] - 1) & (ids[1] == grid[1] - 1) & (ids[2] == grid[2] - 1))
            def _():
                for cp in _scatter_copies(*comm):
                    cp.wait()

    outs = pl.pallas_call(
        body, grid=grid,
        in_specs=[pl.BlockSpec((tk, tm), lambda i, j, k: (k, i)), pl.BlockSpec((tk, tn), lambda i, j, k: (k, j))] + [HBM] * ns,
        out_specs=[pl.BlockSpec((tm, tn), lambda i, j, k: (i, j))] + [HBM] * ns,
        out_shape=[SDS((M, N), F32)] + [SDS((3,) + h.shape[1:], h.dtype) for h in (scatter or [])],
        scratch_shapes=[pltpu.SemaphoreType.DMA((ns, 3)), pltpu.SemaphoreType.DMA((ns, 3))] if ns else [],
        compiler_params=_cp("arbitrary", "arbitrary", "arbitrary"), name=name,
    )(a, b, *(scatter or []))
    return outs[0] if scatter is None else (outs[0], outs[1:])


def matmul_tn_cols(a, bs, tk, name):
    T, M = a.shape
    n = bs[0].shape[1]
    ng = len(bs)

    def body(*refs):
        a_ref, b_refs, o_ref = refs[0], refs[1:1 + ng], refs[1 + ng]

        @pl.when(pl.program_id(0) == 0)
        def _():
            o_ref[...] = jnp.zeros_like(o_ref)

        av = a_ref[...]
        for g, b_ref in enumerate(b_refs):
            o_ref[:, g * n:(g + 1) * n] += _dot_tn(av, b_ref[...])

    return pl.pallas_call(
        body, grid=(T // tk,),
        in_specs=[pl.BlockSpec((tk, M), lambda k: (k, 0))] + [pl.BlockSpec((tk, n), lambda k: (k, 0))] * ng,
        out_specs=pl.BlockSpec((M, ng * n), lambda k: (0, 0)), out_shape=SDS((M, ng * n), F32),
        compiler_params=_cp("arbitrary"), name=name,
    )(a, *bs)


def rope_tables(pos_col, name):
    T = pos_col.shape[0]
    tm = 1024

    def body(p_ref, c_ref, s1_ref, s2_ref):
        lane = lax.broadcasted_iota(jnp.int32, (1, LANE), 1)
        l64 = lane % HD
        inv_freq = jnp.exp((l64 % 8).astype(F32) * (-math.log(ROPE_THETA) / 8.0))
        ang = p_ref[...].astype(F32) * inv_freq
        cs = jnp.cos(ang)
        sn = jnp.sin(ang)
        c_ref[...] = jnp.where(l64 < 16, cs, 1.0)
        s1_ref[...] = jnp.where(l64 < 8, -sn, 0.0)
        s2_ref[...] = jnp.where((l64 >= 8) & (l64 < 16), sn, 0.0)

    tab = pl.BlockSpec((tm, LANE), lambda i: (i, 0))
    return pl.pallas_call(
        body, grid=(T // tm,), in_specs=[pl.BlockSpec((tm, 1), lambda i: (i, 0))], out_specs=[tab, tab, tab],
        out_shape=[SDS((T, LANE), F32)] * 3, compiler_params=_cp("arbitrary"), name=name,
    )(pos_col)


def mixer_proj(x, mod3, g_pre, w_main, w_f, rc, rs1, rs2, name):
    T = x.shape[0]

    def body(x_ref, mod_ref, g_ref, w_ref, wf_ref, c_ref, s1_ref, s2_ref, h_ref, pa_ref, pb_ref, f_ref):
        xf = x_ref[...]
        h = (xf * _rms(xf) * g_ref[...]) * (1.0 + mod_ref[0, 1:2, :]) + mod_ref[0, 0:1, :]
        hb = h.astype(BF16)
        h_ref[...] = hb
        f_ref[...] = _dot(hb, wf_ref[...])
        c, s1, s2 = c_ref[...], s1_ref[...], s2_ref[...]
        for grp in range(2):
            pr = _dot(hb, w_ref[:, grp * WG:(grp + 1) * WG])
            for k in range(WG // LANE):
                t = pr[:, k * LANE:(k + 1) * LANE]
                pa_ref[:, grp * WG + k * LANE:grp * WG + (k + 1) * LANE] = (
                    t * c + pltpu.roll(t, LANE - 8, 1) * s1 + pltpu.roll(t, 8, 1) * s2)
        pa_ref[:, 2 * WG:3 * WG] = _dot(hb, w_ref[:, 2 * WG:3 * WG])
        for grp in range(3):
            pb_ref[:, grp * WG:(grp + 1) * WG] = _dot(hb, w_ref[:, (3 + grp) * WG:(4 + grp) * WG]).astype(BF16)

    tok = pl.BlockSpec((TM, D), lambda i: (i, 0))
    vec = pl.BlockSpec((1, D), lambda i: (0, 0))
    tab = pl.BlockSpec((TM, LANE), lambda i: (i, 0))
    grp3 = pl.BlockSpec((TM, 3 * WG), lambda i: (i, 0))
    return pl.pallas_call(
        body, grid=(T // TM,),
        in_specs=[tok, pl.BlockSpec((1, 3, D), _mod_map), vec, pl.BlockSpec((D, IN_MAIN), lambda i: (0, 0)),
                  pl.BlockSpec((D, LANE), lambda i: (0, 0)), tab, tab, tab],
        out_specs=[tok, grp3, grp3, tab],
        out_shape=[SDS((T, D), BF16), SDS((T, 3 * WG), F32), SDS((T, 3 * WG), BF16), SDS((T, LANE), F32)],
        compiler_params=_cp("arbitrary"), name=name,
    )(x, mod3, g_pre, w_main, w_f, rc, rs1, rs2)


def _head_lanes():
    return lax.broadcasted_iota(jnp.int32, (1, LANE), 1) < HD


def _pair(m0, a, b):
    return jnp.where(m0, a, b)


def _band_rows(i, d, nbc):
    if nbc == 1:
        return i, i, 0
    r, mb = i // nbc, i % nbc
    return r + mb * (QB * d), r + jnp.maximum(mb - 1, 0) * (QB * d), jnp.where(mb > 0, QB, 0)


def _rows(start, size, d):
    return pl.ds(pl.multiple_of(start, QB), size) if d == 1 else pl.ds(start, size, stride=d)


def _band_valid(span, off):
    rq = lax.broadcasted_iota(jnp.int32, (QB, span), 0)
    rel = lax.broadcasted_iota(jnp.int32, (QB, span), 1) - off
    return (rel <= rq) & (rel >= rq - QB)


def band_fwd(pa, name):
    T = pa.shape[0]
    B = T // SEQ
    NP = WG // LANE

    def body(q_ref, k_ref, v_ref, out_ref, lse_ref, o_s, l_s):
        m0 = _head_lanes()
        for pidx, (d, nbc) in enumerate(PATTERNS):
            span = QB if nbc == 1 else 2 * QB

            def blk(it, carry, pidx=pidx, d=d, nbc=nbc, span=span):
                ld = []
                for u in range(BAND_UNROLL):
                    qs, ks, off = _band_rows(it * BAND_UNROLL + u, d, nbc)
                    q = q_ref[_rows(qs, QB, d), :] * ATTN_SCALE
                    ld.append((qs, q, k_ref[_rows(ks, span, d), :].astype(BF16), v_ref[_rows(ks, span, d), :].astype(BF16),
                               _band_valid(span, off)))
                ss = [[jnp.where(valid, _dot_nt(jnp.where(mh, q, 0.0).astype(BF16), k), NEG) for mh in (m0, jnp.logical_not(m0))]
                      for _, q, k, _, valid in ld]
                ps = []
                for pair in ss:
                    row = []
                    for s in pair:
                        m = jnp.max(s, axis=-1, keepdims=True)
                        p = jnp.exp(s - m)
                        row.append((p.astype(BF16), jnp.sum(p, axis=-1, keepdims=True), m))
                    ps.append(row)
                pv = [[_dot(p, ld[u][3]) for p, _, _ in ps[u]] for u in range(BAND_UNROLL)]
                for u in range(BAND_UNROLL):
                    rows = _rows(ld[u][0], QB, d)
                    (_, l0, mx0), (_, l1, mx1) = ps[u]
                    o_s[pidx, rows, :] = _pair(m0, pv[u][0] / l0, pv[u][1] / l1)
                    l_s[pidx, rows, :] = _pair(m0, mx0 + jnp.log(l0), mx1 + jnp.log(l1))
                return carry

            lax.fori_loop(0, SEQ // QB // BAND_UNROLL, blk, 0)
        for c in range(SEQ // FB):
            sl = slice(c * FB, (c + 1) * FB)
            a, b, e = l_s[0, sl, :], l_s[1, sl, :], l_s[2, sl, :]
            m = jnp.maximum(jnp.maximum(a, b), e)
            L = m + jnp.log(jnp.exp(a - m) + jnp.exp(b - m) + jnp.exp(e - m))
            out_ref[sl, :] = jnp.exp(a - L) * o_s[0, sl, :] + jnp.exp(b - L) * o_s[1, sl, :] + jnp.exp(e - L) * o_s[2, sl, :]
            lse_ref[sl, :] = L

    blk_of = lambda g: pl.BlockSpec((SEQ, LANE), lambda b, hp, g=g: (b, g * NP + hp))
    return pl.pallas_call(
        body, grid=(B, NP), in_specs=[blk_of(0), blk_of(1), blk_of(2)], out_specs=[blk_of(0), blk_of(0)],
        out_shape=[SDS((T, WG), F32), SDS((T, WG), F32)],
        scratch_shapes=[pltpu.VMEM((3, SEQ, LANE), F32), pltpu.VMEM((3, SEQ, LANE), F32)],
        compiler_params=_cp("arbitrary", "arbitrary"), name=name,
    )(pa, pa, pa)


def _pair_rowsum(m0, prod):
    s0 = jnp.sum(jnp.where(m0, prod, 0.0), axis=-1, keepdims=True)
    return _pair(m0, s0, jnp.sum(prod, axis=-1, keepdims=True) - s0)


def band_bwd(pa, do, out, lse, rc, rs1, rs2, name):
    T = pa.shape[0]
    B = T // SEQ
    NP = WG // LANE

    def body(q_ref, k_ref, v_ref, do_ref, out_ref, l_ref, c_ref, s1_ref, s2_ref, dqo_ref, dko_ref, dvo_ref, d_s, dq_ref, dk_ref,
             dv_ref):
        m0 = _head_lanes()
        dq_ref[...] = jnp.zeros_like(dq_ref)
        dk_ref[...] = jnp.zeros_like(dk_ref)
        dv_ref[...] = jnp.zeros_like(dv_ref)
        for c in range(SEQ // FB):
            sl = slice(c * FB, (c + 1) * FB)
            d_s[sl, :] = _pair_rowsum(m0, do_ref[sl, :] * out_ref[sl, :])
        for d, nbc in PATTERNS:
            span = QB if nbc == 1 else 2 * QB

            def blk(it, carry, d=d, nbc=nbc, span=span):
                masks = (m0, jnp.logical_not(m0))
                ld = []
                for u in range(BAND_UNROLL_BWD):
                    qs, ks, off = _band_rows(it * BAND_UNROLL_BWD + u, d, nbc)
                    qrow, krow = _rows(qs, QB, d), _rows(ks, span, d)
                    ld.append(dict(qrow=qrow, krow=krow, q=q_ref[qrow, :] * ATTN_SCALE, k=k_ref[krow, :].astype(BF16),
                                   v=v_ref[krow, :].astype(BF16), do=do_ref[qrow, :], l=l_ref[qrow, :], dv=d_s[qrow, :],
                                   valid=_band_valid(span, off)))
                for t in ld:
                    t["qm"] = [jnp.where(mh, t["q"], 0.0).astype(BF16) for mh in masks]
                    t["dom"] = [jnp.where(mh, t["do"], 0.0).astype(BF16) for mh in masks]
                sd = [[(jnp.where(t["valid"], _dot_nt(t["qm"][h], t["k"]), NEG), _dot_nt(t["dom"][h], t["v"])) for h in range(2)]
                      for t in ld]
                pd = []
                for t, pair in zip(ld, sd):
                    row = []
                    for h, (s, dp) in enumerate(pair):
                        col = slice(h * HD, h * HD + 1)
                        p = jnp.exp(s - t["l"][:, col])
                        row.append((p.astype(BF16), (p * (dp - t["dv"][:, col])).astype(BF16)))
                    pd.append(row)
                gr = [(_dot(row[0][1], t["k"]), _dot(row[1][1], t["k"]),
                       _dot_tn(jnp.concatenate([row[0][1], row[1][1]], axis=0), jnp.concatenate(t["qm"], axis=0)),
                       _dot_tn(jnp.concatenate([row[0][0], row[1][0]], axis=0), jnp.concatenate(t["dom"], axis=0)))
                      for t, row in zip(ld, pd)]
                for t, (dq0, dq1, dk, dv) in zip(ld, gr):
                    dq_ref[t["qrow"], :] += _pair(m0, dq0, dq1) * ATTN_SCALE
                    dk_ref[t["krow"], :] += dk
                    dv_ref[t["krow"], :] += dv
                return carry

            lax.fori_loop(0, SEQ // QB // BAND_UNROLL_BWD, blk, 0)
        for c in range(SEQ // FB):
            sl = slice(c * FB, (c + 1) * FB)
            cc, s1, s2 = c_ref[sl, :], s1_ref[sl, :], s2_ref[sl, :]
            for acc, o_ref in ((dq_ref, dqo_ref), (dk_ref, dko_ref)):
                d = acc[sl, :]
                o_ref[sl, :] = (d * cc + pltpu.roll(d * s1, 8, 1) + pltpu.roll(d * s2, LANE - 8, 1)).astype(BF16)
            dvo_ref[sl, :] = dv_ref[sl, :].astype(BF16)

    blk_of = lambda g: pl.BlockSpec((SEQ, LANE), lambda b, hp, g=g: (b, g * NP + hp))
    tab = pl.BlockSpec((SEQ, LANE), lambda b, hp: (b, 0))
    return pl.pallas_call(
        body, grid=(B, NP), in_specs=[blk_of(0), blk_of(1), blk_of(2), blk_of(0), blk_of(0), blk_of(0), tab, tab, tab],
        out_specs=[blk_of(0)] * 3, out_shape=[SDS((T, WG), BF16)] * 3,
        scratch_shapes=[pltpu.VMEM((SEQ, LANE), F32)] * 4,
        compiler_params=_cp("arbitrary", "arbitrary"), name=name,
    )(pa, pa, pa, do, out, lse, rc, rs1, rs2)


def _tile_causal(nq, nk, q0, k0):
    r = lax.broadcasted_iota(jnp.int32, (nq, nk), 0)
    c = lax.broadcasted_iota(jnp.int32, (nq, nk), 1)
    return r + (q0 - k0) >= c


def _row_to_col(row):
    n = row.shape[1]
    return jnp.transpose(jnp.broadcast_to(row, (LANE, n)))[:, 0:1]


def _col_to_row(col):
    n = col.shape[0]
    return jnp.transpose(jnp.broadcast_to(col, (n, LANE)))[0:1, :]


def fox_fwd(pb, fblk, frow, name):
    T = pb.shape[0]
    B = T // SEQ
    NG = WG // (LANE * FOX_PAIRS)
    NHS = 2 * FOX_PAIRS
    W = LANE * FOX_PAIRS
    n = SEQ // FB

    def body(q_ref, k_ref, v_ref, fc_ref, fr_ref, o_ref, lse_ref):
        i = pl.program_id(2)
        m0 = _head_lanes()
        masks = (m0, jnp.logical_not(m0))
        heads = [(hh, slice((hh // 2) * LANE, (hh // 2 + 1) * LANE), masks[hh % 2]) for hh in range(NHS)]
        qh, fq = [], []
        for hh, lanes, mh in heads:
            q = q_ref[:, lanes] * ATTN_SCALE
            qh.append(jnp.where(mh, q, jnp.zeros_like(q)))
            fq.append(_row_to_col(fc_ref[0, hh, 0]))

        def step(t, carry, masked):
            rows = pl.ds(pl.multiple_of(t * FT, FT), FT)
            ss = [_dot_nt(qh[hh], k_ref[rows, lanes]) + fq[hh] - fr_ref[0, hh, t] for hh, lanes, _ in heads]
            if masked:
                ok = _tile_causal(FB, FT, i * FB, t * FT)
                ss = [jnp.where(ok, s, NEG) for s in ss]
            st = []
            for hh, _, _ in heads:
                m2 = jnp.maximum(carry[hh][0], jnp.max(ss[hh], axis=-1, keepdims=True))
                st.append((m2, jnp.exp(carry[hh][0] - m2), jnp.exp(ss[hh] - m2).astype(BF16)))
            pv = []
            for hh, lanes, mh in heads:
                vt = v_ref[rows, lanes]
                pv.append(_dot(st[hh][2], jnp.where(mh, vt, jnp.ones_like(vt))))
            return tuple((st[hh][0], st[hh][1] * carry[hh][1] + pv[hh]) for hh in range(NHS))

        one = (jnp.full((FB, 1), NEG, F32), jnp.zeros((FB, LANE), F32))
        last = (i * FB) // FT
        carry = lax.fori_loop(0, last, lambda t, cr: step(t, cr, False), (one,) * NHS)
        carry = step(last, carry, True)
        for pr in range(FOX_PAIRS):
            (ma, acca), (mb, accb) = carry[2 * pr], carry[2 * pr + 1]
            la, lb = acca[:, HD:HD + 1], accb[:, 0:1]
            o_ref[:, pr * LANE:(pr + 1) * LANE] = _pair(m0, acca / la, accb / lb)
            lse_ref[0, 2 * pr, 0] = _col_to_row(ma + jnp.log(la))
            lse_ref[0, 2 * pr + 1, 0] = _col_to_row(mb + jnp.log(lb))

    qblk = pl.BlockSpec((FB, W), lambda b, g, i: (b * n + i, g))
    full = lambda grp: pl.BlockSpec((SEQ, W), lambda b, g, i, grp=grp: (b, grp * NG + g))
    rowb = pl.BlockSpec((1, NHS, 1, 1, FB), lambda b, g, i: (b, g, i, 0, 0))
    return pl.pallas_call(
        body, grid=(B, NG, n),
        in_specs=[qblk, full(1), full(2), rowb, pl.BlockSpec((1, NHS, SEQ // FT, 1, FT), lambda b, g, i: (b, g, 0, 0, 0))],
        out_specs=[qblk, rowb], out_shape=[SDS((T, WG), F32), SDS((B, NH, n, 1, FB), F32)],
        compiler_params=_cp("arbitrary", "arbitrary", "arbitrary"), name=name,
    )(pb, pb, pb, fblk, frow)


def fox_bwd(pb, do, lrow, drow, fblk, frow, name):
    T = pb.shape[0]
    B = T // SEQ
    PAIRS = FOX_PAIRS_BWD
    NG = WG // (LANE * PAIRS)
    NHS = 2 * PAIRS
    W = LANE * PAIRS
    n = SEQ // FB

    def body(q_ref, k_ref, v_ref, do_ref, l_ref, d_ref, fc_ref, fr_ref, dqo_ref, dk_ref, dv_ref, dfq_ref, dfk_ref, dq_ref):
        j = pl.program_id(2)
        m0 = _head_lanes()
        masks = (m0, jnp.logical_not(m0))
        heads = [(hh, slice((hh // 2) * LANE, (hh // 2 + 1) * LANE), masks[hh % 2]) for hh in range(NHS)]

        @pl.when(j == 0)
        def _():
            dq_ref[...] = jnp.zeros_like(dq_ref)
            dfq_ref[...] = jnp.zeros_like(dfq_ref)

        kj = [k_ref[:, lanes] for _, lanes, _ in heads]
        vj = [v_ref[:, lanes] for _, lanes, _ in heads]
        fk = [_row_to_col(fc_ref[0, hh, 0]) for hh in range(NHS)]

        def step(t, carry, masked):
            rows = pl.ds(pl.multiple_of(t * FT, FT), FT)
            qm, dom = [], []
            for _, lanes, mh in heads:
                qt = q_ref[rows, lanes] * ATTN_SCALE
                qm.append(jnp.where(mh, qt, jnp.zeros_like(qt)))
                dom.append(jnp.where(mh, do_ref[rows, lanes], 0.0).astype(BF16))
            ss = [_dot_nt(kj[hh], qm[hh]) + fr_ref[0, hh, t] - fk[hh] for hh in range(NHS)]
            dps = [_dot_nt(vj[hh], dom[hh]) for hh in range(NHS)]
            if masked:
                key = lax.broadcasted_iota(jnp.int32, (FB, FT), 0)
                qry = lax.broadcasted_iota(jnp.int32, (FB, FT), 1)
                ok = qry + (t * FT - j * FB) >= key
                ss = [jnp.where(ok, s, NEG) for s in ss]
            pds = []
            for hh in range(NHS):
                p = jnp.exp(ss[hh] - l_ref[0, hh, t])
                ds = p * (dps[hh] - d_ref[0, hh, t])
                dfq_ref[0, hh, t] += jnp.sum(ds, axis=0, keepdims=True)
                pds.append((p.astype(BF16), ds.astype(BF16), jnp.sum(ds, axis=-1, keepdims=True)))
            dks = [_dot(pds[hh][1], qm[hh]) for hh in range(NHS)]
            dvs = [_dot(pds[hh][0], dom[hh]) for hh in range(NHS)]
            dqs = [_dot_tn(pds[hh][1], kj[hh]) for hh in range(NHS)]
            for pr in range(PAIRS):
                dq_ref[rows, pr * LANE:(pr + 1) * LANE] += _pair(m0, dqs[2 * pr], dqs[2 * pr + 1]) * ATTN_SCALE
            return tuple((carry[hh][0] + dks[hh], carry[hh][1] + dvs[hh], carry[hh][2] - pds[hh][2]) for hh in range(NHS))

        one = (jnp.zeros((FB, LANE), F32), jnp.zeros((FB, LANE), F32), jnp.zeros((FB, 1), F32))
        first = (j * FB) // FT
        carry = step(first, (one,) * NHS, True)
        carry = lax.fori_loop(first + 1, SEQ // FT, lambda t, cr: step(t, cr, False), carry)
        for pr in range(PAIRS):
            (dka, dva, dfka), (dkb, dvb, dfkb) = carry[2 * pr], carry[2 * pr + 1]
            dk_ref[:, pr * LANE:(pr + 1) * LANE] = _pair(m0, dka, dkb).astype(BF16)
            dv_ref[:, pr * LANE:(pr + 1) * LANE] = _pair(m0, dva, dvb).astype(BF16)
            dfk_ref[0, 2 * pr, 0] = _col_to_row(dfka)
            dfk_ref[0, 2 * pr + 1, 0] = _col_to_row(dfkb)

        @pl.when(j == n - 1)
        def _():
            dqo_ref[...] = dq_ref[...].astype(BF16)

    kblk = lambda grp: pl.BlockSpec((FB, W), lambda b, g, j, grp=grp: (b * n + j, grp * NG + g))
    full = pl.BlockSpec((SEQ, W), lambda b, g, j: (b, g))
    rowf = pl.BlockSpec((1, NHS, SEQ // FT, 1, FT), lambda b, g, j: (b, g, 0, 0, 0))
    rowb = pl.BlockSpec((1, NHS, 1, 1, FB), lambda b, g, j: (b, g, j, 0, 0))
    return pl.pallas_call(
        body, grid=(B, NG, n), in_specs=[full, kblk(1), kblk(2), full, rowf, rowf, rowb, rowf],
        out_specs=[full, kblk(0), kblk(0), rowf, rowb],
        out_shape=[SDS((T, WG), BF16), SDS((T, WG), BF16), SDS((T, WG), BF16), SDS((B, NH, SEQ // FT, 1, FT), F32),
                   SDS((B, NH, n, 1, FB), F32)],
        scratch_shapes=[pltpu.VMEM((SEQ, W), F32)],
        compiler_params=_cp("arbitrary", "arbitrary", "arbitrary"), name=name,
    )(pb, pb, pb, do, lrow, drow, fblk, frow)


def _tri(lower):
    r = lax.broadcasted_iota(jnp.int32, (LANE, LANE), 0)
    c = lax.broadcasted_iota(jnp.int32, (LANE, LANE), 1)
    return ((r >= c) if lower else (r <= c)).astype(F32)


def _tri_dot(t, xblk):
    return jnp.dot(t, xblk, precision=lax.Precision.HIGHEST, preferred_element_type=F32)


def forget_cumsum(flog, bias, name):
    B, S, _ = flog.shape

    def body(f_ref, b_ref, o_ref):
        t = _tri(True)
        carry = jnp.zeros((1, LANE), F32)
        for blk in range(S // LANE):
            z = f_ref[0, blk * LANE:(blk + 1) * LANE, :] + b_ref[...]
            lf = jnp.minimum(z, 0.0) - jnp.log(1.0 + jnp.exp(-jnp.abs(z)))
            cs = _tri_dot(t, lf) + carry
            o_ref[0, blk * LANE:(blk + 1) * LANE, :] = cs
            carry = cs[LANE - 1:LANE, :]

    spec = pl.BlockSpec((1, S, LANE), lambda b: (b, 0, 0))
    return pl.pallas_call(
        body, grid=(B,), in_specs=[spec, pl.BlockSpec((1, LANE), lambda b: (0, 0))], out_specs=spec,
        out_shape=SDS((B, S, LANE), F32), compiler_params=_cp("arbitrary"), name=name,
    )(flog, bias)


def forget_cumsum_bwd(dF, flog, bias, name):
    B, S, _ = flog.shape

    def body(d_ref, f_ref, b_ref, o_ref, db_ref):
        @pl.when(pl.program_id(0) == 0)
        def _():
            db_ref[...] = jnp.zeros_like(db_ref)

        t = _tri(False)
        carry = jnp.zeros((1, LANE), F32)
        tot = jnp.zeros((1, LANE), F32)
        for blk in reversed(range(S // LANE)):
            sl = slice(blk * LANE, (blk + 1) * LANE)
            rc = _tri_dot(t, d_ref[0, sl, :]) + carry
            carry = rc[0:1, :]
            z = f_ref[0, sl, :] + b_ref[...]
            dz = rc * jax.nn.sigmoid(-z)
            o_ref[0, sl, :] = dz
            tot = tot + jnp.sum(dz, axis=0, keepdims=True)
        db_ref[...] += tot

    spec = pl.BlockSpec((1, S, LANE), lambda b: (b, 0, 0))
    vec = pl.BlockSpec((1, LANE), lambda b: (0, 0))
    return pl.pallas_call(
        body, grid=(B,), in_specs=[spec, spec, vec], out_specs=[spec, vec],
        out_shape=[SDS((B, S, LANE), F32), SDS((1, LANE), F32)], compiler_params=_cp("arbitrary"), name=name,
    )(dF, flog, bias)


def mixer_out_fwd(oa, ob, goa, gob, w_out, g_post, x, mod3, name):
    T = x.shape[0]

    def body(oa_ref, ob_ref, goa_ref, gob_ref, w_ref, gp_ref, x_ref, mod_ref, xo_ref, mg_ref, y0_ref):
        a = oa_ref[...]
        b = ob_ref[...]
        mg = jnp.concatenate([a * _rms(a) * goa_ref[...], b * _rms(b) * gob_ref[...]], axis=-1).astype(BF16)
        mg_ref[...] = mg
        y0 = _dot(mg, w_ref[...])
        y0_ref[...] = y0
        xo_ref[...] = x_ref[...] + mod_ref[0, 2:3, :] * (y0 * _rms(y0) * gp_ref[...])

    tok = pl.BlockSpec((TM, D), lambda i: (i, 0))
    half = pl.BlockSpec((TM, WG), lambda i: (i, 0))
    hv = pl.BlockSpec((1, WG), lambda i: (0, 0))
    return pl.pallas_call(
        body, grid=(T // TM,),
        in_specs=[half, half, hv, hv, pl.BlockSpec((D, D), lambda i: (0, 0)), pl.BlockSpec((1, D), lambda i: (0, 0)), tok,
                  pl.BlockSpec((1, 3, D), _mod_map)],
        out_specs=[tok, tok, tok], out_shape=[SDS((T, D), F32), SDS((T, D), BF16), SDS((T, D), F32)],
        compiler_params=_cp("arbitrary"), name=name,
    )(oa, ob, goa, gob, w_out, g_post, x, mod3)


def mixer_out_bwd(dxo, y0, mod3, g_post, w_out, oa, ob, goa, gob, name):
    T = dxo.shape[0]
    nb = T // SEQ
    tiles_per_seq = SEQ // TM

    def body(dxo_ref, y0_ref, mod_ref, gp_ref, w_ref, oa_ref, ob_ref, goa_ref, gob_ref,
             dy0_ref, doa_ref, dob_ref, dmg_ref, dgp_ref, dgoa_ref, dgob_ref, dvb_ref):
        i = pl.program_id(0)

        @pl.when(i == 0)
        def _():
            dgp_ref[...] = jnp.zeros_like(dgp_ref)
            dgoa_ref[...] = jnp.zeros_like(dgoa_ref)
            dgob_ref[...] = jnp.zeros_like(dgob_ref)

        @pl.when(i % tiles_per_seq == 0)
        def _():
            dmg_ref[...] = jnp.zeros_like(dmg_ref)

        dy0, dmg, dg = _post_bwd(dxo_ref[...], y0_ref[...], gp_ref[...], mod_ref[0, 2:3, :], 1.0)
        dmg_ref[0] += dmg
        dgp_ref[...] += dg
        db = dy0.astype(BF16)
        dy0_ref[...] = db
        dm = _dot_nt(db, w_ref[...])
        for o_ref, g_ref, do_ref, dg_ref, sl in ((oa_ref, goa_ref, doa_ref, dgoa_ref, slice(0, WG)),
                                                  (ob_ref, gob_ref, dob_ref, dgob_ref, slice(WG, 2 * WG))):
            o = o_ref[...]
            r = _rms(o)
            oh = o * r
            d = dm[:, sl]
            dg_ref[...] += jnp.sum(d * oh, axis=0, keepdims=True)
            dh = d * g_ref[...]
            do = r * (dh - oh * jnp.mean(dh * oh, axis=-1, keepdims=True))
            do_ref[...] = do
        ind = (lax.broadcasted_iota(jnp.int32, (WG, LANE), 0) // HD == lax.broadcasted_iota(jnp.int32, (WG, LANE), 1)).astype(BF16)
        prod = do * o
        hi = prod.astype(BF16)
        dvb_ref[...] = _dot(hi, ind) + _dot((prod - hi.astype(F32)).astype(BF16), ind)

    tok = pl.BlockSpec((TM, D), lambda i: (i, 0))
    half = pl.BlockSpec((TM, WG), lambda i: (i, 0))
    hv = pl.BlockSpec((1, WG), lambda i: (0, 0))
    vec = pl.BlockSpec((1, D), lambda i: (0, 0))
    return pl.pallas_call(
        body, grid=(T // TM,),
        in_specs=[tok, tok, pl.BlockSpec((1, 3, D), _mod_map), vec, pl.BlockSpec((D, D), lambda i: (0, 0)), half, half, hv, hv],
        out_specs=[tok, half, half, pl.BlockSpec((1, 1, D), _mod_map), vec, hv, hv, pl.BlockSpec((TM, LANE), lambda i: (i, 0))],
        out_shape=[SDS((T, D), BF16), SDS((T, WG), F32), SDS((T, WG), F32), SDS((nb, 1, D), F32), SDS((1, D), F32),
                   SDS((1, WG), F32), SDS((1, WG), F32), SDS((T, LANE), F32)],
        compiler_params=_cp("arbitrary"), name=name,
    )(dxo, y0, mod3, g_post, w_out, oa, ob, goa, gob)


def mixer_proj_bwd(dps, dflog, dxo, x, mod3, g_pre, w_main, w_f, name):
    T = x.shape[0]
    nb = T // SEQ
    tiles_per_seq = SEQ // TM
    ngrp = len(dps)

    def body(*refs):
        dp_refs = refs[:ngrp]
        df_ref, dxo_ref, x_ref, mod_ref, g_ref, w_ref, wf_ref, dx_ref, dmod_ref, dg_ref = refs[ngrp:]
        i = pl.program_id(0)

        @pl.when(i == 0)
        def _():
            dg_ref[...] = jnp.zeros_like(dg_ref)

        @pl.when(i % tiles_per_seq == 0)
        def _():
            dmod_ref[...] = jnp.zeros_like(dmod_ref)

        dh = _dot_nt(df_ref[...].astype(BF16), wf_ref[...])
        for g, dp_ref in enumerate(dp_refs):
            dh = dh + _dot_nt(dp_ref[...], w_ref[:, g * WG:(g + 1) * WG])
        dx, dsh, dsc, dg = _norm_mod_bwd(dh, x_ref[...], g_ref[...], mod_ref[0, 1:2, :])
        dx_ref[...] = dxo_ref[...] + dx
        dmod_ref[0, 0:1, :] += dsh
        dmod_ref[0, 1:2, :] += dsc
        dg_ref[...] += dg

    tok = pl.BlockSpec((TM, D), lambda i: (i, 0))
    vec = pl.BlockSpec((1, D), lambda i: (0, 0))
    return pl.pallas_call(
        body, grid=(T // TM,),
        in_specs=[pl.BlockSpec((TM, WG), lambda i: (i, 0))] * ngrp
        + [pl.BlockSpec((TM, LANE), lambda i: (i, 0)), tok, tok, pl.BlockSpec((1, 3, D), _mod_map), vec,
           pl.BlockSpec((D, IN_MAIN), lambda i: (0, 0)), pl.BlockSpec((D, LANE), lambda i: (0, 0))],
        out_specs=[tok, pl.BlockSpec((1, 2, D), _mod_map), vec],
        out_shape=[SDS((T, D), F32), SDS((nb, 2, D), F32), SDS((1, D), F32)],
        compiler_params=_cp("arbitrary"), name=name,
    )(*dps, dflog, dxo, x, mod3, g_pre, w_main, w_f)


def ada_fwd(c_all, w, b, name):
    n = w.shape[1]
    tn = n // 2

    def body(c_ref, w_ref, b_ref, o_ref):
        cv = c_ref[...]
        o_ref[...] = _dot((cv * jax.nn.sigmoid(cv)).astype(BF16), w_ref[...].astype(BF16)) + b_ref[...]

    R = c_all.shape[0]
    return pl.pallas_call(
        body, grid=(2,),
        in_specs=[pl.BlockSpec((R, D), lambda j: (0, 0)), pl.BlockSpec((D, tn), lambda j: (0, j)), pl.BlockSpec((1, tn), lambda j: (0, j))],
        out_specs=pl.BlockSpec((R, tn), lambda j: (0, j)), out_shape=SDS((R, n), F32),
        compiler_params=_cp("arbitrary"), name=name,
    )(c_all, w, b)


def ada_bwd(c_all, dmod, name):
    R, n = dmod.shape
    tn = n // 2

    def body(c_ref, d_ref, o_ref):
        cv = c_ref[...]
        o_ref[...] = _dot_tn((cv * jax.nn.sigmoid(cv)).astype(BF16), d_ref[...].astype(BF16))

    return pl.pallas_call(
        body, grid=(2,), in_specs=[pl.BlockSpec((R, D), lambda j: (0, 0)), pl.BlockSpec((R, tn), lambda j: (0, j))],
        out_specs=pl.BlockSpec((D, tn), lambda j: (0, j)), out_shape=SDS((D, n), F32),
        compiler_params=_cp("arbitrary"), name=name,
    )(c_all, dmod)


def _adam_math(w, g, m, v):
    m2 = ADAM_B1 * m + (1.0 - ADAM_B1) * g
    v2 = ADAM_B2 * v + (1.0 - ADAM_B2) * (g * g)
    m_hat = m2 / (1.0 - ADAM_B1 ** ADAM_STEP)
    v_hat = v2 / (1.0 - ADAM_B2 ** ADAM_STEP)
    delta = -ADAM_LR * (m_hat / (jnp.sqrt(v_hat) + ADAM_EPS) + ADAM_WD * w)
    return delta, m2, v2


def adam_update(w, g, m, v, tr, name):
    _, R, C = w.shape

    def body(w_ref, g_ref, m_ref, v_ref, d_ref, mo_ref, vo_ref):
        d_ref[0], mo_ref[0], vo_ref[0] = _adam_math(w_ref[0], g_ref[...], m_ref[0], v_ref[0])

    spec = pl.BlockSpec((1, tr, C), lambda i: (0, i, 0))
    gspec = pl.BlockSpec((tr, C), lambda i: (i, 0))
    return pl.pallas_call(
        body, grid=(R // tr,), in_specs=[spec, gspec, spec, spec], out_specs=[spec] * 3, out_shape=[SDS((1, R, C), F32)] * 3,
        compiler_params=_cp("arbitrary"), name=name,
    )(w, g, m, v)


def adam_update_halves(w, mine, other, m, v, cidx, tr, name):
    _, R, C = w.shape
    nh = R // 2 // tr

    def body(c_ref, w_ref, a_ref, b_ref, m_ref, v_ref, g_ref, d_ref, mo_ref, vo_ref):
        first_half = pl.program_id(0) < nh
        g = jnp.where(first_half == (c_ref[0] == 0), a_ref[...], b_ref[...])
        g_ref[0] = g
        d_ref[0], mo_ref[0], vo_ref[0] = _adam_math(w_ref[0], g, m_ref[0], v_ref[0])

    spec = pl.BlockSpec((1, tr, C), lambda i, c_ref: (0, i, 0))
    hspec = pl.BlockSpec((tr, C), lambda i, c_ref: (i % nh, 0))
    return pl.pallas_call(
        body,
        grid_spec=pltpu.PrefetchScalarGridSpec(num_scalar_prefetch=1, grid=(R // tr,), in_specs=[spec, hspec, hspec, spec, spec],
                                               out_specs=[spec] * 4),
        out_shape=[SDS((1, R, C), F32)] * 4, compiler_params=_cp("arbitrary"), name=name,
    )(cidx, w, mine, other, m, v)


def vec_adam(parts, w, m, v, name):
    P, C = parts.shape

    def body(p_ref, w_ref, m_ref, v_ref, g_ref, d_ref, mo_ref, vo_ref):
        g = jnp.sum(p_ref[...], axis=0, keepdims=True)
        g_ref[...] = g
        d_ref[...], mo_ref[...], vo_ref[...] = _adam_math(w_ref[...], g, m_ref[...], v_ref[...])

    return pl.pallas_call(body, out_shape=[SDS((1, C), F32)] * 4, compiler_params=_cp(), name=name)(parts, w, m, v)


HBM = pl.BlockSpec(memory_space=pltpu.HBM)
VMEM = pl.BlockSpec(memory_space=pltpu.VMEM)


def _place():
    x, y, c = lax.axis_index("x"), lax.axis_index("y"), lax.axis_index("c")
    return x, y, c, [(1 - x, y), (x, 1 - y), (1 - x, 1 - y)]


def all_gather8(xs, name):
    R, C = xs.shape

    def body(x_ref, out_ref, send_sems, recv_sems, local_sem):
        x, y, c, chips = _place()
        me, sibling = (x, y, c), (x, y, 1 - c)

        def slot(px, py, pc):
            return out_ref.at[4 * px + 2 * py + pc]

        def copy(k, block, to, src=None):
            return pltpu.make_async_remote_copy(
                src_ref=slot(*block) if src is None else src, dst_ref=slot(*block),
                send_sem=send_sems.at[k], recv_sem=recv_sems.at[k], device_id=to, device_id_type=MESH)

        mine = pltpu.make_async_copy(x_ref, slot(*me), local_sem)
        mine.start()
        first = [copy(0, me, sibling, src=x_ref)]
        first += [copy(1 + j, me, (*chip, c), src=x_ref) for j, chip in enumerate(chips)]
        for cp in first:
            cp.start()
        passed = [copy(4 + j, (*chip, c), sibling) for j, chip in enumerate(chips)]
        for j, chip in enumerate(chips):
            copy(1 + j, (*chip, c), me).wait_recv()
            passed[j].start()
        copy(0, sibling, me).wait_recv()
        for j, chip in enumerate(chips):
            copy(4 + j, (*chip, 1 - c), me).wait_recv()
        for cp in first + passed:
            cp.wait_send()
        mine.wait()

    return pl.pallas_call(
        body, out_shape=SDS((N_DEV, R, C), xs.dtype), in_specs=[VMEM], out_specs=VMEM,
        scratch_shapes=[pltpu.SemaphoreType.DMA((7,)), pltpu.SemaphoreType.DMA((7,)), pltpu.SemaphoreType.DMA],
        compiler_params=pltpu.CompilerParams(vmem_limit_bytes=VMEM_LIMIT), name=name,
    )(xs)


class ShardGather:
    def __init__(self, shapes, splits):
        self.shapes, self.splits, self.n = shapes, splits, len(shapes)

    def scratch(self):
        n = self.n
        return [pltpu.SemaphoreType.DMA((n, 6)), pltpu.SemaphoreType.DMA((n, 6)), pltpu.SemaphoreType.DMA((n,))]

    def out_shapes(self, dtype):
        return [SDS((N_SHARD,) + tuple(s), dtype) for s in self.shapes]

    def _half(self, ref, k, cc):
        lo, hi = (0, self.splits[k]) if cc == 0 else (self.splits[k], self.shapes[k][0])
        return ref.at[pl.ds(lo, hi - lo)]

    def _phase(self, w_refs, o_refs, sems, finish):
        send_sems, recv_sems, local_sems = sems
        x, y, c, chips = _place()
        sibling = (x, y, 1 - c)
        me_s = 2 * x + y

        def rcopy(src, dst, k, s, to):
            return pltpu.make_async_remote_copy(src_ref=src, dst_ref=dst, send_sem=send_sems.at[k, s],
                                                recv_sem=recv_sems.at[k, s], device_id=to, device_id_type=MESH)

        for cc in (0, 1):
            @pl.when(c == cc)
            def _():
                local = [pltpu.make_async_copy(w_refs[k], o_refs[k].at[me_s], local_sems.at[k]) for k in range(self.n)]
                first = [rcopy(self._half(w_refs[k], k, cc), self._half(o_refs[k].at[me_s], k, cc), k, j, (*chip, c))
                         for k in range(self.n) for j, chip in enumerate(chips)]
                if not finish:
                    for cp in local + first:
                        cp.start()
                    return
                passed = []
                for k in range(self.n):
                    for j, chip in enumerate(chips):
                        land = self._half(o_refs[k].at[2 * chip[0] + chip[1]], k, cc)
                        rcopy(land, land, k, j, (*chip, c)).wait_recv()
                        f = rcopy(land, land, k, 3 + j, sibling)
                        f.start()
                        passed.append(f)
                for k in range(self.n):
                    for j, chip in enumerate(chips):
                        other = self._half(o_refs[k].at[2 * chip[0] + chip[1]], k, 1 - cc)
                        rcopy(other, other, k, 3 + j, sibling).wait_recv()
                for s in first + passed:
                    s.wait_send()
                for cp in local:
                    cp.wait()

    def start(self, w_refs, o_refs, sems):
        self._phase(w_refs, o_refs, sems, False)

    def finish(self, w_refs, o_refs, sems):
        self._phase(w_refs, o_refs, sems, True)


def all_gather_shards(ws, splits, name):
    n = len(ws)
    plan = ShardGather([w.shape for w in ws], splits)

    def body(*refs):
        plan.start(refs[:n], refs[n:2 * n], refs[2 * n:])
        plan.finish(refs[:n], refs[n:2 * n], refs[2 * n:])

    return pl.pallas_call(
        body, out_shape=plan.out_shapes(ws[0].dtype), in_specs=[HBM] * n, out_specs=[HBM] * n,
        scratch_shapes=plan.scratch(), name=name,
    )(*ws)


def sibling_send_half(gs, name):
    n = len(gs)

    def body(*refs):
        g_refs, o_refs = refs[:n], refs[n:2 * n]
        send_sems, recv_sems = refs[2 * n:]
        x, y, c, _ = _place()
        cps = []
        for k in range(n):
            hr = gs[k].shape[1] // 2
            src = g_refs[k].at[:, pl.ds(pl.multiple_of((1 - c) * hr, 8), hr)]
            cp = pltpu.make_async_remote_copy(src_ref=src, dst_ref=o_refs[k], send_sem=send_sems.at[k], recv_sem=recv_sems.at[k],
                                              device_id=(x, y, 1 - c), device_id_type=MESH)
            cp.start()
            cps.append(cp)
        for cp in cps:
            cp.wait()

    return pl.pallas_call(
        body, out_shape=[SDS((N_SHARD, g.shape[1] // 2, g.shape[2]), g.dtype) for g in gs], in_specs=[HBM] * n, out_specs=[HBM] * n,
        scratch_shapes=[pltpu.SemaphoreType.DMA((n,)), pltpu.SemaphoreType.DMA((n,))], name=name,
    )(*gs)


def _scatter_copies(h_refs, o_refs, send_sems, recv_sems):
    _, _, c, chips = _place()
    return [pltpu.make_async_remote_copy(
        src_ref=h_refs[k].at[2 * chip[0] + chip[1]], dst_ref=o_refs[k].at[j], send_sem=send_sems.at[k, j],
        recv_sem=recv_sems.at[k, j], device_id=(*chip, c), device_id_type=MESH)
        for k in range(len(h_refs)) for j, chip in enumerate(chips)]


def chip_scatter(hs, name):
    n = len(hs)

    def body(*refs):
        cps = _scatter_copies(refs[:n], refs[n:2 * n], *refs[2 * n:])
        for cp in cps:
            cp.start()
        for cp in cps:
            cp.wait()

    return pl.pallas_call(
        body, out_shape=[SDS((3,) + h.shape[1:], h.dtype) for h in hs], in_specs=[HBM] * n, out_specs=[HBM] * n,
        scratch_shapes=[pltpu.SemaphoreType.DMA((n, 3)), pltpu.SemaphoreType.DMA((n, 3))], name=name,
    )(*hs)


def sibling_swap(ghs, name):
    n = len(ghs)

    def body(*refs):
        g_refs, o_refs = refs[:n], refs[n:2 * n]
        send_sems, recv_sems = refs[2 * n:]
        x, y, c, _ = _place()
        cps = []
        for k in range(n):
            cp = pltpu.make_async_remote_copy(src_ref=g_refs[k], dst_ref=o_refs[k], send_sem=send_sems.at[k],
                                              recv_sem=recv_sems.at[k], device_id=(x, y, 1 - c), device_id_type=MESH)
            cp.start()
            cps.append(cp)
        for cp in cps:
            cp.wait()

    return pl.pallas_call(
        body, out_shape=[SDS(g.shape, g.dtype) for g in ghs], in_specs=[HBM] * n, out_specs=[HBM] * n,
        scratch_shapes=[pltpu.SemaphoreType.DMA((n,)), pltpu.SemaphoreType.DMA((n,))], name=name,
    )(*ghs)


def pair_sum(g, ra, cidx, name):
    _, r, cols = g.shape
    hr = r // 2

    def body(c_ref, g_ref, a_ref, o_ref):
        o_ref[...] = (g_ref[...] + a_ref[...]).astype(BF16)

    return pl.pallas_call(
        body,
        grid_spec=pltpu.PrefetchScalarGridSpec(
            num_scalar_prefetch=1, grid=(N_SHARD,),
            in_specs=[pl.BlockSpec((1, hr, cols), lambda s, c_ref: (s, c_ref[0], 0)),
                      pl.BlockSpec((1, hr, cols), lambda s, c_ref: (s, 0, 0))],
            out_specs=pl.BlockSpec((1, hr, cols), lambda s, c_ref: (s, 0, 0))),
        out_shape=SDS((N_SHARD, hr, cols), BF16), compiler_params=_cp("arbitrary"), name=name,
    )(cidx, g, ra)


def chip_sum(h, rb, sidx, name):
    _, hr, cols = h.shape

    def body(s_ref, h_ref, r_ref, o_ref):
        o_ref[...] = ((h_ref[0].astype(F32) + r_ref[0].astype(F32)) + r_ref[1].astype(F32)) + r_ref[2].astype(F32)

    return pl.pallas_call(
        body,
        grid_spec=pltpu.PrefetchScalarGridSpec(
            num_scalar_prefetch=1, grid=(1,),
            in_specs=[pl.BlockSpec((1, hr, cols), lambda i, s_ref: (s_ref[0], 0, 0)),
                      pl.BlockSpec((3, hr, cols), lambda i, s_ref: (0, 0, 0))],
            out_specs=pl.BlockSpec((hr, cols), lambda i, s_ref: (0, 0))),
        out_shape=SDS((hr, cols), F32), compiler_params=_cp("arbitrary"), name=name,
    )(sidx, h, rb)


def _shard_cols(g, n_valid):
    r = g.shape[0]
    return g[:, :n_valid].reshape(r, N_SHARD, n_valid // N_SHARD).transpose(1, 0, 2)


def _unshard_cols(o, pad_to):
    _, r, n = o.shape
    full = o.transpose(1, 0, 2).reshape(r, N_SHARD * n)
    return jnp.pad(full, ((0, 0), (0, pad_to - N_SHARD * n)))


def _rows_of_tiles(t):
    B, H, S = t.shape
    return t.reshape(B, H, S // FT, 1, FT)


def mixer_fwd(x1, mod3, g_pre, w_main, w_f, b_forget_pad, goa, gob, w_out, g_post, tabs, nb):
    hmix, pa, pb, flog = mixer_proj(x1, mod3, g_pre, w_main, w_f, *tabs, name="mixer_proj")
    out_a, lse_a = band_fwd(pa, name="band_fwd")
    F = forget_cumsum(flog.reshape(nb, SEQ, LANE), b_forget_pad, name="forget_cumsum")
    Fh = F[:, :, :NH].transpose(0, 2, 1)
    fblk = Fh.reshape(nb, NH, SEQ // FB, 1, FB)
    frow = _rows_of_tiles(Fh)
    out_b, lse_b = fox_fwd(pb, fblk, frow, name="fox_fwd")
    x2, merged, y0m = mixer_out_fwd(out_a, out_b, goa, gob, w_out, g_post, x1, mod3, name="mixer_out_fwd")
    res = dict(hmix=hmix, flog=flog, pa=pa, pb=pb, out_a=out_a, lse_a=lse_a, fblk=fblk, frow=frow, out_b=out_b,
               lrow=_rows_of_tiles(lse_b.reshape(nb, NH, SEQ)), merged=merged, y0m=y0m)
    return x2, res


def mixer_bwd(dx2, x1, mod3, g_pre, w_main, w_f, b_forget_pad, goa, gob, w_out, g_post, tabs, res, nb):
    T = nb * SEQ
    dy0m, doa, dob, dmgate, dg_post, dgoa, dgob, dvec_b = mixer_out_bwd(
        dx2, res["y0m"], mod3, g_post, w_out, res["out_a"], res["out_b"], goa, gob, name="mixer_out_bwd")
    dqa, dka, dva = band_bwd(res["pa"], doa, res["out_a"], res["lse_a"], *tabs, name="band_bwd")
    drow = _rows_of_tiles(dvec_b[:, :NH].reshape(nb, SEQ, NH).transpose(0, 2, 1))
    dqb, dkb, dvb, dfq, dfk = fox_bwd(res["pb"], dob, res["lrow"], drow, res["fblk"], res["frow"], name="fox_bwd")
    dF = (dfq.reshape(nb, NH, SEQ) + dfk.reshape(nb, NH, SEQ)).transpose(0, 2, 1)
    dF = jnp.pad(dF, ((0, 0), (0, 0), (0, LANE - NH)))
    dflog, dbf = forget_cumsum_bwd(dF, res["flog"].reshape(nb, SEQ, LANE), b_forget_pad, name="forget_cumsum_bwd")
    dflog = dflog.reshape(T, LANE)
    dps = (dqa, dka, dva, dqb, dkb, dvb)
    dx1, dmod2, dg_pre = mixer_proj_bwd(dps, dflog, dx2, x1, mod3, g_pre, w_main, w_f, name="mixer_proj_bwd")
    g_main = matmul_tn_cols(res["hmix"], dps, 1024, name="grad_w_in")
    g_f = matmul_tn(res["hmix"], dflog.astype(BF16), D, LANE, 1024, name="grad_w_forget")
    g_out = matmul_tn(res["merged"], dy0m, D, D, 1024, name="grad_w_out")
    dmod3 = jnp.concatenate([dmod2, dmgate], axis=1)
    return dx1, dmod3, dict(g_pre=dg_pre, g_post=dg_post, goa=dgoa, gob=dgob, b_forget=dbf[:, :NH],
                            w_in=jnp.concatenate([g_main, g_f[:, :NH]], axis=1), w_out=g_out)


def ffn_grads(h, dy0, act, dgate, dup, pre, reduce=None):
    g_gate = matmul_tn(h, dgate, D, FF_TN, 1024, name=pre + "_grad_gate")
    if reduce is None:
        g_up = matmul_tn(h, dup, D, FF_TN, 1024, name=pre + "_grad_up")
        g_down = matmul_tn(act, dy0, FF_TN, D, 1024, name=pre + "_grad_down")
        return (g_gate, g_up, g_down), {}
    hs_gate = reduce("gate", g_gate)
    g_up, rb_gate = matmul_tn(h, dup, D, FF_TN, 1024, name=pre + "_grad_up", scatter=hs_gate)
    hs_up = reduce("up", g_up)
    g_down, rb_up = matmul_tn(act, dy0, FF_TN, D, 1024, name=pre + "_grad_down", scatter=hs_up)
    return (g_gate, g_up, g_down), {"gate": (hs_gate[0], rb_gate[0]), "up": (hs_up[0], rb_up[0])}


def local_step(x0, tgt, pos_col, mod, wfull, p, late_weights=None, early_grads=None, last_reduce=None):
    T = x0.shape[0]
    nb = T // SEQ
    mod_ff1, mod_mix, mod_ff2 = mod[:, 0:3], mod[:, 3:6], mod[:, 6:9]
    tabs = rope_tables(pos_col, name="rope_tables")
    bf_pad = jnp.pad(p["b_forget"], ((0, 0), (0, LANE - NH)))

    (x1, h1, gate1, up1, y01), gathered = ffn_fwd(
        x0, mod_ff1, p["g_pre_ff1"], p["g_post_ff1"], wfull["w_ff1_gate"], wfull["w_ff1_up"], wfull["w_ff1_down"], 0.5,
        name="ff1_fwd", gather=None if late_weights is None else late_weights[:2])
    if late_weights is not None:
        wfull = {**wfull, **late_weights[2](gathered)}
    x2, res = mixer_fwd(x1, mod_mix, p["g_pre_mix"], wfull["w_main"], wfull["w_f"], bf_pad, p["g_out_a"], p["g_out_b"],
                        wfull["w_out"], p["g_post_mix"], tabs, nb)
    (x3, h2, gate2, up2, y02), _ = ffn_fwd(x2, mod_ff2, p["g_pre_ff2"], p["g_post_ff2"], wfull["w_ff2_gate"],
                                           wfull["w_ff2_up"], wfull["w_ff2_down"], 0.5, name="ff2_fwd")

    (dx2, dy02, act2, dgate2, dup2, dmod_ff2, dgpre2, dgpost2), (loss_part,) = ffn_bwd(
        x3, x2, y02, mod_ff2, p["g_pre_ff2"], p["g_post_ff2"], gate2, up2, wfull["w_ff2_gate"], wfull["w_ff2_up"],
        wfull["w_ff2_down"], 0.5, name="ff2_bwd", target=tgt)
    gw = {}
    (gw["w_ff2_gate"], gw["w_ff2_up"], gw["w_ff2_down"]), _ = ffn_grads(h2, dy02, act2, dgate2, dup2, "ff2")
    dx1, dmod_mix, gmix = mixer_bwd(dx2, x1, mod_mix, p["g_pre_mix"], wfull["w_main"], wfull["w_f"], bf_pad, p["g_out_a"],
                                    p["g_out_b"], wfull["w_out"], p["g_post_mix"], tabs, res, nb)
    gw["w_in"], gw["w_out"] = gmix["w_in"], gmix["w_out"]
    (dx0, dy01, act1, dgate1, dup1, dmod_ff1, dgpre1, dgpost1), scattered = ffn_bwd(
        dx1, x0, y01, mod_ff1, p["g_pre_ff1"], p["g_post_ff1"], gate1, up1, wfull["w_ff1_gate"], wfull["w_ff1_up"],
        wfull["w_ff1_down"], 0.5, name="ff1_bwd", scatter=None if early_grads is None else early_grads(gw))
    (gw["w_ff1_gate"], gw["w_ff1_up"], gw["w_ff1_down"]), chained = ffn_grads(h1, dy01, act1, dgate1, dup1, "ff1", last_reduce)
    dmod = jnp.concatenate([dmod_ff1, dmod_mix, dmod_ff2], axis=1).reshape(nb, 9 * D)
    small = dict(g_pre_ff1=dgpre1, g_post_ff1=dgpost1, g_pre_mix=gmix["g_pre"], g_post_mix=gmix["g_post"], g_pre_ff2=dgpre2,
                 g_post_ff2=dgpost2, g_out_a=gmix["goa"], g_out_b=gmix["gob"], b_forget=gmix["b_forget"])
    return loss_part, dx0, dmod, gw, small, scattered, chained


def kernel(x, c, positions, w_ada, b_ada, g_pre_ff1, g_post_ff1, w_ff1_gate, w_ff1_up, w_ff1_down, g_pre_mix, g_post_mix, w_in, b_forget, g_out_a, g_out_b, w_out, g_pre_ff2, g_post_ff2, w_ff2_gate, w_ff2_up, w_ff2_down, loss_target, m_w_ada, m_b_ada, m_g_pre_ff1, m_g_post_ff1, m_w_ff1_gate, m_w_ff1_up, m_w_ff1_down, m_g_pre_mix, m_g_post_mix, m_w_in, m_b_forget, m_g_out_a, m_g_out_b, m_w_out, m_g_pre_ff2, m_g_post_ff2, m_w_ff2_gate, m_w_ff2_up, m_w_ff2_down, v_w_ada, v_b_ada, v_g_pre_ff1, v_g_post_ff1, v_w_ff1_gate, v_w_ff1_up, v_w_ff1_down, v_g_pre_mix, v_g_post_mix, v_w_in, v_b_forget, v_g_out_a, v_g_out_b, v_w_out, v_g_pre_ff2, v_g_post_ff2, v_w_ff2_gate, v_w_ff2_up, v_w_ff2_down):
    args = dict(locals())
    nb = x.shape[0]
    T = nb * SEQ
    ax, ay, ac = lax.axis_index("x"), lax.axis_index("y"), lax.axis_index("c")
    shard = 2 * ax + ay
    cidx = jnp.reshape(ac, (1,)).astype(jnp.int32)
    sidx = jnp.reshape(shard, (1,)).astype(jnp.int32)

    big = ["w_ff1_gate", "w_ff1_up", "w_ff1_down", "w_in", "w_out", "w_ff2_gate", "w_ff2_up", "w_ff2_down"]
    vecs = ["g_pre_ff1", "g_post_ff1", "g_pre_mix", "g_post_mix", "g_pre_ff2", "g_post_ff2"]

    first, late = big[:3], big[3:]
    splits = dict(zip(big, [512, 512, 352, 512, 128, 512, 512, 352]))

    def assemble(names, gathered):
        out = {}
        for n, o in zip(names, gathered):
            if n.endswith("gate") or n.endswith("up"):
                out[n] = _unshard_cols(o, DFF_PAD)
            elif n.endswith("down"):
                out[n] = jnp.pad(o.reshape(DFF, D), ((0, DFF_PAD - DFF), (0, 0)))
            elif n == "w_in":
                full = _unshard_cols(o, IN_COLS)
                out["w_main"] = full[:, :IN_MAIN]
                out["w_f"] = jnp.pad(full[:, IN_MAIN:], ((0, 0), (0, LANE - NH)))
            else:
                out[n] = o.reshape(D, D)
        return out

    wfull = assemble(first, all_gather_shards([args[n][0].astype(BF16) for n in first], [splits[n] for n in first],
                                              name="all_gather_weights"))
    late_weights = ([args[n][0].astype(BF16) for n in late], [splits[n] for n in late], functools.partial(assemble, late))

    ncol = w_ada.shape[2]
    c_all = all_gather8(c, name="all_gather_c").reshape(N_DEV * nb, D)
    b_loc = lax.dynamic_slice(b_ada, (0, shard * ncol), (1, ncol))
    mod_loc = ada_fwd(c_all, w_ada[0], b_loc, name="ada_fwd")
    mod_g = all_gather8(mod_loc, name="all_gather_mod")
    row0 = (4 * ax + 2 * ay + ac) * nb
    mod_rows = lax.dynamic_slice(mod_g, (0, row0, 0), (N_DEV, nb, ncol))
    mod = jnp.concatenate([mod_rows[2 * s] for s in range(N_SHARD)], axis=-1).reshape(nb, 9, D)

    small_in = dict(g_pre_ff1=g_pre_ff1, g_post_ff1=g_post_ff1, g_pre_mix=g_pre_mix, g_post_mix=g_post_mix, g_pre_ff2=g_pre_ff2,
                    g_post_ff2=g_post_ff2, g_out_a=g_out_a, g_out_b=g_out_b, b_forget=b_forget)
    def shard_blocked(n, g):
        if n.endswith("gate") or n.endswith("up"):
            return _shard_cols(g, DFF)
        if n.endswith("down"):
            return g[:DFF].reshape(N_SHARD, DFF // N_SHARD, D)
        if n == "w_in":
            return _shard_cols(g, IN_COLS)
        return g.reshape(N_SHARD, D // N_SHARD, D)

    def chip_sums(names, gw, tag):
        gsb = [shard_blocked(n, gw[n]) for n in names]
        ras = sibling_send_half(gsb, name="grad_sibling_send_" + tag)
        return [pair_sum(g, ra, cidx, name=f"grad_pair_sum_{n}") for n, g, ra in zip(names, gsb, ras)]

    hs = {}

    def early_grads(gw):
        hs.update(zip(late, chip_sums(late, gw, "late")))
        return [hs[n] for n in late]

    def last_reduce(which, g):
        return chip_sums(["w_ff1_" + which], {"w_ff1_" + which: g}, which)

    loss_part, dx0, dmod, gw, small, rbs_late, chained = local_step(
        x.reshape(T, D), loss_target.reshape(T, D), positions.reshape(T, 1), mod, wfull, small_in, late_weights, early_grads,
        last_reduce)

    dmod_all = all_gather8(dmod, name="all_gather_dmod").reshape(N_DEV * nb, 9 * D)
    dmod_loc = lax.dynamic_slice(dmod_all, (0, shard * ncol), (N_DEV * nb, ncol))
    g_w_ada = ada_bwd(c_all, dmod_loc, name="ada_bwd")

    rbs = dict(zip(late, rbs_late))
    for which, (h, rb) in chained.items():
        hs["w_ff1_" + which], rbs["w_ff1_" + which] = h, rb
    hs["w_ff1_down"] = chip_sums(["w_ff1_down"], gw, "down")[0]
    rbs["w_ff1_down"] = chip_scatter([hs["w_ff1_down"]], name="grad_chip_scatter")[0]
    ghs = [chip_sum(hs[n], rbs[n], sidx, name=f"grad_chip_sum_{n}") for n in big]
    theirs = sibling_swap(ghs, name="grad_sibling_swap")

    row6 = jnp.concatenate([small["g_out_a"], small["g_out_b"]], axis=1)
    row7 = jnp.concatenate([small["b_forget"], loss_part[0:1, 0:1], jnp.zeros((1, D - NH - 1), F32)], axis=1)
    pack = jnp.concatenate([small[n] for n in vecs] + [row6, row7], axis=0)
    packed = all_gather8(pack, name="all_gather_small").reshape(N_DEV, 8 * D)

    def pack_state(pre):
        r6 = jnp.concatenate([args[pre + "g_out_a"], args[pre + "g_out_b"]], axis=1)
        r7 = jnp.pad(args[pre + "b_forget"], ((0, 0), (0, D - NH)))
        return jnp.concatenate([args[pre + n] for n in vecs] + [r6, r7], axis=0).reshape(1, 8 * D)

    sg, sd, sm, sv = (t.reshape(8, D) for t in vec_adam(packed, pack_state(""), pack_state("m_"), pack_state("v_"), name="adam_small"))

    def unpack(t):
        out = {n: t[i:i + 1] for i, n in enumerate(vecs)}
        out["g_out_a"], out["g_out_b"], out["b_forget"] = t[6:7, :WG], t[6:7, WG:], t[7:8, :NH]
        return out

    outs = dict(grad=unpack(sg), delta=unpack(sd), new_m=unpack(sm), new_v=unpack(sv))
    loss = sg[7, NH]
    outs["grad"]["b_ada"], outs["delta"]["b_ada"], outs["new_m"]["b_ada"], outs["new_v"]["b_ada"] = vec_adam(
        dmod_all, b_ada, m_b_ada, v_b_ada, name="adam_b_ada")

    for n, mine, other in zip(big, ghs, theirs):
        tr = 128 if mine.shape[0] % 128 == 0 else mine.shape[0]
        outs["grad"][n], outs["delta"][n], outs["new_m"][n], outs["new_v"][n] = adam_update_halves(
            args[n], mine, other, args["m_" + n], args["v_" + n], cidx, tr, name="adam_" + n)
    outs["delta"]["w_ada"], outs["new_m"]["w_ada"], outs["new_v"]["w_ada"] = adam_update(
        w_ada, g_w_ada, m_w_ada, v_w_ada, 128, name="adam_w_ada")
    outs["grad"]["w_ada"] = g_w_ada[None]

    order = ["w_ada", "b_ada", "g_pre_ff1", "g_post_ff1", "w_ff1_gate", "w_ff1_up", "w_ff1_down", "g_pre_mix", "g_post_mix", "w_in",
             "b_forget", "g_out_a", "g_out_b", "w_out", "g_pre_ff2", "g_post_ff2", "w_ff2_gate", "w_ff2_up", "w_ff2_down"]
    result = [loss, dx0.reshape(nb, SEQ, D)]
    for kind in ("grad", "delta", "new_m", "new_v"):
        result += [outs[kind][n] for n in order]
    return tuple(result)
```

```python
import functools
import math

import jax
import jax.numpy as jnp
from jax import lax
from jax.experimental import pallas as pl
from jax.experimental.pallas import tpu as pltpu

D = 1024
SEQ = 2048
HD = 64
NH = 8
WG = NH * HD
DFF = 2752
DFF_PAD = 2816
IN_MAIN = 6 * WG
IN_COLS = IN_MAIN + NH
N_SHARD = 4
N_DEV = 8
LANE = 128
QB = 128
FB = 256
FT = 512
FOX_PAIRS = 2
FOX_PAIRS_BWD = 1
BAND_UNROLL = 4
BAND_UNROLL_BWD = 4
PATTERNS = ((1, 16), (4, 4), (16, 1))
ROPE_THETA = 500000.0
EPS = 1e-6
NEG = -1e30
ATTN_SCALE = HD ** -0.5
TM = 512
TM_FFN = 512
TM_BWD = 256
VMEM_LIMIT = 56 * 1024 * 1024

ADAM_LR, ADAM_B1, ADAM_B2, ADAM_EPS, ADAM_WD, ADAM_STEP = 0.001, 0.9, 0.999, 1e-08, 0.01, 10

F32 = jnp.float32
BF16 = jnp.bfloat16
MESH = pl.DeviceIdType.MESH
SDS = jax.ShapeDtypeStruct


def _cp(*sem):
    return pltpu.CompilerParams(dimension_semantics=sem, vmem_limit_bytes=VMEM_LIMIT)


def _dot(a, b):
    return jnp.dot(a, b, preferred_element_type=F32)


def _dot_nt(a, b):
    return lax.dot_general(a, b, (((1,), (1,)), ((), ())), preferred_element_type=F32)


def _dot_tn(a, b):
    return lax.dot_general(a, b, (((0,), (0,)), ((), ())), preferred_element_type=F32)


def _rms(xf):
    return lax.rsqrt(jnp.mean(xf * xf, axis=-1, keepdims=True) + EPS)


def _norm_mod_bwd(dh, xf, g, scale):
    r = _rms(xf)
    xh = xf * r
    dsh = jnp.sum(dh, axis=0, keepdims=True)
    dsc = jnp.sum(dh * (xh * g), axis=0, keepdims=True)
    dn = dh * (1.0 + scale)
    dg = jnp.sum(dn * xh, axis=0, keepdims=True)
    dxh = dn * g
    dx = r * (dxh - xh * jnp.mean(dxh * xh, axis=-1, keepdims=True))
    return dx, dsh, dsc, dg


def _post_bwd(dxo, y0, g, mgate, gs):
    r = _rms(y0)
    yh = y0 * r
    dmg = gs * jnp.sum(dxo * (yh * g), axis=0, keepdims=True)
    dy = (gs * mgate) * dxo
    dg = jnp.sum(dy * yh, axis=0, keepdims=True)
    dyh = dy * g
    dy0 = r * (dyh - yh * jnp.mean(dyh * yh, axis=-1, keepdims=True))
    return dy0, dmg, dg


def _mod_map(i, *_):
    return ((i * TM) // SEQ, 0, 0)


FF_TN = 1408
FF_TILES = ((0, 768), (768, 1536), (1536, 2304), (2304, 2816))


def _resident_scratch():
    return [pltpu.VMEM((D, DFF_PAD), BF16), pltpu.VMEM((D, DFF_PAD), BF16), pltpu.VMEM((DFF_PAD, D), BF16),
            pltpu.SemaphoreType.DMA((3,))]


def _load_resident(first_step, srcs, dsts, sems):
    @pl.when(first_step)
    def _():
        cps = [pltpu.make_async_copy(s, d, sems.at[k]) for k, (s, d) in enumerate(zip(srcs, dsts))]
        for cp in cps:
            cp.start()
        for cp in cps:
            cp.wait()


def ffn_fwd(x, mod3, g_pre, g_post, wg, wu, wd, gs, name, gather=None):
    T = x.shape[0]
    tm = TM_FFN
    ng = 0 if gather is None else len(gather[0])
    plan = None if gather is None else ShardGather([w.shape for w in gather[0]], gather[1])

    def body(*refs):
        x_ref, mod_ref, gpre_ref, gpost_ref = refs[:4]
        xo_ref, h_ref, gate_ref, up_ref, y0_ref = refs[7 + ng:12 + ng]
        wg_ref, wu_ref, wd_ref, wsem = refs[12 + 2 * ng:16 + 2 * ng]
        i = pl.program_id(0)
        if plan is not None:
            comm = (refs[7:7 + ng], refs[12 + ng:12 + 2 * ng], refs[16 + 2 * ng:])
            pl.when(i == 0)(lambda: plan.start(*comm))
        _load_resident(i == 0, refs[4:7], (wg_ref, wu_ref, wd_ref), wsem)

        xf = x_ref[...]
        hb = ((xf * _rms(xf) * gpre_ref[...]) * (1.0 + mod_ref[0, 1:2, :]) + mod_ref[0, 0:1, :]).astype(BF16)
        h_ref[...] = hb
        y0 = None
        for lo, hi in FF_TILES:
            gate = _dot(hb, wg_ref[:, lo:hi])
            up = _dot(hb, wu_ref[:, lo:hi])
            gate_ref[:, lo:hi] = gate.astype(BF16)
            up_ref[:, lo:hi] = up.astype(BF16)
            part = _dot((gate * jax.nn.sigmoid(gate) * up).astype(BF16), wd_ref[lo:hi, :])
            y0 = part if y0 is None else y0 + part
        y0_ref[...] = y0
        xo_ref[...] = xf + (gs * mod_ref[0, 2:3, :]) * (y0 * _rms(y0) * gpost_ref[...])

        if plan is not None:
            pl.when(i == T // tm - 1)(lambda: plan.finish(*comm))

    tok = pl.BlockSpec((tm, D), lambda i: (i, 0))
    vec = pl.BlockSpec((1, D), lambda i: (0, 0))
    hid = pl.BlockSpec((tm, DFF_PAD), lambda i: (i, 0))
    outs = pl.pallas_call(
        body, grid=(T // tm,),
        in_specs=[tok, pl.BlockSpec((1, 3, D), lambda i: ((i * tm) // SEQ, 0, 0)), vec, vec, HBM, HBM, HBM] + [HBM] * ng,
        out_specs=[tok, tok, hid, hid, tok] + [HBM] * ng,
        out_shape=[SDS((T, D), F32), SDS((T, D), BF16), SDS((T, DFF_PAD), BF16), SDS((T, DFF_PAD), BF16), SDS((T, D), F32)]
        + ([] if plan is None else plan.out_shapes(BF16)),
        scratch_shapes=_resident_scratch() + ([] if plan is None else plan.scratch()),
        compiler_params=_cp("arbitrary"), name=name,
    )(x, mod3, g_pre, g_post, wg, wu, wd, *([] if gather is None else gather[0]))
    return outs[:5], outs[5:]


def ffn_bwd(dxo, x, y0, mod3, g_pre, g_post, gate, up, wg, wu, wd, gs, name, scatter=None, target=None):
    assert scatter is None or target is None
    T = x.shape[0]
    nb = T // SEQ
    tm = TM_BWD
    tiles_per_seq = SEQ // tm
    ns = 0 if scatter is None else len(scatter)
    ne = ns + (target is not None)

    def body(*refs):
        dxo_ref, x_ref, y0_ref, mod_ref, gpre_ref, gpost_ref, gate_ref, up_ref = refs[:8]
        dx_ref, dy0_ref, act_ref, dgate_ref, dup_ref, dmod_ref, dgpre_ref, dgpost_ref = refs[11 + ne:19 + ne]
        wg_ref, wu_ref, wd_ref, wsem = refs[19 + 2 * ne:23 + 2 * ne]
        i = pl.program_id(0)
        _load_resident(i == 0, refs[8:11], (wg_ref, wu_ref, wd_ref), wsem)
        if ns:
            comm = (refs[11:11 + ns], refs[19 + ns:19 + 2 * ns], *refs[23 + 2 * ns:])

            @pl.when(i == 0)
            def _():
                for cp in _scatter_copies(*comm):
                    cp.start()

        @pl.when(i == 0)
        def _():
            dgpre_ref[...] = jnp.zeros_like(dgpre_ref)
            dgpost_ref[...] = jnp.zeros_like(dgpost_ref)

        @pl.when(i % tiles_per_seq == 0)
        def _():
            dmod_ref[...] = jnp.zeros_like(dmod_ref)

        dxo = dxo_ref[...]
        if target is not None:
            loss_ref = refs[19 + ne]

            @pl.when(i == 0)
            def _():
                loss_ref[...] = jnp.zeros_like(loss_ref)

            err = dxo - refs[11][...]
            loss_ref[...] += jnp.sum(err * err) * (0.5 / D)
            dxo = err * (1.0 / D)
        dy0, dmg, dg = _post_bwd(dxo, y0_ref[...], gpost_ref[...], mod_ref[0, 2:3, :], gs)
        dmod_ref[0, 2:3, :] += dmg
        dgpost_ref[...] += dg
        db = dy0.astype(BF16)
        dy0_ref[...] = db
        dh = None
        for lo, hi in FF_TILES:
            dact = _dot_nt(db, wd_ref[lo:hi, :])
            g = gate_ref[:, lo:hi].astype(F32)
            u = up_ref[:, lo:hi].astype(F32)
            sig = jax.nn.sigmoid(g)
            sl = g * sig
            dgate = (dact * u * (sig * (1.0 + g * (1.0 - sig)))).astype(BF16)
            dup = (dact * sl).astype(BF16)
            act_ref[:, lo:hi] = (sl * u).astype(BF16)
            dgate_ref[:, lo:hi] = dgate
            dup_ref[:, lo:hi] = dup
            part = _dot_nt(dgate, wg_ref[:, lo:hi]) + _dot_nt(dup, wu_ref[:, lo:hi])
            dh = part if dh is None else dh + part
        dx, dsh, dsc, dg = _norm_mod_bwd(dh, x_ref[...], gpre_ref[...], mod_ref[0, 1:2, :])
        dx_ref[...] = dxo + dx
        dmod_ref[0, 0:1, :] += dsh
        dmod_ref[0, 1:2, :] += dsc
        dgpre_ref[...] += dg

        if ns:
            @pl.when(i == T // tm - 1)
            def _():
                for cp in _scatter_copies(*comm):
                    cp.wait()

    tok = pl.BlockSpec((tm, D), lambda i: (i, 0))
    vec = pl.BlockSpec((1, D), lambda i: (0, 0))
    hid = pl.BlockSpec((tm, DFF_PAD), lambda i: (i, 0))
    modspec = pl.BlockSpec((1, 3, D), lambda i: ((i * tm) // SEQ, 0, 0))
    outs = pl.pallas_call(
        body, grid=(T // tm,),
        in_specs=[tok, tok, tok, modspec, vec, vec, hid, hid, HBM, HBM, HBM] + [HBM] * ns + [tok] * (ne - ns),
        out_specs=[tok, tok, hid, hid, hid, modspec, vec, vec] + [HBM] * ns
        + [pl.BlockSpec((8, LANE), lambda i: (0, 0))] * (ne - ns),
        out_shape=[SDS((T, D), F32), SDS((T, D), BF16), SDS((T, DFF_PAD), BF16), SDS((T, DFF_PAD), BF16),
                   SDS((T, DFF_PAD), BF16), SDS((nb, 3, D), F32), SDS((1, D), F32), SDS((1, D), F32)]
        + [SDS((3,) + h.shape[1:], h.dtype) for h in (scatter or [])] + [SDS((8, LANE), F32)] * (ne - ns),
        scratch_shapes=_resident_scratch()
        + ([pltpu.SemaphoreType.DMA((ns, 3)), pltpu.SemaphoreType.DMA((ns, 3))] if ns else []),
        compiler_params=_cp("arbitrary"), name=name,
    )(dxo, x, y0, mod3, g_pre, g_post, gate, up, wg, wu, wd, *(scatter or []), *([] if target is None else [target]))
    return outs[:8], outs[8:]


def matmul_tn(a, b, tm, tn, tk, name, scatter=None):
    T, M = a.shape
    N = b.shape[1]
    grid = (M // tm, N // tn, T // tk)
    ns = 0 if scatter is None else len(scatter)

    def body(*refs):
        a_ref, b_ref = refs[:2]
        o_ref = refs[2 + ns]
        ids = [pl.program_id(ax) for ax in range(3)]
        if ns:
            comm = (refs[2:2 + ns], refs[3 + ns:3 + 2 * ns], *refs[3 + 2 * ns:])

            @pl.when((ids[0] == 0) & (ids[1] == 0) & (ids[2] == 0))
            def _():
                for cp in _scatter_copies(*comm):
                    cp.start()

        @pl.when(ids[2] == 0)
        def _():
            o_ref[...] = jnp.zeros_like(o_ref)

        o_ref[...] += _dot_tn(a_ref[...], b_ref[...])

        if ns:
            @pl.when((ids[0] == grid[0] - 1) & (ids[1] == grid[1] - 1) & (ids[2] == grid[2] - 1))
            def _():
                for cp in _scatter_copies(*comm):
                    cp.wait()

    outs = pl.pallas_call(
        body, grid=grid,
        in_specs=[pl.BlockSpec((tk, tm), lambda i, j, k: (k, i)), pl.BlockSpec((tk, tn), lambda i, j, k: (k, j))] + [HBM] * ns,
        out_specs=[pl.BlockSpec((tm, tn), lambda i, j, k: (i, j))] + [HBM] * ns,
        out_shape=[SDS((M, N), F32)] + [SDS((3,) + h.shape[1:], h.dtype) for h in (scatter or [])],
        scratch_shapes=[pltpu.SemaphoreType.DMA((ns, 3)), pltpu.SemaphoreType.DMA((ns, 3))] if ns else [],
        compiler_params=_cp("arbitrary", "arbitrary", "arbitrary"), name=name,
    )(a, b, *(scatter or []))
    return outs[0] if scatter is None else (outs[0], outs[1:])


def matmul_tn_cols(a, bs, tk, name):
    T, M = a.shape
    n = bs[0].shape[1]
    ng = len(bs)

    def body(*refs):
        a_ref, b_refs, o_ref = refs[0], refs[1:1 + ng], refs[1 + ng]

        @pl.when(pl.program_id(0) == 0)
        def _():
            o_ref[...] = jnp.zeros_like(o_ref)

        av = a_ref[...]
        for g, b_ref in enumerate(b_refs):
            o_ref[:, g * n:(g + 1) * n] += _dot_tn(av, b_ref[...])

    return pl.pallas_call(
        body, grid=(T // tk,),
        in_specs=[pl.BlockSpec((tk, M), lambda k: (k, 0))] + [pl.BlockSpec((tk, n), lambda k: (k, 0))] * ng,
        out_specs=pl.BlockSpec((M, ng * n), lambda k: (0, 0)), out_shape=SDS((M, ng * n), F32),
        compiler_params=_cp("arbitrary"), name=name,
    )(a, *bs)


def rope_tables(pos_col, name):
    T = pos_col.shape[0]
    tm = 1024

    def body(p_ref, c_ref, s1_ref, s2_ref):
        lane = lax.broadcasted_iota(jnp.int32, (1, LANE), 1)
        l64 = lane % HD
        inv_freq = jnp.exp((l64 % 8).astype(F32) * (-math.log(ROPE_THETA) / 8.0))
        ang = p_ref[...].astype(F32) * inv_freq
        cs = jnp.cos(ang)
        sn = jnp.sin(ang)
        c_ref[...] = jnp.where(l64 < 16, cs, 1.0)
        s1_ref[...] = jnp.where(l64 < 8, -sn, 0.0)
        s2_ref[...] = jnp.where((l64 >= 8) & (l64 < 16), sn, 0.0)

    tab = pl.BlockSpec((tm, LANE), lambda i: (i, 0))
    return pl.pallas_call(
        body, grid=(T // tm,), in_specs=[pl.BlockSpec((tm, 1), lambda i: (i, 0))], out_specs=[tab, tab, tab],
        out_shape=[SDS((T, LANE), F32)] * 3, compiler_params=_cp("arbitrary"), name=name,
    )(pos_col)


def mixer_proj(x, mod3, g_pre, w_main, w_f, rc, rs1, rs2, name):
    T = x.shape[0]

    def body(x_ref, mod_ref, g_ref, w_ref, wf_ref, c_ref, s1_ref, s2_ref, h_ref, pa_ref, pb_ref, f_ref):
        xf = x_ref[...]
        h = (xf * _rms(xf) * g_ref[...]) * (1.0 + mod_ref[0, 1:2, :]) + mod_ref[0, 0:1, :]
        hb = h.astype(BF16)
        h_ref[...] = hb
        f_ref[...] = _dot(hb, wf_ref[...])
        c, s1, s2 = c_ref[...], s1_ref[...], s2_ref[...]
        for grp in range(2):
            pr = _dot(hb, w_ref[:, grp * WG:(grp + 1) * WG])
            for k in range(WG // LANE):
                t = pr[:, k * LANE:(k + 1) * LANE]
                pa_ref[:, grp * WG + k * LANE:grp * WG + (k + 1) * LANE] = (
                    t * c + pltpu.roll(t, LANE - 8, 1) * s1 + pltpu.roll(t, 8, 1) * s2)
        pa_ref[:, 2 * WG:3 * WG] = _dot(hb, w_ref[:, 2 * WG:3 * WG])
        for grp in range(3):
            pb_ref[:, grp * WG:(grp + 1) * WG] = _dot(hb, w_ref[:, (3 + grp) * WG:(4 + grp) * WG]).astype(BF16)

    tok = pl.BlockSpec((TM, D), lambda i: (i, 0))
    vec = pl.BlockSpec((1, D), lambda i: (0, 0))
    tab = pl.BlockSpec((TM, LANE), lambda i: (i, 0))
    grp3 = pl.BlockSpec((TM, 3 * WG), lambda i: (i, 0))
    return pl.pallas_call(
        body, grid=(T // TM,),
        in_specs=[tok, pl.BlockSpec((1, 3, D), _mod_map), vec, pl.BlockSpec((D, IN_MAIN), lambda i: (0, 0)),
                  pl.BlockSpec((D, LANE), lambda i: (0, 0)), tab, tab, tab],
        out_specs=[tok, grp3, grp3, tab],
        out_shape=[SDS((T, D), BF16), SDS((T, 3 * WG), F32), SDS((T, 3 * WG), BF16), SDS((T, LANE), F32)],
        compiler_params=_cp("arbitrary"), name=name,
    )(x, mod3, g_pre, w_main, w_f, rc, rs1, rs2)


def _head_lanes():
    return lax.broadcasted_iota(jnp.int32, (1, LANE), 1) < HD


def _pair(m0, a, b):
    return jnp.where(m0, a, b)


def _band_rows(i, d, nbc):
    if nbc == 1:
        return i, i, 0
    r, mb = i // nbc, i % nbc
    return r + mb * (QB * d), r + jnp.maximum(mb - 1, 0) * (QB * d), jnp.where(mb > 0, QB, 0)


def _rows(start, size, d):
    return pl.ds(pl.multiple_of(start, QB), size) if d == 1 else pl.ds(start, size, stride=d)


def _band_valid(span, off):
    rq = lax.broadcasted_iota(jnp.int32, (QB, span), 0)
    rel = lax.broadcasted_iota(jnp.int32, (QB, span), 1) - off
    return (rel <= rq) & (rel >= rq - QB)


def band_fwd(pa, name):
    T = pa.shape[0]
    B = T // SEQ
    NP = WG // LANE

    def body(q_ref, k_ref, v_ref, out_ref, lse_ref, o_s, l_s):
        m0 = _head_lanes()
        for pidx, (d, nbc) in enumerate(PATTERNS):
            span = QB if nbc == 1 else 2 * QB

            def blk(it, carry, pidx=pidx, d=d, nbc=nbc, span=span):
                ld = []
                for u in range(BAND_UNROLL):
                    qs, ks, off = _band_rows(it * BAND_UNROLL + u, d, nbc)
                    q = q_ref[_rows(qs, QB, d), :] * ATTN_SCALE
                    ld.append((qs, q, k_ref[_rows(ks, span, d), :].astype(BF16), v_ref[_rows(ks, span, d), :].astype(BF16),
                               _band_valid(span, off)))
                ss = [[jnp.where(valid, _dot_nt(jnp.where(mh, q, 0.0).astype(BF16), k), NEG) for mh in (m0, jnp.logical_not(m0))]
                      for _, q, k, _, valid in ld]
                ps = []
                for pair in ss:
                    row = []
                    for s in pair:
                        m = jnp.max(s, axis=-1, keepdims=True)
                        p = jnp.exp(s - m)
                        row.append((p.astype(BF16), jnp.sum(p, axis=-1, keepdims=True), m))
                    ps.append(row)
                pv = [[_dot(p, ld[u][3]) for p, _, _ in ps[u]] for u in range(BAND_UNROLL)]
                for u in range(BAND_UNROLL):
                    rows = _rows(ld[u][0], QB, d)
                    (_, l0, mx0), (_, l1, mx1) = ps[u]
                    o_s[pidx, rows, :] = _pair(m0, pv[u][0] / l0, pv[u][1] / l1)
                    l_s[pidx, rows, :] = _pair(m0, mx0 + jnp.log(l0), mx1 + jnp.log(l1))
                return carry

            lax.fori_loop(0, SEQ // QB // BAND_UNROLL, blk, 0)
        for c in range(SEQ // FB):
            sl = slice(c * FB, (c + 1) * FB)
            a, b, e = l_s[0, sl, :], l_s[1, sl, :], l_s[2, sl, :]
            m = jnp.maximum(jnp.maximum(a, b), e)
            L = m + jnp.log(jnp.exp(a - m) + jnp.exp(b - m) + jnp.exp(e - m))
            out_ref[sl, :] = jnp.exp(a - L) * o_s[0, sl, :] + jnp.exp(b - L) * o_s[1, sl, :] + jnp.exp(e - L) * o_s[2, sl, :]
            lse_ref[sl, :] = L

    blk_of = lambda g: pl.BlockSpec((SEQ, LANE), lambda b, hp, g=g: (b, g * NP + hp))
    return pl.pallas_call(
        body, grid=(B, NP), in_specs=[blk_of(0), blk_of(1), blk_of(2)], out_specs=[blk_of(0), blk_of(0)],
        out_shape=[SDS((T, WG), F32), SDS((T, WG), F32)],
        scratch_shapes=[pltpu.VMEM((3, SEQ, LANE), F32), pltpu.VMEM((3, SEQ, LANE), F32)],
        compiler_params=_cp("arbitrary", "arbitrary"), name=name,
    )(pa, pa, pa)


def _pair_rowsum(m0, prod):
    s0 = jnp.sum(jnp.where(m0, prod, 0.0), axis=-1, keepdims=True)
    return _pair(m0, s0, jnp.sum(prod, axis=-1, keepdims=True) - s0)


def band_bwd(pa, do, out, lse, rc, rs1, rs2, name):
    T = pa.shape[0]
    B = T // SEQ
    NP = WG // LANE

    def body(q_ref, k_ref, v_ref, do_ref, out_ref, l_ref, c_ref, s1_ref, s2_ref, dqo_ref, dko_ref, dvo_ref, d_s, dq_ref, dk_ref,
             dv_ref):
        m0 = _head_lanes()
        dq_ref[...] = jnp.zeros_like(dq_ref)
        dk_ref[...] = jnp.zeros_like(dk_ref)
        dv_ref[...] = jnp.zeros_like(dv_ref)
        for c in range(SEQ // FB):
            sl = slice(c * FB, (c + 1) * FB)
            d_s[sl, :] = _pair_rowsum(m0, do_ref[sl, :] * out_ref[sl, :])
        for d, nbc in PATTERNS:
            span = QB if nbc == 1 else 2 * QB

            def blk(it, carry, d=d, nbc=nbc, span=span):
                masks = (m0, jnp.logical_not(m0))
                ld = []
                for u in range(BAND_UNROLL_BWD):
                    qs, ks, off = _band_rows(it * BAND_UNROLL_BWD + u, d, nbc)
                    qrow, krow = _rows(qs, QB, d), _rows(ks, span, d)
                    ld.append(dict(qrow=qrow, krow=krow, q=q_ref[qrow, :] * ATTN_SCALE, k=k_ref[krow, :].astype(BF16),
                                   v=v_ref[krow, :].astype(BF16), do=do_ref[qrow, :], l=l_ref[qrow, :], dv=d_s[qrow, :],
                                   valid=_band_valid(span, off)))
                for t in ld:
                    t["qm"] = [jnp.where(mh, t["q"], 0.0).astype(BF16) for mh in masks]
                    t["dom"] = [jnp.where(mh, t["do"], 0.0).astype(BF16) for mh in masks]
                sd = [[(jnp.where(t["valid"], _dot_nt(t["qm"][h], t["k"]), NEG), _dot_nt(t["dom"][h], t["v"])) for h in range(2)]
                      for t in ld]
                pd = []
                for t, pair in zip(ld, sd):
                    row = []
                    for h, (s, dp) in enumerate(pair):
                        col = slice(h * HD, h * HD + 1)
                        p = jnp.exp(s - t["l"][:, col])
                        row.append((p.astype(BF16), (p * (dp - t["dv"][:, col])).astype(BF16)))
                    pd.append(row)
                gr = [(_dot(row[0][1], t["k"]), _dot(row[1][1], t["k"]),
                       _dot_tn(jnp.concatenate([row[0][1], row[1][1]], axis=0), jnp.concatenate(t["qm"], axis=0)),
                       _dot_tn(jnp.concatenate([row[0][0], row[1][0]], axis=0), jnp.concatenate(t["dom"], axis=0)))
                      for t, row in zip(ld, pd)]
                for t, (dq0, dq1, dk, dv) in zip(ld, gr):
                    dq_ref[t["qrow"], :] += _pair(m0, dq0, dq1) * ATTN_SCALE
                    dk_ref[t["krow"], :] += dk
                    dv_ref[t["krow"], :] += dv
                return carry

            lax.fori_loop(0, SEQ // QB // BAND_UNROLL_BWD, blk, 0)
        for c in range(SEQ // FB):
            sl = slice(c * FB, (c + 1) * FB)
            cc, s1, s2 = c_ref[sl, :], s1_ref[sl, :], s2_ref[sl, :]
            for acc, o_ref in ((dq_ref, dqo_ref), (dk_ref, dko_ref)):
                d = acc[sl, :]
                o_ref[sl, :] = (d * cc + pltpu.roll(d * s1, 8, 1) + pltpu.roll(d * s2, LANE - 8, 1)).astype(BF16)
            dvo_ref[sl, :] = dv_ref[sl, :].astype(BF16)

    blk_of = lambda g: pl.BlockSpec((SEQ, LANE), lambda b, hp, g=g: (b, g * NP + hp))
    tab = pl.BlockSpec((SEQ, LANE), lambda b, hp: (b, 0))
    return pl.pallas_call(
        body, grid=(B, NP), in_specs=[blk_of(0), blk_of(1), blk_of(2), blk_of(0), blk_of(0), blk_of(0), tab, tab, tab],
        out_specs=[blk_of(0)] * 3, out_shape=[SDS((T, WG), BF16)] * 3,
        scratch_shapes=[pltpu.VMEM((SEQ, LANE), F32)] * 4,
        compiler_params=_cp("arbitrary", "arbitrary"), name=name,
    )(pa, pa, pa, do, out, lse, rc, rs1, rs2)


def _tile_causal(nq, nk, q0, k0):
    r = lax.broadcasted_iota(jnp.int32, (nq, nk), 0)
    c = lax.broadcasted_iota(jnp.int32, (nq, nk), 1)
    return r + (q0 - k0) >= c


def _row_to_col(row):
    n = row.shape[1]
    return jnp.transpose(jnp.broadcast_to(row, (LANE, n)))[:, 0:1]


def _col_to_row(col):
    n = col.shape[0]
    return jnp.transpose(jnp.broadcast_to(col, (n, LANE)))[0:1, :]


def fox_fwd(pb, fblk, frow, name, gather=None):
    T = pb.shape[0]
    B = T // SEQ
    NG = WG // (LANE * FOX_PAIRS)
    NHS = 2 * FOX_PAIRS
    W = LANE * FOX_PAIRS
    n = SEQ // FB
    ng = 0 if gather is None else len(gather[0])
    plan = None if gather is None else ShardGather([w.shape for w in gather[0]], gather[1])

    def body(*refs):
        q_ref, k_ref, v_ref, fc_ref, fr_ref = refs[:5]
        o_ref, lse_ref = refs[5 + ng:7 + ng]
        if plan is not None:
            comm = (refs[5:5 + ng], refs[7 + ng:7 + 2 * ng], refs[7 + 2 * ng:])
            ids = [pl.program_id(ax) for ax in range(3)]
            pl.when((ids[0] == 0) & (ids[1] == 0) & (ids[2] == 0))(lambda: plan.start(*comm))
        i = pl.program_id(2)
        m0 = _head_lanes()
        masks = (m0, jnp.logical_not(m0))
        heads = [(hh, slice((hh // 2) * LANE, (hh // 2 + 1) * LANE), masks[hh % 2]) for hh in range(NHS)]
        qh, fq = [], []
        for hh, lanes, mh in heads:
            q = q_ref[:, lanes] * ATTN_SCALE
            qh.append(jnp.where(mh, q, jnp.zeros_like(q)))
            fq.append(_row_to_col(fc_ref[0, hh, 0]))

        def step(t, carry, masked):
            rows = pl.ds(pl.multiple_of(t * FT, FT), FT)
            ss = [_dot_nt(qh[hh], k_ref[rows, lanes]) + fq[hh] - fr_ref[0, hh, t] for hh, lanes, _ in heads]
            if masked:
                ok = _tile_causal(FB, FT, i * FB, t * FT)
                ss = [jnp.where(ok, s, NEG) for s in ss]
            st = []
            for hh, _, _ in heads:
                m2 = jnp.maximum(carry[hh][0], jnp.max(ss[hh], axis=-1, keepdims=True))
                st.append((m2, jnp.exp(carry[hh][0] - m2), jnp.exp(ss[hh] - m2).astype(BF16)))
            pv = []
            for hh, lanes, mh in heads:
                vt = v_ref[rows, lanes]
                pv.append(_dot(st[hh][2], jnp.where(mh, vt, jnp.ones_like(vt))))
            return tuple((st[hh][0], st[hh][1] * carry[hh][1] + pv[hh]) for hh in range(NHS))

        one = (jnp.full((FB, 1), NEG, F32), jnp.zeros((FB, LANE), F32))
        last = (i * FB) // FT
        carry = lax.fori_loop(0, last, lambda t, cr: step(t, cr, False), (one,) * NHS)
        carry = step(last, carry, True)
        for pr in range(FOX_PAIRS):
            (ma, acca), (mb, accb) = carry[2 * pr], carry[2 * pr + 1]
            la, lb = acca[:, HD:HD + 1], accb[:, 0:1]
            o_ref[:, pr * LANE:(pr + 1) * LANE] = _pair(m0, acca / la, accb / lb)
            lse_ref[0, 2 * pr, 0] = _col_to_row(ma + jnp.log(la))
            lse_ref[0, 2 * pr + 1, 0] = _col_to_row(mb + jnp.log(lb))
        if plan is not None:
            pl.when((ids[0] == B - 1) & (ids[1] == NG - 1) & (ids[2] == n - 1))(lambda: plan.finish(*comm))

    qblk = pl.BlockSpec((FB, W), lambda b, g, i: (b * n + i, g))
    full = lambda grp: pl.BlockSpec((SEQ, W), lambda b, g, i, grp=grp: (b, grp * NG + g))
    rowb = pl.BlockSpec((1, NHS, 1, 1, FB), lambda b, g, i: (b, g, i, 0, 0))
    outs = pl.pallas_call(
        body, grid=(B, NG, n),
        in_specs=[qblk, full(1), full(2), rowb, pl.BlockSpec((1, NHS, SEQ // FT, 1, FT), lambda b, g, i: (b, g, 0, 0, 0))]
        + [HBM] * ng,
        out_specs=[qblk, rowb] + [HBM] * ng,
        out_shape=[SDS((T, WG), F32), SDS((B, NH, n, 1, FB), F32)] + ([] if plan is None else plan.out_shapes(BF16)),
        scratch_shapes=[] if plan is None else plan.scratch(),
        compiler_params=_cp("arbitrary", "arbitrary", "arbitrary"), name=name,
    )(pb, pb, pb, fblk, frow, *([] if gather is None else gather[0]))
    return outs[:2], outs[2:]


def fox_bwd(pb, do, lrow, drow, fblk, frow, name):
    T = pb.shape[0]
    B = T // SEQ
    PAIRS = FOX_PAIRS_BWD
    NG = WG // (LANE * PAIRS)
    NHS = 2 * PAIRS
    W = LANE * PAIRS
    n = SEQ // FB

    def body(q_ref, k_ref, v_ref, do_ref, l_ref, d_ref, fc_ref, fr_ref, dqo_ref, dk_ref, dv_ref, dfq_ref, dfk_ref, dq_ref):
        j = pl.program_id(2)
        m0 = _head_lanes()
        masks = (m0, jnp.logical_not(m0))
        heads = [(hh, slice((hh // 2) * LANE, (hh // 2 + 1) * LANE), masks[hh % 2]) for hh in range(NHS)]

        @pl.when(j == 0)
        def _():
            dq_ref[...] = jnp.zeros_like(dq_ref)
            dfq_ref[...] = jnp.zeros_like(dfq_ref)

        kj = [k_ref[:, lanes] for _, lanes, _ in heads]
        vj = [v_ref[:, lanes] for _, lanes, _ in heads]
        fk = [_row_to_col(fc_ref[0, hh, 0]) for hh in range(NHS)]

        def step(t, carry, masked):
            rows = pl.ds(pl.multiple_of(t * FT, FT), FT)
            qm, dom = [], []
            for _, lanes, mh in heads:
                qt = q_ref[rows, lanes] * ATTN_SCALE
                qm.append(jnp.where(mh, qt, jnp.zeros_like(qt)))
                dom.append(jnp.where(mh, do_ref[rows, lanes], 0.0).astype(BF16))
            ss = [_dot_nt(kj[hh], qm[hh]) + fr_ref[0, hh, t] - fk[hh] for hh in range(NHS)]
            dps = [_dot_nt(vj[hh], dom[hh]) for hh in range(NHS)]
            if masked:
                key = lax.broadcasted_iota(jnp.int32, (FB, FT), 0)
                qry = lax.broadcasted_iota(jnp.int32, (FB, FT), 1)
                ok = qry + (t * FT - j * FB) >= key
                ss = [jnp.where(ok, s, NEG) for s in ss]
            pds = []
            for hh in range(NHS):
                p = jnp.exp(ss[hh] - l_ref[0, hh, t])
                ds = p * (dps[hh] - d_ref[0, hh, t])
                dfq_ref[0, hh, t] += jnp.sum(ds, axis=0, keepdims=True)
                pds.append((p.astype(BF16), ds.astype(BF16), jnp.sum(ds, axis=-1, keepdims=True)))
            dks = [_dot(pds[hh][1], qm[hh]) for hh in range(NHS)]
            dvs = [_dot(pds[hh][0], dom[hh]) for hh in range(NHS)]
            dqs = [_dot_tn(pds[hh][1], kj[hh]) for hh in range(NHS)]
            for pr in range(PAIRS):
                dq_ref[rows, pr * LANE:(pr + 1) * LANE] += _pair(m0, dqs[2 * pr], dqs[2 * pr + 1]) * ATTN_SCALE
            return tuple((carry[hh][0] + dks[hh], carry[hh][1] + dvs[hh], carry[hh][2] - pds[hh][2]) for hh in range(NHS))

        one = (jnp.zeros((FB, LANE), F32), jnp.zeros((FB, LANE), F32), jnp.zeros((FB, 1), F32))
        first = (j * FB) // FT
        carry = step(first, (one,) * NHS, True)
        carry = lax.fori_loop(first + 1, SEQ // FT, lambda t, cr: step(t, cr, False), carry)
        for pr in range(PAIRS):
            (dka, dva, dfka), (dkb, dvb, dfkb) = carry[2 * pr], carry[2 * pr + 1]
            dk_ref[:, pr * LANE:(pr + 1) * LANE] = _pair(m0, dka, dkb).astype(BF16)
            dv_ref[:, pr * LANE:(pr + 1) * LANE] = _pair(m0, dva, dvb).astype(BF16)
            dfk_ref[0, 2 * pr, 0] = _col_to_row(dfka)
            dfk_ref[0, 2 * pr + 1, 0] = _col_to_row(dfkb)

        @pl.when(j == n - 1)
        def _():
            dqo_ref[...] = dq_ref[...].astype(BF16)

    kblk = lambda grp: pl.BlockSpec((FB, W), lambda b, g, j, grp=grp: (b * n + j, grp * NG + g))
    full = pl.BlockSpec((SEQ, W), lambda b, g, j: (b, g))
    rowf = pl.BlockSpec((1, NHS, SEQ // FT, 1, FT), lambda b, g, j: (b, g, 0, 0, 0))
    rowb = pl.BlockSpec((1, NHS, 1, 1, FB), lambda b, g, j: (b, g, j, 0, 0))
    return pl.pallas_call(
        body, grid=(B, NG, n), in_specs=[full, kblk(1), kblk(2), full, rowf, rowf, rowb, rowf],
        out_specs=[full, kblk(0), kblk(0), rowf, rowb],
        out_shape=[SDS((T, WG), BF16), SDS((T, WG), BF16), SDS((T, WG), BF16), SDS((B, NH, SEQ // FT, 1, FT), F32),
                   SDS((B, NH, n, 1, FB), F32)],
        scratch_shapes=[pltpu.VMEM((SEQ, W), F32)],
        compiler_params=_cp("arbitrary", "arbitrary", "arbitrary"), name=name,
    )(pb, pb, pb, do, lrow, drow, fblk, frow)


def _tri(lower):
    r = lax.broadcasted_iota(jnp.int32, (LANE, LANE), 0)
    c = lax.broadcasted_iota(jnp.int32, (LANE, LANE), 1)
    return ((r >= c) if lower else (r <= c)).astype(F32)


def _tri_dot(t, xblk):
    return jnp.dot(t, xblk, precision=lax.Precision.HIGHEST, preferred_element_type=F32)


def forget_cumsum(flog, bias, name):
    B, S, _ = flog.shape

    def body(f_ref, b_ref, o_ref):
        t = _tri(True)
        carry = jnp.zeros((1, LANE), F32)
        for blk in range(S // LANE):
            z = f_ref[0, blk * LANE:(blk + 1) * LANE, :] + b_ref[...]
            lf = jnp.minimum(z, 0.0) - jnp.log(1.0 + jnp.exp(-jnp.abs(z)))
            cs = _tri_dot(t, lf) + carry
            o_ref[0, blk * LANE:(blk + 1) * LANE, :] = cs
            carry = cs[LANE - 1:LANE, :]

    spec = pl.BlockSpec((1, S, LANE), lambda b: (b, 0, 0))
    return pl.pallas_call(
        body, grid=(B,), in_specs=[spec, pl.BlockSpec((1, LANE), lambda b: (0, 0))], out_specs=spec,
        out_shape=SDS((B, S, LANE), F32), compiler_params=_cp("arbitrary"), name=name,
    )(flog, bias)


def forget_cumsum_bwd(dF, flog, bias, name):
    B, S, _ = flog.shape

    def body(d_ref, f_ref, b_ref, o_ref, db_ref):
        @pl.when(pl.program_id(0) == 0)
        def _():
            db_ref[...] = jnp.zeros_like(db_ref)

        t = _tri(False)
        carry = jnp.zeros((1, LANE), F32)
        tot = jnp.zeros((1, LANE), F32)
        for blk in reversed(range(S // LANE)):
            sl = slice(blk * LANE, (blk + 1) * LANE)
            rc = _tri_dot(t, d_ref[0, sl, :]) + carry
            carry = rc[0:1, :]
            z = f_ref[0, sl, :] + b_ref[...]
            dz = rc * jax.nn.sigmoid(-z)
            o_ref[0, sl, :] = dz
            tot = tot + jnp.sum(dz, axis=0, keepdims=True)
        db_ref[...] += tot

    spec = pl.BlockSpec((1, S, LANE), lambda b: (b, 0, 0))
    vec = pl.BlockSpec((1, LANE), lambda b: (0, 0))
    return pl.pallas_call(
        body, grid=(B,), in_specs=[spec, spec, vec], out_specs=[spec, vec],
        out_shape=[SDS((B, S, LANE), F32), SDS((1, LANE), F32)], compiler_params=_cp("arbitrary"), name=name,
    )(dF, flog, bias)


def mixer_out_fwd(oa, ob, goa, gob, w_out, g_post, x, mod3, name):
    T = x.shape[0]

    def body(oa_ref, ob_ref, goa_ref, gob_ref, w_ref, gp_ref, x_ref, mod_ref, xo_ref, mg_ref, y0_ref):
        a = oa_ref[...]
        b = ob_ref[...]
        mg = jnp.concatenate([a * _rms(a) * goa_ref[...], b * _rms(b) * gob_ref[...]], axis=-1).astype(BF16)
        mg_ref[...] = mg
        y0 = _dot(mg, w_ref[...])
        y0_ref[...] = y0
        xo_ref[...] = x_ref[...] + mod_ref[0, 2:3, :] * (y0 * _rms(y0) * gp_ref[...])

    tok = pl.BlockSpec((TM, D), lambda i: (i, 0))
    half = pl.BlockSpec((TM, WG), lambda i: (i, 0))
    hv = pl.BlockSpec((1, WG), lambda i: (0, 0))
    return pl.pallas_call(
        body, grid=(T // TM,),
        in_specs=[half, half, hv, hv, pl.BlockSpec((D, D), lambda i: (0, 0)), pl.BlockSpec((1, D), lambda i: (0, 0)), tok,
                  pl.BlockSpec((1, 3, D), _mod_map)],
        out_specs=[tok, tok, tok], out_shape=[SDS((T, D), F32), SDS((T, D), BF16), SDS((T, D), F32)],
        compiler_params=_cp("arbitrary"), name=name,
    )(oa, ob, goa, gob, w_out, g_post, x, mod3)


def mixer_out_bwd(dxo, y0, mod3, g_post, w_out, oa, ob, goa, gob, name):
    T = dxo.shape[0]
    nb = T // SEQ
    tiles_per_seq = SEQ // TM

    def body(dxo_ref, y0_ref, mod_ref, gp_ref, w_ref, oa_ref, ob_ref, goa_ref, gob_ref,
             dy0_ref, doa_ref, dob_ref, dmg_ref, dgp_ref, dgoa_ref, dgob_ref, dvb_ref):
        i = pl.program_id(0)

        @pl.when(i == 0)
        def _():
            dgp_ref[...] = jnp.zeros_like(dgp_ref)
            dgoa_ref[...] = jnp.zeros_like(dgoa_ref)
            dgob_ref[...] = jnp.zeros_like(dgob_ref)

        @pl.when(i % tiles_per_seq == 0)
        def _():
            dmg_ref[...] = jnp.zeros_like(dmg_ref)

        dy0, dmg, dg = _post_bwd(dxo_ref[...], y0_ref[...], gp_ref[...], mod_ref[0, 2:3, :], 1.0)
        dmg_ref[0] += dmg
        dgp_ref[...] += dg
        db = dy0.astype(BF16)
        dy0_ref[...] = db
        dm = _dot_nt(db, w_ref[...])
        for o_ref, g_ref, do_ref, dg_ref, sl in ((oa_ref, goa_ref, doa_ref, dgoa_ref, slice(0, WG)),
                                                  (ob_ref, gob_ref, dob_ref, dgob_ref, slice(WG, 2 * WG))):
            o = o_ref[...]
            r = _rms(o)
            oh = o * r
            d = dm[:, sl]
            dg_ref[...] += jnp.sum(d * oh, axis=0, keepdims=True)
            dh = d * g_ref[...]
            do = r * (dh - oh * jnp.mean(dh * oh, axis=-1, keepdims=True))
            do_ref[...] = do
        ind = (lax.broadcasted_iota(jnp.int32, (WG, LANE), 0) // HD == lax.broadcasted_iota(jnp.int32, (WG, LANE), 1)).astype(BF16)
        prod = do * o
        hi = prod.astype(BF16)
        dvb_ref[...] = _dot(hi, ind) + _dot((prod - hi.astype(F32)).astype(BF16), ind)

    tok = pl.BlockSpec((TM, D), lambda i: (i, 0))
    half = pl.BlockSpec((TM, WG), lambda i: (i, 0))
    hv = pl.BlockSpec((1, WG), lambda i: (0, 0))
    vec = pl.BlockSpec((1, D), lambda i: (0, 0))
    return pl.pallas_call(
        body, grid=(T // TM,),
        in_specs=[tok, tok, pl.BlockSpec((1, 3, D), _mod_map), vec, pl.BlockSpec((D, D), lambda i: (0, 0)), half, half, hv, hv],
        out_specs=[tok, half, half, pl.BlockSpec((1, 1, D), _mod_map), vec, hv, hv, pl.BlockSpec((TM, LANE), lambda i: (i, 0))],
        out_shape=[SDS((T, D), BF16), SDS((T, WG), F32), SDS((T, WG), F32), SDS((nb, 1, D), F32), SDS((1, D), F32),
                   SDS((1, WG), F32), SDS((1, WG), F32), SDS((T, LANE), F32)],
        compiler_params=_cp("arbitrary"), name=name,
    )(dxo, y0, mod3, g_post, w_out, oa, ob, goa, gob)


def mixer_proj_bwd(dps, dflog, dxo, x, mod3, g_pre, w_main, w_f, name):
    T = x.shape[0]
    nb = T // SEQ
    tiles_per_seq = SEQ // TM
    ngrp = len(dps)

    def body(*refs):
        dp_refs = refs[:ngrp]
        df_ref, dxo_ref, x_ref, mod_ref, g_ref, w_ref, wf_ref, dx_ref, dmod_ref, dg_ref = refs[ngrp:]
        i = pl.program_id(0)

        @pl.when(i == 0)
        def _():
            dg_ref[...] = jnp.zeros_like(dg_ref)

        @pl.when(i % tiles_per_seq == 0)
        def _():
            dmod_ref[...] = jnp.zeros_like(dmod_ref)

        dh = _dot_nt(df_ref[...].astype(BF16), wf_ref[...])
        for g, dp_ref in enumerate(dp_refs):
            dh = dh + _dot_nt(dp_ref[...], w_ref[:, g * WG:(g + 1) * WG])
        dx, dsh, dsc, dg = _norm_mod_bwd(dh, x_ref[...], g_ref[...], mod_ref[0, 1:2, :])
        dx_ref[...] = dxo_ref[...] + dx
        dmod_ref[0, 0:1, :] += dsh
        dmod_ref[0, 1:2, :] += dsc
        dg_ref[...] += dg

    tok = pl.BlockSpec((TM, D), lambda i: (i, 0))
    vec = pl.BlockSpec((1, D), lambda i: (0, 0))
    return pl.pallas_call(
        body, grid=(T // TM,),
        in_specs=[pl.BlockSpec((TM, WG), lambda i: (i, 0))] * ngrp
        + [pl.BlockSpec((TM, LANE), lambda i: (i, 0)), tok, tok, pl.BlockSpec((1, 3, D), _mod_map), vec,
           pl.BlockSpec((D, IN_MAIN), lambda i: (0, 0)), pl.BlockSpec((D, LANE), lambda i: (0, 0))],
        out_specs=[tok, pl.BlockSpec((1, 2, D), _mod_map), vec],
        out_shape=[SDS((T, D), F32), SDS((nb, 2, D), F32), SDS((1, D), F32)],
        compiler_params=_cp("arbitrary"), name=name,
    )(*dps, dflog, dxo, x, mod3, g_pre, w_main, w_f)


def ada_fwd(c_all, w, b, name):
    n = w.shape[1]
    tn = n // 2

    def body(c_ref, w_ref, b_ref, o_ref):
        cv = c_ref[...]
        o_ref[...] = _dot((cv * jax.nn.sigmoid(cv)).astype(BF16), w_ref[...].astype(BF16)) + b_ref[...]

    R = c_all.shape[0]
    return pl.pallas_call(
        body, grid=(2,),
        in_specs=[pl.BlockSpec((R, D), lambda j: (0, 0)), pl.BlockSpec((D, tn), lambda j: (0, j)), pl.BlockSpec((1, tn), lambda j: (0, j))],
        out_specs=pl.BlockSpec((R, tn), lambda j: (0, j)), out_shape=SDS((R, n), F32),
        compiler_params=_cp("arbitrary"), name=name,
    )(c_all, w, b)


def ada_bwd(c_all, dmod, name):
    R, n = dmod.shape
    tn = n // 2

    def body(c_ref, d_ref, o_ref):
        cv = c_ref[...]
        o_ref[...] = _dot_tn((cv * jax.nn.sigmoid(cv)).astype(BF16), d_ref[...].astype(BF16))

    return pl.pallas_call(
        body, grid=(2,), in_specs=[pl.BlockSpec((R, D), lambda j: (0, 0)), pl.BlockSpec((R, tn), lambda j: (0, j))],
        out_specs=pl.BlockSpec((D, tn), lambda j: (0, j)), out_shape=SDS((D, n), F32),
        compiler_params=_cp("arbitrary"), name=name,
    )(c_all, dmod)


def _adam_math(w, g, m, v):
    m2 = ADAM_B1 * m + (1.0 - ADAM_B1) * g
    v2 = ADAM_B2 * v + (1.0 - ADAM_B2) * (g * g)
    m_hat = m2 / (1.0 - ADAM_B1 ** ADAM_STEP)
    v_hat = v2 / (1.0 - ADAM_B2 ** ADAM_STEP)
    delta = -ADAM_LR * (m_hat / (jnp.sqrt(v_hat) + ADAM_EPS) + ADAM_WD * w)
    return delta, m2, v2


def adam_update(w, g, m, v, tr, name):
    _, R, C = w.shape

    def body(w_ref, g_ref, m_ref, v_ref, d_ref, mo_ref, vo_ref):
        d_ref[0], mo_ref[0], vo_ref[0] = _adam_math(w_ref[0], g_ref[...], m_ref[0], v_ref[0])

    spec = pl.BlockSpec((1, tr, C), lambda i: (0, i, 0))
    gspec = pl.BlockSpec((tr, C), lambda i: (i, 0))
    return pl.pallas_call(
        body, grid=(R // tr,), in_specs=[spec, gspec, spec, spec], out_specs=[spec] * 3, out_shape=[SDS((1, R, C), F32)] * 3,
        compiler_params=_cp("arbitrary"), name=name,
    )(w, g, m, v)


def adam_update_halves(w, mine, other, m, v, cidx, tr, name):
    _, R, C = w.shape
    nh = R // 2 // tr

    def body(c_ref, w_ref, a_ref, b_ref, m_ref, v_ref, g_ref, d_ref, mo_ref, vo_ref):
        first_half = pl.program_id(0) < nh
        g = jnp.where(first_half == (c_ref[0] == 0), a_ref[...], b_ref[...])
        g_ref[0] = g
        d_ref[0], mo_ref[0], vo_ref[0] = _adam_math(w_ref[0], g, m_ref[0], v_ref[0])

    spec = pl.BlockSpec((1, tr, C), lambda i, c_ref: (0, i, 0))
    hspec = pl.BlockSpec((tr, C), lambda i, c_ref: (i % nh, 0))
    return pl.pallas_call(
        body,
        grid_spec=pltpu.PrefetchScalarGridSpec(num_scalar_prefetch=1, grid=(R // tr,), in_specs=[spec, hspec, hspec, spec, spec],
                                               out_specs=[spec] * 4),
        out_shape=[SDS((1, R, C), F32)] * 4, compiler_params=_cp("arbitrary"), name=name,
    )(cidx, w, mine, other, m, v)


def vec_adam(parts, w, m, v, name):
    P, C = parts.shape

    def body(p_ref, w_ref, m_ref, v_ref, g_ref, d_ref, mo_ref, vo_ref):
        g = jnp.sum(p_ref[...], axis=0, keepdims=True)
        g_ref[...] = g
        d_ref[...], mo_ref[...], vo_ref[...] = _adam_math(w_ref[...], g, m_ref[...], v_ref[...])

    return pl.pallas_call(body, out_shape=[SDS((1, C), F32)] * 4, compiler_params=_cp(), name=name)(parts, w, m, v)


HBM = pl.BlockSpec(memory_space=pltpu.HBM)
VMEM = pl.BlockSpec(memory_space=pltpu.VMEM)


def _place():
    x, y, c = lax.axis_index("x"), lax.axis_index("y"), lax.axis_index("c")
    return x, y, c, [(1 - x, y), (x, 1 - y), (1 - x, 1 - y)]


def all_gather8(xs, name):
    R, C = xs.shape

    def body(x_ref, out_ref, send_sems, recv_sems, local_sem):
        x, y, c, chips = _place()
        me, sibling = (x, y, c), (x, y, 1 - c)

        def slot(px, py, pc):
            return out_ref.at[4 * px + 2 * py + pc]

        def copy(k, block, to, src=None):
            return pltpu.make_async_remote_copy(
                src_ref=slot(*block) if src is None else src, dst_ref=slot(*block),
                send_sem=send_sems.at[k], recv_sem=recv_sems.at[k], device_id=to, device_id_type=MESH)

        mine = pltpu.make_async_copy(x_ref, slot(*me), local_sem)
        mine.start()
        first = [copy(0, me, sibling, src=x_ref)]
        first += [copy(1 + j, me, (*chip, c), src=x_ref) for j, chip in enumerate(chips)]
        for cp in first:
            cp.start()
        passed = [copy(4 + j, (*chip, c), sibling) for j, chip in enumerate(chips)]
        for j, chip in enumerate(chips):
            copy(1 + j, (*chip, c), me).wait_recv()
            passed[j].start()
        copy(0, sibling, me).wait_recv()
        for j, chip in enumerate(chips):
            copy(4 + j, (*chip, 1 - c), me).wait_recv()
        for cp in first + passed:
            cp.wait_send()
        mine.wait()

    return pl.pallas_call(
        body, out_shape=SDS((N_DEV, R, C), xs.dtype), in_specs=[VMEM], out_specs=VMEM,
        scratch_shapes=[pltpu.SemaphoreType.DMA((7,)), pltpu.SemaphoreType.DMA((7,)), pltpu.SemaphoreType.DMA],
        compiler_params=pltpu.CompilerParams(vmem_limit_bytes=VMEM_LIMIT), name=name,
    )(xs)


class ShardGather:
    def __init__(self, shapes, splits):
        self.shapes, self.splits, self.n = shapes, splits, len(shapes)

    def scratch(self):
        n = self.n
        return [pltpu.SemaphoreType.DMA((n, 6)), pltpu.SemaphoreType.DMA((n, 6)), pltpu.SemaphoreType.DMA((n,))]

    def out_shapes(self, dtype):
        return [SDS((N_SHARD,) + tuple(s), dtype) for s in self.shapes]

    def _half(self, ref, k, cc):
        lo, hi = (0, self.splits[k]) if cc == 0 else (self.splits[k], self.shapes[k][0])
        return ref.at[pl.ds(lo, hi - lo)]

    def _phase(self, w_refs, o_refs, sems, finish):
        send_sems, recv_sems, local_sems = sems
        x, y, c, chips = _place()
        sibling = (x, y, 1 - c)
        me_s = 2 * x + y

        def rcopy(src, dst, k, s, to):
            return pltpu.make_async_remote_copy(src_ref=src, dst_ref=dst, send_sem=send_sems.at[k, s],
                                                recv_sem=recv_sems.at[k, s], device_id=to, device_id_type=MESH)

        for cc in (0, 1):
            @pl.when(c == cc)
            def _():
                local = [pltpu.make_async_copy(w_refs[k], o_refs[k].at[me_s], local_sems.at[k]) for k in range(self.n)]
                first = [rcopy(self._half(w_refs[k], k, cc), self._half(o_refs[k].at[me_s], k, cc), k, j, (*chip, c))
                         for k in range(self.n) for j, chip in enumerate(chips)]
                if not finish:
                    for cp in local + first:
                        cp.start()
                    return
                passed = []
                for k in range(self.n):
                    for j, chip in enumerate(chips):
                        land = self._half(o_refs[k].at[2 * chip[0] + chip[1]], k, cc)
                        rcopy(land, land, k, j, (*chip, c)).wait_recv()
                        f = rcopy(land, land, k, 3 + j, sibling)
                        f.start()
                        passed.append(f)
                for k in range(self.n):
                    for j, chip in enumerate(chips):
                        other = self._half(o_refs[k].at[2 * chip[0] + chip[1]], k, 1 - cc)
                        rcopy(other, other, k, 3 + j, sibling).wait_recv()
                for s in first + passed:
                    s.wait_send()
                for cp in local:
                    cp.wait()

    def start(self, w_refs, o_refs, sems):
        self._phase(w_refs, o_refs, sems, False)

    def finish(self, w_refs, o_refs, sems):
        self._phase(w_refs, o_refs, sems, True)


def all_gather_shards(ws, splits, name):
    n = len(ws)
    plan = ShardGather([w.shape for w in ws], splits)

    def body(*refs):
        plan.start(refs[:n], refs[n:2 * n], refs[2 * n:])
        plan.finish(refs[:n], refs[n:2 * n], refs[2 * n:])

    return pl.pallas_call(
        body, out_shape=plan.out_shapes(ws[0].dtype), in_specs=[HBM] * n, out_specs=[HBM] * n,
        scratch_shapes=plan.scratch(), name=name,
    )(*ws)


def sibling_send_half(gs, name):
    n = len(gs)

    def body(*refs):
        g_refs, o_refs = refs[:n], refs[n:2 * n]
        send_sems, recv_sems = refs[2 * n:]
        x, y, c, _ = _place()
        cps = []
        for k in range(n):
            hr = gs[k].shape[1] // 2
            src = g_refs[k].at[:, pl.ds(pl.multiple_of((1 - c) * hr, 8), hr)]
            cp = pltpu.make_async_remote_copy(src_ref=src, dst_ref=o_refs[k], send_sem=send_sems.at[k], recv_sem=recv_sems.at[k],
                                              device_id=(x, y, 1 - c), device_id_type=MESH)
            cp.start()
            cps.append(cp)
        for cp in cps:
            cp.wait()

    return pl.pallas_call(
        body, out_shape=[SDS((N_SHARD, g.shape[1] // 2, g.shape[2]), g.dtype) for g in gs], in_specs=[HBM] * n, out_specs=[HBM] * n,
        scratch_shapes=[pltpu.SemaphoreType.DMA((n,)), pltpu.SemaphoreType.DMA((n,))], name=name,
    )(*gs)


def _scatter_copies(h_refs, o_refs, send_sems, recv_sems):
    _, _, c, chips = _place()
    return [pltpu.make_async_remote_copy(
        src_ref=h_refs[k].at[2 * chip[0] + chip[1]], dst_ref=o_refs[k].at[j], send_sem=send_sems.at[k, j],
        recv_sem=recv_sems.at[k, j], device_id=(*chip, c), device_id_type=MESH)
        for k in range(len(h_refs)) for j, chip in enumerate(chips)]


def chip_scatter(hs, name):
    n = len(hs)

    def body(*refs):
        cps = _scatter_copies(refs[:n], refs[n:2 * n], *refs[2 * n:])
        for cp in cps:
            cp.start()
        for cp in cps:
            cp.wait()

    return pl.pallas_call(
        body, out_shape=[SDS((3,) + h.shape[1:], h.dtype) for h in hs], in_specs=[HBM] * n, out_specs=[HBM] * n,
        scratch_shapes=[pltpu.SemaphoreType.DMA((n, 3)), pltpu.SemaphoreType.DMA((n, 3))], name=name,
    )(*hs)


def sibling_swap(ghs, name):
    n = len(ghs)

    def body(*refs):
        g_refs, o_refs = refs[:n], refs[n:2 * n]
        send_sems, recv_sems = refs[2 * n:]
        x, y, c, _ = _place()
        cps = []
        for k in range(n):
            cp = pltpu.make_async_remote_copy(src_ref=g_refs[k], dst_ref=o_refs[k], send_sem=send_sems.at[k],
                                              recv_sem=recv_sems.at[k], device_id=(x, y, 1 - c), device_id_type=MESH)
            cp.start()
            cps.append(cp)
        for cp in cps:
            cp.wait()

    return pl.pallas_call(
        body, out_shape=[SDS(g.shape, g.dtype) for g in ghs], in_specs=[HBM] * n, out_specs=[HBM] * n,
        scratch_shapes=[pltpu.SemaphoreType.DMA((n,)), pltpu.SemaphoreType.DMA((n,))], name=name,
    )(*ghs)


def pair_sum(g, ra, cidx, name):
    _, r, cols = g.shape
    hr = r // 2

    def body(c_ref, g_ref, a_ref, o_ref):
        o_ref[...] = (g_ref[...] + a_ref[...]).astype(BF16)

    return pl.pallas_call(
        body,
        grid_spec=pltpu.PrefetchScalarGridSpec(
            num_scalar_prefetch=1, grid=(N_SHARD,),
            in_specs=[pl.BlockSpec((1, hr, cols), lambda s, c_ref: (s, c_ref[0], 0)),
                      pl.BlockSpec((1, hr, cols), lambda s, c_ref: (s, 0, 0))],
            out_specs=pl.BlockSpec((1, hr, cols), lambda s, c_ref: (s, 0, 0))),
        out_shape=SDS((N_SHARD, hr, cols), BF16), compiler_params=_cp("arbitrary"), name=name,
    )(cidx, g, ra)


def chip_sum(h, rb, sidx, name):
    _, hr, cols = h.shape

    def body(s_ref, h_ref, r_ref, o_ref):
        o_ref[...] = ((h_ref[0].astype(F32) + r_ref[0].astype(F32)) + r_ref[1].astype(F32)) + r_ref[2].astype(F32)

    return pl.pallas_call(
        body,
        grid_spec=pltpu.PrefetchScalarGridSpec(
            num_scalar_prefetch=1, grid=(1,),
            in_specs=[pl.BlockSpec((1, hr, cols), lambda i, s_ref: (s_ref[0], 0, 0)),
                      pl.BlockSpec((3, hr, cols), lambda i, s_ref: (0, 0, 0))],
            out_specs=pl.BlockSpec((hr, cols), lambda i, s_ref: (0, 0))),
        out_shape=SDS((hr, cols), F32), compiler_params=_cp("arbitrary"), name=name,
    )(sidx, h, rb)


def _shard_cols(g, n_valid):
    r = g.shape[0]
    return g[:, :n_valid].reshape(r, N_SHARD, n_valid // N_SHARD).transpose(1, 0, 2)


def _unshard_cols(o, pad_to):
    _, r, n = o.shape
    full = o.transpose(1, 0, 2).reshape(r, N_SHARD * n)
    return jnp.pad(full, ((0, 0), (0, pad_to - N_SHARD * n)))


def _rows_of_tiles(t):
    B, H, S = t.shape
    return t.reshape(B, H, S // FT, 1, FT)


def mixer_fwd(x1, mod3, g_pre, w_main, w_f, b_forget_pad, goa, gob, w_out, g_post, tabs, nb, gather=None):
    hmix, pa, pb, flog = mixer_proj(x1, mod3, g_pre, w_main, w_f, *tabs, name="mixer_proj")
    out_a, lse_a = band_fwd(pa, name="band_fwd")
    F = forget_cumsum(flog.reshape(nb, SEQ, LANE), b_forget_pad, name="forget_cumsum")
    Fh = F[:, :, :NH].transpose(0, 2, 1)
    fblk = Fh.reshape(nb, NH, SEQ // FB, 1, FB)
    frow = _rows_of_tiles(Fh)
    (out_b, lse_b), gathered = fox_fwd(pb, fblk, frow, name="fox_fwd", gather=gather)
    x2, merged, y0m = mixer_out_fwd(out_a, out_b, goa, gob, w_out, g_post, x1, mod3, name="mixer_out_fwd")
    res = dict(hmix=hmix, flog=flog, pa=pa, pb=pb, out_a=out_a, lse_a=lse_a, fblk=fblk, frow=frow, out_b=out_b,
               lrow=_rows_of_tiles(lse_b.reshape(nb, NH, SEQ)), merged=merged, y0m=y0m)
    return x2, res, gathered


def mixer_bwd(dx2, x1, mod3, g_pre, w_main, w_f, b_forget_pad, goa, gob, w_out, g_post, tabs, res, nb):
    T = nb * SEQ
    dy0m, doa, dob, dmgate, dg_post, dgoa, dgob, dvec_b = mixer_out_bwd(
        dx2, res["y0m"], mod3, g_post, w_out, res["out_a"], res["out_b"], goa, gob, name="mixer_out_bwd")
    dqa, dka, dva = band_bwd(res["pa"], doa, res["out_a"], res["lse_a"], *tabs, name="band_bwd")
    drow = _rows_of_tiles(dvec_b[:, :NH].reshape(nb, SEQ, NH).transpose(0, 2, 1))
    dqb, dkb, dvb, dfq, dfk = fox_bwd(res["pb"], dob, res["lrow"], drow, res["fblk"], res["frow"], name="fox_bwd")
    dF = (dfq.reshape(nb, NH, SEQ) + dfk.reshape(nb, NH, SEQ)).transpose(0, 2, 1)
    dF = jnp.pad(dF, ((0, 0), (0, 0), (0, LANE - NH)))
    dflog, dbf = forget_cumsum_bwd(dF, res["flog"].reshape(nb, SEQ, LANE), b_forget_pad, name="forget_cumsum_bwd")
    dflog = dflog.reshape(T, LANE)
    dps = (dqa, dka, dva, dqb, dkb, dvb)
    dx1, dmod2, dg_pre = mixer_proj_bwd(dps, dflog, dx2, x1, mod3, g_pre, w_main, w_f, name="mixer_proj_bwd")
    g_main = matmul_tn_cols(res["hmix"], dps, 1024, name="grad_w_in")
    g_f = matmul_tn(res["hmix"], dflog.astype(BF16), D, LANE, 1024, name="grad_w_forget")
    g_out = matmul_tn(res["merged"], dy0m, D, D, 1024, name="grad_w_out")
    dmod3 = jnp.concatenate([dmod2, dmgate], axis=1)
    return dx1, dmod3, dict(g_pre=dg_pre, g_post=dg_post, goa=dgoa, gob=dgob, b_forget=dbf[:, :NH],
                            w_in=jnp.concatenate([g_main, g_f[:, :NH]], axis=1), w_out=g_out)


def ffn_grads(h, dy0, act, dgate, dup, pre, reduce=None):
    g_gate = matmul_tn(h, dgate, D, DFF_PAD, 1024, name=pre + "_grad_gate")
    if reduce is None:
        g_up = matmul_tn(h, dup, D, DFF_PAD, 1024, name=pre + "_grad_up")
        g_down = matmul_tn(act, dy0, FF_TN, D, 1024, name=pre + "_grad_down")
        return (g_gate, g_up, g_down), {}
    hs_gate = reduce("gate", g_gate)
    g_up, rb_gate = matmul_tn(h, dup, D, DFF_PAD, 1024, name=pre + "_grad_up", scatter=hs_gate)
    hs_up = reduce("up", g_up)
    g_down, rb_up = matmul_tn(act, dy0, FF_TN, D, 1024, name=pre + "_grad_down", scatter=hs_up)
    return (g_gate, g_up, g_down), {"gate": (hs_gate[0], rb_gate[0]), "up": (hs_up[0], rb_up[0])}


def local_step(x0, tgt, pos_col, mod, wfull, p, late_weights=None, last_weights=None, early_grads=None, last_reduce=None):
    T = x0.shape[0]
    nb = T // SEQ
    mod_ff1, mod_mix, mod_ff2 = mod[:, 0:3], mod[:, 3:6], mod[:, 6:9]
    tabs = rope_tables(pos_col, name="rope_tables")
    bf_pad = jnp.pad(p["b_forget"], ((0, 0), (0, LANE - NH)))

    (x1, h1, gate1, up1, y01), gathered = ffn_fwd(
        x0, mod_ff1, p["g_pre_ff1"], p["g_post_ff1"], wfull["w_ff1_gate"], wfull["w_ff1_up"], wfull["w_ff1_down"], 0.5,
        name="ff1_fwd", gather=None if late_weights is None else late_weights[:2])
    if late_weights is not None:
        wfull = {**wfull, **late_weights[2](gathered)}
    x2, res, gathered = mixer_fwd(x1, mod_mix, p["g_pre_mix"], wfull["w_main"], wfull["w_f"], bf_pad, p["g_out_a"],
                                  p["g_out_b"], wfull["w_out"], p["g_post_mix"], tabs, nb,
                                  gather=None if last_weights is None else last_weights[:2])
    if last_weights is not None:
        wfull = {**wfull, **last_weights[2](gathered)}
    (x3, h2, gate2, up2, y02), _ = ffn_fwd(x2, mod_ff2, p["g_pre_ff2"], p["g_post_ff2"], wfull["w_ff2_gate"],
                                           wfull["w_ff2_up"], wfull["w_ff2_down"], 0.5, name="ff2_fwd")

    (dx2, dy02, act2, dgate2, dup2, dmod_ff2, dgpre2, dgpost2), (loss_part,) = ffn_bwd(
        x3, x2, y02, mod_ff2, p["g_pre_ff2"], p["g_post_ff2"], gate2, up2, wfull["w_ff2_gate"], wfull["w_ff2_up"],
        wfull["w_ff2_down"], 0.5, name="ff2_bwd", target=tgt)
    gw = {}
    (gw["w_ff2_gate"], gw["w_ff2_up"], gw["w_ff2_down"]), _ = ffn_grads(h2, dy02, act2, dgate2, dup2, "ff2")
    dx1, dmod_mix, gmix = mixer_bwd(dx2, x1, mod_mix, p["g_pre_mix"], wfull["w_main"], wfull["w_f"], bf_pad, p["g_out_a"],
                                    p["g_out_b"], wfull["w_out"], p["g_post_mix"], tabs, res, nb)
    gw["w_in"], gw["w_out"] = gmix["w_in"], gmix["w_out"]
    (dx0, dy01, act1, dgate1, dup1, dmod_ff1, dgpre1, dgpost1), scattered = ffn_bwd(
        dx1, x0, y01, mod_ff1, p["g_pre_ff1"], p["g_post_ff1"], gate1, up1, wfull["w_ff1_gate"], wfull["w_ff1_up"],
        wfull["w_ff1_down"], 0.5, name="ff1_bwd", scatter=None if early_grads is None else early_grads(gw))
    (gw["w_ff1_gate"], gw["w_ff1_up"], gw["w_ff1_down"]), chained = ffn_grads(h1, dy01, act1, dgate1, dup1, "ff1", last_reduce)
    dmod = jnp.concatenate([dmod_ff1, dmod_mix, dmod_ff2], axis=1).reshape(nb, 9 * D)
    small = dict(g_pre_ff1=dgpre1, g_post_ff1=dgpost1, g_pre_mix=gmix["g_pre"], g_post_mix=gmix["g_post"], g_pre_ff2=dgpre2,
                 g_post_ff2=dgpost2, g_out_a=gmix["goa"], g_out_b=gmix["gob"], b_forget=gmix["b_forget"])
    return loss_part, dx0, dmod, gw, small, scattered, chained


def kernel(x, c, positions, w_ada, b_ada, g_pre_ff1, g_post_ff1, w_ff1_gate, w_ff1_up, w_ff1_down, g_pre_mix, g_post_mix, w_in, b_forget, g_out_a, g_out_b, w_out, g_pre_ff2, g_post_ff2, w_ff2_gate, w_ff2_up, w_ff2_down, loss_target, m_w_ada, m_b_ada, m_g_pre_ff1, m_g_post_ff1, m_w_ff1_gate, m_w_ff1_up, m_w_ff1_down, m_g_pre_mix, m_g_post_mix, m_w_in, m_b_forget, m_g_out_a, m_g_out_b, m_w_out, m_g_pre_ff2, m_g_post_ff2, m_w_ff2_gate, m_w_ff2_up, m_w_ff2_down, v_w_ada, v_b_ada, v_g_pre_ff1, v_g_post_ff1, v_w_ff1_gate, v_w_ff1_up, v_w_ff1_down, v_g_pre_mix, v_g_post_mix, v_w_in, v_b_forget, v_g_out_a, v_g_out_b, v_w_out, v_g_pre_ff2, v_g_post_ff2, v_w_ff2_gate, v_w_ff2_up, v_w_ff2_down):
    args = dict(locals())
    nb = x.shape[0]
    T = nb * SEQ
    ax, ay, ac = lax.axis_index("x"), lax.axis_index("y"), lax.axis_index("c")
    shard = 2 * ax + ay
    cidx = jnp.reshape(ac, (1,)).astype(jnp.int32)
    sidx = jnp.reshape(shard, (1,)).astype(jnp.int32)

    big = ["w_ff1_gate", "w_ff1_up", "w_ff1_down", "w_in", "w_out", "w_ff2_gate", "w_ff2_up", "w_ff2_down"]
    vecs = ["g_pre_ff1", "g_post_ff1", "g_pre_mix", "g_post_mix", "g_pre_ff2", "g_post_ff2"]

    first, late = big[:3], big[3:]
    splits = dict(zip(big, [512, 512, 352, 512, 128, 512, 512, 352]))

    def assemble(names, gathered):
        out = {}
        for n, o in zip(names, gathered):
            if n.endswith("gate") or n.endswith("up"):
                out[n] = _unshard_cols(o, DFF_PAD)
            elif n.endswith("down"):
                out[n] = jnp.pad(o.reshape(DFF, D), ((0, DFF_PAD - DFF), (0, 0)))
            elif n == "w_in":
                full = _unshard_cols(o, IN_COLS)
                out["w_main"] = full[:, :IN_MAIN]
                out["w_f"] = jnp.pad(full[:, IN_MAIN:], ((0, 0), (0, LANE - NH)))
            else:
                out[n] = o.reshape(D, D)
        return out

    wfull = assemble(first, all_gather_shards([args[n][0].astype(BF16) for n in first], [splits[n] for n in first],
                                              name="all_gather_weights"))
    def gather_plan(names):
        return ([args[n][0].astype(BF16) for n in names], [splits[n] for n in names], functools.partial(assemble, names))

    late_weights, last_weights = gather_plan(late[:2]), gather_plan(late[2:])

    ncol = w_ada.shape[2]
    c_all = all_gather8(c, name="all_gather_c").reshape(N_DEV * nb, D)
    b_loc = lax.dynamic_slice(b_ada, (0, shard * ncol), (1, ncol))
    mod_loc = ada_fwd(c_all, w_ada[0], b_loc, name="ada_fwd")
    mod_g = all_gather8(mod_loc, name="all_gather_mod")
    row0 = (4 * ax + 2 * ay + ac) * nb
    mod_rows = lax.dynamic_slice(mod_g, (0, row0, 0), (N_DEV, nb, ncol))
    mod = jnp.concatenate([mod_rows[2 * s] for s in range(N_SHARD)], axis=-1).reshape(nb, 9, D)

    small_in = dict(g_pre_ff1=g_pre_ff1, g_post_ff1=g_post_ff1, g_pre_mix=g_pre_mix, g_post_mix=g_post_mix, g_pre_ff2=g_pre_ff2,
                    g_post_ff2=g_post_ff2, g_out_a=g_out_a, g_out_b=g_out_b, b_forget=b_forget)
    def shard_blocked(n, g):
        if n.endswith("gate") or n.endswith("up"):
            return _shard_cols(g, DFF)
        if n.endswith("down"):
            return g[:DFF].reshape(N_SHARD, DFF // N_SHARD, D)
        if n == "w_in":
            return _shard_cols(g, IN_COLS)
        return g.reshape(N_SHARD, D // N_SHARD, D)

    def chip_sums(names, gw, tag):
        gsb = [shard_blocked(n, gw[n]) for n in names]
        ras = sibling_send_half(gsb, name="grad_sibling_send_" + tag)
        return [pair_sum(g, ra, cidx, name=f"grad_pair_sum_{n}") for n, g, ra in zip(names, gsb, ras)]

    hs = {}

    def early_grads(gw):
        hs.update(zip(late, chip_sums(late, gw, "late")))
        return [hs[n] for n in late]

    def last_reduce(which, g):
        return chip_sums(["w_ff1_" + which], {"w_ff1_" + which: g}, which)

    loss_part, dx0, dmod, gw, small, rbs_late, chained = local_step(
        x.reshape(T, D), loss_target.reshape(T, D), positions.reshape(T, 1), mod, wfull, small_in, late_weights, last_weights,
        early_grads, last_reduce)

    dmod_all = all_gather8(dmod, name="all_gather_dmod").reshape(N_DEV * nb, 9 * D)
    dmod_loc = lax.dynamic_slice(dmod_all, (0, shard * ncol), (N_DEV * nb, ncol))
    g_w_ada = ada_bwd(c_all, dmod_loc, name="ada_bwd")

    rbs = dict(zip(late, rbs_late))
    for which, (h, rb) in chained.items():
        hs["w_ff1_" + which], rbs["w_ff1_" + which] = h, rb
    hs["w_ff1_down"] = chip_sums(["w_ff1_down"], gw, "down")[0]
    rbs["w_ff1_down"] = chip_scatter([hs["w_ff1_down"]], name="grad_chip_scatter")[0]
    ghs = [chip_sum(hs[n], rbs[n], sidx, name=f"grad_chip_sum_{n}") for n in big]
    theirs = sibling_swap(ghs, name="grad_sibling_swap")

    row6 = jnp.concatenate([small["g_out_a"], small["g_out_b"]], axis=1)
    row7 = jnp.concatenate([small["b_forget"], loss_part[0:1, 0:1], jnp.zeros((1, D - NH - 1), F32)], axis=1)
    pack = jnp.concatenate([small[n] for n in vecs] + [row6, row7], axis=0)
    packed = all_gather8(pack, name="all_gather_small").reshape(N_DEV, 8 * D)

    def pack_state(pre):
        r6 = jnp.concatenate([args[pre + "g_out_a"], args[pre + "g_out_b"]], axis=1)
        r7 = jnp.pad(args[pre + "b_forget"], ((0, 0), (0, D - NH)))
        return jnp.concatenate([args[pre + n] for n in vecs] + [r6, r7], axis=0).reshape(1, 8 * D)

    sg, sd, sm, sv = (t.reshape(8, D) for t in vec_adam(packed, pack_state(""), pack_state("m_"), pack_state("v_"), name="adam_small"))

    def unpack(t):
        out = {n: t[i:i + 1] for i, n in enumerate(vecs)}
        out["g_out_a"], out["g_out_b"], out["b_forget"] = t[6:7, :WG], t[6:7, WG:], t[7:8, :NH]
        return out

    outs = dict(grad=unpack(sg), delta=unpack(sd), new_m=unpack(sm), new_v=unpack(sv))
    loss = sg[7, NH]
    outs["grad"]["b_ada"], outs["delta"]["b_ada"], outs["new_m"]["b_ada"], outs["new_v"]["b_ada"] = vec_adam(
        dmod_all, b_ada, m_b_ada, v_b_ada, name="adam_b_ada")

    for n, mine, other in zip(big, ghs, theirs):
        tr = 128 if mine.shape[0] % 128 == 0 else mine.shape[0]
        outs["grad"][n], outs["delta"][n], outs["new_m"][n], outs["new_v"][n] = adam_update_halves(
            args[n], mine, other, args["m_" + n], args["v_" + n], cidx, tr, name="adam_" + n)
    outs["delta"]["w_ada"], outs["new_m"]["w_ada"], outs["new_v"]["w_ada"] = adam_update(
        w_ada, g_w_ada, m_w_ada, v_w_ada, 128, name="adam_w_ada")
    outs["grad"]["w_ada"] = g_w_ada[None]

    order = ["w_ada", "b_ada", "g_pre_ff1", "g_post_ff1", "w_ff1_gate", "w_ff1_up", "w_ff1_down", "g_pre_mix", "g_post_mix", "w_in",
             "b_forget", "g_out_a", "g_out_b", "w_out", "g_pre_ff2", "g_post_ff2", "w_ff2_gate", "w_ff2_up", "w_ff2_down"]
    result = [loss, dx0.reshape(nb, SEQ, D)]
    for kind in ("grad", "delta", "new_m", "new_v"):
        result += [outs[kind][n] for n in order]
    return tuple(result)
```

```python
import functools
import math

import jax
import jax.numpy as jnp
from jax import lax
from jax.experimental import pallas as pl
from jax.experimental.pallas import tpu as pltpu

D = 1024
SEQ = 2048
HD = 64
NH = 8
WG = NH * HD
DFF = 2752
DFF_PAD = 2816
IN_MAIN = 6 * WG
IN_COLS = IN_MAIN + NH
N_SHARD = 4
N_DEV = 8
LANE = 128
QB = 128
FB = 256
FT = 512
FOX_QB = 256
FOX_PAIRS = 2
FOX_PAIRS_BWD = 2
BAND_UNROLL = 4
BAND_UNROLL_BWD = 4
PATTERNS = ((1, 16), (4, 4), (16, 1))
ROPE_THETA = 500000.0
EPS = 1e-6
NEG = -1e30
ATTN_SCALE = HD ** -0.5
TM = 512
TM_FFN = 512
TM_BWD = 256
VMEM_LIMIT = 56 * 1024 * 1024

ADAM_LR, ADAM_B1, ADAM_B2, ADAM_EPS, ADAM_WD, ADAM_STEP = 0.001, 0.9, 0.999, 1e-08, 0.01, 10

F32 = jnp.float32
BF16 = jnp.bfloat16
MESH = pl.DeviceIdType.MESH
SDS = jax.ShapeDtypeStruct


def _cp(*sem):
    return pltpu.CompilerParams(dimension_semantics=sem, vmem_limit_bytes=VMEM_LIMIT)


def _dot(a, b):
    return jnp.dot(a, b, preferred_element_type=F32)


def _dot_nt(a, b):
    return lax.dot_general(a, b, (((1,), (1,)), ((), ())), preferred_element_type=F32)


def _dot_tn(a, b):
    return lax.dot_general(a, b, (((0,), (0,)), ((), ())), preferred_element_type=F32)


def _rms(xf):
    return lax.rsqrt(jnp.mean(xf * xf, axis=-1, keepdims=True) + EPS)


def _norm_mod_bwd(dh, xf, g, scale):
    r = _rms(xf)
    xh = xf * r
    dsh = jnp.sum(dh, axis=0, keepdims=True)
    dsc = jnp.sum(dh * (xh * g), axis=0, keepdims=True)
    dn = dh * (1.0 + scale)
    dg = jnp.sum(dn * xh, axis=0, keepdims=True)
    dxh = dn * g
    dx = r * (dxh - xh * jnp.mean(dxh * xh, axis=-1, keepdims=True))
    return dx, dsh, dsc, dg


def _post_bwd(dxo, y0, g, mgate, gs):
    r = _rms(y0)
    yh = y0 * r
    dmg = gs * jnp.sum(dxo * (yh * g), axis=0, keepdims=True)
    dy = (gs * mgate) * dxo
    dg = jnp.sum(dy * yh, axis=0, keepdims=True)
    dyh = dy * g
    dy0 = r * (dyh - yh * jnp.mean(dyh * yh, axis=-1, keepdims=True))
    return dy0, dmg, dg


def _mod_map(i, *_):
    return ((i * TM) // SEQ, 0, 0)


FF_TN = 1408
FF_TILES = ((0, 768), (768, 1536), (1536, 2304), (2304, 2816))


def _resident_scratch():
    return [pltpu.VMEM((D, DFF_PAD), BF16), pltpu.VMEM((D, DFF_PAD), BF16), pltpu.VMEM((DFF_PAD, D), BF16),
            pltpu.SemaphoreType.DMA((3,))]


def _load_resident(first_step, srcs, dsts, sems):
    @pl.when(first_step)
    def _():
        cps = [pltpu.make_async_copy(s, d, sems.at[k]) for k, (s, d) in enumerate(zip(srcs, dsts))]
        for cp in cps:
            cp.start()
        for cp in cps:
            cp.wait()


def ffn_fwd(x, mod3, g_pre, g_post, wg, wu, wd, gs, name, gather=None):
    T = x.shape[0]
    tm = TM_FFN
    ng = 0 if gather is None else len(gather[0])
    plan = None if gather is None else ShardGather([w.shape for w in gather[0]], gather[1])

    def body(*refs):
        x_ref, mod_ref, gpre_ref, gpost_ref = refs[:4]
        xo_ref, h_ref, gate_ref, up_ref, y0_ref = refs[7 + ng:12 + ng]
        wg_ref, wu_ref, wd_ref, wsem = refs[12 + 2 * ng:16 + 2 * ng]
        i = pl.program_id(0)
        if plan is not None:
            comm = (refs[7:7 + ng], refs[12 + ng:12 + 2 * ng], refs[16 + 2 * ng:])
            pl.when(i == 0)(lambda: plan.start(*comm))
        _load_resident(i == 0, refs[4:7], (wg_ref, wu_ref, wd_ref), wsem)

        xf = x_ref[...]
        hb = ((xf * _rms(xf) * gpre_ref[...]) * (1.0 + mod_ref[0, 1:2, :]) + mod_ref[0, 0:1, :]).astype(BF16)
        h_ref[...] = hb
        y0 = None
        for lo, hi in FF_TILES:
            gate = _dot(hb, wg_ref[:, lo:hi])
            up = _dot(hb, wu_ref[:, lo:hi])
            gate_ref[:, lo:hi] = gate.astype(BF16)
            up_ref[:, lo:hi] = up.astype(BF16)
            part = _dot((gate * jax.nn.sigmoid(gate) * up).astype(BF16), wd_ref[lo:hi, :])
            y0 = part if y0 is None else y0 + part
        y0_ref[...] = y0
        xo_ref[...] = xf + (gs * mod_ref[0, 2:3, :]) * (y0 * _rms(y0) * gpost_ref[...])

        if plan is not None:
            pl.when(i == T // tm - 1)(lambda: plan.finish(*comm))

    tok = pl.BlockSpec((tm, D), lambda i: (i, 0))
    vec = pl.BlockSpec((1, D), lambda i: (0, 0))
    hid = pl.BlockSpec((tm, DFF_PAD), lambda i: (i, 0))
    outs = pl.pallas_call(
        body, grid=(T // tm,),
        in_specs=[tok, pl.BlockSpec((1, 3, D), lambda i: ((i * tm) // SEQ, 0, 0)), vec, vec, HBM, HBM, HBM] + [HBM] * ng,
        out_specs=[tok, tok, hid, hid, tok] + [HBM] * ng,
        out_shape=[SDS((T, D), F32), SDS((T, D), BF16), SDS((T, DFF_PAD), BF16), SDS((T, DFF_PAD), BF16), SDS((T, D), F32)]
        + ([] if plan is None else plan.out_shapes(BF16)),
        scratch_shapes=_resident_scratch() + ([] if plan is None else plan.scratch()),
        compiler_params=_cp("arbitrary"), name=name,
    )(x, mod3, g_pre, g_post, wg, wu, wd, *([] if gather is None else gather[0]))
    return outs[:5], outs[5:]


def ffn_bwd(dxo, x, y0, mod3, g_pre, g_post, gate, up, wg, wu, wd, gs, name, scatter=None, target=None):
    assert scatter is None or target is None
    T = x.shape[0]
    nb = T // SEQ
    tm = TM_BWD
    tiles_per_seq = SEQ // tm
    ns = 0 if scatter is None else len(scatter)
    ne = ns + (target is not None)

    def body(*refs):
        dxo_ref, x_ref, y0_ref, mod_ref, gpre_ref, gpost_ref, gate_ref, up_ref = refs[:8]
        dx_ref, dy0_ref, act_ref, dgate_ref, dup_ref, dmod_ref, dgpre_ref, dgpost_ref = refs[11 + ne:19 + ne]
        wg_ref, wu_ref, wd_ref, wsem = refs[19 + 2 * ne:23 + 2 * ne]
        i = pl.program_id(0)
        _load_resident(i == 0, refs[8:11], (wg_ref, wu_ref, wd_ref), wsem)
        if ns:
            comm = (refs[11:11 + ns], refs[19 + ns:19 + 2 * ns], *refs[23 + 2 * ns:])

            @pl.when(i == 0)
            def _():
                for cp in _scatter_copies(*comm):
                    cp.start()

        @pl.when(i == 0)
        def _():
            dgpre_ref[...] = jnp.zeros_like(dgpre_ref)
            dgpost_ref[...] = jnp.zeros_like(dgpost_ref)

        @pl.when(i % tiles_per_seq == 0)
        def _():
            dmod_ref[...] = jnp.zeros_like(dmod_ref)

        dxo = dxo_ref[...]
        if target is not None:
            loss_ref = refs[19 + ne]

            @pl.when(i == 0)
            def _():
                loss_ref[...] = jnp.zeros_like(loss_ref)

            err = dxo - refs[11][...]
            loss_ref[...] += jnp.sum(err * err) * (0.5 / D)
            dxo = err * (1.0 / D)
        dy0, dmg, dg = _post_bwd(dxo, y0_ref[...], gpost_ref[...], mod_ref[0, 2:3, :], gs)
        dmod_ref[0, 2:3, :] += dmg
        dgpost_ref[...] += dg
        db = dy0.astype(BF16)
        dy0_ref[...] = db
        dh = None
        for lo, hi in FF_TILES:
            dact = _dot_nt(db, wd_ref[lo:hi, :])
            g = gate_ref[:, lo:hi].astype(F32)
            u = up_ref[:, lo:hi].astype(F32)
            sig = jax.nn.sigmoid(g)
            sl = g * sig
            dgate = (dact * u * (sig * (1.0 + g * (1.0 - sig)))).astype(BF16)
            dup = (dact * sl).astype(BF16)
            act_ref[:, lo:hi] = (sl * u).astype(BF16)
            dgate_ref[:, lo:hi] = dgate
            dup_ref[:, lo:hi] = dup
            part = _dot_nt(dgate, wg_ref[:, lo:hi]) + _dot_nt(dup, wu_ref[:, lo:hi])
            dh = part if dh is None else dh + part
        dx, dsh, dsc, dg = _norm_mod_bwd(dh, x_ref[...], gpre_ref[...], mod_ref[0, 1:2, :])
        dx_ref[...] = dxo + dx
        dmod_ref[0, 0:1, :] += dsh
        dmod_ref[0, 1:2, :] += dsc
        dgpre_ref[...] += dg

        if ns:
            @pl.when(i == T // tm - 1)
            def _():
                for cp in _scatter_copies(*comm):
                    cp.wait()

    tok = pl.BlockSpec((tm, D), lambda i: (i, 0))
    vec = pl.BlockSpec((1, D), lambda i: (0, 0))
    hid = pl.BlockSpec((tm, DFF_PAD), lambda i: (i, 0))
    modspec = pl.BlockSpec((1, 3, D), lambda i: ((i * tm) // SEQ, 0, 0))
    outs = pl.pallas_call(
        body, grid=(T // tm,),
        in_specs=[tok, tok, tok, modspec, vec, vec, hid, hid, HBM, HBM, HBM] + [HBM] * ns + [tok] * (ne - ns),
        out_specs=[tok, tok, hid, hid, hid, modspec, vec, vec] + [HBM] * ns
        + [pl.BlockSpec((8, LANE), lambda i: (0, 0))] * (ne - ns),
        out_shape=[SDS((T, D), F32), SDS((T, D), BF16), SDS((T, DFF_PAD), BF16), SDS((T, DFF_PAD), BF16),
                   SDS((T, DFF_PAD), BF16), SDS((nb, 3, D), F32), SDS((1, D), F32), SDS((1, D), F32)]
        + [SDS((3,) + h.shape[1:], h.dtype) for h in (scatter or [])] + [SDS((8, LANE), F32)] * (ne - ns),
        scratch_shapes=_resident_scratch()
        + ([pltpu.SemaphoreType.DMA((ns, 3)), pltpu.SemaphoreType.DMA((ns, 3))] if ns else []),
        compiler_params=_cp("arbitrary"), name=name,
    )(dxo, x, y0, mod3, g_pre, g_post, gate, up, wg, wu, wd, *(scatter or []), *([] if target is None else [target]))
    return outs[:8], outs[8:]


def matmul_tn(a, b, tm, tn, tk, name, scatter=None):
    T, M = a.shape
    N = b.shape[1]
    grid = (M // tm, N // tn, T // tk)
    ns = 0 if scatter is None else len(scatter)

    def body(*refs):
        a_ref, b_ref = refs[:2]
        o_ref = refs[2 + ns]
        ids = [pl.program_id(ax) for ax in range(3)]
        if ns:
            comm = (refs[2:2 + ns], refs[3 + ns:3 + 2 * ns], *refs[3 + 2 * ns:])

            @pl.when((ids[0] == 0) & (ids[1] == 0) & (ids[2] == 0))
            def _():
                for cp in _scatter_copies(*comm):
                    cp.start()

        @pl.when(ids[2] == 0)
        def _():
            o_ref[...] = jnp.zeros_like(o_ref)

        o_ref[...] += _dot_tn(a_ref[...], b_ref[...])

        if ns:
            @pl.when((ids[0] == grid[0] - 1) & (ids[1] == grid[1] - 1) & (ids[2] == grid[2] - 1))
            def _():
                for cp in _scatter_copies(*comm):
                    cp.wait()

    outs = pl.pallas_call(
        body, grid=grid,
        in_specs=[pl.BlockSpec((tk, tm), lambda i, j, k: (k, i)), pl.BlockSpec((tk, tn), lambda i, j, k: (k, j))] + [HBM] * ns,
        out_specs=[pl.BlockSpec((tm, tn), lambda i, j, k: (i, j))] + [HBM] * ns,
        out_shape=[SDS((M, N), F32)] + [SDS((3,) + h.shape[1:], h.dtype) for h in (scatter or [])],
        scratch_shapes=[pltpu.SemaphoreType.DMA((ns, 3)), pltpu.SemaphoreType.DMA((ns, 3))] if ns else [],
        compiler_params=_cp("arbitrary", "arbitrary", "arbitrary"), name=name,
    )(a, b, *(scatter or []))
    return outs[0] if scatter is None else (outs[0], outs[1:])


def matmul_tn_cols(a, bs, tk, name):
    T, M = a.shape
    n = bs[0].shape[1]
    ng = len(bs)

    def body(*refs):
        a_ref, b_refs, o_ref = refs[0], refs[1:1 + ng], refs[1 + ng]

        @pl.when(pl.program_id(0) == 0)
        def _():
            o_ref[...] = jnp.zeros_like(o_ref)

        av = a_ref[...]
        for g, b_ref in enumerate(b_refs):
            o_ref[:, g * n:(g + 1) * n] += _dot_tn(av, b_ref[...])

    return pl.pallas_call(
        body, grid=(T // tk,),
        in_specs=[pl.BlockSpec((tk, M), lambda k: (k, 0))] + [pl.BlockSpec((tk, n), lambda k: (k, 0))] * ng,
        out_specs=pl.BlockSpec((M, ng * n), lambda k: (0, 0)), out_shape=SDS((M, ng * n), F32),
        compiler_params=_cp("arbitrary"), name=name,
    )(a, *bs)


def rope_tables(pos_col, name):
    T = pos_col.shape[0]
    tm = 1024

    def body(p_ref, c_ref, s1_ref, s2_ref):
        lane = lax.broadcasted_iota(jnp.int32, (1, LANE), 1)
        l64 = lane % HD
        inv_freq = jnp.exp((l64 % 8).astype(F32) * (-math.log(ROPE_THETA) / 8.0))
        ang = p_ref[...].astype(F32) * inv_freq
        cs = jnp.cos(ang)
        sn = jnp.sin(ang)
        c_ref[...] = jnp.where(l64 < 16, cs, 1.0)
        s1_ref[...] = jnp.where(l64 < 8, -sn, 0.0)
        s2_ref[...] = jnp.where((l64 >= 8) & (l64 < 16), sn, 0.0)

    tab = pl.BlockSpec((tm, LANE), lambda i: (i, 0))
    return pl.pallas_call(
        body, grid=(T // tm,), in_specs=[pl.BlockSpec((tm, 1), lambda i: (i, 0))], out_specs=[tab, tab, tab],
        out_shape=[SDS((T, LANE), F32)] * 3, compiler_params=_cp("arbitrary"), name=name,
    )(pos_col)


def mixer_proj(x, mod3, g_pre, w_main, w_f, rc, rs1, rs2, name):
    T = x.shape[0]

    def body(x_ref, mod_ref, g_ref, w_ref, wf_ref, c_ref, s1_ref, s2_ref, h_ref, pa_ref, pb_ref, f_ref):
        xf = x_ref[...]
        h = (xf * _rms(xf) * g_ref[...]) * (1.0 + mod_ref[0, 1:2, :]) + mod_ref[0, 0:1, :]
        hb = h.astype(BF16)
        h_ref[...] = hb
        f_ref[...] = _dot(hb, wf_ref[...])
        c, s1, s2 = c_ref[...], s1_ref[...], s2_ref[...]
        for grp in range(2):
            pr = _dot(hb, w_ref[:, grp * WG:(grp + 1) * WG])
            for k in range(WG // LANE):
                t = pr[:, k * LANE:(k + 1) * LANE]
                pa_ref[:, grp * WG + k * LANE:grp * WG + (k + 1) * LANE] = (
                    t * c + pltpu.roll(t, LANE - 8, 1) * s1 + pltpu.roll(t, 8, 1) * s2)
        pa_ref[:, 2 * WG:3 * WG] = _dot(hb, w_ref[:, 2 * WG:3 * WG])
        for grp in range(3):
            pb_ref[:, grp * WG:(grp + 1) * WG] = _dot(hb, w_ref[:, (3 + grp) * WG:(4 + grp) * WG]).astype(BF16)

    tok = pl.BlockSpec((TM, D), lambda i: (i, 0))
    vec = pl.BlockSpec((1, D), lambda i: (0, 0))
    tab = pl.BlockSpec((TM, LANE), lambda i: (i, 0))
    grp3 = pl.BlockSpec((TM, 3 * WG), lambda i: (i, 0))
    return pl.pallas_call(
        body, grid=(T // TM,),
        in_specs=[tok, pl.BlockSpec((1, 3, D), _mod_map), vec, pl.BlockSpec((D, IN_MAIN), lambda i: (0, 0)),
                  pl.BlockSpec((D, LANE), lambda i: (0, 0)), tab, tab, tab],
        out_specs=[tok, grp3, grp3, tab],
        out_shape=[SDS((T, D), BF16), SDS((T, 3 * WG), F32), SDS((T, 3 * WG), BF16), SDS((T, LANE), F32)],
        compiler_params=_cp("arbitrary"), name=name,
    )(x, mod3, g_pre, w_main, w_f, rc, rs1, rs2)


def _head_lanes():
    return lax.broadcasted_iota(jnp.int32, (1, LANE), 1) < HD


def _pair(m0, a, b):
    return jnp.where(m0, a, b)


def _band_rows(i, d, nbc):
    if nbc == 1:
        return i, i, 0
    r, mb = i // nbc, i % nbc
    return r + mb * (QB * d), r + jnp.maximum(mb - 1, 0) * (QB * d), jnp.where(mb > 0, QB, 0)


def _rows(start, size, d):
    return pl.ds(pl.multiple_of(start, QB), size) if d == 1 else pl.ds(start, size, stride=d)


def _band_valid(span, off):
    rq = lax.broadcasted_iota(jnp.int32, (QB, span), 0)
    rel = lax.broadcasted_iota(jnp.int32, (QB, span), 1) - off
    return (rel <= rq) & (rel >= rq - QB)


def band_fwd(pa, name):
    T = pa.shape[0]
    B = T // SEQ
    NP = WG // LANE

    def body(q_ref, k_ref, v_ref, out_ref, lse_ref, o_s, l_s):
        m0 = _head_lanes()
        for pidx, (d, nbc) in enumerate(PATTERNS):
            span = QB if nbc == 1 else 2 * QB

            def blk(it, carry, pidx=pidx, d=d, nbc=nbc, span=span):
                ld = []
                for u in range(BAND_UNROLL):
                    qs, ks, off = _band_rows(it * BAND_UNROLL + u, d, nbc)
                    q = q_ref[_rows(qs, QB, d), :] * ATTN_SCALE
                    ld.append((qs, q, k_ref[_rows(ks, span, d), :].astype(BF16), v_ref[_rows(ks, span, d), :].astype(BF16),
                               _band_valid(span, off)))
                ss = [[jnp.where(valid, _dot_nt(jnp.where(mh, q, 0.0).astype(BF16), k), NEG) for mh in (m0, jnp.logical_not(m0))]
                      for _, q, k, _, valid in ld]
                ps = []
                for pair in ss:
                    row = []
                    for s in pair:
                        m = jnp.max(s, axis=-1, keepdims=True)
                        p = jnp.exp(s - m)
                        row.append((p.astype(BF16), jnp.sum(p, axis=-1, keepdims=True), m))
                    ps.append(row)
                pv = [[_dot(p, ld[u][3]) for p, _, _ in ps[u]] for u in range(BAND_UNROLL)]
                for u in range(BAND_UNROLL):
                    rows = _rows(ld[u][0], QB, d)
                    (_, l0, mx0), (_, l1, mx1) = ps[u]
                    o_s[pidx, rows, :] = _pair(m0, pv[u][0] / l0, pv[u][1] / l1)
                    l_s[pidx, rows, :] = _pair(m0, mx0 + jnp.log(l0), mx1 + jnp.log(l1))
                return carry

            lax.fori_loop(0, SEQ // QB // BAND_UNROLL, blk, 0)
        for c in range(SEQ // FB):
            sl = slice(c * FB, (c + 1) * FB)
            a, b, e = l_s[0, sl, :], l_s[1, sl, :], l_s[2, sl, :]
            m = jnp.maximum(jnp.maximum(a, b), e)
            L = m + jnp.log(jnp.exp(a - m) + jnp.exp(b - m) + jnp.exp(e - m))
            out_ref[sl, :] = jnp.exp(a - L) * o_s[0, sl, :] + jnp.exp(b - L) * o_s[1, sl, :] + jnp.exp(e - L) * o_s[2, sl, :]
            lse_ref[sl, :] = L

    blk_of = lambda g: pl.BlockSpec((SEQ, LANE), lambda b, hp, g=g: (b, g * NP + hp))
    return pl.pallas_call(
        body, grid=(B, NP), in_specs=[blk_of(0), blk_of(1), blk_of(2)], out_specs=[blk_of(0), blk_of(0)],
        out_shape=[SDS((T, WG), F32), SDS((T, WG), F32)],
        scratch_shapes=[pltpu.VMEM((3, SEQ, LANE), F32), pltpu.VMEM((3, SEQ, LANE), F32)],
        compiler_params=_cp("arbitrary", "arbitrary"), name=name,
    )(pa, pa, pa)


def _pair_rowsum(m0, prod):
    s0 = jnp.sum(jnp.where(m0, prod, 0.0), axis=-1, keepdims=True)
    return _pair(m0, s0, jnp.sum(prod, axis=-1, keepdims=True) - s0)


def band_bwd(pa, do, out, lse, rc, rs1, rs2, name):
    T = pa.shape[0]
    B = T // SEQ
    NP = WG // LANE

    def body(q_ref, k_ref, v_ref, do_ref, out_ref, l_ref, c_ref, s1_ref, s2_ref, dqo_ref, dko_ref, dvo_ref, d_s, dq_ref, dk_ref,
             dv_ref):
        m0 = _head_lanes()
        dq_ref[...] = jnp.zeros_like(dq_ref)
        dk_ref[...] = jnp.zeros_like(dk_ref)
        dv_ref[...] = jnp.zeros_like(dv_ref)
        for c in range(SEQ // FB):
            sl = slice(c * FB, (c + 1) * FB)
            d_s[sl, :] = _pair_rowsum(m0, do_ref[sl, :] * out_ref[sl, :])
        for d, nbc in PATTERNS:
            span = QB if nbc == 1 else 2 * QB

            def blk(it, carry, d=d, nbc=nbc, span=span):
                masks = (m0, jnp.logical_not(m0))
                ld = []
                for u in range(BAND_UNROLL_BWD):
                    qs, ks, off = _band_rows(it * BAND_UNROLL_BWD + u, d, nbc)
                    qrow, krow = _rows(qs, QB, d), _rows(ks, span, d)
                    ld.append(dict(qrow=qrow, krow=krow, q=q_ref[qrow, :] * ATTN_SCALE, k=k_ref[krow, :].astype(BF16),
                                   v=v_ref[krow, :].astype(BF16), do=do_ref[qrow, :], l=l_ref[qrow, :], dv=d_s[qrow, :],
                                   valid=_band_valid(span, off)))
                for t in ld:
                    t["qm"] = [jnp.where(mh, t["q"], 0.0).astype(BF16) for mh in masks]
                    t["dom"] = [jnp.where(mh, t["do"], 0.0).astype(BF16) for mh in masks]
                sd = [[(jnp.where(t["valid"], _dot_nt(t["qm"][h], t["k"]), NEG), _dot_nt(t["dom"][h], t["v"])) for h in range(2)]
                      for t in ld]
                pd = []
                for t, pair in zip(ld, sd):
                    row = []
                    for h, (s, dp) in enumerate(pair):
                        col = slice(h * HD, h * HD + 1)
                        p = jnp.exp(s - t["l"][:, col])
                        row.append((p.astype(BF16), (p * (dp - t["dv"][:, col])).astype(BF16)))
                    pd.append(row)
                gr = [(_dot(row[0][1], t["k"]), _dot(row[1][1], t["k"]),
                       _dot_tn(jnp.concatenate([row[0][1], row[1][1]], axis=0), jnp.concatenate(t["qm"], axis=0)),
                       _dot_tn(jnp.concatenate([row[0][0], row[1][0]], axis=0), jnp.concatenate(t["dom"], axis=0)))
                      for t, row in zip(ld, pd)]
                for t, (dq0, dq1, dk, dv) in zip(ld, gr):
                    dq_ref[t["qrow"], :] += _pair(m0, dq0, dq1) * ATTN_SCALE
                    dk_ref[t["krow"], :] += dk
                    dv_ref[t["krow"], :] += dv
                return carry

            lax.fori_loop(0, SEQ // QB // BAND_UNROLL_BWD, blk, 0)
        for c in range(SEQ // FB):
            sl = slice(c * FB, (c + 1) * FB)
            cc, s1, s2 = c_ref[sl, :], s1_ref[sl, :], s2_ref[sl, :]
            for acc, o_ref in ((dq_ref, dqo_ref), (dk_ref, dko_ref)):
                d = acc[sl, :]
                o_ref[sl, :] = (d * cc + pltpu.roll(d * s1, 8, 1) + pltpu.roll(d * s2, LANE - 8, 1)).astype(BF16)
            dvo_ref[sl, :] = dv_ref[sl, :].astype(BF16)

    blk_of = lambda g: pl.BlockSpec((SEQ, LANE), lambda b, hp, g=g: (b, g * NP + hp))
    tab = pl.BlockSpec((SEQ, LANE), lambda b, hp: (b, 0))
    return pl.pallas_call(
        body, grid=(B, NP), in_specs=[blk_of(0), blk_of(1), blk_of(2), blk_of(0), blk_of(0), blk_of(0), tab, tab, tab],
        out_specs=[blk_of(0)] * 3, out_shape=[SDS((T, WG), BF16)] * 3,
        scratch_shapes=[pltpu.VMEM((SEQ, LANE), F32)] * 4,
        compiler_params=_cp("arbitrary", "arbitrary"), name=name,
    )(pa, pa, pa, do, out, lse, rc, rs1, rs2)


def _tile_causal(nq, nk, q0, k0):
    r = lax.broadcasted_iota(jnp.int32, (nq, nk), 0)
    c = lax.broadcasted_iota(jnp.int32, (nq, nk), 1)
    return r + (q0 - k0) >= c


def _row_to_col(row):
    n = row.shape[1]
    return jnp.transpose(jnp.broadcast_to(row, (LANE, n)))[:, 0:1]


def _col_to_row(col):
    n = col.shape[0]
    return jnp.transpose(jnp.broadcast_to(col, (n, LANE)))[0:1, :]


def fox_fwd(pb, fblk, frow, name, gather=None):
    FQ = FOX_QB
    T = pb.shape[0]
    B = T // SEQ
    NG = WG // (LANE * FOX_PAIRS)
    NHS = 2 * FOX_PAIRS
    W = LANE * FOX_PAIRS
    n = SEQ // FQ
    ng = 0 if gather is None else len(gather[0])
    plan = None if gather is None else ShardGather([w.shape for w in gather[0]], gather[1])

    def body(*refs):
        q_ref, k_ref, v_ref, fc_ref, fr_ref = refs[:5]
        o_ref, lse_ref = refs[5 + ng:7 + ng]
        if plan is not None:
            comm = (refs[5:5 + ng], refs[7 + ng:7 + 2 * ng], refs[7 + 2 * ng:])
            ids = [pl.program_id(ax) for ax in range(3)]
            pl.when((ids[0] == 0) & (ids[1] == 0) & (ids[2] == 0))(lambda: plan.start(*comm))
        i = pl.program_id(2)
        m0 = _head_lanes()
        masks = (m0, jnp.logical_not(m0))
        heads = [(hh, slice((hh // 2) * LANE, (hh // 2 + 1) * LANE), masks[hh % 2]) for hh in range(NHS)]
        qh, fq = [], []
        for hh, lanes, mh in heads:
            q = q_ref[:, lanes] * ATTN_SCALE
            qh.append(jnp.where(mh, q, jnp.zeros_like(q)))
            fq.append(_row_to_col(fc_ref[0, hh, 0]))

        def step(t, carry, masked):
            rows = pl.ds(pl.multiple_of(t * FT, FT), FT)
            ss = [_dot_nt(qh[hh], k_ref[rows, lanes]) + fq[hh] - fr_ref[0, hh, t] for hh, lanes, _ in heads]
            if masked:
                ok = _tile_causal(FQ, FT, i * FQ, t * FT)
                ss = [jnp.where(ok, s, NEG) for s in ss]
            st = []
            for hh, _, _ in heads:
                m2 = jnp.maximum(carry[hh][0], jnp.max(ss[hh], axis=-1, keepdims=True))
                st.append((m2, jnp.exp(carry[hh][0] - m2), jnp.exp(ss[hh] - m2).astype(BF16)))
            pv = []
            for hh, lanes, mh in heads:
                vt = v_ref[rows, lanes]
                pv.append(_dot(st[hh][2], jnp.where(mh, vt, jnp.ones_like(vt))))
            return tuple((st[hh][0], st[hh][1] * carry[hh][1] + pv[hh]) for hh in range(NHS))

        one = (jnp.full((FQ, 1), NEG, F32), jnp.zeros((FQ, LANE), F32))
        last = (i * FQ) // FT
        carry = lax.fori_loop(0, last, lambda t, cr: step(t, cr, False), (one,) * NHS)
        carry = step(last, carry, True)
        for pr in range(FOX_PAIRS):
            (ma, acca), (mb, accb) = carry[2 * pr], carry[2 * pr + 1]
            la, lb = acca[:, HD:HD + 1], accb[:, 0:1]
            o_ref[:, pr * LANE:(pr + 1) * LANE] = _pair(m0, acca / la, accb / lb)
            lse_ref[0, 2 * pr, 0] = _col_to_row(ma + jnp.log(la))
            lse_ref[0, 2 * pr + 1, 0] = _col_to_row(mb + jnp.log(lb))
        if plan is not None:
            pl.when((ids[0] == B - 1) & (ids[1] == NG - 1) & (ids[2] == n - 1))(lambda: plan.finish(*comm))

    qblk = pl.BlockSpec((FQ, W), lambda b, g, i: (b * n + i, g))
    full = lambda grp: pl.BlockSpec((SEQ, W), lambda b, g, i, grp=grp: (b, grp * NG + g))
    rowb = pl.BlockSpec((1, NHS, 1, 1, FQ), lambda b, g, i: (b, g, i, 0, 0))
    outs = pl.pallas_call(
        body, grid=(B, NG, n),
        in_specs=[qblk, full(1), full(2), rowb, pl.BlockSpec((1, NHS, SEQ // FT, 1, FT), lambda b, g, i: (b, g, 0, 0, 0))]
        + [HBM] * ng,
        out_specs=[qblk, rowb] + [HBM] * ng,
        out_shape=[SDS((T, WG), F32), SDS((B, NH, n, 1, FQ), F32)] + ([] if plan is None else plan.out_shapes(BF16)),
        scratch_shapes=[] if plan is None else plan.scratch(),
        compiler_params=_cp("arbitrary", "arbitrary", "arbitrary"), name=name,
    )(pb, pb, pb, fblk, frow, *([] if gather is None else gather[0]))
    return outs[:2], outs[2:]


def fox_bwd(pb, do, lrow, drow, fblk, frow, name):
    T = pb.shape[0]
    B = T // SEQ
    PAIRS = FOX_PAIRS_BWD
    NG = WG // (LANE * PAIRS)
    NHS = 2 * PAIRS
    W = LANE * PAIRS
    n = SEQ // FB

    def body(q_ref, k_ref, v_ref, do_ref, l_ref, d_ref, fc_ref, fr_ref, dqo_ref, dk_ref, dv_ref, dfq_ref, dfk_ref, dq_ref):
        j = pl.program_id(2)
        m0 = _head_lanes()
        masks = (m0, jnp.logical_not(m0))
        heads = [(hh, slice((hh // 2) * LANE, (hh // 2 + 1) * LANE), masks[hh % 2]) for hh in range(NHS)]

        @pl.when(j == 0)
        def _():
            dq_ref[...] = jnp.zeros_like(dq_ref)
            dfq_ref[...] = jnp.zeros_like(dfq_ref)

        kj = [k_ref[:, lanes] for _, lanes, _ in heads]
        vj = [v_ref[:, lanes] for _, lanes, _ in heads]
        fk = [_row_to_col(fc_ref[0, hh, 0]) for hh in range(NHS)]

        def step(t, carry, masked):
            rows = pl.ds(pl.multiple_of(t * FT, FT), FT)
            qm, dom = [], []
            for _, lanes, mh in heads:
                qt = q_ref[rows, lanes] * ATTN_SCALE
                qm.append(jnp.where(mh, qt, jnp.zeros_like(qt)))
                dom.append(jnp.where(mh, do_ref[rows, lanes], 0.0).astype(BF16))
            ss = [_dot_nt(kj[hh], qm[hh]) + fr_ref[0, hh, t] - fk[hh] for hh in range(NHS)]
            dps = [_dot_nt(vj[hh], dom[hh]) for hh in range(NHS)]
            if masked:
                key = lax.broadcasted_iota(jnp.int32, (FB, FT), 0)
                qry = lax.broadcasted_iota(jnp.int32, (FB, FT), 1)
                ok = qry + (t * FT - j * FB) >= key
                ss = [jnp.where(ok, s, NEG) for s in ss]
            pds = []
            for hh in range(NHS):
                p = jnp.exp(ss[hh] - l_ref[0, hh, t])
                ds = p * (dps[hh] - d_ref[0, hh, t])
                dfq_ref[0, hh, t] += jnp.sum(ds, axis=0, keepdims=True)
                pds.append((p.astype(BF16), ds.astype(BF16), jnp.sum(ds, axis=-1, keepdims=True)))
            dks = [_dot(pds[hh][1], qm[hh]) for hh in range(NHS)]
            dvs = [_dot(pds[hh][0], dom[hh]) for hh in range(NHS)]
            dqs = [_dot_tn(pds[hh][1], kj[hh]) for hh in range(NHS)]
            for pr in range(PAIRS):
                dq_ref[rows, pr * LANE:(pr + 1) * LANE] += _pair(m0, dqs[2 * pr], dqs[2 * pr + 1]) * ATTN_SCALE
            return tuple((carry[hh][0] + dks[hh], carry[hh][1] + dvs[hh], carry[hh][2] - pds[hh][2]) for hh in range(NHS))

        one = (jnp.zeros((FB, LANE), F32), jnp.zeros((FB, LANE), F32), jnp.zeros((FB, 1), F32))
        first = (j * FB) // FT
        carry = step(first, (one,) * NHS, True)
        carry = lax.fori_loop(first + 1, SEQ // FT, lambda t, cr: step(t, cr, False), carry)
        for pr in range(PAIRS):
            (dka, dva, dfka), (dkb, dvb, dfkb) = carry[2 * pr], carry[2 * pr + 1]
            dk_ref[:, pr * LANE:(pr + 1) * LANE] = _pair(m0, dka, dkb).astype(BF16)
            dv_ref[:, pr * LANE:(pr + 1) * LANE] = _pair(m0, dva, dvb).astype(BF16)
            dfk_ref[0, 2 * pr, 0] = _col_to_row(dfka)
            dfk_ref[0, 2 * pr + 1, 0] = _col_to_row(dfkb)

        @pl.when(j == n - 1)
        def _():
            dqo_ref[...] = dq_ref[...].astype(BF16)

    kblk = lambda grp: pl.BlockSpec((FB, W), lambda b, g, j, grp=grp: (b * n + j, grp * NG + g))
    full = pl.BlockSpec((SEQ, W), lambda b, g, j: (b, g))
    rowf = pl.BlockSpec((1, NHS, SEQ // FT, 1, FT), lambda b, g, j: (b, g, 0, 0, 0))
    rowb = pl.BlockSpec((1, NHS, 1, 1, FB), lambda b, g, j: (b, g, j, 0, 0))
    return pl.pallas_call(
        body, grid=(B, NG, n), in_specs=[full, kblk(1), kblk(2), full, rowf, rowf, rowb, rowf],
        out_specs=[full, kblk(0), kblk(0), rowf, rowb],
        out_shape=[SDS((T, WG), BF16), SDS((T, WG), BF16), SDS((T, WG), BF16), SDS((B, NH, SEQ // FT, 1, FT), F32),
                   SDS((B, NH, n, 1, FB), F32)],
        scratch_shapes=[pltpu.VMEM((SEQ, W), F32)],
        compiler_params=_cp("arbitrary", "arbitrary", "arbitrary"), name=name,
    )(pb, pb, pb, do, lrow, drow, fblk, frow)


def _tri(lower):
    r = lax.broadcasted_iota(jnp.int32, (LANE, LANE), 0)
    c = lax.broadcasted_iota(jnp.int32, (LANE, LANE), 1)
    return ((r >= c) if lower else (r <= c)).astype(F32)


def _tri_dot(t, xblk):
    return jnp.dot(t, xblk, precision=lax.Precision.HIGHEST, preferred_element_type=F32)


def forget_cumsum(flog, bias, name):
    B, S, _ = flog.shape

    def body(f_ref, b_ref, o_ref):
        t = _tri(True)
        carry = jnp.zeros((1, LANE), F32)
        for blk in range(S // LANE):
            z = f_ref[0, blk * LANE:(blk + 1) * LANE, :] + b_ref[...]
            lf = jnp.minimum(z, 0.0) - jnp.log(1.0 + jnp.exp(-jnp.abs(z)))
            cs = _tri_dot(t, lf) + carry
            o_ref[0, blk * LANE:(blk + 1) * LANE, :] = cs
            carry = cs[LANE - 1:LANE, :]

    spec = pl.BlockSpec((1, S, LANE), lambda b: (b, 0, 0))
    return pl.pallas_call(
        body, grid=(B,), in_specs=[spec, pl.BlockSpec((1, LANE), lambda b: (0, 0))], out_specs=spec,
        out_shape=SDS((B, S, LANE), F32), compiler_params=_cp("arbitrary"), name=name,
    )(flog, bias)


def forget_cumsum_bwd(dF, flog, bias, name):
    B, S, _ = flog.shape

    def body(d_ref, f_ref, b_ref, o_ref, db_ref):
        @pl.when(pl.program_id(0) == 0)
        def _():
            db_ref[...] = jnp.zeros_like(db_ref)

        t = _tri(False)
        carry = jnp.zeros((1, LANE), F32)
        tot = jnp.zeros((1, LANE), F32)
        for blk in reversed(range(S // LANE)):
            sl = slice(blk * LANE, (blk + 1) * LANE)
            rc = _tri_dot(t, d_ref[0, sl, :]) + carry
            carry = rc[0:1, :]
            z = f_ref[0, sl, :] + b_ref[...]
            dz = rc * jax.nn.sigmoid(-z)
            o_ref[0, sl, :] = dz
            tot = tot + jnp.sum(dz, axis=0, keepdims=True)
        db_ref[...] += tot

    spec = pl.BlockSpec((1, S, LANE), lambda b: (b, 0, 0))
    vec = pl.BlockSpec((1, LANE), lambda b: (0, 0))
    return pl.pallas_call(
        body, grid=(B,), in_specs=[spec, spec, vec], out_specs=[spec, vec],
        out_shape=[SDS((B, S, LANE), F32), SDS((1, LANE), F32)], compiler_params=_cp("arbitrary"), name=name,
    )(dF, flog, bias)


def mixer_out_fwd(oa, ob, goa, gob, w_out, g_post, x, mod3, name):
    T = x.shape[0]

    def body(oa_ref, ob_ref, goa_ref, gob_ref, w_ref, gp_ref, x_ref, mod_ref, xo_ref, mg_ref, y0_ref):
        a = oa_ref[...]
        b = ob_ref[...]
        mg = jnp.concatenate([a * _rms(a) * goa_ref[...], b * _rms(b) * gob_ref[...]], axis=-1).astype(BF16)
        mg_ref[...] = mg
        y0 = _dot(mg, w_ref[...])
        y0_ref[...] = y0
        xo_ref[...] = x_ref[...] + mod_ref[0, 2:3, :] * (y0 * _rms(y0) * gp_ref[...])

    tok = pl.BlockSpec((TM, D), lambda i: (i, 0))
    half = pl.BlockSpec((TM, WG), lambda i: (i, 0))
    hv = pl.BlockSpec((1, WG), lambda i: (0, 0))
    return pl.pallas_call(
        body, grid=(T // TM,),
        in_specs=[half, half, hv, hv, pl.BlockSpec((D, D), lambda i: (0, 0)), pl.BlockSpec((1, D), lambda i: (0, 0)), tok,
                  pl.BlockSpec((1, 3, D), _mod_map)],
        out_specs=[tok, tok, tok], out_shape=[SDS((T, D), F32), SDS((T, D), BF16), SDS((T, D), F32)],
        compiler_params=_cp("arbitrary"), name=name,
    )(oa, ob, goa, gob, w_out, g_post, x, mod3)


def mixer_out_bwd(dxo, y0, mod3, g_post, w_out, oa, ob, goa, gob, name):
    T = dxo.shape[0]
    nb = T // SEQ
    tiles_per_seq = SEQ // TM

    def body(dxo_ref, y0_ref, mod_ref, gp_ref, w_ref, oa_ref, ob_ref, goa_ref, gob_ref,
             dy0_ref, doa_ref, dob_ref, dmg_ref, dgp_ref, dgoa_ref, dgob_ref, dvb_ref):
        i = pl.program_id(0)

        @pl.when(i == 0)
        def _():
            dgp_ref[...] = jnp.zeros_like(dgp_ref)
            dgoa_ref[...] = jnp.zeros_like(dgoa_ref)
            dgob_ref[...] = jnp.zeros_like(dgob_ref)

        @pl.when(i % tiles_per_seq == 0)
        def _():
            dmg_ref[...] = jnp.zeros_like(dmg_ref)

        dy0, dmg, dg = _post_bwd(dxo_ref[...], y0_ref[...], gp_ref[...], mod_ref[0, 2:3, :], 1.0)
        dmg_ref[0] += dmg
        dgp_ref[...] += dg
        db = dy0.astype(BF16)
        dy0_ref[...] = db
        dm = _dot_nt(db, w_ref[...])
        for o_ref, g_ref, do_ref, dg_ref, sl in ((oa_ref, goa_ref, doa_ref, dgoa_ref, slice(0, WG)),
                                                  (ob_ref, gob_ref, dob_ref, dgob_ref, slice(WG, 2 * WG))):
            o = o_ref[...]
            r = _rms(o)
            oh = o * r
            d = dm[:, sl]
            dg_ref[...] += jnp.sum(d * oh, axis=0, keepdims=True)
            dh = d * g_ref[...]
            do = r * (dh - oh * jnp.mean(dh * oh, axis=-1, keepdims=True))
            do_ref[...] = do
        ind = (lax.broadcasted_iota(jnp.int32, (WG, LANE), 0) // HD == lax.broadcasted_iota(jnp.int32, (WG, LANE), 1)).astype(BF16)
        prod = do * o
        hi = prod.astype(BF16)
        dvb_ref[...] = _dot(hi, ind) + _dot((prod - hi.astype(F32)).astype(BF16), ind)

    tok = pl.BlockSpec((TM, D), lambda i: (i, 0))
    half = pl.BlockSpec((TM, WG), lambda i: (i, 0))
    hv = pl.BlockSpec((1, WG), lambda i: (0, 0))
    vec = pl.BlockSpec((1, D), lambda i: (0, 0))
    return pl.pallas_call(
        body, grid=(T // TM,),
        in_specs=[tok, tok, pl.BlockSpec((1, 3, D), _mod_map), vec, pl.BlockSpec((D, D), lambda i: (0, 0)), half, half, hv, hv],
        out_specs=[tok, half, half, pl.BlockSpec((1, 1, D), _mod_map), vec, hv, hv, pl.BlockSpec((TM, LANE), lambda i: (i, 0))],
        out_shape=[SDS((T, D), BF16), SDS((T, WG), F32), SDS((T, WG), F32), SDS((nb, 1, D), F32), SDS((1, D), F32),
                   SDS((1, WG), F32), SDS((1, WG), F32), SDS((T, LANE), F32)],
        compiler_params=_cp("arbitrary"), name=name,
    )(dxo, y0, mod3, g_post, w_out, oa, ob, goa, gob)


def mixer_proj_bwd(dps, dflog, dxo, x, mod3, g_pre, w_main, w_f, name):
    T = x.shape[0]
    nb = T // SEQ
    tiles_per_seq = SEQ // TM
    ngrp = len(dps)

    def body(*refs):
        dp_refs = refs[:ngrp]
        df_ref, dxo_ref, x_ref, mod_ref, g_ref, w_ref, wf_ref, dx_ref, dmod_ref, dg_ref = refs[ngrp:]
        i = pl.program_id(0)

        @pl.when(i == 0)
        def _():
            dg_ref[...] = jnp.zeros_like(dg_ref)

        @pl.when(i % tiles_per_seq == 0)
        def _():
            dmod_ref[...] = jnp.zeros_like(dmod_ref)

        dh = _dot_nt(df_ref[...].astype(BF16), wf_ref[...])
        for g, dp_ref in enumerate(dp_refs):
            dh = dh + _dot_nt(dp_ref[...], w_ref[:, g * WG:(g + 1) * WG])
        dx, dsh, dsc, dg = _norm_mod_bwd(dh, x_ref[...], g_ref[...], mod_ref[0, 1:2, :])
        dx_ref[...] = dxo_ref[...] + dx
        dmod_ref[0, 0:1, :] += dsh
        dmod_ref[0, 1:2, :] += dsc
        dg_ref[...] += dg

    tok = pl.BlockSpec((TM, D), lambda i: (i, 0))
    vec = pl.BlockSpec((1, D), lambda i: (0, 0))
    return pl.pallas_call(
        body, grid=(T // TM,),
        in_specs=[pl.BlockSpec((TM, WG), lambda i: (i, 0))] * ngrp
        + [pl.BlockSpec((TM, LANE), lambda i: (i, 0)), tok, tok, pl.BlockSpec((1, 3, D), _mod_map), vec,
           pl.BlockSpec((D, IN_MAIN), lambda i: (0, 0)), pl.BlockSpec((D, LANE), lambda i: (0, 0))],
        out_specs=[tok, pl.BlockSpec((1, 2, D), _mod_map), vec],
        out_shape=[SDS((T, D), F32), SDS((nb, 2, D), F32), SDS((1, D), F32)],
        compiler_params=_cp("arbitrary"), name=name,
    )(*dps, dflog, dxo, x, mod3, g_pre, w_main, w_f)


def ada_fwd(c_all, w, b, name):
    n = w.shape[1]
    tn = n // 2

    def body(c_ref, w_ref, b_ref, o_ref):
        cv = c_ref[...]
        o_ref[...] = _dot((cv * jax.nn.sigmoid(cv)).astype(BF16), w_ref[...].astype(BF16)) + b_ref[...]

    R = c_all.shape[0]
    return pl.pallas_call(
        body, grid=(2,),
        in_specs=[pl.BlockSpec((R, D), lambda j: (0, 0)), pl.BlockSpec((D, tn), lambda j: (0, j)), pl.BlockSpec((1, tn), lambda j: (0, j))],
        out_specs=pl.BlockSpec((R, tn), lambda j: (0, j)), out_shape=SDS((R, n), F32),
        compiler_params=_cp("arbitrary"), name=name,
    )(c_all, w, b)


def ada_bwd(c_all, dmod, name):
    R, n = dmod.shape
    tn = n // 2

    def body(c_ref, d_ref, o_ref):
        cv = c_ref[...]
        o_ref[...] = _dot_tn((cv * jax.nn.sigmoid(cv)).astype(BF16), d_ref[...].astype(BF16))

    return pl.pallas_call(
        body, grid=(2,), in_specs=[pl.BlockSpec((R, D), lambda j: (0, 0)), pl.BlockSpec((R, tn), lambda j: (0, j))],
        out_specs=pl.BlockSpec((D, tn), lambda j: (0, j)), out_shape=SDS((D, n), F32),
        compiler_params=_cp("arbitrary"), name=name,
    )(c_all, dmod)


def _adam_math(w, g, m, v):
    m2 = ADAM_B1 * m + (1.0 - ADAM_B1) * g
    v2 = ADAM_B2 * v + (1.0 - ADAM_B2) * (g * g)
    m_hat = m2 / (1.0 - ADAM_B1 ** ADAM_STEP)
    v_hat = v2 / (1.0 - ADAM_B2 ** ADAM_STEP)
    delta = -ADAM_LR * (m_hat / (jnp.sqrt(v_hat) + ADAM_EPS) + ADAM_WD * w)
    return delta, m2, v2


def adam_update(w, g, m, v, tr, name):
    _, R, C = w.shape

    def body(w_ref, g_ref, m_ref, v_ref, d_ref, mo_ref, vo_ref):
        d_ref[0], mo_ref[0], vo_ref[0] = _adam_math(w_ref[0], g_ref[...], m_ref[0], v_ref[0])

    spec = pl.BlockSpec((1, tr, C), lambda i: (0, i, 0))
    gspec = pl.BlockSpec((tr, C), lambda i: (i, 0))
    return pl.pallas_call(
        body, grid=(R // tr,), in_specs=[spec, gspec, spec, spec], out_specs=[spec] * 3, out_shape=[SDS((1, R, C), F32)] * 3,
        compiler_params=_cp("arbitrary"), name=name,
    )(w, g, m, v)


def adam_update_halves(w, mine, other, m, v, cidx, tr, name):
    _, R, C = w.shape
    nh = R // 2 // tr

    def body(c_ref, w_ref, a_ref, b_ref, m_ref, v_ref, g_ref, d_ref, mo_ref, vo_ref):
        first_half = pl.program_id(0) < nh
        g = jnp.where(first_half == (c_ref[0] == 0), a_ref[...], b_ref[...])
        g_ref[0] = g
        d_ref[0], mo_ref[0], vo_ref[0] = _adam_math(w_ref[0], g, m_ref[0], v_ref[0])

    spec = pl.BlockSpec((1, tr, C), lambda i, c_ref: (0, i, 0))
    hspec = pl.BlockSpec((tr, C), lambda i, c_ref: (i % nh, 0))
    return pl.pallas_call(
        body,
        grid_spec=pltpu.PrefetchScalarGridSpec(num_scalar_prefetch=1, grid=(R // tr,), in_specs=[spec, hspec, hspec, spec, spec],
                                               out_specs=[spec] * 4),
        out_shape=[SDS((1, R, C), F32)] * 4, compiler_params=_cp("arbitrary"), name=name,
    )(cidx, w, mine, other, m, v)


def vec_adam(parts, w, m, v, name):
    P, C = parts.shape

    def body(p_ref, w_ref, m_ref, v_ref, g_ref, d_ref, mo_ref, vo_ref):
        g = jnp.sum(p_ref[...], axis=0, keepdims=True)
        g_ref[...] = g
        d_ref[...], mo_ref[...], vo_ref[...] = _adam_math(w_ref[...], g, m_ref[...], v_ref[...])

    return pl.pallas_call(body, out_shape=[SDS((1, C), F32)] * 4, compiler_params=_cp(), name=name)(parts, w, m, v)


HBM = pl.BlockSpec(memory_space=pltpu.HBM)
VMEM = pl.BlockSpec(memory_space=pltpu.VMEM)


def _place():
    x, y, c = lax.axis_index("x"), lax.axis_index("y"), lax.axis_index("c")
    return x, y, c, [(1 - x, y), (x, 1 - y), (1 - x, 1 - y)]


def all_gather8(xs, name):
    R, C = xs.shape

    def body(x_ref, out_ref, send_sems, recv_sems, local_sem):
        x, y, c, chips = _place()
        me, sibling = (x, y, c), (x, y, 1 - c)

        def slot(px, py, pc):
            return out_ref.at[4 * px + 2 * py + pc]

        def copy(k, block, to, src=None):
            return pltpu.make_async_remote_copy(
                src_ref=slot(*block) if src is None else src, dst_ref=slot(*block),
                send_sem=send_sems.at[k], recv_sem=recv_sems.at[k], device_id=to, device_id_type=MESH)

        mine = pltpu.make_async_copy(x_ref, slot(*me), local_sem)
        mine.start()
        first = [copy(0, me, sibling, src=x_ref)]
        first += [copy(1 + j, me, (*chip, c), src=x_ref) for j, chip in enumerate(chips)]
        for cp in first:
            cp.start()
        passed = [copy(4 + j, (*chip, c), sibling) for j, chip in enumerate(chips)]
        for j, chip in enumerate(chips):
            copy(1 + j, (*chip, c), me).wait_recv()
            passed[j].start()
        copy(0, sibling, me).wait_recv()
        for j, chip in enumerate(chips):
            copy(4 + j, (*chip, 1 - c), me).wait_recv()
        for cp in first + passed:
            cp.wait_send()
        mine.wait()

    return pl.pallas_call(
        body, out_shape=SDS((N_DEV, R, C), xs.dtype), in_specs=[VMEM], out_specs=VMEM,
        scratch_shapes=[pltpu.SemaphoreType.DMA((7,)), pltpu.SemaphoreType.DMA((7,)), pltpu.SemaphoreType.DMA],
        compiler_params=pltpu.CompilerParams(vmem_limit_bytes=VMEM_LIMIT), name=name,
    )(xs)


class ShardGather:
    def __init__(self, shapes, splits):
        self.shapes, self.splits, self.n = shapes, splits, len(shapes)

    def scratch(self):
        n = self.n
        return [pltpu.SemaphoreType.DMA((n, 6)), pltpu.SemaphoreType.DMA((n, 6)), pltpu.SemaphoreType.DMA((n,))]

    def out_shapes(self, dtype):
        return [SDS((N_SHARD,) + tuple(s), dtype) for s in self.shapes]

    def _half(self, ref, k, cc):
        lo, hi = (0, self.splits[k]) if cc == 0 else (self.splits[k], self.shapes[k][0])
        return ref.at[pl.ds(lo, hi - lo)]

    def _phase(self, w_refs, o_refs, sems, finish):
        send_sems, recv_sems, local_sems = sems
        x, y, c, chips = _place()
        sibling = (x, y, 1 - c)
        me_s = 2 * x + y

        def rcopy(src, dst, k, s, to):
            return pltpu.make_async_remote_copy(src_ref=src, dst_ref=dst, send_sem=send_sems.at[k, s],
                                                recv_sem=recv_sems.at[k, s], device_id=to, device_id_type=MESH)

        for cc in (0, 1):
            @pl.when(c == cc)
            def _():
                local = [pltpu.make_async_copy(w_refs[k], o_refs[k].at[me_s], local_sems.at[k]) for k in range(self.n)]
                first = [rcopy(self._half(w_refs[k], k, cc), self._half(o_refs[k].at[me_s], k, cc), k, j, (*chip, c))
                         for k in range(self.n) for j, chip in enumerate(chips)]
                if not finish:
                    for cp in local + first:
                        cp.start()
                    return
                passed = []
                for k in range(self.n):
                    for j, chip in enumerate(chips):
                        land = self._half(o_refs[k].at[2 * chip[0] + chip[1]], k, cc)
                        rcopy(land, land, k, j, (*chip, c)).wait_recv()
                        f = rcopy(land, land, k, 3 + j, sibling)
                        f.start()
                        passed.append(f)
                for k in range(self.n):
                    for j, chip in enumerate(chips):
                        other = self._half(o_refs[k].at[2 * chip[0] + chip[1]], k, 1 - cc)
                        rcopy(other, other, k, 3 + j, sibling).wait_recv()
                for s in first + passed:
                    s.wait_send()
                for cp in local:
                    cp.wait()

    def start(self, w_refs, o_refs, sems):
        self._phase(w_refs, o_refs, sems, False)

    def finish(self, w_refs, o_refs, sems):
        self._phase(w_refs, o_refs, sems, True)


def all_gather_shards(ws, splits, name):
    n = len(ws)
    plan = ShardGather([w.shape for w in ws], splits)

    def body(*refs):
        plan.start(refs[:n], refs[n:2 * n], refs[2 * n:])
        plan.finish(refs[:n], refs[n:2 * n], refs[2 * n:])

    return pl.pallas_call(
        body, out_shape=plan.out_shapes(ws[0].dtype), in_specs=[HBM] * n, out_specs=[HBM] * n,
        scratch_shapes=plan.scratch(), name=name,
    )(*ws)


def sibling_send_half(gs, name):
    n = len(gs)

    def body(*refs):
        g_refs, o_refs = refs[:n], refs[n:2 * n]
        send_sems, recv_sems = refs[2 * n:]
        x, y, c, _ = _place()
        cps = []
        for k in range(n):
            hr = gs[k].shape[1] // 2
            src = g_refs[k].at[:, pl.ds(pl.multiple_of((1 - c) * hr, 8), hr)]
            cp = pltpu.make_async_remote_copy(src_ref=src, dst_ref=o_refs[k], send_sem=send_sems.at[k], recv_sem=recv_sems.at[k],
                                              device_id=(x, y, 1 - c), device_id_type=MESH)
            cp.start()
            cps.append(cp)
        for cp in cps:
            cp.wait()

    return pl.pallas_call(
        body, out_shape=[SDS((N_SHARD, g.shape[1] // 2, g.shape[2]), g.dtype) for g in gs], in_specs=[HBM] * n, out_specs=[HBM] * n,
        scratch_shapes=[pltpu.SemaphoreType.DMA((n,)), pltpu.SemaphoreType.DMA((n,))], name=name,
    )(*gs)


def _scatter_copies(h_refs, o_refs, send_sems, recv_sems):
    _, _, c, chips = _place()
    return [pltpu.make_async_remote_copy(
        src_ref=h_refs[k].at[2 * chip[0] + chip[1]], dst_ref=o_refs[k].at[j], send_sem=send_sems.at[k, j],
        recv_sem=recv_sems.at[k, j], device_id=(*chip, c), device_id_type=MESH)
        for k in range(len(h_refs)) for j, chip in enumerate(chips)]


def chip_scatter(hs, name):
    n = len(hs)

    def body(*refs):
        cps = _scatter_copies(refs[:n], refs[n:2 * n], *refs[2 * n:])
        for cp in cps:
            cp.start()
        for cp in cps:
            cp.wait()

    return pl.pallas_call(
        body, out_shape=[SDS((3,) + h.shape[1:], h.dtype) for h in hs], in_specs=[HBM] * n, out_specs=[HBM] * n,
        scratch_shapes=[pltpu.SemaphoreType.DMA((n, 3)), pltpu.SemaphoreType.DMA((n, 3))], name=name,
    )(*hs)


def sibling_swap(ghs, name):
    n = len(ghs)

    def body(*refs):
        g_refs, o_refs = refs[:n], refs[n:2 * n]
        send_sems, recv_sems = refs[2 * n:]
        x, y, c, _ = _place()
        cps = []
        for k in range(n):
            cp = pltpu.make_async_remote_copy(src_ref=g_refs[k], dst_ref=o_refs[k], send_sem=send_sems.at[k],
                                              recv_sem=recv_sems.at[k], device_id=(x, y, 1 - c), device_id_type=MESH)
            cp.start()
            cps.append(cp)
        for cp in cps:
            cp.wait()

    return pl.pallas_call(
        body, out_shape=[SDS(g.shape, g.dtype) for g in ghs], in_specs=[HBM] * n, out_specs=[HBM] * n,
        scratch_shapes=[pltpu.SemaphoreType.DMA((n,)), pltpu.SemaphoreType.DMA((n,))], name=name,
    )(*ghs)


def pair_sum(g, ra, cidx, name):
    _, r, cols = g.shape
    hr = r // 2

    def body(c_ref, g_ref, a_ref, o_ref):
        o_ref[...] = (g_ref[...] + a_ref[...]).astype(BF16)

    return pl.pallas_call(
        body,
        grid_spec=pltpu.PrefetchScalarGridSpec(
            num_scalar_prefetch=1, grid=(N_SHARD,),
            in_specs=[pl.BlockSpec((1, hr, cols), lambda s, c_ref: (s, c_ref[0], 0)),
                      pl.BlockSpec((1, hr, cols), lambda s, c_ref: (s, 0, 0))],
            out_specs=pl.BlockSpec((1, hr, cols), lambda s, c_ref: (s, 0, 0))),
        out_shape=SDS((N_SHARD, hr, cols), BF16), compiler_params=_cp("arbitrary"), name=name,
    )(cidx, g, ra)


def chip_sum(h, rb, sidx, name):
    _, hr, cols = h.shape

    def body(s_ref, h_ref, r_ref, o_ref):
        o_ref[...] = ((h_ref[0].astype(F32) + r_ref[0].astype(F32)) + r_ref[1].astype(F32)) + r_ref[2].astype(F32)

    return pl.pallas_call(
        body,
        grid_spec=pltpu.PrefetchScalarGridSpec(
            num_scalar_prefetch=1, grid=(1,),
            in_specs=[pl.BlockSpec((1, hr, cols), lambda i, s_ref: (s_ref[0], 0, 0)),
                      pl.BlockSpec((3, hr, cols), lambda i, s_ref: (0, 0, 0))],
            out_specs=pl.BlockSpec((hr, cols), lambda i, s_ref: (0, 0))),
        out_shape=SDS((hr, cols), F32), compiler_params=_cp("arbitrary"), name=name,
    )(sidx, h, rb)


def _shard_cols(g, n_valid):
    r = g.shape[0]
    return g[:, :n_valid].reshape(r, N_SHARD, n_valid // N_SHARD).transpose(1, 0, 2)


def _unshard_cols(o, pad_to):
    _, r, n = o.shape
    full = o.transpose(1, 0, 2).reshape(r, N_SHARD * n)
    return jnp.pad(full, ((0, 0), (0, pad_to - N_SHARD * n)))


def _rows_of_tiles(t):
    B, H, S = t.shape
    return t.reshape(B, H, S // FT, 1, FT)


def mixer_fwd(x1, mod3, g_pre, w_main, w_f, b_forget_pad, goa, gob, w_out, g_post, tabs, nb, gather=None):
    hmix, pa, pb, flog = mixer_proj(x1, mod3, g_pre, w_main, w_f, *tabs, name="mixer_proj")
    out_a, lse_a = band_fwd(pa, name="band_fwd")
    F = forget_cumsum(flog.reshape(nb, SEQ, LANE), b_forget_pad, name="forget_cumsum")
    Fh = F[:, :, :NH].transpose(0, 2, 1)
    fblk = Fh.reshape(nb, NH, SEQ // FB, 1, FB)
    frow = _rows_of_tiles(Fh)
    (out_b, lse_b), gathered = fox_fwd(pb, Fh.reshape(nb, NH, SEQ // FOX_QB, 1, FOX_QB), frow, name="fox_fwd", gather=gather)
    x2, merged, y0m = mixer_out_fwd(out_a, out_b, goa, gob, w_out, g_post, x1, mod3, name="mixer_out_fwd")
    res = dict(hmix=hmix, flog=flog, pa=pa, pb=pb, out_a=out_a, lse_a=lse_a, fblk=fblk, frow=frow, out_b=out_b,
               lrow=_rows_of_tiles(lse_b.reshape(nb, NH, SEQ)), merged=merged, y0m=y0m)
    return x2, res, gathered


def mixer_bwd(dx2, x1, mod3, g_pre, w_main, w_f, b_forget_pad, goa, gob, w_out, g_post, tabs, res, nb):
    T = nb * SEQ
    dy0m, doa, dob, dmgate, dg_post, dgoa, dgob, dvec_b = mixer_out_bwd(
        dx2, res["y0m"], mod3, g_post, w_out, res["out_a"], res["out_b"], goa, gob, name="mixer_out_bwd")
    dqa, dka, dva = band_bwd(res["pa"], doa, res["out_a"], res["lse_a"], *tabs, name="band_bwd")
    drow = _rows_of_tiles(dvec_b[:, :NH].reshape(nb, SEQ, NH).transpose(0, 2, 1))
    dqb, dkb, dvb, dfq, dfk = fox_bwd(res["pb"], dob, res["lrow"], drow, res["fblk"], res["frow"], name="fox_bwd")
    dF = (dfq.reshape(nb, NH, SEQ) + dfk.reshape(nb, NH, SEQ)).transpose(0, 2, 1)
    dF = jnp.pad(dF, ((0, 0), (0, 0), (0, LANE - NH)))
    dflog, dbf = forget_cumsum_bwd(dF, res["flog"].reshape(nb, SEQ, LANE), b_forget_pad, name="forget_cumsum_bwd")
    dflog = dflog.reshape(T, LANE)
    dps = (dqa, dka, dva, dqb, dkb, dvb)
    dx1, dmod2, dg_pre = mixer_proj_bwd(dps, dflog, dx2, x1, mod3, g_pre, w_main, w_f, name="mixer_proj_bwd")
    g_main = matmul_tn_cols(res["hmix"], dps, 1024, name="grad_w_in")
    g_f = matmul_tn(res["hmix"], dflog.astype(BF16), D, LANE, 1024, name="grad_w_forget")
    g_out = matmul_tn(res["merged"], dy0m, D, D, 1024, name="grad_w_out")
    dmod3 = jnp.concatenate([dmod2, dmgate], axis=1)
    return dx1, dmod3, dict(g_pre=dg_pre, g_post=dg_post, goa=dgoa, gob=dgob, b_forget=dbf[:, :NH],
                            w_in=jnp.concatenate([g_main, g_f[:, :NH]], axis=1), w_out=g_out)


def ffn_grads(h, dy0, act, dgate, dup, pre, reduce=None):
    g_gate = matmul_tn(h, dgate, D, DFF_PAD, 1024, name=pre + "_grad_gate")
    if reduce is None:
        g_up = matmul_tn(h, dup, D, DFF_PAD, 1024, name=pre + "_grad_up")
        g_down = matmul_tn(act, dy0, FF_TN, D, 1024, name=pre + "_grad_down")
        return (g_gate, g_up, g_down), {}
    hs_gate = reduce("gate", g_gate)
    g_up, rb_gate = matmul_tn(h, dup, D, DFF_PAD, 1024, name=pre + "_grad_up", scatter=hs_gate)
    hs_up = reduce("up", g_up)
    g_down, rb_up = matmul_tn(act, dy0, FF_TN, D, 1024, name=pre + "_grad_down", scatter=hs_up)
    return (g_gate, g_up, g_down), {"gate": (hs_gate[0], rb_gate[0]), "up": (hs_up[0], rb_up[0])}


def local_step(x0, tgt, pos_col, mod, wfull, p, late_weights=None, last_weights=None, early_grads=None, last_reduce=None):
    T = x0.shape[0]
    nb = T // SEQ
    mod_ff1, mod_mix, mod_ff2 = mod[:, 0:3], mod[:, 3:6], mod[:, 6:9]
    tabs = rope_tables(pos_col, name="rope_tables")
    bf_pad = jnp.pad(p["b_forget"], ((0, 0), (0, LANE - NH)))

    (x1, h1, gate1, up1, y01), gathered = ffn_fwd(
        x0, mod_ff1, p["g_pre_ff1"], p["g_post_ff1"], wfull["w_ff1_gate"], wfull["w_ff1_up"], wfull["w_ff1_down"], 0.5,
        name="ff1_fwd", gather=None if late_weights is None else late_weights[:2])
    if late_weights is not None:
        wfull = {**wfull, **late_weights[2](gathered)}
    x2, res, gathered = mixer_fwd(x1, mod_mix, p["g_pre_mix"], wfull["w_main"], wfull["w_f"], bf_pad, p["g_out_a"],
                                  p["g_out_b"], wfull["w_out"], p["g_post_mix"], tabs, nb,
                                  gather=None if last_weights is None else last_weights[:2])
    if last_weights is not None:
        wfull = {**wfull, **last_weights[2](gathered)}
    (x3, h2, gate2, up2, y02), _ = ffn_fwd(x2, mod_ff2, p["g_pre_ff2"], p["g_post_ff2"], wfull["w_ff2_gate"],
                                           wfull["w_ff2_up"], wfull["w_ff2_down"], 0.5, name="ff2_fwd")

    (dx2, dy02, act2, dgate2, dup2, dmod_ff2, dgpre2, dgpost2), (loss_part,) = ffn_bwd(
        x3, x2, y02, mod_ff2, p["g_pre_ff2"], p["g_post_ff2"], gate2, up2, wfull["w_ff2_gate"], wfull["w_ff2_up"],
        wfull["w_ff2_down"], 0.5, name="ff2_bwd", target=tgt)
    gw = {}
    (gw["w_ff2_gate"], gw["w_ff2_up"], gw["w_ff2_down"]), _ = ffn_grads(h2, dy02, act2, dgate2, dup2, "ff2")
    dx1, dmod_mix, gmix = mixer_bwd(dx2, x1, mod_mix, p["g_pre_mix"], wfull["w_main"], wfull["w_f"], bf_pad, p["g_out_a"],
                                    p["g_out_b"], wfull["w_out"], p["g_post_mix"], tabs, res, nb)
    gw["w_in"], gw["w_out"] = gmix["w_in"], gmix["w_out"]
    (dx0, dy01, act1, dgate1, dup1, dmod_ff1, dgpre1, dgpost1), scattered = ffn_bwd(
        dx1, x0, y01, mod_ff1, p["g_pre_ff1"], p["g_post_ff1"], gate1, up1, wfull["w_ff1_gate"], wfull["w_ff1_up"],
        wfull["w_ff1_down"], 0.5, name="ff1_bwd", scatter=None if early_grads is None else early_grads(gw))
    (gw["w_ff1_gate"], gw["w_ff1_up"], gw["w_ff1_down"]), chained = ffn_grads(h1, dy01, act1, dgate1, dup1, "ff1", last_reduce)
    dmod = jnp.concatenate([dmod_ff1, dmod_mix, dmod_ff2], axis=1).reshape(nb, 9 * D)
    small = dict(g_pre_ff1=dgpre1, g_post_ff1=dgpost1, g_pre_mix=gmix["g_pre"], g_post_mix=gmix["g_post"], g_pre_ff2=dgpre2,
                 g_post_ff2=dgpost2, g_out_a=gmix["goa"], g_out_b=gmix["gob"], b_forget=gmix["b_forget"])
    return loss_part, dx0, dmod, gw, small, scattered, chained


def kernel(x, c, positions, w_ada, b_ada, g_pre_ff1, g_post_ff1, w_ff1_gate, w_ff1_up, w_ff1_down, g_pre_mix, g_post_mix, w_in, b_forget, g_out_a, g_out_b, w_out, g_pre_ff2, g_post_ff2, w_ff2_gate, w_ff2_up, w_ff2_down, loss_target, m_w_ada, m_b_ada, m_g_pre_ff1, m_g_post_ff1, m_w_ff1_gate, m_w_ff1_up, m_w_ff1_down, m_g_pre_mix, m_g_post_mix, m_w_in, m_b_forget, m_g_out_a, m_g_out_b, m_w_out, m_g_pre_ff2, m_g_post_ff2, m_w_ff2_gate, m_w_ff2_up, m_w_ff2_down, v_w_ada, v_b_ada, v_g_pre_ff1, v_g_post_ff1, v_w_ff1_gate, v_w_ff1_up, v_w_ff1_down, v_g_pre_mix, v_g_post_mix, v_w_in, v_b_forget, v_g_out_a, v_g_out_b, v_w_out, v_g_pre_ff2, v_g_post_ff2, v_w_ff2_gate, v_w_ff2_up, v_w_ff2_down):
    args = dict(locals())
    nb = x.shape[0]
    T = nb * SEQ
    ax, ay, ac = lax.axis_index("x"), lax.axis_index("y"), lax.axis_index("c")
    shard = 2 * ax + ay
    cidx = jnp.reshape(ac, (1,)).astype(jnp.int32)
    sidx = jnp.reshape(shard, (1,)).astype(jnp.int32)

    big = ["w_ff1_gate", "w_ff1_up", "w_ff1_down", "w_in", "w_out", "w_ff2_gate", "w_ff2_up", "w_ff2_down"]
    vecs = ["g_pre_ff1", "g_post_ff1", "g_pre_mix", "g_post_mix", "g_pre_ff2", "g_post_ff2"]

    first, late = big[:3], big[3:]
    splits = dict(zip(big, [512, 512, 352, 512, 128, 512, 512, 352]))

    def assemble(names, gathered):
        out = {}
        for n, o in zip(names, gathered):
            if n.endswith("gate") or n.endswith("up"):
                out[n] = _unshard_cols(o, DFF_PAD)
            elif n.endswith("down"):
                out[n] = jnp.pad(o.reshape(DFF, D), ((0, DFF_PAD - DFF), (0, 0)))
            elif n == "w_in":
                full = _unshard_cols(o, IN_COLS)
                out["w_main"] = full[:, :IN_MAIN]
                out["w_f"] = jnp.pad(full[:, IN_MAIN:], ((0, 0), (0, LANE - NH)))
            else:
                out[n] = o.reshape(D, D)
        return out

    wfull = assemble(first, all_gather_shards([args[n][0].astype(BF16) for n in first], [splits[n] for n in first],
                                              name="all_gather_weights"))
    def gather_plan(names):
        return ([args[n][0].astype(BF16) for n in names], [splits[n] for n in names], functools.partial(assemble, names))

    late_weights, last_weights = gather_plan(late[:2]), gather_plan(late[2:])

    ncol = w_ada.shape[2]
    c_all = all_gather8(c, name="all_gather_c").reshape(N_DEV * nb, D)
    b_loc = lax.dynamic_slice(b_ada, (0, shard * ncol), (1, ncol))
    mod_loc = ada_fwd(c_all, w_ada[0], b_loc, name="ada_fwd")
    mod_g = all_gather8(mod_loc, name="all_gather_mod")
    row0 = (4 * ax + 2 * ay + ac) * nb
    mod_rows = lax.dynamic_slice(mod_g, (0, row0, 0), (N_DEV, nb, ncol))
    mod = jnp.concatenate([mod_rows[2 * s] for s in range(N_SHARD)], axis=-1).reshape(nb, 9, D)

    small_in = dict(g_pre_ff1=g_pre_ff1, g_post_ff1=g_post_ff1, g_pre_mix=g_pre_mix, g_post_mix=g_post_mix, g_pre_ff2=g_pre_ff2,
                    g_post_ff2=g_post_ff2, g_out_a=g_out_a, g_out_b=g_out_b, b_forget=b_forget)
    def shard_blocked(n, g):
        if n.endswith("gate") or n.endswith("up"):
            return _shard_cols(g, DFF)
        if n.endswith("down"):
            return g[:DFF].reshape(N_SHARD, DFF // N_SHARD, D)
        if n == "w_in":
            return _shard_cols(g, IN_COLS)
        return g.reshape(N_SHARD, D // N_SHARD, D)

    def chip_sums(names, gw, tag):
        gsb = [shard_blocked(n, gw[n]) for n in names]
        ras = sibling_send_half(gsb, name="grad_sibling_send_" + tag)
        return [pair_sum(g, ra, cidx, name=f"grad_pair_sum_{n}") for n, g, ra in zip(names, gsb, ras)]

    hs = {}

    def early_grads(gw):
        hs.update(zip(late, chip_sums(late, gw, "late")))
        return [hs[n] for n in late]

    def last_reduce(which, g):
        return chip_sums(["w_ff1_" + which], {"w_ff1_" + which: g}, which)

    loss_part, dx0, dmod, gw, small, rbs_late, chained = local_step(
        x.reshape(T, D), loss_target.reshape(T, D), positions.reshape(T, 1), mod, wfull, small_in, late_weights, last_weights,
        early_grads, last_reduce)

    dmod_all = all_gather8(dmod, name="all_gather_dmod").reshape(N_DEV * nb, 9 * D)
    dmod_loc = lax.dynamic_slice(dmod_all, (0, shard * ncol), (N_DEV * nb, ncol))
    g_w_ada = ada_bwd(c_all, dmod_loc, name="ada_bwd")

    rbs = dict(zip(late, rbs_late))
    for which, (h, rb) in chained.items():
        hs["w_ff1_" + which], rbs["w_ff1_" + which] = h, rb
    hs["w_ff1_down"] = chip_sums(["w_ff1_down"], gw, "down")[0]
    rbs["w_ff1_down"] = chip_scatter([hs["w_ff1_down"]], name="grad_chip_scatter")[0]
    ghs = [chip_sum(hs[n], rbs[n], sidx, name=f"grad_chip_sum_{n}") for n in big]
    theirs = sibling_swap(ghs, name="grad_sibling_swap")

    row6 = jnp.concatenate([small["g_out_a"], small["g_out_b"]], axis=1)
    row7 = jnp.concatenate([small["b_forget"], loss_part[0:1, 0:1], jnp.zeros((1, D - NH - 1), F32)], axis=1)
    pack = jnp.concatenate([small[n] for n in vecs] + [row6, row7], axis=0)
    packed = all_gather8(pack, name="all_gather_small").reshape(N_DEV, 8 * D)

    def pack_state(pre):
        r6 = jnp.concatenate([args[pre + "g_out_a"], args[pre + "g_out_b"]], axis=1)
        r7 = jnp.pad(args[pre + "b_forget"], ((0, 0), (0, D - NH)))
        return jnp.concatenate([args[pre + n] for n in vecs] + [r6, r7], axis=0).reshape(1, 8 * D)

    sg, sd, sm, sv = (t.reshape(8, D) for t in vec_adam(packed, pack_state(""), pack_state("m_"), pack_state("v_"), name="adam_small"))

    def unpack(t):
        out = {n: t[i:i + 1] for i, n in enumerate(vecs)}
        out["g_out_a"], out["g_out_b"], out["b_forget"] = t[6:7, :WG], t[6:7, WG:], t[7:8, :NH]
        return out

    outs = dict(grad=unpack(sg), delta=unpack(sd), new_m=unpack(sm), new_v=unpack(sv))
    loss = sg[7, NH]
    outs["grad"]["b_ada"], outs["delta"]["b_ada"], outs["new_m"]["b_ada"], outs["new_v"]["b_ada"] = vec_adam(
        dmod_all, b_ada, m_b_ada, v_b_ada, name="adam_b_ada")

    for n, mine, other in zip(big, ghs, theirs):
        tr = 128 if mine.shape[0] % 128 == 0 else mine.shape[0]
        outs["grad"][n], outs["delta"][n], outs["new_m"][n], outs["new_v"][n] = adam_update_halves(
            args[n], mine, other, args["m_" + n], args["v_" + n], cidx, tr, name="adam_" + n)
    outs["delta"]["w_ada"], outs["new_m"]["w_ada"], outs["new_v"]["w_ada"] = adam_update(
        w_ada, g_w_ada, m_w_ada, v_w_ada, 128, name="adam_w_ada")
    outs["grad"]["w_ada"] = g_w_ada[None]

    order = ["w_ada", "b_ada", "g_pre_ff1", "g_post_ff1", "w_ff1_gate", "w_ff1_up", "w_ff1_down", "g_pre_mix", "g_post_mix", "w_in",
             "b_forget", "g_out_a", "g_out_b", "w_out", "g_pre_ff2", "g_post_ff2", "w_ff2_gate", "w_ff2_up", "w_ff2_down"]
    result = [loss, dx0.reshape(nb, SEQ, D)]
    for kind in ("grad", "delta", "new_m", "new_v"):
        result += [outs[kind][n] for n in order]
    return tuple(result)
```

```python
import functools
import math

import jax
import jax.numpy as jnp
from jax import lax
from jax.experimental import pallas as pl
from jax.experimental.pallas import tpu as pltpu

D = 1024
SEQ = 2048
HD = 64
NH = 8
WG = NH * HD
DFF = 2752
DFF_PAD = 2816
IN_MAIN = 6 * WG
IN_COLS = IN_MAIN + NH
N_SHARD = 4
N_DEV = 8
LANE = 128
QB = 128
ROWS = 256
FB = 512
FT = 512
FOX_QB = 512
FOX_PAIRS = 2
FOX_PAIRS_BWD = 2
BAND_UNROLL = 4
BAND_UNROLL_BWD = 4
PATTERNS = ((1, 16), (4, 4), (16, 1))
ROPE_THETA = 500000.0
EPS = 1e-6
NEG = -1e30
ATTN_SCALE = HD ** -0.5
TM = 512
TM_FFN = 512
TM_BWD = 256
VMEM_LIMIT = 56 * 1024 * 1024

ADAM_LR, ADAM_B1, ADAM_B2, ADAM_EPS, ADAM_WD, ADAM_STEP = 0.001, 0.9, 0.999, 1e-08, 0.01, 10

F32 = jnp.float32
BF16 = jnp.bfloat16
MESH = pl.DeviceIdType.MESH
SDS = jax.ShapeDtypeStruct


def _cp(*sem):
    return pltpu.CompilerParams(dimension_semantics=sem, vmem_limit_bytes=VMEM_LIMIT)


def _dot(a, b):
    return jnp.dot(a, b, preferred_element_type=F32)


def _dot_nt(a, b):
    return lax.dot_general(a, b, (((1,), (1,)), ((), ())), preferred_element_type=F32)


def _dot_tn(a, b):
    return lax.dot_general(a, b, (((0,), (0,)), ((), ())), preferred_element_type=F32)


def _rms(xf):
    return lax.rsqrt(jnp.mean(xf * xf, axis=-1, keepdims=True) + EPS)


def _norm_mod_bwd(dh, xf, g, scale):
    r = _rms(xf)
    xh = xf * r
    dsh = jnp.sum(dh, axis=0, keepdims=True)
    dsc = jnp.sum(dh * (xh * g), axis=0, keepdims=True)
    dn = dh * (1.0 + scale)
    dg = jnp.sum(dn * xh, axis=0, keepdims=True)
    dxh = dn * g
    dx = r * (dxh - xh * jnp.mean(dxh * xh, axis=-1, keepdims=True))
    return dx, dsh, dsc, dg


def _post_bwd(dxo, y0, g, mgate, gs):
    r = _rms(y0)
    yh = y0 * r
    dmg = gs * jnp.sum(dxo * (yh * g), axis=0, keepdims=True)
    dy = (gs * mgate) * dxo
    dg = jnp.sum(dy * yh, axis=0, keepdims=True)
    dyh = dy * g
    dy0 = r * (dyh - yh * jnp.mean(dyh * yh, axis=-1, keepdims=True))
    return dy0, dmg, dg


def _mod_map(i, *_):
    return ((i * TM) // SEQ, 0, 0)


FF_TN = 1408
FF_TILES = ((0, 768), (768, 1536), (1536, 2304), (2304, 2816))


def _resident_scratch():
    return [pltpu.VMEM((D, DFF_PAD), BF16), pltpu.VMEM((D, DFF_PAD), BF16), pltpu.VMEM((DFF_PAD, D), BF16),
            pltpu.SemaphoreType.DMA((3,))]


def _load_resident(first_step, srcs, dsts, sems):
    @pl.when(first_step)
    def _():
        cps = [pltpu.make_async_copy(s, d, sems.at[k]) for k, (s, d) in enumerate(zip(srcs, dsts))]
        for cp in cps:
            cp.start()
        for cp in cps:
            cp.wait()


def ffn_fwd(x, mod3, g_pre, g_post, wg, wu, wd, gs, name, gather=None):
    T = x.shape[0]
    tm = TM_FFN
    ng = 0 if gather is None else len(gather[0])
    plan = None if gather is None else ShardGather([w.shape for w in gather[0]], gather[1])

    def body(*refs):
        x_ref, mod_ref, gpre_ref, gpost_ref = refs[:4]
        xo_ref, h_ref, gate_ref, up_ref, y0_ref = refs[7 + ng:12 + ng]
        wg_ref, wu_ref, wd_ref, wsem = refs[12 + 2 * ng:16 + 2 * ng]
        i = pl.program_id(0)
        if plan is not None:
            comm = (refs[7:7 + ng], refs[12 + ng:12 + 2 * ng], refs[16 + 2 * ng:])
            pl.when(i == 0)(lambda: plan.start(*comm))
        _load_resident(i == 0, refs[4:7], (wg_ref, wu_ref, wd_ref), wsem)

        xf = x_ref[...]
        hb = ((xf * _rms(xf) * gpre_ref[...]) * (1.0 + mod_ref[0, 1:2, :]) + mod_ref[0, 0:1, :]).astype(BF16)
        h_ref[...] = hb
        y0 = None
        for lo, hi in FF_TILES:
            gate = _dot(hb, wg_ref[:, lo:hi])
            up = _dot(hb, wu_ref[:, lo:hi])
            gate_ref[:, lo:hi] = gate.astype(BF16)
            up_ref[:, lo:hi] = up.astype(BF16)
            part = _dot((gate * jax.nn.sigmoid(gate) * up).astype(BF16), wd_ref[lo:hi, :])
            y0 = part if y0 is None else y0 + part
        y0_ref[...] = y0
        xo_ref[...] = xf + (gs * mod_ref[0, 2:3, :]) * (y0 * _rms(y0) * gpost_ref[...])

        if plan is not None:
            pl.when(i == T // tm - 1)(lambda: plan.finish(*comm))

    tok = pl.BlockSpec((tm, D), lambda i: (i, 0))
    vec = pl.BlockSpec((1, D), lambda i: (0, 0))
    hid = pl.BlockSpec((tm, DFF_PAD), lambda i: (i, 0))
    outs = pl.pallas_call(
        body, grid=(T // tm,),
        in_specs=[tok, pl.BlockSpec((1, 3, D), lambda i: ((i * tm) // SEQ, 0, 0)), vec, vec, HBM, HBM, HBM] + [HBM] * ng,
        out_specs=[tok, tok, hid, hid, tok] + [HBM] * ng,
        out_shape=[SDS((T, D), F32), SDS((T, D), BF16), SDS((T, DFF_PAD), BF16), SDS((T, DFF_PAD), BF16), SDS((T, D), F32)]
        + ([] if plan is None else plan.out_shapes(BF16)),
        scratch_shapes=_resident_scratch() + ([] if plan is None else plan.scratch()),
        compiler_params=_cp("arbitrary"), name=name,
    )(x, mod3, g_pre, g_post, wg, wu, wd, *([] if gather is None else gather[0]))
    return outs[:5], outs[5:]


def ffn_bwd(dxo, x, y0, mod3, g_pre, g_post, gate, up, wg, wu, wd, gs, name, scatter=None, target=None):
    assert scatter is None or target is None
    T = x.shape[0]
    nb = T // SEQ
    tm = TM_BWD
    tiles_per_seq = SEQ // tm
    ns = 0 if scatter is None else len(scatter)
    ne = ns + (target is not None)

    def body(*refs):
        dxo_ref, x_ref, y0_ref, mod_ref, gpre_ref, gpost_ref, gate_ref, up_ref = refs[:8]
        dx_ref, dy0_ref, act_ref, dgate_ref, dup_ref, dmod_ref, dgpre_ref, dgpost_ref = refs[11 + ne:19 + ne]
        wg_ref, wu_ref, wd_ref, wsem = refs[19 + 2 * ne:23 + 2 * ne]
        i = pl.program_id(0)
        _load_resident(i == 0, refs[8:11], (wg_ref, wu_ref, wd_ref), wsem)
        if ns:
            comm = (refs[11:11 + ns], refs[19 + ns:19 + 2 * ns], *refs[23 + 2 * ns:])

            @pl.when(i == 0)
            def _():
                for cp in _scatter_copies(*comm):
                    cp.start()

        @pl.when(i == 0)
        def _():
            dgpre_ref[...] = jnp.zeros_like(dgpre_ref)
            dgpost_ref[...] = jnp.zeros_like(dgpost_ref)

        @pl.when(i % tiles_per_seq == 0)
        def _():
            dmod_ref[...] = jnp.zeros_like(dmod_ref)

        dxo = dxo_ref[...]
        if target is not None:
            loss_ref = refs[19 + ne]

            @pl.when(i == 0)
            def _():
                loss_ref[...] = jnp.zeros_like(loss_ref)

            err = dxo - refs[11][...]
            loss_ref[...] += jnp.sum(err * err) * (0.5 / D)
            dxo = err * (1.0 / D)
        dy0, dmg, dg = _post_bwd(dxo, y0_ref[...], gpost_ref[...], mod_ref[0, 2:3, :], gs)
        dmod_ref[0, 2:3, :] += dmg
        dgpost_ref[...] += dg
        db = dy0.astype(BF16)
        dy0_ref[...] = db
        dh = None
        for lo, hi in FF_TILES:
            dact = _dot_nt(db, wd_ref[lo:hi, :])
            g = gate_ref[:, lo:hi].astype(F32)
            u = up_ref[:, lo:hi].astype(F32)
            sig = jax.nn.sigmoid(g)
            sl = g * sig
            dgate = (dact * u * (sig * (1.0 + g * (1.0 - sig)))).astype(BF16)
            dup = (dact * sl).astype(BF16)
            act_ref[:, lo:hi] = (sl * u).astype(BF16)
            dgate_ref[:, lo:hi] = dgate
            dup_ref[:, lo:hi] = dup
            part = _dot_nt(dgate, wg_ref[:, lo:hi]) + _dot_nt(dup, wu_ref[:, lo:hi])
            dh = part if dh is None else dh + part
        dx, dsh, dsc, dg = _norm_mod_bwd(dh, x_ref[...], gpre_ref[...], mod_ref[0, 1:2, :])
        dx_ref[...] = dxo + dx
        dmod_ref[0, 0:1, :] += dsh
        dmod_ref[0, 1:2, :] += dsc
        dgpre_ref[...] += dg

        if ns:
            @pl.when(i == T // tm - 1)
            def _():
                for cp in _scatter_copies(*comm):
                    cp.wait()

    tok = pl.BlockSpec((tm, D), lambda i: (i, 0))
    vec = pl.BlockSpec((1, D), lambda i: (0, 0))
    hid = pl.BlockSpec((tm, DFF_PAD), lambda i: (i, 0))
    modspec = pl.BlockSpec((1, 3, D), lambda i: ((i * tm) // SEQ, 0, 0))
    outs = pl.pallas_call(
        body, grid=(T // tm,),
        in_specs=[tok, tok, tok, modspec, vec, vec, hid, hid, HBM, HBM, HBM] + [HBM] * ns + [tok] * (ne - ns),
        out_specs=[tok, tok, hid, hid, hid, modspec, vec, vec] + [HBM] * ns
        + [pl.BlockSpec((8, LANE), lambda i: (0, 0))] * (ne - ns),
        out_shape=[SDS((T, D), F32), SDS((T, D), BF16), SDS((T, DFF_PAD), BF16), SDS((T, DFF_PAD), BF16),
                   SDS((T, DFF_PAD), BF16), SDS((nb, 3, D), F32), SDS((1, D), F32), SDS((1, D), F32)]
        + [SDS((3,) + h.shape[1:], h.dtype) for h in (scatter or [])] + [SDS((8, LANE), F32)] * (ne - ns),
        scratch_shapes=_resident_scratch()
        + ([pltpu.SemaphoreType.DMA((ns, 3)), pltpu.SemaphoreType.DMA((ns, 3))] if ns else []),
        compiler_params=_cp("arbitrary"), name=name,
    )(dxo, x, y0, mod3, g_pre, g_post, gate, up, wg, wu, wd, *(scatter or []), *([] if target is None else [target]))
    return outs[:8], outs[8:]


def matmul_tn(a, b, tm, tn, tk, name, scatter=None):
    T, M = a.shape
    N = b.shape[1]
    grid = (M // tm, N // tn, T // tk)
    ns = 0 if scatter is None else len(scatter)

    def body(*refs):
        a_ref, b_ref = refs[:2]
        o_ref = refs[2 + ns]
        ids = [pl.program_id(ax) for ax in range(3)]
        if ns:
            comm = (refs[2:2 + ns], refs[3 + ns:3 + 2 * ns], *refs[3 + 2 * ns:])

            @pl.when((ids[0] == 0) & (ids[1] == 0) & (ids[2] == 0))
            def _():
                for cp in _scatter_copies(*comm):
                    cp.start()

        @pl.when(ids[2] == 0)
        def _():
            o_ref[...] = jnp.zeros_like(o_ref)

        o_ref[...] += _dot_tn(a_ref[...], b_ref[...])

        if ns:
            @pl.when((ids[0] == grid[0] - 1) & (ids[1] == grid[1] - 1) & (ids[2] == grid[2] - 1))
            def _():
                for cp in _scatter_copies(*comm):
                    cp.wait()

    outs = pl.pallas_call(
        body, grid=grid,
        in_specs=[pl.BlockSpec((tk, tm), lambda i, j, k: (k, i)), pl.BlockSpec((tk, tn), lambda i, j, k: (k, j))] + [HBM] * ns,
        out_specs=[pl.BlockSpec((tm, tn), lambda i, j, k: (i, j))] + [HBM] * ns,
        out_shape=[SDS((M, N), F32)] + [SDS((3,) + h.shape[1:], h.dtype) for h in (scatter or [])],
        scratch_shapes=[pltpu.SemaphoreType.DMA((ns, 3)), pltpu.SemaphoreType.DMA((ns, 3))] if ns else [],
        compiler_params=_cp("arbitrary", "arbitrary", "arbitrary"), name=name,
    )(a, b, *(scatter or []))
    return outs[0] if scatter is None else (outs[0], outs[1:])


def matmul_tn_cols(a, bs, tk, name):
    T, M = a.shape
    n = bs[0].shape[1]
    ng = len(bs)

    def body(*refs):
        a_ref, b_refs, o_ref = refs[0], refs[1:1 + ng], refs[1 + ng]

        @pl.when(pl.program_id(0) == 0)
        def _():
            o_ref[...] = jnp.zeros_like(o_ref)

        av = a_ref[...]
        for g, b_ref in enumerate(b_refs):
            o_ref[:, g * n:(g + 1) * n] += _dot_tn(av, b_ref[...])

    return pl.pallas_call(
        body, grid=(T // tk,),
        in_specs=[pl.BlockSpec((tk, M), lambda k: (k, 0))] + [pl.BlockSpec((tk, n), lambda k: (k, 0))] * ng,
        out_specs=pl.BlockSpec((M, ng * n), lambda k: (0, 0)), out_shape=SDS((M, ng * n), F32),
        compiler_params=_cp("arbitrary"), name=name,
    )(a, *bs)


def rope_tables(pos_col, name):
    T = pos_col.shape[0]
    tm = 1024

    def body(p_ref, c_ref, s1_ref, s2_ref):
        lane = lax.broadcasted_iota(jnp.int32, (1, LANE), 1)
        l64 = lane % HD
        inv_freq = jnp.exp((l64 % 8).astype(F32) * (-math.log(ROPE_THETA) / 8.0))
        ang = p_ref[...].astype(F32) * inv_freq
        cs = jnp.cos(ang)
        sn = jnp.sin(ang)
        c_ref[...] = jnp.where(l64 < 16, cs, 1.0)
        s1_ref[...] = jnp.where(l64 < 8, -sn, 0.0)
        s2_ref[...] = jnp.where((l64 >= 8) & (l64 < 16), sn, 0.0)

    tab = pl.BlockSpec((tm, LANE), lambda i: (i, 0))
    return pl.pallas_call(
        body, grid=(T // tm,), in_specs=[pl.BlockSpec((tm, 1), lambda i: (i, 0))], out_specs=[tab, tab, tab],
        out_shape=[SDS((T, LANE), F32)] * 3, compiler_params=_cp("arbitrary"), name=name,
    )(pos_col)


def mixer_proj(x, mod3, g_pre, w_main, w_f, rc, rs1, rs2, name):
    T = x.shape[0]

    def body(x_ref, mod_ref, g_ref, w_ref, wf_ref, c_ref, s1_ref, s2_ref, h_ref, pa_ref, pb_ref, f_ref):
        xf = x_ref[...]
        h = (xf * _rms(xf) * g_ref[...]) * (1.0 + mod_ref[0, 1:2, :]) + mod_ref[0, 0:1, :]
        hb = h.astype(BF16)
        h_ref[...] = hb
        f_ref[...] = _dot(hb, wf_ref[...])
        c, s1, s2 = c_ref[...], s1_ref[...], s2_ref[...]
        for grp in range(2):
            pr = _dot(hb, w_ref[:, grp * WG:(grp + 1) * WG])
            for k in range(WG // LANE):
                t = pr[:, k * LANE:(k + 1) * LANE]
                pa_ref[:, grp * WG + k * LANE:grp * WG + (k + 1) * LANE] = (
                    t * c + pltpu.roll(t, LANE - 8, 1) * s1 + pltpu.roll(t, 8, 1) * s2)
        pa_ref[:, 2 * WG:3 * WG] = _dot(hb, w_ref[:, 2 * WG:3 * WG])
        for grp in range(3):
            pb_ref[:, grp * WG:(grp + 1) * WG] = _dot(hb, w_ref[:, (3 + grp) * WG:(4 + grp) * WG]).astype(BF16)

    tok = pl.BlockSpec((TM, D), lambda i: (i, 0))
    vec = pl.BlockSpec((1, D), lambda i: (0, 0))
    tab = pl.BlockSpec((TM, LANE), lambda i: (i, 0))
    grp3 = pl.BlockSpec((TM, 3 * WG), lambda i: (i, 0))
    return pl.pallas_call(
        body, grid=(T // TM,),
        in_specs=[tok, pl.BlockSpec((1, 3, D), _mod_map), vec, pl.BlockSpec((D, IN_MAIN), lambda i: (0, 0)),
                  pl.BlockSpec((D, LANE), lambda i: (0, 0)), tab, tab, tab],
        out_specs=[tok, grp3, grp3, tab],
        out_shape=[SDS((T, D), BF16), SDS((T, 3 * WG), F32), SDS((T, 3 * WG), BF16), SDS((T, LANE), F32)],
        compiler_params=_cp("arbitrary"), name=name,
    )(x, mod3, g_pre, w_main, w_f, rc, rs1, rs2)


def _head_lanes():
    return lax.broadcasted_iota(jnp.int32, (1, LANE), 1) < HD


def _pair(m0, a, b):
    return jnp.where(m0, a, b)


def _band_rows(i, d, nbc):
    if nbc == 1:
        return i, i, 0
    r, mb = i // nbc, i % nbc
    return r + mb * (QB * d), r + jnp.maximum(mb - 1, 0) * (QB * d), jnp.where(mb > 0, QB, 0)


def _rows(start, size, d):
    return pl.ds(pl.multiple_of(start, QB), size) if d == 1 else pl.ds(start, size, stride=d)


def _band_valid(span, off):
    rq = lax.broadcasted_iota(jnp.int32, (QB, span), 0)
    rel = lax.broadcasted_iota(jnp.int32, (QB, span), 1) - off
    return (rel <= rq) & (rel >= rq - QB)


def band_fwd(pa, name):
    T = pa.shape[0]
    B = T // SEQ
    NP = WG // LANE

    def body(q_ref, k_ref, v_ref, out_ref, lse_ref, o_s, l_s):
        m0 = _head_lanes()
        for pidx, (d, nbc) in enumerate(PATTERNS):
            span = QB if nbc == 1 else 2 * QB

            def blk(it, carry, pidx=pidx, d=d, nbc=nbc, span=span):
                ld = []
                for u in range(BAND_UNROLL):
                    qs, ks, off = _band_rows(it * BAND_UNROLL + u, d, nbc)
                    q = q_ref[_rows(qs, QB, d), :] * ATTN_SCALE
                    ld.append((qs, q, k_ref[_rows(ks, span, d), :].astype(BF16), v_ref[_rows(ks, span, d), :].astype(BF16),
                               _band_valid(span, off)))
                ss = [[jnp.where(valid, _dot_nt(jnp.where(mh, q, 0.0).astype(BF16), k), NEG) for mh in (m0, jnp.logical_not(m0))]
                      for _, q, k, _, valid in ld]
                ps = []
                for pair in ss:
                    row = []
                    for s in pair:
                        m = jnp.max(s, axis=-1, keepdims=True)
                        p = jnp.exp(s - m)
                        row.append((p.astype(BF16), jnp.sum(p, axis=-1, keepdims=True), m))
                    ps.append(row)
                pv = [[_dot(p, ld[u][3]) for p, _, _ in ps[u]] for u in range(BAND_UNROLL)]
                for u in range(BAND_UNROLL):
                    rows = _rows(ld[u][0], QB, d)
                    (_, l0, mx0), (_, l1, mx1) = ps[u]
                    o_s[pidx, rows, :] = _pair(m0, pv[u][0] / l0, pv[u][1] / l1)
                    l_s[pidx, rows, :] = _pair(m0, mx0 + jnp.log(l0), mx1 + jnp.log(l1))
                return carry

            lax.fori_loop(0, SEQ // QB // BAND_UNROLL, blk, 0)
        for c in range(SEQ // ROWS):
            sl = slice(c * ROWS, (c + 1) * ROWS)
            a, b, e = l_s[0, sl, :], l_s[1, sl, :], l_s[2, sl, :]
            m = jnp.maximum(jnp.maximum(a, b), e)
            L = m + jnp.log(jnp.exp(a - m) + jnp.exp(b - m) + jnp.exp(e - m))
            out_ref[sl, :] = jnp.exp(a - L) * o_s[0, sl, :] + jnp.exp(b - L) * o_s[1, sl, :] + jnp.exp(e - L) * o_s[2, sl, :]
            lse_ref[sl, :] = L

    blk_of = lambda g: pl.BlockSpec((SEQ, LANE), lambda b, hp, g=g: (b, g * NP + hp))
    return pl.pallas_call(
        body, grid=(B, NP), in_specs=[blk_of(0), blk_of(1), blk_of(2)], out_specs=[blk_of(0), blk_of(0)],
        out_shape=[SDS((T, WG), F32), SDS((T, WG), F32)],
        scratch_shapes=[pltpu.VMEM((3, SEQ, LANE), F32), pltpu.VMEM((3, SEQ, LANE), F32)],
        compiler_params=_cp("arbitrary", "arbitrary"), name=name,
    )(pa, pa, pa)


def _pair_rowsum(m0, prod):
    s0 = jnp.sum(jnp.where(m0, prod, 0.0), axis=-1, keepdims=True)
    return _pair(m0, s0, jnp.sum(prod, axis=-1, keepdims=True) - s0)


def band_bwd(pa, do, out, lse, rc, rs1, rs2, name):
    T = pa.shape[0]
    B = T // SEQ
    NP = WG // LANE

    def body(q_ref, k_ref, v_ref, do_ref, out_ref, l_ref, c_ref, s1_ref, s2_ref, dqo_ref, dko_ref, dvo_ref, d_s, dq_ref, dk_ref,
             dv_ref):
        m0 = _head_lanes()
        dq_ref[...] = jnp.zeros_like(dq_ref)
        dk_ref[...] = jnp.zeros_like(dk_ref)
        dv_ref[...] = jnp.zeros_like(dv_ref)
        for c in range(SEQ // ROWS):
            sl = slice(c * ROWS, (c + 1) * ROWS)
            d_s[sl, :] = _pair_rowsum(m0, do_ref[sl, :] * out_ref[sl, :])
        for d, nbc in PATTERNS:
            span = QB if nbc == 1 else 2 * QB

            def blk(it, carry, d=d, nbc=nbc, span=span):
                masks = (m0, jnp.logical_not(m0))
                ld = []
                for u in range(BAND_UNROLL_BWD):
                    qs, ks, off = _band_rows(it * BAND_UNROLL_BWD + u, d, nbc)
                    qrow, krow = _rows(qs, QB, d), _rows(ks, span, d)
                    ld.append(dict(qrow=qrow, krow=krow, q=q_ref[qrow, :] * ATTN_SCALE, k=k_ref[krow, :].astype(BF16),
                                   v=v_ref[krow, :].astype(BF16), do=do_ref[qrow, :], l=l_ref[qrow, :], dv=d_s[qrow, :],
                                   valid=_band_valid(span, off)))
                for t in ld:
                    t["qm"] = [jnp.where(mh, t["q"], 0.0).astype(BF16) for mh in masks]
                    t["dom"] = [jnp.where(mh, t["do"], 0.0).astype(BF16) for mh in masks]
                sd = [[(jnp.where(t["valid"], _dot_nt(t["qm"][h], t["k"]), NEG), _dot_nt(t["dom"][h], t["v"])) for h in range(2)]
                      for t in ld]
                pd = []
                for t, pair in zip(ld, sd):
                    row = []
                    for h, (s, dp) in enumerate(pair):
                        col = slice(h * HD, h * HD + 1)
                        p = jnp.exp(s - t["l"][:, col])
                        row.append((p.astype(BF16), (p * (dp - t["dv"][:, col])).astype(BF16)))
                    pd.append(row)
                gr = [(_dot(row[0][1], t["k"]), _dot(row[1][1], t["k"]),
                       _dot_tn(jnp.concatenate([row[0][1], row[1][1]], axis=0), jnp.concatenate(t["qm"], axis=0)),
                       _dot_tn(jnp.concatenate([row[0][0], row[1][0]], axis=0), jnp.concatenate(t["dom"], axis=0)))
                      for t, row in zip(ld, pd)]
                for t, (dq0, dq1, dk, dv) in zip(ld, gr):
                    dq_ref[t["qrow"], :] += _pair(m0, dq0, dq1) * ATTN_SCALE
                    dk_ref[t["krow"], :] += dk
                    dv_ref[t["krow"], :] += dv
                return carry

            lax.fori_loop(0, SEQ // QB // BAND_UNROLL_BWD, blk, 0)
        for c in range(SEQ // ROWS):
            sl = slice(c * ROWS, (c + 1) * ROWS)
            cc, s1, s2 = c_ref[sl, :], s1_ref[sl, :], s2_ref[sl, :]
            for acc, o_ref in ((dq_ref, dqo_ref), (dk_ref, dko_ref)):
                d = acc[sl, :]
                o_ref[sl, :] = (d * cc + pltpu.roll(d * s1, 8, 1) + pltpu.roll(d * s2, LANE - 8, 1)).astype(BF16)
            dvo_ref[sl, :] = dv_ref[sl, :].astype(BF16)

    blk_of = lambda g: pl.BlockSpec((SEQ, LANE), lambda b, hp, g=g: (b, g * NP + hp))
    tab = pl.BlockSpec((SEQ, LANE), lambda b, hp: (b, 0))
    return pl.pallas_call(
        body, grid=(B, NP), in_specs=[blk_of(0), blk_of(1), blk_of(2), blk_of(0), blk_of(0), blk_of(0), tab, tab, tab],
        out_specs=[blk_of(0)] * 3, out_shape=[SDS((T, WG), BF16)] * 3,
        scratch_shapes=[pltpu.VMEM((SEQ, LANE), F32)] * 4,
        compiler_params=_cp("arbitrary", "arbitrary"), name=name,
    )(pa, pa, pa, do, out, lse, rc, rs1, rs2)


def _tile_causal(nq, nk, q0, k0):
    r = lax.broadcasted_iota(jnp.int32, (nq, nk), 0)
    c = lax.broadcasted_iota(jnp.int32, (nq, nk), 1)
    return r + (q0 - k0) >= c


def _row_to_col(row):
    n = row.shape[1]
    return jnp.transpose(jnp.broadcast_to(row, (LANE, n)))[:, 0:1]


def _col_to_row(col):
    n = col.shape[0]
    return jnp.transpose(jnp.broadcast_to(col, (n, LANE)))[0:1, :]


def fox_fwd(pb, fblk, frow, name, gather=None):
    FQ = FOX_QB
    T = pb.shape[0]
    B = T // SEQ
    NG = WG // (LANE * FOX_PAIRS)
    NHS = 2 * FOX_PAIRS
    W = LANE * FOX_PAIRS
    n = SEQ // FQ
    ng = 0 if gather is None else len(gather[0])
    plan = None if gather is None else ShardGather([w.shape for w in gather[0]], gather[1])

    def body(*refs):
        q_ref, k_ref, v_ref, fc_ref, fr_ref = refs[:5]
        o_ref, lse_ref = refs[5 + ng:7 + ng]
        if plan is not None:
            comm = (refs[5:5 + ng], refs[7 + ng:7 + 2 * ng], refs[7 + 2 * ng:])
            ids = [pl.program_id(ax) for ax in range(3)]
            pl.when((ids[0] == 0) & (ids[1] == 0) & (ids[2] == 0))(lambda: plan.start(*comm))
        i = pl.program_id(2)
        m0 = _head_lanes()
        masks = (m0, jnp.logical_not(m0))
        heads = [(hh, slice((hh // 2) * LANE, (hh // 2 + 1) * LANE), masks[hh % 2]) for hh in range(NHS)]
        qh, fq = [], []
        for hh, lanes, mh in heads:
            q = q_ref[:, lanes] * ATTN_SCALE
            qh.append(jnp.where(mh, q, jnp.zeros_like(q)))
            fq.append(_row_to_col(fc_ref[0, hh, 0]))

        def step(t, carry, masked):
            rows = pl.ds(pl.multiple_of(t * FT, FT), FT)
            ss = [_dot_nt(qh[hh], k_ref[rows, lanes]) + fq[hh] - fr_ref[0, hh, t] for hh, lanes, _ in heads]
            if masked:
                ok = _tile_causal(FQ, FT, i * FQ, t * FT)
                ss = [jnp.where(ok, s, NEG) for s in ss]
            st = []
            for hh, _, _ in heads:
                m2 = jnp.maximum(carry[hh][0], jnp.max(ss[hh], axis=-1, keepdims=True))
                st.append((m2, jnp.exp(carry[hh][0] - m2), jnp.exp(ss[hh] - m2).astype(BF16)))
            pv = []
            for hh, lanes, mh in heads:
                vt = v_ref[rows, lanes]
                pv.append(_dot(st[hh][2], jnp.where(mh, vt, jnp.ones_like(vt))))
            return tuple((st[hh][0], st[hh][1] * carry[hh][1] + pv[hh]) for hh in range(NHS))

        one = (jnp.full((FQ, 1), NEG, F32), jnp.zeros((FQ, LANE), F32))
        last = (i * FQ) // FT
        carry = lax.fori_loop(0, last, lambda t, cr: step(t, cr, False), (one,) * NHS)
        carry = step(last, carry, True)
        for pr in range(FOX_PAIRS):
            (ma, acca), (mb, accb) = carry[2 * pr], carry[2 * pr + 1]
            la, lb = acca[:, HD:HD + 1], accb[:, 0:1]
            o_ref[:, pr * LANE:(pr + 1) * LANE] = _pair(m0, acca / la, accb / lb)
            lse_ref[0, 2 * pr, 0] = _col_to_row(ma + jnp.log(la))
            lse_ref[0, 2 * pr + 1, 0] = _col_to_row(mb + jnp.log(lb))
        if plan is not None:
            pl.when((ids[0] == B - 1) & (ids[1] == NG - 1) & (ids[2] == n - 1))(lambda: plan.finish(*comm))

    qblk = pl.BlockSpec((FQ, W), lambda b, g, i: (b * n + i, g))
    full = lambda grp: pl.BlockSpec((SEQ, W), lambda b, g, i, grp=grp: (b, grp * NG + g))
    rowb = pl.BlockSpec((1, NHS, 1, 1, FQ), lambda b, g, i: (b, g, i, 0, 0))
    outs = pl.pallas_call(
        body, grid=(B, NG, n),
        in_specs=[qblk, full(1), full(2), rowb, pl.BlockSpec((1, NHS, SEQ // FT, 1, FT), lambda b, g, i: (b, g, 0, 0, 0))]
        + [HBM] * ng,
        out_specs=[qblk, rowb] + [HBM] * ng,
        out_shape=[SDS((T, WG), F32), SDS((B, NH, n, 1, FQ), F32)] + ([] if plan is None else plan.out_shapes(BF16)),
        scratch_shapes=[] if plan is None else plan.scratch(),
        compiler_params=_cp("arbitrary", "arbitrary", "arbitrary"), name=name,
    )(pb, pb, pb, fblk, frow, *([] if gather is None else gather[0]))
    return outs[:2], outs[2:]


def fox_bwd(pb, do, lrow, drow, fblk, frow, name):
    T = pb.shape[0]
    B = T // SEQ
    PAIRS = FOX_PAIRS_BWD
    NG = WG // (LANE * PAIRS)
    NHS = 2 * PAIRS
    W = LANE * PAIRS
    n = SEQ // FB

    def body(q_ref, k_ref, v_ref, do_ref, l_ref, d_ref, fc_ref, fr_ref, dqo_ref, dk_ref, dv_ref, dfq_ref, dfk_ref, dq_ref):
        j = pl.program_id(2)
        m0 = _head_lanes()
        masks = (m0, jnp.logical_not(m0))
        heads = [(hh, slice((hh // 2) * LANE, (hh // 2 + 1) * LANE), masks[hh % 2]) for hh in range(NHS)]

        @pl.when(j == 0)
        def _():
            dq_ref[...] = jnp.zeros_like(dq_ref)
            dfq_ref[...] = jnp.zeros_like(dfq_ref)

        kj = [k_ref[:, lanes] for _, lanes, _ in heads]
        vj = [v_ref[:, lanes] for _, lanes, _ in heads]
        fk = [_row_to_col(fc_ref[0, hh, 0]) for hh in range(NHS)]

        def step(t, carry, masked):
            rows = pl.ds(pl.multiple_of(t * FT, FT), FT)
            qm, dom = [], []
            for _, lanes, mh in heads:
                qt = q_ref[rows, lanes] * ATTN_SCALE
                qm.append(jnp.where(mh, qt, jnp.zeros_like(qt)))
                dom.append(jnp.where(mh, do_ref[rows, lanes], 0.0).astype(BF16))
            ss = [_dot_nt(kj[hh], qm[hh]) + fr_ref[0, hh, t] - fk[hh] for hh in range(NHS)]
            dps = [_dot_nt(vj[hh], dom[hh]) for hh in range(NHS)]
            if masked:
                key = lax.broadcasted_iota(jnp.int32, (FB, FT), 0)
                qry = lax.broadcasted_iota(jnp.int32, (FB, FT), 1)
                ok = qry + (t * FT - j * FB) >= key
                ss = [jnp.where(ok, s, NEG) for s in ss]
            pds = []
            for hh in range(NHS):
                p = jnp.exp(ss[hh] - l_ref[0, hh, t])
                ds = p * (dps[hh] - d_ref[0, hh, t])
                dfq_ref[0, hh, t] += jnp.sum(ds, axis=0, keepdims=True)
                pds.append((p.astype(BF16), ds.astype(BF16), jnp.sum(ds, axis=-1, keepdims=True)))
            dks = [_dot(pds[hh][1], qm[hh]) for hh in range(NHS)]
            dvs = [_dot(pds[hh][0], dom[hh]) for hh in range(NHS)]
            dqs = [_dot_tn(pds[hh][1], kj[hh]) for hh in range(NHS)]
            for pr in range(PAIRS):
                dq_ref[rows, pr * LANE:(pr + 1) * LANE] += _pair(m0, dqs[2 * pr], dqs[2 * pr + 1]) * ATTN_SCALE
            return tuple((carry[hh][0] + dks[hh], carry[hh][1] + dvs[hh], carry[hh][2] - pds[hh][2]) for hh in range(NHS))

        one = (jnp.zeros((FB, LANE), F32), jnp.zeros((FB, LANE), F32), jnp.zeros((FB, 1), F32))
        first = (j * FB) // FT
        carry = step(first, (one,) * NHS, True)
        carry = lax.fori_loop(first + 1, SEQ // FT, lambda t, cr: step(t, cr, False), carry)
        for pr in range(PAIRS):
            (dka, dva, dfka), (dkb, dvb, dfkb) = carry[2 * pr], carry[2 * pr + 1]
            dk_ref[:, pr * LANE:(pr + 1) * LANE] = _pair(m0, dka, dkb).astype(BF16)
            dv_ref[:, pr * LANE:(pr + 1) * LANE] = _pair(m0, dva, dvb).astype(BF16)
            dfk_ref[0, 2 * pr, 0] = _col_to_row(dfka)
            dfk_ref[0, 2 * pr + 1, 0] = _col_to_row(dfkb)

        @pl.when(j == n - 1)
        def _():
            dqo_ref[...] = dq_ref[...].astype(BF16)

    kblk = lambda grp: pl.BlockSpec((FB, W), lambda b, g, j, grp=grp: (b * n + j, grp * NG + g))
    full = pl.BlockSpec((SEQ, W), lambda b, g, j: (b, g))
    rowf = pl.BlockSpec((1, NHS, SEQ // FT, 1, FT), lambda b, g, j: (b, g, 0, 0, 0))
    rowb = pl.BlockSpec((1, NHS, 1, 1, FB), lambda b, g, j: (b, g, j, 0, 0))
    return pl.pallas_call(
        body, grid=(B, NG, n), in_specs=[full, kblk(1), kblk(2), full, rowf, rowf, rowb, rowf],
        out_specs=[full, kblk(0), kblk(0), rowf, rowb],
        out_shape=[SDS((T, WG), BF16), SDS((T, WG), BF16), SDS((T, WG), BF16), SDS((B, NH, SEQ // FT, 1, FT), F32),
                   SDS((B, NH, n, 1, FB), F32)],
        scratch_shapes=[pltpu.VMEM((SEQ, W), F32)],
        compiler_params=_cp("arbitrary", "arbitrary", "arbitrary"), name=name,
    )(pb, pb, pb, do, lrow, drow, fblk, frow)


def _tri(lower):
    r = lax.broadcasted_iota(jnp.int32, (LANE, LANE), 0)
    c = lax.broadcasted_iota(jnp.int32, (LANE, LANE), 1)
    return ((r >= c) if lower else (r <= c)).astype(F32)


def _tri_dot(t, xblk):
    return jnp.dot(t, xblk, precision=lax.Precision.HIGHEST, preferred_element_type=F32)


def forget_cumsum(flog, bias, name):
    B, S, _ = flog.shape

    def body(f_ref, b_ref, o_ref):
        t = _tri(True)
        carry = jnp.zeros((1, LANE), F32)
        for blk in range(S // LANE):
            z = f_ref[0, blk * LANE:(blk + 1) * LANE, :] + b_ref[...]
            lf = jnp.minimum(z, 0.0) - jnp.log(1.0 + jnp.exp(-jnp.abs(z)))
            cs = _tri_dot(t, lf) + carry
            o_ref[0, blk * LANE:(blk + 1) * LANE, :] = cs
            carry = cs[LANE - 1:LANE, :]

    spec = pl.BlockSpec((1, S, LANE), lambda b: (b, 0, 0))
    return pl.pallas_call(
        body, grid=(B,), in_specs=[spec, pl.BlockSpec((1, LANE), lambda b: (0, 0))], out_specs=spec,
        out_shape=SDS((B, S, LANE), F32), compiler_params=_cp("arbitrary"), name=name,
    )(flog, bias)


def forget_cumsum_bwd(dF, flog, bias, name):
    B, S, _ = flog.shape

    def body(d_ref, f_ref, b_ref, o_ref, db_ref):
        @pl.when(pl.program_id(0) == 0)
        def _():
            db_ref[...] = jnp.zeros_like(db_ref)

        t = _tri(False)
        carry = jnp.zeros((1, LANE), F32)
        tot = jnp.zeros((1, LANE), F32)
        for blk in reversed(range(S // LANE)):
            sl = slice(blk * LANE, (blk + 1) * LANE)
            rc = _tri_dot(t, d_ref[0, sl, :]) + carry
            carry = rc[0:1, :]
            z = f_ref[0, sl, :] + b_ref[...]
            dz = rc * jax.nn.sigmoid(-z)
            o_ref[0, sl, :] = dz
            tot = tot + jnp.sum(dz, axis=0, keepdims=True)
        db_ref[...] += tot

    spec = pl.BlockSpec((1, S, LANE), lambda b: (b, 0, 0))
    vec = pl.BlockSpec((1, LANE), lambda b: (0, 0))
    return pl.pallas_call(
        body, grid=(B,), in_specs=[spec, spec, vec], out_specs=[spec, vec],
        out_shape=[SDS((B, S, LANE), F32), SDS((1, LANE), F32)], compiler_params=_cp("arbitrary"), name=name,
    )(dF, flog, bias)


def mixer_out_fwd(oa, ob, goa, gob, w_out, g_post, x, mod3, name):
    T = x.shape[0]

    def body(oa_ref, ob_ref, goa_ref, gob_ref, w_ref, gp_ref, x_ref, mod_ref, xo_ref, mg_ref, y0_ref):
        a = oa_ref[...]
        b = ob_ref[...]
        mg = jnp.concatenate([a * _rms(a) * goa_ref[...], b * _rms(b) * gob_ref[...]], axis=-1).astype(BF16)
        mg_ref[...] = mg
        y0 = _dot(mg, w_ref[...])
        y0_ref[...] = y0
        xo_ref[...] = x_ref[...] + mod_ref[0, 2:3, :] * (y0 * _rms(y0) * gp_ref[...])

    tok = pl.BlockSpec((TM, D), lambda i: (i, 0))
    half = pl.BlockSpec((TM, WG), lambda i: (i, 0))
    hv = pl.BlockSpec((1, WG), lambda i: (0, 0))
    return pl.pallas_call(
        body, grid=(T // TM,),
        in_specs=[half, half, hv, hv, pl.BlockSpec((D, D), lambda i: (0, 0)), pl.BlockSpec((1, D), lambda i: (0, 0)), tok,
                  pl.BlockSpec((1, 3, D), _mod_map)],
        out_specs=[tok, tok, tok], out_shape=[SDS((T, D), F32), SDS((T, D), BF16), SDS((T, D), F32)],
        compiler_params=_cp("arbitrary"), name=name,
    )(oa, ob, goa, gob, w_out, g_post, x, mod3)


def mixer_out_bwd(dxo, y0, mod3, g_post, w_out, oa, ob, goa, gob, name):
    T = dxo.shape[0]
    nb = T // SEQ
    tiles_per_seq = SEQ // TM

    def body(dxo_ref, y0_ref, mod_ref, gp_ref, w_ref, oa_ref, ob_ref, goa_ref, gob_ref,
             dy0_ref, doa_ref, dob_ref, dmg_ref, dgp_ref, dgoa_ref, dgob_ref, dvb_ref):
        i = pl.program_id(0)

        @pl.when(i == 0)
        def _():
            dgp_ref[...] = jnp.zeros_like(dgp_ref)
            dgoa_ref[...] = jnp.zeros_like(dgoa_ref)
            dgob_ref[...] = jnp.zeros_like(dgob_ref)

        @pl.when(i % tiles_per_seq == 0)
        def _():
            dmg_ref[...] = jnp.zeros_like(dmg_ref)

        dy0, dmg, dg = _post_bwd(dxo_ref[...], y0_ref[...], gp_ref[...], mod_ref[0, 2:3, :], 1.0)
        dmg_ref[0] += dmg
        dgp_ref[...] += dg
        db = dy0.astype(BF16)
        dy0_ref[...] = db
        dm = _dot_nt(db, w_ref[...])
        for o_ref, g_ref, do_ref, dg_ref, sl in ((oa_ref, goa_ref, doa_ref, dgoa_ref, slice(0, WG)),
                                                  (ob_ref, gob_ref, dob_ref, dgob_ref, slice(WG, 2 * WG))):
            o = o_ref[...]
            r = _rms(o)
            oh = o * r
            d = dm[:, sl]
            dg_ref[...] += jnp.sum(d * oh, axis=0, keepdims=True)
            dh = d * g_ref[...]
            do = r * (dh - oh * jnp.mean(dh * oh, axis=-1, keepdims=True))
            do_ref[...] = do
        ind = (lax.broadcasted_iota(jnp.int32, (WG, LANE), 0) // HD == lax.broadcasted_iota(jnp.int32, (WG, LANE), 1)).astype(BF16)
        prod = do * o
        hi = prod.astype(BF16)
        dvb_ref[...] = _dot(hi, ind) + _dot((prod - hi.astype(F32)).astype(BF16), ind)

    tok = pl.BlockSpec((TM, D), lambda i: (i, 0))
    half = pl.BlockSpec((TM, WG), lambda i: (i, 0))
    hv = pl.BlockSpec((1, WG), lambda i: (0, 0))
    vec = pl.BlockSpec((1, D), lambda i: (0, 0))
    return pl.pallas_call(
        body, grid=(T // TM,),
        in_specs=[tok, tok, pl.BlockSpec((1, 3, D), _mod_map), vec, pl.BlockSpec((D, D), lambda i: (0, 0)), half, half, hv, hv],
        out_specs=[tok, half, half, pl.BlockSpec((1, 1, D), _mod_map), vec, hv, hv, pl.BlockSpec((TM, LANE), lambda i: (i, 0))],
        out_shape=[SDS((T, D), BF16), SDS((T, WG), F32), SDS((T, WG), F32), SDS((nb, 1, D), F32), SDS((1, D), F32),
                   SDS((1, WG), F32), SDS((1, WG), F32), SDS((T, LANE), F32)],
        compiler_params=_cp("arbitrary"), name=name,
    )(dxo, y0, mod3, g_post, w_out, oa, ob, goa, gob)


def mixer_proj_bwd(dps, dflog, dxo, x, mod3, g_pre, w_main, w_f, name):
    T = x.shape[0]
    nb = T // SEQ
    tiles_per_seq = SEQ // TM
    ngrp = len(dps)

    def body(*refs):
        dp_refs = refs[:ngrp]
        df_ref, dxo_ref, x_ref, mod_ref, g_ref, w_ref, wf_ref, dx_ref, dmod_ref, dg_ref = refs[ngrp:]
        i = pl.program_id(0)

        @pl.when(i == 0)
        def _():
            dg_ref[...] = jnp.zeros_like(dg_ref)

        @pl.when(i % tiles_per_seq == 0)
        def _():
            dmod_ref[...] = jnp.zeros_like(dmod_ref)

        dh = _dot_nt(df_ref[...].astype(BF16), wf_ref[...])
        for g, dp_ref in enumerate(dp_refs):
            dh = dh + _dot_nt(dp_ref[...], w_ref[:, g * WG:(g + 1) * WG])
        dx, dsh, dsc, dg = _norm_mod_bwd(dh, x_ref[...], g_ref[...], mod_ref[0, 1:2, :])
        dx_ref[...] = dxo_ref[...] + dx
        dmod_ref[0, 0:1, :] += dsh
        dmod_ref[0, 1:2, :] += dsc
        dg_ref[...] += dg

    tok = pl.BlockSpec((TM, D), lambda i: (i, 0))
    vec = pl.BlockSpec((1, D), lambda i: (0, 0))
    return pl.pallas_call(
        body, grid=(T // TM,),
        in_specs=[pl.BlockSpec((TM, WG), lambda i: (i, 0))] * ngrp
        + [pl.BlockSpec((TM, LANE), lambda i: (i, 0)), tok, tok, pl.BlockSpec((1, 3, D), _mod_map), vec,
           pl.BlockSpec((D, IN_MAIN), lambda i: (0, 0)), pl.BlockSpec((D, LANE), lambda i: (0, 0))],
        out_specs=[tok, pl.BlockSpec((1, 2, D), _mod_map), vec],
        out_shape=[SDS((T, D), F32), SDS((nb, 2, D), F32), SDS((1, D), F32)],
        compiler_params=_cp("arbitrary"), name=name,
    )(*dps, dflog, dxo, x, mod3, g_pre, w_main, w_f)


def ada_fwd(c_all, w, b, name):
    n = w.shape[1]
    tn = n // 2

    def body(c_ref, w_ref, b_ref, o_ref):
        cv = c_ref[...]
        o_ref[...] = _dot((cv * jax.nn.sigmoid(cv)).astype(BF16), w_ref[...].astype(BF16)) + b_ref[...]

    R = c_all.shape[0]
    return pl.pallas_call(
        body, grid=(2,),
        in_specs=[pl.BlockSpec((R, D), lambda j: (0, 0)), pl.BlockSpec((D, tn), lambda j: (0, j)), pl.BlockSpec((1, tn), lambda j: (0, j))],
        out_specs=pl.BlockSpec((R, tn), lambda j: (0, j)), out_shape=SDS((R, n), F32),
        compiler_params=_cp("arbitrary"), name=name,
    )(c_all, w, b)


def ada_bwd(c_all, dmod, name):
    R, n = dmod.shape
    tn = n // 2

    def body(c_ref, d_ref, o_ref):
        cv = c_ref[...]
        o_ref[...] = _dot_tn((cv * jax.nn.sigmoid(cv)).astype(BF16), d_ref[...].astype(BF16))

    return pl.pallas_call(
        body, grid=(2,), in_specs=[pl.BlockSpec((R, D), lambda j: (0, 0)), pl.BlockSpec((R, tn), lambda j: (0, j))],
        out_specs=pl.BlockSpec((D, tn), lambda j: (0, j)), out_shape=SDS((D, n), F32),
        compiler_params=_cp("arbitrary"), name=name,
    )(c_all, dmod)


def _adam_math(w, g, m, v):
    m2 = ADAM_B1 * m + (1.0 - ADAM_B1) * g
    v2 = ADAM_B2 * v + (1.0 - ADAM_B2) * (g * g)
    m_hat = m2 / (1.0 - ADAM_B1 ** ADAM_STEP)
    v_hat = v2 / (1.0 - ADAM_B2 ** ADAM_STEP)
    delta = -ADAM_LR * (m_hat / (jnp.sqrt(v_hat) + ADAM_EPS) + ADAM_WD * w)
    return delta, m2, v2


def adam_update(w, g, m, v, tr, name):
    _, R, C = w.shape

    def body(w_ref, g_ref, m_ref, v_ref, d_ref, mo_ref, vo_ref):
        d_ref[0], mo_ref[0], vo_ref[0] = _adam_math(w_ref[0], g_ref[...], m_ref[0], v_ref[0])

    spec = pl.BlockSpec((1, tr, C), lambda i: (0, i, 0))
    gspec = pl.BlockSpec((tr, C), lambda i: (i, 0))
    return pl.pallas_call(
        body, grid=(R // tr,), in_specs=[spec, gspec, spec, spec], out_specs=[spec] * 3, out_shape=[SDS((1, R, C), F32)] * 3,
        compiler_params=_cp("arbitrary"), name=name,
    )(w, g, m, v)


def adam_update_halves(w, mine, other, m, v, cidx, tr, name):
    _, R, C = w.shape
    nh = R // 2 // tr

    def body(c_ref, w_ref, a_ref, b_ref, m_ref, v_ref, g_ref, d_ref, mo_ref, vo_ref):
        first_half = pl.program_id(0) < nh
        g = jnp.where(first_half == (c_ref[0] == 0), a_ref[...], b_ref[...])
        g_ref[0] = g
        d_ref[0], mo_ref[0], vo_ref[0] = _adam_math(w_ref[0], g, m_ref[0], v_ref[0])

    spec = pl.BlockSpec((1, tr, C), lambda i, c_ref: (0, i, 0))
    hspec = pl.BlockSpec((tr, C), lambda i, c_ref: (i % nh, 0))
    return pl.pallas_call(
        body,
        grid_spec=pltpu.PrefetchScalarGridSpec(num_scalar_prefetch=1, grid=(R // tr,), in_specs=[spec, hspec, hspec, spec, spec],
                                               out_specs=[spec] * 4),
        out_shape=[SDS((1, R, C), F32)] * 4, compiler_params=_cp("arbitrary"), name=name,
    )(cidx, w, mine, other, m, v)


def vec_adam(parts, w, m, v, name):
    P, C = parts.shape

    def body(p_ref, w_ref, m_ref, v_ref, g_ref, d_ref, mo_ref, vo_ref):
        g = jnp.sum(p_ref[...], axis=0, keepdims=True)
        g_ref[...] = g
        d_ref[...], mo_ref[...], vo_ref[...] = _adam_math(w_ref[...], g, m_ref[...], v_ref[...])

    return pl.pallas_call(body, out_shape=[SDS((1, C), F32)] * 4, compiler_params=_cp(), name=name)(parts, w, m, v)


HBM = pl.BlockSpec(memory_space=pltpu.HBM)
VMEM = pl.BlockSpec(memory_space=pltpu.VMEM)


def _place():
    x, y, c = lax.axis_index("x"), lax.axis_index("y"), lax.axis_index("c")
    return x, y, c, [(1 - x, y), (x, 1 - y), (1 - x, 1 - y)]


def all_gather8(xs, name):
    R, C = xs.shape

    def body(x_ref, out_ref, send_sems, recv_sems, local_sem):
        x, y, c, chips = _place()
        me, sibling = (x, y, c), (x, y, 1 - c)

        def slot(px, py, pc):
            return out_ref.at[4 * px + 2 * py + pc]

        def copy(k, block, to, src=None):
            return pltpu.make_async_remote_copy(
                src_ref=slot(*block) if src is None else src, dst_ref=slot(*block),
                send_sem=send_sems.at[k], recv_sem=recv_sems.at[k], device_id=to, device_id_type=MESH)

        mine = pltpu.make_async_copy(x_ref, slot(*me), local_sem)
        mine.start()
        first = [copy(0, me, sibling, src=x_ref)]
        first += [copy(1 + j, me, (*chip, c), src=x_ref) for j, chip in enumerate(chips)]
        for cp in first:
            cp.start()
        passed = [copy(4 + j, (*chip, c), sibling) for j, chip in enumerate(chips)]
        for j, chip in enumerate(chips):
            copy(1 + j, (*chip, c), me).wait_recv()
            passed[j].start()
        copy(0, sibling, me).wait_recv()
        for j, chip in enumerate(chips):
            copy(4 + j, (*chip, 1 - c), me).wait_recv()
        for cp in first + passed:
            cp.wait_send()
        mine.wait()

    return pl.pallas_call(
        body, out_shape=SDS((N_DEV, R, C), xs.dtype), in_specs=[VMEM], out_specs=VMEM,
        scratch_shapes=[pltpu.SemaphoreType.DMA((7,)), pltpu.SemaphoreType.DMA((7,)), pltpu.SemaphoreType.DMA],
        compiler_params=pltpu.CompilerParams(vmem_limit_bytes=VMEM_LIMIT), name=name,
    )(xs)


class ShardGather:
    def __init__(self, shapes, splits):
        self.shapes, self.splits, self.n = shapes, splits, len(shapes)

    def scratch(self):
        n = self.n
        return [pltpu.SemaphoreType.DMA((n, 6)), pltpu.SemaphoreType.DMA((n, 6)), pltpu.SemaphoreType.DMA((n,))]

    def out_shapes(self, dtype):
        return [SDS((N_SHARD,) + tuple(s), dtype) for s in self.shapes]

    def _half(self, ref, k, cc):
        lo, hi = (0, self.splits[k]) if cc == 0 else (self.splits[k], self.shapes[k][0])
        return ref.at[pl.ds(lo, hi - lo)]

    def _phase(self, w_refs, o_refs, sems, finish):
        send_sems, recv_sems, local_sems = sems
        x, y, c, chips = _place()
        sibling = (x, y, 1 - c)
        me_s = 2 * x + y

        def rcopy(src, dst, k, s, to):
            return pltpu.make_async_remote_copy(src_ref=src, dst_ref=dst, send_sem=send_sems.at[k, s],
                                                recv_sem=recv_sems.at[k, s], device_id=to, device_id_type=MESH)

        for cc in (0, 1):
            @pl.when(c == cc)
            def _():
                local = [pltpu.make_async_copy(w_refs[k], o_refs[k].at[me_s], local_sems.at[k]) for k in range(self.n)]
                first = [rcopy(self._half(w_refs[k], k, cc), self._half(o_refs[k].at[me_s], k, cc), k, j, (*chip, c))
                         for k in range(self.n) for j, chip in enumerate(chips)]
                if not finish:
                    for cp in local + first:
                        cp.start()
                    return
                passed = []
                for k in range(self.n):
                    for j, chip in enumerate(chips):
                        land = self._half(o_refs[k].at[2 * chip[0] + chip[1]], k, cc)
                        rcopy(land, land, k, j, (*chip, c)).wait_recv()
                        f = rcopy(land, land, k, 3 + j, sibling)
                        f.start()
                        passed.append(f)
                for k in range(self.n):
                    for j, chip in enumerate(chips):
                        other = self._half(o_refs[k].at[2 * chip[0] + chip[1]], k, 1 - cc)
                        rcopy(other, other, k, 3 + j, sibling).wait_recv()
                for s in first + passed:
                    s.wait_send()
                for cp in local:
                    cp.wait()

    def start(self, w_refs, o_refs, sems):
        self._phase(w_refs, o_refs, sems, False)

    def finish(self, w_refs, o_refs, sems):
        self._phase(w_refs, o_refs, sems, True)


def all_gather_shards(ws, splits, name):
    n = len(ws)
    plan = ShardGather([w.shape for w in ws], splits)

    def body(*refs):
        plan.start(refs[:n], refs[n:2 * n], refs[2 * n:])
        plan.finish(refs[:n], refs[n:2 * n], refs[2 * n:])

    return pl.pallas_call(
        body, out_shape=plan.out_shapes(ws[0].dtype), in_specs=[HBM] * n, out_specs=[HBM] * n,
        scratch_shapes=plan.scratch(), name=name,
    )(*ws)


def sibling_send_half(gs, name):
    n = len(gs)

    def body(*refs):
        g_refs, o_refs = refs[:n], refs[n:2 * n]
        send_sems, recv_sems = refs[2 * n:]
        x, y, c, _ = _place()
        cps = []
        for k in range(n):
            hr = gs[k].shape[1] // 2
            src = g_refs[k].at[:, pl.ds(pl.multiple_of((1 - c) * hr, 8), hr)]
            cp = pltpu.make_async_remote_copy(src_ref=src, dst_ref=o_refs[k], send_sem=send_sems.at[k], recv_sem=recv_sems.at[k],
                                              device_id=(x, y, 1 - c), device_id_type=MESH)
            cp.start()
            cps.append(cp)
        for cp in cps:
            cp.wait()

    return pl.pallas_call(
        body, out_shape=[SDS((N_SHARD, g.shape[1] // 2, g.shape[2]), g.dtype) for g in gs], in_specs=[HBM] * n, out_specs=[HBM] * n,
        scratch_shapes=[pltpu.SemaphoreType.DMA((n,)), pltpu.SemaphoreType.DMA((n,))], name=name,
    )(*gs)


def _scatter_copies(h_refs, o_refs, send_sems, recv_sems):
    _, _, c, chips = _place()
    return [pltpu.make_async_remote_copy(
        src_ref=h_refs[k].at[2 * chip[0] + chip[1]], dst_ref=o_refs[k].at[j], send_sem=send_sems.at[k, j],
        recv_sem=recv_sems.at[k, j], device_id=(*chip, c), device_id_type=MESH)
        for k in range(len(h_refs)) for j, chip in enumerate(chips)]


def chip_scatter(hs, name):
    n = len(hs)

    def body(*refs):
        cps = _scatter_copies(refs[:n], refs[n:2 * n], *refs[2 * n:])
        for cp in cps:
            cp.start()
        for cp in cps:
            cp.wait()

    return pl.pallas_call(
        body, out_shape=[SDS((3,) + h.shape[1:], h.dtype) for h in hs], in_specs=[HBM] * n, out_specs=[HBM] * n,
        scratch_shapes=[pltpu.SemaphoreType.DMA((n, 3)), pltpu.SemaphoreType.DMA((n, 3))], name=name,
    )(*hs)


def sibling_swap(ghs, name):
    n = len(ghs)

    def body(*refs):
        g_refs, o_refs = refs[:n], refs[n:2 * n]
        send_sems, recv_sems = refs[2 * n:]
        x, y, c, _ = _place()
        cps = []
        for k in range(n):
            cp = pltpu.make_async_remote_copy(src_ref=g_refs[k], dst_ref=o_refs[k], send_sem=send_sems.at[k],
                                              recv_sem=recv_sems.at[k], device_id=(x, y, 1 - c), device_id_type=MESH)
            cp.start()
            cps.append(cp)
        for cp in cps:
            cp.wait()

    return pl.pallas_call(
        body, out_shape=[SDS(g.shape, g.dtype) for g in ghs], in_specs=[HBM] * n, out_specs=[HBM] * n,
        scratch_shapes=[pltpu.SemaphoreType.DMA((n,)), pltpu.SemaphoreType.DMA((n,))], name=name,
    )(*ghs)


def pair_sum(g, ra, cidx, name):
    _, r, cols = g.shape
    hr = r // 2

    def body(c_ref, g_ref, a_ref, o_ref):
        o_ref[...] = (g_ref[...] + a_ref[...]).astype(BF16)

    return pl.pallas_call(
        body,
        grid_spec=pltpu.PrefetchScalarGridSpec(
            num_scalar_prefetch=1, grid=(N_SHARD,),
            in_specs=[pl.BlockSpec((1, hr, cols), lambda s, c_ref: (s, c_ref[0], 0)),
                      pl.BlockSpec((1, hr, cols), lambda s, c_ref: (s, 0, 0))],
            out_specs=pl.BlockSpec((1, hr, cols), lambda s, c_ref: (s, 0, 0))),
        out_shape=SDS((N_SHARD, hr, cols), BF16), compiler_params=_cp("arbitrary"), name=name,
    )(cidx, g, ra)


def chip_sum(h, rb, sidx, name):
    _, hr, cols = h.shape

    def body(s_ref, h_ref, r_ref, o_ref):
        o_ref[...] = ((h_ref[0].astype(F32) + r_ref[0].astype(F32)) + r_ref[1].astype(F32)) + r_ref[2].astype(F32)

    return pl.pallas_call(
        body,
        grid_spec=pltpu.PrefetchScalarGridSpec(
            num_scalar_prefetch=1, grid=(1,),
            in_specs=[pl.BlockSpec((1, hr, cols), lambda i, s_ref: (s_ref[0], 0, 0)),
                      pl.BlockSpec((3, hr, cols), lambda i, s_ref: (0, 0, 0))],
            out_specs=pl.BlockSpec((hr, cols), lambda i, s_ref: (0, 0))),
        out_shape=SDS((hr, cols), F32), compiler_params=_cp("arbitrary"), name=name,
    )(sidx, h, rb)


def _shard_cols(g, n_valid):
    r = g.shape[0]
    return g[:, :n_valid].reshape(r, N_SHARD, n_valid // N_SHARD).transpose(1, 0, 2)


def _unshard_cols(o, pad_to):
    _, r, n = o.shape
    full = o.transpose(1, 0, 2).reshape(r, N_SHARD * n)
    return jnp.pad(full, ((0, 0), (0, pad_to - N_SHARD * n)))


def _rows_of_tiles(t):
    B, H, S = t.shape
    return t.reshape(B, H, S // FT, 1, FT)


def mixer_fwd(x1, mod3, g_pre, w_main, w_f, b_forget_pad, goa, gob, w_out, g_post, tabs, nb, gather=None):
    hmix, pa, pb, flog = mixer_proj(x1, mod3, g_pre, w_main, w_f, *tabs, name="mixer_proj")
    out_a, lse_a = band_fwd(pa, name="band_fwd")
    F = forget_cumsum(flog.reshape(nb, SEQ, LANE), b_forget_pad, name="forget_cumsum")
    Fh = F[:, :, :NH].transpose(0, 2, 1)
    fblk = Fh.reshape(nb, NH, SEQ // FB, 1, FB)
    frow = _rows_of_tiles(Fh)
    (out_b, lse_b), gathered = fox_fwd(pb, Fh.reshape(nb, NH, SEQ // FOX_QB, 1, FOX_QB), frow, name="fox_fwd", gather=gather)
    x2, merged, y0m = mixer_out_fwd(out_a, out_b, goa, gob, w_out, g_post, x1, mod3, name="mixer_out_fwd")
    res = dict(hmix=hmix, flog=flog, pa=pa, pb=pb, out_a=out_a, lse_a=lse_a, fblk=fblk, frow=frow, out_b=out_b,
               lrow=_rows_of_tiles(lse_b.reshape(nb, NH, SEQ)), merged=merged, y0m=y0m)
    return x2, res, gathered


def mixer_bwd(dx2, x1, mod3, g_pre, w_main, w_f, b_forget_pad, goa, gob, w_out, g_post, tabs, res, nb):
    T = nb * SEQ
    dy0m, doa, dob, dmgate, dg_post, dgoa, dgob, dvec_b = mixer_out_bwd(
        dx2, res["y0m"], mod3, g_post, w_out, res["out_a"], res["out_b"], goa, gob, name="mixer_out_bwd")
    dqa, dka, dva = band_bwd(res["pa"], doa, res["out_a"], res["lse_a"], *tabs, name="band_bwd")
    drow = _rows_of_tiles(dvec_b[:, :NH].reshape(nb, SEQ, NH).transpose(0, 2, 1))
    dqb, dkb, dvb, dfq, dfk = fox_bwd(res["pb"], dob, res["lrow"], drow, res["fblk"], res["frow"], name="fox_bwd")
    dF = (dfq.reshape(nb, NH, SEQ) + dfk.reshape(nb, NH, SEQ)).transpose(0, 2, 1)
    dF = jnp.pad(dF, ((0, 0), (0, 0), (0, LANE - NH)))
    dflog, dbf = forget_cumsum_bwd(dF, res["flog"].reshape(nb, SEQ, LANE), b_forget_pad, name="forget_cumsum_bwd")
    dflog = dflog.reshape(T, LANE)
    dps = (dqa, dka, dva, dqb, dkb, dvb)
    dx1, dmod2, dg_pre = mixer_proj_bwd(dps, dflog, dx2, x1, mod3, g_pre, w_main, w_f, name="mixer_proj_bwd")
    g_main = matmul_tn_cols(res["hmix"], dps, 1024, name="grad_w_in")
    g_f = matmul_tn(res["hmix"], dflog.astype(BF16), D, LANE, 1024, name="grad_w_forget")
    g_out = matmul_tn(res["merged"], dy0m, D, D, 1024, name="grad_w_out")
    dmod3 = jnp.concatenate([dmod2, dmgate], axis=1)
    return dx1, dmod3, dict(g_pre=dg_pre, g_post=dg_post, goa=dgoa, gob=dgob, b_forget=dbf[:, :NH],
                            w_in=jnp.concatenate([g_main, g_f[:, :NH]], axis=1), w_out=g_out)


def ffn_grads(h, dy0, act, dgate, dup, pre, reduce=None):
    g_gate = matmul_tn(h, dgate, D, DFF_PAD, 1024, name=pre + "_grad_gate")
    if reduce is None:
        g_up = matmul_tn(h, dup, D, DFF_PAD, 1024, name=pre + "_grad_up")
        g_down = matmul_tn(act, dy0, FF_TN, D, 1024, name=pre + "_grad_down")
        return (g_gate, g_up, g_down), {}
    hs_gate = reduce("gate", g_gate)
    g_up, rb_gate = matmul_tn(h, dup, D, DFF_PAD, 1024, name=pre + "_grad_up", scatter=hs_gate)
    hs_up = reduce("up", g_up)
    g_down, rb_up = matmul_tn(act, dy0, FF_TN, D, 1024, name=pre + "_grad_down", scatter=hs_up)
    return (g_gate, g_up, g_down), {"gate": (hs_gate[0], rb_gate[0]), "up": (hs_up[0], rb_up[0])}


def local_step(x0, tgt, pos_col, mod, wfull, p, late_weights=None, last_weights=None, early_grads=None, last_reduce=None):
    T = x0.shape[0]
    nb = T // SEQ
    mod_ff1, mod_mix, mod_ff2 = mod[:, 0:3], mod[:, 3:6], mod[:, 6:9]
    tabs = rope_tables(pos_col, name="rope_tables")
    bf_pad = jnp.pad(p["b_forget"], ((0, 0), (0, LANE - NH)))

    (x1, h1, gate1, up1, y01), gathered = ffn_fwd(
        x0, mod_ff1, p["g_pre_ff1"], p["g_post_ff1"], wfull["w_ff1_gate"], wfull["w_ff1_up"], wfull["w_ff1_down"], 0.5,
        name="ff1_fwd", gather=None if late_weights is None else late_weights[:2])
    if late_weights is not None:
        wfull = {**wfull, **late_weights[2](gathered)}
    x2, res, gathered = mixer_fwd(x1, mod_mix, p["g_pre_mix"], wfull["w_main"], wfull["w_f"], bf_pad, p["g_out_a"],
                                  p["g_out_b"], wfull["w_out"], p["g_post_mix"], tabs, nb,
                                  gather=None if last_weights is None else last_weights[:2])
    if last_weights is not None:
        wfull = {**wfull, **last_weights[2](gathered)}
    (x3, h2, gate2, up2, y02), _ = ffn_fwd(x2, mod_ff2, p["g_pre_ff2"], p["g_post_ff2"], wfull["w_ff2_gate"],
                                           wfull["w_ff2_up"], wfull["w_ff2_down"], 0.5, name="ff2_fwd")

    (dx2, dy02, act2, dgate2, dup2, dmod_ff2, dgpre2, dgpost2), (loss_part,) = ffn_bwd(
        x3, x2, y02, mod_ff2, p["g_pre_ff2"], p["g_post_ff2"], gate2, up2, wfull["w_ff2_gate"], wfull["w_ff2_up"],
        wfull["w_ff2_down"], 0.5, name="ff2_bwd", target=tgt)
    gw = {}
    (gw["w_ff2_gate"], gw["w_ff2_up"], gw["w_ff2_down"]), _ = ffn_grads(h2, dy02, act2, dgate2, dup2, "ff2")
    dx1, dmod_mix, gmix = mixer_bwd(dx2, x1, mod_mix, p["g_pre_mix"], wfull["w_main"], wfull["w_f"], bf_pad, p["g_out_a"],
                                    p["g_out_b"], wfull["w_out"], p["g_post_mix"], tabs, res, nb)
    gw["w_in"], gw["w_out"] = gmix["w_in"], gmix["w_out"]
    (dx0, dy01, act1, dgate1, dup1, dmod_ff1, dgpre1, dgpost1), scattered = ffn_bwd(
        dx1, x0, y01, mod_ff1, p["g_pre_ff1"], p["g_post_ff1"], gate1, up1, wfull["w_ff1_gate"], wfull["w_ff1_up"],
        wfull["w_ff1_down"], 0.5, name="ff1_bwd", scatter=None if early_grads is None else early_grads(gw))
    (gw["w_ff1_gate"], gw["w_ff1_up"], gw["w_ff1_down"]), chained = ffn_grads(h1, dy01, act1, dgate1, dup1, "ff1", last_reduce)
    dmod = jnp.concatenate([dmod_ff1, dmod_mix, dmod_ff2], axis=1).reshape(nb, 9 * D)
    small = dict(g_pre_ff1=dgpre1, g_post_ff1=dgpost1, g_pre_mix=gmix["g_pre"], g_post_mix=gmix["g_post"], g_pre_ff2=dgpre2,
                 g_post_ff2=dgpost2, g_out_a=gmix["goa"], g_out_b=gmix["gob"], b_forget=gmix["b_forget"])
    return loss_part, dx0, dmod, gw, small, scattered, chained


def kernel(x, c, positions, w_ada, b_ada, g_pre_ff1, g_post_ff1, w_ff1_gate, w_ff1_up, w_ff1_down, g_pre_mix, g_post_mix, w_in, b_forget, g_out_a, g_out_b, w_out, g_pre_ff2, g_post_ff2, w_ff2_gate, w_ff2_up, w_ff2_down, loss_target, m_w_ada, m_b_ada, m_g_pre_ff1, m_g_post_ff1, m_w_ff1_gate, m_w_ff1_up, m_w_ff1_down, m_g_pre_mix, m_g_post_mix, m_w_in, m_b_forget, m_g_out_a, m_g_out_b, m_w_out, m_g_pre_ff2, m_g_post_ff2, m_w_ff2_gate, m_w_ff2_up, m_w_ff2_down, v_w_ada, v_b_ada, v_g_pre_ff1, v_g_post_ff1, v_w_ff1_gate, v_w_ff1_up, v_w_ff1_down, v_g_pre_mix, v_g_post_mix, v_w_in, v_b_forget, v_g_out_a, v_g_out_b, v_w_out, v_g_pre_ff2, v_g_post_ff2, v_w_ff2_gate, v_w_ff2_up, v_w_ff2_down):
    args = dict(locals())
    nb = x.shape[0]
    T = nb * SEQ
    ax, ay, ac = lax.axis_index("x"), lax.axis_index("y"), lax.axis_index("c")
    shard = 2 * ax + ay
    cidx = jnp.reshape(ac, (1,)).astype(jnp.int32)
    sidx = jnp.reshape(shard, (1,)).astype(jnp.int32)

    big = ["w_ff1_gate", "w_ff1_up", "w_ff1_down", "w_in", "w_out", "w_ff2_gate", "w_ff2_up", "w_ff2_down"]
    vecs = ["g_pre_ff1", "g_post_ff1", "g_pre_mix", "g_post_mix", "g_pre_ff2", "g_post_ff2"]

    first, late = big[:3], big[3:]
    splits = dict(zip(big, [512, 512, 352, 512, 128, 512, 512, 352]))

    def assemble(names, gathered):
        out = {}
        for n, o in zip(names, gathered):
            if n.endswith("gate") or n.endswith("up"):
                out[n] = _unshard_cols(o, DFF_PAD)
            elif n.endswith("down"):
                out[n] = jnp.pad(o.reshape(DFF, D), ((0, DFF_PAD - DFF), (0, 0)))
            elif n == "w_in":
                full = _unshard_cols(o, IN_COLS)
                out["w_main"] = full[:, :IN_MAIN]
                out["w_f"] = jnp.pad(full[:, IN_MAIN:], ((0, 0), (0, LANE - NH)))
            else:
                out[n] = o.reshape(D, D)
        return out

    wfull = assemble(first, all_gather_shards([args[n][0].astype(BF16) for n in first], [splits[n] for n in first],
                                              name="all_gather_weights"))
    def gather_plan(names):
        return ([args[n][0].astype(BF16) for n in names], [splits[n] for n in names], functools.partial(assemble, names))

    late_weights, last_weights = gather_plan(late[:2]), gather_plan(late[2:])

    ncol = w_ada.shape[2]
    c_all = all_gather8(c, name="all_gather_c").reshape(N_DEV * nb, D)
    b_loc = lax.dynamic_slice(b_ada, (0, shard * ncol), (1, ncol))
    mod_loc = ada_fwd(c_all, w_ada[0], b_loc, name="ada_fwd")
    mod_g = all_gather8(mod_loc, name="all_gather_mod")
    row0 = (4 * ax + 2 * ay + ac) * nb
    mod_rows = lax.dynamic_slice(mod_g, (0, row0, 0), (N_DEV, nb, ncol))
    mod = jnp.concatenate([mod_rows[2 * s] for s in range(N_SHARD)], axis=-1).reshape(nb, 9, D)

    small_in = dict(g_pre_ff1=g_pre_ff1, g_post_ff1=g_post_ff1, g_pre_mix=g_pre_mix, g_post_mix=g_post_mix, g_pre_ff2=g_pre_ff2,
                    g_post_ff2=g_post_ff2, g_out_a=g_out_a, g_out_b=g_out_b, b_forget=b_forget)
    def shard_blocked(n, g):
        if n.endswith("gate") or n.endswith("up"):
            return _shard_cols(g, DFF)
        if n.endswith("down"):
            return g[:DFF].reshape(N_SHARD, DFF // N_SHARD, D)
        if n == "w_in":
            return _shard_cols(g, IN_COLS)
        return g.reshape(N_SHARD, D // N_SHARD, D)

    def chip_sums(names, gw, tag):
        gsb = [shard_blocked(n, gw[n]) for n in names]
        ras = sibling_send_half(gsb, name="grad_sibling_send_" + tag)
        return [pair_sum(g, ra, cidx, name=f"grad_pair_sum_{n}") for n, g, ra in zip(names, gsb, ras)]

    hs = {}

    def early_grads(gw):
        hs.update(zip(late, chip_sums(late, gw, "late")))
        return [hs[n] for n in late]

    def last_reduce(which, g):
        return chip_sums(["w_ff1_" + which], {"w_ff1_" + which: g}, which)

    loss_part, dx0, dmod, gw, small, rbs_late, chained = local_step(
        x.reshape(T, D), loss_target.reshape(T, D), positions.reshape(T, 1), mod, wfull, small_in, late_weights, last_weights,
        early_grads, last_reduce)

    dmod_all = all_gather8(dmod, name="all_gather_dmod").reshape(N_DEV * nb, 9 * D)
    dmod_loc = lax.dynamic_slice(dmod_all, (0, shard * ncol), (N_DEV * nb, ncol))
    g_w_ada = ada_bwd(c_all, dmod_loc, name="ada_bwd")

    rbs = dict(zip(late, rbs_late))
    for which, (h, rb) in chained.items():
        hs["w_ff1_" + which], rbs["w_ff1_" + which] = h, rb
    hs["w_ff1_down"] = chip_sums(["w_ff1_down"], gw, "down")[0]
    rbs["w_ff1_down"] = chip_scatter([hs["w_ff1_down"]], name="grad_chip_scatter")[0]
    ghs = [chip_sum(hs[n], rbs[n], sidx, name=f"grad_chip_sum_{n}") for n in big]
    theirs = sibling_swap(ghs, name="grad_sibling_swap")

    row6 = jnp.concatenate([small["g_out_a"], small["g_out_b"]], axis=1)
    row7 = jnp.concatenate([small["b_forget"], loss_part[0:1, 0:1], jnp.zeros((1, D - NH - 1), F32)], axis=1)
    pack = jnp.concatenate([small[n] for n in vecs] + [row6, row7], axis=0)
    packed = all_gather8(pack, name="all_gather_small").reshape(N_DEV, 8 * D)

    def pack_state(pre):
        r6 = jnp.concatenate([args[pre + "g_out_a"], args[pre + "g_out_b"]], axis=1)
        r7 = jnp.pad(args[pre + "b_forget"], ((0, 0), (0, D - NH)))
        return jnp.concatenate([args[pre + n] for n in vecs] + [r6, r7], axis=0).reshape(1, 8 * D)

    sg, sd, sm, sv = (t.reshape(8, D) for t in vec_adam(packed, pack_state(""), pack_state("m_"), pack_state("v_"), name="adam_small"))

    def unpack(t):
        out = {n: t[i:i + 1] for i, n in enumerate(vecs)}
        out["g_out_a"], out["g_out_b"], out["b_forget"] = t[6:7, :WG], t[6:7, WG:], t[7:8, :NH]
        return out

    outs = dict(grad=unpack(sg), delta=unpack(sd), new_m=unpack(sm), new_v=unpack(sv))
    loss = sg[7, NH]
    outs["grad"]["b_ada"], outs["delta"]["b_ada"], outs["new_m"]["b_ada"], outs["new_v"]["b_ada"] = vec_adam(
        dmod_all, b_ada, m_b_ada, v_b_ada, name="adam_b_ada")

    for n, mine, other in zip(big, ghs, theirs):
        tr = 128 if mine.shape[0] % 128 == 0 else mine.shape[0]
        outs["grad"][n], outs["delta"][n], outs["new_m"][n], outs["new_v"][n] = adam_update_halves(
            args[n], mine, other, args["m_" + n], args["v_" + n], cidx, tr, name="adam_" + n)
    outs["delta"]["w_ada"], outs["new_m"]["w_ada"], outs["new_v"]["w_ada"] = adam_update(
        w_ada, g_w_ada, m_w_ada, v_w_ada, 128, name="adam_w_ada")
    outs["grad"]["w_ada"] = g_w_ada[None]

    order = ["w_ada", "b_ada", "g_pre_ff1", "g_post_ff1", "w_ff1_gate", "w_ff1_up", "w_ff1_down", "g_pre_mix", "g_post_mix", "w_in",
             "b_forget", "g_out_a", "g_out_b", "w_out", "g_pre_ff2", "g_post_ff2", "w_ff2_gate", "w_ff2_up", "w_ff2_down"]
    result = [loss, dx0.reshape(nb, SEQ, D)]
    for kind in ("grad", "delta", "new_m", "new_v"):
        result += [outs[kind][n] for n in order]
    return tuple(result)
```

```python
import functools
import math

import jax
import jax.numpy as jnp
from jax import lax
from jax.experimental import pallas as pl
from jax.experimental.pallas import tpu as pltpu

D = 1024
SEQ = 2048
HD = 64
NH = 8
WG = NH * HD
DFF = 2752
DFF_PAD = 2816
IN_MAIN = 6 * WG
IN_COLS = IN_MAIN + NH
N_SHARD = 4
N_DEV = 8
LANE = 128
QB = 128
ROWS = 256
FB = 512
FT = 512
FOX_QB = 512
FOX_PAIRS = 4
FOX_PAIRS_BWD = 2
BAND_UNROLL = 4
BAND_UNROLL_BWD = 4
PATTERNS = ((1, 16), (4, 4), (16, 1))
ROPE_THETA = 500000.0
EPS = 1e-6
NEG = -1e30
ATTN_SCALE = HD ** -0.5
TM = 512
TM_FFN = 512
TM_BWD = 256
VMEM_LIMIT = 56 * 1024 * 1024

ADAM_LR, ADAM_B1, ADAM_B2, ADAM_EPS, ADAM_WD, ADAM_STEP = 0.001, 0.9, 0.999, 1e-08, 0.01, 10

F32 = jnp.float32
BF16 = jnp.bfloat16
MESH = pl.DeviceIdType.MESH
SDS = jax.ShapeDtypeStruct


def _cp(*sem):
    return pltpu.CompilerParams(dimension_semantics=sem, vmem_limit_bytes=VMEM_LIMIT)


def _dot(a, b):
    return jnp.dot(a, b, preferred_element_type=F32)


def _dot_nt(a, b):
    return lax.dot_general(a, b, (((1,), (1,)), ((), ())), preferred_element_type=F32)


def _dot_tn(a, b):
    return lax.dot_general(a, b, (((0,), (0,)), ((), ())), preferred_element_type=F32)


def _rms(xf):
    return lax.rsqrt(jnp.mean(xf * xf, axis=-1, keepdims=True) + EPS)


def _norm_mod_bwd(dh, xf, g, scale):
    r = _rms(xf)
    xh = xf * r
    dsh = jnp.sum(dh, axis=0, keepdims=True)
    dsc = jnp.sum(dh * (xh * g), axis=0, keepdims=True)
    dn = dh * (1.0 + scale)
    dg = jnp.sum(dn * xh, axis=0, keepdims=True)
    dxh = dn * g
    dx = r * (dxh - xh * jnp.mean(dxh * xh, axis=-1, keepdims=True))
    return dx, dsh, dsc, dg


def _post_bwd(dxo, y0, g, mgate, gs):
    r = _rms(y0)
    yh = y0 * r
    dmg = gs * jnp.sum(dxo * (yh * g), axis=0, keepdims=True)
    dy = (gs * mgate) * dxo
    dg = jnp.sum(dy * yh, axis=0, keepdims=True)
    dyh = dy * g
    dy0 = r * (dyh - yh * jnp.mean(dyh * yh, axis=-1, keepdims=True))
    return dy0, dmg, dg


def _mod_map(i, *_):
    return ((i * TM) // SEQ, 0, 0)


FF_TN = 1408
FF_TILES = ((0, 768), (768, 1536), (1536, 2304), (2304, 2816))


def _resident_scratch():
    return [pltpu.VMEM((D, DFF_PAD), BF16), pltpu.VMEM((D, DFF_PAD), BF16), pltpu.VMEM((DFF_PAD, D), BF16),
            pltpu.SemaphoreType.DMA((3,))]


def _load_resident(first_step, srcs, dsts, sems):
    @pl.when(first_step)
    def _():
        cps = [pltpu.make_async_copy(s, d, sems.at[k]) for k, (s, d) in enumerate(zip(srcs, dsts))]
        for cp in cps:
            cp.start()
        for cp in cps:
            cp.wait()


def ffn_fwd(x, mod3, g_pre, g_post, wg, wu, wd, gs, name, gather=None):
    T = x.shape[0]
    tm = TM_FFN
    ng = 0 if gather is None else len(gather[0])
    plan = None if gather is None else ShardGather([w.shape for w in gather[0]], gather[1])

    def body(*refs):
        x_ref, mod_ref, gpre_ref, gpost_ref = refs[:4]
        xo_ref, h_ref, gate_ref, up_ref, y0_ref = refs[7 + ng:12 + ng]
        wg_ref, wu_ref, wd_ref, wsem = refs[12 + 2 * ng:16 + 2 * ng]
        i = pl.program_id(0)
        if plan is not None:
            comm = (refs[7:7 + ng], refs[12 + ng:12 + 2 * ng], refs[16 + 2 * ng:])
            pl.when(i == 0)(lambda: plan.start(*comm))
        _load_resident(i == 0, refs[4:7], (wg_ref, wu_ref, wd_ref), wsem)

        xf = x_ref[...]
        hb = ((xf * _rms(xf) * gpre_ref[...]) * (1.0 + mod_ref[0, 1:2, :]) + mod_ref[0, 0:1, :]).astype(BF16)
        h_ref[...] = hb
        y0 = None
        for lo, hi in FF_TILES:
            gate = _dot(hb, wg_ref[:, lo:hi])
            up = _dot(hb, wu_ref[:, lo:hi])
            gate_ref[:, lo:hi] = gate.astype(BF16)
            up_ref[:, lo:hi] = up.astype(BF16)
            part = _dot((gate * jax.nn.sigmoid(gate) * up).astype(BF16), wd_ref[lo:hi, :])
            y0 = part if y0 is None else y0 + part
        y0_ref[...] = y0
        xo_ref[...] = xf + (gs * mod_ref[0, 2:3, :]) * (y0 * _rms(y0) * gpost_ref[...])

        if plan is not None:
            pl.when(i == T // tm - 1)(lambda: plan.finish(*comm))

    tok = pl.BlockSpec((tm, D), lambda i: (i, 0))
    vec = pl.BlockSpec((1, D), lambda i: (0, 0))
    hid = pl.BlockSpec((tm, DFF_PAD), lambda i: (i, 0))
    outs = pl.pallas_call(
        body, grid=(T // tm,),
        in_specs=[tok, pl.BlockSpec((1, 3, D), lambda i: ((i * tm) // SEQ, 0, 0)), vec, vec, HBM, HBM, HBM] + [HBM] * ng,
        out_specs=[tok, tok, hid, hid, tok] + [HBM] * ng,
        out_shape=[SDS((T, D), F32), SDS((T, D), BF16), SDS((T, DFF_PAD), BF16), SDS((T, DFF_PAD), BF16), SDS((T, D), F32)]
        + ([] if plan is None else plan.out_shapes(BF16)),
        scratch_shapes=_resident_scratch() + ([] if plan is None else plan.scratch()),
        compiler_params=_cp("arbitrary"), name=name,
    )(x, mod3, g_pre, g_post, wg, wu, wd, *([] if gather is None else gather[0]))
    return outs[:5], outs[5:]


def ffn_bwd(dxo, x, y0, mod3, g_pre, g_post, gate, up, wg, wu, wd, gs, name, scatter=None, target=None):
    assert scatter is None or target is None
    T = x.shape[0]
    nb = T // SEQ
    tm = TM_BWD
    tiles_per_seq = SEQ // tm
    ns = 0 if scatter is None else len(scatter)
    ne = ns + (target is not None)

    def body(*refs):
        dxo_ref, x_ref, y0_ref, mod_ref, gpre_ref, gpost_ref, gate_ref, up_ref = refs[:8]
        dx_ref, dy0_ref, act_ref, dgate_ref, dup_ref, dmod_ref, dgpre_ref, dgpost_ref = refs[11 + ne:19 + ne]
        wg_ref, wu_ref, wd_ref, wsem = refs[19 + 2 * ne:23 + 2 * ne]
        i = pl.program_id(0)
        _load_resident(i == 0, refs[8:11], (wg_ref, wu_ref, wd_ref), wsem)
        if ns:
            comm = (refs[11:11 + ns], refs[19 + ns:19 + 2 * ns], *refs[23 + 2 * ns:])

            @pl.when(i == 0)
            def _():
                for cp in _scatter_copies(*comm):
                    cp.start()

        @pl.when(i == 0)
        def _():
            dgpre_ref[...] = jnp.zeros_like(dgpre_ref)
            dgpost_ref[...] = jnp.zeros_like(dgpost_ref)

        @pl.when(i % tiles_per_seq == 0)
        def _():
            dmod_ref[...] = jnp.zeros_like(dmod_ref)

        dxo = dxo_ref[...]
        if target is not None:
            loss_ref = refs[19 + ne]

            @pl.when(i == 0)
            def _():
                loss_ref[...] = jnp.zeros_like(loss_ref)

            err = dxo - refs[11][...]
            loss_ref[...] += jnp.sum(err * err) * (0.5 / D)
            dxo = err * (1.0 / D)
        dy0, dmg, dg = _post_bwd(dxo, y0_ref[...], gpost_ref[...], mod_ref[0, 2:3, :], gs)
        dmod_ref[0, 2:3, :] += dmg
        dgpost_ref[...] += dg
        db = dy0.astype(BF16)
        dy0_ref[...] = db
        dh = None
        for lo, hi in FF_TILES:
            dact = _dot_nt(db, wd_ref[lo:hi, :])
            g = gate_ref[:, lo:hi].astype(F32)
            u = up_ref[:, lo:hi].astype(F32)
            sig = jax.nn.sigmoid(g)
            sl = g * sig
            dgate = (dact * u * (sig * (1.0 + g * (1.0 - sig)))).astype(BF16)
            dup = (dact * sl).astype(BF16)
            act_ref[:, lo:hi] = (sl * u).astype(BF16)
            dgate_ref[:, lo:hi] = dgate
            dup_ref[:, lo:hi] = dup
            part = _dot_nt(dgate, wg_ref[:, lo:hi]) + _dot_nt(dup, wu_ref[:, lo:hi])
            dh = part if dh is None else dh + part
        dx, dsh, dsc, dg = _norm_mod_bwd(dh, x_ref[...], gpre_ref[...], mod_ref[0, 1:2, :])
        dx_ref[...] = dxo + dx
        dmod_ref[0, 0:1, :] += dsh
        dmod_ref[0, 1:2, :] += dsc
        dgpre_ref[...] += dg

        if ns:
            @pl.when(i == T // tm - 1)
            def _():
                for cp in _scatter_copies(*comm):
                    cp.wait()

    tok = pl.BlockSpec((tm, D), lambda i: (i, 0))
    vec = pl.BlockSpec((1, D), lambda i: (0, 0))
    hid = pl.BlockSpec((tm, DFF_PAD), lambda i: (i, 0))
    modspec = pl.BlockSpec((1, 3, D), lambda i: ((i * tm) // SEQ, 0, 0))
    outs = pl.pallas_call(
        body, grid=(T // tm,),
        in_specs=[tok, tok, tok, modspec, vec, vec, hid, hid, HBM, HBM, HBM] + [HBM] * ns + [tok] * (ne - ns),
        out_specs=[tok, tok, hid, hid, hid, modspec, vec, vec] + [HBM] * ns
        + [pl.BlockSpec((8, LANE), lambda i: (0, 0))] * (ne - ns),
        out_shape=[SDS((T, D), F32), SDS((T, D), BF16), SDS((T, DFF_PAD), BF16), SDS((T, DFF_PAD), BF16),
                   SDS((T, DFF_PAD), BF16), SDS((nb, 3, D), F32), SDS((1, D), F32), SDS((1, D), F32)]
        + [SDS((3,) + h.shape[1:], h.dtype) for h in (scatter or [])] + [SDS((8, LANE), F32)] * (ne - ns),
        scratch_shapes=_resident_scratch()
        + ([pltpu.SemaphoreType.DMA((ns, 3)), pltpu.SemaphoreType.DMA((ns, 3))] if ns else []),
        compiler_params=_cp("arbitrary"), name=name,
    )(dxo, x, y0, mod3, g_pre, g_post, gate, up, wg, wu, wd, *(scatter or []), *([] if target is None else [target]))
    return outs[:8], outs[8:]


def matmul_tn(a, b, tm, tn, tk, name, scatter=None):
    T, M = a.shape
    N = b.shape[1]
    grid = (M // tm, N // tn, T // tk)
    ns = 0 if scatter is None else len(scatter)

    def body(*refs):
        a_ref, b_ref = refs[:2]
        o_ref = refs[2 + ns]
        ids = [pl.program_id(ax) for ax in range(3)]
        if ns:
            comm = (refs[2:2 + ns], refs[3 + ns:3 + 2 * ns], *refs[3 + 2 * ns:])

            @pl.when((ids[0] == 0) & (ids[1] == 0) & (ids[2] == 0))
            def _():
                for cp in _scatter_copies(*comm):
                    cp.start()

        @pl.when(ids[2] == 0)
        def _():
            o_ref[...] = jnp.zeros_like(o_ref)

        o_ref[...] += _dot_tn(a_ref[...], b_ref[...])

        if ns:
            @pl.when((ids[0] == grid[0] - 1) & (ids[1] == grid[1] - 1) & (ids[2] == grid[2] - 1))
            def _():
                for cp in _scatter_copies(*comm):
                    cp.wait()

    outs = pl.pallas_call(
        body, grid=grid,
        in_specs=[pl.BlockSpec((tk, tm), lambda i, j, k: (k, i)), pl.BlockSpec((tk, tn), lambda i, j, k: (k, j))] + [HBM] * ns,
        out_specs=[pl.BlockSpec((tm, tn), lambda i, j, k: (i, j))] + [HBM] * ns,
        out_shape=[SDS((M, N), F32)] + [SDS((3,) + h.shape[1:], h.dtype) for h in (scatter or [])],
        scratch_shapes=[pltpu.SemaphoreType.DMA((ns, 3)), pltpu.SemaphoreType.DMA((ns, 3))] if ns else [],
        compiler_params=_cp("arbitrary", "arbitrary", "arbitrary"), name=name,
    )(a, b, *(scatter or []))
    return outs[0] if scatter is None else (outs[0], outs[1:])


def matmul_tn_cols(a, bs, tk, name):
    T, M = a.shape
    n = bs[0].shape[1]
    ng = len(bs)

    def body(*refs):
        a_ref, b_refs, o_ref = refs[0], refs[1:1 + ng], refs[1 + ng]

        @pl.when(pl.program_id(0) == 0)
        def _():
            o_ref[...] = jnp.zeros_like(o_ref)

        av = a_ref[...]
        for g, b_ref in enumerate(b_refs):
            o_ref[:, g * n:(g + 1) * n] += _dot_tn(av, b_ref[...])

    return pl.pallas_call(
        body, grid=(T // tk,),
        in_specs=[pl.BlockSpec((tk, M), lambda k: (k, 0))] + [pl.BlockSpec((tk, n), lambda k: (k, 0))] * ng,
        out_specs=pl.BlockSpec((M, ng * n), lambda k: (0, 0)), out_shape=SDS((M, ng * n), F32),
        compiler_params=_cp("arbitrary"), name=name,
    )(a, *bs)


def rope_tables(pos_col, name):
    T = pos_col.shape[0]
    tm = 1024

    def body(p_ref, c_ref, s1_ref, s2_ref):
        lane = lax.broadcasted_iota(jnp.int32, (1, LANE), 1)
        l64 = lane % HD
        inv_freq = jnp.exp((l64 % 8).astype(F32) * (-math.log(ROPE_THETA) / 8.0))
        ang = p_ref[...].astype(F32) * inv_freq
        cs = jnp.cos(ang)
        sn = jnp.sin(ang)
        c_ref[...] = jnp.where(l64 < 16, cs, 1.0)
        s1_ref[...] = jnp.where(l64 < 8, -sn, 0.0)
        s2_ref[...] = jnp.where((l64 >= 8) & (l64 < 16), sn, 0.0)

    tab = pl.BlockSpec((tm, LANE), lambda i: (i, 0))
    return pl.pallas_call(
        body, grid=(T // tm,), in_specs=[pl.BlockSpec((tm, 1), lambda i: (i, 0))], out_specs=[tab, tab, tab],
        out_shape=[SDS((T, LANE), F32)] * 3, compiler_params=_cp("arbitrary"), name=name,
    )(pos_col)


def mixer_proj(x, mod3, g_pre, w_main, w_f, rc, rs1, rs2, name):
    T = x.shape[0]

    def body(x_ref, mod_ref, g_ref, w_ref, wf_ref, c_ref, s1_ref, s2_ref, h_ref, pa_ref, pb_ref, f_ref):
        xf = x_ref[...]
        h = (xf * _rms(xf) * g_ref[...]) * (1.0 + mod_ref[0, 1:2, :]) + mod_ref[0, 0:1, :]
        hb = h.astype(BF16)
        h_ref[...] = hb
        f_ref[...] = _dot(hb, wf_ref[...])
        c, s1, s2 = c_ref[...], s1_ref[...], s2_ref[...]
        for grp in range(2):
            pr = _dot(hb, w_ref[:, grp * WG:(grp + 1) * WG])
            for k in range(WG // LANE):
                t = pr[:, k * LANE:(k + 1) * LANE]
                pa_ref[:, grp * WG + k * LANE:grp * WG + (k + 1) * LANE] = (
                    t * c + pltpu.roll(t, LANE - 8, 1) * s1 + pltpu.roll(t, 8, 1) * s2)
        pa_ref[:, 2 * WG:3 * WG] = _dot(hb, w_ref[:, 2 * WG:3 * WG])
        for grp in range(3):
            pb_ref[:, grp * WG:(grp + 1) * WG] = _dot(hb, w_ref[:, (3 + grp) * WG:(4 + grp) * WG]).astype(BF16)

    tok = pl.BlockSpec((TM, D), lambda i: (i, 0))
    vec = pl.BlockSpec((1, D), lambda i: (0, 0))
    tab = pl.BlockSpec((TM, LANE), lambda i: (i, 0))
    grp3 = pl.BlockSpec((TM, 3 * WG), lambda i: (i, 0))
    return pl.pallas_call(
        body, grid=(T // TM,),
        in_specs=[tok, pl.BlockSpec((1, 3, D), _mod_map), vec, pl.BlockSpec((D, IN_MAIN), lambda i: (0, 0)),
                  pl.BlockSpec((D, LANE), lambda i: (0, 0)), tab, tab, tab],
        out_specs=[tok, grp3, grp3, tab],
        out_shape=[SDS((T, D), BF16), SDS((T, 3 * WG), F32), SDS((T, 3 * WG), BF16), SDS((T, LANE), F32)],
        compiler_params=_cp("arbitrary"), name=name,
    )(x, mod3, g_pre, w_main, w_f, rc, rs1, rs2)


def _head_lanes():
    return lax.broadcasted_iota(jnp.int32, (1, LANE), 1) < HD


def _pair(m0, a, b):
    return jnp.where(m0, a, b)


def _band_rows(i, d, nbc):
    if nbc == 1:
        return i, i, 0
    r, mb = i // nbc, i % nbc
    return r + mb * (QB * d), r + jnp.maximum(mb - 1, 0) * (QB * d), jnp.where(mb > 0, QB, 0)


def _rows(start, size, d):
    return pl.ds(pl.multiple_of(start, QB), size) if d == 1 else pl.ds(start, size, stride=d)


def _band_valid(span, off):
    rq = lax.broadcasted_iota(jnp.int32, (QB, span), 0)
    rel = lax.broadcasted_iota(jnp.int32, (QB, span), 1) - off
    return (rel <= rq) & (rel >= rq - QB)


def band_fwd(pa, name):
    T = pa.shape[0]
    B = T // SEQ
    NP = WG // LANE

    def body(q_ref, k_ref, v_ref, out_ref, lse_ref, o_s, l_s):
        m0 = _head_lanes()
        for pidx, (d, nbc) in enumerate(PATTERNS):
            span = QB if nbc == 1 else 2 * QB

            def blk(it, carry, pidx=pidx, d=d, nbc=nbc, span=span):
                ld = []
                for u in range(BAND_UNROLL):
                    qs, ks, off = _band_rows(it * BAND_UNROLL + u, d, nbc)
                    q = q_ref[_rows(qs, QB, d), :] * ATTN_SCALE
                    ld.append((qs, q, k_ref[_rows(ks, span, d), :].astype(BF16), v_ref[_rows(ks, span, d), :].astype(BF16),
                               _band_valid(span, off)))
                ss = [[jnp.where(valid, _dot_nt(jnp.where(mh, q, 0.0).astype(BF16), k), NEG) for mh in (m0, jnp.logical_not(m0))]
                      for _, q, k, _, valid in ld]
                ps = []
                for pair in ss:
                    row = []
                    for s in pair:
                        m = jnp.max(s, axis=-1, keepdims=True)
                        p = jnp.exp(s - m)
                        row.append((p.astype(BF16), jnp.sum(p, axis=-1, keepdims=True), m))
                    ps.append(row)
                pv = [[_dot(p, ld[u][3]) for p, _, _ in ps[u]] for u in range(BAND_UNROLL)]
                for u in range(BAND_UNROLL):
                    rows = _rows(ld[u][0], QB, d)
                    (_, l0, mx0), (_, l1, mx1) = ps[u]
                    o_s[pidx, rows, :] = _pair(m0, pv[u][0] / l0, pv[u][1] / l1)
                    l_s[pidx, rows, :] = _pair(m0, mx0 + jnp.log(l0), mx1 + jnp.log(l1))
                return carry

            lax.fori_loop(0, SEQ // QB // BAND_UNROLL, blk, 0)
        for c in range(SEQ // ROWS):
            sl = slice(c * ROWS, (c + 1) * ROWS)
            a, b, e = l_s[0, sl, :], l_s[1, sl, :], l_s[2, sl, :]
            m = jnp.maximum(jnp.maximum(a, b), e)
            L = m + jnp.log(jnp.exp(a - m) + jnp.exp(b - m) + jnp.exp(e - m))
            out_ref[sl, :] = jnp.exp(a - L) * o_s[0, sl, :] + jnp.exp(b - L) * o_s[1, sl, :] + jnp.exp(e - L) * o_s[2, sl, :]
            lse_ref[sl, :] = L

    blk_of = lambda g: pl.BlockSpec((SEQ, LANE), lambda b, hp, g=g: (b, g * NP + hp))
    return pl.pallas_call(
        body, grid=(B, NP), in_specs=[blk_of(0), blk_of(1), blk_of(2)], out_specs=[blk_of(0), blk_of(0)],
        out_shape=[SDS((T, WG), F32), SDS((T, WG), F32)],
        scratch_shapes=[pltpu.VMEM((3, SEQ, LANE), F32), pltpu.VMEM((3, SEQ, LANE), F32)],
        compiler_params=_cp("arbitrary", "arbitrary"), name=name,
    )(pa, pa, pa)


def _pair_rowsum(m0, prod):
    s0 = jnp.sum(jnp.where(m0, prod, 0.0), axis=-1, keepdims=True)
    return _pair(m0, s0, jnp.sum(prod, axis=-1, keepdims=True) - s0)


def band_bwd(pa, do, out, lse, rc, rs1, rs2, name):
    T = pa.shape[0]
    B = T // SEQ
    NP = WG // LANE

    def body(q_ref, k_ref, v_ref, do_ref, out_ref, l_ref, c_ref, s1_ref, s2_ref, dqo_ref, dko_ref, dvo_ref, d_s, dq_ref, dk_ref,
             dv_ref):
        m0 = _head_lanes()
        dq_ref[...] = jnp.zeros_like(dq_ref)
        dk_ref[...] = jnp.zeros_like(dk_ref)
        dv_ref[...] = jnp.zeros_like(dv_ref)
        for c in range(SEQ // ROWS):
            sl = slice(c * ROWS, (c + 1) * ROWS)
            d_s[sl, :] = _pair_rowsum(m0, do_ref[sl, :] * out_ref[sl, :])
        for d, nbc in PATTERNS:
            span = QB if nbc == 1 else 2 * QB

            def blk(it, carry, d=d, nbc=nbc, span=span):
                masks = (m0, jnp.logical_not(m0))
                ld = []
                for u in range(BAND_UNROLL_BWD):
                    qs, ks, off = _band_rows(it * BAND_UNROLL_BWD + u, d, nbc)
                    qrow, krow = _rows(qs, QB, d), _rows(ks, span, d)
                    ld.append(dict(qrow=qrow, krow=krow, q=q_ref[qrow, :] * ATTN_SCALE, k=k_ref[krow, :].astype(BF16),
                                   v=v_ref[krow, :].astype(BF16), do=do_ref[qrow, :], l=l_ref[qrow, :], dv=d_s[qrow, :],
                                   valid=_band_valid(span, off)))
                for t in ld:
                    t["qm"] = [jnp.where(mh, t["q"], 0.0).astype(BF16) for mh in masks]
                    t["dom"] = [jnp.where(mh, t["do"], 0.0).astype(BF16) for mh in masks]
                sd = [[(jnp.where(t["valid"], _dot_nt(t["qm"][h], t["k"]), NEG), _dot_nt(t["dom"][h], t["v"])) for h in range(2)]
                      for t in ld]
                pd = []
                for t, pair in zip(ld, sd):
                    row = []
                    for h, (s, dp) in enumerate(pair):
                        col = slice(h * HD, h * HD + 1)
                        p = jnp.exp(s - t["l"][:, col])
                        row.append((p.astype(BF16), (p * (dp - t["dv"][:, col])).astype(BF16)))
                    pd.append(row)
                gr = [(_dot(row[0][1], t["k"]), _dot(row[1][1], t["k"]),
                       _dot_tn(jnp.concatenate([row[0][1], row[1][1]], axis=0), jnp.concatenate(t["qm"], axis=0)),
                       _dot_tn(jnp.concatenate([row[0][0], row[1][0]], axis=0), jnp.concatenate(t["dom"], axis=0)))
                      for t, row in zip(ld, pd)]
                for t, (dq0, dq1, dk, dv) in zip(ld, gr):
                    dq_ref[t["qrow"], :] += _pair(m0, dq0, dq1) * ATTN_SCALE
                    dk_ref[t["krow"], :] += dk
                    dv_ref[t["krow"], :] += dv
                return carry

            lax.fori_loop(0, SEQ // QB // BAND_UNROLL_BWD, blk, 0)
        for c in range(SEQ // ROWS):
            sl = slice(c * ROWS, (c + 1) * ROWS)
            cc, s1, s2 = c_ref[sl, :], s1_ref[sl, :], s2_ref[sl, :]
            for acc, o_ref in ((dq_ref, dqo_ref), (dk_ref, dko_ref)):
                d = acc[sl, :]
                o_ref[sl, :] = (d * cc + pltpu.roll(d * s1, 8, 1) + pltpu.roll(d * s2, LANE - 8, 1)).astype(BF16)
            dvo_ref[sl, :] = dv_ref[sl, :].astype(BF16)

    blk_of = lambda g: pl.BlockSpec((SEQ, LANE), lambda b, hp, g=g: (b, g * NP + hp))
    tab = pl.BlockSpec((SEQ, LANE), lambda b, hp: (b, 0))
    return pl.pallas_call(
        body, grid=(B, NP), in_specs=[blk_of(0), blk_of(1), blk_of(2), blk_of(0), blk_of(0), blk_of(0), tab, tab, tab],
        out_specs=[blk_of(0)] * 3, out_shape=[SDS((T, WG), BF16)] * 3,
        scratch_shapes=[pltpu.VMEM((SEQ, LANE), F32)] * 4,
        compiler_params=_cp("arbitrary", "arbitrary"), name=name,
    )(pa, pa, pa, do, out, lse, rc, rs1, rs2)


def _tile_causal(nq, nk, q0, k0):
    r = lax.broadcasted_iota(jnp.int32, (nq, nk), 0)
    c = lax.broadcasted_iota(jnp.int32, (nq, nk), 1)
    return r + (q0 - k0) >= c


def _row_to_col(row):
    n = row.shape[1]
    return jnp.transpose(jnp.broadcast_to(row, (LANE, n)))[:, 0:1]


def _col_to_row(col):
    n = col.shape[0]
    return jnp.transpose(jnp.broadcast_to(col, (n, LANE)))[0:1, :]


def fox_fwd(pb, fblk, frow, name, gather=None):
    FQ = FOX_QB
    T = pb.shape[0]
    B = T // SEQ
    NG = WG // (LANE * FOX_PAIRS)
    NHS = 2 * FOX_PAIRS
    W = LANE * FOX_PAIRS
    n = SEQ // FQ
    ng = 0 if gather is None else len(gather[0])
    plan = None if gather is None else ShardGather([w.shape for w in gather[0]], gather[1])

    def body(*refs):
        q_ref, k_ref, v_ref, fc_ref, fr_ref = refs[:5]
        o_ref, lse_ref = refs[5 + ng:7 + ng]
        if plan is not None:
            comm = (refs[5:5 + ng], refs[7 + ng:7 + 2 * ng], refs[7 + 2 * ng:])
            ids = [pl.program_id(ax) for ax in range(3)]
            pl.when((ids[0] == 0) & (ids[1] == 0) & (ids[2] == 0))(lambda: plan.start(*comm))
        i = pl.program_id(2)
        m0 = _head_lanes()
        masks = (m0, jnp.logical_not(m0))
        heads = [(hh, slice((hh // 2) * LANE, (hh // 2 + 1) * LANE), masks[hh % 2]) for hh in range(NHS)]
        qh, fq = [], []
        for hh, lanes, mh in heads:
            q = q_ref[:, lanes] * ATTN_SCALE
            qh.append(jnp.where(mh, q, jnp.zeros_like(q)))
            fq.append(_row_to_col(fc_ref[0, hh, 0]))

        def step(t, carry, masked):
            rows = pl.ds(pl.multiple_of(t * FT, FT), FT)
            ss = [_dot_nt(qh[hh], k_ref[rows, lanes]) + fq[hh] - fr_ref[0, hh, t] for hh, lanes, _ in heads]
            if masked:
                ok = _tile_causal(FQ, FT, i * FQ, t * FT)
                ss = [jnp.where(ok, s, NEG) for s in ss]
            st = []
            for hh, _, _ in heads:
                m2 = jnp.maximum(carry[hh][0], jnp.max(ss[hh], axis=-1, keepdims=True))
                st.append((m2, jnp.exp(carry[hh][0] - m2), jnp.exp(ss[hh] - m2).astype(BF16)))
            pv = []
            for hh, lanes, mh in heads:
                vt = v_ref[rows, lanes]
                pv.append(_dot(st[hh][2], jnp.where(mh, vt, jnp.ones_like(vt))))
            return tuple((st[hh][0], st[hh][1] * carry[hh][1] + pv[hh]) for hh in range(NHS))

        one = (jnp.full((FQ, 1), NEG, F32), jnp.zeros((FQ, LANE), F32))
        last = (i * FQ) // FT
        carry = lax.fori_loop(0, last, lambda t, cr: step(t, cr, False), (one,) * NHS)
        carry = step(last, carry, True)
        for pr in range(FOX_PAIRS):
            (ma, acca), (mb, accb) = carry[2 * pr], carry[2 * pr + 1]
            la, lb = acca[:, HD:HD + 1], accb[:, 0:1]
            o_ref[:, pr * LANE:(pr + 1) * LANE] = _pair(m0, acca / la, accb / lb)
            lse_ref[0, 2 * pr, 0] = _col_to_row(ma + jnp.log(la))
            lse_ref[0, 2 * pr + 1, 0] = _col_to_row(mb + jnp.log(lb))
        if plan is not None:
            pl.when((ids[0] == B - 1) & (ids[1] == NG - 1) & (ids[2] == n - 1))(lambda: plan.finish(*comm))

    qblk = pl.BlockSpec((FQ, W), lambda b, g, i: (b * n + i, g))
    full = lambda grp: pl.BlockSpec((SEQ, W), lambda b, g, i, grp=grp: (b, grp * NG + g))
    rowb = pl.BlockSpec((1, NHS, 1, 1, FQ), lambda b, g, i: (b, g, i, 0, 0))
    outs = pl.pallas_call(
        body, grid=(B, NG, n),
        in_specs=[qblk, full(1), full(2), rowb, pl.BlockSpec((1, NHS, SEQ // FT, 1, FT), lambda b, g, i: (b, g, 0, 0, 0))]
        + [HBM] * ng,
        out_specs=[qblk, rowb] + [HBM] * ng,
        out_shape=[SDS((T, WG), F32), SDS((B, NH, n, 1, FQ), F32)] + ([] if plan is None else plan.out_shapes(BF16)),
        scratch_shapes=[] if plan is None else plan.scratch(),
        compiler_params=_cp("arbitrary", "arbitrary", "arbitrary"), name=name,
    )(pb, pb, pb, fblk, frow, *([] if gather is None else gather[0]))
    return outs[:2], outs[2:]


def fox_bwd(pb, do, lrow, drow, fblk, frow, name):
    T = pb.shape[0]
    B = T // SEQ
    PAIRS = FOX_PAIRS_BWD
    NG = WG // (LANE * PAIRS)
    NHS = 2 * PAIRS
    W = LANE * PAIRS
    n = SEQ // FB

    def body(q_ref, k_ref, v_ref, do_ref, l_ref, d_ref, fc_ref, fr_ref, dqo_ref, dk_ref, dv_ref, dfq_ref, dfk_ref, dq_ref):
        j = pl.program_id(2)
        m0 = _head_lanes()
        masks = (m0, jnp.logical_not(m0))
        heads = [(hh, slice((hh // 2) * LANE, (hh // 2 + 1) * LANE), masks[hh % 2]) for hh in range(NHS)]

        @pl.when(j == 0)
        def _():
            dq_ref[...] = jnp.zeros_like(dq_ref)
            dfq_ref[...] = jnp.zeros_like(dfq_ref)

        kj = [k_ref[:, lanes] for _, lanes, _ in heads]
        vj = [v_ref[:, lanes] for _, lanes, _ in heads]
        fk = [_row_to_col(fc_ref[0, hh, 0]) for hh in range(NHS)]

        def step(t, carry, masked):
            rows = pl.ds(pl.multiple_of(t * FT, FT), FT)
            qm, dom = [], []
            for _, lanes, mh in heads:
                qt = q_ref[rows, lanes] * ATTN_SCALE
                qm.append(jnp.where(mh, qt, jnp.zeros_like(qt)))
                dom.append(jnp.where(mh, do_ref[rows, lanes], 0.0).astype(BF16))
            ss = [_dot_nt(kj[hh], qm[hh]) + fr_ref[0, hh, t] - fk[hh] for hh in range(NHS)]
            dps = [_dot_nt(vj[hh], dom[hh]) for hh in range(NHS)]
            if masked:
                key = lax.broadcasted_iota(jnp.int32, (FB, FT), 0)
                qry = lax.broadcasted_iota(jnp.int32, (FB, FT), 1)
                ok = qry + (t * FT - j * FB) >= key
                ss = [jnp.where(ok, s, NEG) for s in ss]
            pds = []
            for hh in range(NHS):
                p = jnp.exp(ss[hh] - l_ref[0, hh, t])
                ds = p * (dps[hh] - d_ref[0, hh, t])
                dfq_ref[0, hh, t] += jnp.sum(ds, axis=0, keepdims=True)
                pds.append((p.astype(BF16), ds.astype(BF16), jnp.sum(ds, axis=-1, keepdims=True)))
            dks = [_dot(pds[hh][1], qm[hh]) for hh in range(NHS)]
            dvs = [_dot(pds[hh][0], dom[hh]) for hh in range(NHS)]
            dqs = [_dot_tn(pds[hh][1], kj[hh]) for hh in range(NHS)]
            for pr in range(PAIRS):
                dq_ref[rows, pr * LANE:(pr + 1) * LANE] += _pair(m0, dqs[2 * pr], dqs[2 * pr + 1]) * ATTN_SCALE
            return tuple((carry[hh][0] + dks[hh], carry[hh][1] + dvs[hh], carry[hh][2] - pds[hh][2]) for hh in range(NHS))

        one = (jnp.zeros((FB, LANE), F32), jnp.zeros((FB, LANE), F32), jnp.zeros((FB, 1), F32))
        first = (j * FB) // FT
        carry = step(first, (one,) * NHS, True)
        carry = lax.fori_loop(first + 1, SEQ // FT, lambda t, cr: step(t, cr, False), carry)
        for pr in range(PAIRS):
            (dka, dva, dfka), (dkb, dvb, dfkb) = carry[2 * pr], carry[2 * pr + 1]
            dk_ref[:, pr * LANE:(pr + 1) * LANE] = _pair(m0, dka, dkb).astype(BF16)
            dv_ref[:, pr * LANE:(pr + 1) * LANE] = _pair(m0, dva, dvb).astype(BF16)
            dfk_ref[0, 2 * pr, 0] = _col_to_row(dfka)
            dfk_ref[0, 2 * pr + 1, 0] = _col_to_row(dfkb)

        @pl.when(j == n - 1)
        def _():
            dqo_ref[...] = dq_ref[...].astype(BF16)

    kblk = lambda grp: pl.BlockSpec((FB, W), lambda b, g, j, grp=grp: (b * n + j, grp * NG + g))
    full = pl.BlockSpec((SEQ, W), lambda b, g, j: (b, g))
    rowf = pl.BlockSpec((1, NHS, SEQ // FT, 1, FT), lambda b, g, j: (b, g, 0, 0, 0))
    rowb = pl.BlockSpec((1, NHS, 1, 1, FB), lambda b, g, j: (b, g, j, 0, 0))
    return pl.pallas_call(
        body, grid=(B, NG, n), in_specs=[full, kblk(1), kblk(2), full, rowf, rowf, rowb, rowf],
        out_specs=[full, kblk(0), kblk(0), rowf, rowb],
        out_shape=[SDS((T, WG), BF16), SDS((T, WG), BF16), SDS((T, WG), BF16), SDS((B, NH, SEQ // FT, 1, FT), F32),
                   SDS((B, NH, n, 1, FB), F32)],
        scratch_shapes=[pltpu.VMEM((SEQ, W), F32)],
        compiler_params=_cp("arbitrary", "arbitrary", "arbitrary"), name=name,
    )(pb, pb, pb, do, lrow, drow, fblk, frow)


def _tri(lower):
    r = lax.broadcasted_iota(jnp.int32, (LANE, LANE), 0)
    c = lax.broadcasted_iota(jnp.int32, (LANE, LANE), 1)
    return ((r >= c) if lower else (r <= c)).astype(F32)


def _tri_dot(t, xblk):
    return jnp.dot(t, xblk, precision=lax.Precision.HIGHEST, preferred_element_type=F32)


def forget_cumsum(flog, bias, name):
    B, S, _ = flog.shape

    def body(f_ref, b_ref, o_ref):
        t = _tri(True)
        carry = jnp.zeros((1, LANE), F32)
        for blk in range(S // LANE):
            z = f_ref[0, blk * LANE:(blk + 1) * LANE, :] + b_ref[...]
            lf = jnp.minimum(z, 0.0) - jnp.log(1.0 + jnp.exp(-jnp.abs(z)))
            cs = _tri_dot(t, lf) + carry
            o_ref[0, blk * LANE:(blk + 1) * LANE, :] = cs
            carry = cs[LANE - 1:LANE, :]

    spec = pl.BlockSpec((1, S, LANE), lambda b: (b, 0, 0))
    return pl.pallas_call(
        body, grid=(B,), in_specs=[spec, pl.BlockSpec((1, LANE), lambda b: (0, 0))], out_specs=spec,
        out_shape=SDS((B, S, LANE), F32), compiler_params=_cp("arbitrary"), name=name,
    )(flog, bias)


def forget_cumsum_bwd(dF, flog, bias, name):
    B, S, _ = flog.shape

    def body(d_ref, f_ref, b_ref, o_ref, db_ref):
        @pl.when(pl.program_id(0) == 0)
        def _():
            db_ref[...] = jnp.zeros_like(db_ref)

        t = _tri(False)
        carry = jnp.zeros((1, LANE), F32)
        tot = jnp.zeros((1, LANE), F32)
        for blk in reversed(range(S // LANE)):
            sl = slice(blk * LANE, (blk + 1) * LANE)
            rc = _tri_dot(t, d_ref[0, sl, :]) + carry
            carry = rc[0:1, :]
            z = f_ref[0, sl, :] + b_ref[...]
            dz = rc * jax.nn.sigmoid(-z)
            o_ref[0, sl, :] = dz
            tot = tot + jnp.sum(dz, axis=0, keepdims=True)
        db_ref[...] += tot

    spec = pl.BlockSpec((1, S, LANE), lambda b: (b, 0, 0))
    vec = pl.BlockSpec((1, LANE), lambda b: (0, 0))
    return pl.pallas_call(
        body, grid=(B,), in_specs=[spec, spec, vec], out_specs=[spec, vec],
        out_shape=[SDS((B, S, LANE), F32), SDS((1, LANE), F32)], compiler_params=_cp("arbitrary"), name=name,
    )(dF, flog, bias)


def mixer_out_fwd(oa, ob, goa, gob, w_out, g_post, x, mod3, name):
    T = x.shape[0]

    def body(oa_ref, ob_ref, goa_ref, gob_ref, w_ref, gp_ref, x_ref, mod_ref, xo_ref, mg_ref, y0_ref):
        a = oa_ref[...]
        b = ob_ref[...]
        mg = jnp.concatenate([a * _rms(a) * goa_ref[...], b * _rms(b) * gob_ref[...]], axis=-1).astype(BF16)
        mg_ref[...] = mg
        y0 = _dot(mg, w_ref[...])
        y0_ref[...] = y0
        xo_ref[...] = x_ref[...] + mod_ref[0, 2:3, :] * (y0 * _rms(y0) * gp_ref[...])

    tok = pl.BlockSpec((TM, D), lambda i: (i, 0))
    half = pl.BlockSpec((TM, WG), lambda i: (i, 0))
    hv = pl.BlockSpec((1, WG), lambda i: (0, 0))
    return pl.pallas_call(
        body, grid=(T // TM,),
        in_specs=[half, half, hv, hv, pl.BlockSpec((D, D), lambda i: (0, 0)), pl.BlockSpec((1, D), lambda i: (0, 0)), tok,
                  pl.BlockSpec((1, 3, D), _mod_map)],
        out_specs=[tok, tok, tok], out_shape=[SDS((T, D), F32), SDS((T, D), BF16), SDS((T, D), F32)],
        compiler_params=_cp("arbitrary"), name=name,
    )(oa, ob, goa, gob, w_out, g_post, x, mod3)


def mixer_out_bwd(dxo, y0, mod3, g_post, w_out, oa, ob, goa, gob, name):
    T = dxo.shape[0]
    nb = T // SEQ
    tiles_per_seq = SEQ // TM

    def body(dxo_ref, y0_ref, mod_ref, gp_ref, w_ref, oa_ref, ob_ref, goa_ref, gob_ref,
             dy0_ref, doa_ref, dob_ref, dmg_ref, dgp_ref, dgoa_ref, dgob_ref, dvb_ref):
        i = pl.program_id(0)

        @pl.when(i == 0)
        def _():
            dgp_ref[...] = jnp.zeros_like(dgp_ref)
            dgoa_ref[...] = jnp.zeros_like(dgoa_ref)
            dgob_ref[...] = jnp.zeros_like(dgob_ref)

        @pl.when(i % tiles_per_seq == 0)
        def _():
            dmg_ref[...] = jnp.zeros_like(dmg_ref)

        dy0, dmg, dg = _post_bwd(dxo_ref[...], y0_ref[...], gp_ref[...], mod_ref[0, 2:3, :], 1.0)
        dmg_ref[0] += dmg
        dgp_ref[...] += dg
        db = dy0.astype(BF16)
        dy0_ref[...] = db
        dm = _dot_nt(db, w_ref[...])
        for o_ref, g_ref, do_ref, dg_ref, sl in ((oa_ref, goa_ref, doa_ref, dgoa_ref, slice(0, WG)),
                                                  (ob_ref, gob_ref, dob_ref, dgob_ref, slice(WG, 2 * WG))):
            o = o_ref[...]
            r = _rms(o)
            oh = o * r
            d = dm[:, sl]
            dg_ref[...] += jnp.sum(d * oh, axis=0, keepdims=True)
            dh = d * g_ref[...]
            do = r * (dh - oh * jnp.mean(dh * oh, axis=-1, keepdims=True))
            do_ref[...] = do
        ind = (lax.broadcasted_iota(jnp.int32, (WG, LANE), 0) // HD == lax.broadcasted_iota(jnp.int32, (WG, LANE), 1)).astype(BF16)
        prod = do * o
        hi = prod.astype(BF16)
        dvb_ref[...] = _dot(hi, ind) + _dot((prod - hi.astype(F32)).astype(BF16), ind)

    tok = pl.BlockSpec((TM, D), lambda i: (i, 0))
    half = pl.BlockSpec((TM, WG), lambda i: (i, 0))
    hv = pl.BlockSpec((1, WG), lambda i: (0, 0))
    vec = pl.BlockSpec((1, D), lambda i: (0, 0))
    return pl.pallas_call(
        body, grid=(T // TM,),
        in_specs=[tok, tok, pl.BlockSpec((1, 3, D), _mod_map), vec, pl.BlockSpec((D, D), lambda i: (0, 0)), half, half, hv, hv],
        out_specs=[tok, half, half, pl.BlockSpec((1, 1, D), _mod_map), vec, hv, hv, pl.BlockSpec((TM, LANE), lambda i: (i, 0))],
        out_shape=[SDS((T, D), BF16), SDS((T, WG), F32), SDS((T, WG), F32), SDS((nb, 1, D), F32), SDS((1, D), F32),
                   SDS((1, WG), F32), SDS((1, WG), F32), SDS((T, LANE), F32)],
        compiler_params=_cp("arbitrary"), name=name,
    )(dxo, y0, mod3, g_post, w_out, oa, ob, goa, gob)


def mixer_proj_bwd(dps, dflog, dxo, x, mod3, g_pre, w_main, w_f, name):
    T = x.shape[0]
    nb = T // SEQ
    tiles_per_seq = SEQ // TM
    ngrp = len(dps)

    def body(*refs):
        dp_refs = refs[:ngrp]
        df_ref, dxo_ref, x_ref, mod_ref, g_ref, w_ref, wf_ref, dx_ref, dmod_ref, dg_ref = refs[ngrp:]
        i = pl.program_id(0)

        @pl.when(i == 0)
        def _():
            dg_ref[...] = jnp.zeros_like(dg_ref)

        @pl.when(i % tiles_per_seq == 0)
        def _():
            dmod_ref[...] = jnp.zeros_like(dmod_ref)

        dh = _dot_nt(df_ref[...].astype(BF16), wf_ref[...])
        for g, dp_ref in enumerate(dp_refs):
            dh = dh + _dot_nt(dp_ref[...], w_ref[:, g * WG:(g + 1) * WG])
        dx, dsh, dsc, dg = _norm_mod_bwd(dh, x_ref[...], g_ref[...], mod_ref[0, 1:2, :])
        dx_ref[...] = dxo_ref[...] + dx
        dmod_ref[0, 0:1, :] += dsh
        dmod_ref[0, 1:2, :] += dsc
        dg_ref[...] += dg

    tok = pl.BlockSpec((TM, D), lambda i: (i, 0))
    vec = pl.BlockSpec((1, D), lambda i: (0, 0))
    return pl.pallas_call(
        body, grid=(T // TM,),
        in_specs=[pl.BlockSpec((TM, WG), lambda i: (i, 0))] * ngrp
        + [pl.BlockSpec((TM, LANE), lambda i: (i, 0)), tok, tok, pl.BlockSpec((1, 3, D), _mod_map), vec,
           pl.BlockSpec((D, IN_MAIN), lambda i: (0, 0)), pl.BlockSpec((D, LANE), lambda i: (0, 0))],
        out_specs=[tok, pl.BlockSpec((1, 2, D), _mod_map), vec],
        out_shape=[SDS((T, D), F32), SDS((nb, 2, D), F32), SDS((1, D), F32)],
        compiler_params=_cp("arbitrary"), name=name,
    )(*dps, dflog, dxo, x, mod3, g_pre, w_main, w_f)


def ada_fwd(c_all, w, b, name):
    n = w.shape[1]
    tn = n // 2

    def body(c_ref, w_ref, b_ref, o_ref):
        cv = c_ref[...]
        o_ref[...] = _dot((cv * jax.nn.sigmoid(cv)).astype(BF16), w_ref[...].astype(BF16)) + b_ref[...]

    R = c_all.shape[0]
    return pl.pallas_call(
        body, grid=(2,),
        in_specs=[pl.BlockSpec((R, D), lambda j: (0, 0)), pl.BlockSpec((D, tn), lambda j: (0, j)), pl.BlockSpec((1, tn), lambda j: (0, j))],
        out_specs=pl.BlockSpec((R, tn), lambda j: (0, j)), out_shape=SDS((R, n), F32),
        compiler_params=_cp("arbitrary"), name=name,
    )(c_all, w, b)


def ada_bwd(c_all, dmod, name):
    R, n = dmod.shape
    tn = n // 2

    def body(c_ref, d_ref, o_ref):
        cv = c_ref[...]
        o_ref[...] = _dot_tn((cv * jax.nn.sigmoid(cv)).astype(BF16), d_ref[...].astype(BF16))

    return pl.pallas_call(
        body, grid=(2,), in_specs=[pl.BlockSpec((R, D), lambda j: (0, 0)), pl.BlockSpec((R, tn), lambda j: (0, j))],
        out_specs=pl.BlockSpec((D, tn), lambda j: (0, j)), out_shape=SDS((D, n), F32),
        compiler_params=_cp("arbitrary"), name=name,
    )(c_all, dmod)


def _adam_math(w, g, m, v):
    m2 = ADAM_B1 * m + (1.0 - ADAM_B1) * g
    v2 = ADAM_B2 * v + (1.0 - ADAM_B2) * (g * g)
    m_hat = m2 / (1.0 - ADAM_B1 ** ADAM_STEP)
    v_hat = v2 / (1.0 - ADAM_B2 ** ADAM_STEP)
    delta = -ADAM_LR * (m_hat / (jnp.sqrt(v_hat) + ADAM_EPS) + ADAM_WD * w)
    return delta, m2, v2


def adam_update(w, g, m, v, tr, name):
    _, R, C = w.shape

    def body(w_ref, g_ref, m_ref, v_ref, d_ref, mo_ref, vo_ref):
        d_ref[0], mo_ref[0], vo_ref[0] = _adam_math(w_ref[0], g_ref[...], m_ref[0], v_ref[0])

    spec = pl.BlockSpec((1, tr, C), lambda i: (0, i, 0))
    gspec = pl.BlockSpec((tr, C), lambda i: (i, 0))
    return pl.pallas_call(
        body, grid=(R // tr,), in_specs=[spec, gspec, spec, spec], out_specs=[spec] * 3, out_shape=[SDS((1, R, C), F32)] * 3,
        compiler_params=_cp("arbitrary"), name=name,
    )(w, g, m, v)


def adam_update_halves(w, mine, other, m, v, cidx, tr, name):
    _, R, C = w.shape
    nh = R // 2 // tr

    def body(c_ref, w_ref, a_ref, b_ref, m_ref, v_ref, g_ref, d_ref, mo_ref, vo_ref):
        first_half = pl.program_id(0) < nh
        g = jnp.where(first_half == (c_ref[0] == 0), a_ref[...], b_ref[...])
        g_ref[0] = g
        d_ref[0], mo_ref[0], vo_ref[0] = _adam_math(w_ref[0], g, m_ref[0], v_ref[0])

    spec = pl.BlockSpec((1, tr, C), lambda i, c_ref: (0, i, 0))
    hspec = pl.BlockSpec((tr, C), lambda i, c_ref: (i % nh, 0))
    return pl.pallas_call(
        body,
        grid_spec=pltpu.PrefetchScalarGridSpec(num_scalar_prefetch=1, grid=(R // tr,), in_specs=[spec, hspec, hspec, spec, spec],
                                               out_specs=[spec] * 4),
        out_shape=[SDS((1, R, C), F32)] * 4, compiler_params=_cp("arbitrary"), name=name,
    )(cidx, w, mine, other, m, v)


def vec_adam(parts, w, m, v, name):
    P, C = parts.shape

    def body(p_ref, w_ref, m_ref, v_ref, g_ref, d_ref, mo_ref, vo_ref):
        g = jnp.sum(p_ref[...], axis=0, keepdims=True)
        g_ref[...] = g
        d_ref[...], mo_ref[...], vo_ref[...] = _adam_math(w_ref[...], g, m_ref[...], v_ref[...])

    return pl.pallas_call(body, out_shape=[SDS((1, C), F32)] * 4, compiler_params=_cp(), name=name)(parts, w, m, v)


HBM = pl.BlockSpec(memory_space=pltpu.HBM)
VMEM = pl.BlockSpec(memory_space=pltpu.VMEM)


def _place():
    x, y, c = lax.axis_index("x"), lax.axis_index("y"), lax.axis_index("c")
    return x, y, c, [(1 - x, y), (x, 1 - y), (1 - x, 1 - y)]


def all_gather8(xs, name):
    R, C = xs.shape

    def body(x_ref, out_ref, send_sems, recv_sems, local_sem):
        x, y, c, chips = _place()
        me, sibling = (x, y, c), (x, y, 1 - c)

        def slot(px, py, pc):
            return out_ref.at[4 * px + 2 * py + pc]

        def copy(k, block, to, src=None):
            return pltpu.make_async_remote_copy(
                src_ref=slot(*block) if src is None else src, dst_ref=slot(*block),
                send_sem=send_sems.at[k], recv_sem=recv_sems.at[k], device_id=to, device_id_type=MESH)

        mine = pltpu.make_async_copy(x_ref, slot(*me), local_sem)
        mine.start()
        first = [copy(0, me, sibling, src=x_ref)]
        first += [copy(1 + j, me, (*chip, c), src=x_ref) for j, chip in enumerate(chips)]
        for cp in first:
            cp.start()
        passed = [copy(4 + j, (*chip, c), sibling) for j, chip in enumerate(chips)]
        for j, chip in enumerate(chips):
            copy(1 + j, (*chip, c), me).wait_recv()
            passed[j].start()
        copy(0, sibling, me).wait_recv()
        for j, chip in enumerate(chips):
            copy(4 + j, (*chip, 1 - c), me).wait_recv()
        for cp in first + passed:
            cp.wait_send()
        mine.wait()

    return pl.pallas_call(
        body, out_shape=SDS((N_DEV, R, C), xs.dtype), in_specs=[VMEM], out_specs=VMEM,
        scratch_shapes=[pltpu.SemaphoreType.DMA((7,)), pltpu.SemaphoreType.DMA((7,)), pltpu.SemaphoreType.DMA],
        compiler_params=pltpu.CompilerParams(vmem_limit_bytes=VMEM_LIMIT), name=name,
    )(xs)


class ShardGather:
    def __init__(self, shapes, splits):
        self.shapes, self.splits, self.n = shapes, splits, len(shapes)

    def scratch(self):
        n = self.n
        return [pltpu.SemaphoreType.DMA((n, 6)), pltpu.SemaphoreType.DMA((n, 6)), pltpu.SemaphoreType.DMA((n,))]

    def out_shapes(self, dtype):
        return [SDS((N_SHARD,) + tuple(s), dtype) for s in self.shapes]

    def _half(self, ref, k, cc):
        lo, hi = (0, self.splits[k]) if cc == 0 else (self.splits[k], self.shapes[k][0])
        return ref.at[pl.ds(lo, hi - lo)]

    def _phase(self, w_refs, o_refs, sems, finish):
        send_sems, recv_sems, local_sems = sems
        x, y, c, chips = _place()
        sibling = (x, y, 1 - c)
        me_s = 2 * x + y

        def rcopy(src, dst, k, s, to):
            return pltpu.make_async_remote_copy(src_ref=src, dst_ref=dst, send_sem=send_sems.at[k, s],
                                                recv_sem=recv_sems.at[k, s], device_id=to, device_id_type=MESH)

        for cc in (0, 1):
            @pl.when(c == cc)
            def _():
                local = [pltpu.make_async_copy(w_refs[k], o_refs[k].at[me_s], local_sems.at[k]) for k in range(self.n)]
                first = [rcopy(self._half(w_refs[k], k, cc), self._half(o_refs[k].at[me_s], k, cc), k, j, (*chip, c))
                         for k in range(self.n) for j, chip in enumerate(chips)]
                if not finish:
                    for cp in local + first:
                        cp.start()
                    return
                passed = []
                for k in range(self.n):
                    for j, chip in enumerate(chips):
                        land = self._half(o_refs[k].at[2 * chip[0] + chip[1]], k, cc)
                        rcopy(land, land, k, j, (*chip, c)).wait_recv()
                        f = rcopy(land, land, k, 3 + j, sibling)
                        f.start()
                        passed.append(f)
                for k in range(self.n):
                    for j, chip in enumerate(chips):
                        other = self._half(o_refs[k].at[2 * chip[0] + chip[1]], k, 1 - cc)
                        rcopy(other, other, k, 3 + j, sibling).wait_recv()
                for s in first + passed:
                    s.wait_send()
                for cp in local:
                    cp.wait()

    def start(self, w_refs, o_refs, sems):
        self._phase(w_refs, o_refs, sems, False)

    def finish(self, w_refs, o_refs, sems):
        self._phase(w_refs, o_refs, sems, True)


def all_gather_shards(ws, splits, name):
    n = len(ws)
    plan = ShardGather([w.shape for w in ws], splits)

    def body(*refs):
        plan.start(refs[:n], refs[n:2 * n], refs[2 * n:])
        plan.finish(refs[:n], refs[n:2 * n], refs[2 * n:])

    return pl.pallas_call(
        body, out_shape=plan.out_shapes(ws[0].dtype), in_specs=[HBM] * n, out_specs=[HBM] * n,
        scratch_shapes=plan.scratch(), name=name,
    )(*ws)


def sibling_send_half(gs, name):
    n = len(gs)

    def body(*refs):
        g_refs, o_refs = refs[:n], refs[n:2 * n]
        send_sems, recv_sems = refs[2 * n:]
        x, y, c, _ = _place()
        cps = []
        for k in range(n):
            hr = gs[k].shape[1] // 2
            src = g_refs[k].at[:, pl.ds(pl.multiple_of((1 - c) * hr, 8), hr)]
            cp = pltpu.make_async_remote_copy(src_ref=src, dst_ref=o_refs[k], send_sem=send_sems.at[k], recv_sem=recv_sems.at[k],
                                              device_id=(x, y, 1 - c), device_id_type=MESH)
            cp.start()
            cps.append(cp)
        for cp in cps:
            cp.wait()

    return pl.pallas_call(
        body, out_shape=[SDS((N_SHARD, g.shape[1] // 2, g.shape[2]), g.dtype) for g in gs], in_specs=[HBM] * n, out_specs=[HBM] * n,
        scratch_shapes=[pltpu.SemaphoreType.DMA((n,)), pltpu.SemaphoreType.DMA((n,))], name=name,
    )(*gs)


def _scatter_copies(h_refs, o_refs, send_sems, recv_sems):
    _, _, c, chips = _place()
    return [pltpu.make_async_remote_copy(
        src_ref=h_refs[k].at[2 * chip[0] + chip[1]], dst_ref=o_refs[k].at[j], send_sem=send_sems.at[k, j],
        recv_sem=recv_sems.at[k, j], device_id=(*chip, c), device_id_type=MESH)
        for k in range(len(h_refs)) for j, chip in enumerate(chips)]


def chip_scatter(hs, name):
    n = len(hs)

    def body(*refs):
        cps = _scatter_copies(refs[:n], refs[n:2 * n], *refs[2 * n:])
        for cp in cps:
            cp.start()
        for cp in cps:
            cp.wait()

    return pl.pallas_call(
        body, out_shape=[SDS((3,) + h.shape[1:], h.dtype) for h in hs], in_specs=[HBM] * n, out_specs=[HBM] * n,
        scratch_shapes=[pltpu.SemaphoreType.DMA((n, 3)), pltpu.SemaphoreType.DMA((n, 3))], name=name,
    )(*hs)


def sibling_swap(ghs, name):
    n = len(ghs)

    def body(*refs):
        g_refs, o_refs = refs[:n], refs[n:2 * n]
        send_sems, recv_sems = refs[2 * n:]
        x, y, c, _ = _place()
        cps = []
        for k in range(n):
            cp = pltpu.make_async_remote_copy(src_ref=g_refs[k], dst_ref=o_refs[k], send_sem=send_sems.at[k],
                                              recv_sem=recv_sems.at[k], device_id=(x, y, 1 - c), device_id_type=MESH)
            cp.start()
            cps.append(cp)
        for cp in cps:
            cp.wait()

    return pl.pallas_call(
        body, out_shape=[SDS(g.shape, g.dtype) for g in ghs], in_specs=[HBM] * n, out_specs=[HBM] * n,
        scratch_shapes=[pltpu.SemaphoreType.DMA((n,)), pltpu.SemaphoreType.DMA((n,))], name=name,
    )(*ghs)


def pair_sum(g, ra, cidx, name):
    _, r, cols = g.shape
    hr = r // 2

    def body(c_ref, g_ref, a_ref, o_ref):
        o_ref[...] = (g_ref[...] + a_ref[...]).astype(BF16)

    return pl.pallas_call(
        body,
        grid_spec=pltpu.PrefetchScalarGridSpec(
            num_scalar_prefetch=1, grid=(N_SHARD,),
            in_specs=[pl.BlockSpec((1, hr, cols), lambda s, c_ref: (s, c_ref[0], 0)),
                      pl.BlockSpec((1, hr, cols), lambda s, c_ref: (s, 0, 0))],
            out_specs=pl.BlockSpec((1, hr, cols), lambda s, c_ref: (s, 0, 0))),
        out_shape=SDS((N_SHARD, hr, cols), BF16), compiler_params=_cp("arbitrary"), name=name,
    )(cidx, g, ra)


def chip_sum(h, rb, sidx, name):
    _, hr, cols = h.shape

    def body(s_ref, h_ref, r_ref, o_ref):
        o_ref[...] = ((h_ref[0].astype(F32) + r_ref[0].astype(F32)) + r_ref[1].astype(F32)) + r_ref[2].astype(F32)

    return pl.pallas_call(
        body,
        grid_spec=pltpu.PrefetchScalarGridSpec(
            num_scalar_prefetch=1, grid=(1,),
            in_specs=[pl.BlockSpec((1, hr, cols), lambda i, s_ref: (s_ref[0], 0, 0)),
                      pl.BlockSpec((3, hr, cols), lambda i, s_ref: (0, 0, 0))],
            out_specs=pl.BlockSpec((hr, cols), lambda i, s_ref: (0, 0))),
        out_shape=SDS((hr, cols), F32), compiler_params=_cp("arbitrary"), name=name,
    )(sidx, h, rb)


def _shard_cols(g, n_valid):
    r = g.shape[0]
    return g[:, :n_valid].reshape(r, N_SHARD, n_valid // N_SHARD).transpose(1, 0, 2)


def _unshard_cols(o, pad_to):
    _, r, n = o.shape
    full = o.transpose(1, 0, 2).reshape(r, N_SHARD * n)
    return jnp.pad(full, ((0, 0), (0, pad_to - N_SHARD * n)))


def _rows_of_tiles(t):
    B, H, S = t.shape
    return t.reshape(B, H, S // FT, 1, FT)


def mixer_fwd(x1, mod3, g_pre, w_main, w_f, b_forget_pad, goa, gob, w_out, g_post, tabs, nb, gather=None):
    hmix, pa, pb, flog = mixer_proj(x1, mod3, g_pre, w_main, w_f, *tabs, name="mixer_proj")
    out_a, lse_a = band_fwd(pa, name="band_fwd")
    F = forget_cumsum(flog.reshape(nb, SEQ, LANE), b_forget_pad, name="forget_cumsum")
    Fh = F[:, :, :NH].transpose(0, 2, 1)
    fblk = Fh.reshape(nb, NH, SEQ // FB, 1, FB)
    frow = _rows_of_tiles(Fh)
    (out_b, lse_b), gathered = fox_fwd(pb, Fh.reshape(nb, NH, SEQ // FOX_QB, 1, FOX_QB), frow, name="fox_fwd", gather=gather)
    x2, merged, y0m = mixer_out_fwd(out_a, out_b, goa, gob, w_out, g_post, x1, mod3, name="mixer_out_fwd")
    res = dict(hmix=hmix, flog=flog, pa=pa, pb=pb, out_a=out_a, lse_a=lse_a, fblk=fblk, frow=frow, out_b=out_b,
               lrow=_rows_of_tiles(lse_b.reshape(nb, NH, SEQ)), merged=merged, y0m=y0m)
    return x2, res, gathered


def mixer_bwd(dx2, x1, mod3, g_pre, w_main, w_f, b_forget_pad, goa, gob, w_out, g_post, tabs, res, nb):
    T = nb * SEQ
    dy0m, doa, dob, dmgate, dg_post, dgoa, dgob, dvec_b = mixer_out_bwd(
        dx2, res["y0m"], mod3, g_post, w_out, res["out_a"], res["out_b"], goa, gob, name="mixer_out_bwd")
    dqa, dka, dva = band_bwd(res["pa"], doa, res["out_a"], res["lse_a"], *tabs, name="band_bwd")
    drow = _rows_of_tiles(dvec_b[:, :NH].reshape(nb, SEQ, NH).transpose(0, 2, 1))
    dqb, dkb, dvb, dfq, dfk = fox_bwd(res["pb"], dob, res["lrow"], drow, res["fblk"], res["frow"], name="fox_bwd")
    dF = (dfq.reshape(nb, NH, SEQ) + dfk.reshape(nb, NH, SEQ)).transpose(0, 2, 1)
    dF = jnp.pad(dF, ((0, 0), (0, 0), (0, LANE - NH)))
    dflog, dbf = forget_cumsum_bwd(dF, res["flog"].reshape(nb, SEQ, LANE), b_forget_pad, name="forget_cumsum_bwd")
    dflog = dflog.reshape(T, LANE)
    dps = (dqa, dka, dva, dqb, dkb, dvb)
    dx1, dmod2, dg_pre = mixer_proj_bwd(dps, dflog, dx2, x1, mod3, g_pre, w_main, w_f, name="mixer_proj_bwd")
    g_main = matmul_tn_cols(res["hmix"], dps, 1024, name="grad_w_in")
    g_f = matmul_tn(res["hmix"], dflog.astype(BF16), D, LANE, 1024, name="grad_w_forget")
    g_out = matmul_tn(res["merged"], dy0m, D, D, 1024, name="grad_w_out")
    dmod3 = jnp.concatenate([dmod2, dmgate], axis=1)
    return dx1, dmod3, dict(g_pre=dg_pre, g_post=dg_post, goa=dgoa, gob=dgob, b_forget=dbf[:, :NH],
                            w_in=jnp.concatenate([g_main, g_f[:, :NH]], axis=1), w_out=g_out)


def ffn_grads(h, dy0, act, dgate, dup, pre, reduce=None):
    g_gate = matmul_tn(h, dgate, D, DFF_PAD, 1024, name=pre + "_grad_gate")
    if reduce is None:
        g_up = matmul_tn(h, dup, D, DFF_PAD, 1024, name=pre + "_grad_up")
        g_down = matmul_tn(act, dy0, FF_TN, D, 1024, name=pre + "_grad_down")
        return (g_gate, g_up, g_down), {}
    hs_gate = reduce("gate", g_gate)
    g_up, rb_gate = matmul_tn(h, dup, D, DFF_PAD, 1024, name=pre + "_grad_up", scatter=hs_gate)
    hs_up = reduce("up", g_up)
    g_down, rb_up = matmul_tn(act, dy0, FF_TN, D, 1024, name=pre + "_grad_down", scatter=hs_up)
    return (g_gate, g_up, g_down), {"gate": (hs_gate[0], rb_gate[0]), "up": (hs_up[0], rb_up[0])}


def local_step(x0, tgt, pos_col, mod, wfull, p, late_weights=None, last_weights=None, early_grads=None, last_reduce=None):
    T = x0.shape[0]
    nb = T // SEQ
    mod_ff1, mod_mix, mod_ff2 = mod[:, 0:3], mod[:, 3:6], mod[:, 6:9]
    tabs = rope_tables(pos_col, name="rope_tables")
    bf_pad = jnp.pad(p["b_forget"], ((0, 0), (0, LANE - NH)))

    (x1, h1, gate1, up1, y01), gathered = ffn_fwd(
        x0, mod_ff1, p["g_pre_ff1"], p["g_post_ff1"], wfull["w_ff1_gate"], wfull["w_ff1_up"], wfull["w_ff1_down"], 0.5,
        name="ff1_fwd", gather=None if late_weights is None else late_weights[:2])
    if late_weights is not None:
        wfull = {**wfull, **late_weights[2](gathered)}
    x2, res, gathered = mixer_fwd(x1, mod_mix, p["g_pre_mix"], wfull["w_main"], wfull["w_f"], bf_pad, p["g_out_a"],
                                  p["g_out_b"], wfull["w_out"], p["g_post_mix"], tabs, nb,
                                  gather=None if last_weights is None else last_weights[:2])
    if last_weights is not None:
        wfull = {**wfull, **last_weights[2](gathered)}
    (x3, h2, gate2, up2, y02), _ = ffn_fwd(x2, mod_ff2, p["g_pre_ff2"], p["g_post_ff2"], wfull["w_ff2_gate"],
                                           wfull["w_ff2_up"], wfull["w_ff2_down"], 0.5, name="ff2_fwd")

    (dx2, dy02, act2, dgate2, dup2, dmod_ff2, dgpre2, dgpost2), (loss_part,) = ffn_bwd(
        x3, x2, y02, mod_ff2, p["g_pre_ff2"], p["g_post_ff2"], gate2, up2, wfull["w_ff2_gate"], wfull["w_ff2_up"],
        wfull["w_ff2_down"], 0.5, name="ff2_bwd", target=tgt)
    gw = {}
    (gw["w_ff2_gate"], gw["w_ff2_up"], gw["w_ff2_down"]), _ = ffn_grads(h2, dy02, act2, dgate2, dup2, "ff2")
    dx1, dmod_mix, gmix = mixer_bwd(dx2, x1, mod_mix, p["g_pre_mix"], wfull["w_main"], wfull["w_f"], bf_pad, p["g_out_a"],
                                    p["g_out_b"], wfull["w_out"], p["g_post_mix"], tabs, res, nb)
    gw["w_in"], gw["w_out"] = gmix["w_in"], gmix["w_out"]
    (dx0, dy01, act1, dgate1, dup1, dmod_ff1, dgpre1, dgpost1), scattered = ffn_bwd(
        dx1, x0, y01, mod_ff1, p["g_pre_ff1"], p["g_post_ff1"], gate1, up1, wfull["w_ff1_gate"], wfull["w_ff1_up"],
        wfull["w_ff1_down"], 0.5, name="ff1_bwd", scatter=None if early_grads is None else early_grads(gw))
    (gw["w_ff1_gate"], gw["w_ff1_up"], gw["w_ff1_down"]), chained = ffn_grads(h1, dy01, act1, dgate1, dup1, "ff1", last_reduce)
    dmod = jnp.concatenate([dmod_ff1, dmod_mix, dmod_ff2], axis=1).reshape(nb, 9 * D)
    small = dict(g_pre_ff1=dgpre1, g_post_ff1=dgpost1, g_pre_mix=gmix["g_pre"], g_post_mix=gmix["g_post"], g_pre_ff2=dgpre2,
                 g_post_ff2=dgpost2, g_out_a=gmix["goa"], g_out_b=gmix["gob"], b_forget=gmix["b_forget"])
    return loss_part, dx0, dmod, gw, small, scattered, chained


def kernel(x, c, positions, w_ada, b_ada, g_pre_ff1, g_post_ff1, w_ff1_gate, w_ff1_up, w_ff1_down, g_pre_mix, g_post_mix, w_in, b_forget, g_out_a, g_out_b, w_out, g_pre_ff2, g_post_ff2, w_ff2_gate, w_ff2_up, w_ff2_down, loss_target, m_w_ada, m_b_ada, m_g_pre_ff1, m_g_post_ff1, m_w_ff1_gate, m_w_ff1_up, m_w_ff1_down, m_g_pre_mix, m_g_post_mix, m_w_in, m_b_forget, m_g_out_a, m_g_out_b, m_w_out, m_g_pre_ff2, m_g_post_ff2, m_w_ff2_gate, m_w_ff2_up, m_w_ff2_down, v_w_ada, v_b_ada, v_g_pre_ff1, v_g_post_ff1, v_w_ff1_gate, v_w_ff1_up, v_w_ff1_down, v_g_pre_mix, v_g_post_mix, v_w_in, v_b_forget, v_g_out_a, v_g_out_b, v_w_out, v_g_pre_ff2, v_g_post_ff2, v_w_ff2_gate, v_w_ff2_up, v_w_ff2_down):
    args = dict(locals())
    nb = x.shape[0]
    T = nb * SEQ
    ax, ay, ac = lax.axis_index("x"), lax.axis_index("y"), lax.axis_index("c")
    shard = 2 * ax + ay
    cidx = jnp.reshape(ac, (1,)).astype(jnp.int32)
    sidx = jnp.reshape(shard, (1,)).astype(jnp.int32)

    big = ["w_ff1_gate", "w_ff1_up", "w_ff1_down", "w_in", "w_out", "w_ff2_gate", "w_ff2_up", "w_ff2_down"]
    vecs = ["g_pre_ff1", "g_post_ff1", "g_pre_mix", "g_post_mix", "g_pre_ff2", "g_post_ff2"]

    first, late = big[:3], big[3:]
    splits = dict(zip(big, [512, 512, 352, 512, 128, 512, 512, 352]))

    def assemble(names, gathered):
        out = {}
        for n, o in zip(names, gathered):
            if n.endswith("gate") or n.endswith("up"):
                out[n] = _unshard_cols(o, DFF_PAD)
            elif n.endswith("down"):
                out[n] = jnp.pad(o.reshape(DFF, D), ((0, DFF_PAD - DFF), (0, 0)))
            elif n == "w_in":
                full = _unshard_cols(o, IN_COLS)
                out["w_main"] = full[:, :IN_MAIN]
                out["w_f"] = jnp.pad(full[:, IN_MAIN:], ((0, 0), (0, LANE - NH)))
            else:
                out[n] = o.reshape(D, D)
        return out

    wfull = assemble(first, all_gather_shards([args[n][0].astype(BF16) for n in first], [splits[n] for n in first],
                                              name="all_gather_weights"))
    def gather_plan(names):
        return ([args[n][0].astype(BF16) for n in names], [splits[n] for n in names], functools.partial(assemble, names))

    late_weights, last_weights = gather_plan(late[:2]), gather_plan(late[2:])

    ncol = w_ada.shape[2]
    c_all = all_gather8(c, name="all_gather_c").reshape(N_DEV * nb, D)
    b_loc = lax.dynamic_slice(b_ada, (0, shard * ncol), (1, ncol))
    mod_loc = ada_fwd(c_all, w_ada[0], b_loc, name="ada_fwd")
    mod_g = all_gather8(mod_loc, name="all_gather_mod")
    row0 = (4 * ax + 2 * ay + ac) * nb
    mod_rows = lax.dynamic_slice(mod_g, (0, row0, 0), (N_DEV, nb, ncol))
    mod = jnp.concatenate([mod_rows[2 * s] for s in range(N_SHARD)], axis=-1).reshape(nb, 9, D)

    small_in = dict(g_pre_ff1=g_pre_ff1, g_post_ff1=g_post_ff1, g_pre_mix=g_pre_mix, g_post_mix=g_post_mix, g_pre_ff2=g_pre_ff2,
                    g_post_ff2=g_post_ff2, g_out_a=g_out_a, g_out_b=g_out_b, b_forget=b_forget)
    def shard_blocked(n, g):
        if n.endswith("gate") or n.endswith("up"):
            return _shard_cols(g, DFF)
        if n.endswith("down"):
            return g[:DFF].reshape(N_SHARD, DFF // N_SHARD, D)
        if n == "w_in":
            return _shard_cols(g, IN_COLS)
        return g.reshape(N_SHARD, D // N_SHARD, D)

    def chip_sums(names, gw, tag):
        gsb = [shard_blocked(n, gw[n]) for n in names]
        ras = sibling_send_half(gsb, name="grad_sibling_send_" + tag)
        return [pair_sum(g, ra, cidx, name=f"grad_pair_sum_{n}") for n, g, ra in zip(names, gsb, ras)]

    hs = {}

    def early_grads(gw):
        hs.update(zip(late, chip_sums(late, gw, "late")))
        return [hs[n] for n in late]

    def last_reduce(which, g):
        return chip_sums(["w_ff1_" + which], {"w_ff1_" + which: g}, which)

    loss_part, dx0, dmod, gw, small, rbs_late, chained = local_step(
        x.reshape(T, D), loss_target.reshape(T, D), positions.reshape(T, 1), mod, wfull, small_in, late_weights, last_weights,
        early_grads, last_reduce)

    dmod_all = all_gather8(dmod, name="all_gather_dmod").reshape(N_DEV * nb, 9 * D)
    dmod_loc = lax.dynamic_slice(dmod_all, (0, shard * ncol), (N_DEV * nb, ncol))
    g_w_ada = ada_bwd(c_all, dmod_loc, name="ada_bwd")

    rbs = dict(zip(late, rbs_late))
    for which, (h, rb) in chained.items():
        hs["w_ff1_" + which], rbs["w_ff1_" + which] = h, rb
    hs["w_ff1_down"] = chip_sums(["w_ff1_down"], gw, "down")[0]
    rbs["w_ff1_down"] = chip_scatter([hs["w_ff1_down"]], name="grad_chip_scatter")[0]
    ghs = [chip_sum(hs[n], rbs[n], sidx, name=f"grad_chip_sum_{n}") for n in big]
    theirs = sibling_swap(ghs, name="grad_sibling_swap")

    row6 = jnp.concatenate([small["g_out_a"], small["g_out_b"]], axis=1)
    row7 = jnp.concatenate([small["b_forget"], loss_part[0:1, 0:1], jnp.zeros((1, D - NH - 1), F32)], axis=1)
    pack = jnp.concatenate([small[n] for n in vecs] + [row6, row7], axis=0)
    packed = all_gather8(pack, name="all_gather_small").reshape(N_DEV, 8 * D)

    def pack_state(pre):
        r6 = jnp.concatenate([args[pre + "g_out_a"], args[pre + "g_out_b"]], axis=1)
        r7 = jnp.pad(args[pre + "b_forget"], ((0, 0), (0, D - NH)))
        return jnp.concatenate([args[pre + n] for n in vecs] + [r6, r7], axis=0).reshape(1, 8 * D)

    sg, sd, sm, sv = (t.reshape(8, D) for t in vec_adam(packed, pack_state(""), pack_state("m_"), pack_state("v_"), name="adam_small"))

    def unpack(t):
        out = {n: t[i:i + 1] for i, n in enumerate(vecs)}
        out["g_out_a"], out["g_out_b"], out["b_forget"] = t[6:7, :WG], t[6:7, WG:], t[7:8, :NH]
        return out

    outs = dict(grad=unpack(sg), delta=unpack(sd), new_m=unpack(sm), new_v=unpack(sv))
    loss = sg[7, NH]
    outs["grad"]["b_ada"], outs["delta"]["b_ada"], outs["new_m"]["b_ada"], outs["new_v"]["b_ada"] = vec_adam(
        dmod_all, b_ada, m_b_ada, v_b_ada, name="adam_b_ada")

    for n, mine, other in zip(big, ghs, theirs):
        tr = 128 if mine.shape[0] % 128 == 0 else mine.shape[0]
        outs["grad"][n], outs["delta"][n], outs["new_m"][n], outs["new_v"][n] = adam_update_halves(
            args[n], mine, other, args["m_" + n], args["v_" + n], cidx, tr, name="adam_" + n)
    outs["delta"]["w_ada"], outs["new_m"]["w_ada"], outs["new_v"]["w_ada"] = adam_update(
        w_ada, g_w_ada, m_w_ada, v_w_ada, 128, name="adam_w_ada")
    outs["grad"]["w_ada"] = g_w_ada[None]

    order = ["w_ada", "b_ada", "g_pre_ff1", "g_post_ff1", "w_ff1_gate", "w_ff1_up", "w_ff1_down", "g_pre_mix", "g_post_mix", "w_in",
             "b_forget", "g_out_a", "g_out_b", "w_out", "g_pre_ff2", "g_post_ff2", "w_ff2_gate", "w_ff2_up", "w_ff2_down"]
    result = [loss, dx0.reshape(nb, SEQ, D)]
    for kind in ("grad", "delta", "new_m", "new_v"):
        result += [outs[kind][n] for n in order]
    return tuple(result)
```

```python
import functools
import math

import jax
import jax.numpy as jnp
from jax import lax
from jax.experimental import pallas as pl
from jax.experimental.pallas import tpu as pltpu

D = 1024
SEQ = 2048
HD = 64
NH = 8
WG = NH * HD
DFF = 2752
DFF_PAD = 2816
IN_MAIN = 6 * WG
IN_COLS = IN_MAIN + NH
N_SHARD = 4
N_DEV = 8
LANE = 128
QB = 128
ROWS = 256
FB = 512
FT = 512
FOX_QB = 512
FOX_PAIRS = 4
FOX_PAIRS_BWD = 2
BAND_UNROLL = 8
BAND_UNROLL_BWD = 4
PATTERNS = ((1, 16), (4, 4), (16, 1))
ROPE_THETA = 500000.0
EPS = 1e-6
NEG = -1e30
ATTN_SCALE = HD ** -0.5
TM = 512
TM_FFN = 512
TM_BWD = 256
VMEM_LIMIT = 56 * 1024 * 1024

ADAM_LR, ADAM_B1, ADAM_B2, ADAM_EPS, ADAM_WD, ADAM_STEP = 0.001, 0.9, 0.999, 1e-08, 0.01, 10

F32 = jnp.float32
BF16 = jnp.bfloat16
MESH = pl.DeviceIdType.MESH
SDS = jax.ShapeDtypeStruct


def _cp(*sem):
    return pltpu.CompilerParams(dimension_semantics=sem, vmem_limit_bytes=VMEM_LIMIT)


def _dot(a, b):
    return jnp.dot(a, b, preferred_element_type=F32)


def _dot_nt(a, b):
    return lax.dot_general(a, b, (((1,), (1,)), ((), ())), preferred_element_type=F32)


def _dot_tn(a, b):
    return lax.dot_general(a, b, (((0,), (0,)), ((), ())), preferred_element_type=F32)


def _rms(xf):
    return lax.rsqrt(jnp.mean(xf * xf, axis=-1, keepdims=True) + EPS)


def _norm_mod_bwd(dh, xf, g, scale):
    r = _rms(xf)
    xh = xf * r
    dsh = jnp.sum(dh, axis=0, keepdims=True)
    dsc = jnp.sum(dh * (xh * g), axis=0, keepdims=True)
    dn = dh * (1.0 + scale)
    dg = jnp.sum(dn * xh, axis=0, keepdims=True)
    dxh = dn * g
    dx = r * (dxh - xh * jnp.mean(dxh * xh, axis=-1, keepdims=True))
    return dx, dsh, dsc, dg


def _post_bwd(dxo, y0, g, mgate, gs):
    r = _rms(y0)
    yh = y0 * r
    dmg = gs * jnp.sum(dxo * (yh * g), axis=0, keepdims=True)
    dy = (gs * mgate) * dxo
    dg = jnp.sum(dy * yh, axis=0, keepdims=True)
    dyh = dy * g
    dy0 = r * (dyh - yh * jnp.mean(dyh * yh, axis=-1, keepdims=True))
    return dy0, dmg, dg


def _mod_map(i, *_):
    return ((i * TM) // SEQ, 0, 0)


FF_TN = 1408
FF_TILES = ((0, 768), (768, 1536), (1536, 2304), (2304, 2816))


def _resident_scratch():
    return [pltpu.VMEM((D, DFF_PAD), BF16), pltpu.VMEM((D, DFF_PAD), BF16), pltpu.VMEM((DFF_PAD, D), BF16),
            pltpu.SemaphoreType.DMA((3,))]


def _load_resident(first_step, srcs, dsts, sems):
    @pl.when(first_step)
    def _():
        cps = [pltpu.make_async_copy(s, d, sems.at[k]) for k, (s, d) in enumerate(zip(srcs, dsts))]
        for cp in cps:
            cp.start()
        for cp in cps:
            cp.wait()


def ffn_fwd(x, mod3, g_pre, g_post, wg, wu, wd, gs, name, gather=None):
    T = x.shape[0]
    tm = TM_FFN
    ng = 0 if gather is None else len(gather[0])
    plan = None if gather is None else ShardGather([w.shape for w in gather[0]], gather[1])

    def body(*refs):
        x_ref, mod_ref, gpre_ref, gpost_ref = refs[:4]
        xo_ref, h_ref, gate_ref, up_ref, y0_ref = refs[7 + ng:12 + ng]
        wg_ref, wu_ref, wd_ref, wsem = refs[12 + 2 * ng:16 + 2 * ng]
        i = pl.program_id(0)
        if plan is not None:
            comm = (refs[7:7 + ng], refs[12 + ng:12 + 2 * ng], refs[16 + 2 * ng:])
            pl.when(i == 0)(lambda: plan.start(*comm))
        _load_resident(i == 0, refs[4:7], (wg_ref, wu_ref, wd_ref), wsem)

        xf = x_ref[...]
        hb = ((xf * _rms(xf) * gpre_ref[...]) * (1.0 + mod_ref[0, 1:2, :]) + mod_ref[0, 0:1, :]).astype(BF16)
        h_ref[...] = hb
        y0 = None
        for lo, hi in FF_TILES:
            gate = _dot(hb, wg_ref[:, lo:hi])
            up = _dot(hb, wu_ref[:, lo:hi])
            gate_ref[:, lo:hi] = gate.astype(BF16)
            up_ref[:, lo:hi] = up.astype(BF16)
            part = _dot((gate * jax.nn.sigmoid(gate) * up).astype(BF16), wd_ref[lo:hi, :])
            y0 = part if y0 is None else y0 + part
        y0_ref[...] = y0
        xo_ref[...] = xf + (gs * mod_ref[0, 2:3, :]) * (y0 * _rms(y0) * gpost_ref[...])

        if plan is not None:
            pl.when(i == T // tm - 1)(lambda: plan.finish(*comm))

    tok = pl.BlockSpec((tm, D), lambda i: (i, 0))
    vec = pl.BlockSpec((1, D), lambda i: (0, 0))
    hid = pl.BlockSpec((tm, DFF_PAD), lambda i: (i, 0))
    outs = pl.pallas_call(
        body, grid=(T // tm,),
        in_specs=[tok, pl.BlockSpec((1, 3, D), lambda i: ((i * tm) // SEQ, 0, 0)), vec, vec, HBM, HBM, HBM] + [HBM] * ng,
        out_specs=[tok, tok, hid, hid, tok] + [HBM] * ng,
        out_shape=[SDS((T, D), F32), SDS((T, D), BF16), SDS((T, DFF_PAD), BF16), SDS((T, DFF_PAD), BF16), SDS((T, D), F32)]
        + ([] if plan is None else plan.out_shapes(BF16)),
        scratch_shapes=_resident_scratch() + ([] if plan is None else plan.scratch()),
        compiler_params=_cp("arbitrary"), name=name,
    )(x, mod3, g_pre, g_post, wg, wu, wd, *([] if gather is None else gather[0]))
    return outs[:5], outs[5:]


def ffn_bwd(dxo, x, y0, mod3, g_pre, g_post, gate, up, wg, wu, wd, gs, name, scatter=None, target=None):
    assert scatter is None or target is None
    T = x.shape[0]
    nb = T // SEQ
    tm = TM_BWD
    tiles_per_seq = SEQ // tm
    ns = 0 if scatter is None else len(scatter)
    ne = ns + (target is not None)

    def body(*refs):
        dxo_ref, x_ref, y0_ref, mod_ref, gpre_ref, gpost_ref, gate_ref, up_ref = refs[:8]
        dx_ref, dy0_ref, act_ref, dgate_ref, dup_ref, dmod_ref, dgpre_ref, dgpost_ref = refs[11 + ne:19 + ne]
        wg_ref, wu_ref, wd_ref, wsem = refs[19 + 2 * ne:23 + 2 * ne]
        i = pl.program_id(0)
        _load_resident(i == 0, refs[8:11], (wg_ref, wu_ref, wd_ref), wsem)
        if ns:
            comm = (refs[11:11 + ns], refs[19 + ns:19 + 2 * ns], *refs[23 + 2 * ns:])

            @pl.when(i == 0)
            def _():
                for cp in _scatter_copies(*comm):
                    cp.start()

        @pl.when(i == 0)
        def _():
            dgpre_ref[...] = jnp.zeros_like(dgpre_ref)
            dgpost_ref[...] = jnp.zeros_like(dgpost_ref)

        @pl.when(i % tiles_per_seq == 0)
        def _():
            dmod_ref[...] = jnp.zeros_like(dmod_ref)

        dxo = dxo_ref[...]
        if target is not None:
            loss_ref = refs[19 + ne]

            @pl.when(i == 0)
            def _():
                loss_ref[...] = jnp.zeros_like(loss_ref)

            err = dxo - refs[11][...]
            loss_ref[...] += jnp.sum(err * err) * (0.5 / D)
            dxo = err * (1.0 / D)
        dy0, dmg, dg = _post_bwd(dxo, y0_ref[...], gpost_ref[...], mod_ref[0, 2:3, :], gs)
        dmod_ref[0, 2:3, :] += dmg
        dgpost_ref[...] += dg
        db = dy0.astype(BF16)
        dy0_ref[...] = db
        dh = None
        for lo, hi in FF_TILES:
            dact = _dot_nt(db, wd_ref[lo:hi, :])
            g = gate_ref[:, lo:hi].astype(F32)
            u = up_ref[:, lo:hi].astype(F32)
            sig = jax.nn.sigmoid(g)
            sl = g * sig
            dgate = (dact * u * (sig * (1.0 + g * (1.0 - sig)))).astype(BF16)
            dup = (dact * sl).astype(BF16)
            act_ref[:, lo:hi] = (sl * u).astype(BF16)
            dgate_ref[:, lo:hi] = dgate
            dup_ref[:, lo:hi] = dup
            part = _dot_nt(dgate, wg_ref[:, lo:hi]) + _dot_nt(dup, wu_ref[:, lo:hi])
            dh = part if dh is None else dh + part
        dx, dsh, dsc, dg = _norm_mod_bwd(dh, x_ref[...], gpre_ref[...], mod_ref[0, 1:2, :])
        dx_ref[...] = dxo + dx
        dmod_ref[0, 0:1, :] += dsh
        dmod_ref[0, 1:2, :] += dsc
        dgpre_ref[...] += dg

        if ns:
            @pl.when(i == T // tm - 1)
            def _():
                for cp in _scatter_copies(*comm):
                    cp.wait()

    tok = pl.BlockSpec((tm, D), lambda i: (i, 0))
    vec = pl.BlockSpec((1, D), lambda i: (0, 0))
    hid = pl.BlockSpec((tm, DFF_PAD), lambda i: (i, 0))
    modspec = pl.BlockSpec((1, 3, D), lambda i: ((i * tm) // SEQ, 0, 0))
    outs = pl.pallas_call(
        body, grid=(T // tm,),
        in_specs=[tok, tok, tok, modspec, vec, vec, hid, hid, HBM, HBM, HBM] + [HBM] * ns + [tok] * (ne - ns),
        out_specs=[tok, tok, hid, hid, hid, modspec, vec, vec] + [HBM] * ns
        + [pl.BlockSpec((8, LANE), lambda i: (0, 0))] * (ne - ns),
        out_shape=[SDS((T, D), F32), SDS((T, D), BF16), SDS((T, DFF_PAD), BF16), SDS((T, DFF_PAD), BF16),
                   SDS((T, DFF_PAD), BF16), SDS((nb, 3, D), F32), SDS((1, D), F32), SDS((1, D), F32)]
        + [SDS((3,) + h.shape[1:], h.dtype) for h in (scatter or [])] + [SDS((8, LANE), F32)] * (ne - ns),
        scratch_shapes=_resident_scratch()
        + ([pltpu.SemaphoreType.DMA((ns, 3)), pltpu.SemaphoreType.DMA((ns, 3))] if ns else []),
        compiler_params=_cp("arbitrary"), name=name,
    )(dxo, x, y0, mod3, g_pre, g_post, gate, up, wg, wu, wd, *(scatter or []), *([] if target is None else [target]))
    return outs[:8], outs[8:]


def matmul_tn(a, b, tm, tn, tk, name, scatter=None):
    T, M = a.shape
    N = b.shape[1]
    grid = (M // tm, N // tn, T // tk)
    ns = 0 if scatter is None else len(scatter)

    def body(*refs):
        a_ref, b_ref = refs[:2]
        o_ref = refs[2 + ns]
        ids = [pl.program_id(ax) for ax in range(3)]
        if ns:
            comm = (refs[2:2 + ns], refs[3 + ns:3 + 2 * ns], *refs[3 + 2 * ns:])

            @pl.when((ids[0] == 0) & (ids[1] == 0) & (ids[2] == 0))
            def _():
                for cp in _scatter_copies(*comm):
                    cp.start()

        @pl.when(ids[2] == 0)
        def _():
            o_ref[...] = jnp.zeros_like(o_ref)

        o_ref[...] += _dot_tn(a_ref[...], b_ref[...])

        if ns:
            @pl.when((ids[0] == grid[0] - 1) & (ids[1] == grid[1] - 1) & (ids[2] == grid[2] - 1))
            def _():
                for cp in _scatter_copies(*comm):
                    cp.wait()

    outs = pl.pallas_call(
        body, grid=grid,
        in_specs=[pl.BlockSpec((tk, tm), lambda i, j, k: (k, i)), pl.BlockSpec((tk, tn), lambda i, j, k: (k, j))] + [HBM] * ns,
        out_specs=[pl.BlockSpec((tm, tn), lambda i, j, k: (i, j))] + [HBM] * ns,
        out_shape=[SDS((M, N), F32)] + [SDS((3,) + h.shape[1:], h.dtype) for h in (scatter or [])],
        scratch_shapes=[pltpu.SemaphoreType.DMA((ns, 3)), pltpu.SemaphoreType.DMA((ns, 3))] if ns else [],
        compiler_params=_cp("arbitrary", "arbitrary", "arbitrary"), name=name,
    )(a, b, *(scatter or []))
    return outs[0] if scatter is None else (outs[0], outs[1:])


def matmul_tn_cols(a, bs, tk, name):
    T, M = a.shape
    n = bs[0].shape[1]
    ng = len(bs)

    def body(*refs):
        a_ref, b_refs, o_ref = refs[0], refs[1:1 + ng], refs[1 + ng]

        @pl.when(pl.program_id(0) == 0)
        def _():
            o_ref[...] = jnp.zeros_like(o_ref)

        av = a_ref[...]
        for g, b_ref in enumerate(b_refs):
            o_ref[:, g * n:(g + 1) * n] += _dot_tn(av, b_ref[...])

    return pl.pallas_call(
        body, grid=(T // tk,),
        in_specs=[pl.BlockSpec((tk, M), lambda k: (k, 0))] + [pl.BlockSpec((tk, n), lambda k: (k, 0))] * ng,
        out_specs=pl.BlockSpec((M, ng * n), lambda k: (0, 0)), out_shape=SDS((M, ng * n), F32),
        compiler_params=_cp("arbitrary"), name=name,
    )(a, *bs)


def rope_tables(pos_col, name):
    T = pos_col.shape[0]
    tm = 1024

    def body(p_ref, c_ref, s1_ref, s2_ref):
        lane = lax.broadcasted_iota(jnp.int32, (1, LANE), 1)
        l64 = lane % HD
        inv_freq = jnp.exp((l64 % 8).astype(F32) * (-math.log(ROPE_THETA) / 8.0))
        ang = p_ref[...].astype(F32) * inv_freq
        cs = jnp.cos(ang)
        sn = jnp.sin(ang)
        c_ref[...] = jnp.where(l64 < 16, cs, 1.0)
        s1_ref[...] = jnp.where(l64 < 8, -sn, 0.0)
        s2_ref[...] = jnp.where((l64 >= 8) & (l64 < 16), sn, 0.0)

    tab = pl.BlockSpec((tm, LANE), lambda i: (i, 0))
    return pl.pallas_call(
        body, grid=(T // tm,), in_specs=[pl.BlockSpec((tm, 1), lambda i: (i, 0))], out_specs=[tab, tab, tab],
        out_shape=[SDS((T, LANE), F32)] * 3, compiler_params=_cp("arbitrary"), name=name,
    )(pos_col)


def mixer_proj(x, mod3, g_pre, w_main, w_f, rc, rs1, rs2, name):
    T = x.shape[0]

    def body(x_ref, mod_ref, g_ref, w_ref, wf_ref, c_ref, s1_ref, s2_ref, h_ref, pa_ref, pb_ref, f_ref):
        xf = x_ref[...]
        h = (xf * _rms(xf) * g_ref[...]) * (1.0 + mod_ref[0, 1:2, :]) + mod_ref[0, 0:1, :]
        hb = h.astype(BF16)
        h_ref[...] = hb
        f_ref[...] = _dot(hb, wf_ref[...])
        c, s1, s2 = c_ref[...], s1_ref[...], s2_ref[...]
        for grp in range(2):
            pr = _dot(hb, w_ref[:, grp * WG:(grp + 1) * WG])
            for k in range(WG // LANE):
                t = pr[:, k * LANE:(k + 1) * LANE]
                pa_ref[:, grp * WG + k * LANE:grp * WG + (k + 1) * LANE] = (
                    t * c + pltpu.roll(t, LANE - 8, 1) * s1 + pltpu.roll(t, 8, 1) * s2)
        pa_ref[:, 2 * WG:3 * WG] = _dot(hb, w_ref[:, 2 * WG:3 * WG])
        for grp in range(3):
            pb_ref[:, grp * WG:(grp + 1) * WG] = _dot(hb, w_ref[:, (3 + grp) * WG:(4 + grp) * WG]).astype(BF16)

    tok = pl.BlockSpec((TM, D), lambda i: (i, 0))
    vec = pl.BlockSpec((1, D), lambda i: (0, 0))
    tab = pl.BlockSpec((TM, LANE), lambda i: (i, 0))
    grp3 = pl.BlockSpec((TM, 3 * WG), lambda i: (i, 0))
    return pl.pallas_call(
        body, grid=(T // TM,),
        in_specs=[tok, pl.BlockSpec((1, 3, D), _mod_map), vec, pl.BlockSpec((D, IN_MAIN), lambda i: (0, 0)),
                  pl.BlockSpec((D, LANE), lambda i: (0, 0)), tab, tab, tab],
        out_specs=[tok, grp3, grp3, tab],
        out_shape=[SDS((T, D), BF16), SDS((T, 3 * WG), F32), SDS((T, 3 * WG), BF16), SDS((T, LANE), F32)],
        compiler_params=_cp("arbitrary"), name=name,
    )(x, mod3, g_pre, w_main, w_f, rc, rs1, rs2)


def _head_lanes():
    return lax.broadcasted_iota(jnp.int32, (1, LANE), 1) < HD


def _pair(m0, a, b):
    return jnp.where(m0, a, b)


def _band_rows(i, d, nbc):
    if nbc == 1:
        return i, i, 0
    r, mb = i // nbc, i % nbc
    return r + mb * (QB * d), r + jnp.maximum(mb - 1, 0) * (QB * d), jnp.where(mb > 0, QB, 0)


def _rows(start, size, d):
    return pl.ds(pl.multiple_of(start, QB), size) if d == 1 else pl.ds(start, size, stride=d)


def _band_valid(span, off):
    rq = lax.broadcasted_iota(jnp.int32, (QB, span), 0)
    rel = lax.broadcasted_iota(jnp.int32, (QB, span), 1) - off
    return (rel <= rq) & (rel >= rq - QB)


def band_fwd(pa, name):
    T = pa.shape[0]
    B = T // SEQ
    NP = WG // LANE

    def body(q_ref, k_ref, v_ref, out_ref, lse_ref, o_s, l_s):
        m0 = _head_lanes()
        for pidx, (d, nbc) in enumerate(PATTERNS):
            span = QB if nbc == 1 else 2 * QB

            def blk(it, carry, pidx=pidx, d=d, nbc=nbc, span=span):
                ld = []
                for u in range(BAND_UNROLL):
                    qs, ks, off = _band_rows(it * BAND_UNROLL + u, d, nbc)
                    q = q_ref[_rows(qs, QB, d), :] * ATTN_SCALE
                    ld.append((qs, q, k_ref[_rows(ks, span, d), :].astype(BF16), v_ref[_rows(ks, span, d), :].astype(BF16),
                               _band_valid(span, off)))
                ss = [[jnp.where(valid, _dot_nt(jnp.where(mh, q, 0.0).astype(BF16), k), NEG) for mh in (m0, jnp.logical_not(m0))]
                      for _, q, k, _, valid in ld]
                ps = []
                for pair in ss:
                    row = []
                    for s in pair:
                        m = jnp.max(s, axis=-1, keepdims=True)
                        p = jnp.exp(s - m)
                        row.append((p.astype(BF16), jnp.sum(p, axis=-1, keepdims=True), m))
                    ps.append(row)
                pv = [[_dot(p, ld[u][3]) for p, _, _ in ps[u]] for u in range(BAND_UNROLL)]
                for u in range(BAND_UNROLL):
                    rows = _rows(ld[u][0], QB, d)
                    (_, l0, mx0), (_, l1, mx1) = ps[u]
                    o_s[pidx, rows, :] = _pair(m0, pv[u][0] / l0, pv[u][1] / l1)
                    l_s[pidx, rows, :] = _pair(m0, mx0 + jnp.log(l0), mx1 + jnp.log(l1))
                return carry

            lax.fori_loop(0, SEQ // QB // BAND_UNROLL, blk, 0)
        for c in range(SEQ // ROWS):
            sl = slice(c * ROWS, (c + 1) * ROWS)
            a, b, e = l_s[0, sl, :], l_s[1, sl, :], l_s[2, sl, :]
            m = jnp.maximum(jnp.maximum(a, b), e)
            L = m + jnp.log(jnp.exp(a - m) + jnp.exp(b - m) + jnp.exp(e - m))
            out_ref[sl, :] = jnp.exp(a - L) * o_s[0, sl, :] + jnp.exp(b - L) * o_s[1, sl, :] + jnp.exp(e - L) * o_s[2, sl, :]
            lse_ref[sl, :] = L

    blk_of = lambda g: pl.BlockSpec((SEQ, LANE), lambda b, hp, g=g: (b, g * NP + hp))
    return pl.pallas_call(
        body, grid=(B, NP), in_specs=[blk_of(0), blk_of(1), blk_of(2)], out_specs=[blk_of(0), blk_of(0)],
        out_shape=[SDS((T, WG), F32), SDS((T, WG), F32)],
        scratch_shapes=[pltpu.VMEM((3, SEQ, LANE), F32), pltpu.VMEM((3, SEQ, LANE), F32)],
        compiler_params=_cp("arbitrary", "arbitrary"), name=name,
    )(pa, pa, pa)


def _pair_rowsum(m0, prod):
    s0 = jnp.sum(jnp.where(m0, prod, 0.0), axis=-1, keepdims=True)
    return _pair(m0, s0, jnp.sum(prod, axis=-1, keepdims=True) - s0)


def band_bwd(pa, do, out, lse, rc, rs1, rs2, name):
    T = pa.shape[0]
    B = T // SEQ
    NP = WG // LANE

    def body(q_ref, k_ref, v_ref, do_ref, out_ref, l_ref, c_ref, s1_ref, s2_ref, dqo_ref, dko_ref, dvo_ref, d_s, dq_ref, dk_ref,
             dv_ref):
        m0 = _head_lanes()
        dq_ref[...] = jnp.zeros_like(dq_ref)
        dk_ref[...] = jnp.zeros_like(dk_ref)
        dv_ref[...] = jnp.zeros_like(dv_ref)
        for c in range(SEQ // ROWS):
            sl = slice(c * ROWS, (c + 1) * ROWS)
            d_s[sl, :] = _pair_rowsum(m0, do_ref[sl, :] * out_ref[sl, :])
        for d, nbc in PATTERNS:
            span = QB if nbc == 1 else 2 * QB

            def blk(it, carry, d=d, nbc=nbc, span=span):
                masks = (m0, jnp.logical_not(m0))
                ld = []
                for u in range(BAND_UNROLL_BWD):
                    qs, ks, off = _band_rows(it * BAND_UNROLL_BWD + u, d, nbc)
                    qrow, krow = _rows(qs, QB, d), _rows(ks, span, d)
                    ld.append(dict(qrow=qrow, krow=krow, q=q_ref[qrow, :] * ATTN_SCALE, k=k_ref[krow, :].astype(BF16),
                                   v=v_ref[krow, :].astype(BF16), do=do_ref[qrow, :], l=l_ref[qrow, :], dv=d_s[qrow, :],
                                   valid=_band_valid(span, off)))
                for t in ld:
                    t["qm"] = [jnp.where(mh, t["q"], 0.0).astype(BF16) for mh in masks]
                    t["dom"] = [jnp.where(mh, t["do"], 0.0).astype(BF16) for mh in masks]
                sd = [[(jnp.where(t["valid"], _dot_nt(t["qm"][h], t["k"]), NEG), _dot_nt(t["dom"][h], t["v"])) for h in range(2)]
                      for t in ld]
                pd = []
                for t, pair in zip(ld, sd):
                    row = []
                    for h, (s, dp) in enumerate(pair):
                        col = slice(h * HD, h * HD + 1)
                        p = jnp.exp(s - t["l"][:, col])
                        row.append((p.astype(BF16), (p * (dp - t["dv"][:, col])).astype(BF16)))
                    pd.append(row)
                gr = [(_dot(row[0][1], t["k"]), _dot(row[1][1], t["k"]),
                       _dot_tn(jnp.concatenate([row[0][1], row[1][1]], axis=0), jnp.concatenate(t["qm"], axis=0)),
                       _dot_tn(jnp.concatenate([row[0][0], row[1][0]], axis=0), jnp.concatenate(t["dom"], axis=0)))
                      for t, row in zip(ld, pd)]
                for t, (dq0, dq1, dk, dv) in zip(ld, gr):
                    dq_ref[t["qrow"], :] += _pair(m0, dq0, dq1) * ATTN_SCALE
                    dk_ref[t["krow"], :] += dk
                    dv_ref[t["krow"], :] += dv
                return carry

            lax.fori_loop(0, SEQ // QB // BAND_UNROLL_BWD, blk, 0)
        for c in range(SEQ // ROWS):
            sl = slice(c * ROWS, (c + 1) * ROWS)
            cc, s1, s2 = c_ref[sl, :], s1_ref[sl, :], s2_ref[sl, :]
            for acc, o_ref in ((dq_ref, dqo_ref), (dk_ref, dko_ref)):
                d = acc[sl, :]
                o_ref[sl, :] = (d * cc + pltpu.roll(d * s1, 8, 1) + pltpu.roll(d * s2, LANE - 8, 1)).astype(BF16)
            dvo_ref[sl, :] = dv_ref[sl, :].astype(BF16)

    blk_of = lambda g: pl.BlockSpec((SEQ, LANE), lambda b, hp, g=g: (b, g * NP + hp))
    tab = pl.BlockSpec((SEQ, LANE), lambda b, hp: (b, 0))
    return pl.pallas_call(
        body, grid=(B, NP), in_specs=[blk_of(0), blk_of(1), blk_of(2), blk_of(0), blk_of(0), blk_of(0), tab, tab, tab],
        out_specs=[blk_of(0)] * 3, out_shape=[SDS((T, WG), BF16)] * 3,
        scratch_shapes=[pltpu.VMEM((SEQ, LANE), F32)] * 4,
        compiler_params=_cp("arbitrary", "arbitrary"), name=name,
    )(pa, pa, pa, do, out, lse, rc, rs1, rs2)


def _tile_causal(nq, nk, q0, k0):
    r = lax.broadcasted_iota(jnp.int32, (nq, nk), 0)
    c = lax.broadcasted_iota(jnp.int32, (nq, nk), 1)
    return r + (q0 - k0) >= c


def _row_to_col(row):
    n = row.shape[1]
    return jnp.transpose(jnp.broadcast_to(row, (LANE, n)))[:, 0:1]


def _col_to_row(col):
    n = col.shape[0]
    return jnp.transpose(jnp.broadcast_to(col, (n, LANE)))[0:1, :]


def fox_fwd(pb, fblk, frow, name, gather=None):
    FQ = FOX_QB
    T = pb.shape[0]
    B = T // SEQ
    NG = WG // (LANE * FOX_PAIRS)
    NHS = 2 * FOX_PAIRS
    W = LANE * FOX_PAIRS
    n = SEQ // FQ
    ng = 0 if gather is None else len(gather[0])
    plan = None if gather is None else ShardGather([w.shape for w in gather[0]], gather[1])

    def body(*refs):
        q_ref, k_ref, v_ref, fc_ref, fr_ref = refs[:5]
        o_ref, lse_ref = refs[5 + ng:7 + ng]
        if plan is not None:
            comm = (refs[5:5 + ng], refs[7 + ng:7 + 2 * ng], refs[7 + 2 * ng:])
            ids = [pl.program_id(ax) for ax in range(3)]
            pl.when((ids[0] == 0) & (ids[1] == 0) & (ids[2] == 0))(lambda: plan.start(*comm))
        i = pl.program_id(2)
        m0 = _head_lanes()
        masks = (m0, jnp.logical_not(m0))
        heads = [(hh, slice((hh // 2) * LANE, (hh // 2 + 1) * LANE), masks[hh % 2]) for hh in range(NHS)]
        qh, fq = [], []
        for hh, lanes, mh in heads:
            q = q_ref[:, lanes] * ATTN_SCALE
            qh.append(jnp.where(mh, q, jnp.zeros_like(q)))
            fq.append(_row_to_col(fc_ref[0, hh, 0]))

        def step(t, carry, masked):
            rows = pl.ds(pl.multiple_of(t * FT, FT), FT)
            ss = [_dot_nt(qh[hh], k_ref[rows, lanes]) + fq[hh] - fr_ref[0, hh, t] for hh, lanes, _ in heads]
            if masked:
                ok = _tile_causal(FQ, FT, i * FQ, t * FT)
                ss = [jnp.where(ok, s, NEG) for s in ss]
            st = []
            for hh, _, _ in heads:
                m2 = jnp.maximum(carry[hh][0], jnp.max(ss[hh], axis=-1, keepdims=True))
                st.append((m2, jnp.exp(carry[hh][0] - m2), jnp.exp(ss[hh] - m2).astype(BF16)))
            pv = []
            for hh, lanes, mh in heads:
                vt = v_ref[rows, lanes]
                pv.append(_dot(st[hh][2], jnp.where(mh, vt, jnp.ones_like(vt))))
            return tuple((st[hh][0], st[hh][1] * carry[hh][1] + pv[hh]) for hh in range(NHS))

        one = (jnp.full((FQ, 1), NEG, F32), jnp.zeros((FQ, LANE), F32))
        last = (i * FQ) // FT
        carry = lax.fori_loop(0, last, lambda t, cr: step(t, cr, False), (one,) * NHS)
        carry = step(last, carry, True)
        for pr in range(FOX_PAIRS):
            (ma, acca), (mb, accb) = carry[2 * pr], carry[2 * pr + 1]
            la, lb = acca[:, HD:HD + 1], accb[:, 0:1]
            o_ref[:, pr * LANE:(pr + 1) * LANE] = _pair(m0, acca / la, accb / lb)
            lse_ref[0, 2 * pr, 0] = _col_to_row(ma + jnp.log(la))
            lse_ref[0, 2 * pr + 1, 0] = _col_to_row(mb + jnp.log(lb))
        if plan is not None:
            pl.when((ids[0] == B - 1) & (ids[1] == NG - 1) & (ids[2] == n - 1))(lambda: plan.finish(*comm))

    qblk = pl.BlockSpec((FQ, W), lambda b, g, i: (b * n + i, g))
    full = lambda grp: pl.BlockSpec((SEQ, W), lambda b, g, i, grp=grp: (b, grp * NG + g))
    rowb = pl.BlockSpec((1, NHS, 1, 1, FQ), lambda b, g, i: (b, g, i, 0, 0))
    outs = pl.pallas_call(
        body, grid=(B, NG, n),
        in_specs=[qblk, full(1), full(2), rowb, pl.BlockSpec((1, NHS, SEQ // FT, 1, FT), lambda b, g, i: (b, g, 0, 0, 0))]
        + [HBM] * ng,
        out_specs=[qblk, rowb] + [HBM] * ng,
        out_shape=[SDS((T, WG), F32), SDS((B, NH, n, 1, FQ), F32)] + ([] if plan is None else plan.out_shapes(BF16)),
        scratch_shapes=[] if plan is None else plan.scratch(),
        compiler_params=_cp("arbitrary", "arbitrary", "arbitrary"), name=name,
    )(pb, pb, pb, fblk, frow, *([] if gather is None else gather[0]))
    return outs[:2], outs[2:]


def fox_bwd(pb, do, lrow, drow, fblk, frow, name):
    T = pb.shape[0]
    B = T // SEQ
    PAIRS = FOX_PAIRS_BWD
    NG = WG // (LANE * PAIRS)
    NHS = 2 * PAIRS
    W = LANE * PAIRS
    n = SEQ // FB

    def body(q_ref, k_ref, v_ref, do_ref, l_ref, d_ref, fc_ref, fr_ref, dqo_ref, dk_ref, dv_ref, dfq_ref, dfk_ref, dq_ref):
        j = pl.program_id(2)
        m0 = _head_lanes()
        masks = (m0, jnp.logical_not(m0))
        heads = [(hh, slice((hh // 2) * LANE, (hh // 2 + 1) * LANE), masks[hh % 2]) for hh in range(NHS)]

        @pl.when(j == 0)
        def _():
            dq_ref[...] = jnp.zeros_like(dq_ref)
            dfq_ref[...] = jnp.zeros_like(dfq_ref)

        kj = [k_ref[:, lanes] for _, lanes, _ in heads]
        vj = [v_ref[:, lanes] for _, lanes, _ in heads]
        fk = [_row_to_col(fc_ref[0, hh, 0]) for hh in range(NHS)]

        def step(t, carry, masked):
            rows = pl.ds(pl.multiple_of(t * FT, FT), FT)
            qm, dom = [], []
            for _, lanes, mh in heads:
                qt = q_ref[rows, lanes] * ATTN_SCALE
                qm.append(jnp.where(mh, qt, jnp.zeros_like(qt)))
                dom.append(jnp.where(mh, do_ref[rows, lanes], 0.0).astype(BF16))
            ss = [_dot_nt(kj[hh], qm[hh]) + fr_ref[0, hh, t] - fk[hh] for hh in range(NHS)]
            dps = [_dot_nt(vj[hh], dom[hh]) for hh in range(NHS)]
            if masked:
                key = lax.broadcasted_iota(jnp.int32, (FB, FT), 0)
                qry = lax.broadcasted_iota(jnp.int32, (FB, FT), 1)
                ok = qry + (t * FT - j * FB) >= key
                ss = [jnp.where(ok, s, NEG) for s in ss]
            pds = []
            for hh in range(NHS):
                p = jnp.exp(ss[hh] - l_ref[0, hh, t])
                ds = p * (dps[hh] - d_ref[0, hh, t])
                dfq_ref[0, hh, t] += jnp.sum(ds, axis=0, keepdims=True)
                pds.append((p.astype(BF16), ds.astype(BF16), jnp.sum(ds, axis=-1, keepdims=True)))
            dks = [_dot(pds[hh][1], qm[hh]) for hh in range(NHS)]
            dvs = [_dot(pds[hh][0], dom[hh]) for hh in range(NHS)]
            dqs = [_dot_tn(pds[hh][1], kj[hh]) for hh in range(NHS)]
            for pr in range(PAIRS):
                dq_ref[rows, pr * LANE:(pr + 1) * LANE] += _pair(m0, dqs[2 * pr], dqs[2 * pr + 1]) * ATTN_SCALE
            return tuple((carry[hh][0] + dks[hh], carry[hh][1] + dvs[hh], carry[hh][2] - pds[hh][2]) for hh in range(NHS))

        one = (jnp.zeros((FB, LANE), F32), jnp.zeros((FB, LANE), F32), jnp.zeros((FB, 1), F32))
        first = (j * FB) // FT
        carry = step(first, (one,) * NHS, True)
        carry = lax.fori_loop(first + 1, SEQ // FT, lambda t, cr: step(t, cr, False), carry)
        for pr in range(PAIRS):
            (dka, dva, dfka), (dkb, dvb, dfkb) = carry[2 * pr], carry[2 * pr + 1]
            dk_ref[:, pr * LANE:(pr + 1) * LANE] = _pair(m0, dka, dkb).astype(BF16)
            dv_ref[:, pr * LANE:(pr + 1) * LANE] = _pair(m0, dva, dvb).astype(BF16)
            dfk_ref[0, 2 * pr, 0] = _col_to_row(dfka)
            dfk_ref[0, 2 * pr + 1, 0] = _col_to_row(dfkb)

        @pl.when(j == n - 1)
        def _():
            dqo_ref[...] = dq_ref[...].astype(BF16)

    kblk = lambda grp: pl.BlockSpec((FB, W), lambda b, g, j, grp=grp: (b * n + j, grp * NG + g))
    full = pl.BlockSpec((SEQ, W), lambda b, g, j: (b, g))
    rowf = pl.BlockSpec((1, NHS, SEQ // FT, 1, FT), lambda b, g, j: (b, g, 0, 0, 0))
    rowb = pl.BlockSpec((1, NHS, 1, 1, FB), lambda b, g, j: (b, g, j, 0, 0))
    return pl.pallas_call(
        body, grid=(B, NG, n), in_specs=[full, kblk(1), kblk(2), full, rowf, rowf, rowb, rowf],
        out_specs=[full, kblk(0), kblk(0), rowf, rowb],
        out_shape=[SDS((T, WG), BF16), SDS((T, WG), BF16), SDS((T, WG), BF16), SDS((B, NH, SEQ // FT, 1, FT), F32),
                   SDS((B, NH, n, 1, FB), F32)],
        scratch_shapes=[pltpu.VMEM((SEQ, W), F32)],
        compiler_params=_cp("arbitrary", "arbitrary", "arbitrary"), name=name,
    )(pb, pb, pb, do, lrow, drow, fblk, frow)


def _tri(lower):
    r = lax.broadcasted_iota(jnp.int32, (LANE, LANE), 0)
    c = lax.broadcasted_iota(jnp.int32, (LANE, LANE), 1)
    return ((r >= c) if lower else (r <= c)).astype(F32)


def _tri_dot(t, xblk):
    return jnp.dot(t, xblk, precision=lax.Precision.HIGHEST, preferred_element_type=F32)


def forget_cumsum(flog, bias, name):
    B, S, _ = flog.shape

    def body(f_ref, b_ref, o_ref):
        t = _tri(True)
        carry = jnp.zeros((1, LANE), F32)
        for blk in range(S // LANE):
            z = f_ref[0, blk * LANE:(blk + 1) * LANE, :] + b_ref[...]
            lf = jnp.minimum(z, 0.0) - jnp.log(1.0 + jnp.exp(-jnp.abs(z)))
            cs = _tri_dot(t, lf) + carry
            o_ref[0, blk * LANE:(blk + 1) * LANE, :] = cs
            carry = cs[LANE - 1:LANE, :]

    spec = pl.BlockSpec((1, S, LANE), lambda b: (b, 0, 0))
    return pl.pallas_call(
        body, grid=(B,), in_specs=[spec, pl.BlockSpec((1, LANE), lambda b: (0, 0))], out_specs=spec,
        out_shape=SDS((B, S, LANE), F32), compiler_params=_cp("arbitrary"), name=name,
    )(flog, bias)


def forget_cumsum_bwd(dF, flog, bias, name):
    B, S, _ = flog.shape

    def body(d_ref, f_ref, b_ref, o_ref, db_ref):
        @pl.when(pl.program_id(0) == 0)
        def _():
            db_ref[...] = jnp.zeros_like(db_ref)

        t = _tri(False)
        carry = jnp.zeros((1, LANE), F32)
        tot = jnp.zeros((1, LANE), F32)
        for blk in reversed(range(S // LANE)):
            sl = slice(blk * LANE, (blk + 1) * LANE)
            rc = _tri_dot(t, d_ref[0, sl, :]) + carry
            carry = rc[0:1, :]
            z = f_ref[0, sl, :] + b_ref[...]
            dz = rc * jax.nn.sigmoid(-z)
            o_ref[0, sl, :] = dz
            tot = tot + jnp.sum(dz, axis=0, keepdims=True)
        db_ref[...] += tot

    spec = pl.BlockSpec((1, S, LANE), lambda b: (b, 0, 0))
    vec = pl.BlockSpec((1, LANE), lambda b: (0, 0))
    return pl.pallas_call(
        body, grid=(B,), in_specs=[spec, spec, vec], out_specs=[spec, vec],
        out_shape=[SDS((B, S, LANE), F32), SDS((1, LANE), F32)], compiler_params=_cp("arbitrary"), name=name,
    )(dF, flog, bias)


def mixer_out_fwd(oa, ob, goa, gob, w_out, g_post, x, mod3, name):
    T = x.shape[0]

    def body(oa_ref, ob_ref, goa_ref, gob_ref, w_ref, gp_ref, x_ref, mod_ref, xo_ref, mg_ref, y0_ref):
        a = oa_ref[...]
        b = ob_ref[...]
        mg = jnp.concatenate([a * _rms(a) * goa_ref[...], b * _rms(b) * gob_ref[...]], axis=-1).astype(BF16)
        mg_ref[...] = mg
        y0 = _dot(mg, w_ref[...])
        y0_ref[...] = y0
        xo_ref[...] = x_ref[...] + mod_ref[0, 2:3, :] * (y0 * _rms(y0) * gp_ref[...])

    tok = pl.BlockSpec((TM, D), lambda i: (i, 0))
    half = pl.BlockSpec((TM, WG), lambda i: (i, 0))
    hv = pl.BlockSpec((1, WG), lambda i: (0, 0))
    return pl.pallas_call(
        body, grid=(T // TM,),
        in_specs=[half, half, hv, hv, pl.BlockSpec((D, D), lambda i: (0, 0)), pl.BlockSpec((1, D), lambda i: (0, 0)), tok,
                  pl.BlockSpec((1, 3, D), _mod_map)],
        out_specs=[tok, tok, tok], out_shape=[SDS((T, D), F32), SDS((T, D), BF16), SDS((T, D), F32)],
        compiler_params=_cp("arbitrary"), name=name,
    )(oa, ob, goa, gob, w_out, g_post, x, mod3)


def mixer_out_bwd(dxo, y0, mod3, g_post, w_out, oa, ob, goa, gob, name):
    T = dxo.shape[0]
    nb = T // SEQ
    tiles_per_seq = SEQ // TM

    def body(dxo_ref, y0_ref, mod_ref, gp_ref, w_ref, oa_ref, ob_ref, goa_ref, gob_ref,
             dy0_ref, doa_ref, dob_ref, dmg_ref, dgp_ref, dgoa_ref, dgob_ref, dvb_ref):
        i = pl.program_id(0)

        @pl.when(i == 0)
        def _():
            dgp_ref[...] = jnp.zeros_like(dgp_ref)
            dgoa_ref[...] = jnp.zeros_like(dgoa_ref)
            dgob_ref[...] = jnp.zeros_like(dgob_ref)

        @pl.when(i % tiles_per_seq == 0)
        def _():
            dmg_ref[...] = jnp.zeros_like(dmg_ref)

        dy0, dmg, dg = _post_bwd(dxo_ref[...], y0_ref[...], gp_ref[...], mod_ref[0, 2:3, :], 1.0)
        dmg_ref[0] += dmg
        dgp_ref[...] += dg
        db = dy0.astype(BF16)
        dy0_ref[...] = db
        dm = _dot_nt(db, w_ref[...])
        for o_ref, g_ref, do_ref, dg_ref, sl in ((oa_ref, goa_ref, doa_ref, dgoa_ref, slice(0, WG)),
                                                  (ob_ref, gob_ref, dob_ref, dgob_ref, slice(WG, 2 * WG))):
            o = o_ref[...]
            r = _rms(o)
            oh = o * r
            d = dm[:, sl]
            dg_ref[...] += jnp.sum(d * oh, axis=0, keepdims=True)
            dh = d * g_ref[...]
            do = r * (dh - oh * jnp.mean(dh * oh, axis=-1, keepdims=True))
            do_ref[...] = do
        ind = (lax.broadcasted_iota(jnp.int32, (WG, LANE), 0) // HD == lax.broadcasted_iota(jnp.int32, (WG, LANE), 1)).astype(BF16)
        prod = do * o
        hi = prod.astype(BF16)
        dvb_ref[...] = _dot(hi, ind) + _dot((prod - hi.astype(F32)).astype(BF16), ind)

    tok = pl.BlockSpec((TM, D), lambda i: (i, 0))
    half = pl.BlockSpec((TM, WG), lambda i: (i, 0))
    hv = pl.BlockSpec((1, WG), lambda i: (0, 0))
    vec = pl.BlockSpec((1, D), lambda i: (0, 0))
    return pl.pallas_call(
        body, grid=(T // TM,),
        in_specs=[tok, tok, pl.BlockSpec((1, 3, D), _mod_map), vec, pl.BlockSpec((D, D), lambda i: (0, 0)), half, half, hv, hv],
        out_specs=[tok, half, half, pl.BlockSpec((1, 1, D), _mod_map), vec, hv, hv, pl.BlockSpec((TM, LANE), lambda i: (i, 0))],
        out_shape=[SDS((T, D), BF16), SDS((T, WG), F32), SDS((T, WG), F32), SDS((nb, 1, D), F32), SDS((1, D), F32),
                   SDS((1, WG), F32), SDS((1, WG), F32), SDS((T, LANE), F32)],
        compiler_params=_cp("arbitrary"), name=name,
    )(dxo, y0, mod3, g_post, w_out, oa, ob, goa, gob)


def mixer_proj_bwd(dps, dflog, dxo, x, mod3, g_pre, w_main, w_f, name):
    T = x.shape[0]
    nb = T // SEQ
    tiles_per_seq = SEQ // TM
    ngrp = len(dps)

    def body(*refs):
        dp_refs = refs[:ngrp]
        df_ref, dxo_ref, x_ref, mod_ref, g_ref, w_ref, wf_ref, dx_ref, dmod_ref, dg_ref = refs[ngrp:]
        i = pl.program_id(0)

        @pl.when(i == 0)
        def _():
            dg_ref[...] = jnp.zeros_like(dg_ref)

        @pl.when(i % tiles_per_seq == 0)
        def _():
            dmod_ref[...] = jnp.zeros_like(dmod_ref)

        dh = _dot_nt(df_ref[...].astype(BF16), wf_ref[...])
        for g, dp_ref in enumerate(dp_refs):
            dh = dh + _dot_nt(dp_ref[...], w_ref[:, g * WG:(g + 1) * WG])
        dx, dsh, dsc, dg = _norm_mod_bwd(dh, x_ref[...], g_ref[...], mod_ref[0, 1:2, :])
        dx_ref[...] = dxo_ref[...] + dx
        dmod_ref[0, 0:1, :] += dsh
        dmod_ref[0, 1:2, :] += dsc
        dg_ref[...] += dg

    tok = pl.BlockSpec((TM, D), lambda i: (i, 0))
    vec = pl.BlockSpec((1, D), lambda i: (0, 0))
    return pl.pallas_call(
        body, grid=(T // TM,),
        in_specs=[pl.BlockSpec((TM, WG), lambda i: (i, 0))] * ngrp
        + [pl.BlockSpec((TM, LANE), lambda i: (i, 0)), tok, tok, pl.BlockSpec((1, 3, D), _mod_map), vec,
           pl.BlockSpec((D, IN_MAIN), lambda i: (0, 0)), pl.BlockSpec((D, LANE), lambda i: (0, 0))],
        out_specs=[tok, pl.BlockSpec((1, 2, D), _mod_map), vec],
        out_shape=[SDS((T, D), F32), SDS((nb, 2, D), F32), SDS((1, D), F32)],
        compiler_params=_cp("arbitrary"), name=name,
    )(*dps, dflog, dxo, x, mod3, g_pre, w_main, w_f)


def ada_fwd(c_all, w, b, name):
    n = w.shape[1]
    tn = n // 2

    def body(c_ref, w_ref, b_ref, o_ref):
        cv = c_ref[...]
        o_ref[...] = _dot((cv * jax.nn.sigmoid(cv)).astype(BF16), w_ref[...].astype(BF16)) + b_ref[...]

    R = c_all.shape[0]
    return pl.pallas_call(
        body, grid=(2,),
        in_specs=[pl.BlockSpec((R, D), lambda j: (0, 0)), pl.BlockSpec((D, tn), lambda j: (0, j)), pl.BlockSpec((1, tn), lambda j: (0, j))],
        out_specs=pl.BlockSpec((R, tn), lambda j: (0, j)), out_shape=SDS((R, n), F32),
        compiler_params=_cp("arbitrary"), name=name,
    )(c_all, w, b)


def ada_bwd(c_all, dmod, name):
    R, n = dmod.shape
    tn = n // 2

    def body(c_ref, d_ref, o_ref):
        cv = c_ref[...]
        o_ref[...] = _dot_tn((cv * jax.nn.sigmoid(cv)).astype(BF16), d_ref[...].astype(BF16))

    return pl.pallas_call(
        body, grid=(2,), in_specs=[pl.BlockSpec((R, D), lambda j: (0, 0)), pl.BlockSpec((R, tn), lambda j: (0, j))],
        out_specs=pl.BlockSpec((D, tn), lambda j: (0, j)), out_shape=SDS((D, n), F32),
        compiler_params=_cp("arbitrary"), name=name,
    )(c_all, dmod)


def _adam_math(w, g, m, v):
    m2 = ADAM_B1 * m + (1.0 - ADAM_B1) * g
    v2 = ADAM_B2 * v + (1.0 - ADAM_B2) * (g * g)
    m_hat = m2 / (1.0 - ADAM_B1 ** ADAM_STEP)
    v_hat = v2 / (1.0 - ADAM_B2 ** ADAM_STEP)
    delta = -ADAM_LR * (m_hat / (jnp.sqrt(v_hat) + ADAM_EPS) + ADAM_WD * w)
    return delta, m2, v2


def adam_update(w, g, m, v, tr, name):
    _, R, C = w.shape

    def body(w_ref, g_ref, m_ref, v_ref, d_ref, mo_ref, vo_ref):
        d_ref[0], mo_ref[0], vo_ref[0] = _adam_math(w_ref[0], g_ref[...], m_ref[0], v_ref[0])

    spec = pl.BlockSpec((1, tr, C), lambda i: (0, i, 0))
    gspec = pl.BlockSpec((tr, C), lambda i: (i, 0))
    return pl.pallas_call(
        body, grid=(R // tr,), in_specs=[spec, gspec, spec, spec], out_specs=[spec] * 3, out_shape=[SDS((1, R, C), F32)] * 3,
        compiler_params=_cp("arbitrary"), name=name,
    )(w, g, m, v)


def adam_update_halves(w, mine, other, m, v, cidx, tr, name):
    _, R, C = w.shape
    nh = R // 2 // tr

    def body(c_ref, w_ref, a_ref, b_ref, m_ref, v_ref, g_ref, d_ref, mo_ref, vo_ref):
        first_half = pl.program_id(0) < nh
        g = jnp.where(first_half == (c_ref[0] == 0), a_ref[...], b_ref[...])
        g_ref[0] = g
        d_ref[0], mo_ref[0], vo_ref[0] = _adam_math(w_ref[0], g, m_ref[0], v_ref[0])

    spec = pl.BlockSpec((1, tr, C), lambda i, c_ref: (0, i, 0))
    hspec = pl.BlockSpec((tr, C), lambda i, c_ref: (i % nh, 0))
    return pl.pallas_call(
        body,
        grid_spec=pltpu.PrefetchScalarGridSpec(num_scalar_prefetch=1, grid=(R // tr,), in_specs=[spec, hspec, hspec, spec, spec],
                                               out_specs=[spec] * 4),
        out_shape=[SDS((1, R, C), F32)] * 4, compiler_params=_cp("arbitrary"), name=name,
    )(cidx, w, mine, other, m, v)


def vec_adam(parts, w, m, v, name):
    P, C = parts.shape

    def body(p_ref, w_ref, m_ref, v_ref, g_ref, d_ref, mo_ref, vo_ref):
        g = jnp.sum(p_ref[...], axis=0, keepdims=True)
        g_ref[...] = g
        d_ref[...], mo_ref[...], vo_ref[...] = _adam_math(w_ref[...], g, m_ref[...], v_ref[...])

    return pl.pallas_call(body, out_shape=[SDS((1, C), F32)] * 4, compiler_params=_cp(), name=name)(parts, w, m, v)


def small_adam(parts, layout, ws, ms, vs, name):
    P, C = parts.shape
    k = len(layout)

    def body(*refs):
        p_ref = refs[0]
        w_refs, m_refs, v_refs = refs[1:1 + k], refs[1 + k:1 + 2 * k], refs[1 + 2 * k:1 + 3 * k]
        outs = refs[1 + 3 * k:]
        g_all = jnp.sum(p_ref[...], axis=0, keepdims=True)
        outs[4 * k][...] = g_all
        for n, (off, width) in enumerate(layout):
            g = g_all[:, off:off + width]
            outs[4 * n][...] = g
            outs[4 * n + 1][...], outs[4 * n + 2][...], outs[4 * n + 3][...] = _adam_math(
                w_refs[n][...], g, m_refs[n][...], v_refs[n][...])

    shapes = [SDS((1, width), F32) for _, width in layout for _ in range(4)] + [SDS((1, C), F32)]
    res = pl.pallas_call(body, out_shape=shapes, compiler_params=_cp(), name=name)(parts, *ws, *ms, *vs)
    return [tuple(res[4 * n:4 * n + 4]) for n in range(k)], res[4 * k]


HBM = pl.BlockSpec(memory_space=pltpu.HBM)
VMEM = pl.BlockSpec(memory_space=pltpu.VMEM)


def _place():
    x, y, c = lax.axis_index("x"), lax.axis_index("y"), lax.axis_index("c")
    return x, y, c, [(1 - x, y), (x, 1 - y), (1 - x, 1 - y)]


def all_gather8(xs, name):
    R, C = xs.shape

    def body(x_ref, out_ref, send_sems, recv_sems, local_sem):
        x, y, c, chips = _place()
        me, sibling = (x, y, c), (x, y, 1 - c)

        def slot(px, py, pc):
            return out_ref.at[4 * px + 2 * py + pc]

        def copy(k, block, to, src=None):
            return pltpu.make_async_remote_copy(
                src_ref=slot(*block) if src is None else src, dst_ref=slot(*block),
                send_sem=send_sems.at[k], recv_sem=recv_sems.at[k], device_id=to, device_id_type=MESH)

        mine = pltpu.make_async_copy(x_ref, slot(*me), local_sem)
        mine.start()
        first = [copy(0, me, sibling, src=x_ref)]
        first += [copy(1 + j, me, (*chip, c), src=x_ref) for j, chip in enumerate(chips)]
        for cp in first:
            cp.start()
        passed = [copy(4 + j, (*chip, c), sibling) for j, chip in enumerate(chips)]
        for j, chip in enumerate(chips):
            copy(1 + j, (*chip, c), me).wait_recv()
            passed[j].start()
        copy(0, sibling, me).wait_recv()
        for j, chip in enumerate(chips):
            copy(4 + j, (*chip, 1 - c), me).wait_recv()
        for cp in first + passed:
            cp.wait_send()
        mine.wait()

    return pl.pallas_call(
        body, out_shape=SDS((N_DEV, R, C), xs.dtype), in_specs=[VMEM], out_specs=VMEM,
        scratch_shapes=[pltpu.SemaphoreType.DMA((7,)), pltpu.SemaphoreType.DMA((7,)), pltpu.SemaphoreType.DMA],
        compiler_params=pltpu.CompilerParams(vmem_limit_bytes=VMEM_LIMIT), name=name,
    )(xs)


class ShardGather:
    def __init__(self, shapes, splits):
        self.shapes, self.splits, self.n = shapes, splits, len(shapes)

    def scratch(self):
        n = self.n
        return [pltpu.SemaphoreType.DMA((n, 6)), pltpu.SemaphoreType.DMA((n, 6)), pltpu.SemaphoreType.DMA((n,))]

    def out_shapes(self, dtype):
        return [SDS((N_SHARD,) + tuple(s), dtype) for s in self.shapes]

    def _half(self, ref, k, cc):
        lo, hi = (0, self.splits[k]) if cc == 0 else (self.splits[k], self.shapes[k][0])
        return ref.at[pl.ds(lo, hi - lo)]

    def _phase(self, w_refs, o_refs, sems, finish):
        send_sems, recv_sems, local_sems = sems
        x, y, c, chips = _place()
        sibling = (x, y, 1 - c)
        me_s = 2 * x + y

        def rcopy(src, dst, k, s, to):
            return pltpu.make_async_remote_copy(src_ref=src, dst_ref=dst, send_sem=send_sems.at[k, s],
                                                recv_sem=recv_sems.at[k, s], device_id=to, device_id_type=MESH)

        for cc in (0, 1):
            @pl.when(c == cc)
            def _():
                local = [pltpu.make_async_copy(w_refs[k], o_refs[k].at[me_s], local_sems.at[k]) for k in range(self.n)]
                first = [rcopy(self._half(w_refs[k], k, cc), self._half(o_refs[k].at[me_s], k, cc), k, j, (*chip, c))
                         for k in range(self.n) for j, chip in enumerate(chips)]
                if not finish:
                    for cp in local + first:
                        cp.start()
                    return
                passed = []
                for k in range(self.n):
                    for j, chip in enumerate(chips):
                        land = self._half(o_refs[k].at[2 * chip[0] + chip[1]], k, cc)
                        rcopy(land, land, k, j, (*chip, c)).wait_recv()
                        f = rcopy(land, land, k, 3 + j, sibling)
                        f.start()
                        passed.append(f)
                for k in range(self.n):
                    for j, chip in enumerate(chips):
                        other = self._half(o_refs[k].at[2 * chip[0] + chip[1]], k, 1 - cc)
                        rcopy(other, other, k, 3 + j, sibling).wait_recv()
                for s in first + passed:
                    s.wait_send()
                for cp in local:
                    cp.wait()

    def start(self, w_refs, o_refs, sems):
        self._phase(w_refs, o_refs, sems, False)

    def finish(self, w_refs, o_refs, sems):
        self._phase(w_refs, o_refs, sems, True)


def all_gather_shards(ws, splits, name):
    n = len(ws)
    plan = ShardGather([w.shape for w in ws], splits)

    def body(*refs):
        plan.start(refs[:n], refs[n:2 * n], refs[2 * n:])
        plan.finish(refs[:n], refs[n:2 * n], refs[2 * n:])

    return pl.pallas_call(
        body, out_shape=plan.out_shapes(ws[0].dtype), in_specs=[HBM] * n, out_specs=[HBM] * n,
        scratch_shapes=plan.scratch(), name=name,
    )(*ws)


def sibling_send_half(gs, name):
    n = len(gs)

    def body(*refs):
        g_refs, o_refs = refs[:n], refs[n:2 * n]
        send_sems, recv_sems = refs[2 * n:]
        x, y, c, _ = _place()
        cps = []
        for k in range(n):
            hr = gs[k].shape[1] // 2
            src = g_refs[k].at[:, pl.ds(pl.multiple_of((1 - c) * hr, 8), hr)]
            cp = pltpu.make_async_remote_copy(src_ref=src, dst_ref=o_refs[k], send_sem=send_sems.at[k], recv_sem=recv_sems.at[k],
                                              device_id=(x, y, 1 - c), device_id_type=MESH)
            cp.start()
            cps.append(cp)
        for cp in cps:
            cp.wait()

    return pl.pallas_call(
        body, out_shape=[SDS((N_SHARD, g.shape[1] // 2, g.shape[2]), g.dtype) for g in gs], in_specs=[HBM] * n, out_specs=[HBM] * n,
        scratch_shapes=[pltpu.SemaphoreType.DMA((n,)), pltpu.SemaphoreType.DMA((n,))], name=name,
    )(*gs)


def _scatter_copies(h_refs, o_refs, send_sems, recv_sems):
    _, _, c, chips = _place()
    return [pltpu.make_async_remote_copy(
        src_ref=h_refs[k].at[2 * chip[0] + chip[1]], dst_ref=o_refs[k].at[j], send_sem=send_sems.at[k, j],
        recv_sem=recv_sems.at[k, j], device_id=(*chip, c), device_id_type=MESH)
        for k in range(len(h_refs)) for j, chip in enumerate(chips)]


def chip_scatter(hs, name):
    n = len(hs)

    def body(*refs):
        cps = _scatter_copies(refs[:n], refs[n:2 * n], *refs[2 * n:])
        for cp in cps:
            cp.start()
        for cp in cps:
            cp.wait()

    return pl.pallas_call(
        body, out_shape=[SDS((3,) + h.shape[1:], h.dtype) for h in hs], in_specs=[HBM] * n, out_specs=[HBM] * n,
        scratch_shapes=[pltpu.SemaphoreType.DMA((n, 3)), pltpu.SemaphoreType.DMA((n, 3))], name=name,
    )(*hs)


def sibling_swap(ghs, name):
    n = len(ghs)

    def body(*refs):
        g_refs, o_refs = refs[:n], refs[n:2 * n]
        send_sems, recv_sems = refs[2 * n:]
        x, y, c, _ = _place()
        cps = []
        for k in range(n):
            cp = pltpu.make_async_remote_copy(src_ref=g_refs[k], dst_ref=o_refs[k], send_sem=send_sems.at[k],
                                              recv_sem=recv_sems.at[k], device_id=(x, y, 1 - c), device_id_type=MESH)
            cp.start()
            cps.append(cp)
        for cp in cps:
            cp.wait()

    return pl.pallas_call(
        body, out_shape=[SDS(g.shape, g.dtype) for g in ghs], in_specs=[HBM] * n, out_specs=[HBM] * n,
        scratch_shapes=[pltpu.SemaphoreType.DMA((n,)), pltpu.SemaphoreType.DMA((n,))], name=name,
    )(*ghs)


def pair_sum(g, ra, cidx, name):
    _, r, cols = g.shape
    hr = r // 2

    def body(c_ref, g_ref, a_ref, o_ref):
        o_ref[...] = (g_ref[...] + a_ref[...]).astype(BF16)

    return pl.pallas_call(
        body,
        grid_spec=pltpu.PrefetchScalarGridSpec(
            num_scalar_prefetch=1, grid=(N_SHARD,),
            in_specs=[pl.BlockSpec((1, hr, cols), lambda s, c_ref: (s, c_ref[0], 0)),
                      pl.BlockSpec((1, hr, cols), lambda s, c_ref: (s, 0, 0))],
            out_specs=pl.BlockSpec((1, hr, cols), lambda s, c_ref: (s, 0, 0))),
        out_shape=SDS((N_SHARD, hr, cols), BF16), compiler_params=_cp("arbitrary"), name=name,
    )(cidx, g, ra)


def chip_sum(h, rb, sidx, name):
    _, hr, cols = h.shape

    def body(s_ref, h_ref, r_ref, o_ref):
        o_ref[...] = ((h_ref[0].astype(F32) + r_ref[0].astype(F32)) + r_ref[1].astype(F32)) + r_ref[2].astype(F32)

    return pl.pallas_call(
        body,
        grid_spec=pltpu.PrefetchScalarGridSpec(
            num_scalar_prefetch=1, grid=(1,),
            in_specs=[pl.BlockSpec((1, hr, cols), lambda i, s_ref: (s_ref[0], 0, 0)),
                      pl.BlockSpec((3, hr, cols), lambda i, s_ref: (0, 0, 0))],
            out_specs=pl.BlockSpec((hr, cols), lambda i, s_ref: (0, 0))),
        out_shape=SDS((hr, cols), F32), compiler_params=_cp("arbitrary"), name=name,
    )(sidx, h, rb)


def _shard_cols(g, n_valid):
    r = g.shape[0]
    return g[:, :n_valid].reshape(r, N_SHARD, n_valid // N_SHARD).transpose(1, 0, 2)


def _unshard_cols(o, pad_to):
    _, r, n = o.shape
    full = o.transpose(1, 0, 2).reshape(r, N_SHARD * n)
    return jnp.pad(full, ((0, 0), (0, pad_to - N_SHARD * n)))


def _rows_of_tiles(t):
    B, H, S = t.shape
    return t.reshape(B, H, S // FT, 1, FT)


def mixer_fwd(x1, mod3, g_pre, w_main, w_f, b_forget_pad, goa, gob, w_out, g_post, tabs, nb, gather=None):
    hmix, pa, pb, flog = mixer_proj(x1, mod3, g_pre, w_main, w_f, *tabs, name="mixer_proj")
    out_a, lse_a = band_fwd(pa, name="band_fwd")
    F = forget_cumsum(flog.reshape(nb, SEQ, LANE), b_forget_pad, name="forget_cumsum")
    Fh = F[:, :, :NH].transpose(0, 2, 1)
    fblk = Fh.reshape(nb, NH, SEQ // FB, 1, FB)
    frow = _rows_of_tiles(Fh)
    (out_b, lse_b), gathered = fox_fwd(pb, Fh.reshape(nb, NH, SEQ // FOX_QB, 1, FOX_QB), frow, name="fox_fwd", gather=gather)
    x2, merged, y0m = mixer_out_fwd(out_a, out_b, goa, gob, w_out, g_post, x1, mod3, name="mixer_out_fwd")
    res = dict(hmix=hmix, flog=flog, pa=pa, pb=pb, out_a=out_a, lse_a=lse_a, fblk=fblk, frow=frow, out_b=out_b,
               lrow=_rows_of_tiles(lse_b.reshape(nb, NH, SEQ)), merged=merged, y0m=y0m)
    return x2, res, gathered


def mixer_bwd(dx2, x1, mod3, g_pre, w_main, w_f, b_forget_pad, goa, gob, w_out, g_post, tabs, res, nb):
    T = nb * SEQ
    dy0m, doa, dob, dmgate, dg_post, dgoa, dgob, dvec_b = mixer_out_bwd(
        dx2, res["y0m"], mod3, g_post, w_out, res["out_a"], res["out_b"], goa, gob, name="mixer_out_bwd")
    dqa, dka, dva = band_bwd(res["pa"], doa, res["out_a"], res["lse_a"], *tabs, name="band_bwd")
    drow = _rows_of_tiles(dvec_b[:, :NH].reshape(nb, SEQ, NH).transpose(0, 2, 1))
    dqb, dkb, dvb, dfq, dfk = fox_bwd(res["pb"], dob, res["lrow"], drow, res["fblk"], res["frow"], name="fox_bwd")
    dF = (dfq.reshape(nb, NH, SEQ) + dfk.reshape(nb, NH, SEQ)).transpose(0, 2, 1)
    dF = jnp.pad(dF, ((0, 0), (0, 0), (0, LANE - NH)))
    dflog, dbf = forget_cumsum_bwd(dF, res["flog"].reshape(nb, SEQ, LANE), b_forget_pad, name="forget_cumsum_bwd")
    dflog = dflog.reshape(T, LANE)
    dps = (dqa, dka, dva, dqb, dkb, dvb)
    dx1, dmod2, dg_pre = mixer_proj_bwd(dps, dflog, dx2, x1, mod3, g_pre, w_main, w_f, name="mixer_proj_bwd")
    g_main = matmul_tn_cols(res["hmix"], dps, 1024, name="grad_w_in")
    g_f = matmul_tn(res["hmix"], dflog.astype(BF16), D, LANE, 1024, name="grad_w_forget")
    g_out = matmul_tn(res["merged"], dy0m, D, D, 1024, name="grad_w_out")
    dmod3 = jnp.concatenate([dmod2, dmgate], axis=1)
    return dx1, dmod3, dict(g_pre=dg_pre, g_post=dg_post, goa=dgoa, gob=dgob, b_forget=dbf[:, :NH],
                            w_in=jnp.concatenate([g_main, g_f[:, :NH]], axis=1), w_out=g_out)


def ffn_grads(h, dy0, act, dgate, dup, pre, reduce=None):
    g_gate = matmul_tn(h, dgate, D, DFF_PAD, 1024, name=pre + "_grad_gate")
    if reduce is None:
        g_up = matmul_tn(h, dup, D, DFF_PAD, 1024, name=pre + "_grad_up")
        g_down = matmul_tn(act, dy0, FF_TN, D, 1024, name=pre + "_grad_down")
        return (g_gate, g_up, g_down), {}
    hs_gate = reduce("gate", g_gate)
    g_up, rb_gate = matmul_tn(h, dup, D, DFF_PAD, 1024, name=pre + "_grad_up", scatter=hs_gate)
    hs_up = reduce("up", g_up)
    g_down, rb_up = matmul_tn(act, dy0, FF_TN, D, 1024, name=pre + "_grad_down", scatter=hs_up)
    return (g_gate, g_up, g_down), {"gate": (hs_gate[0], rb_gate[0]), "up": (hs_up[0], rb_up[0])}


def local_step(x0, tgt, pos_col, mod, wfull, p, late_weights=None, last_weights=None, early_grads=None, last_reduce=None):
    T = x0.shape[0]
    nb = T // SEQ
    mod_ff1, mod_mix, mod_ff2 = mod[:, 0:3], mod[:, 3:6], mod[:, 6:9]
    tabs = rope_tables(pos_col, name="rope_tables")
    bf_pad = jnp.pad(p["b_forget"], ((0, 0), (0, LANE - NH)))

    (x1, h1, gate1, up1, y01), gathered = ffn_fwd(
        x0, mod_ff1, p["g_pre_ff1"], p["g_post_ff1"], wfull["w_ff1_gate"], wfull["w_ff1_up"], wfull["w_ff1_down"], 0.5,
        name="ff1_fwd", gather=None if late_weights is None else late_weights[:2])
    if late_weights is not None:
        wfull = {**wfull, **late_weights[2](gathered)}
    x2, res, gathered = mixer_fwd(x1, mod_mix, p["g_pre_mix"], wfull["w_main"], wfull["w_f"], bf_pad, p["g_out_a"],
                                  p["g_out_b"], wfull["w_out"], p["g_post_mix"], tabs, nb,
                                  gather=None if last_weights is None else last_weights[:2])
    if last_weights is not None:
        wfull = {**wfull, **last_weights[2](gathered)}
    (x3, h2, gate2, up2, y02), _ = ffn_fwd(x2, mod_ff2, p["g_pre_ff2"], p["g_post_ff2"], wfull["w_ff2_gate"],
                                           wfull["w_ff2_up"], wfull["w_ff2_down"], 0.5, name="ff2_fwd")

    (dx2, dy02, act2, dgate2, dup2, dmod_ff2, dgpre2, dgpost2), (loss_part,) = ffn_bwd(
        x3, x2, y02, mod_ff2, p["g_pre_ff2"], p["g_post_ff2"], gate2, up2, wfull["w_ff2_gate"], wfull["w_ff2_up"],
        wfull["w_ff2_down"], 0.5, name="ff2_bwd", target=tgt)
    gw = {}
    (gw["w_ff2_gate"], gw["w_ff2_up"], gw["w_ff2_down"]), _ = ffn_grads(h2, dy02, act2, dgate2, dup2, "ff2")
    dx1, dmod_mix, gmix = mixer_bwd(dx2, x1, mod_mix, p["g_pre_mix"], wfull["w_main"], wfull["w_f"], bf_pad, p["g_out_a"],
                                    p["g_out_b"], wfull["w_out"], p["g_post_mix"], tabs, res, nb)
    gw["w_in"], gw["w_out"] = gmix["w_in"], gmix["w_out"]
    (dx0, dy01, act1, dgate1, dup1, dmod_ff1, dgpre1, dgpost1), scattered = ffn_bwd(
        dx1, x0, y01, mod_ff1, p["g_pre_ff1"], p["g_post_ff1"], gate1, up1, wfull["w_ff1_gate"], wfull["w_ff1_up"],
        wfull["w_ff1_down"], 0.5, name="ff1_bwd", scatter=None if early_grads is None else early_grads(gw))
    (gw["w_ff1_gate"], gw["w_ff1_up"], gw["w_ff1_down"]), chained = ffn_grads(h1, dy01, act1, dgate1, dup1, "ff1", last_reduce)
    dmod = jnp.concatenate([dmod_ff1, dmod_mix, dmod_ff2], axis=1).reshape(nb, 9 * D)
    small = dict(g_pre_ff1=dgpre1, g_post_ff1=dgpost1, g_pre_mix=gmix["g_pre"], g_post_mix=gmix["g_post"], g_pre_ff2=dgpre2,
                 g_post_ff2=dgpost2, g_out_a=gmix["goa"], g_out_b=gmix["gob"], b_forget=gmix["b_forget"])
    return loss_part, dx0, dmod, gw, small, scattered, chained


def kernel(x, c, positions, w_ada, b_ada, g_pre_ff1, g_post_ff1, w_ff1_gate, w_ff1_up, w_ff1_down, g_pre_mix, g_post_mix, w_in, b_forget, g_out_a, g_out_b, w_out, g_pre_ff2, g_post_ff2, w_ff2_gate, w_ff2_up, w_ff2_down, loss_target, m_w_ada, m_b_ada, m_g_pre_ff1, m_g_post_ff1, m_w_ff1_gate, m_w_ff1_up, m_w_ff1_down, m_g_pre_mix, m_g_post_mix, m_w_in, m_b_forget, m_g_out_a, m_g_out_b, m_w_out, m_g_pre_ff2, m_g_post_ff2, m_w_ff2_gate, m_w_ff2_up, m_w_ff2_down, v_w_ada, v_b_ada, v_g_pre_ff1, v_g_post_ff1, v_w_ff1_gate, v_w_ff1_up, v_w_ff1_down, v_g_pre_mix, v_g_post_mix, v_w_in, v_b_forget, v_g_out_a, v_g_out_b, v_w_out, v_g_pre_ff2, v_g_post_ff2, v_w_ff2_gate, v_w_ff2_up, v_w_ff2_down):
    args = dict(locals())
    nb = x.shape[0]
    T = nb * SEQ
    ax, ay, ac = lax.axis_index("x"), lax.axis_index("y"), lax.axis_index("c")
    shard = 2 * ax + ay
    cidx = jnp.reshape(ac, (1,)).astype(jnp.int32)
    sidx = jnp.reshape(shard, (1,)).astype(jnp.int32)

    big = ["w_ff1_gate", "w_ff1_up", "w_ff1_down", "w_in", "w_out", "w_ff2_gate", "w_ff2_up", "w_ff2_down"]
    vecs = ["g_pre_ff1", "g_post_ff1", "g_pre_mix", "g_post_mix", "g_pre_ff2", "g_post_ff2"]

    first, late = big[:3], big[3:]
    splits = dict(zip(big, [512, 512, 352, 512, 128, 512, 512, 352]))

    def assemble(names, gathered):
        out = {}
        for n, o in zip(names, gathered):
            if n.endswith("gate") or n.endswith("up"):
                out[n] = _unshard_cols(o, DFF_PAD)
            elif n.endswith("down"):
                out[n] = jnp.pad(o.reshape(DFF, D), ((0, DFF_PAD - DFF), (0, 0)))
            elif n == "w_in":
                full = _unshard_cols(o, IN_COLS)
                out["w_main"] = full[:, :IN_MAIN]
                out["w_f"] = jnp.pad(full[:, IN_MAIN:], ((0, 0), (0, LANE - NH)))
            else:
                out[n] = o.reshape(D, D)
        return out

    wfull = assemble(first, all_gather_shards([args[n][0].astype(BF16) for n in first], [splits[n] for n in first],
                                              name="all_gather_weights"))
    def gather_plan(names):
        return ([args[n][0].astype(BF16) for n in names], [splits[n] for n in names], functools.partial(assemble, names))

    late_weights, last_weights = gather_plan(late[:2]), gather_plan(late[2:])

    ncol = w_ada.shape[2]
    c_all = all_gather8(c, name="all_gather_c").reshape(N_DEV * nb, D)
    b_loc = lax.dynamic_slice(b_ada, (0, shard * ncol), (1, ncol))
    mod_loc = ada_fwd(c_all, w_ada[0], b_loc, name="ada_fwd")
    mod_g = all_gather8(mod_loc, name="all_gather_mod")
    row0 = (4 * ax + 2 * ay + ac) * nb
    mod_rows = lax.dynamic_slice(mod_g, (0, row0, 0), (N_DEV, nb, ncol))
    mod = jnp.concatenate([mod_rows[2 * s] for s in range(N_SHARD)], axis=-1).reshape(nb, 9, D)

    small_in = dict(g_pre_ff1=g_pre_ff1, g_post_ff1=g_post_ff1, g_pre_mix=g_pre_mix, g_post_mix=g_post_mix, g_pre_ff2=g_pre_ff2,
                    g_post_ff2=g_post_ff2, g_out_a=g_out_a, g_out_b=g_out_b, b_forget=b_forget)
    def shard_blocked(n, g):
        if n.endswith("gate") or n.endswith("up"):
            return _shard_cols(g, DFF)
        if n.endswith("down"):
            return g[:DFF].reshape(N_SHARD, DFF // N_SHARD, D)
        if n == "w_in":
            return _shard_cols(g, IN_COLS)
        return g.reshape(N_SHARD, D // N_SHARD, D)

    def chip_sums(names, gw, tag):
        gsb = [shard_blocked(n, gw[n]) for n in names]
        ras = sibling_send_half(gsb, name="grad_sibling_send_" + tag)
        return [pair_sum(g, ra, cidx, name=f"grad_pair_sum_{n}") for n, g, ra in zip(names, gsb, ras)]

    hs = {}

    def early_grads(gw):
        hs.update(zip(late, chip_sums(late, gw, "late")))
        return [hs[n] for n in late]

    def last_reduce(which, g):
        return chip_sums(["w_ff1_" + which], {"w_ff1_" + which: g}, which)

    loss_part, dx0, dmod, gw, small, rbs_late, chained = local_step(
        x.reshape(T, D), loss_target.reshape(T, D), positions.reshape(T, 1), mod, wfull, small_in, late_weights, last_weights,
        early_grads, last_reduce)

    dmod_all = all_gather8(dmod, name="all_gather_dmod").reshape(N_DEV * nb, 9 * D)
    dmod_loc = lax.dynamic_slice(dmod_all, (0, shard * ncol), (N_DEV * nb, ncol))
    g_w_ada = ada_bwd(c_all, dmod_loc, name="ada_bwd")

    rbs = dict(zip(late, rbs_late))
    for which, (h, rb) in chained.items():
        hs["w_ff1_" + which], rbs["w_ff1_" + which] = h, rb
    hs["w_ff1_down"] = chip_sums(["w_ff1_down"], gw, "down")[0]
    rbs["w_ff1_down"] = chip_scatter([hs["w_ff1_down"]], name="grad_chip_scatter")[0]
    ghs = [chip_sum(hs[n], rbs[n], sidx, name=f"grad_chip_sum_{n}") for n in big]
    theirs = sibling_swap(ghs, name="grad_sibling_swap")

    row6 = jnp.concatenate([small["g_out_a"], small["g_out_b"]], axis=1)
    row7 = jnp.concatenate([small["b_forget"], loss_part[0:1, 0:1], jnp.zeros((1, D - NH - 1), F32)], axis=1)
    pack = jnp.concatenate([small[n] for n in vecs] + [row6, row7], axis=0)
    packed = all_gather8(pack, name="all_gather_small").reshape(N_DEV, 8 * D)

    names = vecs + ["g_out_a", "g_out_b", "b_forget"]
    layout = [(i * D, D) for i in range(len(vecs))] + [(6 * D, WG), (6 * D + WG, WG), (7 * D, NH)]
    per_param, packed_sum = small_adam(packed, layout, [args[n] for n in names], [args["m_" + n] for n in names],
                                       [args["v_" + n] for n in names], name="adam_small")
    outs = dict(grad={}, delta={}, new_m={}, new_v={})
    for n, (g, d, m2, v2) in zip(names, per_param):
        outs["grad"][n], outs["delta"][n], outs["new_m"][n], outs["new_v"][n] = g, d, m2, v2
    loss = packed_sum[0, 7 * D + NH]
    outs["grad"]["b_ada"], outs["delta"]["b_ada"], outs["new_m"]["b_ada"], outs["new_v"]["b_ada"] = vec_adam(
        dmod_all, b_ada, m_b_ada, v_b_ada, name="adam_b_ada")

    for n, mine, other in zip(big, ghs, theirs):
        tr = 128 if mine.shape[0] % 128 == 0 else mine.shape[0]
        outs["grad"][n], outs["delta"][n], outs["new_m"][n], outs["new_v"][n] = adam_update_halves(
            args[n], mine, other, args["m_" + n], args["v_" + n], cidx, tr, name="adam_" + n)
    outs["delta"]["w_ada"], outs["new_m"]["w_ada"], outs["new_v"]["w_ada"] = adam_update(
        w_ada, g_w_ada, m_w_ada, v_w_ada, 128, name="adam_w_ada")
    outs["grad"]["w_ada"] = g_w_ada[None]

    order = ["w_ada", "b_ada", "g_pre_ff1", "g_post_ff1", "w_ff1_gate", "w_ff1_up", "w_ff1_down", "g_pre_mix", "g_post_mix", "w_in",
             "b_forget", "g_out_a", "g_out_b", "w_out", "g_pre_ff2", "g_post_ff2", "w_ff2_gate", "w_ff2_up", "w_ff2_down"]
    result = [loss, dx0.reshape(nb, SEQ, D)]
    for kind in ("grad", "delta", "new_m", "new_v"):
        result += [outs[kind][n] for n in order]
    return tuple(result)
```

```python
import functools
import math

import jax
import jax.numpy as jnp
from jax import lax
from jax.experimental import pallas as pl
from jax.experimental.pallas import tpu as pltpu

D = 1024
SEQ = 2048
HD = 64
NH = 8
WG = NH * HD
DFF = 2752
DFF_PAD = 2816
IN_MAIN = 6 * WG
IN_COLS = IN_MAIN + NH
N_SHARD = 4
N_DEV = 8
LANE = 128
BF16_ROW_TILE = 16
QB = 128
ROWS = 256
FB = 512
FT = 512
FOX_QB = 512
FOX_PAIRS = 4
FOX_PAIRS_BWD = 2
BAND_UNROLL = 8
BAND_UNROLL_BWD = 4
PATTERNS = ((1, 16), (4, 4), (16, 1))
ROPE_THETA = 500000.0
EPS = 1e-6
NEG = -1e30
ATTN_SCALE = HD ** -0.5
TM = 512
TM_FFN = 512
TM_BWD = 256
VMEM_LIMIT = 56 * 1024 * 1024

ADAM_LR, ADAM_B1, ADAM_B2, ADAM_EPS, ADAM_WD, ADAM_STEP = 0.001, 0.9, 0.999, 1e-08, 0.01, 10

F32 = jnp.float32
BF16 = jnp.bfloat16
MESH = pl.DeviceIdType.MESH
SDS = jax.ShapeDtypeStruct


def _cp(*sem):
    return pltpu.CompilerParams(dimension_semantics=sem, vmem_limit_bytes=VMEM_LIMIT)


def _dot(a, b):
    return jnp.dot(a, b, preferred_element_type=F32)


def _dot_nt(a, b):
    return lax.dot_general(a, b, (((1,), (1,)), ((), ())), preferred_element_type=F32)


def _dot_tn(a, b):
    return lax.dot_general(a, b, (((0,), (0,)), ((), ())), preferred_element_type=F32)


def _rms(xf):
    return lax.rsqrt(jnp.mean(xf * xf, axis=-1, keepdims=True) + EPS)


def _norm_mod_bwd(dh, xf, g, scale):
    r = _rms(xf)
    xh = xf * r
    dsh = jnp.sum(dh, axis=0, keepdims=True)
    dsc = jnp.sum(dh * (xh * g), axis=0, keepdims=True)
    dn = dh * (1.0 + scale)
    dg = jnp.sum(dn * xh, axis=0, keepdims=True)
    dxh = dn * g
    dx = r * (dxh - xh * jnp.mean(dxh * xh, axis=-1, keepdims=True))
    return dx, dsh, dsc, dg


def _post_bwd(dxo, y0, g, mgate, gs):
    r = _rms(y0)
    yh = y0 * r
    dmg = gs * jnp.sum(dxo * (yh * g), axis=0, keepdims=True)
    dy = (gs * mgate) * dxo
    dg = jnp.sum(dy * yh, axis=0, keepdims=True)
    dyh = dy * g
    dy0 = r * (dyh - yh * jnp.mean(dyh * yh, axis=-1, keepdims=True))
    return dy0, dmg, dg


def _mod_map(i, *_):
    return ((i * TM) // SEQ, 0, 0)


FF_TN = 1408
FF_TILES = ((0, 768), (768, 1536), (1536, 2304), (2304, 2816))


def _resident_scratch():
    return [pltpu.VMEM((D, DFF_PAD), BF16), pltpu.VMEM((D, DFF_PAD), BF16), pltpu.VMEM((DFF_PAD, D), BF16),
            pltpu.SemaphoreType.DMA((3,))]


def _load_resident(first_step, srcs, dsts, sems):
    @pl.when(first_step)
    def _():
        cps = [pltpu.make_async_copy(s, d, sems.at[k]) for k, (s, d) in enumerate(zip(srcs, dsts))]
        for cp in cps:
            cp.start()
        for cp in cps:
            cp.wait()


def ffn_fwd(x, mod3, g_pre, g_post, wg, wu, wd, gs, name, gather=None):
    T = x.shape[0]
    tm = TM_FFN
    ng = 0 if gather is None else len(gather[0])
    plan = None if gather is None else ShardGather([w.shape for w in gather[0]], gather[1])

    def body(*refs):
        x_ref, mod_ref, gpre_ref, gpost_ref = refs[:4]
        xo_ref, h_ref, gate_ref, up_ref, y0_ref = refs[7 + ng:12 + ng]
        wg_ref, wu_ref, wd_ref, wsem = refs[12 + 2 * ng:16 + 2 * ng]
        i = pl.program_id(0)
        if plan is not None:
            comm = (refs[7:7 + ng], refs[12 + ng:12 + 2 * ng], refs[16 + 2 * ng:])
            pl.when(i == 0)(lambda: plan.start(*comm))
        _load_resident(i == 0, refs[4:7], (wg_ref, wu_ref, wd_ref), wsem)

        xf = x_ref[...]
        hb = ((xf * _rms(xf) * gpre_ref[...]) * (1.0 + mod_ref[0, 1:2, :]) + mod_ref[0, 0:1, :]).astype(BF16)
        h_ref[...] = hb
        y0 = None
        for lo, hi in FF_TILES:
            gate = _dot(hb, wg_ref[:, lo:hi])
            up = _dot(hb, wu_ref[:, lo:hi])
            gate_ref[:, lo:hi] = gate.astype(BF16)
            up_ref[:, lo:hi] = up.astype(BF16)
            part = _dot((gate * jax.nn.sigmoid(gate) * up).astype(BF16), wd_ref[lo:hi, :])
            y0 = part if y0 is None else y0 + part
        y0_ref[...] = y0
        xo_ref[...] = xf + (gs * mod_ref[0, 2:3, :]) * (y0 * _rms(y0) * gpost_ref[...])

        if plan is not None:
            pl.when(i == T // tm - 1)(lambda: plan.finish(*comm))

    tok = pl.BlockSpec((tm, D), lambda i: (i, 0))
    vec = pl.BlockSpec((1, D), lambda i: (0, 0))
    hid = pl.BlockSpec((tm, DFF_PAD), lambda i: (i, 0))
    outs = pl.pallas_call(
        body, grid=(T // tm,),
        in_specs=[tok, pl.BlockSpec((1, 3, D), lambda i: ((i * tm) // SEQ, 0, 0)), vec, vec, HBM, HBM, HBM] + [HBM] * ng,
        out_specs=[tok, tok, hid, hid, tok] + [HBM] * ng,
        out_shape=[SDS((T, D), F32), SDS((T, D), BF16), SDS((T, DFF_PAD), BF16), SDS((T, DFF_PAD), BF16), SDS((T, D), F32)]
        + ([] if plan is None else plan.out_shapes(BF16)),
        scratch_shapes=_resident_scratch() + ([] if plan is None else plan.scratch()),
        compiler_params=_cp("arbitrary"), name=name,
    )(x, mod3, g_pre, g_post, wg, wu, wd, *([] if gather is None else gather[0]))
    return outs[:5], outs[5:]


def ffn_bwd(dxo, x, y0, mod3, g_pre, g_post, gate, up, wg, wu, wd, gs, name, scatter=None, target=None):
    assert scatter is None or target is None
    T = x.shape[0]
    nb = T // SEQ
    tm = TM_BWD
    tiles_per_seq = SEQ // tm
    ns = 0 if scatter is None else len(scatter)
    ne = ns + (target is not None)

    def body(*refs):
        dxo_ref, x_ref, y0_ref, mod_ref, gpre_ref, gpost_ref, gate_ref, up_ref = refs[:8]
        dx_ref, dy0_ref, act_ref, dgate_ref, dup_ref, dmod_ref, dgpre_ref, dgpost_ref = refs[11 + ne:19 + ne]
        wg_ref, wu_ref, wd_ref, wsem = refs[19 + 2 * ne:23 + 2 * ne]
        i = pl.program_id(0)
        _load_resident(i == 0, refs[8:11], (wg_ref, wu_ref, wd_ref), wsem)
        if ns:
            comm = (refs[11:11 + ns], refs[19 + ns:19 + 2 * ns], *refs[23 + 2 * ns:])

            @pl.when(i == 0)
            def _():
                for cp in _scatter_copies(*comm):
                    cp.start()

        @pl.when(i == 0)
        def _():
            dgpre_ref[...] = jnp.zeros_like(dgpre_ref)
            dgpost_ref[...] = jnp.zeros_like(dgpost_ref)

        @pl.when(i % tiles_per_seq == 0)
        def _():
            dmod_ref[...] = jnp.zeros_like(dmod_ref)

        dxo = dxo_ref[...]
        if target is not None:
            loss_ref = refs[19 + ne]

            @pl.when(i == 0)
            def _():
                loss_ref[...] = jnp.zeros_like(loss_ref)

            err = dxo - refs[11][...]
            loss_ref[...] += jnp.sum(err * err) * (0.5 / D)
            dxo = err * (1.0 / D)
        dy0, dmg, dg = _post_bwd(dxo, y0_ref[...], gpost_ref[...], mod_ref[0, 2:3, :], gs)
        dmod_ref[0, 2:3, :] += dmg
        dgpost_ref[...] += dg
        db = dy0.astype(BF16)
        dy0_ref[...] = db
        dh = None
        for lo, hi in FF_TILES:
            dact = _dot_nt(db, wd_ref[lo:hi, :])
            g = gate_ref[:, lo:hi].astype(F32)
            u = up_ref[:, lo:hi].astype(F32)
            sig = jax.nn.sigmoid(g)
            sl = g * sig
            dgate = (dact * u * (sig * (1.0 + g * (1.0 - sig)))).astype(BF16)
            dup = (dact * sl).astype(BF16)
            act_ref[:, lo:hi] = (sl * u).astype(BF16)
            dgate_ref[:, lo:hi] = dgate
            dup_ref[:, lo:hi] = dup
            part = _dot_nt(dgate, wg_ref[:, lo:hi]) + _dot_nt(dup, wu_ref[:, lo:hi])
            dh = part if dh is None else dh + part
        dx, dsh, dsc, dg = _norm_mod_bwd(dh, x_ref[...], gpre_ref[...], mod_ref[0, 1:2, :])
        dx_ref[...] = dxo + dx
        dmod_ref[0, 0:1, :] += dsh
        dmod_ref[0, 1:2, :] += dsc
        dgpre_ref[...] += dg

        if ns:
            @pl.when(i == T // tm - 1)
            def _():
                for cp in _scatter_copies(*comm):
                    cp.wait()

    tok = pl.BlockSpec((tm, D), lambda i: (i, 0))
    vec = pl.BlockSpec((1, D), lambda i: (0, 0))
    hid = pl.BlockSpec((tm, DFF_PAD), lambda i: (i, 0))
    modspec = pl.BlockSpec((1, 3, D), lambda i: ((i * tm) // SEQ, 0, 0))
    outs = pl.pallas_call(
        body, grid=(T // tm,),
        in_specs=[tok, tok, tok, modspec, vec, vec, hid, hid, HBM, HBM, HBM] + [HBM] * ns + [tok] * (ne - ns),
        out_specs=[tok, tok, hid, hid, hid, modspec, vec, vec] + [HBM] * ns
        + [pl.BlockSpec((8, LANE), lambda i: (0, 0))] * (ne - ns),
        out_shape=[SDS((T, D), F32), SDS((T, D), BF16), SDS((T, DFF_PAD), BF16), SDS((T, DFF_PAD), BF16),
                   SDS((T, DFF_PAD), BF16), SDS((nb, 3, D), F32), SDS((1, D), F32), SDS((1, D), F32)]
        + [SDS((3,) + h.shape[1:], h.dtype) for h in (scatter or [])] + [SDS((8, LANE), F32)] * (ne - ns),
        scratch_shapes=_resident_scratch()
        + ([pltpu.SemaphoreType.DMA((ns, 3)), pltpu.SemaphoreType.DMA((ns, 3))] if ns else []),
        compiler_params=_cp("arbitrary"), name=name,
    )(dxo, x, y0, mod3, g_pre, g_post, gate, up, wg, wu, wd, *(scatter or []), *([] if target is None else [target]))
    return outs[:8], outs[8:]


def matmul_tn(a, b, tm, tn, tk, name, scatter=None):
    T, M = a.shape
    N = b.shape[1]
    grid = (M // tm, N // tn, T // tk)
    ns = 0 if scatter is None else len(scatter)

    def body(*refs):
        a_ref, b_ref = refs[:2]
        o_ref = refs[2 + ns]
        ids = [pl.program_id(ax) for ax in range(3)]
        if ns:
            comm = (refs[2:2 + ns], refs[3 + ns:3 + 2 * ns], *refs[3 + 2 * ns:])

            @pl.when((ids[0] == 0) & (ids[1] == 0) & (ids[2] == 0))
            def _():
                for cp in _scatter_copies(*comm):
                    cp.start()

        @pl.when(ids[2] == 0)
        def _():
            o_ref[...] = jnp.zeros_like(o_ref)

        o_ref[...] += _dot_tn(a_ref[...], b_ref[...])

        if ns:
            @pl.when((ids[0] == grid[0] - 1) & (ids[1] == grid[1] - 1) & (ids[2] == grid[2] - 1))
            def _():
                for cp in _scatter_copies(*comm):
                    cp.wait()

    outs = pl.pallas_call(
        body, grid=grid,
        in_specs=[pl.BlockSpec((tk, tm), lambda i, j, k: (k, i)), pl.BlockSpec((tk, tn), lambda i, j, k: (k, j))] + [HBM] * ns,
        out_specs=[pl.BlockSpec((tm, tn), lambda i, j, k: (i, j))] + [HBM] * ns,
        out_shape=[SDS((M, N), F32)] + [SDS((3,) + h.shape[1:], h.dtype) for h in (scatter or [])],
        scratch_shapes=[pltpu.SemaphoreType.DMA((ns, 3)), pltpu.SemaphoreType.DMA((ns, 3))] if ns else [],
        compiler_params=_cp("arbitrary", "arbitrary", "arbitrary"), name=name,
    )(a, b, *(scatter or []))
    return outs[0] if scatter is None else (outs[0], outs[1:])


def matmul_tn_cols(a, bs, tk, name):
    T, M = a.shape
    n = bs[0].shape[1]
    ng = len(bs)

    def body(*refs):
        a_ref, b_refs, o_ref = refs[0], refs[1:1 + ng], refs[1 + ng]

        @pl.when(pl.program_id(0) == 0)
        def _():
            o_ref[...] = jnp.zeros_like(o_ref)

        av = a_ref[...]
        for g, b_ref in enumerate(b_refs):
            o_ref[:, g * n:(g + 1) * n] += _dot_tn(av, b_ref[...])

    return pl.pallas_call(
        body, grid=(T // tk,),
        in_specs=[pl.BlockSpec((tk, M), lambda k: (k, 0))] + [pl.BlockSpec((tk, n), lambda k: (k, 0))] * ng,
        out_specs=pl.BlockSpec((M, ng * n), lambda k: (0, 0)), out_shape=SDS((M, ng * n), F32),
        compiler_params=_cp("arbitrary"), name=name,
    )(a, *bs)


def rope_tables(pos_col, name):
    T = pos_col.shape[0]
    tm = 1024

    def body(p_ref, c_ref, s1_ref, s2_ref):
        lane = lax.broadcasted_iota(jnp.int32, (1, LANE), 1)
        l64 = lane % HD
        inv_freq = jnp.exp((l64 % 8).astype(F32) * (-math.log(ROPE_THETA) / 8.0))
        ang = p_ref[...].astype(F32) * inv_freq
        cs = jnp.cos(ang)
        sn = jnp.sin(ang)
        c_ref[...] = jnp.where(l64 < 16, cs, 1.0)
        s1_ref[...] = jnp.where(l64 < 8, -sn, 0.0)
        s2_ref[...] = jnp.where((l64 >= 8) & (l64 < 16), sn, 0.0)

    tab = pl.BlockSpec((tm, LANE), lambda i: (i, 0))
    return pl.pallas_call(
        body, grid=(T // tm,), in_specs=[pl.BlockSpec((tm, 1), lambda i: (i, 0))], out_specs=[tab, tab, tab],
        out_shape=[SDS((T, LANE), F32)] * 3, compiler_params=_cp("arbitrary"), name=name,
    )(pos_col)


def mixer_proj(x, mod3, g_pre, w_main, w_f, rc, rs1, rs2, name):
    T = x.shape[0]

    def body(x_ref, mod_ref, g_ref, w_ref, wf_ref, c_ref, s1_ref, s2_ref, h_ref, pa_ref, pb_ref, f_ref):
        xf = x_ref[...]
        h = (xf * _rms(xf) * g_ref[...]) * (1.0 + mod_ref[0, 1:2, :]) + mod_ref[0, 0:1, :]
        hb = h.astype(BF16)
        h_ref[...] = hb
        f_ref[...] = _dot(hb, wf_ref[...])
        c, s1, s2 = c_ref[...], s1_ref[...], s2_ref[...]
        for grp in range(2):
            pr = _dot(hb, w_ref[:, grp * WG:(grp + 1) * WG])
            for k in range(WG // LANE):
                t = pr[:, k * LANE:(k + 1) * LANE]
                pa_ref[:, grp * WG + k * LANE:grp * WG + (k + 1) * LANE] = (
                    t * c + pltpu.roll(t, LANE - 8, 1) * s1 + pltpu.roll(t, 8, 1) * s2)
        pa_ref[:, 2 * WG:3 * WG] = _dot(hb, w_ref[:, 2 * WG:3 * WG])
        for grp in range(3):
            pb_ref[:, grp * WG:(grp + 1) * WG] = _dot(hb, w_ref[:, (3 + grp) * WG:(4 + grp) * WG]).astype(BF16)

    tok = pl.BlockSpec((TM, D), lambda i: (i, 0))
    vec = pl.BlockSpec((1, D), lambda i: (0, 0))
    tab = pl.BlockSpec((TM, LANE), lambda i: (i, 0))
    grp3 = pl.BlockSpec((TM, 3 * WG), lambda i: (i, 0))
    return pl.pallas_call(
        body, grid=(T // TM,),
        in_specs=[tok, pl.BlockSpec((1, 3, D), _mod_map), vec, pl.BlockSpec((D, IN_MAIN), lambda i: (0, 0)),
                  pl.BlockSpec((D, LANE), lambda i: (0, 0)), tab, tab, tab],
        out_specs=[tok, grp3, grp3, tab],
        out_shape=[SDS((T, D), BF16), SDS((T, 3 * WG), F32), SDS((T, 3 * WG), BF16), SDS((T, LANE), F32)],
        compiler_params=_cp("arbitrary"), name=name,
    )(x, mod3, g_pre, w_main, w_f, rc, rs1, rs2)


def _head_lanes():
    return lax.broadcasted_iota(jnp.int32, (1, LANE), 1) < HD


def _pair(m0, a, b):
    return jnp.where(m0, a, b)


def _band_rows(i, d, nbc):
    if nbc == 1:
        return i, i, 0
    r, mb = i // nbc, i % nbc
    return r + mb * (QB * d), r + jnp.maximum(mb - 1, 0) * (QB * d), jnp.where(mb > 0, QB, 0)


def _rows(start, size, d):
    return pl.ds(pl.multiple_of(start, QB), size) if d == 1 else pl.ds(start, size, stride=d)


def _band_valid(span, off):
    rq = lax.broadcasted_iota(jnp.int32, (QB, span), 0)
    rel = lax.broadcasted_iota(jnp.int32, (QB, span), 1) - off
    return (rel <= rq) & (rel >= rq - QB)


def band_fwd(pa, name):
    T = pa.shape[0]
    B = T // SEQ
    NP = WG // LANE

    def body(q_ref, k_ref, v_ref, out_ref, lse_ref, o_s, l_s):
        m0 = _head_lanes()
        for pidx, (d, nbc) in enumerate(PATTERNS):
            span = QB if nbc == 1 else 2 * QB

            def blk(it, carry, pidx=pidx, d=d, nbc=nbc, span=span):
                ld = []
                for u in range(BAND_UNROLL):
                    qs, ks, off = _band_rows(it * BAND_UNROLL + u, d, nbc)
                    q = q_ref[_rows(qs, QB, d), :] * ATTN_SCALE
                    ld.append((qs, q, k_ref[_rows(ks, span, d), :].astype(BF16), v_ref[_rows(ks, span, d), :].astype(BF16),
                               _band_valid(span, off)))
                ss = [[jnp.where(valid, _dot_nt(jnp.where(mh, q, 0.0).astype(BF16), k), NEG) for mh in (m0, jnp.logical_not(m0))]
                      for _, q, k, _, valid in ld]
                ps = []
                for pair in ss:
                    row = []
                    for s in pair:
                        m = jnp.max(s, axis=-1, keepdims=True)
                        p = jnp.exp(s - m)
                        row.append((p.astype(BF16), jnp.sum(p, axis=-1, keepdims=True), m))
                    ps.append(row)
                pv = [[_dot(p, ld[u][3]) for p, _, _ in ps[u]] for u in range(BAND_UNROLL)]
                for u in range(BAND_UNROLL):
                    rows = _rows(ld[u][0], QB, d)
                    (_, l0, mx0), (_, l1, mx1) = ps[u]
                    o_s[pidx, rows, :] = _pair(m0, pv[u][0] / l0, pv[u][1] / l1)
                    l_s[pidx, rows, :] = _pair(m0, mx0 + jnp.log(l0), mx1 + jnp.log(l1))
                return carry

            lax.fori_loop(0, SEQ // QB // BAND_UNROLL, blk, 0)
        for c in range(SEQ // ROWS):
            sl = slice(c * ROWS, (c + 1) * ROWS)
            a, b, e = l_s[0, sl, :], l_s[1, sl, :], l_s[2, sl, :]
            m = jnp.maximum(jnp.maximum(a, b), e)
            L = m + jnp.log(jnp.exp(a - m) + jnp.exp(b - m) + jnp.exp(e - m))
            out_ref[sl, :] = jnp.exp(a - L) * o_s[0, sl, :] + jnp.exp(b - L) * o_s[1, sl, :] + jnp.exp(e - L) * o_s[2, sl, :]
            lse_ref[sl, :] = L

    blk_of = lambda g: pl.BlockSpec((SEQ, LANE), lambda b, hp, g=g: (b, g * NP + hp))
    return pl.pallas_call(
        body, grid=(B, NP), in_specs=[blk_of(0), blk_of(1), blk_of(2)], out_specs=[blk_of(0), blk_of(0)],
        out_shape=[SDS((T, WG), F32), SDS((T, WG), F32)],
        scratch_shapes=[pltpu.VMEM((3, SEQ, LANE), F32), pltpu.VMEM((3, SEQ, LANE), F32)],
        compiler_params=_cp("arbitrary", "arbitrary"), name=name,
    )(pa, pa, pa)


def _pair_rowsum(m0, prod):
    s0 = jnp.sum(jnp.where(m0, prod, 0.0), axis=-1, keepdims=True)
    return _pair(m0, s0, jnp.sum(prod, axis=-1, keepdims=True) - s0)


def band_bwd(pa, do, out, lse, rc, rs1, rs2, name):
    T = pa.shape[0]
    B = T // SEQ
    NP = WG // LANE

    def body(q_ref, k_ref, v_ref, do_ref, out_ref, l_ref, c_ref, s1_ref, s2_ref, dqo_ref, dko_ref, dvo_ref, d_s, dq_ref, dk_ref,
             dv_ref):
        m0 = _head_lanes()
        dq_ref[...] = jnp.zeros_like(dq_ref)
        dk_ref[...] = jnp.zeros_like(dk_ref)
        dv_ref[...] = jnp.zeros_like(dv_ref)
        for c in range(SEQ // ROWS):
            sl = slice(c * ROWS, (c + 1) * ROWS)
            d_s[sl, :] = _pair_rowsum(m0, do_ref[sl, :] * out_ref[sl, :])
        for d, nbc in PATTERNS:
            span = QB if nbc == 1 else 2 * QB

            def blk(it, carry, d=d, nbc=nbc, span=span):
                masks = (m0, jnp.logical_not(m0))
                ld = []
                for u in range(BAND_UNROLL_BWD):
                    qs, ks, off = _band_rows(it * BAND_UNROLL_BWD + u, d, nbc)
                    qrow, krow = _rows(qs, QB, d), _rows(ks, span, d)
                    ld.append(dict(qrow=qrow, krow=krow, q=q_ref[qrow, :] * ATTN_SCALE, k=k_ref[krow, :].astype(BF16),
                                   v=v_ref[krow, :].astype(BF16), do=do_ref[qrow, :], l=l_ref[qrow, :], dv=d_s[qrow, :],
                                   valid=_band_valid(span, off)))
                for t in ld:
                    t["qm"] = [jnp.where(mh, t["q"], 0.0).astype(BF16) for mh in masks]
                    t["dom"] = [jnp.where(mh, t["do"], 0.0).astype(BF16) for mh in masks]
                sd = [[(jnp.where(t["valid"], _dot_nt(t["qm"][h], t["k"]), NEG), _dot_nt(t["dom"][h], t["v"])) for h in range(2)]
                      for t in ld]
                pd = []
                for t, pair in zip(ld, sd):
                    row = []
                    for h, (s, dp) in enumerate(pair):
                        col = slice(h * HD, h * HD + 1)
                        p = jnp.exp(s - t["l"][:, col])
                        row.append((p.astype(BF16), (p * (dp - t["dv"][:, col])).astype(BF16)))
                    pd.append(row)
                gr = [(_dot(row[0][1], t["k"]), _dot(row[1][1], t["k"]),
                       _dot_tn(jnp.concatenate([row[0][1], row[1][1]], axis=0), jnp.concatenate(t["qm"], axis=0)),
                       _dot_tn(jnp.concatenate([row[0][0], row[1][0]], axis=0), jnp.concatenate(t["dom"], axis=0)))
                      for t, row in zip(ld, pd)]
                for t, (dq0, dq1, dk, dv) in zip(ld, gr):
                    dq_ref[t["qrow"], :] += _pair(m0, dq0, dq1) * ATTN_SCALE
                    dk_ref[t["krow"], :] += dk
                    dv_ref[t["krow"], :] += dv
                return carry

            lax.fori_loop(0, SEQ // QB // BAND_UNROLL_BWD, blk, 0)
        for c in range(SEQ // ROWS):
            sl = slice(c * ROWS, (c + 1) * ROWS)
            cc, s1, s2 = c_ref[sl, :], s1_ref[sl, :], s2_ref[sl, :]
            for acc, o_ref in ((dq_ref, dqo_ref), (dk_ref, dko_ref)):
                d = acc[sl, :]
                o_ref[sl, :] = (d * cc + pltpu.roll(d * s1, 8, 1) + pltpu.roll(d * s2, LANE - 8, 1)).astype(BF16)
            dvo_ref[sl, :] = dv_ref[sl, :].astype(BF16)

    blk_of = lambda g: pl.BlockSpec((SEQ, LANE), lambda b, hp, g=g: (b, g * NP + hp))
    tab = pl.BlockSpec((SEQ, LANE), lambda b, hp: (b, 0))
    return pl.pallas_call(
        body, grid=(B, NP), in_specs=[blk_of(0), blk_of(1), blk_of(2), blk_of(0), blk_of(0), blk_of(0), tab, tab, tab],
        out_specs=[blk_of(0)] * 3, out_shape=[SDS((T, WG), BF16)] * 3,
        scratch_shapes=[pltpu.VMEM((SEQ, LANE), F32)] * 4,
        compiler_params=_cp("arbitrary", "arbitrary"), name=name,
    )(pa, pa, pa, do, out, lse, rc, rs1, rs2)


def _tile_causal(nq, nk, q0, k0):
    r = lax.broadcasted_iota(jnp.int32, (nq, nk), 0)
    c = lax.broadcasted_iota(jnp.int32, (nq, nk), 1)
    return r + (q0 - k0) >= c


def _row_to_col(row):
    n = row.shape[1]
    return jnp.transpose(jnp.broadcast_to(row, (LANE, n)))[:, 0:1]


def _col_to_row(col):
    n = col.shape[0]
    return jnp.transpose(jnp.broadcast_to(col, (n, LANE)))[0:1, :]


def fox_fwd(pb, fblk, frow, name, gather=None):
    FQ = FOX_QB
    T = pb.shape[0]
    B = T // SEQ
    NG = WG // (LANE * FOX_PAIRS)
    NHS = 2 * FOX_PAIRS
    W = LANE * FOX_PAIRS
    n = SEQ // FQ
    ng = 0 if gather is None else len(gather[0])
    plan = None if gather is None else ShardGather([w.shape for w in gather[0]], gather[1])

    def body(*refs):
        q_ref, k_ref, v_ref, fc_ref, fr_ref = refs[:5]
        o_ref, lse_ref = refs[5 + ng:7 + ng]
        if plan is not None:
            comm = (refs[5:5 + ng], refs[7 + ng:7 + 2 * ng], refs[7 + 2 * ng:])
            ids = [pl.program_id(ax) for ax in range(3)]
            pl.when((ids[0] == 0) & (ids[1] == 0) & (ids[2] == 0))(lambda: plan.start(*comm))
        i = pl.program_id(2)
        m0 = _head_lanes()
        masks = (m0, jnp.logical_not(m0))
        heads = [(hh, slice((hh // 2) * LANE, (hh // 2 + 1) * LANE), masks[hh % 2]) for hh in range(NHS)]
        qh, fq = [], []
        for hh, lanes, mh in heads:
            q = q_ref[:, lanes] * ATTN_SCALE
            qh.append(jnp.where(mh, q, jnp.zeros_like(q)))
            fq.append(_row_to_col(fc_ref[0, hh, 0]))

        def step(t, carry, masked):
            rows = pl.ds(pl.multiple_of(t * FT, FT), FT)
            ss = [_dot_nt(qh[hh], k_ref[rows, lanes]) + fq[hh] - fr_ref[0, hh, t] for hh, lanes, _ in heads]
            if masked:
                ok = _tile_causal(FQ, FT, i * FQ, t * FT)
                ss = [jnp.where(ok, s, NEG) for s in ss]
            st = []
            for hh, _, _ in heads:
                m2 = jnp.maximum(carry[hh][0], jnp.max(ss[hh], axis=-1, keepdims=True))
                st.append((m2, jnp.exp(carry[hh][0] - m2), jnp.exp(ss[hh] - m2).astype(BF16)))
            pv = []
            for hh, lanes, mh in heads:
                vt = v_ref[rows, lanes]
                pv.append(_dot(st[hh][2], jnp.where(mh, vt, jnp.ones_like(vt))))
            return tuple((st[hh][0], st[hh][1] * carry[hh][1] + pv[hh]) for hh in range(NHS))

        one = (jnp.full((FQ, 1), NEG, F32), jnp.zeros((FQ, LANE), F32))
        last = (i * FQ) // FT
        carry = lax.fori_loop(0, last, lambda t, cr: step(t, cr, False), (one,) * NHS)
        carry = step(last, carry, True)
        for pr in range(FOX_PAIRS):
            (ma, acca), (mb, accb) = carry[2 * pr], carry[2 * pr + 1]
            la, lb = acca[:, HD:HD + 1], accb[:, 0:1]
            o_ref[:, pr * LANE:(pr + 1) * LANE] = _pair(m0, acca / la, accb / lb)
            lse_ref[0, 2 * pr, 0] = _col_to_row(ma + jnp.log(la))
            lse_ref[0, 2 * pr + 1, 0] = _col_to_row(mb + jnp.log(lb))
        if plan is not None:
            pl.when((ids[0] == B - 1) & (ids[1] == NG - 1) & (ids[2] == n - 1))(lambda: plan.finish(*comm))

    qblk = pl.BlockSpec((FQ, W), lambda b, g, i: (b * n + i, g))
    full = lambda grp: pl.BlockSpec((SEQ, W), lambda b, g, i, grp=grp: (b, grp * NG + g))
    rowb = pl.BlockSpec((1, NHS, 1, 1, FQ), lambda b, g, i: (b, g, i, 0, 0))
    outs = pl.pallas_call(
        body, grid=(B, NG, n),
        in_specs=[qblk, full(1), full(2), rowb, pl.BlockSpec((1, NHS, SEQ // FT, 1, FT), lambda b, g, i: (b, g, 0, 0, 0))]
        + [HBM] * ng,
        out_specs=[qblk, rowb] + [HBM] * ng,
        out_shape=[SDS((T, WG), F32), SDS((B, NH, n, 1, FQ), F32)] + ([] if plan is None else plan.out_shapes(BF16)),
        scratch_shapes=[] if plan is None else plan.scratch(),
        compiler_params=_cp("arbitrary", "arbitrary", "arbitrary"), name=name,
    )(pb, pb, pb, fblk, frow, *([] if gather is None else gather[0]))
    return outs[:2], outs[2:]


def fox_bwd(pb, do, lrow, drow, fblk, frow, name):
    T = pb.shape[0]
    B = T // SEQ
    PAIRS = FOX_PAIRS_BWD
    NG = WG // (LANE * PAIRS)
    NHS = 2 * PAIRS
    W = LANE * PAIRS
    n = SEQ // FB

    def body(q_ref, k_ref, v_ref, do_ref, l_ref, d_ref, fc_ref, fr_ref, dqo_ref, dk_ref, dv_ref, dfq_ref, dfk_ref, dq_ref):
        j = pl.program_id(2)
        m0 = _head_lanes()
        masks = (m0, jnp.logical_not(m0))
        heads = [(hh, slice((hh // 2) * LANE, (hh // 2 + 1) * LANE), masks[hh % 2]) for hh in range(NHS)]

        @pl.when(j == 0)
        def _():
            dq_ref[...] = jnp.zeros_like(dq_ref)
            dfq_ref[...] = jnp.zeros_like(dfq_ref)

        kj = [k_ref[:, lanes] for _, lanes, _ in heads]
        vj = [v_ref[:, lanes] for _, lanes, _ in heads]
        fk = [_row_to_col(fc_ref[0, hh, 0]) for hh in range(NHS)]

        def step(t, carry, masked):
            rows = pl.ds(pl.multiple_of(t * FT, FT), FT)
            qm, dom = [], []
            for _, lanes, mh in heads:
                qt = q_ref[rows, lanes] * ATTN_SCALE
                qm.append(jnp.where(mh, qt, jnp.zeros_like(qt)))
                dom.append(jnp.where(mh, do_ref[rows, lanes], 0.0).astype(BF16))
            ss = [_dot_nt(kj[hh], qm[hh]) + fr_ref[0, hh, t] - fk[hh] for hh in range(NHS)]
            dps = [_dot_nt(vj[hh], dom[hh]) for hh in range(NHS)]
            if masked:
                key = lax.broadcasted_iota(jnp.int32, (FB, FT), 0)
                qry = lax.broadcasted_iota(jnp.int32, (FB, FT), 1)
                ok = qry + (t * FT - j * FB) >= key
                ss = [jnp.where(ok, s, NEG) for s in ss]
            pds = []
            for hh in range(NHS):
                p = jnp.exp(ss[hh] - l_ref[0, hh, t])
                ds = p * (dps[hh] - d_ref[0, hh, t])
                dfq_ref[0, hh, t] += jnp.sum(ds, axis=0, keepdims=True)
                pds.append((p.astype(BF16), ds.astype(BF16), jnp.sum(ds, axis=-1, keepdims=True)))
            dks = [_dot(pds[hh][1], qm[hh]) for hh in range(NHS)]
            dvs = [_dot(pds[hh][0], dom[hh]) for hh in range(NHS)]
            dqs = [_dot_tn(pds[hh][1], kj[hh]) for hh in range(NHS)]
            for pr in range(PAIRS):
                dq_ref[rows, pr * LANE:(pr + 1) * LANE] += _pair(m0, dqs[2 * pr], dqs[2 * pr + 1]) * ATTN_SCALE
            return tuple((carry[hh][0] + dks[hh], carry[hh][1] + dvs[hh], carry[hh][2] - pds[hh][2]) for hh in range(NHS))

        one = (jnp.zeros((FB, LANE), F32), jnp.zeros((FB, LANE), F32), jnp.zeros((FB, 1), F32))
        first = (j * FB) // FT
        carry = step(first, (one,) * NHS, True)
        carry = lax.fori_loop(first + 1, SEQ // FT, lambda t, cr: step(t, cr, False), carry)
        for pr in range(PAIRS):
            (dka, dva, dfka), (dkb, dvb, dfkb) = carry[2 * pr], carry[2 * pr + 1]
            dk_ref[:, pr * LANE:(pr + 1) * LANE] = _pair(m0, dka, dkb).astype(BF16)
            dv_ref[:, pr * LANE:(pr + 1) * LANE] = _pair(m0, dva, dvb).astype(BF16)
            dfk_ref[0, 2 * pr, 0] = _col_to_row(dfka)
            dfk_ref[0, 2 * pr + 1, 0] = _col_to_row(dfkb)

        @pl.when(j == n - 1)
        def _():
            dqo_ref[...] = dq_ref[...].astype(BF16)

    kblk = lambda grp: pl.BlockSpec((FB, W), lambda b, g, j, grp=grp: (b * n + j, grp * NG + g))
    full = pl.BlockSpec((SEQ, W), lambda b, g, j: (b, g))
    rowf = pl.BlockSpec((1, NHS, SEQ // FT, 1, FT), lambda b, g, j: (b, g, 0, 0, 0))
    rowb = pl.BlockSpec((1, NHS, 1, 1, FB), lambda b, g, j: (b, g, j, 0, 0))
    return pl.pallas_call(
        body, grid=(B, NG, n), in_specs=[full, kblk(1), kblk(2), full, rowf, rowf, rowb, rowf],
        out_specs=[full, kblk(0), kblk(0), rowf, rowb],
        out_shape=[SDS((T, WG), BF16), SDS((T, WG), BF16), SDS((T, WG), BF16), SDS((B, NH, SEQ // FT, 1, FT), F32),
                   SDS((B, NH, n, 1, FB), F32)],
        scratch_shapes=[pltpu.VMEM((SEQ, W), F32)],
        compiler_params=_cp("arbitrary", "arbitrary", "arbitrary"), name=name,
    )(pb, pb, pb, do, lrow, drow, fblk, frow)


def _tri(lower):
    r = lax.broadcasted_iota(jnp.int32, (LANE, LANE), 0)
    c = lax.broadcasted_iota(jnp.int32, (LANE, LANE), 1)
    return ((r >= c) if lower else (r <= c)).astype(F32)


def _tri_dot(t, xblk):
    return jnp.dot(t, xblk, precision=lax.Precision.HIGHEST, preferred_element_type=F32)


def forget_cumsum(flog, bias, name):
    B, S, _ = flog.shape

    def body(f_ref, b_ref, o_ref):
        t = _tri(True)
        carry = jnp.zeros((1, LANE), F32)
        for blk in range(S // LANE):
            z = f_ref[0, blk * LANE:(blk + 1) * LANE, :] + b_ref[...]
            lf = jnp.minimum(z, 0.0) - jnp.log(1.0 + jnp.exp(-jnp.abs(z)))
            cs = _tri_dot(t, lf) + carry
            o_ref[0, blk * LANE:(blk + 1) * LANE, :] = cs
            carry = cs[LANE - 1:LANE, :]

    spec = pl.BlockSpec((1, S, LANE), lambda b: (b, 0, 0))
    return pl.pallas_call(
        body, grid=(B,), in_specs=[spec, pl.BlockSpec((1, LANE), lambda b: (0, 0))], out_specs=spec,
        out_shape=SDS((B, S, LANE), F32), compiler_params=_cp("arbitrary"), name=name,
    )(flog, bias)


def forget_cumsum_bwd(dF, flog, bias, name):
    B, S, _ = flog.shape

    def body(d_ref, f_ref, b_ref, o_ref, db_ref):
        @pl.when(pl.program_id(0) == 0)
        def _():
            db_ref[...] = jnp.zeros_like(db_ref)

        t = _tri(False)
        carry = jnp.zeros((1, LANE), F32)
        tot = jnp.zeros((1, LANE), F32)
        for blk in reversed(range(S // LANE)):
            sl = slice(blk * LANE, (blk + 1) * LANE)
            rc = _tri_dot(t, d_ref[0, sl, :]) + carry
            carry = rc[0:1, :]
            z = f_ref[0, sl, :] + b_ref[...]
            dz = rc * jax.nn.sigmoid(-z)
            o_ref[0, sl, :] = dz
            tot = tot + jnp.sum(dz, axis=0, keepdims=True)
        db_ref[...] += tot

    spec = pl.BlockSpec((1, S, LANE), lambda b: (b, 0, 0))
    vec = pl.BlockSpec((1, LANE), lambda b: (0, 0))
    return pl.pallas_call(
        body, grid=(B,), in_specs=[spec, spec, vec], out_specs=[spec, vec],
        out_shape=[SDS((B, S, LANE), F32), SDS((1, LANE), F32)], compiler_params=_cp("arbitrary"), name=name,
    )(dF, flog, bias)


def mixer_out_fwd(oa, ob, goa, gob, w_out, g_post, x, mod3, name):
    T = x.shape[0]

    def body(oa_ref, ob_ref, goa_ref, gob_ref, w_ref, gp_ref, x_ref, mod_ref, xo_ref, mg_ref, y0_ref):
        a = oa_ref[...]
        b = ob_ref[...]
        mg = jnp.concatenate([a * _rms(a) * goa_ref[...], b * _rms(b) * gob_ref[...]], axis=-1).astype(BF16)
        mg_ref[...] = mg
        y0 = _dot(mg, w_ref[...])
        y0_ref[...] = y0
        xo_ref[...] = x_ref[...] + mod_ref[0, 2:3, :] * (y0 * _rms(y0) * gp_ref[...])

    tok = pl.BlockSpec((TM, D), lambda i: (i, 0))
    half = pl.BlockSpec((TM, WG), lambda i: (i, 0))
    hv = pl.BlockSpec((1, WG), lambda i: (0, 0))
    return pl.pallas_call(
        body, grid=(T // TM,),
        in_specs=[half, half, hv, hv, pl.BlockSpec((D, D), lambda i: (0, 0)), pl.BlockSpec((1, D), lambda i: (0, 0)), tok,
                  pl.BlockSpec((1, 3, D), _mod_map)],
        out_specs=[tok, tok, tok], out_shape=[SDS((T, D), F32), SDS((T, D), BF16), SDS((T, D), F32)],
        compiler_params=_cp("arbitrary"), name=name,
    )(oa, ob, goa, gob, w_out, g_post, x, mod3)


def mixer_out_bwd(dxo, y0, mod3, g_post, w_out, oa, ob, goa, gob, name):
    T = dxo.shape[0]
    nb = T // SEQ
    tiles_per_seq = SEQ // TM

    def body(dxo_ref, y0_ref, mod_ref, gp_ref, w_ref, oa_ref, ob_ref, goa_ref, gob_ref,
             dy0_ref, doa_ref, dob_ref, dmg_ref, dgp_ref, dgoa_ref, dgob_ref, dvb_ref):
        i = pl.program_id(0)

        @pl.when(i == 0)
        def _():
            dgp_ref[...] = jnp.zeros_like(dgp_ref)
            dgoa_ref[...] = jnp.zeros_like(dgoa_ref)
            dgob_ref[...] = jnp.zeros_like(dgob_ref)

        @pl.when(i % tiles_per_seq == 0)
        def _():
            dmg_ref[...] = jnp.zeros_like(dmg_ref)

        dy0, dmg, dg = _post_bwd(dxo_ref[...], y0_ref[...], gp_ref[...], mod_ref[0, 2:3, :], 1.0)
        dmg_ref[0] += dmg
        dgp_ref[...] += dg
        db = dy0.astype(BF16)
        dy0_ref[...] = db
        dm = _dot_nt(db, w_ref[...])
        for o_ref, g_ref, do_ref, dg_ref, sl in ((oa_ref, goa_ref, doa_ref, dgoa_ref, slice(0, WG)),
                                                  (ob_ref, gob_ref, dob_ref, dgob_ref, slice(WG, 2 * WG))):
            o = o_ref[...]
            r = _rms(o)
            oh = o * r
            d = dm[:, sl]
            dg_ref[...] += jnp.sum(d * oh, axis=0, keepdims=True)
            dh = d * g_ref[...]
            do = r * (dh - oh * jnp.mean(dh * oh, axis=-1, keepdims=True))
            do_ref[...] = do
        ind = (lax.broadcasted_iota(jnp.int32, (WG, LANE), 0) // HD == lax.broadcasted_iota(jnp.int32, (WG, LANE), 1)).astype(BF16)
        prod = do * o
        hi = prod.astype(BF16)
        dvb_ref[...] = _dot(hi, ind) + _dot((prod - hi.astype(F32)).astype(BF16), ind)

    tok = pl.BlockSpec((TM, D), lambda i: (i, 0))
    half = pl.BlockSpec((TM, WG), lambda i: (i, 0))
    hv = pl.BlockSpec((1, WG), lambda i: (0, 0))
    vec = pl.BlockSpec((1, D), lambda i: (0, 0))
    return pl.pallas_call(
        body, grid=(T // TM,),
        in_specs=[tok, tok, pl.BlockSpec((1, 3, D), _mod_map), vec, pl.BlockSpec((D, D), lambda i: (0, 0)), half, half, hv, hv],
        out_specs=[tok, half, half, pl.BlockSpec((1, 1, D), _mod_map), vec, hv, hv, pl.BlockSpec((TM, LANE), lambda i: (i, 0))],
        out_shape=[SDS((T, D), BF16), SDS((T, WG), F32), SDS((T, WG), F32), SDS((nb, 1, D), F32), SDS((1, D), F32),
                   SDS((1, WG), F32), SDS((1, WG), F32), SDS((T, LANE), F32)],
        compiler_params=_cp("arbitrary"), name=name,
    )(dxo, y0, mod3, g_post, w_out, oa, ob, goa, gob)


def mixer_proj_bwd(dps, dflog, dxo, x, mod3, g_pre, w_main, w_f, name):
    T = x.shape[0]
    nb = T // SEQ
    tiles_per_seq = SEQ // TM
    ngrp = len(dps)

    def body(*refs):
        dp_refs = refs[:ngrp]
        df_ref, dxo_ref, x_ref, mod_ref, g_ref, w_ref, wf_ref, dx_ref, dmod_ref, dg_ref = refs[ngrp:]
        i = pl.program_id(0)

        @pl.when(i == 0)
        def _():
            dg_ref[...] = jnp.zeros_like(dg_ref)

        @pl.when(i % tiles_per_seq == 0)
        def _():
            dmod_ref[...] = jnp.zeros_like(dmod_ref)

        dh = _dot_nt(df_ref[...].astype(BF16), wf_ref[...])
        for g, dp_ref in enumerate(dp_refs):
            dh = dh + _dot_nt(dp_ref[...], w_ref[:, g * WG:(g + 1) * WG])
        dx, dsh, dsc, dg = _norm_mod_bwd(dh, x_ref[...], g_ref[...], mod_ref[0, 1:2, :])
        dx_ref[...] = dxo_ref[...] + dx
        dmod_ref[0, 0:1, :] += dsh
        dmod_ref[0, 1:2, :] += dsc
        dg_ref[...] += dg

    tok = pl.BlockSpec((TM, D), lambda i: (i, 0))
    vec = pl.BlockSpec((1, D), lambda i: (0, 0))
    return pl.pallas_call(
        body, grid=(T // TM,),
        in_specs=[pl.BlockSpec((TM, WG), lambda i: (i, 0))] * ngrp
        + [pl.BlockSpec((TM, LANE), lambda i: (i, 0)), tok, tok, pl.BlockSpec((1, 3, D), _mod_map), vec,
           pl.BlockSpec((D, IN_MAIN), lambda i: (0, 0)), pl.BlockSpec((D, LANE), lambda i: (0, 0))],
        out_specs=[tok, pl.BlockSpec((1, 2, D), _mod_map), vec],
        out_shape=[SDS((T, D), F32), SDS((nb, 2, D), F32), SDS((1, D), F32)],
        compiler_params=_cp("arbitrary"), name=name,
    )(*dps, dflog, dxo, x, mod3, g_pre, w_main, w_f)


def ada_fwd(c_all, w, b, name):
    n = w.shape[1]
    tn = n // 2

    def body(c_ref, w_ref, b_ref, o_ref):
        cv = c_ref[...]
        o_ref[...] = _dot((cv * jax.nn.sigmoid(cv)).astype(BF16), w_ref[...].astype(BF16)) + b_ref[...]

    R = c_all.shape[0]
    return pl.pallas_call(
        body, grid=(2,),
        in_specs=[pl.BlockSpec((R, D), lambda j: (0, 0)), pl.BlockSpec((D, tn), lambda j: (0, j)), pl.BlockSpec((1, tn), lambda j: (0, j))],
        out_specs=pl.BlockSpec((R, tn), lambda j: (0, j)), out_shape=SDS((R, n), F32),
        compiler_params=_cp("arbitrary"), name=name,
    )(c_all, w, b)


def ada_bwd(c_all, dmod, name):
    R, n = dmod.shape
    tn = n // 2

    def body(c_ref, d_ref, o_ref):
        cv = c_ref[...]
        o_ref[...] = _dot_tn((cv * jax.nn.sigmoid(cv)).astype(BF16), d_ref[...].astype(BF16))

    return pl.pallas_call(
        body, grid=(2,), in_specs=[pl.BlockSpec((R, D), lambda j: (0, 0)), pl.BlockSpec((R, tn), lambda j: (0, j))],
        out_specs=pl.BlockSpec((D, tn), lambda j: (0, j)), out_shape=SDS((D, n), F32),
        compiler_params=_cp("arbitrary"), name=name,
    )(c_all, dmod)


def _adam_math(w, g, m, v):
    m2 = ADAM_B1 * m + (1.0 - ADAM_B1) * g
    v2 = ADAM_B2 * v + (1.0 - ADAM_B2) * (g * g)
    m_hat = m2 / (1.0 - ADAM_B1 ** ADAM_STEP)
    v_hat = v2 / (1.0 - ADAM_B2 ** ADAM_STEP)
    delta = -ADAM_LR * (m_hat / (jnp.sqrt(v_hat) + ADAM_EPS) + ADAM_WD * w)
    return delta, m2, v2


def adam_update(w, g, m, v, tr, name):
    _, R, C = w.shape

    def body(w_ref, g_ref, m_ref, v_ref, d_ref, mo_ref, vo_ref):
        d_ref[0], mo_ref[0], vo_ref[0] = _adam_math(w_ref[0], g_ref[...], m_ref[0], v_ref[0])

    spec = pl.BlockSpec((1, tr, C), lambda i: (0, i, 0))
    gspec = pl.BlockSpec((tr, C), lambda i: (i, 0))
    return pl.pallas_call(
        body, grid=(R // tr,), in_specs=[spec, gspec, spec, spec], out_specs=[spec] * 3, out_shape=[SDS((1, R, C), F32)] * 3,
        compiler_params=_cp("arbitrary"), name=name,
    )(w, g, m, v)


def adam_update_halves(w, mine, other, m, v, cidx, tr, name):
    _, R, C = w.shape
    nh = R // 2 // tr

    def body(c_ref, w_ref, a_ref, b_ref, m_ref, v_ref, g_ref, d_ref, mo_ref, vo_ref):
        first_half = pl.program_id(0) < nh
        g = jnp.where(first_half == (c_ref[0] == 0), a_ref[...], b_ref[...])
        g_ref[0] = g
        d_ref[0], mo_ref[0], vo_ref[0] = _adam_math(w_ref[0], g, m_ref[0], v_ref[0])

    spec = pl.BlockSpec((1, tr, C), lambda i, c_ref: (0, i, 0))
    hspec = pl.BlockSpec((tr, C), lambda i, c_ref: (i % nh, 0))
    return pl.pallas_call(
        body,
        grid_spec=pltpu.PrefetchScalarGridSpec(num_scalar_prefetch=1, grid=(R // tr,), in_specs=[spec, hspec, hspec, spec, spec],
                                               out_specs=[spec] * 4),
        out_shape=[SDS((1, R, C), F32)] * 4, compiler_params=_cp("arbitrary"), name=name,
    )(cidx, w, mine, other, m, v)


def vec_adam(parts, w, m, v, name):
    P, C = parts.shape

    def body(p_ref, w_ref, m_ref, v_ref, g_ref, d_ref, mo_ref, vo_ref):
        g = jnp.sum(p_ref[...], axis=0, keepdims=True)
        g_ref[...] = g
        d_ref[...], mo_ref[...], vo_ref[...] = _adam_math(w_ref[...], g, m_ref[...], v_ref[...])

    return pl.pallas_call(body, out_shape=[SDS((1, C), F32)] * 4, compiler_params=_cp(), name=name)(parts, w, m, v)


def small_adam(parts, layout, ws, ms, vs, name):
    P, C = parts.shape
    k = len(layout)

    def body(*refs):
        p_ref = refs[0]
        w_refs, m_refs, v_refs = refs[1:1 + k], refs[1 + k:1 + 2 * k], refs[1 + 2 * k:1 + 3 * k]
        outs = refs[1 + 3 * k:]
        g_all = jnp.sum(p_ref[...], axis=0, keepdims=True)
        outs[4 * k][...] = g_all
        for n, (off, width) in enumerate(layout):
            g = g_all[:, off:off + width]
            outs[4 * n][...] = g
            outs[4 * n + 1][...], outs[4 * n + 2][...], outs[4 * n + 3][...] = _adam_math(
                w_refs[n][...], g, m_refs[n][...], v_refs[n][...])

    shapes = [SDS((1, width), F32) for _, width in layout for _ in range(4)] + [SDS((1, C), F32)]
    res = pl.pallas_call(body, out_shape=shapes, compiler_params=_cp(), name=name)(parts, *ws, *ms, *vs)
    return [tuple(res[4 * n:4 * n + 4]) for n in range(k)], res[4 * k]


HBM = pl.BlockSpec(memory_space=pltpu.HBM)
VMEM = pl.BlockSpec(memory_space=pltpu.VMEM)


def _place():
    x, y, c = lax.axis_index("x"), lax.axis_index("y"), lax.axis_index("c")
    return x, y, c, [(1 - x, y), (x, 1 - y), (1 - x, 1 - y)]


def all_gather8(xs, name):
    R, C = xs.shape

    def body(x_ref, out_ref, send_sems, recv_sems, local_sem):
        x, y, c, chips = _place()
        me, sibling = (x, y, c), (x, y, 1 - c)

        def slot(px, py, pc):
            return out_ref.at[4 * px + 2 * py + pc]

        def copy(k, block, to, src=None):
            return pltpu.make_async_remote_copy(
                src_ref=slot(*block) if src is None else src, dst_ref=slot(*block),
                send_sem=send_sems.at[k], recv_sem=recv_sems.at[k], device_id=to, device_id_type=MESH)

        mine = pltpu.make_async_copy(x_ref, slot(*me), local_sem)
        mine.start()
        first = [copy(0, me, sibling, src=x_ref)]
        first += [copy(1 + j, me, (*chip, c), src=x_ref) for j, chip in enumerate(chips)]
        for cp in first:
            cp.start()
        passed = [copy(4 + j, (*chip, c), sibling) for j, chip in enumerate(chips)]
        for j, chip in enumerate(chips):
            copy(1 + j, (*chip, c), me).wait_recv()
            passed[j].start()
        copy(0, sibling, me).wait_recv()
        for j, chip in enumerate(chips):
            copy(4 + j, (*chip, 1 - c), me).wait_recv()
        for cp in first + passed:
            cp.wait_send()
        mine.wait()

    return pl.pallas_call(
        body, out_shape=SDS((N_DEV, R, C), xs.dtype), in_specs=[VMEM], out_specs=VMEM,
        scratch_shapes=[pltpu.SemaphoreType.DMA((7,)), pltpu.SemaphoreType.DMA((7,)), pltpu.SemaphoreType.DMA],
        compiler_params=pltpu.CompilerParams(vmem_limit_bytes=VMEM_LIMIT), name=name,
    )(xs)


class ShardGather:
    def __init__(self, shapes, splits):
        self.shapes, self.splits, self.n = shapes, splits, len(shapes)

    def scratch(self):
        n = self.n
        return [pltpu.SemaphoreType.DMA((n, 6)), pltpu.SemaphoreType.DMA((n, 6)), pltpu.SemaphoreType.DMA((n,))]

    def out_shapes(self, dtype):
        return [SDS((N_SHARD,) + tuple(s), dtype) for s in self.shapes]

    def _half(self, ref, k, cc):
        lo, hi = (0, self.splits[k]) if cc == 0 else (self.splits[k], self.shapes[k][0])
        return ref.at[pl.ds(lo, hi - lo)]

    def _phase(self, w_refs, o_refs, sems, finish):
        send_sems, recv_sems, local_sems = sems
        x, y, c, chips = _place()
        sibling = (x, y, 1 - c)
        me_s = 2 * x + y

        def rcopy(src, dst, k, s, to):
            return pltpu.make_async_remote_copy(src_ref=src, dst_ref=dst, send_sem=send_sems.at[k, s],
                                                recv_sem=recv_sems.at[k, s], device_id=to, device_id_type=MESH)

        for cc in (0, 1):
            @pl.when(c == cc)
            def _():
                local = [pltpu.make_async_copy(w_refs[k], o_refs[k].at[me_s], local_sems.at[k]) for k in range(self.n)]
                first = [rcopy(self._half(w_refs[k], k, cc), self._half(o_refs[k].at[me_s], k, cc), k, j, (*chip, c))
                         for k in range(self.n) for j, chip in enumerate(chips)]
                if not finish:
                    for cp in local + first:
                        cp.start()
                    return
                passed = []
                for k in range(self.n):
                    for j, chip in enumerate(chips):
                        land = self._half(o_refs[k].at[2 * chip[0] + chip[1]], k, cc)
                        rcopy(land, land, k, j, (*chip, c)).wait_recv()
                        f = rcopy(land, land, k, 3 + j, sibling)
                        f.start()
                        passed.append(f)
                for k in range(self.n):
                    for j, chip in enumerate(chips):
                        other = self._half(o_refs[k].at[2 * chip[0] + chip[1]], k, 1 - cc)
                        rcopy(other, other, k, 3 + j, sibling).wait_recv()
                for s in first + passed:
                    s.wait_send()
                for cp in local:
                    cp.wait()

    def start(self, w_refs, o_refs, sems):
        self._phase(w_refs, o_refs, sems, False)

    def finish(self, w_refs, o_refs, sems):
        self._phase(w_refs, o_refs, sems, True)


def all_gather_shards(ws, splits, name):
    n = len(ws)
    plan = ShardGather([w.shape for w in ws], splits)

    def body(*refs):
        plan.start(refs[:n], refs[n:2 * n], refs[2 * n:])
        plan.finish(refs[:n], refs[n:2 * n], refs[2 * n:])

    return pl.pallas_call(
        body, out_shape=plan.out_shapes(ws[0].dtype), in_specs=[HBM] * n, out_specs=[HBM] * n,
        scratch_shapes=plan.scratch(), name=name,
    )(*ws)


def sibling_send_half(gs, name):
    n = len(gs)

    def body(*refs):
        g_refs, o_refs = refs[:n], refs[n:2 * n]
        send_sems, recv_sems = refs[2 * n:]
        x, y, c, _ = _place()
        cps = []
        for k in range(n):
            hr = gs[k].shape[1] // 2
            src = g_refs[k].at[:, pl.ds(pl.multiple_of((1 - c) * hr, 8), hr)]
            cp = pltpu.make_async_remote_copy(src_ref=src, dst_ref=o_refs[k], send_sem=send_sems.at[k], recv_sem=recv_sems.at[k],
                                              device_id=(x, y, 1 - c), device_id_type=MESH)
            cp.start()
            cps.append(cp)
        for cp in cps:
            cp.wait()

    return pl.pallas_call(
        body, out_shape=[SDS((N_SHARD, g.shape[1] // 2, g.shape[2]), g.dtype) for g in gs], in_specs=[HBM] * n, out_specs=[HBM] * n,
        scratch_shapes=[pltpu.SemaphoreType.DMA((n,)), pltpu.SemaphoreType.DMA((n,))], name=name,
    )(*gs)


def _scatter_copies(h_refs, o_refs, send_sems, recv_sems):
    _, _, c, chips = _place()
    return [pltpu.make_async_remote_copy(
        src_ref=h_refs[k].at[2 * chip[0] + chip[1]], dst_ref=o_refs[k].at[j], send_sem=send_sems.at[k, j],
        recv_sem=recv_sems.at[k, j], device_id=(*chip, c), device_id_type=MESH)
        for k in range(len(h_refs)) for j, chip in enumerate(chips)]


def chip_scatter(hs, name):
    n = len(hs)

    def body(*refs):
        cps = _scatter_copies(refs[:n], refs[n:2 * n], *refs[2 * n:])
        for cp in cps:
            cp.start()
        for cp in cps:
            cp.wait()

    return pl.pallas_call(
        body, out_shape=[SDS((3,) + h.shape[1:], h.dtype) for h in hs], in_specs=[HBM] * n, out_specs=[HBM] * n,
        scratch_shapes=[pltpu.SemaphoreType.DMA((n, 3)), pltpu.SemaphoreType.DMA((n, 3))], name=name,
    )(*hs)


def sibling_swap(ghs, name):
    n = len(ghs)

    def body(*refs):
        g_refs, o_refs = refs[:n], refs[n:2 * n]
        send_sems, recv_sems = refs[2 * n:]
        x, y, c, _ = _place()
        cps = []
        for k in range(n):
            cp = pltpu.make_async_remote_copy(src_ref=g_refs[k], dst_ref=o_refs[k], send_sem=send_sems.at[k],
                                              recv_sem=recv_sems.at[k], device_id=(x, y, 1 - c), device_id_type=MESH)
            cp.start()
            cps.append(cp)
        for cp in cps:
            cp.wait()

    return pl.pallas_call(
        body, out_shape=[SDS(g.shape, g.dtype) for g in ghs], in_specs=[HBM] * n, out_specs=[HBM] * n,
        scratch_shapes=[pltpu.SemaphoreType.DMA((n,)), pltpu.SemaphoreType.DMA((n,))], name=name,
    )(*ghs)


def pair_sums(gs, ras, cidx, name):
    n = len(gs)
    halves = [(g.shape[1] // 2, g.shape[2]) for g in gs]

    def body(c_ref, *refs):
        for g_ref, a_ref, o_ref in zip(refs[:n], refs[n:2 * n], refs[2 * n:]):
            o_ref[...] = (g_ref[...] + a_ref[...]).astype(BF16)

    mine = [pl.BlockSpec((1, hr, cols), lambda s, c_ref: (s, c_ref[0], 0)) for hr, cols in halves]
    whole = [pl.BlockSpec((1, hr, cols), lambda s, c_ref: (s, 0, 0)) for hr, cols in halves]
    return pl.pallas_call(
        body,
        grid_spec=pltpu.PrefetchScalarGridSpec(num_scalar_prefetch=1, grid=(N_SHARD,), in_specs=mine + whole, out_specs=whole),
        out_shape=[SDS((N_SHARD, hr, cols), BF16) for hr, cols in halves], compiler_params=_cp("arbitrary"), name=name,
    )(cidx, *gs, *ras)


def chip_sums_total(hs, rbs, sidx, name):
    n = len(hs)
    halves = [h.shape[1:] for h in hs]

    def body(s_ref, *refs):
        for h_ref, r_ref, o_ref in zip(refs[:n], refs[n:2 * n], refs[2 * n:]):
            o_ref[...] = ((h_ref[0].astype(F32) + r_ref[0].astype(F32)) + r_ref[1].astype(F32)) + r_ref[2].astype(F32)

    return pl.pallas_call(
        body,
        grid_spec=pltpu.PrefetchScalarGridSpec(
            num_scalar_prefetch=1, grid=(1,),
            in_specs=[pl.BlockSpec((1, hr, cols), lambda i, s_ref: (s_ref[0], 0, 0)) for hr, cols in halves]
            + [pl.BlockSpec((3, hr, cols), lambda i, s_ref: (0, 0, 0)) for hr, cols in halves],
            out_specs=[pl.BlockSpec((hr, cols), lambda i, s_ref: (0, 0)) for hr, cols in halves]),
        out_shape=[SDS((hr, cols), F32) for hr, cols in halves], compiler_params=_cp("arbitrary"), name=name,
    )(sidx, *hs, *rbs)


def _shard_cols(g, n_valid):
    r = g.shape[0]
    return g[:, :n_valid].reshape(r, N_SHARD, n_valid // N_SHARD).transpose(1, 0, 2)


def _unshard_cols(o, pad_to):
    _, r, n = o.shape
    full = o.transpose(1, 0, 2).reshape(r, N_SHARD * n)
    return jnp.pad(full, ((0, 0), (0, pad_to - N_SHARD * n)))


def _rows_of_tiles(t):
    B, H, S = t.shape
    return t.reshape(B, H, S // FT, 1, FT)


def mixer_fwd(x1, mod3, g_pre, w_main, w_f, b_forget_pad, goa, gob, w_out, g_post, tabs, nb, gather=None):
    hmix, pa, pb, flog = mixer_proj(x1, mod3, g_pre, w_main, w_f, *tabs, name="mixer_proj")
    out_a, lse_a = band_fwd(pa, name="band_fwd")
    F = forget_cumsum(flog.reshape(nb, SEQ, LANE), b_forget_pad, name="forget_cumsum")
    Fh = F[:, :, :NH].transpose(0, 2, 1)
    fblk = Fh.reshape(nb, NH, SEQ // FB, 1, FB)
    frow = _rows_of_tiles(Fh)
    (out_b, lse_b), gathered = fox_fwd(pb, Fh.reshape(nb, NH, SEQ // FOX_QB, 1, FOX_QB), frow, name="fox_fwd", gather=gather)
    x2, merged, y0m = mixer_out_fwd(out_a, out_b, goa, gob, w_out, g_post, x1, mod3, name="mixer_out_fwd")
    res = dict(hmix=hmix, flog=flog, pa=pa, pb=pb, out_a=out_a, lse_a=lse_a, fblk=fblk, frow=frow, out_b=out_b,
               lrow=_rows_of_tiles(lse_b.reshape(nb, NH, SEQ)), merged=merged, y0m=y0m)
    return x2, res, gathered


def mixer_bwd(dx2, x1, mod3, g_pre, w_main, w_f, b_forget_pad, goa, gob, w_out, g_post, tabs, res, nb):
    T = nb * SEQ
    dy0m, doa, dob, dmgate, dg_post, dgoa, dgob, dvec_b = mixer_out_bwd(
        dx2, res["y0m"], mod3, g_post, w_out, res["out_a"], res["out_b"], goa, gob, name="mixer_out_bwd")
    dqa, dka, dva = band_bwd(res["pa"], doa, res["out_a"], res["lse_a"], *tabs, name="band_bwd")
    drow = _rows_of_tiles(dvec_b[:, :NH].reshape(nb, SEQ, NH).transpose(0, 2, 1))
    dqb, dkb, dvb, dfq, dfk = fox_bwd(res["pb"], dob, res["lrow"], drow, res["fblk"], res["frow"], name="fox_bwd")
    dF = (dfq.reshape(nb, NH, SEQ) + dfk.reshape(nb, NH, SEQ)).transpose(0, 2, 1)
    dF = jnp.pad(dF, ((0, 0), (0, 0), (0, LANE - NH)))
    dflog, dbf = forget_cumsum_bwd(dF, res["flog"].reshape(nb, SEQ, LANE), b_forget_pad, name="forget_cumsum_bwd")
    dflog = dflog.reshape(T, LANE)
    dps = (dqa, dka, dva, dqb, dkb, dvb)
    dx1, dmod2, dg_pre = mixer_proj_bwd(dps, dflog, dx2, x1, mod3, g_pre, w_main, w_f, name="mixer_proj_bwd")
    g_main = matmul_tn_cols(res["hmix"], dps, 1024, name="grad_w_in")
    g_f = matmul_tn(res["hmix"], dflog.astype(BF16), D, LANE, 1024, name="grad_w_forget")
    g_out = matmul_tn(res["merged"], dy0m, D, D, 1024, name="grad_w_out")
    dmod3 = jnp.concatenate([dmod2, dmgate], axis=1)
    return dx1, dmod3, dict(g_pre=dg_pre, g_post=dg_post, goa=dgoa, gob=dgob, b_forget=dbf[:, :NH],
                            w_in=jnp.concatenate([g_main, g_f[:, :NH]], axis=1), w_out=g_out)


def ffn_grads(h, dy0, act, dgate, dup, pre, reduce=None):
    g_gate = matmul_tn(h, dgate, D, DFF_PAD, 1024, name=pre + "_grad_gate")
    if reduce is None:
        g_up = matmul_tn(h, dup, D, DFF_PAD, 1024, name=pre + "_grad_up")
        g_down = matmul_tn(act, dy0, FF_TN, D, 1024, name=pre + "_grad_down")
        return (g_gate, g_up, g_down), {}
    hs_gate = reduce("gate", g_gate)
    g_up, rb_gate = matmul_tn(h, dup, D, DFF_PAD, 1024, name=pre + "_grad_up", scatter=hs_gate)
    hs_up = reduce("up", g_up)
    g_down, rb_up = matmul_tn(act, dy0, FF_TN, D, 1024, name=pre + "_grad_down", scatter=hs_up)
    return (g_gate, g_up, g_down), {"gate": (hs_gate[0], rb_gate[0]), "up": (hs_up[0], rb_up[0])}


def local_step(x0, tgt, pos_col, mod, wfull, p, late_weights=None, last_weights=None, early_grads=None, last_reduce=None):
    T = x0.shape[0]
    nb = T // SEQ
    mod_ff1, mod_mix, mod_ff2 = mod[:, 0:3], mod[:, 3:6], mod[:, 6:9]
    tabs = rope_tables(pos_col, name="rope_tables")
    bf_pad = jnp.pad(p["b_forget"], ((0, 0), (0, LANE - NH)))

    (x1, h1, gate1, up1, y01), gathered = ffn_fwd(
        x0, mod_ff1, p["g_pre_ff1"], p["g_post_ff1"], wfull["w_ff1_gate"], wfull["w_ff1_up"], wfull["w_ff1_down"], 0.5,
        name="ff1_fwd", gather=None if late_weights is None else late_weights[:2])
    if late_weights is not None:
        wfull = {**wfull, **late_weights[2](gathered)}
    x2, res, gathered = mixer_fwd(x1, mod_mix, p["g_pre_mix"], wfull["w_main"], wfull["w_f"], bf_pad, p["g_out_a"],
                                  p["g_out_b"], wfull["w_out"], p["g_post_mix"], tabs, nb,
                                  gather=None if last_weights is None else last_weights[:2])
    if last_weights is not None:
        wfull = {**wfull, **last_weights[2](gathered)}
    (x3, h2, gate2, up2, y02), _ = ffn_fwd(x2, mod_ff2, p["g_pre_ff2"], p["g_post_ff2"], wfull["w_ff2_gate"],
                                           wfull["w_ff2_up"], wfull["w_ff2_down"], 0.5, name="ff2_fwd")

    (dx2, dy02, act2, dgate2, dup2, dmod_ff2, dgpre2, dgpost2), (loss_part,) = ffn_bwd(
        x3, x2, y02, mod_ff2, p["g_pre_ff2"], p["g_post_ff2"], gate2, up2, wfull["w_ff2_gate"], wfull["w_ff2_up"],
        wfull["w_ff2_down"], 0.5, name="ff2_bwd", target=tgt)
    gw = {}
    (gw["w_ff2_gate"], gw["w_ff2_up"], gw["w_ff2_down"]), _ = ffn_grads(h2, dy02, act2, dgate2, dup2, "ff2")
    dx1, dmod_mix, gmix = mixer_bwd(dx2, x1, mod_mix, p["g_pre_mix"], wfull["w_main"], wfull["w_f"], bf_pad, p["g_out_a"],
                                    p["g_out_b"], wfull["w_out"], p["g_post_mix"], tabs, res, nb)
    gw["w_in"], gw["w_out"] = gmix["w_in"], gmix["w_out"]
    (dx0, dy01, act1, dgate1, dup1, dmod_ff1, dgpre1, dgpost1), scattered = ffn_bwd(
        dx1, x0, y01, mod_ff1, p["g_pre_ff1"], p["g_post_ff1"], gate1, up1, wfull["w_ff1_gate"], wfull["w_ff1_up"],
        wfull["w_ff1_down"], 0.5, name="ff1_bwd", scatter=None if early_grads is None else early_grads(gw))
    (gw["w_ff1_gate"], gw["w_ff1_up"], gw["w_ff1_down"]), chained = ffn_grads(h1, dy01, act1, dgate1, dup1, "ff1", last_reduce)
    dmod = jnp.concatenate([dmod_ff1, dmod_mix, dmod_ff2], axis=1).reshape(nb, 9 * D)
    small = dict(g_pre_ff1=dgpre1, g_post_ff1=dgpost1, g_pre_mix=gmix["g_pre"], g_post_mix=gmix["g_post"], g_pre_ff2=dgpre2,
                 g_post_ff2=dgpost2, g_out_a=gmix["goa"], g_out_b=gmix["gob"], b_forget=gmix["b_forget"])
    return loss_part, dx0, dmod, gw, small, scattered, chained


def kernel(x, c, positions, w_ada, b_ada, g_pre_ff1, g_post_ff1, w_ff1_gate, w_ff1_up, w_ff1_down, g_pre_mix, g_post_mix, w_in, b_forget, g_out_a, g_out_b, w_out, g_pre_ff2, g_post_ff2, w_ff2_gate, w_ff2_up, w_ff2_down, loss_target, m_w_ada, m_b_ada, m_g_pre_ff1, m_g_post_ff1, m_w_ff1_gate, m_w_ff1_up, m_w_ff1_down, m_g_pre_mix, m_g_post_mix, m_w_in, m_b_forget, m_g_out_a, m_g_out_b, m_w_out, m_g_pre_ff2, m_g_post_ff2, m_w_ff2_gate, m_w_ff2_up, m_w_ff2_down, v_w_ada, v_b_ada, v_g_pre_ff1, v_g_post_ff1, v_w_ff1_gate, v_w_ff1_up, v_w_ff1_down, v_g_pre_mix, v_g_post_mix, v_w_in, v_b_forget, v_g_out_a, v_g_out_b, v_w_out, v_g_pre_ff2, v_g_post_ff2, v_w_ff2_gate, v_w_ff2_up, v_w_ff2_down):
    args = dict(locals())
    nb = x.shape[0]
    T = nb * SEQ
    ax, ay, ac = lax.axis_index("x"), lax.axis_index("y"), lax.axis_index("c")
    shard = 2 * ax + ay
    cidx = jnp.reshape(ac, (1,)).astype(jnp.int32)
    sidx = jnp.reshape(shard, (1,)).astype(jnp.int32)

    big = ["w_ff1_gate", "w_ff1_up", "w_ff1_down", "w_in", "w_out", "w_ff2_gate", "w_ff2_up", "w_ff2_down"]
    vecs = ["g_pre_ff1", "g_post_ff1", "g_pre_mix", "g_post_mix", "g_pre_ff2", "g_post_ff2"]

    first, late = big[:3], big[3:]
    splits = {n: -(-(args[n].shape[1] // 2) // BF16_ROW_TILE) * BF16_ROW_TILE for n in big}

    def assemble(names, gathered):
        out = {}
        for n, o in zip(names, gathered):
            if n.endswith("gate") or n.endswith("up"):
                out[n] = _unshard_cols(o, DFF_PAD)
            elif n.endswith("down"):
                out[n] = jnp.pad(o.reshape(DFF, D), ((0, DFF_PAD - DFF), (0, 0)))
            elif n == "w_in":
                full = _unshard_cols(o, IN_COLS)
                out["w_main"] = full[:, :IN_MAIN]
                out["w_f"] = jnp.pad(full[:, IN_MAIN:], ((0, 0), (0, LANE - NH)))
            else:
                out[n] = o.reshape(D, D)
        return out

    wfull = assemble(first, all_gather_shards([args[n][0].astype(BF16) for n in first], [splits[n] for n in first],
                                              name="all_gather_weights"))
    def gather_plan(names):
        return ([args[n][0].astype(BF16) for n in names], [splits[n] for n in names], functools.partial(assemble, names))

    late_weights, last_weights = gather_plan(late[:2]), gather_plan(late[2:])

    ncol = w_ada.shape[2]
    c_all = all_gather8(c, name="all_gather_c").reshape(N_DEV * nb, D)
    b_loc = lax.dynamic_slice(b_ada, (0, shard * ncol), (1, ncol))
    mod_loc = ada_fwd(c_all, w_ada[0], b_loc, name="ada_fwd")
    mod_g = all_gather8(mod_loc, name="all_gather_mod")
    row0 = (4 * ax + 2 * ay + ac) * nb
    mod_rows = lax.dynamic_slice(mod_g, (0, row0, 0), (N_DEV, nb, ncol))
    mod = jnp.concatenate([mod_rows[2 * s] for s in range(N_SHARD)], axis=-1).reshape(nb, 9, D)

    small_in = dict(g_pre_ff1=g_pre_ff1, g_post_ff1=g_post_ff1, g_pre_mix=g_pre_mix, g_post_mix=g_post_mix, g_pre_ff2=g_pre_ff2,
                    g_post_ff2=g_post_ff2, g_out_a=g_out_a, g_out_b=g_out_b, b_forget=b_forget)
    def shard_blocked(n, g):
        if n.endswith("gate") or n.endswith("up"):
            return _shard_cols(g, DFF)
        if n.endswith("down"):
            return g[:DFF].reshape(N_SHARD, DFF // N_SHARD, D)
        if n == "w_in":
            return _shard_cols(g, IN_COLS)
        return g.reshape(N_SHARD, D // N_SHARD, D)

    def chip_sums(names, gw, tag):
        gsb = [shard_blocked(n, gw[n]) for n in names]
        ras = sibling_send_half(gsb, name="grad_sibling_send_" + tag)
        return pair_sums(gsb, ras, cidx, name="grad_pair_sum_" + tag)

    hs = {}

    def early_grads(gw):
        hs.update(zip(late, chip_sums(late, gw, "late")))
        return [hs[n] for n in late]

    def last_reduce(which, g):
        return chip_sums(["w_ff1_" + which], {"w_ff1_" + which: g}, which)

    loss_part, dx0, dmod, gw, small, rbs_late, chained = local_step(
        x.reshape(T, D), loss_target.reshape(T, D), positions.reshape(T, 1), mod, wfull, small_in, late_weights, last_weights,
        early_grads, last_reduce)

    dmod_all = all_gather8(dmod, name="all_gather_dmod").reshape(N_DEV * nb, 9 * D)
    dmod_loc = lax.dynamic_slice(dmod_all, (0, shard * ncol), (N_DEV * nb, ncol))
    g_w_ada = ada_bwd(c_all, dmod_loc, name="ada_bwd")

    rbs = dict(zip(late, rbs_late))
    for which, (h, rb) in chained.items():
        hs["w_ff1_" + which], rbs["w_ff1_" + which] = h, rb
    hs["w_ff1_down"] = chip_sums(["w_ff1_down"], gw, "down")[0]
    rbs["w_ff1_down"] = chip_scatter([hs["w_ff1_down"]], name="grad_chip_scatter")[0]
    ghs = []
    for part, names in enumerate((big[:4], big[4:])):
        ghs += chip_sums_total([hs[n] for n in names], [rbs[n] for n in names], sidx, name=f"grad_chip_sum_{part}")
    theirs = sibling_swap(ghs, name="grad_sibling_swap")

    row6 = jnp.concatenate([small["g_out_a"], small["g_out_b"]], axis=1)
    row7 = jnp.concatenate([small["b_forget"], loss_part[0:1, 0:1], jnp.zeros((1, D - NH - 1), F32)], axis=1)
    pack = jnp.concatenate([small[n] for n in vecs] + [row6, row7], axis=0)
    packed = all_gather8(pack, name="all_gather_small").reshape(N_DEV, 8 * D)

    names = vecs + ["g_out_a", "g_out_b", "b_forget"]
    layout = [(i * D, D) for i in range(len(vecs))] + [(6 * D, WG), (6 * D + WG, WG), (7 * D, NH)]
    per_param, packed_sum = small_adam(packed, layout, [args[n] for n in names], [args["m_" + n] for n in names],
                                       [args["v_" + n] for n in names], name="adam_small")
    outs = dict(grad={}, delta={}, new_m={}, new_v={})
    for n, (g, d, m2, v2) in zip(names, per_param):
        outs["grad"][n], outs["delta"][n], outs["new_m"][n], outs["new_v"][n] = g, d, m2, v2
    loss = packed_sum[0, 7 * D + NH]
    outs["grad"]["b_ada"], outs["delta"]["b_ada"], outs["new_m"]["b_ada"], outs["new_v"]["b_ada"] = vec_adam(
        dmod_all, b_ada, m_b_ada, v_b_ada, name="adam_b_ada")

    for n, mine, other in zip(big, ghs, theirs):
        tr = 128 if mine.shape[0] % 128 == 0 else mine.shape[0]
        outs["grad"][n], outs["delta"][n], outs["new_m"][n], outs["new_v"][n] = adam_update_halves(
            args[n], mine, other, args["m_" + n], args["v_" + n], cidx, tr, name="adam_" + n)
    outs["delta"]["w_ada"], outs["new_m"]["w_ada"], outs["new_v"]["w_ada"] = adam_update(
        w_ada, g_w_ada, m_w_ada, v_w_ada, 128, name="adam_w_ada")
    outs["grad"]["w_ada"] = g_w_ada[None]

    order = ["w_ada", "b_ada", "g_pre_ff1", "g_post_ff1", "w_ff1_gate", "w_ff1_up", "w_ff1_down", "g_pre_mix", "g_post_mix", "w_in",
             "b_forget", "g_out_a", "g_out_b", "w_out", "g_pre_ff2", "g_post_ff2", "w_ff2_gate", "w_ff2_up", "w_ff2_down"]
    result = [loss, dx0.reshape(nb, SEQ, D)]
    for kind in ("grad", "delta", "new_m", "new_v"):
        result += [outs[kind][n] for n in order]
    return tuple(result)
```

```python
import functools
import math

import jax
import jax.numpy as jnp
from jax import lax
from jax.experimental import pallas as pl
from jax.experimental.pallas import tpu as pltpu

D = 1024
SEQ = 2048
HD = 64
NH = 8
WG = NH * HD
DFF = 2752
DFF_PAD = 2816
IN_MAIN = 6 * WG
IN_COLS = IN_MAIN + NH
N_SHARD = 4
N_DEV = 8
LANE = 128
BF16_ROW_TILE = 16
QB = 128
ROWS = 256
FB = 512
FT = 512
FOX_QB = 512
FOX_PAIRS = 4
FOX_PAIRS_BWD = 2
BAND_UNROLL = 8
BAND_UNROLL_BWD = 8
PATTERNS = ((1, 16), (4, 4), (16, 1))
ROPE_THETA = 500000.0
EPS = 1e-6
NEG = -1e30
ATTN_SCALE = HD ** -0.5
TM = 512
TM_FFN = 512
TM_BWD = 256
VMEM_LIMIT = 56 * 1024 * 1024

ADAM_LR, ADAM_B1, ADAM_B2, ADAM_EPS, ADAM_WD, ADAM_STEP = 0.001, 0.9, 0.999, 1e-08, 0.01, 10

F32 = jnp.float32
BF16 = jnp.bfloat16
MESH = pl.DeviceIdType.MESH
SDS = jax.ShapeDtypeStruct


def _cp(*sem):
    return pltpu.CompilerParams(dimension_semantics=sem, vmem_limit_bytes=VMEM_LIMIT)


def _dot(a, b):
    return jnp.dot(a, b, preferred_element_type=F32)


def _dot_nt(a, b):
    return lax.dot_general(a, b, (((1,), (1,)), ((), ())), preferred_element_type=F32)


def _dot_tn(a, b):
    return lax.dot_general(a, b, (((0,), (0,)), ((), ())), preferred_element_type=F32)


def _rms(xf):
    return lax.rsqrt(jnp.mean(xf * xf, axis=-1, keepdims=True) + EPS)


def _norm_mod_bwd(dh, xf, g, scale):
    r = _rms(xf)
    xh = xf * r
    dsh = jnp.sum(dh, axis=0, keepdims=True)
    dsc = jnp.sum(dh * (xh * g), axis=0, keepdims=True)
    dn = dh * (1.0 + scale)
    dg = jnp.sum(dn * xh, axis=0, keepdims=True)
    dxh = dn * g
    dx = r * (dxh - xh * jnp.mean(dxh * xh, axis=-1, keepdims=True))
    return dx, dsh, dsc, dg


def _post_bwd(dxo, y0, g, mgate, gs):
    r = _rms(y0)
    yh = y0 * r
    dmg = gs * jnp.sum(dxo * (yh * g), axis=0, keepdims=True)
    dy = (gs * mgate) * dxo
    dg = jnp.sum(dy * yh, axis=0, keepdims=True)
    dyh = dy * g
    dy0 = r * (dyh - yh * jnp.mean(dyh * yh, axis=-1, keepdims=True))
    return dy0, dmg, dg


def _mod_map(i, *_):
    return ((i * TM) // SEQ, 0, 0)


FF_TN = 1408
FF_TILES = ((0, 768), (768, 1536), (1536, 2304), (2304, 2816))


def _resident_scratch():
    return [pltpu.VMEM((D, DFF_PAD), BF16), pltpu.VMEM((D, DFF_PAD), BF16), pltpu.VMEM((DFF_PAD, D), BF16),
            pltpu.SemaphoreType.DMA((3,))]


def _load_resident(first_step, srcs, dsts, sems):
    @pl.when(first_step)
    def _():
        cps = [pltpu.make_async_copy(s, d, sems.at[k]) for k, (s, d) in enumerate(zip(srcs, dsts))]
        for cp in cps:
            cp.start()
        for cp in cps:
            cp.wait()


def ffn_fwd(x, mod3, g_pre, g_post, wg, wu, wd, gs, name, gather=None):
    T = x.shape[0]
    tm = TM_FFN
    ng = 0 if gather is None else len(gather[0])
    plan = None if gather is None else ShardGather([w.shape for w in gather[0]], gather[1])

    def body(*refs):
        x_ref, mod_ref, gpre_ref, gpost_ref = refs[:4]
        xo_ref, h_ref, gate_ref, up_ref, y0_ref = refs[7 + ng:12 + ng]
        wg_ref, wu_ref, wd_ref, wsem = refs[12 + 2 * ng:16 + 2 * ng]
        i = pl.program_id(0)
        if plan is not None:
            comm = (refs[7:7 + ng], refs[12 + ng:12 + 2 * ng], refs[16 + 2 * ng:])
            pl.when(i == 0)(lambda: plan.start(*comm))
        _load_resident(i == 0, refs[4:7], (wg_ref, wu_ref, wd_ref), wsem)

        xf = x_ref[...]
        hb = ((xf * _rms(xf) * gpre_ref[...]) * (1.0 + mod_ref[0, 1:2, :]) + mod_ref[0, 0:1, :]).astype(BF16)
        h_ref[...] = hb
        y0 = None
        for lo, hi in FF_TILES:
            gate = _dot(hb, wg_ref[:, lo:hi])
            up = _dot(hb, wu_ref[:, lo:hi])
            gate_ref[:, lo:hi] = gate.astype(BF16)
            up_ref[:, lo:hi] = up.astype(BF16)
            part = _dot((gate * jax.nn.sigmoid(gate) * up).astype(BF16), wd_ref[lo:hi, :])
            y0 = part if y0 is None else y0 + part
        y0_ref[...] = y0
        xo_ref[...] = xf + (gs * mod_ref[0, 2:3, :]) * (y0 * _rms(y0) * gpost_ref[...])

        if plan is not None:
            pl.when(i == T // tm - 1)(lambda: plan.finish(*comm))

    tok = pl.BlockSpec((tm, D), lambda i: (i, 0))
    vec = pl.BlockSpec((1, D), lambda i: (0, 0))
    hid = pl.BlockSpec((tm, DFF_PAD), lambda i: (i, 0))
    outs = pl.pallas_call(
        body, grid=(T // tm,),
        in_specs=[tok, pl.BlockSpec((1, 3, D), lambda i: ((i * tm) // SEQ, 0, 0)), vec, vec, HBM, HBM, HBM] + [HBM] * ng,
        out_specs=[tok, tok, hid, hid, tok] + [HBM] * ng,
        out_shape=[SDS((T, D), F32), SDS((T, D), BF16), SDS((T, DFF_PAD), BF16), SDS((T, DFF_PAD), BF16), SDS((T, D), F32)]
        + ([] if plan is None else plan.out_shapes(BF16)),
        scratch_shapes=_resident_scratch() + ([] if plan is None else plan.scratch()),
        compiler_params=_cp("arbitrary"), name=name,
    )(x, mod3, g_pre, g_post, wg, wu, wd, *([] if gather is None else gather[0]))
    return outs[:5], outs[5:]


def ffn_bwd(dxo, x, y0, mod3, g_pre, g_post, gate, up, wg, wu, wd, gs, name, scatter=None, target=None):
    assert scatter is None or target is None
    T = x.shape[0]
    nb = T // SEQ
    tm = TM_BWD
    tiles_per_seq = SEQ // tm
    ns = 0 if scatter is None else len(scatter)
    ne = ns + (target is not None)

    def body(*refs):
        dxo_ref, x_ref, y0_ref, mod_ref, gpre_ref, gpost_ref, gate_ref, up_ref = refs[:8]
        dx_ref, dy0_ref, act_ref, dgate_ref, dup_ref, dmod_ref, dgpre_ref, dgpost_ref = refs[11 + ne:19 + ne]
        wg_ref, wu_ref, wd_ref, wsem = refs[19 + 2 * ne:23 + 2 * ne]
        i = pl.program_id(0)
        _load_resident(i == 0, refs[8:11], (wg_ref, wu_ref, wd_ref), wsem)
        if ns:
            comm = (refs[11:11 + ns], refs[19 + ns:19 + 2 * ns], *refs[23 + 2 * ns:])

            @pl.when(i == 0)
            def _():
                for cp in _scatter_copies(*comm):
                    cp.start()

        @pl.when(i == 0)
        def _():
            dgpre_ref[...] = jnp.zeros_like(dgpre_ref)
            dgpost_ref[...] = jnp.zeros_like(dgpost_ref)

        @pl.when(i % tiles_per_seq == 0)
        def _():
            dmod_ref[...] = jnp.zeros_like(dmod_ref)

        dxo = dxo_ref[...]
        if target is not None:
            loss_ref = refs[19 + ne]

            @pl.when(i == 0)
            def _():
                loss_ref[...] = jnp.zeros_like(loss_ref)

            err = dxo - refs[11][...]
            loss_ref[...] += jnp.sum(err * err) * (0.5 / D)
            dxo = err * (1.0 / D)
        dy0, dmg, dg = _post_bwd(dxo, y0_ref[...], gpost_ref[...], mod_ref[0, 2:3, :], gs)
        dmod_ref[0, 2:3, :] += dmg
        dgpost_ref[...] += dg
        db = dy0.astype(BF16)
        dy0_ref[...] = db
        dh = None
        for lo, hi in FF_TILES:
            dact = _dot_nt(db, wd_ref[lo:hi, :])
            g = gate_ref[:, lo:hi].astype(F32)
            u = up_ref[:, lo:hi].astype(F32)
            sig = jax.nn.sigmoid(g)
            sl = g * sig
            dgate = (dact * u * (sig * (1.0 + g * (1.0 - sig)))).astype(BF16)
            dup = (dact * sl).astype(BF16)
            act_ref[:, lo:hi] = (sl * u).astype(BF16)
            dgate_ref[:, lo:hi] = dgate
            dup_ref[:, lo:hi] = dup
            part = _dot_nt(dgate, wg_ref[:, lo:hi]) + _dot_nt(dup, wu_ref[:, lo:hi])
            dh = part if dh is None else dh + part
        dx, dsh, dsc, dg = _norm_mod_bwd(dh, x_ref[...], gpre_ref[...], mod_ref[0, 1:2, :])
        dx_ref[...] = dxo + dx
        dmod_ref[0, 0:1, :] += dsh
        dmod_ref[0, 1:2, :] += dsc
        dgpre_ref[...] += dg

        if ns:
            @pl.when(i == T // tm - 1)
            def _():
                for cp in _scatter_copies(*comm):
                    cp.wait()

    tok = pl.BlockSpec((tm, D), lambda i: (i, 0))
    vec = pl.BlockSpec((1, D), lambda i: (0, 0))
    hid = pl.BlockSpec((tm, DFF_PAD), lambda i: (i, 0))
    modspec = pl.BlockSpec((1, 3, D), lambda i: ((i * tm) // SEQ, 0, 0))
    outs = pl.pallas_call(
        body, grid=(T // tm,),
        in_specs=[tok, tok, tok, modspec, vec, vec, hid, hid, HBM, HBM, HBM] + [HBM] * ns + [tok] * (ne - ns),
        out_specs=[tok, tok, hid, hid, hid, modspec, vec, vec] + [HBM] * ns
        + [pl.BlockSpec((8, LANE), lambda i: (0, 0))] * (ne - ns),
        out_shape=[SDS((T, D), F32), SDS((T, D), BF16), SDS((T, DFF_PAD), BF16), SDS((T, DFF_PAD), BF16),
                   SDS((T, DFF_PAD), BF16), SDS((nb, 3, D), F32), SDS((1, D), F32), SDS((1, D), F32)]
        + [SDS((3,) + h.shape[1:], h.dtype) for h in (scatter or [])] + [SDS((8, LANE), F32)] * (ne - ns),
        scratch_shapes=_resident_scratch()
        + ([pltpu.SemaphoreType.DMA((ns, 3)), pltpu.SemaphoreType.DMA((ns, 3))] if ns else []),
        compiler_params=_cp("arbitrary"), name=name,
    )(dxo, x, y0, mod3, g_pre, g_post, gate, up, wg, wu, wd, *(scatter or []), *([] if target is None else [target]))
    return outs[:8], outs[8:]


def matmul_tn(a, b, tm, tn, tk, name, scatter=None):
    T, M = a.shape
    N = b.shape[1]
    grid = (M // tm, N // tn, T // tk)
    ns = 0 if scatter is None else len(scatter)

    def body(*refs):
        a_ref, b_ref = refs[:2]
        o_ref = refs[2 + ns]
        ids = [pl.program_id(ax) for ax in range(3)]
        if ns:
            comm = (refs[2:2 + ns], refs[3 + ns:3 + 2 * ns], *refs[3 + 2 * ns:])

            @pl.when((ids[0] == 0) & (ids[1] == 0) & (ids[2] == 0))
            def _():
                for cp in _scatter_copies(*comm):
                    cp.start()

        @pl.when(ids[2] == 0)
        def _():
            o_ref[...] = jnp.zeros_like(o_ref)

        o_ref[...] += _dot_tn(a_ref[...], b_ref[...])

        if ns:
            @pl.when((ids[0] == grid[0] - 1) & (ids[1] == grid[1] - 1) & (ids[2] == grid[2] - 1))
            def _():
                for cp in _scatter_copies(*comm):
                    cp.wait()

    outs = pl.pallas_call(
        body, grid=grid,
        in_specs=[pl.BlockSpec((tk, tm), lambda i, j, k: (k, i)), pl.BlockSpec((tk, tn), lambda i, j, k: (k, j))] + [HBM] * ns,
        out_specs=[pl.BlockSpec((tm, tn), lambda i, j, k: (i, j))] + [HBM] * ns,
        out_shape=[SDS((M, N), F32)] + [SDS((3,) + h.shape[1:], h.dtype) for h in (scatter or [])],
        scratch_shapes=[pltpu.SemaphoreType.DMA((ns, 3)), pltpu.SemaphoreType.DMA((ns, 3))] if ns else [],
        compiler_params=_cp("arbitrary", "arbitrary", "arbitrary"), name=name,
    )(a, b, *(scatter or []))
    return outs[0] if scatter is None else (outs[0], outs[1:])


def matmul_tn_cols(a, bs, tk, name):
    T, M = a.shape
    n = bs[0].shape[1]
    ng = len(bs)

    def body(*refs):
        a_ref, b_refs, o_ref = refs[0], refs[1:1 + ng], refs[1 + ng]

        @pl.when(pl.program_id(0) == 0)
        def _():
            o_ref[...] = jnp.zeros_like(o_ref)

        av = a_ref[...]
        for g, b_ref in enumerate(b_refs):
            o_ref[:, g * n:(g + 1) * n] += _dot_tn(av, b_ref[...])

    return pl.pallas_call(
        body, grid=(T // tk,),
        in_specs=[pl.BlockSpec((tk, M), lambda k: (k, 0))] + [pl.BlockSpec((tk, n), lambda k: (k, 0))] * ng,
        out_specs=pl.BlockSpec((M, ng * n), lambda k: (0, 0)), out_shape=SDS((M, ng * n), F32),
        compiler_params=_cp("arbitrary"), name=name,
    )(a, *bs)


def rope_tables(pos_col, name):
    T = pos_col.shape[0]
    tm = 1024

    def body(p_ref, c_ref, s1_ref, s2_ref):
        lane = lax.broadcasted_iota(jnp.int32, (1, LANE), 1)
        l64 = lane % HD
        inv_freq = jnp.exp((l64 % 8).astype(F32) * (-math.log(ROPE_THETA) / 8.0))
        ang = p_ref[...].astype(F32) * inv_freq
        cs = jnp.cos(ang)
        sn = jnp.sin(ang)
        c_ref[...] = jnp.where(l64 < 16, cs, 1.0)
        s1_ref[...] = jnp.where(l64 < 8, -sn, 0.0)
        s2_ref[...] = jnp.where((l64 >= 8) & (l64 < 16), sn, 0.0)

    tab = pl.BlockSpec((tm, LANE), lambda i: (i, 0))
    return pl.pallas_call(
        body, grid=(T // tm,), in_specs=[pl.BlockSpec((tm, 1), lambda i: (i, 0))], out_specs=[tab, tab, tab],
        out_shape=[SDS((T, LANE), F32)] * 3, compiler_params=_cp("arbitrary"), name=name,
    )(pos_col)


def mixer_proj(x, mod3, g_pre, w_main, w_f, rc, rs1, rs2, name):
    T = x.shape[0]

    def body(x_ref, mod_ref, g_ref, w_ref, wf_ref, c_ref, s1_ref, s2_ref, h_ref, pa_ref, pb_ref, f_ref):
        xf = x_ref[...]
        h = (xf * _rms(xf) * g_ref[...]) * (1.0 + mod_ref[0, 1:2, :]) + mod_ref[0, 0:1, :]
        hb = h.astype(BF16)
        h_ref[...] = hb
        f_ref[...] = _dot(hb, wf_ref[...])
        c, s1, s2 = c_ref[...], s1_ref[...], s2_ref[...]
        for grp in range(2):
            pr = _dot(hb, w_ref[:, grp * WG:(grp + 1) * WG])
            for k in range(WG // LANE):
                t = pr[:, k * LANE:(k + 1) * LANE]
                pa_ref[:, grp * WG + k * LANE:grp * WG + (k + 1) * LANE] = (
                    t * c + pltpu.roll(t, LANE - 8, 1) * s1 + pltpu.roll(t, 8, 1) * s2)
        pa_ref[:, 2 * WG:3 * WG] = _dot(hb, w_ref[:, 2 * WG:3 * WG])
        for grp in range(3):
            pb_ref[:, grp * WG:(grp + 1) * WG] = _dot(hb, w_ref[:, (3 + grp) * WG:(4 + grp) * WG]).astype(BF16)

    tok = pl.BlockSpec((TM, D), lambda i: (i, 0))
    vec = pl.BlockSpec((1, D), lambda i: (0, 0))
    tab = pl.BlockSpec((TM, LANE), lambda i: (i, 0))
    grp3 = pl.BlockSpec((TM, 3 * WG), lambda i: (i, 0))
    return pl.pallas_call(
        body, grid=(T // TM,),
        in_specs=[tok, pl.BlockSpec((1, 3, D), _mod_map), vec, pl.BlockSpec((D, IN_MAIN), lambda i: (0, 0)),
                  pl.BlockSpec((D, LANE), lambda i: (0, 0)), tab, tab, tab],
        out_specs=[tok, grp3, grp3, tab],
        out_shape=[SDS((T, D), BF16), SDS((T, 3 * WG), F32), SDS((T, 3 * WG), BF16), SDS((T, LANE), F32)],
        compiler_params=_cp("arbitrary"), name=name,
    )(x, mod3, g_pre, w_main, w_f, rc, rs1, rs2)


def _head_lanes():
    return lax.broadcasted_iota(jnp.int32, (1, LANE), 1) < HD


def _pair(m0, a, b):
    return jnp.where(m0, a, b)


def _band_rows(i, d, nbc):
    if nbc == 1:
        return i, i, 0
    r, mb = i // nbc, i % nbc
    return r + mb * (QB * d), r + jnp.maximum(mb - 1, 0) * (QB * d), jnp.where(mb > 0, QB, 0)


def _rows(start, size, d):
    return pl.ds(pl.multiple_of(start, QB), size) if d == 1 else pl.ds(start, size, stride=d)


def _band_valid(span, off):
    rq = lax.broadcasted_iota(jnp.int32, (QB, span), 0)
    rel = lax.broadcasted_iota(jnp.int32, (QB, span), 1) - off
    return (rel <= rq) & (rel >= rq - QB)


def band_fwd(pa, name):
    T = pa.shape[0]
    B = T // SEQ
    NP = WG // LANE

    def body(q_ref, k_ref, v_ref, out_ref, lse_ref, o_s, l_s):
        m0 = _head_lanes()
        for pidx, (d, nbc) in enumerate(PATTERNS):
            span = QB if nbc == 1 else 2 * QB

            def blk(it, carry, pidx=pidx, d=d, nbc=nbc, span=span):
                ld = []
                for u in range(BAND_UNROLL):
                    qs, ks, off = _band_rows(it * BAND_UNROLL + u, d, nbc)
                    q = q_ref[_rows(qs, QB, d), :] * ATTN_SCALE
                    ld.append((qs, q, k_ref[_rows(ks, span, d), :].astype(BF16), v_ref[_rows(ks, span, d), :].astype(BF16),
                               _band_valid(span, off)))
                ss = [[jnp.where(valid, _dot_nt(jnp.where(mh, q, 0.0).astype(BF16), k), NEG) for mh in (m0, jnp.logical_not(m0))]
                      for _, q, k, _, valid in ld]
                ps = []
                for pair in ss:
                    row = []
                    for s in pair:
                        m = jnp.max(s, axis=-1, keepdims=True)
                        p = jnp.exp(s - m)
                        row.append((p.astype(BF16), jnp.sum(p, axis=-1, keepdims=True), m))
                    ps.append(row)
                pv = [[_dot(p, ld[u][3]) for p, _, _ in ps[u]] for u in range(BAND_UNROLL)]
                for u in range(BAND_UNROLL):
                    rows = _rows(ld[u][0], QB, d)
                    (_, l0, mx0), (_, l1, mx1) = ps[u]
                    o_s[pidx, rows, :] = _pair(m0, pv[u][0] / l0, pv[u][1] / l1)
                    l_s[pidx, rows, :] = _pair(m0, mx0 + jnp.log(l0), mx1 + jnp.log(l1))
                return carry

            lax.fori_loop(0, SEQ // QB // BAND_UNROLL, blk, 0)
        for c in range(SEQ // ROWS):
            sl = slice(c * ROWS, (c + 1) * ROWS)
            a, b, e = l_s[0, sl, :], l_s[1, sl, :], l_s[2, sl, :]
            m = jnp.maximum(jnp.maximum(a, b), e)
            L = m + jnp.log(jnp.exp(a - m) + jnp.exp(b - m) + jnp.exp(e - m))
            out_ref[sl, :] = jnp.exp(a - L) * o_s[0, sl, :] + jnp.exp(b - L) * o_s[1, sl, :] + jnp.exp(e - L) * o_s[2, sl, :]
            lse_ref[sl, :] = L

    blk_of = lambda g: pl.BlockSpec((SEQ, LANE), lambda b, hp, g=g: (b, g * NP + hp))
    return pl.pallas_call(
        body, grid=(B, NP), in_specs=[blk_of(0), blk_of(1), blk_of(2)], out_specs=[blk_of(0), blk_of(0)],
        out_shape=[SDS((T, WG), F32), SDS((T, WG), F32)],
        scratch_shapes=[pltpu.VMEM((3, SEQ, LANE), F32), pltpu.VMEM((3, SEQ, LANE), F32)],
        compiler_params=_cp("arbitrary", "arbitrary"), name=name,
    )(pa, pa, pa)


def _pair_rowsum(m0, prod):
    s0 = jnp.sum(jnp.where(m0, prod, 0.0), axis=-1, keepdims=True)
    return _pair(m0, s0, jnp.sum(prod, axis=-1, keepdims=True) - s0)


def band_bwd(pa, do, out, lse, rc, rs1, rs2, name):
    T = pa.shape[0]
    B = T // SEQ
    NP = WG // LANE

    def body(q_ref, k_ref, v_ref, do_ref, out_ref, l_ref, c_ref, s1_ref, s2_ref, dqo_ref, dko_ref, dvo_ref, d_s, dq_ref, dk_ref,
             dv_ref):
        m0 = _head_lanes()
        dq_ref[...] = jnp.zeros_like(dq_ref)
        dk_ref[...] = jnp.zeros_like(dk_ref)
        dv_ref[...] = jnp.zeros_like(dv_ref)
        for c in range(SEQ // ROWS):
            sl = slice(c * ROWS, (c + 1) * ROWS)
            d_s[sl, :] = _pair_rowsum(m0, do_ref[sl, :] * out_ref[sl, :])
        for d, nbc in PATTERNS:
            span = QB if nbc == 1 else 2 * QB

            def blk(it, carry, d=d, nbc=nbc, span=span):
                masks = (m0, jnp.logical_not(m0))
                ld = []
                for u in range(BAND_UNROLL_BWD):
                    qs, ks, off = _band_rows(it * BAND_UNROLL_BWD + u, d, nbc)
                    qrow, krow = _rows(qs, QB, d), _rows(ks, span, d)
                    ld.append(dict(qrow=qrow, krow=krow, q=q_ref[qrow, :] * ATTN_SCALE, k=k_ref[krow, :].astype(BF16),
                                   v=v_ref[krow, :].astype(BF16), do=do_ref[qrow, :], l=l_ref[qrow, :], dv=d_s[qrow, :],
                                   valid=_band_valid(span, off)))
                for t in ld:
                    t["qm"] = [jnp.where(mh, t["q"], 0.0).astype(BF16) for mh in masks]
                    t["dom"] = [jnp.where(mh, t["do"], 0.0).astype(BF16) for mh in masks]
                sd = [[(jnp.where(t["valid"], _dot_nt(t["qm"][h], t["k"]), NEG), _dot_nt(t["dom"][h], t["v"])) for h in range(2)]
                      for t in ld]
                pd = []
                for t, pair in zip(ld, sd):
                    row = []
                    for h, (s, dp) in enumerate(pair):
                        col = slice(h * HD, h * HD + 1)
                        p = jnp.exp(s - t["l"][:, col])
                        row.append((p.astype(BF16), (p * (dp - t["dv"][:, col])).astype(BF16)))
                    pd.append(row)
                gr = [(_dot(row[0][1], t["k"]), _dot(row[1][1], t["k"]),
                       _dot_tn(jnp.concatenate([row[0][1], row[1][1]], axis=0), jnp.concatenate(t["qm"], axis=0)),
                       _dot_tn(jnp.concatenate([row[0][0], row[1][0]], axis=0), jnp.concatenate(t["dom"], axis=0)))
                      for t, row in zip(ld, pd)]
                for t, (dq0, dq1, dk, dv) in zip(ld, gr):
                    dq_ref[t["qrow"], :] += _pair(m0, dq0, dq1) * ATTN_SCALE
                    dk_ref[t["krow"], :] += dk
                    dv_ref[t["krow"], :] += dv
                return carry

            lax.fori_loop(0, SEQ // QB // BAND_UNROLL_BWD, blk, 0)
        for c in range(SEQ // ROWS):
            sl = slice(c * ROWS, (c + 1) * ROWS)
            cc, s1, s2 = c_ref[sl, :], s1_ref[sl, :], s2_ref[sl, :]
            for acc, o_ref in ((dq_ref, dqo_ref), (dk_ref, dko_ref)):
                d = acc[sl, :]
                o_ref[sl, :] = (d * cc + pltpu.roll(d * s1, 8, 1) + pltpu.roll(d * s2, LANE - 8, 1)).astype(BF16)
            dvo_ref[sl, :] = dv_ref[sl, :].astype(BF16)

    blk_of = lambda g: pl.BlockSpec((SEQ, LANE), lambda b, hp, g=g: (b, g * NP + hp))
    tab = pl.BlockSpec((SEQ, LANE), lambda b, hp: (b, 0))
    return pl.pallas_call(
        body, grid=(B, NP), in_specs=[blk_of(0), blk_of(1), blk_of(2), blk_of(0), blk_of(0), blk_of(0), tab, tab, tab],
        out_specs=[blk_of(0)] * 3, out_shape=[SDS((T, WG), BF16)] * 3,
        scratch_shapes=[pltpu.VMEM((SEQ, LANE), F32)] * 4,
        compiler_params=_cp("arbitrary", "arbitrary"), name=name,
    )(pa, pa, pa, do, out, lse, rc, rs1, rs2)


def _tile_causal(nq, nk, q0, k0):
    r = lax.broadcasted_iota(jnp.int32, (nq, nk), 0)
    c = lax.broadcasted_iota(jnp.int32, (nq, nk), 1)
    return r + (q0 - k0) >= c


def _row_to_col(row):
    n = row.shape[1]
    return jnp.transpose(jnp.broadcast_to(row, (LANE, n)))[:, 0:1]


def _col_to_row(col):
    n = col.shape[0]
    return jnp.transpose(jnp.broadcast_to(col, (n, LANE)))[0:1, :]


def fox_fwd(pb, fblk, frow, name, gather=None):
    FQ = FOX_QB
    T = pb.shape[0]
    B = T // SEQ
    NG = WG // (LANE * FOX_PAIRS)
    NHS = 2 * FOX_PAIRS
    W = LANE * FOX_PAIRS
    n = SEQ // FQ
    ng = 0 if gather is None else len(gather[0])
    plan = None if gather is None else ShardGather([w.shape for w in gather[0]], gather[1])

    def body(*refs):
        q_ref, k_ref, v_ref, fc_ref, fr_ref = refs[:5]
        o_ref, lse_ref = refs[5 + ng:7 + ng]
        if plan is not None:
            comm = (refs[5:5 + ng], refs[7 + ng:7 + 2 * ng], refs[7 + 2 * ng:])
            ids = [pl.program_id(ax) for ax in range(3)]
            pl.when((ids[0] == 0) & (ids[1] == 0) & (ids[2] == 0))(lambda: plan.start(*comm))
        i = pl.program_id(2)
        m0 = _head_lanes()
        masks = (m0, jnp.logical_not(m0))
        heads = [(hh, slice((hh // 2) * LANE, (hh // 2 + 1) * LANE), masks[hh % 2]) for hh in range(NHS)]
        qh, fq = [], []
        for hh, lanes, mh in heads:
            q = q_ref[:, lanes] * ATTN_SCALE
            qh.append(jnp.where(mh, q, jnp.zeros_like(q)))
            fq.append(_row_to_col(fc_ref[0, hh, 0]))

        def step(t, carry, masked):
            rows = pl.ds(pl.multiple_of(t * FT, FT), FT)
            ss = [_dot_nt(qh[hh], k_ref[rows, lanes]) + fq[hh] - fr_ref[0, hh, t] for hh, lanes, _ in heads]
            if masked:
                ok = _tile_causal(FQ, FT, i * FQ, t * FT)
                ss = [jnp.where(ok, s, NEG) for s in ss]
            st = []
            for hh, _, _ in heads:
                m2 = jnp.maximum(carry[hh][0], jnp.max(ss[hh], axis=-1, keepdims=True))
                st.append((m2, jnp.exp(carry[hh][0] - m2), jnp.exp(ss[hh] - m2).astype(BF16)))
            pv = []
            for hh, lanes, mh in heads:
                vt = v_ref[rows, lanes]
                pv.append(_dot(st[hh][2], jnp.where(mh, vt, jnp.ones_like(vt))))
            return tuple((st[hh][0], st[hh][1] * carry[hh][1] + pv[hh]) for hh in range(NHS))

        one = (jnp.full((FQ, 1), NEG, F32), jnp.zeros((FQ, LANE), F32))
        last = (i * FQ) // FT
        carry = lax.fori_loop(0, last, lambda t, cr: step(t, cr, False), (one,) * NHS)
        carry = step(last, carry, True)
        for pr in range(FOX_PAIRS):
            (ma, acca), (mb, accb) = carry[2 * pr], carry[2 * pr + 1]
            la, lb = acca[:, HD:HD + 1], accb[:, 0:1]
            o_ref[:, pr * LANE:(pr + 1) * LANE] = _pair(m0, acca / la, accb / lb)
            lse_ref[0, 2 * pr, 0] = _col_to_row(ma + jnp.log(la))
            lse_ref[0, 2 * pr + 1, 0] = _col_to_row(mb + jnp.log(lb))
        if plan is not None:
            pl.when((ids[0] == B - 1) & (ids[1] == NG - 1) & (ids[2] == n - 1))(lambda: plan.finish(*comm))

    qblk = pl.BlockSpec((FQ, W), lambda b, g, i: (b * n + i, g))
    full = lambda grp: pl.BlockSpec((SEQ, W), lambda b, g, i, grp=grp: (b, grp * NG + g))
    rowb = pl.BlockSpec((1, NHS, 1, 1, FQ), lambda b, g, i: (b, g, i, 0, 0))
    outs = pl.pallas_call(
        body, grid=(B, NG, n),
        in_specs=[qblk, full(1), full(2), rowb, pl.BlockSpec((1, NHS, SEQ // FT, 1, FT), lambda b, g, i: (b, g, 0, 0, 0))]
        + [HBM] * ng,
        out_specs=[qblk, rowb] + [HBM] * ng,
        out_shape=[SDS((T, WG), F32), SDS((B, NH, n, 1, FQ), F32)] + ([] if plan is None else plan.out_shapes(BF16)),
        scratch_shapes=[] if plan is None else plan.scratch(),
        compiler_params=_cp("arbitrary", "arbitrary", "arbitrary"), name=name,
    )(pb, pb, pb, fblk, frow, *([] if gather is None else gather[0]))
    return outs[:2], outs[2:]


def fox_bwd(pb, do, lrow, drow, fblk, frow, name):
    T = pb.shape[0]
    B = T // SEQ
    PAIRS = FOX_PAIRS_BWD
    NG = WG // (LANE * PAIRS)
    NHS = 2 * PAIRS
    W = LANE * PAIRS
    n = SEQ // FB

    def body(q_ref, k_ref, v_ref, do_ref, l_ref, d_ref, fc_ref, fr_ref, dqo_ref, dk_ref, dv_ref, dfq_ref, dfk_ref, dq_ref):
        j = pl.program_id(2)
        m0 = _head_lanes()
        masks = (m0, jnp.logical_not(m0))
        heads = [(hh, slice((hh // 2) * LANE, (hh // 2 + 1) * LANE), masks[hh % 2]) for hh in range(NHS)]

        @pl.when(j == 0)
        def _():
            dq_ref[...] = jnp.zeros_like(dq_ref)
            dfq_ref[...] = jnp.zeros_like(dfq_ref)

        kj = [k_ref[:, lanes] for _, lanes, _ in heads]
        vj = [v_ref[:, lanes] for _, lanes, _ in heads]
        fk = [_row_to_col(fc_ref[0, hh, 0]) for hh in range(NHS)]

        def step(t, carry, masked):
            rows = pl.ds(pl.multiple_of(t * FT, FT), FT)
            qm, dom = [], []
            for _, lanes, mh in heads:
                qt = q_ref[rows, lanes] * ATTN_SCALE
                qm.append(jnp.where(mh, qt, jnp.zeros_like(qt)))
                dom.append(jnp.where(mh, do_ref[rows, lanes], 0.0).astype(BF16))
            ss = [_dot_nt(kj[hh], qm[hh]) + fr_ref[0, hh, t] - fk[hh] for hh in range(NHS)]
            dps = [_dot_nt(vj[hh], dom[hh]) for hh in range(NHS)]
            if masked:
                key = lax.broadcasted_iota(jnp.int32, (FB, FT), 0)
                qry = lax.broadcasted_iota(jnp.int32, (FB, FT), 1)
                ok = qry + (t * FT - j * FB) >= key
                ss = [jnp.where(ok, s, NEG) for s in ss]
            pds = []
            for hh in range(NHS):
                p = jnp.exp(ss[hh] - l_ref[0, hh, t])
                ds = p * (dps[hh] - d_ref[0, hh, t])
                dfq_ref[0, hh, t] += jnp.sum(ds, axis=0, keepdims=True)
                pds.append((p.astype(BF16), ds.astype(BF16), jnp.sum(ds, axis=-1, keepdims=True)))
            dks = [_dot(pds[hh][1], qm[hh]) for hh in range(NHS)]
            dvs = [_dot(pds[hh][0], dom[hh]) for hh in range(NHS)]
            dqs = [_dot_tn(pds[hh][1], kj[hh]) for hh in range(NHS)]
            for pr in range(PAIRS):
                dq_ref[rows, pr * LANE:(pr + 1) * LANE] += _pair(m0, dqs[2 * pr], dqs[2 * pr + 1]) * ATTN_SCALE
            return tuple((carry[hh][0] + dks[hh], carry[hh][1] + dvs[hh], carry[hh][2] - pds[hh][2]) for hh in range(NHS))

        one = (jnp.zeros((FB, LANE), F32), jnp.zeros((FB, LANE), F32), jnp.zeros((FB, 1), F32))
        first = (j * FB) // FT
        carry = step(first, (one,) * NHS, True)
        carry = lax.fori_loop(first + 1, SEQ // FT, lambda t, cr: step(t, cr, False), carry)
        for pr in range(PAIRS):
            (dka, dva, dfka), (dkb, dvb, dfkb) = carry[2 * pr], carry[2 * pr + 1]
            dk_ref[:, pr * LANE:(pr + 1) * LANE] = _pair(m0, dka, dkb).astype(BF16)
            dv_ref[:, pr * LANE:(pr + 1) * LANE] = _pair(m0, dva, dvb).astype(BF16)
            dfk_ref[0, 2 * pr, 0] = _col_to_row(dfka)
            dfk_ref[0, 2 * pr + 1, 0] = _col_to_row(dfkb)

        @pl.when(j == n - 1)
        def _():
            dqo_ref[...] = dq_ref[...].astype(BF16)

    kblk = lambda grp: pl.BlockSpec((FB, W), lambda b, g, j, grp=grp: (b * n + j, grp * NG + g))
    full = pl.BlockSpec((SEQ, W), lambda b, g, j: (b, g))
    rowf = pl.BlockSpec((1, NHS, SEQ // FT, 1, FT), lambda b, g, j: (b, g, 0, 0, 0))
    rowb = pl.BlockSpec((1, NHS, 1, 1, FB), lambda b, g, j: (b, g, j, 0, 0))
    return pl.pallas_call(
        body, grid=(B, NG, n), in_specs=[full, kblk(1), kblk(2), full, rowf, rowf, rowb, rowf],
        out_specs=[full, kblk(0), kblk(0), rowf, rowb],
        out_shape=[SDS((T, WG), BF16), SDS((T, WG), BF16), SDS((T, WG), BF16), SDS((B, NH, SEQ // FT, 1, FT), F32),
                   SDS((B, NH, n, 1, FB), F32)],
        scratch_shapes=[pltpu.VMEM((SEQ, W), F32)],
        compiler_params=_cp("arbitrary", "arbitrary", "arbitrary"), name=name,
    )(pb, pb, pb, do, lrow, drow, fblk, frow)


def _tri(lower):
    r = lax.broadcasted_iota(jnp.int32, (LANE, LANE), 0)
    c = lax.broadcasted_iota(jnp.int32, (LANE, LANE), 1)
    return ((r >= c) if lower else (r <= c)).astype(F32)


def _tri_dot(t, xblk):
    return jnp.dot(t, xblk, precision=lax.Precision.HIGHEST, preferred_element_type=F32)


def forget_cumsum(flog, bias, name):
    B, S, _ = flog.shape

    def body(f_ref, b_ref, o_ref):
        t = _tri(True)
        carry = jnp.zeros((1, LANE), F32)
        for blk in range(S // LANE):
            z = f_ref[0, blk * LANE:(blk + 1) * LANE, :] + b_ref[...]
            lf = jnp.minimum(z, 0.0) - jnp.log(1.0 + jnp.exp(-jnp.abs(z)))
            cs = _tri_dot(t, lf) + carry
            o_ref[0, blk * LANE:(blk + 1) * LANE, :] = cs
            carry = cs[LANE - 1:LANE, :]

    spec = pl.BlockSpec((1, S, LANE), lambda b: (b, 0, 0))
    return pl.pallas_call(
        body, grid=(B,), in_specs=[spec, pl.BlockSpec((1, LANE), lambda b: (0, 0))], out_specs=spec,
        out_shape=SDS((B, S, LANE), F32), compiler_params=_cp("arbitrary"), name=name,
    )(flog, bias)


def forget_cumsum_bwd(dF, flog, bias, name):
    B, S, _ = flog.shape

    def body(d_ref, f_ref, b_ref, o_ref, db_ref):
        @pl.when(pl.program_id(0) == 0)
        def _():
            db_ref[...] = jnp.zeros_like(db_ref)

        t = _tri(False)
        carry = jnp.zeros((1, LANE), F32)
        tot = jnp.zeros((1, LANE), F32)
        for blk in reversed(range(S // LANE)):
            sl = slice(blk * LANE, (blk + 1) * LANE)
            rc = _tri_dot(t, d_ref[0, sl, :]) + carry
            carry = rc[0:1, :]
            z = f_ref[0, sl, :] + b_ref[...]
            dz = rc * jax.nn.sigmoid(-z)
            o_ref[0, sl, :] = dz
            tot = tot + jnp.sum(dz, axis=0, keepdims=True)
        db_ref[...] += tot

    spec = pl.BlockSpec((1, S, LANE), lambda b: (b, 0, 0))
    vec = pl.BlockSpec((1, LANE), lambda b: (0, 0))
    return pl.pallas_call(
        body, grid=(B,), in_specs=[spec, spec, vec], out_specs=[spec, vec],
        out_shape=[SDS((B, S, LANE), F32), SDS((1, LANE), F32)], compiler_params=_cp("arbitrary"), name=name,
    )(dF, flog, bias)


def mixer_out_fwd(oa, ob, goa, gob, w_out, g_post, x, mod3, name):
    T = x.shape[0]

    def body(oa_ref, ob_ref, goa_ref, gob_ref, w_ref, gp_ref, x_ref, mod_ref, xo_ref, mg_ref, y0_ref):
        a = oa_ref[...]
        b = ob_ref[...]
        mg = jnp.concatenate([a * _rms(a) * goa_ref[...], b * _rms(b) * gob_ref[...]], axis=-1).astype(BF16)
        mg_ref[...] = mg
        y0 = _dot(mg, w_ref[...])
        y0_ref[...] = y0
        xo_ref[...] = x_ref[...] + mod_ref[0, 2:3, :] * (y0 * _rms(y0) * gp_ref[...])

    tok = pl.BlockSpec((TM, D), lambda i: (i, 0))
    half = pl.BlockSpec((TM, WG), lambda i: (i, 0))
    hv = pl.BlockSpec((1, WG), lambda i: (0, 0))
    return pl.pallas_call(
        body, grid=(T // TM,),
        in_specs=[half, half, hv, hv, pl.BlockSpec((D, D), lambda i: (0, 0)), pl.BlockSpec((1, D), lambda i: (0, 0)), tok,
                  pl.BlockSpec((1, 3, D), _mod_map)],
        out_specs=[tok, tok, tok], out_shape=[SDS((T, D), F32), SDS((T, D), BF16), SDS((T, D), F32)],
        compiler_params=_cp("arbitrary"), name=name,
    )(oa, ob, goa, gob, w_out, g_post, x, mod3)


def mixer_out_bwd(dxo, y0, mod3, g_post, w_out, oa, ob, goa, gob, name):
    T = dxo.shape[0]
    nb = T // SEQ
    tiles_per_seq = SEQ // TM

    def body(dxo_ref, y0_ref, mod_ref, gp_ref, w_ref, oa_ref, ob_ref, goa_ref, gob_ref,
             dy0_ref, doa_ref, dob_ref, dmg_ref, dgp_ref, dgoa_ref, dgob_ref, dvb_ref):
        i = pl.program_id(0)

        @pl.when(i == 0)
        def _():
            dgp_ref[...] = jnp.zeros_like(dgp_ref)
            dgoa_ref[...] = jnp.zeros_like(dgoa_ref)
            dgob_ref[...] = jnp.zeros_like(dgob_ref)

        @pl.when(i % tiles_per_seq == 0)
        def _():
            dmg_ref[...] = jnp.zeros_like(dmg_ref)

        dy0, dmg, dg = _post_bwd(dxo_ref[...], y0_ref[...], gp_ref[...], mod_ref[0, 2:3, :], 1.0)
        dmg_ref[0] += dmg
        dgp_ref[...] += dg
        db = dy0.astype(BF16)
        dy0_ref[...] = db
        dm = _dot_nt(db, w_ref[...])
        for o_ref, g_ref, do_ref, dg_ref, sl in ((oa_ref, goa_ref, doa_ref, dgoa_ref, slice(0, WG)),
                                                  (ob_ref, gob_ref, dob_ref, dgob_ref, slice(WG, 2 * WG))):
            o = o_ref[...]
            r = _rms(o)
            oh = o * r
            d = dm[:, sl]
            dg_ref[...] += jnp.sum(d * oh, axis=0, keepdims=True)
            dh = d * g_ref[...]
            do = r * (dh - oh * jnp.mean(dh * oh, axis=-1, keepdims=True))
            do_ref[...] = do
        ind = (lax.broadcasted_iota(jnp.int32, (WG, LANE), 0) // HD == lax.broadcasted_iota(jnp.int32, (WG, LANE), 1)).astype(BF16)
        prod = do * o
        hi = prod.astype(BF16)
        dvb_ref[...] = _dot(hi, ind) + _dot((prod - hi.astype(F32)).astype(BF16), ind)

    tok = pl.BlockSpec((TM, D), lambda i: (i, 0))
    half = pl.BlockSpec((TM, WG), lambda i: (i, 0))
    hv = pl.BlockSpec((1, WG), lambda i: (0, 0))
    vec = pl.BlockSpec((1, D), lambda i: (0, 0))
    return pl.pallas_call(
        body, grid=(T // TM,),
        in_specs=[tok, tok, pl.BlockSpec((1, 3, D), _mod_map), vec, pl.BlockSpec((D, D), lambda i: (0, 0)), half, half, hv, hv],
        out_specs=[tok, half, half, pl.BlockSpec((1, 1, D), _mod_map), vec, hv, hv, pl.BlockSpec((TM, LANE), lambda i: (i, 0))],
        out_shape=[SDS((T, D), BF16), SDS((T, WG), F32), SDS((T, WG), F32), SDS((nb, 1, D), F32), SDS((1, D), F32),
                   SDS((1, WG), F32), SDS((1, WG), F32), SDS((T, LANE), F32)],
        compiler_params=_cp("arbitrary"), name=name,
    )(dxo, y0, mod3, g_post, w_out, oa, ob, goa, gob)


def mixer_proj_bwd(dps, dflog, dxo, x, mod3, g_pre, w_main, w_f, name):
    T = x.shape[0]
    nb = T // SEQ
    tiles_per_seq = SEQ // TM
    ngrp = len(dps)

    def body(*refs):
        dp_refs = refs[:ngrp]
        df_ref, dxo_ref, x_ref, mod_ref, g_ref, w_ref, wf_ref, dx_ref, dmod_ref, dg_ref = refs[ngrp:]
        i = pl.program_id(0)

        @pl.when(i == 0)
        def _():
            dg_ref[...] = jnp.zeros_like(dg_ref)

        @pl.when(i % tiles_per_seq == 0)
        def _():
            dmod_ref[...] = jnp.zeros_like(dmod_ref)

        dh = _dot_nt(df_ref[...].astype(BF16), wf_ref[...])
        for g, dp_ref in enumerate(dp_refs):
            dh = dh + _dot_nt(dp_ref[...], w_ref[:, g * WG:(g + 1) * WG])
        dx, dsh, dsc, dg = _norm_mod_bwd(dh, x_ref[...], g_ref[...], mod_ref[0, 1:2, :])
        dx_ref[...] = dxo_ref[...] + dx
        dmod_ref[0, 0:1, :] += dsh
        dmod_ref[0, 1:2, :] += dsc
        dg_ref[...] += dg

    tok = pl.BlockSpec((TM, D), lambda i: (i, 0))
    vec = pl.BlockSpec((1, D), lambda i: (0, 0))
    return pl.pallas_call(
        body, grid=(T // TM,),
        in_specs=[pl.BlockSpec((TM, WG), lambda i: (i, 0))] * ngrp
        + [pl.BlockSpec((TM, LANE), lambda i: (i, 0)), tok, tok, pl.BlockSpec((1, 3, D), _mod_map), vec,
           pl.BlockSpec((D, IN_MAIN), lambda i: (0, 0)), pl.BlockSpec((D, LANE), lambda i: (0, 0))],
        out_specs=[tok, pl.BlockSpec((1, 2, D), _mod_map), vec],
        out_shape=[SDS((T, D), F32), SDS((nb, 2, D), F32), SDS((1, D), F32)],
        compiler_params=_cp("arbitrary"), name=name,
    )(*dps, dflog, dxo, x, mod3, g_pre, w_main, w_f)


def ada_fwd(c_all, w, b, name):
    n = w.shape[1]
    tn = n // 2

    def body(c_ref, w_ref, b_ref, o_ref):
        cv = c_ref[...]
        o_ref[...] = _dot((cv * jax.nn.sigmoid(cv)).astype(BF16), w_ref[...].astype(BF16)) + b_ref[...]

    R = c_all.shape[0]
    return pl.pallas_call(
        body, grid=(2,),
        in_specs=[pl.BlockSpec((R, D), lambda j: (0, 0)), pl.BlockSpec((D, tn), lambda j: (0, j)), pl.BlockSpec((1, tn), lambda j: (0, j))],
        out_specs=pl.BlockSpec((R, tn), lambda j: (0, j)), out_shape=SDS((R, n), F32),
        compiler_params=_cp("arbitrary"), name=name,
    )(c_all, w, b)


def ada_bwd(c_all, dmod, name):
    R, n = dmod.shape
    tn = n // 2

    def body(c_ref, d_ref, o_ref):
        cv = c_ref[...]
        o_ref[...] = _dot_tn((cv * jax.nn.sigmoid(cv)).astype(BF16), d_ref[...].astype(BF16))

    return pl.pallas_call(
        body, grid=(2,), in_specs=[pl.BlockSpec((R, D), lambda j: (0, 0)), pl.BlockSpec((R, tn), lambda j: (0, j))],
        out_specs=pl.BlockSpec((D, tn), lambda j: (0, j)), out_shape=SDS((D, n), F32),
        compiler_params=_cp("arbitrary"), name=name,
    )(c_all, dmod)


def _adam_math(w, g, m, v):
    m2 = ADAM_B1 * m + (1.0 - ADAM_B1) * g
    v2 = ADAM_B2 * v + (1.0 - ADAM_B2) * (g * g)
    m_hat = m2 / (1.0 - ADAM_B1 ** ADAM_STEP)
    v_hat = v2 / (1.0 - ADAM_B2 ** ADAM_STEP)
    delta = -ADAM_LR * (m_hat / (jnp.sqrt(v_hat) + ADAM_EPS) + ADAM_WD * w)
    return delta, m2, v2


def adam_update(w, g, m, v, tr, name):
    _, R, C = w.shape

    def body(w_ref, g_ref, m_ref, v_ref, d_ref, mo_ref, vo_ref):
        d_ref[0], mo_ref[0], vo_ref[0] = _adam_math(w_ref[0], g_ref[...], m_ref[0], v_ref[0])

    spec = pl.BlockSpec((1, tr, C), lambda i: (0, i, 0))
    gspec = pl.BlockSpec((tr, C), lambda i: (i, 0))
    return pl.pallas_call(
        body, grid=(R // tr,), in_specs=[spec, gspec, spec, spec], out_specs=[spec] * 3, out_shape=[SDS((1, R, C), F32)] * 3,
        compiler_params=_cp("arbitrary"), name=name,
    )(w, g, m, v)


def adam_update_halves(w, mine, other, m, v, cidx, tr, name):
    _, R, C = w.shape
    nh = R // 2 // tr

    def body(c_ref, w_ref, a_ref, b_ref, m_ref, v_ref, g_ref, d_ref, mo_ref, vo_ref):
        first_half = pl.program_id(0) < nh
        g = jnp.where(first_half == (c_ref[0] == 0), a_ref[...], b_ref[...])
        g_ref[0] = g
        d_ref[0], mo_ref[0], vo_ref[0] = _adam_math(w_ref[0], g, m_ref[0], v_ref[0])

    spec = pl.BlockSpec((1, tr, C), lambda i, c_ref: (0, i, 0))
    hspec = pl.BlockSpec((tr, C), lambda i, c_ref: (i % nh, 0))
    return pl.pallas_call(
        body,
        grid_spec=pltpu.PrefetchScalarGridSpec(num_scalar_prefetch=1, grid=(R // tr,), in_specs=[spec, hspec, hspec, spec, spec],
                                               out_specs=[spec] * 4),
        out_shape=[SDS((1, R, C), F32)] * 4, compiler_params=_cp("arbitrary"), name=name,
    )(cidx, w, mine, other, m, v)


def vec_adam(parts, w, m, v, name):
    P, C = parts.shape

    def body(p_ref, w_ref, m_ref, v_ref, g_ref, d_ref, mo_ref, vo_ref):
        g = jnp.sum(p_ref[...], axis=0, keepdims=True)
        g_ref[...] = g
        d_ref[...], mo_ref[...], vo_ref[...] = _adam_math(w_ref[...], g, m_ref[...], v_ref[...])

    return pl.pallas_call(body, out_shape=[SDS((1, C), F32)] * 4, compiler_params=_cp(), name=name)(parts, w, m, v)


def small_adam(parts, layout, ws, ms, vs, name):
    P, C = parts.shape
    k = len(layout)

    def body(*refs):
        p_ref = refs[0]
        w_refs, m_refs, v_refs = refs[1:1 + k], refs[1 + k:1 + 2 * k], refs[1 + 2 * k:1 + 3 * k]
        outs = refs[1 + 3 * k:]
        g_all = jnp.sum(p_ref[...], axis=0, keepdims=True)
        outs[4 * k][...] = g_all
        for n, (off, width) in enumerate(layout):
            g = g_all[:, off:off + width]
            outs[4 * n][...] = g
            outs[4 * n + 1][...], outs[4 * n + 2][...], outs[4 * n + 3][...] = _adam_math(
                w_refs[n][...], g, m_refs[n][...], v_refs[n][...])

    shapes = [SDS((1, width), F32) for _, width in layout for _ in range(4)] + [SDS((1, C), F32)]
    res = pl.pallas_call(body, out_shape=shapes, compiler_params=_cp(), name=name)(parts, *ws, *ms, *vs)
    return [tuple(res[4 * n:4 * n + 4]) for n in range(k)], res[4 * k]


HBM = pl.BlockSpec(memory_space=pltpu.HBM)
VMEM = pl.BlockSpec(memory_space=pltpu.VMEM)


def _place():
    x, y, c = lax.axis_index("x"), lax.axis_index("y"), lax.axis_index("c")
    return x, y, c, [(1 - x, y), (x, 1 - y), (1 - x, 1 - y)]


def all_gather8(xs, name):
    R, C = xs.shape

    def body(x_ref, out_ref, send_sems, recv_sems, local_sem):
        x, y, c, chips = _place()
        me, sibling = (x, y, c), (x, y, 1 - c)

        def slot(px, py, pc):
            return out_ref.at[4 * px + 2 * py + pc]

        def copy(k, block, to, src=None):
            return pltpu.make_async_remote_copy(
                src_ref=slot(*block) if src is None else src, dst_ref=slot(*block),
                send_sem=send_sems.at[k], recv_sem=recv_sems.at[k], device_id=to, device_id_type=MESH)

        mine = pltpu.make_async_copy(x_ref, slot(*me), local_sem)
        mine.start()
        first = [copy(0, me, sibling, src=x_ref)]
        first += [copy(1 + j, me, (*chip, c), src=x_ref) for j, chip in enumerate(chips)]
        for cp in first:
            cp.start()
        passed = [copy(4 + j, (*chip, c), sibling) for j, chip in enumerate(chips)]
        for j, chip in enumerate(chips):
            copy(1 + j, (*chip, c), me).wait_recv()
            passed[j].start()
        copy(0, sibling, me).wait_recv()
        for j, chip in enumerate(chips):
            copy(4 + j, (*chip, 1 - c), me).wait_recv()
        for cp in first + passed:
            cp.wait_send()
        mine.wait()

    return pl.pallas_call(
        body, out_shape=SDS((N_DEV, R, C), xs.dtype), in_specs=[VMEM], out_specs=VMEM,
        scratch_shapes=[pltpu.SemaphoreType.DMA((7,)), pltpu.SemaphoreType.DMA((7,)), pltpu.SemaphoreType.DMA],
        compiler_params=pltpu.CompilerParams(vmem_limit_bytes=VMEM_LIMIT), name=name,
    )(xs)


class ShardGather:
    def __init__(self, shapes, splits):
        self.shapes, self.splits, self.n = shapes, splits, len(shapes)

    def scratch(self):
        n = self.n
        return [pltpu.SemaphoreType.DMA((n, 6)), pltpu.SemaphoreType.DMA((n, 6)), pltpu.SemaphoreType.DMA((n,))]

    def out_shapes(self, dtype):
        return [SDS((N_SHARD,) + tuple(s), dtype) for s in self.shapes]

    def _half(self, ref, k, cc):
        lo, hi = (0, self.splits[k]) if cc == 0 else (self.splits[k], self.shapes[k][0])
        return ref.at[pl.ds(lo, hi - lo)]

    def _phase(self, w_refs, o_refs, sems, finish):
        send_sems, recv_sems, local_sems = sems
        x, y, c, chips = _place()
        sibling = (x, y, 1 - c)
        me_s = 2 * x + y

        def rcopy(src, dst, k, s, to):
            return pltpu.make_async_remote_copy(src_ref=src, dst_ref=dst, send_sem=send_sems.at[k, s],
                                                recv_sem=recv_sems.at[k, s], device_id=to, device_id_type=MESH)

        for cc in (0, 1):
            @pl.when(c == cc)
            def _():
                local = [pltpu.make_async_copy(w_refs[k], o_refs[k].at[me_s], local_sems.at[k]) for k in range(self.n)]
                first = [rcopy(self._half(w_refs[k], k, cc), self._half(o_refs[k].at[me_s], k, cc), k, j, (*chip, c))
                         for k in range(self.n) for j, chip in enumerate(chips)]
                if not finish:
                    for cp in local + first:
                        cp.start()
                    return
                passed = []
                for k in range(self.n):
                    for j, chip in enumerate(chips):
                        land = self._half(o_refs[k].at[2 * chip[0] + chip[1]], k, cc)
                        rcopy(land, land, k, j, (*chip, c)).wait_recv()
                        f = rcopy(land, land, k, 3 + j, sibling)
                        f.start()
                        passed.append(f)
                for k in range(self.n):
                    for j, chip in enumerate(chips):
                        other = self._half(o_refs[k].at[2 * chip[0] + chip[1]], k, 1 - cc)
                        rcopy(other, other, k, 3 + j, sibling).wait_recv()
                for s in first + passed:
                    s.wait_send()
                for cp in local:
                    cp.wait()

    def start(self, w_refs, o_refs, sems):
        self._phase(w_refs, o_refs, sems, False)

    def finish(self, w_refs, o_refs, sems):
        self._phase(w_refs, o_refs, sems, True)


def all_gather_shards(ws, splits, name):
    n = len(ws)
    plan = ShardGather([w.shape for w in ws], splits)

    def body(*refs):
        plan.start(refs[:n], refs[n:2 * n], refs[2 * n:])
        plan.finish(refs[:n], refs[n:2 * n], refs[2 * n:])

    return pl.pallas_call(
        body, out_shape=plan.out_shapes(ws[0].dtype), in_specs=[HBM] * n, out_specs=[HBM] * n,
        scratch_shapes=plan.scratch(), name=name,
    )(*ws)


def sibling_send_half(gs, name):
    n = len(gs)

    def body(*refs):
        g_refs, o_refs = refs[:n], refs[n:2 * n]
        send_sems, recv_sems = refs[2 * n:]
        x, y, c, _ = _place()
        cps = []
        for k in range(n):
            hr = gs[k].shape[1] // 2
            src = g_refs[k].at[:, pl.ds(pl.multiple_of((1 - c) * hr, 8), hr)]
            cp = pltpu.make_async_remote_copy(src_ref=src, dst_ref=o_refs[k], send_sem=send_sems.at[k], recv_sem=recv_sems.at[k],
                                              device_id=(x, y, 1 - c), device_id_type=MESH)
            cp.start()
            cps.append(cp)
        for cp in cps:
            cp.wait()

    return pl.pallas_call(
        body, out_shape=[SDS((N_SHARD, g.shape[1] // 2, g.shape[2]), g.dtype) for g in gs], in_specs=[HBM] * n, out_specs=[HBM] * n,
        scratch_shapes=[pltpu.SemaphoreType.DMA((n,)), pltpu.SemaphoreType.DMA((n,))], name=name,
    )(*gs)


def _scatter_copies(h_refs, o_refs, send_sems, recv_sems):
    _, _, c, chips = _place()
    return [pltpu.make_async_remote_copy(
        src_ref=h_refs[k].at[2 * chip[0] + chip[1]], dst_ref=o_refs[k].at[j], send_sem=send_sems.at[k, j],
        recv_sem=recv_sems.at[k, j], device_id=(*chip, c), device_id_type=MESH)
        for k in range(len(h_refs)) for j, chip in enumerate(chips)]


def chip_scatter(hs, name):
    n = len(hs)

    def body(*refs):
        cps = _scatter_copies(refs[:n], refs[n:2 * n], *refs[2 * n:])
        for cp in cps:
            cp.start()
        for cp in cps:
            cp.wait()

    return pl.pallas_call(
        body, out_shape=[SDS((3,) + h.shape[1:], h.dtype) for h in hs], in_specs=[HBM] * n, out_specs=[HBM] * n,
        scratch_shapes=[pltpu.SemaphoreType.DMA((n, 3)), pltpu.SemaphoreType.DMA((n, 3))], name=name,
    )(*hs)


def sibling_swap(ghs, name):
    n = len(ghs)

    def body(*refs):
        g_refs, o_refs = refs[:n], refs[n:2 * n]
        send_sems, recv_sems = refs[2 * n:]
        x, y, c, _ = _place()
        cps = []
        for k in range(n):
            cp = pltpu.make_async_remote_copy(src_ref=g_refs[k], dst_ref=o_refs[k], send_sem=send_sems.at[k],
                                              recv_sem=recv_sems.at[k], device_id=(x, y, 1 - c), device_id_type=MESH)
            cp.start()
            cps.append(cp)
        for cp in cps:
            cp.wait()

    return pl.pallas_call(
        body, out_shape=[SDS(g.shape, g.dtype) for g in ghs], in_specs=[HBM] * n, out_specs=[HBM] * n,
        scratch_shapes=[pltpu.SemaphoreType.DMA((n,)), pltpu.SemaphoreType.DMA((n,))], name=name,
    )(*ghs)


def pair_sums(gs, ras, cidx, name):
    n = len(gs)
    halves = [(g.shape[1] // 2, g.shape[2]) for g in gs]

    def body(c_ref, *refs):
        for g_ref, a_ref, o_ref in zip(refs[:n], refs[n:2 * n], refs[2 * n:]):
            o_ref[...] = (g_ref[...] + a_ref[...]).astype(BF16)

    mine = [pl.BlockSpec((1, hr, cols), lambda s, c_ref: (s, c_ref[0], 0)) for hr, cols in halves]
    whole = [pl.BlockSpec((1, hr, cols), lambda s, c_ref: (s, 0, 0)) for hr, cols in halves]
    return pl.pallas_call(
        body,
        grid_spec=pltpu.PrefetchScalarGridSpec(num_scalar_prefetch=1, grid=(N_SHARD,), in_specs=mine + whole, out_specs=whole),
        out_shape=[SDS((N_SHARD, hr, cols), BF16) for hr, cols in halves], compiler_params=_cp("arbitrary"), name=name,
    )(cidx, *gs, *ras)


def chip_sums_total(hs, rbs, sidx, name):
    n = len(hs)
    halves = [h.shape[1:] for h in hs]

    def body(s_ref, *refs):
        for h_ref, r_ref, o_ref in zip(refs[:n], refs[n:2 * n], refs[2 * n:]):
            o_ref[...] = ((h_ref[0].astype(F32) + r_ref[0].astype(F32)) + r_ref[1].astype(F32)) + r_ref[2].astype(F32)

    return pl.pallas_call(
        body,
        grid_spec=pltpu.PrefetchScalarGridSpec(
            num_scalar_prefetch=1, grid=(1,),
            in_specs=[pl.BlockSpec((1, hr, cols), lambda i, s_ref: (s_ref[0], 0, 0)) for hr, cols in halves]
            + [pl.BlockSpec((3, hr, cols), lambda i, s_ref: (0, 0, 0)) for hr, cols in halves],
            out_specs=[pl.BlockSpec((hr, cols), lambda i, s_ref: (0, 0)) for hr, cols in halves]),
        out_shape=[SDS((hr, cols), F32) for hr, cols in halves], compiler_params=_cp("arbitrary"), name=name,
    )(sidx, *hs, *rbs)


def _shard_cols(g, n_valid):
    r = g.shape[0]
    return g[:, :n_valid].reshape(r, N_SHARD, n_valid // N_SHARD).transpose(1, 0, 2)


def _unshard_cols(o, pad_to):
    _, r, n = o.shape
    full = o.transpose(1, 0, 2).reshape(r, N_SHARD * n)
    return jnp.pad(full, ((0, 0), (0, pad_to - N_SHARD * n)))


def _rows_of_tiles(t):
    B, H, S = t.shape
    return t.reshape(B, H, S // FT, 1, FT)


def mixer_fwd(x1, mod3, g_pre, w_main, w_f, b_forget_pad, goa, gob, w_out, g_post, tabs, nb, gather=None):
    hmix, pa, pb, flog = mixer_proj(x1, mod3, g_pre, w_main, w_f, *tabs, name="mixer_proj")
    out_a, lse_a = band_fwd(pa, name="band_fwd")
    F = forget_cumsum(flog.reshape(nb, SEQ, LANE), b_forget_pad, name="forget_cumsum")
    Fh = F[:, :, :NH].transpose(0, 2, 1)
    fblk = Fh.reshape(nb, NH, SEQ // FB, 1, FB)
    frow = _rows_of_tiles(Fh)
    (out_b, lse_b), gathered = fox_fwd(pb, Fh.reshape(nb, NH, SEQ // FOX_QB, 1, FOX_QB), frow, name="fox_fwd", gather=gather)
    x2, merged, y0m = mixer_out_fwd(out_a, out_b, goa, gob, w_out, g_post, x1, mod3, name="mixer_out_fwd")
    res = dict(hmix=hmix, flog=flog, pa=pa, pb=pb, out_a=out_a, lse_a=lse_a, fblk=fblk, frow=frow, out_b=out_b,
               lrow=_rows_of_tiles(lse_b.reshape(nb, NH, SEQ)), merged=merged, y0m=y0m)
    return x2, res, gathered


def mixer_bwd(dx2, x1, mod3, g_pre, w_main, w_f, b_forget_pad, goa, gob, w_out, g_post, tabs, res, nb):
    T = nb * SEQ
    dy0m, doa, dob, dmgate, dg_post, dgoa, dgob, dvec_b = mixer_out_bwd(
        dx2, res["y0m"], mod3, g_post, w_out, res["out_a"], res["out_b"], goa, gob, name="mixer_out_bwd")
    dqa, dka, dva = band_bwd(res["pa"], doa, res["out_a"], res["lse_a"], *tabs, name="band_bwd")
    drow = _rows_of_tiles(dvec_b[:, :NH].reshape(nb, SEQ, NH).transpose(0, 2, 1))
    dqb, dkb, dvb, dfq, dfk = fox_bwd(res["pb"], dob, res["lrow"], drow, res["fblk"], res["frow"], name="fox_bwd")
    dF = (dfq.reshape(nb, NH, SEQ) + dfk.reshape(nb, NH, SEQ)).transpose(0, 2, 1)
    dF = jnp.pad(dF, ((0, 0), (0, 0), (0, LANE - NH)))
    dflog, dbf = forget_cumsum_bwd(dF, res["flog"].reshape(nb, SEQ, LANE), b_forget_pad, name="forget_cumsum_bwd")
    dflog = dflog.reshape(T, LANE)
    dps = (dqa, dka, dva, dqb, dkb, dvb)
    dx1, dmod2, dg_pre = mixer_proj_bwd(dps, dflog, dx2, x1, mod3, g_pre, w_main, w_f, name="mixer_proj_bwd")
    g_main = matmul_tn_cols(res["hmix"], dps, 1024, name="grad_w_in")
    g_f = matmul_tn(res["hmix"], dflog.astype(BF16), D, LANE, 1024, name="grad_w_forget")
    g_out = matmul_tn(res["merged"], dy0m, D, D, 1024, name="grad_w_out")
    dmod3 = jnp.concatenate([dmod2, dmgate], axis=1)
    return dx1, dmod3, dict(g_pre=dg_pre, g_post=dg_post, goa=dgoa, gob=dgob, b_forget=dbf[:, :NH],
                            w_in=jnp.concatenate([g_main, g_f[:, :NH]], axis=1), w_out=g_out)


def ffn_grads(h, dy0, act, dgate, dup, pre, reduce=None):
    g_gate = matmul_tn(h, dgate, D, DFF_PAD, 1024, name=pre + "_grad_gate")
    if reduce is None:
        g_up = matmul_tn(h, dup, D, DFF_PAD, 1024, name=pre + "_grad_up")
        g_down = matmul_tn(act, dy0, FF_TN, D, 1024, name=pre + "_grad_down")
        return (g_gate, g_up, g_down), {}
    hs_gate = reduce("gate", g_gate)
    g_up, rb_gate = matmul_tn(h, dup, D, DFF_PAD, 1024, name=pre + "_grad_up", scatter=hs_gate)
    hs_up = reduce("up", g_up)
    g_down, rb_up = matmul_tn(act, dy0, FF_TN, D, 1024, name=pre + "_grad_down", scatter=hs_up)
    return (g_gate, g_up, g_down), {"gate": (hs_gate[0], rb_gate[0]), "up": (hs_up[0], rb_up[0])}


def local_step(x0, tgt, pos_col, mod, wfull, p, late_weights=None, last_weights=None, early_grads=None, last_reduce=None):
    T = x0.shape[0]
    nb = T // SEQ
    mod_ff1, mod_mix, mod_ff2 = mod[:, 0:3], mod[:, 3:6], mod[:, 6:9]
    tabs = rope_tables(pos_col, name="rope_tables")
    bf_pad = jnp.pad(p["b_forget"], ((0, 0), (0, LANE - NH)))

    (x1, h1, gate1, up1, y01), gathered = ffn_fwd(
        x0, mod_ff1, p["g_pre_ff1"], p["g_post_ff1"], wfull["w_ff1_gate"], wfull["w_ff1_up"], wfull["w_ff1_down"], 0.5,
        name="ff1_fwd", gather=None if late_weights is None else late_weights[:2])
    if late_weights is not None:
        wfull = {**wfull, **late_weights[2](gathered)}
    x2, res, gathered = mixer_fwd(x1, mod_mix, p["g_pre_mix"], wfull["w_main"], wfull["w_f"], bf_pad, p["g_out_a"],
                                  p["g_out_b"], wfull["w_out"], p["g_post_mix"], tabs, nb,
                                  gather=None if last_weights is None else last_weights[:2])
    if last_weights is not None:
        wfull = {**wfull, **last_weights[2](gathered)}
    (x3, h2, gate2, up2, y02), _ = ffn_fwd(x2, mod_ff2, p["g_pre_ff2"], p["g_post_ff2"], wfull["w_ff2_gate"],
                                           wfull["w_ff2_up"], wfull["w_ff2_down"], 0.5, name="ff2_fwd")

    (dx2, dy02, act2, dgate2, dup2, dmod_ff2, dgpre2, dgpost2), (loss_part,) = ffn_bwd(
        x3, x2, y02, mod_ff2, p["g_pre_ff2"], p["g_post_ff2"], gate2, up2, wfull["w_ff2_gate"], wfull["w_ff2_up"],
        wfull["w_ff2_down"], 0.5, name="ff2_bwd", target=tgt)
    gw = {}
    (gw["w_ff2_gate"], gw["w_ff2_up"], gw["w_ff2_down"]), _ = ffn_grads(h2, dy02, act2, dgate2, dup2, "ff2")
    dx1, dmod_mix, gmix = mixer_bwd(dx2, x1, mod_mix, p["g_pre_mix"], wfull["w_main"], wfull["w_f"], bf_pad, p["g_out_a"],
                                    p["g_out_b"], wfull["w_out"], p["g_post_mix"], tabs, res, nb)
    gw["w_in"], gw["w_out"] = gmix["w_in"], gmix["w_out"]
    (dx0, dy01, act1, dgate1, dup1, dmod_ff1, dgpre1, dgpost1), scattered = ffn_bwd(
        dx1, x0, y01, mod_ff1, p["g_pre_ff1"], p["g_post_ff1"], gate1, up1, wfull["w_ff1_gate"], wfull["w_ff1_up"],
        wfull["w_ff1_down"], 0.5, name="ff1_bwd", scatter=None if early_grads is None else early_grads(gw))
    (gw["w_ff1_gate"], gw["w_ff1_up"], gw["w_ff1_down"]), chained = ffn_grads(h1, dy01, act1, dgate1, dup1, "ff1", last_reduce)
    dmod = jnp.concatenate([dmod_ff1, dmod_mix, dmod_ff2], axis=1).reshape(nb, 9 * D)
    small = dict(g_pre_ff1=dgpre1, g_post_ff1=dgpost1, g_pre_mix=gmix["g_pre"], g_post_mix=gmix["g_post"], g_pre_ff2=dgpre2,
                 g_post_ff2=dgpost2, g_out_a=gmix["goa"], g_out_b=gmix["gob"], b_forget=gmix["b_forget"])
    return loss_part, dx0, dmod, gw, small, scattered, chained


def kernel(x, c, positions, w_ada, b_ada, g_pre_ff1, g_post_ff1, w_ff1_gate, w_ff1_up, w_ff1_down, g_pre_mix, g_post_mix, w_in, b_forget, g_out_a, g_out_b, w_out, g_pre_ff2, g_post_ff2, w_ff2_gate, w_ff2_up, w_ff2_down, loss_target, m_w_ada, m_b_ada, m_g_pre_ff1, m_g_post_ff1, m_w_ff1_gate, m_w_ff1_up, m_w_ff1_down, m_g_pre_mix, m_g_post_mix, m_w_in, m_b_forget, m_g_out_a, m_g_out_b, m_w_out, m_g_pre_ff2, m_g_post_ff2, m_w_ff2_gate, m_w_ff2_up, m_w_ff2_down, v_w_ada, v_b_ada, v_g_pre_ff1, v_g_post_ff1, v_w_ff1_gate, v_w_ff1_up, v_w_ff1_down, v_g_pre_mix, v_g_post_mix, v_w_in, v_b_forget, v_g_out_a, v_g_out_b, v_w_out, v_g_pre_ff2, v_g_post_ff2, v_w_ff2_gate, v_w_ff2_up, v_w_ff2_down):
    args = dict(locals())
    nb = x.shape[0]
    T = nb * SEQ
    ax, ay, ac = lax.axis_index("x"), lax.axis_index("y"), lax.axis_index("c")
    shard = 2 * ax + ay
    cidx = jnp.reshape(ac, (1,)).astype(jnp.int32)
    sidx = jnp.reshape(shard, (1,)).astype(jnp.int32)

    big = ["w_ff1_gate", "w_ff1_up", "w_ff1_down", "w_in", "w_out", "w_ff2_gate", "w_ff2_up", "w_ff2_down"]
    vecs = ["g_pre_ff1", "g_post_ff1", "g_pre_mix", "g_post_mix", "g_pre_ff2", "g_post_ff2"]

    first, late = big[:3], big[3:]
    splits = {n: -(-(args[n].shape[1] // 2) // BF16_ROW_TILE) * BF16_ROW_TILE for n in big}

    def assemble(names, gathered):
        out = {}
        for n, o in zip(names, gathered):
            if n.endswith("gate") or n.endswith("up"):
                out[n] = _unshard_cols(o, DFF_PAD)
            elif n.endswith("down"):
                out[n] = jnp.pad(o.reshape(DFF, D), ((0, DFF_PAD - DFF), (0, 0)))
            elif n == "w_in":
                full = _unshard_cols(o, IN_COLS)
                out["w_main"] = full[:, :IN_MAIN]
                out["w_f"] = jnp.pad(full[:, IN_MAIN:], ((0, 0), (0, LANE - NH)))
            else:
                out[n] = o.reshape(D, D)
        return out

    wfull = assemble(first, all_gather_shards([args[n][0].astype(BF16) for n in first], [splits[n] for n in first],
                                              name="all_gather_weights"))
    def gather_plan(names):
        return ([args[n][0].astype(BF16) for n in names], [splits[n] for n in names], functools.partial(assemble, names))

    late_weights, last_weights = gather_plan(late[:2]), gather_plan(late[2:])

    ncol = w_ada.shape[2]
    c_all = all_gather8(c, name="all_gather_c").reshape(N_DEV * nb, D)
    b_loc = lax.dynamic_slice(b_ada, (0, shard * ncol), (1, ncol))
    mod_loc = ada_fwd(c_all, w_ada[0], b_loc, name="ada_fwd")
    mod_g = all_gather8(mod_loc, name="all_gather_mod")
    row0 = (4 * ax + 2 * ay + ac) * nb
    mod_rows = lax.dynamic_slice(mod_g, (0, row0, 0), (N_DEV, nb, ncol))
    mod = jnp.concatenate([mod_rows[2 * s] for s in range(N_SHARD)], axis=-1).reshape(nb, 9, D)

    small_in = dict(g_pre_ff1=g_pre_ff1, g_post_ff1=g_post_ff1, g_pre_mix=g_pre_mix, g_post_mix=g_post_mix, g_pre_ff2=g_pre_ff2,
                    g_post_ff2=g_post_ff2, g_out_a=g_out_a, g_out_b=g_out_b, b_forget=b_forget)
    def shard_blocked(n, g):
        if n.endswith("gate") or n.endswith("up"):
            return _shard_cols(g, DFF)
        if n.endswith("down"):
            return g[:DFF].reshape(N_SHARD, DFF // N_SHARD, D)
        if n == "w_in":
            return _shard_cols(g, IN_COLS)
        return g.reshape(N_SHARD, D // N_SHARD, D)

    def chip_sums(names, gw, tag):
        gsb = [shard_blocked(n, gw[n]) for n in names]
        ras = sibling_send_half(gsb, name="grad_sibling_send_" + tag)
        return pair_sums(gsb, ras, cidx, name="grad_pair_sum_" + tag)

    hs = {}

    def early_grads(gw):
        hs.update(zip(late, chip_sums(late, gw, "late")))
        return [hs[n] for n in late]

    def last_reduce(which, g):
        return chip_sums(["w_ff1_" + which], {"w_ff1_" + which: g}, which)

    loss_part, dx0, dmod, gw, small, rbs_late, chained = local_step(
        x.reshape(T, D), loss_target.reshape(T, D), positions.reshape(T, 1), mod, wfull, small_in, late_weights, last_weights,
        early_grads, last_reduce)

    dmod_all = all_gather8(dmod, name="all_gather_dmod").reshape(N_DEV * nb, 9 * D)
    dmod_loc = lax.dynamic_slice(dmod_all, (0, shard * ncol), (N_DEV * nb, ncol))
    g_w_ada = ada_bwd(c_all, dmod_loc, name="ada_bwd")

    rbs = dict(zip(late, rbs_late))
    for which, (h, rb) in chained.items():
        hs["w_ff1_" + which], rbs["w_ff1_" + which] = h, rb
    hs["w_ff1_down"] = chip_sums(["w_ff1_down"], gw, "down")[0]
    rbs["w_ff1_down"] = chip_scatter([hs["w_ff1_down"]], name="grad_chip_scatter")[0]
    ghs = []
    for part, names in enumerate((big[:4], big[4:])):
        ghs += chip_sums_total([hs[n] for n in names], [rbs[n] for n in names], sidx, name=f"grad_chip_sum_{part}")
    theirs = sibling_swap(ghs, name="grad_sibling_swap")

    row6 = jnp.concatenate([small["g_out_a"], small["g_out_b"]], axis=1)
    row7 = jnp.concatenate([small["b_forget"], loss_part[0:1, 0:1], jnp.zeros((1, D - NH - 1), F32)], axis=1)
    pack = jnp.concatenate([small[n] for n in vecs] + [row6, row7], axis=0)
    packed = all_gather8(pack, name="all_gather_small").reshape(N_DEV, 8 * D)

    names = vecs + ["g_out_a", "g_out_b", "b_forget"]
    layout = [(i * D, D) for i in range(len(vecs))] + [(6 * D, WG), (6 * D + WG, WG), (7 * D, NH)]
    per_param, packed_sum = small_adam(packed, layout, [args[n] for n in names], [args["m_" + n] for n in names],
                                       [args["v_" + n] for n in names], name="adam_small")
    outs = dict(grad={}, delta={}, new_m={}, new_v={})
    for n, (g, d, m2, v2) in zip(names, per_param):
        outs["grad"][n], outs["delta"][n], outs["new_m"][n], outs["new_v"][n] = g, d, m2, v2
    loss = packed_sum[0, 7 * D + NH]
    outs["grad"]["b_ada"], outs["delta"]["b_ada"], outs["new_m"]["b_ada"], outs["new_v"]["b_ada"] = vec_adam(
        dmod_all, b_ada, m_b_ada, v_b_ada, name="adam_b_ada")

    for n, mine, other in zip(big, ghs, theirs):
        tr = 128 if mine.shape[0] % 128 == 0 else mine.shape[0]
        outs["grad"][n], outs["delta"][n], outs["new_m"][n], outs["new_v"][n] = adam_update_halves(
            args[n], mine, other, args["m_" + n], args["v_" + n], cidx, tr, name="adam_" + n)
    outs["delta"]["w_ada"], outs["new_m"]["w_ada"], outs["new_v"]["w_ada"] = adam_update(
        w_ada, g_w_ada, m_w_ada, v_w_ada, 128, name="adam_w_ada")
    outs["grad"]["w_ada"] = g_w_ada[None]

    order = ["w_ada", "b_ada", "g_pre_ff1", "g_post_ff1", "w_ff1_gate", "w_ff1_up", "w_ff1_down", "g_pre_mix", "g_post_mix", "w_in",
             "b_forget", "g_out_a", "g_out_b", "w_out", "g_pre_ff2", "g_post_ff2", "w_ff2_gate", "w_ff2_up", "w_ff2_down"]
    result = [loss, dx0.reshape(nb, SEQ, D)]
    for kind in ("grad", "delta", "new_m", "new_v"):
        result += [outs[kind][n] for n in order]
    return tuple(result)
```

```python
import functools
import math

import jax
import jax.numpy as jnp
from jax import lax
from jax.experimental import pallas as pl
from jax.experimental.pallas import tpu as pltpu

D = 1024
SEQ = 2048
HD = 64
NH = 8
WG = NH * HD
DFF = 2752
DFF_PAD = 2816
IN_MAIN = 6 * WG
IN_COLS = IN_MAIN + NH
N_SHARD = 4
N_DEV = 8
LANE = 128
BF16_ROW_TILE = 16
QB = 128
ROWS = 256
FB = 512
FT = 512
FOX_QB = 512
FOX_PAIRS = 4
FOX_PAIRS_BWD = 2
BAND_UNROLL = 8
BAND_UNROLL_BWD = 8
PATTERNS = ((1, 16), (4, 4), (16, 1))
ROPE_THETA = 500000.0
EPS = 1e-6
NEG = -1e30
ATTN_SCALE = HD ** -0.5
TM = 512
TM_FFN = 512
TM_BWD = 256
VMEM_LIMIT = 56 * 1024 * 1024

ADAM_LR, ADAM_B1, ADAM_B2, ADAM_EPS, ADAM_WD, ADAM_STEP = 0.001, 0.9, 0.999, 1e-08, 0.01, 10

F32 = jnp.float32
BF16 = jnp.bfloat16
MESH = pl.DeviceIdType.MESH
SDS = jax.ShapeDtypeStruct


def _cp(*sem):
    return pltpu.CompilerParams(dimension_semantics=sem, vmem_limit_bytes=VMEM_LIMIT)


def _dot(a, b):
    return jnp.dot(a, b, preferred_element_type=F32)


def _dot_nt(a, b):
    return lax.dot_general(a, b, (((1,), (1,)), ((), ())), preferred_element_type=F32)


def _dot_tn(a, b):
    return lax.dot_general(a, b, (((0,), (0,)), ((), ())), preferred_element_type=F32)


def _rms(xf):
    return lax.rsqrt(jnp.mean(xf * xf, axis=-1, keepdims=True) + EPS)


def _norm_mod_bwd(dh, xf, g, scale):
    r = _rms(xf)
    xh = xf * r
    dsh = jnp.sum(dh, axis=0, keepdims=True)
    dsc = jnp.sum(dh * (xh * g), axis=0, keepdims=True)
    dn = dh * (1.0 + scale)
    dg = jnp.sum(dn * xh, axis=0, keepdims=True)
    dxh = dn * g
    dx = r * (dxh - xh * jnp.mean(dxh * xh, axis=-1, keepdims=True))
    return dx, dsh, dsc, dg


def _post_bwd(dxo, y0, g, mgate, gs):
    r = _rms(y0)
    yh = y0 * r
    dmg = gs * jnp.sum(dxo * (yh * g), axis=0, keepdims=True)
    dy = (gs * mgate) * dxo
    dg = jnp.sum(dy * yh, axis=0, keepdims=True)
    dyh = dy * g
    dy0 = r * (dyh - yh * jnp.mean(dyh * yh, axis=-1, keepdims=True))
    return dy0, dmg, dg


def _mod_map(i, *_):
    return ((i * TM) // SEQ, 0, 0)


FF_TN = 1408
FF_TILES = ((0, 1536), (1536, 2816))


def _resident_scratch():
    return [pltpu.VMEM((D, DFF_PAD), BF16), pltpu.VMEM((D, DFF_PAD), BF16), pltpu.VMEM((DFF_PAD, D), BF16),
            pltpu.SemaphoreType.DMA((3,))]


def _load_resident(first_step, srcs, dsts, sems):
    @pl.when(first_step)
    def _():
        cps = [pltpu.make_async_copy(s, d, sems.at[k]) for k, (s, d) in enumerate(zip(srcs, dsts))]
        for cp in cps:
            cp.start()
        for cp in cps:
            cp.wait()


def ffn_fwd(x, mod3, g_pre, g_post, wg, wu, wd, gs, name, gather=None):
    T = x.shape[0]
    tm = TM_FFN
    ng = 0 if gather is None else len(gather[0])
    plan = None if gather is None else ShardGather([w.shape for w in gather[0]], gather[1])

    def body(*refs):
        x_ref, mod_ref, gpre_ref, gpost_ref = refs[:4]
        xo_ref, h_ref, gate_ref, up_ref, y0_ref = refs[7 + ng:12 + ng]
        wg_ref, wu_ref, wd_ref, wsem = refs[12 + 2 * ng:16 + 2 * ng]
        i = pl.program_id(0)
        if plan is not None:
            comm = (refs[7:7 + ng], refs[12 + ng:12 + 2 * ng], refs[16 + 2 * ng:])
            pl.when(i == 0)(lambda: plan.start(*comm))
        _load_resident(i == 0, refs[4:7], (wg_ref, wu_ref, wd_ref), wsem)

        xf = x_ref[...]
        hb = ((xf * _rms(xf) * gpre_ref[...]) * (1.0 + mod_ref[0, 1:2, :]) + mod_ref[0, 0:1, :]).astype(BF16)
        h_ref[...] = hb
        y0 = None
        for lo, hi in FF_TILES:
            gate = _dot(hb, wg_ref[:, lo:hi])
            up = _dot(hb, wu_ref[:, lo:hi])
            gate_ref[:, lo:hi] = gate.astype(BF16)
            up_ref[:, lo:hi] = up.astype(BF16)
            part = _dot((gate * jax.nn.sigmoid(gate) * up).astype(BF16), wd_ref[lo:hi, :])
            y0 = part if y0 is None else y0 + part
        y0_ref[...] = y0
        xo_ref[...] = xf + (gs * mod_ref[0, 2:3, :]) * (y0 * _rms(y0) * gpost_ref[...])

        if plan is not None:
            pl.when(i == T // tm - 1)(lambda: plan.finish(*comm))

    tok = pl.BlockSpec((tm, D), lambda i: (i, 0))
    vec = pl.BlockSpec((1, D), lambda i: (0, 0))
    hid = pl.BlockSpec((tm, DFF_PAD), lambda i: (i, 0))
    outs = pl.pallas_call(
        body, grid=(T // tm,),
        in_specs=[tok, pl.BlockSpec((1, 3, D), lambda i: ((i * tm) // SEQ, 0, 0)), vec, vec, HBM, HBM, HBM] + [HBM] * ng,
        out_specs=[tok, tok, hid, hid, tok] + [HBM] * ng,
        out_shape=[SDS((T, D), F32), SDS((T, D), BF16), SDS((T, DFF_PAD), BF16), SDS((T, DFF_PAD), BF16), SDS((T, D), F32)]
        + ([] if plan is None else plan.out_shapes(BF16)),
        scratch_shapes=_resident_scratch() + ([] if plan is None else plan.scratch()),
        compiler_params=_cp("arbitrary"), name=name,
    )(x, mod3, g_pre, g_post, wg, wu, wd, *([] if gather is None else gather[0]))
    return outs[:5], outs[5:]


def ffn_bwd(dxo, x, y0, mod3, g_pre, g_post, gate, up, wg, wu, wd, gs, name, scatter=None, target=None):
    assert scatter is None or target is None
    T = x.shape[0]
    nb = T // SEQ
    tm = TM_BWD
    tiles_per_seq = SEQ // tm
    ns = 0 if scatter is None else len(scatter)
    ne = ns + (target is not None)

    def body(*refs):
        dxo_ref, x_ref, y0_ref, mod_ref, gpre_ref, gpost_ref, gate_ref, up_ref = refs[:8]
        dx_ref, dy0_ref, act_ref, dgate_ref, dup_ref, dmod_ref, dgpre_ref, dgpost_ref = refs[11 + ne:19 + ne]
        wg_ref, wu_ref, wd_ref, wsem = refs[19 + 2 * ne:23 + 2 * ne]
        i = pl.program_id(0)
        _load_resident(i == 0, refs[8:11], (wg_ref, wu_ref, wd_ref), wsem)
        if ns:
            comm = (refs[11:11 + ns], refs[19 + ns:19 + 2 * ns], *refs[23 + 2 * ns:])

            @pl.when(i == 0)
            def _():
                for cp in _scatter_copies(*comm):
                    cp.start()

        @pl.when(i == 0)
        def _():
            dgpre_ref[...] = jnp.zeros_like(dgpre_ref)
            dgpost_ref[...] = jnp.zeros_like(dgpost_ref)

        @pl.when(i % tiles_per_seq == 0)
        def _():
            dmod_ref[...] = jnp.zeros_like(dmod_ref)

        dxo = dxo_ref[...]
        if target is not None:
            loss_ref = refs[19 + ne]

            @pl.when(i == 0)
            def _():
                loss_ref[...] = jnp.zeros_like(loss_ref)

            err = dxo - refs[11][...]
            loss_ref[...] += jnp.sum(err * err) * (0.5 / D)
            dxo = err * (1.0 / D)
        dy0, dmg, dg = _post_bwd(dxo, y0_ref[...], gpost_ref[...], mod_ref[0, 2:3, :], gs)
        dmod_ref[0, 2:3, :] += dmg
        dgpost_ref[...] += dg
        db = dy0.astype(BF16)
        dy0_ref[...] = db
        dh = None
        for lo, hi in FF_TILES:
            dact = _dot_nt(db, wd_ref[lo:hi, :])
            g = gate_ref[:, lo:hi].astype(F32)
            u = up_ref[:, lo:hi].astype(F32)
            sig = jax.nn.sigmoid(g)
            sl = g * sig
            dgate = (dact * u * (sig * (1.0 + g * (1.0 - sig)))).astype(BF16)
            dup = (dact * sl).astype(BF16)
            act_ref[:, lo:hi] = (sl * u).astype(BF16)
            dgate_ref[:, lo:hi] = dgate
            dup_ref[:, lo:hi] = dup
            part = _dot_nt(dgate, wg_ref[:, lo:hi]) + _dot_nt(dup, wu_ref[:, lo:hi])
            dh = part if dh is None else dh + part
        dx, dsh, dsc, dg = _norm_mod_bwd(dh, x_ref[...], gpre_ref[...], mod_ref[0, 1:2, :])
        dx_ref[...] = dxo + dx
        dmod_ref[0, 0:1, :] += dsh
        dmod_ref[0, 1:2, :] += dsc
        dgpre_ref[...] += dg

        if ns:
            @pl.when(i == T // tm - 1)
            def _():
                for cp in _scatter_copies(*comm):
                    cp.wait()

    tok = pl.BlockSpec((tm, D), lambda i: (i, 0))
    vec = pl.BlockSpec((1, D), lambda i: (0, 0))
    hid = pl.BlockSpec((tm, DFF_PAD), lambda i: (i, 0))
    modspec = pl.BlockSpec((1, 3, D), lambda i: ((i * tm) // SEQ, 0, 0))
    outs = pl.pallas_call(
        body, grid=(T // tm,),
        in_specs=[tok, tok, tok, modspec, vec, vec, hid, hid, HBM, HBM, HBM] + [HBM] * ns + [tok] * (ne - ns),
        out_specs=[tok, tok, hid, hid, hid, modspec, vec, vec] + [HBM] * ns
        + [pl.BlockSpec((8, LANE), lambda i: (0, 0))] * (ne - ns),
        out_shape=[SDS((T, D), F32), SDS((T, D), BF16), SDS((T, DFF_PAD), BF16), SDS((T, DFF_PAD), BF16),
                   SDS((T, DFF_PAD), BF16), SDS((nb, 3, D), F32), SDS((1, D), F32), SDS((1, D), F32)]
        + [SDS((3,) + h.shape[1:], h.dtype) for h in (scatter or [])] + [SDS((8, LANE), F32)] * (ne - ns),
        scratch_shapes=_resident_scratch()
        + ([pltpu.SemaphoreType.DMA((ns, 3)), pltpu.SemaphoreType.DMA((ns, 3))] if ns else []),
        compiler_params=_cp("arbitrary"), name=name,
    )(dxo, x, y0, mod3, g_pre, g_post, gate, up, wg, wu, wd, *(scatter or []), *([] if target is None else [target]))
    return outs[:8], outs[8:]


def matmul_tn(a, b, tm, tn, tk, name, scatter=None):
    T, M = a.shape
    N = b.shape[1]
    grid = (M // tm, N // tn, T // tk)
    ns = 0 if scatter is None else len(scatter)

    def body(*refs):
        a_ref, b_ref = refs[:2]
        o_ref = refs[2 + ns]
        ids = [pl.program_id(ax) for ax in range(3)]
        if ns:
            comm = (refs[2:2 + ns], refs[3 + ns:3 + 2 * ns], *refs[3 + 2 * ns:])

            @pl.when((ids[0] == 0) & (ids[1] == 0) & (ids[2] == 0))
            def _():
                for cp in _scatter_copies(*comm):
                    cp.start()

        @pl.when(ids[2] == 0)
        def _():
            o_ref[...] = jnp.zeros_like(o_ref)

        o_ref[...] += _dot_tn(a_ref[...], b_ref[...])

        if ns:
            @pl.when((ids[0] == grid[0] - 1) & (ids[1] == grid[1] - 1) & (ids[2] == grid[2] - 1))
            def _():
                for cp in _scatter_copies(*comm):
                    cp.wait()

    outs = pl.pallas_call(
        body, grid=grid,
        in_specs=[pl.BlockSpec((tk, tm), lambda i, j, k: (k, i)), pl.BlockSpec((tk, tn), lambda i, j, k: (k, j))] + [HBM] * ns,
        out_specs=[pl.BlockSpec((tm, tn), lambda i, j, k: (i, j))] + [HBM] * ns,
        out_shape=[SDS((M, N), F32)] + [SDS((3,) + h.shape[1:], h.dtype) for h in (scatter or [])],
        scratch_shapes=[pltpu.SemaphoreType.DMA((ns, 3)), pltpu.SemaphoreType.DMA((ns, 3))] if ns else [],
        compiler_params=_cp("arbitrary", "arbitrary", "arbitrary"), name=name,
    )(a, b, *(scatter or []))
    return outs[0] if scatter is None else (outs[0], outs[1:])


def matmul_tn_cols(a, bs, tk, name):
    T, M = a.shape
    n = bs[0].shape[1]
    ng = len(bs)

    def body(*refs):
        a_ref, b_refs, o_ref = refs[0], refs[1:1 + ng], refs[1 + ng]

        @pl.when(pl.program_id(0) == 0)
        def _():
            o_ref[...] = jnp.zeros_like(o_ref)

        av = a_ref[...]
        for g, b_ref in enumerate(b_refs):
            o_ref[:, g * n:(g + 1) * n] += _dot_tn(av, b_ref[...])

    return pl.pallas_call(
        body, grid=(T // tk,),
        in_specs=[pl.BlockSpec((tk, M), lambda k: (k, 0))] + [pl.BlockSpec((tk, n), lambda k: (k, 0))] * ng,
        out_specs=pl.BlockSpec((M, ng * n), lambda k: (0, 0)), out_shape=SDS((M, ng * n), F32),
        compiler_params=_cp("arbitrary"), name=name,
    )(a, *bs)


def rope_tables(pos_col, name):
    T = pos_col.shape[0]
    tm = 1024

    def body(p_ref, c_ref, s1_ref, s2_ref):
        lane = lax.broadcasted_iota(jnp.int32, (1, LANE), 1)
        l64 = lane % HD
        inv_freq = jnp.exp((l64 % 8).astype(F32) * (-math.log(ROPE_THETA) / 8.0))
        ang = p_ref[...].astype(F32) * inv_freq
        cs = jnp.cos(ang)
        sn = jnp.sin(ang)
        c_ref[...] = jnp.where(l64 < 16, cs, 1.0)
        s1_ref[...] = jnp.where(l64 < 8, -sn, 0.0)
        s2_ref[...] = jnp.where((l64 >= 8) & (l64 < 16), sn, 0.0)

    tab = pl.BlockSpec((tm, LANE), lambda i: (i, 0))
    return pl.pallas_call(
        body, grid=(T // tm,), in_specs=[pl.BlockSpec((tm, 1), lambda i: (i, 0))], out_specs=[tab, tab, tab],
        out_shape=[SDS((T, LANE), F32)] * 3, compiler_params=_cp("arbitrary"), name=name,
    )(pos_col)


def mixer_proj(x, mod3, g_pre, w_main, w_f, rc, rs1, rs2, name):
    T = x.shape[0]

    def body(x_ref, mod_ref, g_ref, w_ref, wf_ref, c_ref, s1_ref, s2_ref, h_ref, pa_ref, pb_ref, f_ref):
        xf = x_ref[...]
        h = (xf * _rms(xf) * g_ref[...]) * (1.0 + mod_ref[0, 1:2, :]) + mod_ref[0, 0:1, :]
        hb = h.astype(BF16)
        h_ref[...] = hb
        f_ref[...] = _dot(hb, wf_ref[...])
        c, s1, s2 = c_ref[...], s1_ref[...], s2_ref[...]
        for grp in range(2):
            pr = _dot(hb, w_ref[:, grp * WG:(grp + 1) * WG])
            for k in range(WG // LANE):
                t = pr[:, k * LANE:(k + 1) * LANE]
                pa_ref[:, grp * WG + k * LANE:grp * WG + (k + 1) * LANE] = (
                    t * c + pltpu.roll(t, LANE - 8, 1) * s1 + pltpu.roll(t, 8, 1) * s2)
        pa_ref[:, 2 * WG:3 * WG] = _dot(hb, w_ref[:, 2 * WG:3 * WG])
        for grp in range(3):
            pb_ref[:, grp * WG:(grp + 1) * WG] = _dot(hb, w_ref[:, (3 + grp) * WG:(4 + grp) * WG]).astype(BF16)

    tok = pl.BlockSpec((TM, D), lambda i: (i, 0))
    vec = pl.BlockSpec((1, D), lambda i: (0, 0))
    tab = pl.BlockSpec((TM, LANE), lambda i: (i, 0))
    grp3 = pl.BlockSpec((TM, 3 * WG), lambda i: (i, 0))
    return pl.pallas_call(
        body, grid=(T // TM,),
        in_specs=[tok, pl.BlockSpec((1, 3, D), _mod_map), vec, pl.BlockSpec((D, IN_MAIN), lambda i: (0, 0)),
                  pl.BlockSpec((D, LANE), lambda i: (0, 0)), tab, tab, tab],
        out_specs=[tok, grp3, grp3, tab],
        out_shape=[SDS((T, D), BF16), SDS((T, 3 * WG), F32), SDS((T, 3 * WG), BF16), SDS((T, LANE), F32)],
        compiler_params=_cp("arbitrary"), name=name,
    )(x, mod3, g_pre, w_main, w_f, rc, rs1, rs2)


def _head_lanes():
    return lax.broadcasted_iota(jnp.int32, (1, LANE), 1) < HD


def _pair(m0, a, b):
    return jnp.where(m0, a, b)


def _band_rows(i, d, nbc):
    if nbc == 1:
        return i, i, 0
    r, mb = i // nbc, i % nbc
    return r + mb * (QB * d), r + jnp.maximum(mb - 1, 0) * (QB * d), jnp.where(mb > 0, QB, 0)


def _rows(start, size, d):
    return pl.ds(pl.multiple_of(start, QB), size) if d == 1 else pl.ds(start, size, stride=d)


def _band_valid(span, off):
    rq = lax.broadcasted_iota(jnp.int32, (QB, span), 0)
    rel = lax.broadcasted_iota(jnp.int32, (QB, span), 1) - off
    return (rel <= rq) & (rel >= rq - QB)


def band_fwd(pa, name):
    T = pa.shape[0]
    B = T // SEQ
    NP = WG // LANE

    def body(q_ref, k_ref, v_ref, out_ref, lse_ref, o_s, l_s):
        m0 = _head_lanes()
        for pidx, (d, nbc) in enumerate(PATTERNS):
            span = QB if nbc == 1 else 2 * QB

            def blk(it, carry, pidx=pidx, d=d, nbc=nbc, span=span):
                ld = []
                for u in range(BAND_UNROLL):
                    qs, ks, off = _band_rows(it * BAND_UNROLL + u, d, nbc)
                    q = q_ref[_rows(qs, QB, d), :] * ATTN_SCALE
                    ld.append((qs, q, k_ref[_rows(ks, span, d), :].astype(BF16), v_ref[_rows(ks, span, d), :].astype(BF16),
                               _band_valid(span, off)))
                ss = [[jnp.where(valid, _dot_nt(jnp.where(mh, q, 0.0).astype(BF16), k), NEG) for mh in (m0, jnp.logical_not(m0))]
                      for _, q, k, _, valid in ld]
                ps = []
                for pair in ss:
                    row = []
                    for s in pair:
                        m = jnp.max(s, axis=-1, keepdims=True)
                        p = jnp.exp(s - m)
                        row.append((p.astype(BF16), jnp.sum(p, axis=-1, keepdims=True), m))
                    ps.append(row)
                pv = [[_dot(p, ld[u][3]) for p, _, _ in ps[u]] for u in range(BAND_UNROLL)]
                for u in range(BAND_UNROLL):
                    rows = _rows(ld[u][0], QB, d)
                    (_, l0, mx0), (_, l1, mx1) = ps[u]
                    o_s[pidx, rows, :] = _pair(m0, pv[u][0] / l0, pv[u][1] / l1)
                    l_s[pidx, rows, :] = _pair(m0, mx0 + jnp.log(l0), mx1 + jnp.log(l1))
                return carry

            lax.fori_loop(0, SEQ // QB // BAND_UNROLL, blk, 0)
        for c in range(SEQ // ROWS):
            sl = slice(c * ROWS, (c + 1) * ROWS)
            a, b, e = l_s[0, sl, :], l_s[1, sl, :], l_s[2, sl, :]
            m = jnp.maximum(jnp.maximum(a, b), e)
            L = m + jnp.log(jnp.exp(a - m) + jnp.exp(b - m) + jnp.exp(e - m))
            out_ref[sl, :] = jnp.exp(a - L) * o_s[0, sl, :] + jnp.exp(b - L) * o_s[1, sl, :] + jnp.exp(e - L) * o_s[2, sl, :]
            lse_ref[sl, :] = L

    blk_of = lambda g: pl.BlockSpec((SEQ, LANE), lambda b, hp, g=g: (b, g * NP + hp))
    return pl.pallas_call(
        body, grid=(B, NP), in_specs=[blk_of(0), blk_of(1), blk_of(2)], out_specs=[blk_of(0), blk_of(0)],
        out_shape=[SDS((T, WG), F32), SDS((T, WG), F32)],
        scratch_shapes=[pltpu.VMEM((3, SEQ, LANE), F32), pltpu.VMEM((3, SEQ, LANE), F32)],
        compiler_params=_cp("arbitrary", "arbitrary"), name=name,
    )(pa, pa, pa)


def _pair_rowsum(m0, prod):
    s0 = jnp.sum(jnp.where(m0, prod, 0.0), axis=-1, keepdims=True)
    return _pair(m0, s0, jnp.sum(prod, axis=-1, keepdims=True) - s0)


def band_bwd(pa, do, out, lse, rc, rs1, rs2, name):
    T = pa.shape[0]
    B = T // SEQ
    NP = WG // LANE

    def body(q_ref, k_ref, v_ref, do_ref, out_ref, l_ref, c_ref, s1_ref, s2_ref, dqo_ref, dko_ref, dvo_ref, d_s, dq_ref, dk_ref,
             dv_ref):
        m0 = _head_lanes()
        dq_ref[...] = jnp.zeros_like(dq_ref)
        dk_ref[...] = jnp.zeros_like(dk_ref)
        dv_ref[...] = jnp.zeros_like(dv_ref)
        for c in range(SEQ // ROWS):
            sl = slice(c * ROWS, (c + 1) * ROWS)
            d_s[sl, :] = _pair_rowsum(m0, do_ref[sl, :] * out_ref[sl, :])
        for d, nbc in PATTERNS:
            span = QB if nbc == 1 else 2 * QB

            def blk(it, carry, d=d, nbc=nbc, span=span):
                masks = (m0, jnp.logical_not(m0))
                ld = []
                for u in range(BAND_UNROLL_BWD):
                    qs, ks, off = _band_rows(it * BAND_UNROLL_BWD + u, d, nbc)
                    qrow, krow = _rows(qs, QB, d), _rows(ks, span, d)
                    ld.append(dict(qrow=qrow, krow=krow, q=q_ref[qrow, :] * ATTN_SCALE, k=k_ref[krow, :].astype(BF16),
                                   v=v_ref[krow, :].astype(BF16), do=do_ref[qrow, :], l=l_ref[qrow, :], dv=d_s[qrow, :],
                                   valid=_band_valid(span, off)))
                for t in ld:
                    t["qm"] = [jnp.where(mh, t["q"], 0.0).astype(BF16) for mh in masks]
                    t["dom"] = [jnp.where(mh, t["do"], 0.0).astype(BF16) for mh in masks]
                sd = [[(jnp.where(t["valid"], _dot_nt(t["qm"][h], t["k"]), NEG), _dot_nt(t["dom"][h], t["v"])) for h in range(2)]
                      for t in ld]
                pd = []
                for t, pair in zip(ld, sd):
                    row = []
                    for h, (s, dp) in enumerate(pair):
                        col = slice(h * HD, h * HD + 1)
                        p = jnp.exp(s - t["l"][:, col])
                        row.append((p.astype(BF16), (p * (dp - t["dv"][:, col])).astype(BF16)))
                    pd.append(row)
                gr = [(_dot(row[0][1], t["k"]), _dot(row[1][1], t["k"]),
                       _dot_tn(jnp.concatenate([row[0][1], row[1][1]], axis=0), jnp.concatenate(t["qm"], axis=0)),
                       _dot_tn(jnp.concatenate([row[0][0], row[1][0]], axis=0), jnp.concatenate(t["dom"], axis=0)))
                      for t, row in zip(ld, pd)]
                for t, (dq0, dq1, dk, dv) in zip(ld, gr):
                    dq_ref[t["qrow"], :] += _pair(m0, dq0, dq1) * ATTN_SCALE
                    dk_ref[t["krow"], :] += dk
                    dv_ref[t["krow"], :] += dv
                return carry

            lax.fori_loop(0, SEQ // QB // BAND_UNROLL_BWD, blk, 0)
        for c in range(SEQ // ROWS):
            sl = slice(c * ROWS, (c + 1) * ROWS)
            cc, s1, s2 = c_ref[sl, :], s1_ref[sl, :], s2_ref[sl, :]
            for acc, o_ref in ((dq_ref, dqo_ref), (dk_ref, dko_ref)):
                d = acc[sl, :]
                o_ref[sl, :] = (d * cc + pltpu.roll(d * s1, 8, 1) + pltpu.roll(d * s2, LANE - 8, 1)).astype(BF16)
            dvo_ref[sl, :] = dv_ref[sl, :].astype(BF16)

    blk_of = lambda g: pl.BlockSpec((SEQ, LANE), lambda b, hp, g=g: (b, g * NP + hp))
    tab = pl.BlockSpec((SEQ, LANE), lambda b, hp: (b, 0))
    return pl.pallas_call(
        body, grid=(B, NP), in_specs=[blk_of(0), blk_of(1), blk_of(2), blk_of(0), blk_of(0), blk_of(0), tab, tab, tab],
        out_specs=[blk_of(0)] * 3, out_shape=[SDS((T, WG), BF16)] * 3,
        scratch_shapes=[pltpu.VMEM((SEQ, LANE), F32)] * 4,
        compiler_params=_cp("arbitrary", "arbitrary"), name=name,
    )(pa, pa, pa, do, out, lse, rc, rs1, rs2)


def _tile_causal(nq, nk, q0, k0):
    r = lax.broadcasted_iota(jnp.int32, (nq, nk), 0)
    c = lax.broadcasted_iota(jnp.int32, (nq, nk), 1)
    return r + (q0 - k0) >= c


def _row_to_col(row):
    n = row.shape[1]
    return jnp.transpose(jnp.broadcast_to(row, (LANE, n)))[:, 0:1]


def _col_to_row(col):
    n = col.shape[0]
    return jnp.transpose(jnp.broadcast_to(col, (n, LANE)))[0:1, :]


def fox_fwd(pb, fblk, frow, name, gather=None):
    FQ = FOX_QB
    T = pb.shape[0]
    B = T // SEQ
    NG = WG // (LANE * FOX_PAIRS)
    NHS = 2 * FOX_PAIRS
    W = LANE * FOX_PAIRS
    n = SEQ // FQ
    ng = 0 if gather is None else len(gather[0])
    plan = None if gather is None else ShardGather([w.shape for w in gather[0]], gather[1])

    def body(*refs):
        q_ref, k_ref, v_ref, fc_ref, fr_ref = refs[:5]
        o_ref, lse_ref = refs[5 + ng:7 + ng]
        if plan is not None:
            comm = (refs[5:5 + ng], refs[7 + ng:7 + 2 * ng], refs[7 + 2 * ng:])
            ids = [pl.program_id(ax) for ax in range(3)]
            pl.when((ids[0] == 0) & (ids[1] == 0) & (ids[2] == 0))(lambda: plan.start(*comm))
        i = pl.program_id(2)
        m0 = _head_lanes()
        masks = (m0, jnp.logical_not(m0))
        heads = [(hh, slice((hh // 2) * LANE, (hh // 2 + 1) * LANE), masks[hh % 2]) for hh in range(NHS)]
        qh, fq = [], []
        for hh, lanes, mh in heads:
            q = q_ref[:, lanes] * ATTN_SCALE
            qh.append(jnp.where(mh, q, jnp.zeros_like(q)))
            fq.append(_row_to_col(fc_ref[0, hh, 0]))

        def step(t, carry, masked):
            rows = pl.ds(pl.multiple_of(t * FT, FT), FT)
            ss = [_dot_nt(qh[hh], k_ref[rows, lanes]) + fq[hh] - fr_ref[0, hh, t] for hh, lanes, _ in heads]
            if masked:
                ok = _tile_causal(FQ, FT, i * FQ, t * FT)
                ss = [jnp.where(ok, s, NEG) for s in ss]
            st = []
            for hh, _, _ in heads:
                m2 = jnp.maximum(carry[hh][0], jnp.max(ss[hh], axis=-1, keepdims=True))
                st.append((m2, jnp.exp(carry[hh][0] - m2), jnp.exp(ss[hh] - m2).astype(BF16)))
            pv = []
            for hh, lanes, mh in heads:
                vt = v_ref[rows, lanes]
                pv.append(_dot(st[hh][2], jnp.where(mh, vt, jnp.ones_like(vt))))
            return tuple((st[hh][0], st[hh][1] * carry[hh][1] + pv[hh]) for hh in range(NHS))

        one = (jnp.full((FQ, 1), NEG, F32), jnp.zeros((FQ, LANE), F32))
        last = (i * FQ) // FT
        carry = lax.fori_loop(0, last, lambda t, cr: step(t, cr, False), (one,) * NHS)
        carry = step(last, carry, True)
        for pr in range(FOX_PAIRS):
            (ma, acca), (mb, accb) = carry[2 * pr], carry[2 * pr + 1]
            la, lb = acca[:, HD:HD + 1], accb[:, 0:1]
            o_ref[:, pr * LANE:(pr + 1) * LANE] = _pair(m0, acca / la, accb / lb)
            lse_ref[0, 2 * pr, 0] = _col_to_row(ma + jnp.log(la))
            lse_ref[0, 2 * pr + 1, 0] = _col_to_row(mb + jnp.log(lb))
        if plan is not None:
            pl.when((ids[0] == B - 1) & (ids[1] == NG - 1) & (ids[2] == n - 1))(lambda: plan.finish(*comm))

    qblk = pl.BlockSpec((FQ, W), lambda b, g, i: (b * n + i, g))
    full = lambda grp: pl.BlockSpec((SEQ, W), lambda b, g, i, grp=grp: (b, grp * NG + g))
    rowb = pl.BlockSpec((1, NHS, 1, 1, FQ), lambda b, g, i: (b, g, i, 0, 0))
    outs = pl.pallas_call(
        body, grid=(B, NG, n),
        in_specs=[qblk, full(1), full(2), rowb, pl.BlockSpec((1, NHS, SEQ // FT, 1, FT), lambda b, g, i: (b, g, 0, 0, 0))]
        + [HBM] * ng,
        out_specs=[qblk, rowb] + [HBM] * ng,
        out_shape=[SDS((T, WG), F32), SDS((B, NH, n, 1, FQ), F32)] + ([] if plan is None else plan.out_shapes(BF16)),
        scratch_shapes=[] if plan is None else plan.scratch(),
        compiler_params=_cp("arbitrary", "arbitrary", "arbitrary"), name=name,
    )(pb, pb, pb, fblk, frow, *([] if gather is None else gather[0]))
    return outs[:2], outs[2:]


def fox_bwd(pb, do, lrow, drow, fblk, frow, name):
    T = pb.shape[0]
    B = T // SEQ
    PAIRS = FOX_PAIRS_BWD
    NG = WG // (LANE * PAIRS)
    NHS = 2 * PAIRS
    W = LANE * PAIRS
    n = SEQ // FB

    def body(q_ref, k_ref, v_ref, do_ref, l_ref, d_ref, fc_ref, fr_ref, dqo_ref, dk_ref, dv_ref, dfq_ref, dfk_ref, dq_ref):
        j = pl.program_id(2)
        m0 = _head_lanes()
        masks = (m0, jnp.logical_not(m0))
        heads = [(hh, slice((hh // 2) * LANE, (hh // 2 + 1) * LANE), masks[hh % 2]) for hh in range(NHS)]

        @pl.when(j == 0)
        def _():
            dq_ref[...] = jnp.zeros_like(dq_ref)
            dfq_ref[...] = jnp.zeros_like(dfq_ref)

        kj = [k_ref[:, lanes] for _, lanes, _ in heads]
        vj = [v_ref[:, lanes] for _, lanes, _ in heads]
        fk = [_row_to_col(fc_ref[0, hh, 0]) for hh in range(NHS)]

        def step(t, carry, masked):
            rows = pl.ds(pl.multiple_of(t * FT, FT), FT)
            qm, dom = [], []
            for _, lanes, mh in heads:
                qt = q_ref[rows, lanes] * ATTN_SCALE
                qm.append(jnp.where(mh, qt, jnp.zeros_like(qt)))
                dom.append(jnp.where(mh, do_ref[rows, lanes], 0.0).astype(BF16))
            ss = [_dot_nt(kj[hh], qm[hh]) + fr_ref[0, hh, t] - fk[hh] for hh in range(NHS)]
            dps = [_dot_nt(vj[hh], dom[hh]) for hh in range(NHS)]
            if masked:
                key = lax.broadcasted_iota(jnp.int32, (FB, FT), 0)
                qry = lax.broadcasted_iota(jnp.int32, (FB, FT), 1)
                ok = qry + (t * FT - j * FB) >= key
                ss = [jnp.where(ok, s, NEG) for s in ss]
            pds = []
            for hh in range(NHS):
                p = jnp.exp(ss[hh] - l_ref[0, hh, t])
                ds = p * (dps[hh] - d_ref[0, hh, t])
                dfq_ref[0, hh, t] += jnp.sum(ds, axis=0, keepdims=True)
                pds.append((p.astype(BF16), ds.astype(BF16), jnp.sum(ds, axis=-1, keepdims=True)))
            dks = [_dot(pds[hh][1], qm[hh]) for hh in range(NHS)]
            dvs = [_dot(pds[hh][0], dom[hh]) for hh in range(NHS)]
            dqs = [_dot_tn(pds[hh][1], kj[hh]) for hh in range(NHS)]
            for pr in range(PAIRS):
                dq_ref[rows, pr * LANE:(pr + 1) * LANE] += _pair(m0, dqs[2 * pr], dqs[2 * pr + 1]) * ATTN_SCALE
            return tuple((carry[hh][0] + dks[hh], carry[hh][1] + dvs[hh], carry[hh][2] - pds[hh][2]) for hh in range(NHS))

        one = (jnp.zeros((FB, LANE), F32), jnp.zeros((FB, LANE), F32), jnp.zeros((FB, 1), F32))
        first = (j * FB) // FT
        carry = step(first, (one,) * NHS, True)
        carry = lax.fori_loop(first + 1, SEQ // FT, lambda t, cr: step(t, cr, False), carry)
        for pr in range(PAIRS):
            (dka, dva, dfka), (dkb, dvb, dfkb) = carry[2 * pr], carry[2 * pr + 1]
            dk_ref[:, pr * LANE:(pr + 1) * LANE] = _pair(m0, dka, dkb).astype(BF16)
            dv_ref[:, pr * LANE:(pr + 1) * LANE] = _pair(m0, dva, dvb).astype(BF16)
            dfk_ref[0, 2 * pr, 0] = _col_to_row(dfka)
            dfk_ref[0, 2 * pr + 1, 0] = _col_to_row(dfkb)

        @pl.when(j == n - 1)
        def _():
            dqo_ref[...] = dq_ref[...].astype(BF16)

    kblk = lambda grp: pl.BlockSpec((FB, W), lambda b, g, j, grp=grp: (b * n + j, grp * NG + g))
    full = pl.BlockSpec((SEQ, W), lambda b, g, j: (b, g))
    rowf = pl.BlockSpec((1, NHS, SEQ // FT, 1, FT), lambda b, g, j: (b, g, 0, 0, 0))
    rowb = pl.BlockSpec((1, NHS, 1, 1, FB), lambda b, g, j: (b, g, j, 0, 0))
    return pl.pallas_call(
        body, grid=(B, NG, n), in_specs=[full, kblk(1), kblk(2), full, rowf, rowf, rowb, rowf],
        out_specs=[full, kblk(0), kblk(0), rowf, rowb],
        out_shape=[SDS((T, WG), BF16), SDS((T, WG), BF16), SDS((T, WG), BF16), SDS((B, NH, SEQ // FT, 1, FT), F32),
                   SDS((B, NH, n, 1, FB), F32)],
        scratch_shapes=[pltpu.VMEM((SEQ, W), F32)],
        compiler_params=_cp("arbitrary", "arbitrary", "arbitrary"), name=name,
    )(pb, pb, pb, do, lrow, drow, fblk, frow)


def _tri(lower):
    r = lax.broadcasted_iota(jnp.int32, (LANE, LANE), 0)
    c = lax.broadcasted_iota(jnp.int32, (LANE, LANE), 1)
    return ((r >= c) if lower else (r <= c)).astype(F32)


def _tri_dot(t, xblk):
    return jnp.dot(t, xblk, precision=lax.Precision.HIGHEST, preferred_element_type=F32)


def forget_cumsum(flog, bias, name):
    B, S, _ = flog.shape

    def body(f_ref, b_ref, o_ref):
        t = _tri(True)
        carry = jnp.zeros((1, LANE), F32)
        for blk in range(S // LANE):
            z = f_ref[0, blk * LANE:(blk + 1) * LANE, :] + b_ref[...]
            lf = jnp.minimum(z, 0.0) - jnp.log(1.0 + jnp.exp(-jnp.abs(z)))
            cs = _tri_dot(t, lf) + carry
            o_ref[0, blk * LANE:(blk + 1) * LANE, :] = cs
            carry = cs[LANE - 1:LANE, :]

    spec = pl.BlockSpec((1, S, LANE), lambda b: (b, 0, 0))
    return pl.pallas_call(
        body, grid=(B,), in_specs=[spec, pl.BlockSpec((1, LANE), lambda b: (0, 0))], out_specs=spec,
        out_shape=SDS((B, S, LANE), F32), compiler_params=_cp("arbitrary"), name=name,
    )(flog, bias)


def forget_cumsum_bwd(dF, flog, bias, name):
    B, S, _ = flog.shape

    def body(d_ref, f_ref, b_ref, o_ref, db_ref):
        @pl.when(pl.program_id(0) == 0)
        def _():
            db_ref[...] = jnp.zeros_like(db_ref)

        t = _tri(False)
        carry = jnp.zeros((1, LANE), F32)
        tot = jnp.zeros((1, LANE), F32)
        for blk in reversed(range(S // LANE)):
            sl = slice(blk * LANE, (blk + 1) * LANE)
            rc = _tri_dot(t, d_ref[0, sl, :]) + carry
            carry = rc[0:1, :]
            z = f_ref[0, sl, :] + b_ref[...]
            dz = rc * jax.nn.sigmoid(-z)
            o_ref[0, sl, :] = dz
            tot = tot + jnp.sum(dz, axis=0, keepdims=True)
        db_ref[...] += tot

    spec = pl.BlockSpec((1, S, LANE), lambda b: (b, 0, 0))
    vec = pl.BlockSpec((1, LANE), lambda b: (0, 0))
    return pl.pallas_call(
        body, grid=(B,), in_specs=[spec, spec, vec], out_specs=[spec, vec],
        out_shape=[SDS((B, S, LANE), F32), SDS((1, LANE), F32)], compiler_params=_cp("arbitrary"), name=name,
    )(dF, flog, bias)


def mixer_out_fwd(oa, ob, goa, gob, w_out, g_post, x, mod3, name):
    T = x.shape[0]

    def body(oa_ref, ob_ref, goa_ref, gob_ref, w_ref, gp_ref, x_ref, mod_ref, xo_ref, mg_ref, y0_ref):
        a = oa_ref[...]
        b = ob_ref[...]
        mg = jnp.concatenate([a * _rms(a) * goa_ref[...], b * _rms(b) * gob_ref[...]], axis=-1).astype(BF16)
        mg_ref[...] = mg
        y0 = _dot(mg, w_ref[...])
        y0_ref[...] = y0
        xo_ref[...] = x_ref[...] + mod_ref[0, 2:3, :] * (y0 * _rms(y0) * gp_ref[...])

    tok = pl.BlockSpec((TM, D), lambda i: (i, 0))
    half = pl.BlockSpec((TM, WG), lambda i: (i, 0))
    hv = pl.BlockSpec((1, WG), lambda i: (0, 0))
    return pl.pallas_call(
        body, grid=(T // TM,),
        in_specs=[half, half, hv, hv, pl.BlockSpec((D, D), lambda i: (0, 0)), pl.BlockSpec((1, D), lambda i: (0, 0)), tok,
                  pl.BlockSpec((1, 3, D), _mod_map)],
        out_specs=[tok, tok, tok], out_shape=[SDS((T, D), F32), SDS((T, D), BF16), SDS((T, D), F32)],
        compiler_params=_cp("arbitrary"), name=name,
    )(oa, ob, goa, gob, w_out, g_post, x, mod3)


def mixer_out_bwd(dxo, y0, mod3, g_post, w_out, oa, ob, goa, gob, name):
    T = dxo.shape[0]
    nb = T // SEQ
    tiles_per_seq = SEQ // TM

    def body(dxo_ref, y0_ref, mod_ref, gp_ref, w_ref, oa_ref, ob_ref, goa_ref, gob_ref,
             dy0_ref, doa_ref, dob_ref, dmg_ref, dgp_ref, dgoa_ref, dgob_ref, dvb_ref):
        i = pl.program_id(0)

        @pl.when(i == 0)
        def _():
            dgp_ref[...] = jnp.zeros_like(dgp_ref)
            dgoa_ref[...] = jnp.zeros_like(dgoa_ref)
            dgob_ref[...] = jnp.zeros_like(dgob_ref)

        @pl.when(i % tiles_per_seq == 0)
        def _():
            dmg_ref[...] = jnp.zeros_like(dmg_ref)

        dy0, dmg, dg = _post_bwd(dxo_ref[...], y0_ref[...], gp_ref[...], mod_ref[0, 2:3, :], 1.0)
        dmg_ref[0] += dmg
        dgp_ref[...] += dg
        db = dy0.astype(BF16)
        dy0_ref[...] = db
        dm = _dot_nt(db, w_ref[...])
        for o_ref, g_ref, do_ref, dg_ref, sl in ((oa_ref, goa_ref, doa_ref, dgoa_ref, slice(0, WG)),
                                                  (ob_ref, gob_ref, dob_ref, dgob_ref, slice(WG, 2 * WG))):
            o = o_ref[...]
            r = _rms(o)
            oh = o * r
            d = dm[:, sl]
            dg_ref[...] += jnp.sum(d * oh, axis=0, keepdims=True)
            dh = d * g_ref[...]
            do = r * (dh - oh * jnp.mean(dh * oh, axis=-1, keepdims=True))
            do_ref[...] = do
        ind = (lax.broadcasted_iota(jnp.int32, (WG, LANE), 0) // HD == lax.broadcasted_iota(jnp.int32, (WG, LANE), 1)).astype(BF16)
        prod = do * o
        hi = prod.astype(BF16)
        dvb_ref[...] = _dot(hi, ind) + _dot((prod - hi.astype(F32)).astype(BF16), ind)

    tok = pl.BlockSpec((TM, D), lambda i: (i, 0))
    half = pl.BlockSpec((TM, WG), lambda i: (i, 0))
    hv = pl.BlockSpec((1, WG), lambda i: (0, 0))
    vec = pl.BlockSpec((1, D), lambda i: (0, 0))
    return pl.pallas_call(
        body, grid=(T // TM,),
        in_specs=[tok, tok, pl.BlockSpec((1, 3, D), _mod_map), vec, pl.BlockSpec((D, D), lambda i: (0, 0)), half, half, hv, hv],
        out_specs=[tok, half, half, pl.BlockSpec((1, 1, D), _mod_map), vec, hv, hv, pl.BlockSpec((TM, LANE), lambda i: (i, 0))],
        out_shape=[SDS((T, D), BF16), SDS((T, WG), F32), SDS((T, WG), F32), SDS((nb, 1, D), F32), SDS((1, D), F32),
                   SDS((1, WG), F32), SDS((1, WG), F32), SDS((T, LANE), F32)],
        compiler_params=_cp("arbitrary"), name=name,
    )(dxo, y0, mod3, g_post, w_out, oa, ob, goa, gob)


def mixer_proj_bwd(dps, dflog, dxo, x, mod3, g_pre, w_main, w_f, name):
    T = x.shape[0]
    nb = T // SEQ
    tiles_per_seq = SEQ // TM
    ngrp = len(dps)

    def body(*refs):
        dp_refs = refs[:ngrp]
        df_ref, dxo_ref, x_ref, mod_ref, g_ref, w_ref, wf_ref, dx_ref, dmod_ref, dg_ref = refs[ngrp:]
        i = pl.program_id(0)

        @pl.when(i == 0)
        def _():
            dg_ref[...] = jnp.zeros_like(dg_ref)

        @pl.when(i % tiles_per_seq == 0)
        def _():
            dmod_ref[...] = jnp.zeros_like(dmod_ref)

        dh = _dot_nt(df_ref[...].astype(BF16), wf_ref[...])
        for g, dp_ref in enumerate(dp_refs):
            dh = dh + _dot_nt(dp_ref[...], w_ref[:, g * WG:(g + 1) * WG])
        dx, dsh, dsc, dg = _norm_mod_bwd(dh, x_ref[...], g_ref[...], mod_ref[0, 1:2, :])
        dx_ref[...] = dxo_ref[...] + dx
        dmod_ref[0, 0:1, :] += dsh
        dmod_ref[0, 1:2, :] += dsc
        dg_ref[...] += dg

    tok = pl.BlockSpec((TM, D), lambda i: (i, 0))
    vec = pl.BlockSpec((1, D), lambda i: (0, 0))
    return pl.pallas_call(
        body, grid=(T // TM,),
        in_specs=[pl.BlockSpec((TM, WG), lambda i: (i, 0))] * ngrp
        + [pl.BlockSpec((TM, LANE), lambda i: (i, 0)), tok, tok, pl.BlockSpec((1, 3, D), _mod_map), vec,
           pl.BlockSpec((D, IN_MAIN), lambda i: (0, 0)), pl.BlockSpec((D, LANE), lambda i: (0, 0))],
        out_specs=[tok, pl.BlockSpec((1, 2, D), _mod_map), vec],
        out_shape=[SDS((T, D), F32), SDS((nb, 2, D), F32), SDS((1, D), F32)],
        compiler_params=_cp("arbitrary"), name=name,
    )(*dps, dflog, dxo, x, mod3, g_pre, w_main, w_f)


def ada_fwd(c_all, w, b, name):
    n = w.shape[1]
    tn = n // 2

    def body(c_ref, w_ref, b_ref, o_ref):
        cv = c_ref[...]
        o_ref[...] = _dot((cv * jax.nn.sigmoid(cv)).astype(BF16), w_ref[...].astype(BF16)) + b_ref[...]

    R = c_all.shape[0]
    return pl.pallas_call(
        body, grid=(2,),
        in_specs=[pl.BlockSpec((R, D), lambda j: (0, 0)), pl.BlockSpec((D, tn), lambda j: (0, j)), pl.BlockSpec((1, tn), lambda j: (0, j))],
        out_specs=pl.BlockSpec((R, tn), lambda j: (0, j)), out_shape=SDS((R, n), F32),
        compiler_params=_cp("arbitrary"), name=name,
    )(c_all, w, b)


def ada_bwd(c_all, dmod, name):
    R, n = dmod.shape
    tn = n // 2

    def body(c_ref, d_ref, o_ref):
        cv = c_ref[...]
        o_ref[...] = _dot_tn((cv * jax.nn.sigmoid(cv)).astype(BF16), d_ref[...].astype(BF16))

    return pl.pallas_call(
        body, grid=(2,), in_specs=[pl.BlockSpec((R, D), lambda j: (0, 0)), pl.BlockSpec((R, tn), lambda j: (0, j))],
        out_specs=pl.BlockSpec((D, tn), lambda j: (0, j)), out_shape=SDS((D, n), F32),
        compiler_params=_cp("arbitrary"), name=name,
    )(c_all, dmod)


def _adam_math(w, g, m, v):
    m2 = ADAM_B1 * m + (1.0 - ADAM_B1) * g
    v2 = ADAM_B2 * v + (1.0 - ADAM_B2) * (g * g)
    m_hat = m2 / (1.0 - ADAM_B1 ** ADAM_STEP)
    v_hat = v2 / (1.0 - ADAM_B2 ** ADAM_STEP)
    delta = -ADAM_LR * (m_hat / (jnp.sqrt(v_hat) + ADAM_EPS) + ADAM_WD * w)
    return delta, m2, v2


def adam_update(w, g, m, v, tr, name):
    _, R, C = w.shape

    def body(w_ref, g_ref, m_ref, v_ref, d_ref, mo_ref, vo_ref):
        d_ref[0], mo_ref[0], vo_ref[0] = _adam_math(w_ref[0], g_ref[...], m_ref[0], v_ref[0])

    spec = pl.BlockSpec((1, tr, C), lambda i: (0, i, 0))
    gspec = pl.BlockSpec((tr, C), lambda i: (i, 0))
    return pl.pallas_call(
        body, grid=(R // tr,), in_specs=[spec, gspec, spec, spec], out_specs=[spec] * 3, out_shape=[SDS((1, R, C), F32)] * 3,
        compiler_params=_cp("arbitrary"), name=name,
    )(w, g, m, v)


def adam_update_halves(w, mine, other, m, v, cidx, tr, name):
    _, R, C = w.shape
    nh = R // 2 // tr

    def body(c_ref, w_ref, a_ref, b_ref, m_ref, v_ref, g_ref, d_ref, mo_ref, vo_ref):
        first_half = pl.program_id(0) < nh
        g = jnp.where(first_half == (c_ref[0] == 0), a_ref[...], b_ref[...])
        g_ref[0] = g
        d_ref[0], mo_ref[0], vo_ref[0] = _adam_math(w_ref[0], g, m_ref[0], v_ref[0])

    spec = pl.BlockSpec((1, tr, C), lambda i, c_ref: (0, i, 0))
    hspec = pl.BlockSpec((tr, C), lambda i, c_ref: (i % nh, 0))
    return pl.pallas_call(
        body,
        grid_spec=pltpu.PrefetchScalarGridSpec(num_scalar_prefetch=1, grid=(R // tr,), in_specs=[spec, hspec, hspec, spec, spec],
                                               out_specs=[spec] * 4),
        out_shape=[SDS((1, R, C), F32)] * 4, compiler_params=_cp("arbitrary"), name=name,
    )(cidx, w, mine, other, m, v)


def vec_adam(parts, w, m, v, name):
    P, C = parts.shape

    def body(p_ref, w_ref, m_ref, v_ref, g_ref, d_ref, mo_ref, vo_ref):
        g = jnp.sum(p_ref[...], axis=0, keepdims=True)
        g_ref[...] = g
        d_ref[...], mo_ref[...], vo_ref[...] = _adam_math(w_ref[...], g, m_ref[...], v_ref[...])

    return pl.pallas_call(body, out_shape=[SDS((1, C), F32)] * 4, compiler_params=_cp(), name=name)(parts, w, m, v)


def small_adam(parts, layout, ws, ms, vs, name):
    P, C = parts.shape
    k = len(layout)

    def body(*refs):
        p_ref = refs[0]
        w_refs, m_refs, v_refs = refs[1:1 + k], refs[1 + k:1 + 2 * k], refs[1 + 2 * k:1 + 3 * k]
        outs = refs[1 + 3 * k:]
        g_all = jnp.sum(p_ref[...], axis=0, keepdims=True)
        outs[4 * k][...] = g_all
        for n, (off, width) in enumerate(layout):
            g = g_all[:, off:off + width]
            outs[4 * n][...] = g
            outs[4 * n + 1][...], outs[4 * n + 2][...], outs[4 * n + 3][...] = _adam_math(
                w_refs[n][...], g, m_refs[n][...], v_refs[n][...])

    shapes = [SDS((1, width), F32) for _, width in layout for _ in range(4)] + [SDS((1, C), F32)]
    res = pl.pallas_call(body, out_shape=shapes, compiler_params=_cp(), name=name)(parts, *ws, *ms, *vs)
    return [tuple(res[4 * n:4 * n + 4]) for n in range(k)], res[4 * k]


HBM = pl.BlockSpec(memory_space=pltpu.HBM)
VMEM = pl.BlockSpec(memory_space=pltpu.VMEM)


def _place():
    x, y, c = lax.axis_index("x"), lax.axis_index("y"), lax.axis_index("c")
    return x, y, c, [(1 - x, y), (x, 1 - y), (1 - x, 1 - y)]


def all_gather8(xs, name):
    R, C = xs.shape

    def body(x_ref, out_ref, send_sems, recv_sems, local_sem):
        x, y, c, chips = _place()
        me, sibling = (x, y, c), (x, y, 1 - c)

        def slot(px, py, pc):
            return out_ref.at[4 * px + 2 * py + pc]

        def copy(k, block, to, src=None):
            return pltpu.make_async_remote_copy(
                src_ref=slot(*block) if src is None else src, dst_ref=slot(*block),
                send_sem=send_sems.at[k], recv_sem=recv_sems.at[k], device_id=to, device_id_type=MESH)

        mine = pltpu.make_async_copy(x_ref, slot(*me), local_sem)
        mine.start()
        first = [copy(0, me, sibling, src=x_ref)]
        first += [copy(1 + j, me, (*chip, c), src=x_ref) for j, chip in enumerate(chips)]
        for cp in first:
            cp.start()
        passed = [copy(4 + j, (*chip, c), sibling) for j, chip in enumerate(chips)]
        for j, chip in enumerate(chips):
            copy(1 + j, (*chip, c), me).wait_recv()
            passed[j].start()
        copy(0, sibling, me).wait_recv()
        for j, chip in enumerate(chips):
            copy(4 + j, (*chip, 1 - c), me).wait_recv()
        for cp in first + passed:
            cp.wait_send()
        mine.wait()

    return pl.pallas_call(
        body, out_shape=SDS((N_DEV, R, C), xs.dtype), in_specs=[VMEM], out_specs=VMEM,
        scratch_shapes=[pltpu.SemaphoreType.DMA((7,)), pltpu.SemaphoreType.DMA((7,)), pltpu.SemaphoreType.DMA],
        compiler_params=pltpu.CompilerParams(vmem_limit_bytes=VMEM_LIMIT), name=name,
    )(xs)


class ShardGather:
    def __init__(self, shapes, splits):
        self.shapes, self.splits, self.n = shapes, splits, len(shapes)

    def scratch(self):
        n = self.n
        return [pltpu.SemaphoreType.DMA((n, 6)), pltpu.SemaphoreType.DMA((n, 6)), pltpu.SemaphoreType.DMA((n,))]

    def out_shapes(self, dtype):
        return [SDS((N_SHARD,) + tuple(s), dtype) for s in self.shapes]

    def _half(self, ref, k, cc):
        lo, hi = (0, self.splits[k]) if cc == 0 else (self.splits[k], self.shapes[k][0])
        return ref.at[pl.ds(lo, hi - lo)]

    def _phase(self, w_refs, o_refs, sems, finish):
        send_sems, recv_sems, local_sems = sems
        x, y, c, chips = _place()
        sibling = (x, y, 1 - c)
        me_s = 2 * x + y

        def rcopy(src, dst, k, s, to):
            return pltpu.make_async_remote_copy(src_ref=src, dst_ref=dst, send_sem=send_sems.at[k, s],
                                                recv_sem=recv_sems.at[k, s], device_id=to, device_id_type=MESH)

        for cc in (0, 1):
            @pl.when(c == cc)
            def _():
                local = [pltpu.make_async_copy(w_refs[k], o_refs[k].at[me_s], local_sems.at[k]) for k in range(self.n)]
                first = [rcopy(self._half(w_refs[k], k, cc), self._half(o_refs[k].at[me_s], k, cc), k, j, (*chip, c))
                         for k in range(self.n) for j, chip in enumerate(chips)]
                if not finish:
                    for cp in local + first:
                        cp.start()
                    return
                passed = []
                for k in range(self.n):
                    for j, chip in enumerate(chips):
                        land = self._half(o_refs[k].at[2 * chip[0] + chip[1]], k, cc)
                        rcopy(land, land, k, j, (*chip, c)).wait_recv()
                        f = rcopy(land, land, k, 3 + j, sibling)
                        f.start()
                        passed.append(f)
                for k in range(self.n):
                    for j, chip in enumerate(chips):
                        other = self._half(o_refs[k].at[2 * chip[0] + chip[1]], k, 1 - cc)
                        rcopy(other, other, k, 3 + j, sibling).wait_recv()
                for s in first + passed:
                    s.wait_send()
                for cp in local:
                    cp.wait()

    def start(self, w_refs, o_refs, sems):
        self._phase(w_refs, o_refs, sems, False)

    def finish(self, w_refs, o_refs, sems):
        self._phase(w_refs, o_refs, sems, True)


def all_gather_shards(ws, splits, name):
    n = len(ws)
    plan = ShardGather([w.shape for w in ws], splits)

    def body(*refs):
        plan.start(refs[:n], refs[n:2 * n], refs[2 * n:])
        plan.finish(refs[:n], refs[n:2 * n], refs[2 * n:])

    return pl.pallas_call(
        body, out_shape=plan.out_shapes(ws[0].dtype), in_specs=[HBM] * n, out_specs=[HBM] * n,
        scratch_shapes=plan.scratch(), name=name,
    )(*ws)


def sibling_send_half(gs, name):
    n = len(gs)

    def body(*refs):
        g_refs, o_refs = refs[:n], refs[n:2 * n]
        send_sems, recv_sems = refs[2 * n:]
        x, y, c, _ = _place()
        cps = []
        for k in range(n):
            hr = gs[k].shape[1] // 2
            src = g_refs[k].at[:, pl.ds(pl.multiple_of((1 - c) * hr, 8), hr)]
            cp = pltpu.make_async_remote_copy(src_ref=src, dst_ref=o_refs[k], send_sem=send_sems.at[k], recv_sem=recv_sems.at[k],
                                              device_id=(x, y, 1 - c), device_id_type=MESH)
            cp.start()
            cps.append(cp)
        for cp in cps:
            cp.wait()

    return pl.pallas_call(
        body, out_shape=[SDS((N_SHARD, g.shape[1] // 2, g.shape[2]), g.dtype) for g in gs], in_specs=[HBM] * n, out_specs=[HBM] * n,
        scratch_shapes=[pltpu.SemaphoreType.DMA((n,)), pltpu.SemaphoreType.DMA((n,))], name=name,
    )(*gs)


def _scatter_copies(h_refs, o_refs, send_sems, recv_sems):
    _, _, c, chips = _place()
    return [pltpu.make_async_remote_copy(
        src_ref=h_refs[k].at[2 * chip[0] + chip[1]], dst_ref=o_refs[k].at[j], send_sem=send_sems.at[k, j],
        recv_sem=recv_sems.at[k, j], device_id=(*chip, c), device_id_type=MESH)
        for k in range(len(h_refs)) for j, chip in enumerate(chips)]


def chip_scatter(hs, name):
    n = len(hs)

    def body(*refs):
        cps = _scatter_copies(refs[:n], refs[n:2 * n], *refs[2 * n:])
        for cp in cps:
            cp.start()
        for cp in cps:
            cp.wait()

    return pl.pallas_call(
        body, out_shape=[SDS((3,) + h.shape[1:], h.dtype) for h in hs], in_specs=[HBM] * n, out_specs=[HBM] * n,
        scratch_shapes=[pltpu.SemaphoreType.DMA((n, 3)), pltpu.SemaphoreType.DMA((n, 3))], name=name,
    )(*hs)


def sibling_swap(ghs, name):
    n = len(ghs)

    def body(*refs):
        g_refs, o_refs = refs[:n], refs[n:2 * n]
        send_sems, recv_sems = refs[2 * n:]
        x, y, c, _ = _place()
        cps = []
        for k in range(n):
            cp = pltpu.make_async_remote_copy(src_ref=g_refs[k], dst_ref=o_refs[k], send_sem=send_sems.at[k],
                                              recv_sem=recv_sems.at[k], device_id=(x, y, 1 - c), device_id_type=MESH)
            cp.start()
            cps.append(cp)
        for cp in cps:
            cp.wait()

    return pl.pallas_call(
        body, out_shape=[SDS(g.shape, g.dtype) for g in ghs], in_specs=[HBM] * n, out_specs=[HBM] * n,
        scratch_shapes=[pltpu.SemaphoreType.DMA((n,)), pltpu.SemaphoreType.DMA((n,))], name=name,
    )(*ghs)


def pair_sums(gs, ras, cidx, name):
    n = len(gs)
    halves = [(g.shape[1] // 2, g.shape[2]) for g in gs]

    def body(c_ref, *refs):
        for g_ref, a_ref, o_ref in zip(refs[:n], refs[n:2 * n], refs[2 * n:]):
            o_ref[...] = (g_ref[...] + a_ref[...]).astype(BF16)

    mine = [pl.BlockSpec((1, hr, cols), lambda s, c_ref: (s, c_ref[0], 0)) for hr, cols in halves]
    whole = [pl.BlockSpec((1, hr, cols), lambda s, c_ref: (s, 0, 0)) for hr, cols in halves]
    return pl.pallas_call(
        body,
        grid_spec=pltpu.PrefetchScalarGridSpec(num_scalar_prefetch=1, grid=(N_SHARD,), in_specs=mine + whole, out_specs=whole),
        out_shape=[SDS((N_SHARD, hr, cols), BF16) for hr, cols in halves], compiler_params=_cp("arbitrary"), name=name,
    )(cidx, *gs, *ras)


def chip_sums_total(hs, rbs, sidx, name):
    n = len(hs)
    halves = [h.shape[1:] for h in hs]

    def body(s_ref, *refs):
        for h_ref, r_ref, o_ref in zip(refs[:n], refs[n:2 * n], refs[2 * n:]):
            o_ref[...] = ((h_ref[0].astype(F32) + r_ref[0].astype(F32)) + r_ref[1].astype(F32)) + r_ref[2].astype(F32)

    return pl.pallas_call(
        body,
        grid_spec=pltpu.PrefetchScalarGridSpec(
            num_scalar_prefetch=1, grid=(1,),
            in_specs=[pl.BlockSpec((1, hr, cols), lambda i, s_ref: (s_ref[0], 0, 0)) for hr, cols in halves]
            + [pl.BlockSpec((3, hr, cols), lambda i, s_ref: (0, 0, 0)) for hr, cols in halves],
            out_specs=[pl.BlockSpec((hr, cols), lambda i, s_ref: (0, 0)) for hr, cols in halves]),
        out_shape=[SDS((hr, cols), F32) for hr, cols in halves], compiler_params=_cp("arbitrary"), name=name,
    )(sidx, *hs, *rbs)


def _shard_cols(g, n_valid):
    r = g.shape[0]
    return g[:, :n_valid].reshape(r, N_SHARD, n_valid // N_SHARD).transpose(1, 0, 2)


def _unshard_cols(o, pad_to):
    _, r, n = o.shape
    full = o.transpose(1, 0, 2).reshape(r, N_SHARD * n)
    return jnp.pad(full, ((0, 0), (0, pad_to - N_SHARD * n)))


def _rows_of_tiles(t):
    B, H, S = t.shape
    return t.reshape(B, H, S // FT, 1, FT)


def mixer_fwd(x1, mod3, g_pre, w_main, w_f, b_forget_pad, goa, gob, w_out, g_post, tabs, nb, gather=None):
    hmix, pa, pb, flog = mixer_proj(x1, mod3, g_pre, w_main, w_f, *tabs, name="mixer_proj")
    out_a, lse_a = band_fwd(pa, name="band_fwd")
    F = forget_cumsum(flog.reshape(nb, SEQ, LANE), b_forget_pad, name="forget_cumsum")
    Fh = F[:, :, :NH].transpose(0, 2, 1)
    fblk = Fh.reshape(nb, NH, SEQ // FB, 1, FB)
    frow = _rows_of_tiles(Fh)
    (out_b, lse_b), gathered = fox_fwd(pb, Fh.reshape(nb, NH, SEQ // FOX_QB, 1, FOX_QB), frow, name="fox_fwd", gather=gather)
    x2, merged, y0m = mixer_out_fwd(out_a, out_b, goa, gob, w_out, g_post, x1, mod3, name="mixer_out_fwd")
    res = dict(hmix=hmix, flog=flog, pa=pa, pb=pb, out_a=out_a, lse_a=lse_a, fblk=fblk, frow=frow, out_b=out_b,
               lrow=_rows_of_tiles(lse_b.reshape(nb, NH, SEQ)), merged=merged, y0m=y0m)
    return x2, res, gathered


def mixer_bwd(dx2, x1, mod3, g_pre, w_main, w_f, b_forget_pad, goa, gob, w_out, g_post, tabs, res, nb):
    T = nb * SEQ
    dy0m, doa, dob, dmgate, dg_post, dgoa, dgob, dvec_b = mixer_out_bwd(
        dx2, res["y0m"], mod3, g_post, w_out, res["out_a"], res["out_b"], goa, gob, name="mixer_out_bwd")
    dqa, dka, dva = band_bwd(res["pa"], doa, res["out_a"], res["lse_a"], *tabs, name="band_bwd")
    drow = _rows_of_tiles(dvec_b[:, :NH].reshape(nb, SEQ, NH).transpose(0, 2, 1))
    dqb, dkb, dvb, dfq, dfk = fox_bwd(res["pb"], dob, res["lrow"], drow, res["fblk"], res["frow"], name="fox_bwd")
    dF = (dfq.reshape(nb, NH, SEQ) + dfk.reshape(nb, NH, SEQ)).transpose(0, 2, 1)
    dF = jnp.pad(dF, ((0, 0), (0, 0), (0, LANE - NH)))
    dflog, dbf = forget_cumsum_bwd(dF, res["flog"].reshape(nb, SEQ, LANE), b_forget_pad, name="forget_cumsum_bwd")
    dflog = dflog.reshape(T, LANE)
    dps = (dqa, dka, dva, dqb, dkb, dvb)
    dx1, dmod2, dg_pre = mixer_proj_bwd(dps, dflog, dx2, x1, mod3, g_pre, w_main, w_f, name="mixer_proj_bwd")
    g_main = matmul_tn_cols(res["hmix"], dps, 1024, name="grad_w_in")
    g_f = matmul_tn(res["hmix"], dflog.astype(BF16), D, LANE, 1024, name="grad_w_forget")
    g_out = matmul_tn(res["merged"], dy0m, D, D, 1024, name="grad_w_out")
    dmod3 = jnp.concatenate([dmod2, dmgate], axis=1)
    return dx1, dmod3, dict(g_pre=dg_pre, g_post=dg_post, goa=dgoa, gob=dgob, b_forget=dbf[:, :NH],
                            w_in=jnp.concatenate([g_main, g_f[:, :NH]], axis=1), w_out=g_out)


def ffn_grads(h, dy0, act, dgate, dup, pre, reduce=None):
    g_gate = matmul_tn(h, dgate, D, DFF_PAD, 1024, name=pre + "_grad_gate")
    if reduce is None:
        g_up = matmul_tn(h, dup, D, DFF_PAD, 1024, name=pre + "_grad_up")
        g_down = matmul_tn(act, dy0, FF_TN, D, 1024, name=pre + "_grad_down")
        return (g_gate, g_up, g_down), {}
    hs_gate = reduce("gate", g_gate)
    g_up, rb_gate = matmul_tn(h, dup, D, DFF_PAD, 1024, name=pre + "_grad_up", scatter=hs_gate)
    hs_up = reduce("up", g_up)
    g_down, rb_up = matmul_tn(act, dy0, FF_TN, D, 1024, name=pre + "_grad_down", scatter=hs_up)
    return (g_gate, g_up, g_down), {"gate": (hs_gate[0], rb_gate[0]), "up": (hs_up[0], rb_up[0])}


def local_step(x0, tgt, pos_col, mod, wfull, p, late_weights=None, last_weights=None, early_grads=None, last_reduce=None):
    T = x0.shape[0]
    nb = T // SEQ
    mod_ff1, mod_mix, mod_ff2 = mod[:, 0:3], mod[:, 3:6], mod[:, 6:9]
    tabs = rope_tables(pos_col, name="rope_tables")
    bf_pad = jnp.pad(p["b_forget"], ((0, 0), (0, LANE - NH)))

    (x1, h1, gate1, up1, y01), gathered = ffn_fwd(
        x0, mod_ff1, p["g_pre_ff1"], p["g_post_ff1"], wfull["w_ff1_gate"], wfull["w_ff1_up"], wfull["w_ff1_down"], 0.5,
        name="ff1_fwd", gather=None if late_weights is None else late_weights[:2])
    if late_weights is not None:
        wfull = {**wfull, **late_weights[2](gathered)}
    x2, res, gathered = mixer_fwd(x1, mod_mix, p["g_pre_mix"], wfull["w_main"], wfull["w_f"], bf_pad, p["g_out_a"],
                                  p["g_out_b"], wfull["w_out"], p["g_post_mix"], tabs, nb,
                                  gather=None if last_weights is None else last_weights[:2])
    if last_weights is not None:
        wfull = {**wfull, **last_weights[2](gathered)}
    (x3, h2, gate2, up2, y02), _ = ffn_fwd(x2, mod_ff2, p["g_pre_ff2"], p["g_post_ff2"], wfull["w_ff2_gate"],
                                           wfull["w_ff2_up"], wfull["w_ff2_down"], 0.5, name="ff2_fwd")

    (dx2, dy02, act2, dgate2, dup2, dmod_ff2, dgpre2, dgpost2), (loss_part,) = ffn_bwd(
        x3, x2, y02, mod_ff2, p["g_pre_ff2"], p["g_post_ff2"], gate2, up2, wfull["w_ff2_gate"], wfull["w_ff2_up"],
        wfull["w_ff2_down"], 0.5, name="ff2_bwd", target=tgt)
    gw = {}
    (gw["w_ff2_gate"], gw["w_ff2_up"], gw["w_ff2_down"]), _ = ffn_grads(h2, dy02, act2, dgate2, dup2, "ff2")
    dx1, dmod_mix, gmix = mixer_bwd(dx2, x1, mod_mix, p["g_pre_mix"], wfull["w_main"], wfull["w_f"], bf_pad, p["g_out_a"],
                                    p["g_out_b"], wfull["w_out"], p["g_post_mix"], tabs, res, nb)
    gw["w_in"], gw["w_out"] = gmix["w_in"], gmix["w_out"]
    (dx0, dy01, act1, dgate1, dup1, dmod_ff1, dgpre1, dgpost1), scattered = ffn_bwd(
        dx1, x0, y01, mod_ff1, p["g_pre_ff1"], p["g_post_ff1"], gate1, up1, wfull["w_ff1_gate"], wfull["w_ff1_up"],
        wfull["w_ff1_down"], 0.5, name="ff1_bwd", scatter=None if early_grads is None else early_grads(gw))
    (gw["w_ff1_gate"], gw["w_ff1_up"], gw["w_ff1_down"]), chained = ffn_grads(h1, dy01, act1, dgate1, dup1, "ff1", last_reduce)
    dmod = jnp.concatenate([dmod_ff1, dmod_mix, dmod_ff2], axis=1).reshape(nb, 9 * D)
    small = dict(g_pre_ff1=dgpre1, g_post_ff1=dgpost1, g_pre_mix=gmix["g_pre"], g_post_mix=gmix["g_post"], g_pre_ff2=dgpre2,
                 g_post_ff2=dgpost2, g_out_a=gmix["goa"], g_out_b=gmix["gob"], b_forget=gmix["b_forget"])
    return loss_part, dx0, dmod, gw, small, scattered, chained


def kernel(x, c, positions, w_ada, b_ada, g_pre_ff1, g_post_ff1, w_ff1_gate, w_ff1_up, w_ff1_down, g_pre_mix, g_post_mix, w_in, b_forget, g_out_a, g_out_b, w_out, g_pre_ff2, g_post_ff2, w_ff2_gate, w_ff2_up, w_ff2_down, loss_target, m_w_ada, m_b_ada, m_g_pre_ff1, m_g_post_ff1, m_w_ff1_gate, m_w_ff1_up, m_w_ff1_down, m_g_pre_mix, m_g_post_mix, m_w_in, m_b_forget, m_g_out_a, m_g_out_b, m_w_out, m_g_pre_ff2, m_g_post_ff2, m_w_ff2_gate, m_w_ff2_up, m_w_ff2_down, v_w_ada, v_b_ada, v_g_pre_ff1, v_g_post_ff1, v_w_ff1_gate, v_w_ff1_up, v_w_ff1_down, v_g_pre_mix, v_g_post_mix, v_w_in, v_b_forget, v_g_out_a, v_g_out_b, v_w_out, v_g_pre_ff2, v_g_post_ff2, v_w_ff2_gate, v_w_ff2_up, v_w_ff2_down):
    args = dict(locals())
    nb = x.shape[0]
    T = nb * SEQ
    ax, ay, ac = lax.axis_index("x"), lax.axis_index("y"), lax.axis_index("c")
    shard = 2 * ax + ay
    cidx = jnp.reshape(ac, (1,)).astype(jnp.int32)
    sidx = jnp.reshape(shard, (1,)).astype(jnp.int32)

    big = ["w_ff1_gate", "w_ff1_up", "w_ff1_down", "w_in", "w_out", "w_ff2_gate", "w_ff2_up", "w_ff2_down"]
    vecs = ["g_pre_ff1", "g_post_ff1", "g_pre_mix", "g_post_mix", "g_pre_ff2", "g_post_ff2"]

    first, late = big[:3], big[3:]
    splits = {n: -(-(args[n].shape[1] // 2) // BF16_ROW_TILE) * BF16_ROW_TILE for n in big}

    def assemble(names, gathered):
        out = {}
        for n, o in zip(names, gathered):
            if n.endswith("gate") or n.endswith("up"):
                out[n] = _unshard_cols(o, DFF_PAD)
            elif n.endswith("down"):
                out[n] = jnp.pad(o.reshape(DFF, D), ((0, DFF_PAD - DFF), (0, 0)))
            elif n == "w_in":
                full = _unshard_cols(o, IN_COLS)
                out["w_main"] = full[:, :IN_MAIN]
                out["w_f"] = jnp.pad(full[:, IN_MAIN:], ((0, 0), (0, LANE - NH)))
            else:
                out[n] = o.reshape(D, D)
        return out

    wfull = assemble(first, all_gather_shards([args[n][0].astype(BF16) for n in first], [splits[n] for n in first],
                                              name="all_gather_weights"))
    def gather_plan(names):
        return ([args[n][0].astype(BF16) for n in names], [splits[n] for n in names], functools.partial(assemble, names))

    late_weights, last_weights = gather_plan(late[:2]), gather_plan(late[2:])

    ncol = w_ada.shape[2]
    c_all = all_gather8(c, name="all_gather_c").reshape(N_DEV * nb, D)
    b_loc = lax.dynamic_slice(b_ada, (0, shard * ncol), (1, ncol))
    mod_loc = ada_fwd(c_all, w_ada[0], b_loc, name="ada_fwd")
    mod_g = all_gather8(mod_loc, name="all_gather_mod")
    row0 = (4 * ax + 2 * ay + ac) * nb
    mod_rows = lax.dynamic_slice(mod_g, (0, row0, 0), (N_DEV, nb, ncol))
    mod = jnp.concatenate([mod_rows[2 * s] for s in range(N_SHARD)], axis=-1).reshape(nb, 9, D)

    small_in = dict(g_pre_ff1=g_pre_ff1, g_post_ff1=g_post_ff1, g_pre_mix=g_pre_mix, g_post_mix=g_post_mix, g_pre_ff2=g_pre_ff2,
                    g_post_ff2=g_post_ff2, g_out_a=g_out_a, g_out_b=g_out_b, b_forget=b_forget)
    def shard_blocked(n, g):
        if n.endswith("gate") or n.endswith("up"):
            return _shard_cols(g, DFF)
        if n.endswith("down"):
            return g[:DFF].reshape(N_SHARD, DFF // N_SHARD, D)
        if n == "w_in":
            return _shard_cols(g, IN_COLS)
        return g.reshape(N_SHARD, D // N_SHARD, D)

    def chip_sums(names, gw, tag):
        gsb = [shard_blocked(n, gw[n]) for n in names]
        ras = sibling_send_half(gsb, name="grad_sibling_send_" + tag)
        return pair_sums(gsb, ras, cidx, name="grad_pair_sum_" + tag)

    hs = {}

    def early_grads(gw):
        hs.update(zip(late, chip_sums(late, gw, "late")))
        return [hs[n] for n in late]

    def last_reduce(which, g):
        return chip_sums(["w_ff1_" + which], {"w_ff1_" + which: g}, which)

    loss_part, dx0, dmod, gw, small, rbs_late, chained = local_step(
        x.reshape(T, D), loss_target.reshape(T, D), positions.reshape(T, 1), mod, wfull, small_in, late_weights, last_weights,
        early_grads, last_reduce)

    dmod_all = all_gather8(dmod, name="all_gather_dmod").reshape(N_DEV * nb, 9 * D)
    dmod_loc = lax.dynamic_slice(dmod_all, (0, shard * ncol), (N_DEV * nb, ncol))
    g_w_ada = ada_bwd(c_all, dmod_loc, name="ada_bwd")

    rbs = dict(zip(late, rbs_late))
    for which, (h, rb) in chained.items():
        hs["w_ff1_" + which], rbs["w_ff1_" + which] = h, rb
    hs["w_ff1_down"] = chip_sums(["w_ff1_down"], gw, "down")[0]
    rbs["w_ff1_down"] = chip_scatter([hs["w_ff1_down"]], name="grad_chip_scatter")[0]
    ghs = []
    for part, names in enumerate((big[:4], big[4:])):
        ghs += chip_sums_total([hs[n] for n in names], [rbs[n] for n in names], sidx, name=f"grad_chip_sum_{part}")
    theirs = sibling_swap(ghs, name="grad_sibling_swap")

    row6 = jnp.concatenate([small["g_out_a"], small["g_out_b"]], axis=1)
    row7 = jnp.concatenate([small["b_forget"], loss_part[0:1, 0:1], jnp.zeros((1, D - NH - 1), F32)], axis=1)
    pack = jnp.concatenate([small[n] for n in vecs] + [row6, row7], axis=0)
    packed = all_gather8(pack, name="all_gather_small").reshape(N_DEV, 8 * D)

    names = vecs + ["g_out_a", "g_out_b", "b_forget"]
    layout = [(i * D, D) for i in range(len(vecs))] + [(6 * D, WG), (6 * D + WG, WG), (7 * D, NH)]
    per_param, packed_sum = small_adam(packed, layout, [args[n] for n in names], [args["m_" + n] for n in names],
                                       [args["v_" + n] for n in names], name="adam_small")
    outs = dict(grad={}, delta={}, new_m={}, new_v={})
    for n, (g, d, m2, v2) in zip(names, per_param):
        outs["grad"][n], outs["delta"][n], outs["new_m"][n], outs["new_v"][n] = g, d, m2, v2
    loss = packed_sum[0, 7 * D + NH]
    outs["grad"]["b_ada"], outs["delta"]["b_ada"], outs["new_m"]["b_ada"], outs["new_v"]["b_ada"] = vec_adam(
        dmod_all, b_ada, m_b_ada, v_b_ada, name="adam_b_ada")

    for n, mine, other in zip(big, ghs, theirs):
        tr = 128 if mine.shape[0] % 128 == 0 else mine.shape[0]
        outs["grad"][n], outs["delta"][n], outs["new_m"][n], outs["new_v"][n] = adam_update_halves(
            args[n], mine, other, args["m_" + n], args["v_" + n], cidx, tr, name="adam_" + n)
    outs["delta"]["w_ada"], outs["new_m"]["w_ada"], outs["new_v"]["w_ada"] = adam_update(
        w_ada, g_w_ada, m_w_ada, v_w_ada, 128, name="adam_w_ada")
    outs["grad"]["w_ada"] = g_w_ada[None]

    order = ["w_ada", "b_ada", "g_pre_ff1", "g_post_ff1", "w_ff1_gate", "w_ff1_up", "w_ff1_down", "g_pre_mix", "g_post_mix", "w_in",
             "b_forget", "g_out_a", "g_out_b", "w_out", "g_pre_ff2", "g_post_ff2", "w_ff2_gate", "w_ff2_up", "w_ff2_down"]
    result = [loss, dx0.reshape(nb, SEQ, D)]
    for kind in ("grad", "delta", "new_m", "new_v"):
        result += [outs[kind][n] for n in order]
    return tuple(result)
```

```python
import functools
import math

import jax
import jax.numpy as jnp
from jax import lax
from jax.experimental import pallas as pl
from jax.experimental.pallas import tpu as pltpu

D = 1024
SEQ = 2048
HD = 64
NH = 8
WG = NH * HD
DFF = 2752
DFF_PAD = 2816
IN_MAIN = 6 * WG
IN_COLS = IN_MAIN + NH
N_SHARD = 4
N_DEV = 8
LANE = 128
BF16_ROW_TILE = 16
QB = 128
ROWS = 256
FB = 512
FT = 512
FOX_QB = 512
FOX_PAIRS = 4
FOX_PAIRS_BWD = 2
BAND_UNROLL = 8
BAND_UNROLL_BWD = 8
PATTERNS = ((1, 16), (4, 4), (16, 1))
ROPE_THETA = 500000.0
EPS = 1e-6
NEG = -1e30
ATTN_SCALE = HD ** -0.5
TM = 512
TM_FFN = 512
TM_BWD = 256
VMEM_LIMIT = 56 * 1024 * 1024

ADAM_LR, ADAM_B1, ADAM_B2, ADAM_EPS, ADAM_WD, ADAM_STEP = 0.001, 0.9, 0.999, 1e-08, 0.01, 10

F32 = jnp.float32
BF16 = jnp.bfloat16
MESH = pl.DeviceIdType.MESH
SDS = jax.ShapeDtypeStruct


def _cp(*sem):
    return pltpu.CompilerParams(dimension_semantics=sem, vmem_limit_bytes=VMEM_LIMIT)


def _dot(a, b):
    return jnp.dot(a, b, preferred_element_type=F32)


def _dot_nt(a, b):
    return lax.dot_general(a, b, (((1,), (1,)), ((), ())), preferred_element_type=F32)


def _dot_tn(a, b):
    return lax.dot_general(a, b, (((0,), (0,)), ((), ())), preferred_element_type=F32)


def _rms(xf):
    return lax.rsqrt(jnp.mean(xf * xf, axis=-1, keepdims=True) + EPS)


def _norm_mod_bwd(dh, xf, g, scale):
    r = _rms(xf)
    xh = xf * r
    dsh = jnp.sum(dh, axis=0, keepdims=True)
    dsc = jnp.sum(dh * (xh * g), axis=0, keepdims=True)
    dn = dh * (1.0 + scale)
    dg = jnp.sum(dn * xh, axis=0, keepdims=True)
    dxh = dn * g
    dx = r * (dxh - xh * jnp.mean(dxh * xh, axis=-1, keepdims=True))
    return dx, dsh, dsc, dg


def _post_bwd(dxo, y0, g, mgate, gs):
    r = _rms(y0)
    yh = y0 * r
    dmg = gs * jnp.sum(dxo * (yh * g), axis=0, keepdims=True)
    dy = (gs * mgate) * dxo
    dg = jnp.sum(dy * yh, axis=0, keepdims=True)
    dyh = dy * g
    dy0 = r * (dyh - yh * jnp.mean(dyh * yh, axis=-1, keepdims=True))
    return dy0, dmg, dg


def _mod_map(i, *_):
    return ((i * TM) // SEQ, 0, 0)


FF_TN = 1408
FF_TILES = ((0, 768), (768, 1536), (1536, 2304), (2304, 2816))


def _resident_scratch():
    return [pltpu.VMEM((D, DFF_PAD), BF16), pltpu.VMEM((D, DFF_PAD), BF16), pltpu.VMEM((DFF_PAD, D), BF16),
            pltpu.SemaphoreType.DMA((3,))]


def _load_resident(first_step, srcs, dsts, sems):
    @pl.when(first_step)
    def _():
        cps = [pltpu.make_async_copy(s, d, sems.at[k]) for k, (s, d) in enumerate(zip(srcs, dsts))]
        for cp in cps:
            cp.start()
        for cp in cps:
            cp.wait()


def ffn_fwd(x, mod3, g_pre, g_post, wg, wu, wd, gs, name, gather=None):
    T = x.shape[0]
    tm = TM_FFN
    ng = 0 if gather is None else len(gather[0])
    plan = None if gather is None else ShardGather([w.shape for w in gather[0]], gather[1])

    def body(*refs):
        x_ref, mod_ref, gpre_ref, gpost_ref = refs[:4]
        xo_ref, h_ref, gate_ref, up_ref, y0_ref = refs[7 + ng:12 + ng]
        wg_ref, wu_ref, wd_ref, wsem = refs[12 + 2 * ng:16 + 2 * ng]
        i = pl.program_id(0)
        if plan is not None:
            comm = (refs[7:7 + ng], refs[12 + ng:12 + 2 * ng], refs[16 + 2 * ng:])
            pl.when(i == 0)(lambda: plan.start(*comm))
        _load_resident(i == 0, refs[4:7], (wg_ref, wu_ref, wd_ref), wsem)

        xf = x_ref[...]
        hb = ((xf * _rms(xf) * gpre_ref[...]) * (1.0 + mod_ref[0, 1:2, :]) + mod_ref[0, 0:1, :]).astype(BF16)
        h_ref[...] = hb
        y0 = None
        for lo, hi in FF_TILES:
            gate = _dot(hb, wg_ref[:, lo:hi])
            up = _dot(hb, wu_ref[:, lo:hi])
            gate_ref[:, lo:hi] = gate.astype(BF16)
            up_ref[:, lo:hi] = up.astype(BF16)
            part = _dot((gate * jax.nn.sigmoid(gate) * up).astype(BF16), wd_ref[lo:hi, :])
            y0 = part if y0 is None else y0 + part
        y0_ref[...] = y0
        xo_ref[...] = xf + (gs * mod_ref[0, 2:3, :]) * (y0 * _rms(y0) * gpost_ref[...])

        if plan is not None:
            pl.when(i == T // tm - 1)(lambda: plan.finish(*comm))

    tok = pl.BlockSpec((tm, D), lambda i: (i, 0))
    vec = pl.BlockSpec((1, D), lambda i: (0, 0))
    hid = pl.BlockSpec((tm, DFF_PAD), lambda i: (i, 0))
    outs = pl.pallas_call(
        body, grid=(T // tm,),
        in_specs=[tok, pl.BlockSpec((1, 3, D), lambda i: ((i * tm) // SEQ, 0, 0)), vec, vec, HBM, HBM, HBM] + [HBM] * ng,
        out_specs=[tok, tok, hid, hid, tok] + [HBM] * ng,
        out_shape=[SDS((T, D), F32), SDS((T, D), BF16), SDS((T, DFF_PAD), BF16), SDS((T, DFF_PAD), BF16), SDS((T, D), F32)]
        + ([] if plan is None else plan.out_shapes(BF16)),
        scratch_shapes=_resident_scratch() + ([] if plan is None else plan.scratch()),
        compiler_params=_cp("arbitrary"), name=name,
    )(x, mod3, g_pre, g_post, wg, wu, wd, *([] if gather is None else gather[0]))
    return outs[:5], outs[5:]


def ffn_bwd(dxo, x, y0, mod3, g_pre, g_post, gate, up, wg, wu, wd, gs, name, scatter=None, target=None):
    assert scatter is None or target is None
    T = x.shape[0]
    nb = T // SEQ
    tm = TM_BWD
    tiles_per_seq = SEQ // tm
    ns = 0 if scatter is None else len(scatter)
    ne = ns + (target is not None)

    def body(*refs):
        dxo_ref, x_ref, y0_ref, mod_ref, gpre_ref, gpost_ref, gate_ref, up_ref = refs[:8]
        dx_ref, dy0_ref, act_ref, dgate_ref, dup_ref, dmod_ref, dgpre_ref, dgpost_ref = refs[11 + ne:19 + ne]
        wg_ref, wu_ref, wd_ref, wsem = refs[19 + 2 * ne:23 + 2 * ne]
        i = pl.program_id(0)
        _load_resident(i == 0, refs[8:11], (wg_ref, wu_ref, wd_ref), wsem)
        if ns:
            comm = (refs[11:11 + ns], refs[19 + ns:19 + 2 * ns], *refs[23 + 2 * ns:])

            @pl.when(i == 0)
            def _():
                for cp in _scatter_copies(*comm):
                    cp.start()

        @pl.when(i == 0)
        def _():
            dgpre_ref[...] = jnp.zeros_like(dgpre_ref)
            dgpost_ref[...] = jnp.zeros_like(dgpost_ref)

        @pl.when(i % tiles_per_seq == 0)
        def _():
            dmod_ref[...] = jnp.zeros_like(dmod_ref)

        dxo = dxo_ref[...]
        if target is not None:
            loss_ref = refs[19 + ne]

            @pl.when(i == 0)
            def _():
                loss_ref[...] = jnp.zeros_like(loss_ref)

            err = dxo - refs[11][...]
            loss_ref[...] += jnp.sum(err * err) * (0.5 / D)
            dxo = err * (1.0 / D)
        dy0, dmg, dg = _post_bwd(dxo, y0_ref[...], gpost_ref[...], mod_ref[0, 2:3, :], gs)
        dmod_ref[0, 2:3, :] += dmg
        dgpost_ref[...] += dg
        db = dy0.astype(BF16)
        dy0_ref[...] = db
        dh = None
        for lo, hi in FF_TILES:
            dact = _dot_nt(db, wd_ref[lo:hi, :])
            g = gate_ref[:, lo:hi].astype(F32)
            u = up_ref[:, lo:hi].astype(F32)
            sig = jax.nn.sigmoid(g)
            sl = g * sig
            dgate = (dact * u * (sig * (1.0 + g * (1.0 - sig)))).astype(BF16)
            dup = (dact * sl).astype(BF16)
            act_ref[:, lo:hi] = (sl * u).astype(BF16)
            dgate_ref[:, lo:hi] = dgate
            dup_ref[:, lo:hi] = dup
            part = _dot_nt(dgate, wg_ref[:, lo:hi]) + _dot_nt(dup, wu_ref[:, lo:hi])
            dh = part if dh is None else dh + part
        dx, dsh, dsc, dg = _norm_mod_bwd(dh, x_ref[...], gpre_ref[...], mod_ref[0, 1:2, :])
        dx_ref[...] = dxo + dx
        dmod_ref[0, 0:1, :] += dsh
        dmod_ref[0, 1:2, :] += dsc
        dgpre_ref[...] += dg

        if ns:
            @pl.when(i == T // tm - 1)
            def _():
                for cp in _scatter_copies(*comm):
                    cp.wait()

    tok = pl.BlockSpec((tm, D), lambda i: (i, 0))
    vec = pl.BlockSpec((1, D), lambda i: (0, 0))
    hid = pl.BlockSpec((tm, DFF_PAD), lambda i: (i, 0))
    modspec = pl.BlockSpec((1, 3, D), lambda i: ((i * tm) // SEQ, 0, 0))
    outs = pl.pallas_call(
        body, grid=(T // tm,),
        in_specs=[tok, tok, tok, modspec, vec, vec, hid, hid, HBM, HBM, HBM] + [HBM] * ns + [tok] * (ne - ns),
        out_specs=[tok, tok, hid, hid, hid, modspec, vec, vec] + [HBM] * ns
        + [pl.BlockSpec((8, LANE), lambda i: (0, 0))] * (ne - ns),
        out_shape=[SDS((T, D), F32), SDS((T, D), BF16), SDS((T, DFF_PAD), BF16), SDS((T, DFF_PAD), BF16),
                   SDS((T, DFF_PAD), BF16), SDS((nb, 3, D), F32), SDS((1, D), F32), SDS((1, D), F32)]
        + [SDS((3,) + h.shape[1:], h.dtype) for h in (scatter or [])] + [SDS((8, LANE), F32)] * (ne - ns),
        scratch_shapes=_resident_scratch()
        + ([pltpu.SemaphoreType.DMA((ns, 3)), pltpu.SemaphoreType.DMA((ns, 3))] if ns else []),
        compiler_params=_cp("arbitrary"), name=name,
    )(dxo, x, y0, mod3, g_pre, g_post, gate, up, wg, wu, wd, *(scatter or []), *([] if target is None else [target]))
    return outs[:8], outs[8:]


def matmul_tn(a, b, tm, tn, tk, name, scatter=None, rows=None):
    T, M = a.shape
    N = b.shape[1]
    grid = (M // tm, N // tn, T // tk)
    ns = 0 if scatter is None else len(scatter)

    def body(*refs):
        a_ref, b_ref = refs[:2]
        o_ref = refs[2 + ns]
        ids = [pl.program_id(ax) for ax in range(3)]
        if ns:
            comm = (refs[2:2 + ns], refs[3 + ns:3 + 2 * ns], *refs[3 + 2 * ns:])

            @pl.when((ids[0] == 0) & (ids[1] == 0) & (ids[2] == 0))
            def _():
                for cp in _scatter_copies(*comm):
                    cp.start()

        @pl.when(ids[2] == 0)
        def _():
            o_ref[...] = jnp.zeros_like(o_ref)

        o_ref[...] += _dot_tn(a_ref[...], b_ref[...])

        if ns:
            @pl.when((ids[0] == grid[0] - 1) & (ids[1] == grid[1] - 1) & (ids[2] == grid[2] - 1))
            def _():
                for cp in _scatter_copies(*comm):
                    cp.wait()

    outs = pl.pallas_call(
        body, grid=grid,
        in_specs=[pl.BlockSpec((tk, tm), lambda i, j, k: (k, i)), pl.BlockSpec((tk, tn), lambda i, j, k: (k, j))] + [HBM] * ns,
        out_specs=[pl.BlockSpec((tm, tn), lambda i, j, k: (i, j))] + [HBM] * ns,
        out_shape=[SDS((rows or M, N), F32)] + [SDS((3,) + h.shape[1:], h.dtype) for h in (scatter or [])],
        scratch_shapes=[pltpu.SemaphoreType.DMA((ns, 3)), pltpu.SemaphoreType.DMA((ns, 3))] if ns else [],
        compiler_params=_cp("arbitrary", "arbitrary", "arbitrary"), name=name,
    )(a, b, *(scatter or []))
    return outs[0] if scatter is None else (outs[0], outs[1:])


def matmul_tn_cols(a, bs, tk, name):
    T, M = a.shape
    n = bs[0].shape[1]
    ng = len(bs)

    def body(*refs):
        a_ref, b_refs, o_ref = refs[0], refs[1:1 + ng], refs[1 + ng]

        @pl.when(pl.program_id(0) == 0)
        def _():
            o_ref[...] = jnp.zeros_like(o_ref)

        av = a_ref[...]
        for g, b_ref in enumerate(b_refs):
            o_ref[:, g * n:(g + 1) * n] += _dot_tn(av, b_ref[...])

    return pl.pallas_call(
        body, grid=(T // tk,),
        in_specs=[pl.BlockSpec((tk, M), lambda k: (k, 0))] + [pl.BlockSpec((tk, n), lambda k: (k, 0))] * ng,
        out_specs=pl.BlockSpec((M, ng * n), lambda k: (0, 0)), out_shape=SDS((M, ng * n), F32),
        compiler_params=_cp("arbitrary"), name=name,
    )(a, *bs)


def rope_tables(pos_col, name):
    T = pos_col.shape[0]
    tm = 1024

    def body(p_ref, c_ref, s1_ref, s2_ref):
        lane = lax.broadcasted_iota(jnp.int32, (1, LANE), 1)
        l64 = lane % HD
        inv_freq = jnp.exp((l64 % 8).astype(F32) * (-math.log(ROPE_THETA) / 8.0))
        ang = p_ref[...].astype(F32) * inv_freq
        cs = jnp.cos(ang)
        sn = jnp.sin(ang)
        c_ref[...] = jnp.where(l64 < 16, cs, 1.0)
        s1_ref[...] = jnp.where(l64 < 8, -sn, 0.0)
        s2_ref[...] = jnp.where((l64 >= 8) & (l64 < 16), sn, 0.0)

    tab = pl.BlockSpec((tm, LANE), lambda i: (i, 0))
    return pl.pallas_call(
        body, grid=(T // tm,), in_specs=[pl.BlockSpec((tm, 1), lambda i: (i, 0))], out_specs=[tab, tab, tab],
        out_shape=[SDS((T, LANE), F32)] * 3, compiler_params=_cp("arbitrary"), name=name,
    )(pos_col)


def mixer_proj(x, mod3, g_pre, w_main, w_f, rc, rs1, rs2, name):
    T = x.shape[0]

    def body(x_ref, mod_ref, g_ref, w_ref, wf_ref, c_ref, s1_ref, s2_ref, h_ref, pa_ref, pb_ref, f_ref):
        xf = x_ref[...]
        h = (xf * _rms(xf) * g_ref[...]) * (1.0 + mod_ref[0, 1:2, :]) + mod_ref[0, 0:1, :]
        hb = h.astype(BF16)
        h_ref[...] = hb
        f_ref[...] = _dot(hb, wf_ref[...])
        c, s1, s2 = c_ref[...], s1_ref[...], s2_ref[...]
        for grp in range(2):
            pr = _dot(hb, w_ref[:, grp * WG:(grp + 1) * WG])
            for k in range(WG // LANE):
                t = pr[:, k * LANE:(k + 1) * LANE]
                pa_ref[:, grp * WG + k * LANE:grp * WG + (k + 1) * LANE] = (
                    t * c + pltpu.roll(t, LANE - 8, 1) * s1 + pltpu.roll(t, 8, 1) * s2)
        pa_ref[:, 2 * WG:3 * WG] = _dot(hb, w_ref[:, 2 * WG:3 * WG])
        for grp in range(3):
            pb_ref[:, grp * WG:(grp + 1) * WG] = _dot(hb, w_ref[:, (3 + grp) * WG:(4 + grp) * WG]).astype(BF16)

    tok = pl.BlockSpec((TM, D), lambda i: (i, 0))
    vec = pl.BlockSpec((1, D), lambda i: (0, 0))
    tab = pl.BlockSpec((TM, LANE), lambda i: (i, 0))
    grp3 = pl.BlockSpec((TM, 3 * WG), lambda i: (i, 0))
    return pl.pallas_call(
        body, grid=(T // TM,),
        in_specs=[tok, pl.BlockSpec((1, 3, D), _mod_map), vec, pl.BlockSpec((D, IN_MAIN), lambda i: (0, 0)),
                  pl.BlockSpec((D, LANE), lambda i: (0, 0)), tab, tab, tab],
        out_specs=[tok, grp3, grp3, tab],
        out_shape=[SDS((T, D), BF16), SDS((T, 3 * WG), F32), SDS((T, 3 * WG), BF16), SDS((T, LANE), F32)],
        compiler_params=_cp("arbitrary"), name=name,
    )(x, mod3, g_pre, w_main, w_f, rc, rs1, rs2)


def _head_lanes():
    return lax.broadcasted_iota(jnp.int32, (1, LANE), 1) < HD


def _pair(m0, a, b):
    return jnp.where(m0, a, b)


def _band_rows(i, d, nbc):
    if nbc == 1:
        return i, i, 0
    r, mb = i // nbc, i % nbc
    return r + mb * (QB * d), r + jnp.maximum(mb - 1, 0) * (QB * d), jnp.where(mb > 0, QB, 0)


def _rows(start, size, d):
    return pl.ds(pl.multiple_of(start, QB), size) if d == 1 else pl.ds(start, size, stride=d)


def _band_valid(span, off):
    rq = lax.broadcasted_iota(jnp.int32, (QB, span), 0)
    rel = lax.broadcasted_iota(jnp.int32, (QB, span), 1) - off
    return (rel <= rq) & (rel >= rq - QB)


def band_fwd(pa, name):
    T = pa.shape[0]
    B = T // SEQ
    NP = WG // LANE

    def body(q_ref, k_ref, v_ref, out_ref, lse_ref, o_s, l_s):
        m0 = _head_lanes()
        for pidx, (d, nbc) in enumerate(PATTERNS):
            span = QB if nbc == 1 else 2 * QB

            def blk(it, carry, pidx=pidx, d=d, nbc=nbc, span=span):
                ld = []
                for u in range(BAND_UNROLL):
                    qs, ks, off = _band_rows(it * BAND_UNROLL + u, d, nbc)
                    q = q_ref[_rows(qs, QB, d), :] * ATTN_SCALE
                    ld.append((qs, q, k_ref[_rows(ks, span, d), :].astype(BF16), v_ref[_rows(ks, span, d), :].astype(BF16),
                               _band_valid(span, off)))
                ss = [[jnp.where(valid, _dot_nt(jnp.where(mh, q, 0.0).astype(BF16), k), NEG) for mh in (m0, jnp.logical_not(m0))]
                      for _, q, k, _, valid in ld]
                ps = []
                for pair in ss:
                    row = []
                    for s in pair:
                        m = jnp.max(s, axis=-1, keepdims=True)
                        p = jnp.exp(s - m)
                        row.append((p.astype(BF16), jnp.sum(p, axis=-1, keepdims=True), m))
                    ps.append(row)
                pv = [[_dot(p, ld[u][3]) for p, _, _ in ps[u]] for u in range(BAND_UNROLL)]
                for u in range(BAND_UNROLL):
                    rows = _rows(ld[u][0], QB, d)
                    (_, l0, mx0), (_, l1, mx1) = ps[u]
                    o_s[pidx, rows, :] = _pair(m0, pv[u][0] / l0, pv[u][1] / l1)
                    l_s[pidx, rows, :] = _pair(m0, mx0 + jnp.log(l0), mx1 + jnp.log(l1))
                return carry

            lax.fori_loop(0, SEQ // QB // BAND_UNROLL, blk, 0)
        for c in range(SEQ // ROWS):
            sl = slice(c * ROWS, (c + 1) * ROWS)
            a, b, e = l_s[0, sl, :], l_s[1, sl, :], l_s[2, sl, :]
            m = jnp.maximum(jnp.maximum(a, b), e)
            L = m + jnp.log(jnp.exp(a - m) + jnp.exp(b - m) + jnp.exp(e - m))
            out_ref[sl, :] = jnp.exp(a - L) * o_s[0, sl, :] + jnp.exp(b - L) * o_s[1, sl, :] + jnp.exp(e - L) * o_s[2, sl, :]
            lse_ref[sl, :] = L

    blk_of = lambda g: pl.BlockSpec((SEQ, LANE), lambda b, hp, g=g: (b, g * NP + hp))
    return pl.pallas_call(
        body, grid=(B, NP), in_specs=[blk_of(0), blk_of(1), blk_of(2)], out_specs=[blk_of(0), blk_of(0)],
        out_shape=[SDS((T, WG), F32), SDS((T, WG), F32)],
        scratch_shapes=[pltpu.VMEM((3, SEQ, LANE), F32), pltpu.VMEM((3, SEQ, LANE), F32)],
        compiler_params=_cp("arbitrary", "arbitrary"), name=name,
    )(pa, pa, pa)


def _pair_rowsum(m0, prod):
    s0 = jnp.sum(jnp.where(m0, prod, 0.0), axis=-1, keepdims=True)
    return _pair(m0, s0, jnp.sum(prod, axis=-1, keepdims=True) - s0)


def band_bwd(pa, do, out, lse, rc, rs1, rs2, name):
    T = pa.shape[0]
    B = T // SEQ
    NP = WG // LANE

    def body(q_ref, k_ref, v_ref, do_ref, out_ref, l_ref, c_ref, s1_ref, s2_ref, dqo_ref, dko_ref, dvo_ref, d_s, dq_ref, dk_ref,
             dv_ref):
        m0 = _head_lanes()
        dq_ref[...] = jnp.zeros_like(dq_ref)
        dk_ref[...] = jnp.zeros_like(dk_ref)
        dv_ref[...] = jnp.zeros_like(dv_ref)
        for c in range(SEQ // ROWS):
            sl = slice(c * ROWS, (c + 1) * ROWS)
            d_s[sl, :] = _pair_rowsum(m0, do_ref[sl, :] * out_ref[sl, :])
        for d, nbc in PATTERNS:
            span = QB if nbc == 1 else 2 * QB

            def blk(it, carry, d=d, nbc=nbc, span=span):
                masks = (m0, jnp.logical_not(m0))
                ld = []
                for u in range(BAND_UNROLL_BWD):
                    qs, ks, off = _band_rows(it * BAND_UNROLL_BWD + u, d, nbc)
                    qrow, krow = _rows(qs, QB, d), _rows(ks, span, d)
                    ld.append(dict(qrow=qrow, krow=krow, q=q_ref[qrow, :] * ATTN_SCALE, k=k_ref[krow, :].astype(BF16),
                                   v=v_ref[krow, :].astype(BF16), do=do_ref[qrow, :], l=l_ref[qrow, :], dv=d_s[qrow, :],
                                   valid=_band_valid(span, off)))
                for t in ld:
                    t["qm"] = [jnp.where(mh, t["q"], 0.0).astype(BF16) for mh in masks]
                    t["dom"] = [jnp.where(mh, t["do"], 0.0).astype(BF16) for mh in masks]
                sd = [[(jnp.where(t["valid"], _dot_nt(t["qm"][h], t["k"]), NEG), _dot_nt(t["dom"][h], t["v"])) for h in range(2)]
                      for t in ld]
                pd = []
                for t, pair in zip(ld, sd):
                    row = []
                    for h, (s, dp) in enumerate(pair):
                        col = slice(h * HD, h * HD + 1)
                        p = jnp.exp(s - t["l"][:, col])
                        row.append((p.astype(BF16), (p * (dp - t["dv"][:, col])).astype(BF16)))
                    pd.append(row)
                gr = [(_dot(row[0][1], t["k"]), _dot(row[1][1], t["k"]),
                       _dot_tn(jnp.concatenate([row[0][1], row[1][1]], axis=0), jnp.concatenate(t["qm"], axis=0)),
                       _dot_tn(jnp.concatenate([row[0][0], row[1][0]], axis=0), jnp.concatenate(t["dom"], axis=0)))
                      for t, row in zip(ld, pd)]
                for t, (dq0, dq1, dk, dv) in zip(ld, gr):
                    dq_ref[t["qrow"], :] += _pair(m0, dq0, dq1) * ATTN_SCALE
                    dk_ref[t["krow"], :] += dk
                    dv_ref[t["krow"], :] += dv
                return carry

            lax.fori_loop(0, SEQ // QB // BAND_UNROLL_BWD, blk, 0)
        for c in range(SEQ // ROWS):
            sl = slice(c * ROWS, (c + 1) * ROWS)
            cc, s1, s2 = c_ref[sl, :], s1_ref[sl, :], s2_ref[sl, :]
            for acc, o_ref in ((dq_ref, dqo_ref), (dk_ref, dko_ref)):
                d = acc[sl, :]
                o_ref[sl, :] = (d * cc + pltpu.roll(d * s1, 8, 1) + pltpu.roll(d * s2, LANE - 8, 1)).astype(BF16)
            dvo_ref[sl, :] = dv_ref[sl, :].astype(BF16)

    blk_of = lambda g: pl.BlockSpec((SEQ, LANE), lambda b, hp, g=g: (b, g * NP + hp))
    tab = pl.BlockSpec((SEQ, LANE), lambda b, hp: (b, 0))
    return pl.pallas_call(
        body, grid=(B, NP), in_specs=[blk_of(0), blk_of(1), blk_of(2), blk_of(0), blk_of(0), blk_of(0), tab, tab, tab],
        out_specs=[blk_of(0)] * 3, out_shape=[SDS((T, WG), BF16)] * 3,
        scratch_shapes=[pltpu.VMEM((SEQ, LANE), F32)] * 4,
        compiler_params=_cp("arbitrary", "arbitrary"), name=name,
    )(pa, pa, pa, do, out, lse, rc, rs1, rs2)


def _tile_causal(nq, nk, q0, k0):
    r = lax.broadcasted_iota(jnp.int32, (nq, nk), 0)
    c = lax.broadcasted_iota(jnp.int32, (nq, nk), 1)
    return r + (q0 - k0) >= c


def _row_to_col(row):
    n = row.shape[1]
    return jnp.transpose(jnp.broadcast_to(row, (LANE, n)))[:, 0:1]


def _col_to_row(col):
    n = col.shape[0]
    return jnp.transpose(jnp.broadcast_to(col, (n, LANE)))[0:1, :]


def fox_fwd(pb, fblk, frow, name, gather=None):
    FQ = FOX_QB
    T = pb.shape[0]
    B = T // SEQ
    NG = WG // (LANE * FOX_PAIRS)
    NHS = 2 * FOX_PAIRS
    W = LANE * FOX_PAIRS
    n = SEQ // FQ
    ng = 0 if gather is None else len(gather[0])
    plan = None if gather is None else ShardGather([w.shape for w in gather[0]], gather[1])

    def body(*refs):
        q_ref, k_ref, v_ref, fc_ref, fr_ref = refs[:5]
        o_ref, lse_ref = refs[5 + ng:7 + ng]
        if plan is not None:
            comm = (refs[5:5 + ng], refs[7 + ng:7 + 2 * ng], refs[7 + 2 * ng:])
            ids = [pl.program_id(ax) for ax in range(3)]
            pl.when((ids[0] == 0) & (ids[1] == 0) & (ids[2] == 0))(lambda: plan.start(*comm))
        i = pl.program_id(2)
        m0 = _head_lanes()
        masks = (m0, jnp.logical_not(m0))
        heads = [(hh, slice((hh // 2) * LANE, (hh // 2 + 1) * LANE), masks[hh % 2]) for hh in range(NHS)]
        qh, fq = [], []
        for hh, lanes, mh in heads:
            q = q_ref[:, lanes] * ATTN_SCALE
            qh.append(jnp.where(mh, q, jnp.zeros_like(q)))
            fq.append(_row_to_col(fc_ref[0, hh, 0]))

        def step(t, carry, masked):
            rows = pl.ds(pl.multiple_of(t * FT, FT), FT)
            ss = [_dot_nt(qh[hh], k_ref[rows, lanes]) + fq[hh] - fr_ref[0, hh, t] for hh, lanes, _ in heads]
            if masked:
                ok = _tile_causal(FQ, FT, i * FQ, t * FT)
                ss = [jnp.where(ok, s, NEG) for s in ss]
            st = []
            for hh, _, _ in heads:
                m2 = jnp.maximum(carry[hh][0], jnp.max(ss[hh], axis=-1, keepdims=True))
                st.append((m2, jnp.exp(carry[hh][0] - m2), jnp.exp(ss[hh] - m2).astype(BF16)))
            pv = []
            for hh, lanes, mh in heads:
                vt = v_ref[rows, lanes]
                pv.append(_dot(st[hh][2], jnp.where(mh, vt, jnp.ones_like(vt))))
            return tuple((st[hh][0], st[hh][1] * carry[hh][1] + pv[hh]) for hh in range(NHS))

        one = (jnp.full((FQ, 1), NEG, F32), jnp.zeros((FQ, LANE), F32))
        last = (i * FQ) // FT
        carry = lax.fori_loop(0, last, lambda t, cr: step(t, cr, False), (one,) * NHS)
        carry = step(last, carry, True)
        for pr in range(FOX_PAIRS):
            (ma, acca), (mb, accb) = carry[2 * pr], carry[2 * pr + 1]
            la, lb = acca[:, HD:HD + 1], accb[:, 0:1]
            o_ref[:, pr * LANE:(pr + 1) * LANE] = _pair(m0, acca / la, accb / lb)
            lse_ref[0, 2 * pr, 0] = _col_to_row(ma + jnp.log(la))
            lse_ref[0, 2 * pr + 1, 0] = _col_to_row(mb + jnp.log(lb))
        if plan is not None:
            pl.when((ids[0] == B - 1) & (ids[1] == NG - 1) & (ids[2] == n - 1))(lambda: plan.finish(*comm))

    qblk = pl.BlockSpec((FQ, W), lambda b, g, i: (b * n + i, g))
    full = lambda grp: pl.BlockSpec((SEQ, W), lambda b, g, i, grp=grp: (b, grp * NG + g))
    rowb = pl.BlockSpec((1, NHS, 1, 1, FQ), lambda b, g, i: (b, g, i, 0, 0))
    outs = pl.pallas_call(
        body, grid=(B, NG, n),
        in_specs=[qblk, full(1), full(2), rowb, pl.BlockSpec((1, NHS, SEQ // FT, 1, FT), lambda b, g, i: (b, g, 0, 0, 0))]
        + [HBM] * ng,
        out_specs=[qblk, rowb] + [HBM] * ng,
        out_shape=[SDS((T, WG), F32), SDS((B, NH, n, 1, FQ), F32)] + ([] if plan is None else plan.out_shapes(BF16)),
        scratch_shapes=[] if plan is None else plan.scratch(),
        compiler_params=_cp("arbitrary", "arbitrary", "arbitrary"), name=name,
    )(pb, pb, pb, fblk, frow, *([] if gather is None else gather[0]))
    return outs[:2], outs[2:]


def fox_bwd(pb, do, lrow, drow, fblk, frow, name):
    T = pb.shape[0]
    B = T // SEQ
    PAIRS = FOX_PAIRS_BWD
    NG = WG // (LANE * PAIRS)
    NHS = 2 * PAIRS
    W = LANE * PAIRS
    n = SEQ // FB

    def body(q_ref, k_ref, v_ref, do_ref, l_ref, d_ref, fc_ref, fr_ref, dqo_ref, dk_ref, dv_ref, dfq_ref, dfk_ref, dq_ref):
        j = pl.program_id(2)
        m0 = _head_lanes()
        masks = (m0, jnp.logical_not(m0))
        heads = [(hh, slice((hh // 2) * LANE, (hh // 2 + 1) * LANE), masks[hh % 2]) for hh in range(NHS)]

        @pl.when(j == 0)
        def _():
            dq_ref[...] = jnp.zeros_like(dq_ref)
            dfq_ref[...] = jnp.zeros_like(dfq_ref)

        kj = [k_ref[:, lanes] for _, lanes, _ in heads]
        vj = [v_ref[:, lanes] for _, lanes, _ in heads]
        fk = [_row_to_col(fc_ref[0, hh, 0]) for hh in range(NHS)]

        def step(t, carry, masked):
            rows = pl.ds(pl.multiple_of(t * FT, FT), FT)
            qm, dom = [], []
            for _, lanes, mh in heads:
                qt = q_ref[rows, lanes] * ATTN_SCALE
                qm.append(jnp.where(mh, qt, jnp.zeros_like(qt)))
                dom.append(jnp.where(mh, do_ref[rows, lanes], 0.0).astype(BF16))
            ss = [_dot_nt(kj[hh], qm[hh]) + fr_ref[0, hh, t] - fk[hh] for hh in range(NHS)]
            dps = [_dot_nt(vj[hh], dom[hh]) for hh in range(NHS)]
            if masked:
                key = lax.broadcasted_iota(jnp.int32, (FB, FT), 0)
                qry = lax.broadcasted_iota(jnp.int32, (FB, FT), 1)
                ok = qry + (t * FT - j * FB) >= key
                ss = [jnp.where(ok, s, NEG) for s in ss]
            pds = []
            for hh in range(NHS):
                p = jnp.exp(ss[hh] - l_ref[0, hh, t])
                ds = p * (dps[hh] - d_ref[0, hh, t])
                dfq_ref[0, hh, t] += jnp.sum(ds, axis=0, keepdims=True)
                pds.append((p.astype(BF16), ds.astype(BF16), jnp.sum(ds, axis=-1, keepdims=True)))
            dks = [_dot(pds[hh][1], qm[hh]) for hh in range(NHS)]
            dvs = [_dot(pds[hh][0], dom[hh]) for hh in range(NHS)]
            dqs = [_dot_tn(pds[hh][1], kj[hh]) for hh in range(NHS)]
            for pr in range(PAIRS):
                dq_ref[rows, pr * LANE:(pr + 1) * LANE] += _pair(m0, dqs[2 * pr], dqs[2 * pr + 1]) * ATTN_SCALE
            return tuple((carry[hh][0] + dks[hh], carry[hh][1] + dvs[hh], carry[hh][2] - pds[hh][2]) for hh in range(NHS))

        one = (jnp.zeros((FB, LANE), F32), jnp.zeros((FB, LANE), F32), jnp.zeros((FB, 1), F32))
        first = (j * FB) // FT
        carry = step(first, (one,) * NHS, True)
        carry = lax.fori_loop(first + 1, SEQ // FT, lambda t, cr: step(t, cr, False), carry)
        for pr in range(PAIRS):
            (dka, dva, dfka), (dkb, dvb, dfkb) = carry[2 * pr], carry[2 * pr + 1]
            dk_ref[:, pr * LANE:(pr + 1) * LANE] = _pair(m0, dka, dkb).astype(BF16)
            dv_ref[:, pr * LANE:(pr + 1) * LANE] = _pair(m0, dva, dvb).astype(BF16)
            dfk_ref[0, 2 * pr, 0] = _col_to_row(dfka)
            dfk_ref[0, 2 * pr + 1, 0] = _col_to_row(dfkb)

        @pl.when(j == n - 1)
        def _():
            dqo_ref[...] = dq_ref[...].astype(BF16)

    kblk = lambda grp: pl.BlockSpec((FB, W), lambda b, g, j, grp=grp: (b * n + j, grp * NG + g))
    full = pl.BlockSpec((SEQ, W), lambda b, g, j: (b, g))
    rowf = pl.BlockSpec((1, NHS, SEQ // FT, 1, FT), lambda b, g, j: (b, g, 0, 0, 0))
    rowb = pl.BlockSpec((1, NHS, 1, 1, FB), lambda b, g, j: (b, g, j, 0, 0))
    return pl.pallas_call(
        body, grid=(B, NG, n), in_specs=[full, kblk(1), kblk(2), full, rowf, rowf, rowb, rowf],
        out_specs=[full, kblk(0), kblk(0), rowf, rowb],
        out_shape=[SDS((T, WG), BF16), SDS((T, WG), BF16), SDS((T, WG), BF16), SDS((B, NH, SEQ // FT, 1, FT), F32),
                   SDS((B, NH, n, 1, FB), F32)],
        scratch_shapes=[pltpu.VMEM((SEQ, W), F32)],
        compiler_params=_cp("arbitrary", "arbitrary", "arbitrary"), name=name,
    )(pb, pb, pb, do, lrow, drow, fblk, frow)


def _tri(lower):
    r = lax.broadcasted_iota(jnp.int32, (LANE, LANE), 0)
    c = lax.broadcasted_iota(jnp.int32, (LANE, LANE), 1)
    return ((r >= c) if lower else (r <= c)).astype(F32)


def _tri_dot(t, xblk):
    return jnp.dot(t, xblk, precision=lax.Precision.HIGHEST, preferred_element_type=F32)


def forget_cumsum(flog, bias, name):
    B, S, _ = flog.shape

    def body(f_ref, b_ref, o_ref):
        t = _tri(True)
        carry = jnp.zeros((1, LANE), F32)
        for blk in range(S // LANE):
            z = f_ref[0, blk * LANE:(blk + 1) * LANE, :] + b_ref[...]
            lf = jnp.minimum(z, 0.0) - jnp.log(1.0 + jnp.exp(-jnp.abs(z)))
            cs = _tri_dot(t, lf) + carry
            o_ref[0, blk * LANE:(blk + 1) * LANE, :] = cs
            carry = cs[LANE - 1:LANE, :]

    spec = pl.BlockSpec((1, S, LANE), lambda b: (b, 0, 0))
    return pl.pallas_call(
        body, grid=(B,), in_specs=[spec, pl.BlockSpec((1, LANE), lambda b: (0, 0))], out_specs=spec,
        out_shape=SDS((B, S, LANE), F32), compiler_params=_cp("arbitrary"), name=name,
    )(flog, bias)


def forget_cumsum_bwd(dF, flog, bias, name):
    B, S, _ = flog.shape

    def body(d_ref, f_ref, b_ref, o_ref, db_ref):
        @pl.when(pl.program_id(0) == 0)
        def _():
            db_ref[...] = jnp.zeros_like(db_ref)

        t = _tri(False)
        carry = jnp.zeros((1, LANE), F32)
        tot = jnp.zeros((1, LANE), F32)
        for blk in reversed(range(S // LANE)):
            sl = slice(blk * LANE, (blk + 1) * LANE)
            rc = _tri_dot(t, d_ref[0, sl, :]) + carry
            carry = rc[0:1, :]
            z = f_ref[0, sl, :] + b_ref[...]
            dz = rc * jax.nn.sigmoid(-z)
            o_ref[0, sl, :] = dz
            tot = tot + jnp.sum(dz, axis=0, keepdims=True)
        db_ref[...] += tot

    spec = pl.BlockSpec((1, S, LANE), lambda b: (b, 0, 0))
    vec = pl.BlockSpec((1, LANE), lambda b: (0, 0))
    return pl.pallas_call(
        body, grid=(B,), in_specs=[spec, spec, vec], out_specs=[spec, vec],
        out_shape=[SDS((B, S, LANE), F32), SDS((1, LANE), F32)], compiler_params=_cp("arbitrary"), name=name,
    )(dF, flog, bias)


def mixer_out_fwd(oa, ob, goa, gob, w_out, g_post, x, mod3, name):
    T = x.shape[0]

    def body(oa_ref, ob_ref, goa_ref, gob_ref, w_ref, gp_ref, x_ref, mod_ref, xo_ref, mg_ref, y0_ref):
        a = oa_ref[...]
        b = ob_ref[...]
        mg = jnp.concatenate([a * _rms(a) * goa_ref[...], b * _rms(b) * gob_ref[...]], axis=-1).astype(BF16)
        mg_ref[...] = mg
        y0 = _dot(mg, w_ref[...])
        y0_ref[...] = y0
        xo_ref[...] = x_ref[...] + mod_ref[0, 2:3, :] * (y0 * _rms(y0) * gp_ref[...])

    tok = pl.BlockSpec((TM, D), lambda i: (i, 0))
    half = pl.BlockSpec((TM, WG), lambda i: (i, 0))
    hv = pl.BlockSpec((1, WG), lambda i: (0, 0))
    return pl.pallas_call(
        body, grid=(T // TM,),
        in_specs=[half, half, hv, hv, pl.BlockSpec((D, D), lambda i: (0, 0)), pl.BlockSpec((1, D), lambda i: (0, 0)), tok,
                  pl.BlockSpec((1, 3, D), _mod_map)],
        out_specs=[tok, tok, tok], out_shape=[SDS((T, D), F32), SDS((T, D), BF16), SDS((T, D), F32)],
        compiler_params=_cp("arbitrary"), name=name,
    )(oa, ob, goa, gob, w_out, g_post, x, mod3)


def mixer_out_bwd(dxo, y0, mod3, g_post, w_out, oa, ob, goa, gob, name):
    T = dxo.shape[0]
    nb = T // SEQ
    tiles_per_seq = SEQ // TM

    def body(dxo_ref, y0_ref, mod_ref, gp_ref, w_ref, oa_ref, ob_ref, goa_ref, gob_ref,
             dy0_ref, doa_ref, dob_ref, dmg_ref, dgp_ref, dgoa_ref, dgob_ref, dvb_ref):
        i = pl.program_id(0)

        @pl.when(i == 0)
        def _():
            dgp_ref[...] = jnp.zeros_like(dgp_ref)
            dgoa_ref[...] = jnp.zeros_like(dgoa_ref)
            dgob_ref[...] = jnp.zeros_like(dgob_ref)

        @pl.when(i % tiles_per_seq == 0)
        def _():
            dmg_ref[...] = jnp.zeros_like(dmg_ref)

        dy0, dmg, dg = _post_bwd(dxo_ref[...], y0_ref[...], gp_ref[...], mod_ref[0, 2:3, :], 1.0)
        dmg_ref[0] += dmg
        dgp_ref[...] += dg
        db = dy0.astype(BF16)
        dy0_ref[...] = db
        dm = _dot_nt(db, w_ref[...])
        for o_ref, g_ref, do_ref, dg_ref, sl in ((oa_ref, goa_ref, doa_ref, dgoa_ref, slice(0, WG)),
                                                  (ob_ref, gob_ref, dob_ref, dgob_ref, slice(WG, 2 * WG))):
            o = o_ref[...]
            r = _rms(o)
            oh = o * r
            d = dm[:, sl]
            dg_ref[...] += jnp.sum(d * oh, axis=0, keepdims=True)
            dh = d * g_ref[...]
            do = r * (dh - oh * jnp.mean(dh * oh, axis=-1, keepdims=True))
            do_ref[...] = do
        ind = (lax.broadcasted_iota(jnp.int32, (WG, LANE), 0) // HD == lax.broadcasted_iota(jnp.int32, (WG, LANE), 1)).astype(BF16)
        prod = do * o
        hi = prod.astype(BF16)
        dvb_ref[...] = _dot(hi, ind) + _dot((prod - hi.astype(F32)).astype(BF16), ind)

    tok = pl.BlockSpec((TM, D), lambda i: (i, 0))
    half = pl.BlockSpec((TM, WG), lambda i: (i, 0))
    hv = pl.BlockSpec((1, WG), lambda i: (0, 0))
    vec = pl.BlockSpec((1, D), lambda i: (0, 0))
    return pl.pallas_call(
        body, grid=(T // TM,),
        in_specs=[tok, tok, pl.BlockSpec((1, 3, D), _mod_map), vec, pl.BlockSpec((D, D), lambda i: (0, 0)), half, half, hv, hv],
        out_specs=[tok, half, half, pl.BlockSpec((1, 1, D), _mod_map), vec, hv, hv, pl.BlockSpec((TM, LANE), lambda i: (i, 0))],
        out_shape=[SDS((T, D), BF16), SDS((T, WG), F32), SDS((T, WG), F32), SDS((nb, 1, D), F32), SDS((1, D), F32),
                   SDS((1, WG), F32), SDS((1, WG), F32), SDS((T, LANE), F32)],
        compiler_params=_cp("arbitrary"), name=name,
    )(dxo, y0, mod3, g_post, w_out, oa, ob, goa, gob)


def mixer_proj_bwd(dps, dflog, dxo, x, mod3, g_pre, w_main, w_f, name):
    T = x.shape[0]
    nb = T // SEQ
    tiles_per_seq = SEQ // TM
    ngrp = len(dps)

    def body(*refs):
        dp_refs = refs[:ngrp]
        df_ref, dxo_ref, x_ref, mod_ref, g_ref, w_ref, wf_ref, dx_ref, dmod_ref, dg_ref = refs[ngrp:]
        i = pl.program_id(0)

        @pl.when(i == 0)
        def _():
            dg_ref[...] = jnp.zeros_like(dg_ref)

        @pl.when(i % tiles_per_seq == 0)
        def _():
            dmod_ref[...] = jnp.zeros_like(dmod_ref)

        dh = _dot_nt(df_ref[...].astype(BF16), wf_ref[...])
        for g, dp_ref in enumerate(dp_refs):
            dh = dh + _dot_nt(dp_ref[...], w_ref[:, g * WG:(g + 1) * WG])
        dx, dsh, dsc, dg = _norm_mod_bwd(dh, x_ref[...], g_ref[...], mod_ref[0, 1:2, :])
        dx_ref[...] = dxo_ref[...] + dx
        dmod_ref[0, 0:1, :] += dsh
        dmod_ref[0, 1:2, :] += dsc
        dg_ref[...] += dg

    tok = pl.BlockSpec((TM, D), lambda i: (i, 0))
    vec = pl.BlockSpec((1, D), lambda i: (0, 0))
    return pl.pallas_call(
        body, grid=(T // TM,),
        in_specs=[pl.BlockSpec((TM, WG), lambda i: (i, 0))] * ngrp
        + [pl.BlockSpec((TM, LANE), lambda i: (i, 0)), tok, tok, pl.BlockSpec((1, 3, D), _mod_map), vec,
           pl.BlockSpec((D, IN_MAIN), lambda i: (0, 0)), pl.BlockSpec((D, LANE), lambda i: (0, 0))],
        out_specs=[tok, pl.BlockSpec((1, 2, D), _mod_map), vec],
        out_shape=[SDS((T, D), F32), SDS((nb, 2, D), F32), SDS((1, D), F32)],
        compiler_params=_cp("arbitrary"), name=name,
    )(*dps, dflog, dxo, x, mod3, g_pre, w_main, w_f)


def ada_fwd(c_all, w, b, name):
    n = w.shape[1]
    tn = n // 2

    def body(c_ref, w_ref, b_ref, o_ref):
        cv = c_ref[...]
        o_ref[...] = _dot((cv * jax.nn.sigmoid(cv)).astype(BF16), w_ref[...].astype(BF16)) + b_ref[...]

    R = c_all.shape[0]
    return pl.pallas_call(
        body, grid=(2,),
        in_specs=[pl.BlockSpec((R, D), lambda j: (0, 0)), pl.BlockSpec((D, tn), lambda j: (0, j)), pl.BlockSpec((1, tn), lambda j: (0, j))],
        out_specs=pl.BlockSpec((R, tn), lambda j: (0, j)), out_shape=SDS((R, n), F32),
        compiler_params=_cp("arbitrary"), name=name,
    )(c_all, w, b)


def ada_bwd(c_all, dmod, name):
    R, n = dmod.shape
    tn = n // 2

    def body(c_ref, d_ref, o_ref):
        cv = c_ref[...]
        o_ref[...] = _dot_tn((cv * jax.nn.sigmoid(cv)).astype(BF16), d_ref[...].astype(BF16))

    return pl.pallas_call(
        body, grid=(2,), in_specs=[pl.BlockSpec((R, D), lambda j: (0, 0)), pl.BlockSpec((R, tn), lambda j: (0, j))],
        out_specs=pl.BlockSpec((D, tn), lambda j: (0, j)), out_shape=SDS((D, n), F32),
        compiler_params=_cp("arbitrary"), name=name,
    )(c_all, dmod)


def _adam_math(w, g, m, v):
    m2 = ADAM_B1 * m + (1.0 - ADAM_B1) * g
    v2 = ADAM_B2 * v + (1.0 - ADAM_B2) * (g * g)
    m_hat = m2 / (1.0 - ADAM_B1 ** ADAM_STEP)
    v_hat = v2 / (1.0 - ADAM_B2 ** ADAM_STEP)
    delta = -ADAM_LR * (m_hat / (jnp.sqrt(v_hat) + ADAM_EPS) + ADAM_WD * w)
    return delta, m2, v2


def adam_update(w, g, m, v, tr, name):
    _, R, C = w.shape

    def body(w_ref, g_ref, m_ref, v_ref, d_ref, mo_ref, vo_ref):
        d_ref[0], mo_ref[0], vo_ref[0] = _adam_math(w_ref[0], g_ref[...], m_ref[0], v_ref[0])

    spec = pl.BlockSpec((1, tr, C), lambda i: (0, i, 0))
    gspec = pl.BlockSpec((tr, C), lambda i: (i, 0))
    return pl.pallas_call(
        body, grid=(R // tr,), in_specs=[spec, gspec, spec, spec], out_specs=[spec] * 3, out_shape=[SDS((1, R, C), F32)] * 3,
        compiler_params=_cp("arbitrary"), name=name,
    )(w, g, m, v)


def adam_update_halves(w, mine, other, m, v, cidx, tr, name):
    _, R, C = w.shape
    nh = R // 2 // tr

    def body(c_ref, w_ref, a_ref, b_ref, m_ref, v_ref, g_ref, d_ref, mo_ref, vo_ref):
        first_half = pl.program_id(0) < nh
        g = jnp.where(first_half == (c_ref[0] == 0), a_ref[...], b_ref[...])
        g_ref[0] = g
        d_ref[0], mo_ref[0], vo_ref[0] = _adam_math(w_ref[0], g, m_ref[0], v_ref[0])

    spec = pl.BlockSpec((1, tr, C), lambda i, c_ref: (0, i, 0))
    hspec = pl.BlockSpec((tr, C), lambda i, c_ref: (i % nh, 0))
    return pl.pallas_call(
        body,
        grid_spec=pltpu.PrefetchScalarGridSpec(num_scalar_prefetch=1, grid=(R // tr,), in_specs=[spec, hspec, hspec, spec, spec],
                                               out_specs=[spec] * 4),
        out_shape=[SDS((1, R, C), F32)] * 4, compiler_params=_cp("arbitrary"), name=name,
    )(cidx, w, mine, other, m, v)


def vec_adam(parts, w, m, v, name):
    P, C = parts.shape

    def body(p_ref, w_ref, m_ref, v_ref, g_ref, d_ref, mo_ref, vo_ref):
        g = jnp.sum(p_ref[...], axis=0, keepdims=True)
        g_ref[...] = g
        d_ref[...], mo_ref[...], vo_ref[...] = _adam_math(w_ref[...], g, m_ref[...], v_ref[...])

    return pl.pallas_call(body, out_shape=[SDS((1, C), F32)] * 4, compiler_params=_cp(), name=name)(parts, w, m, v)


def small_adam(parts, layout, ws, ms, vs, name):
    P, C = parts.shape
    k = len(layout)

    def body(*refs):
        p_ref = refs[0]
        w_refs, m_refs, v_refs = refs[1:1 + k], refs[1 + k:1 + 2 * k], refs[1 + 2 * k:1 + 3 * k]
        outs = refs[1 + 3 * k:]
        g_all = jnp.sum(p_ref[...], axis=0, keepdims=True)
        outs[4 * k][...] = g_all
        for n, (off, width) in enumerate(layout):
            g = g_all[:, off:off + width]
            outs[4 * n][...] = g
            outs[4 * n + 1][...], outs[4 * n + 2][...], outs[4 * n + 3][...] = _adam_math(
                w_refs[n][...], g, m_refs[n][...], v_refs[n][...])

    shapes = [SDS((1, width), F32) for _, width in layout for _ in range(4)] + [SDS((1, C), F32)]
    res = pl.pallas_call(body, out_shape=shapes, compiler_params=_cp(), name=name)(parts, *ws, *ms, *vs)
    return [tuple(res[4 * n:4 * n + 4]) for n in range(k)], res[4 * k]


HBM = pl.BlockSpec(memory_space=pltpu.HBM)
VMEM = pl.BlockSpec(memory_space=pltpu.VMEM)


def _place():
    x, y, c = lax.axis_index("x"), lax.axis_index("y"), lax.axis_index("c")
    return x, y, c, [(1 - x, y), (x, 1 - y), (1 - x, 1 - y)]


def all_gather8(xs, name):
    R, C = xs.shape

    def body(x_ref, out_ref, send_sems, recv_sems, local_sem):
        x, y, c, chips = _place()
        me, sibling = (x, y, c), (x, y, 1 - c)

        def slot(px, py, pc):
            return out_ref.at[4 * px + 2 * py + pc]

        def copy(k, block, to, src=None):
            return pltpu.make_async_remote_copy(
                src_ref=slot(*block) if src is None else src, dst_ref=slot(*block),
                send_sem=send_sems.at[k], recv_sem=recv_sems.at[k], device_id=to, device_id_type=MESH)

        mine = pltpu.make_async_copy(x_ref, slot(*me), local_sem)
        mine.start()
        first = [copy(0, me, sibling, src=x_ref)]
        first += [copy(1 + j, me, (*chip, c), src=x_ref) for j, chip in enumerate(chips)]
        for cp in first:
            cp.start()
        passed = [copy(4 + j, (*chip, c), sibling) for j, chip in enumerate(chips)]
        for j, chip in enumerate(chips):
            copy(1 + j, (*chip, c), me).wait_recv()
            passed[j].start()
        copy(0, sibling, me).wait_recv()
        for j, chip in enumerate(chips):
            copy(4 + j, (*chip, 1 - c), me).wait_recv()
        for cp in first + passed:
            cp.wait_send()
        mine.wait()

    return pl.pallas_call(
        body, out_shape=SDS((N_DEV, R, C), xs.dtype), in_specs=[VMEM], out_specs=VMEM,
        scratch_shapes=[pltpu.SemaphoreType.DMA((7,)), pltpu.SemaphoreType.DMA((7,)), pltpu.SemaphoreType.DMA],
        compiler_params=pltpu.CompilerParams(vmem_limit_bytes=VMEM_LIMIT), name=name,
    )(xs)


class ShardGather:
    def __init__(self, shapes, splits):
        self.shapes, self.splits, self.n = shapes, splits, len(shapes)

    def scratch(self):
        n = self.n
        return [pltpu.SemaphoreType.DMA((n, 6)), pltpu.SemaphoreType.DMA((n, 6)), pltpu.SemaphoreType.DMA((n,))]

    def out_shapes(self, dtype):
        return [SDS((N_SHARD,) + tuple(s), dtype) for s in self.shapes]

    def _half(self, ref, k, cc):
        lo, hi = (0, self.splits[k]) if cc == 0 else (self.splits[k], self.shapes[k][0])
        return ref.at[pl.ds(lo, hi - lo)]

    def _phase(self, w_refs, o_refs, sems, finish):
        send_sems, recv_sems, local_sems = sems
        x, y, c, chips = _place()
        sibling = (x, y, 1 - c)
        me_s = 2 * x + y

        def rcopy(src, dst, k, s, to):
            return pltpu.make_async_remote_copy(src_ref=src, dst_ref=dst, send_sem=send_sems.at[k, s],
                                                recv_sem=recv_sems.at[k, s], device_id=to, device_id_type=MESH)

        for cc in (0, 1):
            @pl.when(c == cc)
            def _():
                local = [pltpu.make_async_copy(w_refs[k], o_refs[k].at[me_s], local_sems.at[k]) for k in range(self.n)]
                first = [rcopy(self._half(w_refs[k], k, cc), self._half(o_refs[k].at[me_s], k, cc), k, j, (*chip, c))
                         for k in range(self.n) for j, chip in enumerate(chips)]
                if not finish:
                    for cp in local + first:
                        cp.start()
                    return
                passed = []
                for k in range(self.n):
                    for j, chip in enumerate(chips):
                        land = self._half(o_refs[k].at[2 * chip[0] + chip[1]], k, cc)
                        rcopy(land, land, k, j, (*chip, c)).wait_recv()
                        f = rcopy(land, land, k, 3 + j, sibling)
                        f.start()
                        passed.append(f)
                for k in range(self.n):
                    for j, chip in enumerate(chips):
                        other = self._half(o_refs[k].at[2 * chip[0] + chip[1]], k, 1 - cc)
                        rcopy(other, other, k, 3 + j, sibling).wait_recv()
                for s in first + passed:
                    s.wait_send()
                for cp in local:
                    cp.wait()

    def start(self, w_refs, o_refs, sems):
        self._phase(w_refs, o_refs, sems, False)

    def finish(self, w_refs, o_refs, sems):
        self._phase(w_refs, o_refs, sems, True)


def all_gather_shards(ws, splits, name):
    n = len(ws)
    plan = ShardGather([w.shape for w in ws], splits)

    def body(*refs):
        plan.start(refs[:n], refs[n:2 * n], refs[2 * n:])
        plan.finish(refs[:n], refs[n:2 * n], refs[2 * n:])

    return pl.pallas_call(
        body, out_shape=plan.out_shapes(ws[0].dtype), in_specs=[HBM] * n, out_specs=[HBM] * n,
        scratch_shapes=plan.scratch(), name=name,
    )(*ws)


def sibling_send_half(gs, name):
    n = len(gs)

    def body(*refs):
        g_refs, o_refs = refs[:n], refs[n:2 * n]
        send_sems, recv_sems = refs[2 * n:]
        x, y, c, _ = _place()
        cps = []
        for k in range(n):
            hr = gs[k].shape[1] // 2
            src = g_refs[k].at[:, pl.ds(pl.multiple_of((1 - c) * hr, 8), hr)]
            cp = pltpu.make_async_remote_copy(src_ref=src, dst_ref=o_refs[k], send_sem=send_sems.at[k], recv_sem=recv_sems.at[k],
                                              device_id=(x, y, 1 - c), device_id_type=MESH)
            cp.start()
            cps.append(cp)
        for cp in cps:
            cp.wait()

    return pl.pallas_call(
        body, out_shape=[SDS((N_SHARD, g.shape[1] // 2, g.shape[2]), g.dtype) for g in gs], in_specs=[HBM] * n, out_specs=[HBM] * n,
        scratch_shapes=[pltpu.SemaphoreType.DMA((n,)), pltpu.SemaphoreType.DMA((n,))], name=name,
    )(*gs)


def _scatter_copies(h_refs, o_refs, send_sems, recv_sems):
    _, _, c, chips = _place()
    return [pltpu.make_async_remote_copy(
        src_ref=h_refs[k].at[2 * chip[0] + chip[1]], dst_ref=o_refs[k].at[j], send_sem=send_sems.at[k, j],
        recv_sem=recv_sems.at[k, j], device_id=(*chip, c), device_id_type=MESH)
        for k in range(len(h_refs)) for j, chip in enumerate(chips)]


def chip_scatter(hs, name):
    n = len(hs)

    def body(*refs):
        cps = _scatter_copies(refs[:n], refs[n:2 * n], *refs[2 * n:])
        for cp in cps:
            cp.start()
        for cp in cps:
            cp.wait()

    return pl.pallas_call(
        body, out_shape=[SDS((3,) + h.shape[1:], h.dtype) for h in hs], in_specs=[HBM] * n, out_specs=[HBM] * n,
        scratch_shapes=[pltpu.SemaphoreType.DMA((n, 3)), pltpu.SemaphoreType.DMA((n, 3))], name=name,
    )(*hs)


def sibling_swap(ghs, name):
    n = len(ghs)

    def body(*refs):
        g_refs, o_refs = refs[:n], refs[n:2 * n]
        send_sems, recv_sems = refs[2 * n:]
        x, y, c, _ = _place()
        cps = []
        for k in range(n):
            cp = pltpu.make_async_remote_copy(src_ref=g_refs[k], dst_ref=o_refs[k], send_sem=send_sems.at[k],
                                              recv_sem=recv_sems.at[k], device_id=(x, y, 1 - c), device_id_type=MESH)
            cp.start()
            cps.append(cp)
        for cp in cps:
            cp.wait()

    return pl.pallas_call(
        body, out_shape=[SDS(g.shape, g.dtype) for g in ghs], in_specs=[HBM] * n, out_specs=[HBM] * n,
        scratch_shapes=[pltpu.SemaphoreType.DMA((n,)), pltpu.SemaphoreType.DMA((n,))], name=name,
    )(*ghs)


def pair_sums(gs, ras, cidx, name):
    n = len(gs)
    halves = [(g.shape[1] // 2, g.shape[2]) for g in gs]

    def body(c_ref, *refs):
        for g_ref, a_ref, o_ref in zip(refs[:n], refs[n:2 * n], refs[2 * n:]):
            o_ref[...] = (g_ref[...] + a_ref[...]).astype(BF16)

    mine = [pl.BlockSpec((1, hr, cols), lambda s, c_ref: (s, c_ref[0], 0)) for hr, cols in halves]
    whole = [pl.BlockSpec((1, hr, cols), lambda s, c_ref: (s, 0, 0)) for hr, cols in halves]
    return pl.pallas_call(
        body,
        grid_spec=pltpu.PrefetchScalarGridSpec(num_scalar_prefetch=1, grid=(N_SHARD,), in_specs=mine + whole, out_specs=whole),
        out_shape=[SDS((N_SHARD, hr, cols), BF16) for hr, cols in halves], compiler_params=_cp("arbitrary"), name=name,
    )(cidx, *gs, *ras)


def chip_sums_total(hs, rbs, sidx, name):
    n = len(hs)
    halves = [h.shape[1:] for h in hs]

    def body(s_ref, *refs):
        for h_ref, r_ref, o_ref in zip(refs[:n], refs[n:2 * n], refs[2 * n:]):
            o_ref[...] = ((h_ref[0].astype(F32) + r_ref[0].astype(F32)) + r_ref[1].astype(F32)) + r_ref[2].astype(F32)

    return pl.pallas_call(
        body,
        grid_spec=pltpu.PrefetchScalarGridSpec(
            num_scalar_prefetch=1, grid=(1,),
            in_specs=[pl.BlockSpec((1, hr, cols), lambda i, s_ref: (s_ref[0], 0, 0)) for hr, cols in halves]
            + [pl.BlockSpec((3, hr, cols), lambda i, s_ref: (0, 0, 0)) for hr, cols in halves],
            out_specs=[pl.BlockSpec((hr, cols), lambda i, s_ref: (0, 0)) for hr, cols in halves]),
        out_shape=[SDS((hr, cols), F32) for hr, cols in halves], compiler_params=_cp("arbitrary"), name=name,
    )(sidx, *hs, *rbs)


def _shard_cols(g, n_valid):
    r = g.shape[0]
    return g[:, :n_valid].reshape(r, N_SHARD, n_valid // N_SHARD).transpose(1, 0, 2)


def _unshard_cols(o, pad_to):
    _, r, n = o.shape
    full = o.transpose(1, 0, 2).reshape(r, N_SHARD * n)
    return jnp.pad(full, ((0, 0), (0, pad_to - N_SHARD * n)))


def _rows_of_tiles(t):
    B, H, S = t.shape
    return t.reshape(B, H, S // FT, 1, FT)


def mixer_fwd(x1, mod3, g_pre, w_main, w_f, b_forget_pad, goa, gob, w_out, g_post, tabs, nb, gather=None):
    hmix, pa, pb, flog = mixer_proj(x1, mod3, g_pre, w_main, w_f, *tabs, name="mixer_proj")
    out_a, lse_a = band_fwd(pa, name="band_fwd")
    F = forget_cumsum(flog.reshape(nb, SEQ, LANE), b_forget_pad, name="forget_cumsum")
    Fh = F[:, :, :NH].transpose(0, 2, 1)
    fblk = Fh.reshape(nb, NH, SEQ // FB, 1, FB)
    frow = _rows_of_tiles(Fh)
    (out_b, lse_b), gathered = fox_fwd(pb, Fh.reshape(nb, NH, SEQ // FOX_QB, 1, FOX_QB), frow, name="fox_fwd", gather=gather)
    x2, merged, y0m = mixer_out_fwd(out_a, out_b, goa, gob, w_out, g_post, x1, mod3, name="mixer_out_fwd")
    res = dict(hmix=hmix, flog=flog, pa=pa, pb=pb, out_a=out_a, lse_a=lse_a, fblk=fblk, frow=frow, out_b=out_b,
               lrow=_rows_of_tiles(lse_b.reshape(nb, NH, SEQ)), merged=merged, y0m=y0m)
    return x2, res, gathered


def mixer_bwd(dx2, x1, mod3, g_pre, w_main, w_f, b_forget_pad, goa, gob, w_out, g_post, tabs, res, nb):
    T = nb * SEQ
    dy0m, doa, dob, dmgate, dg_post, dgoa, dgob, dvec_b = mixer_out_bwd(
        dx2, res["y0m"], mod3, g_post, w_out, res["out_a"], res["out_b"], goa, gob, name="mixer_out_bwd")
    dqa, dka, dva = band_bwd(res["pa"], doa, res["out_a"], res["lse_a"], *tabs, name="band_bwd")
    drow = _rows_of_tiles(dvec_b[:, :NH].reshape(nb, SEQ, NH).transpose(0, 2, 1))
    dqb, dkb, dvb, dfq, dfk = fox_bwd(res["pb"], dob, res["lrow"], drow, res["fblk"], res["frow"], name="fox_bwd")
    dF = (dfq.reshape(nb, NH, SEQ) + dfk.reshape(nb, NH, SEQ)).transpose(0, 2, 1)
    dF = jnp.pad(dF, ((0, 0), (0, 0), (0, LANE - NH)))
    dflog, dbf = forget_cumsum_bwd(dF, res["flog"].reshape(nb, SEQ, LANE), b_forget_pad, name="forget_cumsum_bwd")
    dflog = dflog.reshape(T, LANE)
    dps = (dqa, dka, dva, dqb, dkb, dvb)
    dx1, dmod2, dg_pre = mixer_proj_bwd(dps, dflog, dx2, x1, mod3, g_pre, w_main, w_f, name="mixer_proj_bwd")
    g_main = matmul_tn_cols(res["hmix"], dps, 1024, name="grad_w_in")
    g_f = matmul_tn(res["hmix"], dflog.astype(BF16), D, LANE, 1024, name="grad_w_forget")
    g_out = matmul_tn(res["merged"], dy0m, D, D, 1024, name="grad_w_out")
    dmod3 = jnp.concatenate([dmod2, dmgate], axis=1)
    return dx1, dmod3, dict(g_pre=dg_pre, g_post=dg_post, goa=dgoa, gob=dgob, b_forget=dbf[:, :NH],
                            w_in=jnp.concatenate([g_main, g_f[:, :NH]], axis=1), w_out=g_out)


def ffn_grads(h, dy0, act, dgate, dup, pre, reduce=None):
    g_gate = matmul_tn(h, dgate, D, DFF_PAD, 1024, name=pre + "_grad_gate")
    if reduce is None:
        g_up = matmul_tn(h, dup, D, DFF_PAD, 1024, name=pre + "_grad_up")
        g_down = matmul_tn(act, dy0, FF_TN, D, 1024, name=pre + "_grad_down", rows=DFF)
        return (g_gate, g_up, g_down), {}
    hs_gate = reduce("gate", g_gate)
    g_up, rb_gate = matmul_tn(h, dup, D, DFF_PAD, 1024, name=pre + "_grad_up", scatter=hs_gate)
    hs_up = reduce("up", g_up)
    g_down, rb_up = matmul_tn(act, dy0, FF_TN, D, 1024, name=pre + "_grad_down", scatter=hs_up, rows=DFF)
    return (g_gate, g_up, g_down), {"gate": (hs_gate[0], rb_gate[0]), "up": (hs_up[0], rb_up[0])}


def local_step(x0, tgt, pos_col, mod, wfull, p, late_weights=None, last_weights=None, early_grads=None, last_reduce=None):
    T = x0.shape[0]
    nb = T // SEQ
    mod_ff1, mod_mix, mod_ff2 = mod[:, 0:3], mod[:, 3:6], mod[:, 6:9]
    tabs = rope_tables(pos_col, name="rope_tables")
    bf_pad = jnp.pad(p["b_forget"], ((0, 0), (0, LANE - NH)))

    (x1, h1, gate1, up1, y01), gathered = ffn_fwd(
        x0, mod_ff1, p["g_pre_ff1"], p["g_post_ff1"], wfull["w_ff1_gate"], wfull["w_ff1_up"], wfull["w_ff1_down"], 0.5,
        name="ff1_fwd", gather=None if late_weights is None else late_weights[:2])
    if late_weights is not None:
        wfull = {**wfull, **late_weights[2](gathered)}
    x2, res, gathered = mixer_fwd(x1, mod_mix, p["g_pre_mix"], wfull["w_main"], wfull["w_f"], bf_pad, p["g_out_a"],
                                  p["g_out_b"], wfull["w_out"], p["g_post_mix"], tabs, nb,
                                  gather=None if last_weights is None else last_weights[:2])
    if last_weights is not None:
        wfull = {**wfull, **last_weights[2](gathered)}
    (x3, h2, gate2, up2, y02), _ = ffn_fwd(x2, mod_ff2, p["g_pre_ff2"], p["g_post_ff2"], wfull["w_ff2_gate"],
                                           wfull["w_ff2_up"], wfull["w_ff2_down"], 0.5, name="ff2_fwd")

    (dx2, dy02, act2, dgate2, dup2, dmod_ff2, dgpre2, dgpost2), (loss_part,) = ffn_bwd(
        x3, x2, y02, mod_ff2, p["g_pre_ff2"], p["g_post_ff2"], gate2, up2, wfull["w_ff2_gate"], wfull["w_ff2_up"],
        wfull["w_ff2_down"], 0.5, name="ff2_bwd", target=tgt)
    gw = {}
    (gw["w_ff2_gate"], gw["w_ff2_up"], gw["w_ff2_down"]), _ = ffn_grads(h2, dy02, act2, dgate2, dup2, "ff2")
    dx1, dmod_mix, gmix = mixer_bwd(dx2, x1, mod_mix, p["g_pre_mix"], wfull["w_main"], wfull["w_f"], bf_pad, p["g_out_a"],
                                    p["g_out_b"], wfull["w_out"], p["g_post_mix"], tabs, res, nb)
    gw["w_in"], gw["w_out"] = gmix["w_in"], gmix["w_out"]
    (dx0, dy01, act1, dgate1, dup1, dmod_ff1, dgpre1, dgpost1), scattered = ffn_bwd(
        dx1, x0, y01, mod_ff1, p["g_pre_ff1"], p["g_post_ff1"], gate1, up1, wfull["w_ff1_gate"], wfull["w_ff1_up"],
        wfull["w_ff1_down"], 0.5, name="ff1_bwd", scatter=None if early_grads is None else early_grads(gw))
    (gw["w_ff1_gate"], gw["w_ff1_up"], gw["w_ff1_down"]), chained = ffn_grads(h1, dy01, act1, dgate1, dup1, "ff1", last_reduce)
    dmod = jnp.concatenate([dmod_ff1, dmod_mix, dmod_ff2], axis=1).reshape(nb, 9 * D)
    small = dict(g_pre_ff1=dgpre1, g_post_ff1=dgpost1, g_pre_mix=gmix["g_pre"], g_post_mix=gmix["g_post"], g_pre_ff2=dgpre2,
                 g_post_ff2=dgpost2, g_out_a=gmix["goa"], g_out_b=gmix["gob"], b_forget=gmix["b_forget"])
    return loss_part, dx0, dmod, gw, small, scattered, chained


def kernel(x, c, positions, w_ada, b_ada, g_pre_ff1, g_post_ff1, w_ff1_gate, w_ff1_up, w_ff1_down, g_pre_mix, g_post_mix, w_in, b_forget, g_out_a, g_out_b, w_out, g_pre_ff2, g_post_ff2, w_ff2_gate, w_ff2_up, w_ff2_down, loss_target, m_w_ada, m_b_ada, m_g_pre_ff1, m_g_post_ff1, m_w_ff1_gate, m_w_ff1_up, m_w_ff1_down, m_g_pre_mix, m_g_post_mix, m_w_in, m_b_forget, m_g_out_a, m_g_out_b, m_w_out, m_g_pre_ff2, m_g_post_ff2, m_w_ff2_gate, m_w_ff2_up, m_w_ff2_down, v_w_ada, v_b_ada, v_g_pre_ff1, v_g_post_ff1, v_w_ff1_gate, v_w_ff1_up, v_w_ff1_down, v_g_pre_mix, v_g_post_mix, v_w_in, v_b_forget, v_g_out_a, v_g_out_b, v_w_out, v_g_pre_ff2, v_g_post_ff2, v_w_ff2_gate, v_w_ff2_up, v_w_ff2_down):
    args = dict(locals())
    nb = x.shape[0]
    T = nb * SEQ
    ax, ay, ac = lax.axis_index("x"), lax.axis_index("y"), lax.axis_index("c")
    shard = 2 * ax + ay
    cidx = jnp.reshape(ac, (1,)).astype(jnp.int32)
    sidx = jnp.reshape(shard, (1,)).astype(jnp.int32)

    big = ["w_ff1_gate", "w_ff1_up", "w_ff1_down", "w_in", "w_out", "w_ff2_gate", "w_ff2_up", "w_ff2_down"]
    vecs = ["g_pre_ff1", "g_post_ff1", "g_pre_mix", "g_post_mix", "g_pre_ff2", "g_post_ff2"]

    first, late = big[:3], big[3:]
    splits = {n: -(-(args[n].shape[1] // 2) // BF16_ROW_TILE) * BF16_ROW_TILE for n in big}

    def assemble(names, gathered):
        out = {}
        for n, o in zip(names, gathered):
            if n.endswith("gate") or n.endswith("up"):
                out[n] = _unshard_cols(o, DFF_PAD)
            elif n.endswith("down"):
                out[n] = jnp.pad(o.reshape(DFF, D), ((0, DFF_PAD - DFF), (0, 0)))
            elif n == "w_in":
                full = _unshard_cols(o, IN_COLS)
                out["w_main"] = full[:, :IN_MAIN]
                out["w_f"] = jnp.pad(full[:, IN_MAIN:], ((0, 0), (0, LANE - NH)))
            else:
                out[n] = o.reshape(D, D)
        return out

    wfull = assemble(first, all_gather_shards([args[n][0].astype(BF16) for n in first], [splits[n] for n in first],
                                              name="all_gather_weights"))
    def gather_plan(names):
        return ([args[n][0].astype(BF16) for n in names], [splits[n] for n in names], functools.partial(assemble, names))

    late_weights, last_weights = gather_plan(late[:2]), gather_plan(late[2:])

    ncol = w_ada.shape[2]
    c_all = all_gather8(c, name="all_gather_c").reshape(N_DEV * nb, D)
    b_loc = lax.dynamic_slice(b_ada, (0, shard * ncol), (1, ncol))
    mod_loc = ada_fwd(c_all, w_ada[0], b_loc, name="ada_fwd")
    mod_g = all_gather8(mod_loc, name="all_gather_mod")
    row0 = (4 * ax + 2 * ay + ac) * nb
    mod_rows = lax.dynamic_slice(mod_g, (0, row0, 0), (N_DEV, nb, ncol))
    mod = jnp.concatenate([mod_rows[2 * s] for s in range(N_SHARD)], axis=-1).reshape(nb, 9, D)

    small_in = dict(g_pre_ff1=g_pre_ff1, g_post_ff1=g_post_ff1, g_pre_mix=g_pre_mix, g_post_mix=g_post_mix, g_pre_ff2=g_pre_ff2,
                    g_post_ff2=g_post_ff2, g_out_a=g_out_a, g_out_b=g_out_b, b_forget=b_forget)
    def shard_blocked(n, g):
        if n.endswith("gate") or n.endswith("up"):
            return _shard_cols(g, DFF)
        if n.endswith("down"):
            return g.reshape(N_SHARD, DFF // N_SHARD, D)
        if n == "w_in":
            return _shard_cols(g, IN_COLS)
        return g.reshape(N_SHARD, D // N_SHARD, D)

    def chip_sums(names, gw, tag):
        gsb = [shard_blocked(n, gw[n]) for n in names]
        ras = sibling_send_half(gsb, name="grad_sibling_send_" + tag)
        return pair_sums(gsb, ras, cidx, name="grad_pair_sum_" + tag)

    hs = {}

    def early_grads(gw):
        hs.update(zip(late, chip_sums(late, gw, "late")))
        return [hs[n] for n in late]

    def last_reduce(which, g):
        return chip_sums(["w_ff1_" + which], {"w_ff1_" + which: g}, which)

    loss_part, dx0, dmod, gw, small, rbs_late, chained = local_step(
        x.reshape(T, D), loss_target.reshape(T, D), positions.reshape(T, 1), mod, wfull, small_in, late_weights, last_weights,
        early_grads, last_reduce)

    dmod_all = all_gather8(dmod, name="all_gather_dmod").reshape(N_DEV * nb, 9 * D)
    dmod_loc = lax.dynamic_slice(dmod_all, (0, shard * ncol), (N_DEV * nb, ncol))
    g_w_ada = ada_bwd(c_all, dmod_loc, name="ada_bwd")

    rbs = dict(zip(late, rbs_late))
    for which, (h, rb) in chained.items():
        hs["w_ff1_" + which], rbs["w_ff1_" + which] = h, rb
    hs["w_ff1_down"] = chip_sums(["w_ff1_down"], gw, "down")[0]
    rbs["w_ff1_down"] = chip_scatter([hs["w_ff1_down"]], name="grad_chip_scatter")[0]
    ghs = []
    for part, names in enumerate((big[:4], big[4:])):
        ghs += chip_sums_total([hs[n] for n in names], [rbs[n] for n in names], sidx, name=f"grad_chip_sum_{part}")
    theirs = sibling_swap(ghs, name="grad_sibling_swap")

    row6 = jnp.concatenate([small["g_out_a"], small["g_out_b"]], axis=1)
    row7 = jnp.concatenate([small["b_forget"], loss_part[0:1, 0:1], jnp.zeros((1, D - NH - 1), F32)], axis=1)
    pack = jnp.concatenate([small[n] for n in vecs] + [row6, row7], axis=0)
    packed = all_gather8(pack, name="all_gather_small").reshape(N_DEV, 8 * D)

    names = vecs + ["g_out_a", "g_out_b", "b_forget"]
    layout = [(i * D, D) for i in range(len(vecs))] + [(6 * D, WG), (6 * D + WG, WG), (7 * D, NH)]
    per_param, packed_sum = small_adam(packed, layout, [args[n] for n in names], [args["m_" + n] for n in names],
                                       [args["v_" + n] for n in names], name="adam_small")
    outs = dict(grad={}, delta={}, new_m={}, new_v={})
    for n, (g, d, m2, v2) in zip(names, per_param):
        outs["grad"][n], outs["delta"][n], outs["new_m"][n], outs["new_v"][n] = g, d, m2, v2
    loss = packed_sum[0, 7 * D + NH]
    outs["grad"]["b_ada"], outs["delta"]["b_ada"], outs["new_m"]["b_ada"], outs["new_v"]["b_ada"] = vec_adam(
        dmod_all, b_ada, m_b_ada, v_b_ada, name="adam_b_ada")

    for n, mine, other in zip(big, ghs, theirs):
        tr = 128 if mine.shape[0] % 128 == 0 else mine.shape[0]
        outs["grad"][n], outs["delta"][n], outs["new_m"][n], outs["new_v"][n] = adam_update_halves(
            args[n], mine, other, args["m_" + n], args["v_" + n], cidx, tr, name="adam_" + n)
    outs["delta"]["w_ada"], outs["new_m"]["w_ada"], outs["new_v"]["w_ada"] = adam_update(
        w_ada, g_w_ada, m_w_ada, v_w_ada, 128, name="adam_w_ada")
    outs["grad"]["w_ada"] = g_w_ada[None]

    order = ["w_ada", "b_ada", "g_pre_ff1", "g_post_ff1", "w_ff1_gate", "w_ff1_up", "w_ff1_down", "g_pre_mix", "g_post_mix", "w_in",
             "b_forget", "g_out_a", "g_out_b", "w_out", "g_pre_ff2", "g_post_ff2", "w_ff2_gate", "w_ff2_up", "w_ff2_down"]
    result = [loss, dx0.reshape(nb, SEQ, D)]
    for kind in ("grad", "delta", "new_m", "new_v"):
        result += [outs[kind][n] for n in order]
    return tuple(result)
```

```python
import functools
import math

import jax
import jax.numpy as jnp
from jax import lax
from jax.experimental import pallas as pl
from jax.experimental.pallas import tpu as pltpu

D = 1024
SEQ = 2048
HD = 64
NH = 8
WG = NH * HD
DFF = 2752
DFF_PAD = 2816
IN_MAIN = 6 * WG
IN_COLS = IN_MAIN + NH
N_SHARD = 4
N_DEV = 8
LANE = 128
BF16_ROW_TILE = 16
QB = 128
ROWS = 256
FB = 512
FT = 512
FOX_QB = 512
FOX_PAIRS = 4
FOX_PAIRS_BWD = 2
BAND_UNROLL = 8
BAND_UNROLL_BWD = 8
PATTERNS = ((1, 16), (4, 4), (16, 1))
ROPE_THETA = 500000.0
EPS = 1e-6
NEG = -1e30
ATTN_SCALE = HD ** -0.5
TM = 512
TM_FFN = 512
TM_BWD = 256
VMEM_LIMIT = 56 * 1024 * 1024

ADAM_LR, ADAM_B1, ADAM_B2, ADAM_EPS, ADAM_WD, ADAM_STEP = 0.001, 0.9, 0.999, 1e-08, 0.01, 10

F32 = jnp.float32
BF16 = jnp.bfloat16
MESH = pl.DeviceIdType.MESH
SDS = jax.ShapeDtypeStruct


def _cp(*sem):
    return pltpu.CompilerParams(dimension_semantics=sem, vmem_limit_bytes=VMEM_LIMIT)


def _dot(a, b):
    return jnp.dot(a, b, preferred_element_type=F32)


def _dot_nt(a, b):
    return lax.dot_general(a, b, (((1,), (1,)), ((), ())), preferred_element_type=F32)


def _dot_tn(a, b):
    return lax.dot_general(a, b, (((0,), (0,)), ((), ())), preferred_element_type=F32)


def _rms(xf):
    return lax.rsqrt(jnp.mean(xf * xf, axis=-1, keepdims=True) + EPS)


def _norm_mod_bwd(dh, xf, g, scale):
    r = _rms(xf)
    xh = xf * r
    dsh = jnp.sum(dh, axis=0, keepdims=True)
    dsc = jnp.sum(dh * (xh * g), axis=0, keepdims=True)
    dn = dh * (1.0 + scale)
    dg = jnp.sum(dn * xh, axis=0, keepdims=True)
    dxh = dn * g
    dx = r * (dxh - xh * jnp.mean(dxh * xh, axis=-1, keepdims=True))
    return dx, dsh, dsc, dg


def _post_bwd(dxo, y0, g, mgate, gs):
    r = _rms(y0)
    yh = y0 * r
    dmg = gs * jnp.sum(dxo * (yh * g), axis=0, keepdims=True)
    dy = (gs * mgate) * dxo
    dg = jnp.sum(dy * yh, axis=0, keepdims=True)
    dyh = dy * g
    dy0 = r * (dyh - yh * jnp.mean(dyh * yh, axis=-1, keepdims=True))
    return dy0, dmg, dg


def _mod_map(i, *_):
    return ((i * TM) // SEQ, 0, 0)


FF_TN = 1408
FF_TILES = ((0, 768), (768, 1536), (1536, 2304), (2304, 2816))


def _resident_scratch():
    return [pltpu.VMEM((D, DFF_PAD), BF16), pltpu.VMEM((D, DFF_PAD), BF16), pltpu.VMEM((DFF_PAD, D), BF16),
            pltpu.SemaphoreType.DMA((3,))]


def _load_resident(first_step, srcs, dsts, sems):
    @pl.when(first_step)
    def _():
        cps = [pltpu.make_async_copy(s, d, sems.at[k]) for k, (s, d) in enumerate(zip(srcs, dsts))]
        for cp in cps:
            cp.start()
        for cp in cps:
            cp.wait()


def ffn_fwd(x, mod3, g_pre, g_post, wg, wu, wd, gs, name, gather=None):
    T = x.shape[0]
    tm = TM_FFN
    ng = 0 if gather is None else len(gather[0])
    plan = None if gather is None else ShardGather([w.shape for w in gather[0]], gather[1])

    def body(*refs):
        x_ref, mod_ref, gpre_ref, gpost_ref = refs[:4]
        xo_ref, h_ref, gate_ref, up_ref, y0_ref = refs[7 + ng:12 + ng]
        wg_ref, wu_ref, wd_ref, wsem = refs[12 + 2 * ng:16 + 2 * ng]
        i = pl.program_id(0)
        if plan is not None:
            comm = (refs[7:7 + ng], refs[12 + ng:12 + 2 * ng], refs[16 + 2 * ng:])
            pl.when(i == 0)(lambda: plan.start(*comm))
        _load_resident(i == 0, refs[4:7], (wg_ref, wu_ref, wd_ref), wsem)

        xf = x_ref[...]
        hb = ((xf * _rms(xf) * gpre_ref[...]) * (1.0 + mod_ref[0, 1:2, :]) + mod_ref[0, 0:1, :]).astype(BF16)
        h_ref[...] = hb
        y0 = None
        for lo, hi in FF_TILES:
            gate = _dot(hb, wg_ref[:, lo:hi])
            up = _dot(hb, wu_ref[:, lo:hi])
            gate_ref[:, lo:hi] = gate.astype(BF16)
            up_ref[:, lo:hi] = up.astype(BF16)
            part = _dot((gate * jax.nn.sigmoid(gate) * up).astype(BF16), wd_ref[lo:hi, :])
            y0 = part if y0 is None else y0 + part
        y0_ref[...] = y0
        xo_ref[...] = xf + (gs * mod_ref[0, 2:3, :]) * (y0 * _rms(y0) * gpost_ref[...])

        if plan is not None:
            pl.when(i == T // tm - 1)(lambda: plan.finish(*comm))

    tok = pl.BlockSpec((tm, D), lambda i: (i, 0))
    vec = pl.BlockSpec((1, D), lambda i: (0, 0))
    hid = pl.BlockSpec((tm, DFF_PAD), lambda i: (i, 0))
    outs = pl.pallas_call(
        body, grid=(T // tm,),
        in_specs=[tok, pl.BlockSpec((1, 3, D), lambda i: ((i * tm) // SEQ, 0, 0)), vec, vec, HBM, HBM, HBM] + [HBM] * ng,
        out_specs=[tok, tok, hid, hid, tok] + [HBM] * ng,
        out_shape=[SDS((T, D), F32), SDS((T, D), BF16), SDS((T, DFF_PAD), BF16), SDS((T, DFF_PAD), BF16), SDS((T, D), F32)]
        + ([] if plan is None else plan.out_shapes(BF16)),
        scratch_shapes=_resident_scratch() + ([] if plan is None else plan.scratch()),
        compiler_params=_cp("arbitrary"), name=name,
    )(x, mod3, g_pre, g_post, wg, wu, wd, *([] if gather is None else gather[0]))
    return outs[:5], outs[5:]


def ffn_bwd(dxo, x, y0, mod3, g_pre, g_post, gate, up, wg, wu, wd, gs, name, scatter=None, target=None):
    assert scatter is None or target is None
    T = x.shape[0]
    nb = T // SEQ
    tm = TM_BWD
    tiles_per_seq = SEQ // tm
    ns = 0 if scatter is None else len(scatter)
    ne = ns + (target is not None)

    def body(*refs):
        dxo_ref, x_ref, y0_ref, mod_ref, gpre_ref, gpost_ref, gate_ref, up_ref = refs[:8]
        dx_ref, dy0_ref, act_ref, dgate_ref, dup_ref, dmod_ref, dgpre_ref, dgpost_ref = refs[11 + ne:19 + ne]
        wg_ref, wu_ref, wd_ref, wsem = refs[19 + 2 * ne:23 + 2 * ne]
        i = pl.program_id(0)
        _load_resident(i == 0, refs[8:11], (wg_ref, wu_ref, wd_ref), wsem)
        if ns:
            comm = (refs[11:11 + ns], refs[19 + ns:19 + 2 * ns], *refs[23 + 2 * ns:])

            @pl.when(i == 0)
            def _():
                for cp in _scatter_copies(*comm):
                    cp.start()

        @pl.when(i == 0)
        def _():
            dgpre_ref[...] = jnp.zeros_like(dgpre_ref)
            dgpost_ref[...] = jnp.zeros_like(dgpost_ref)

        @pl.when(i % tiles_per_seq == 0)
        def _():
            dmod_ref[...] = jnp.zeros_like(dmod_ref)

        dxo = dxo_ref[...]
        if target is not None:
            loss_ref = refs[19 + ne]

            @pl.when(i == 0)
            def _():
                loss_ref[...] = jnp.zeros_like(loss_ref)

            err = dxo - refs[11][...]
            loss_ref[...] += jnp.sum(err * err) * (0.5 / D)
            dxo = err * (1.0 / D)
        dy0, dmg, dg = _post_bwd(dxo, y0_ref[...], gpost_ref[...], mod_ref[0, 2:3, :], gs)
        dmod_ref[0, 2:3, :] += dmg
        dgpost_ref[...] += dg
        db = dy0.astype(BF16)
        dy0_ref[...] = db
        dh = None
        for lo, hi in FF_TILES:
            dact = _dot_nt(db, wd_ref[lo:hi, :])
            g = gate_ref[:, lo:hi].astype(F32)
            u = up_ref[:, lo:hi].astype(F32)
            sig = jax.nn.sigmoid(g)
            sl = g * sig
            dgate = (dact * u * (sig * (1.0 + g * (1.0 - sig)))).astype(BF16)
            dup = (dact * sl).astype(BF16)
            act_ref[:, lo:hi] = (sl * u).astype(BF16)
            dgate_ref[:, lo:hi] = dgate
            dup_ref[:, lo:hi] = dup
            part = _dot_nt(dgate, wg_ref[:, lo:hi]) + _dot_nt(dup, wu_ref[:, lo:hi])
            dh = part if dh is None else dh + part
        dx, dsh, dsc, dg = _norm_mod_bwd(dh, x_ref[...], gpre_ref[...], mod_ref[0, 1:2, :])
        dx_ref[...] = dxo + dx
        dmod_ref[0, 0:1, :] += dsh
        dmod_ref[0, 1:2, :] += dsc
        dgpre_ref[...] += dg

        if ns:
            @pl.when(i == T // tm - 1)
            def _():
                for cp in _scatter_copies(*comm):
                    cp.wait()

    tok = pl.BlockSpec((tm, D), lambda i: (i, 0))
    vec = pl.BlockSpec((1, D), lambda i: (0, 0))
    hid = pl.BlockSpec((tm, DFF_PAD), lambda i: (i, 0))
    modspec = pl.BlockSpec((1, 3, D), lambda i: ((i * tm) // SEQ, 0, 0))
    outs = pl.pallas_call(
        body, grid=(T // tm,),
        in_specs=[tok, tok, tok, modspec, vec, vec, hid, hid, HBM, HBM, HBM] + [HBM] * ns + [tok] * (ne - ns),
        out_specs=[tok, tok, hid, hid, hid, modspec, vec, vec] + [HBM] * ns
        + [pl.BlockSpec((8, LANE), lambda i: (0, 0))] * (ne - ns),
        out_shape=[SDS((T, D), F32), SDS((T, D), BF16), SDS((T, DFF_PAD), BF16), SDS((T, DFF_PAD), BF16),
                   SDS((T, DFF_PAD), BF16), SDS((nb, 3, D), F32), SDS((1, D), F32), SDS((1, D), F32)]
        + [SDS((3,) + h.shape[1:], h.dtype) for h in (scatter or [])] + [SDS((8, LANE), F32)] * (ne - ns),
        scratch_shapes=_resident_scratch()
        + ([pltpu.SemaphoreType.DMA((ns, 3)), pltpu.SemaphoreType.DMA((ns, 3))] if ns else []),
        compiler_params=_cp("arbitrary"), name=name,
    )(dxo, x, y0, mod3, g_pre, g_post, gate, up, wg, wu, wd, *(scatter or []), *([] if target is None else [target]))
    return outs[:8], outs[8:]


def matmul_tn(a, b, tm, tn, tk, name, scatter=None, rows=None):
    T, M = a.shape
    N = b.shape[1]
    grid = (M // tm, N // tn, T // tk)
    ns = 0 if scatter is None else len(scatter)

    def body(*refs):
        a_ref, b_ref = refs[:2]
        o_ref = refs[2 + ns]
        ids = [pl.program_id(ax) for ax in range(3)]
        if ns:
            comm = (refs[2:2 + ns], refs[3 + ns:3 + 2 * ns], *refs[3 + 2 * ns:])

            @pl.when((ids[0] == 0) & (ids[1] == 0) & (ids[2] == 0))
            def _():
                for cp in _scatter_copies(*comm):
                    cp.start()

        @pl.when(ids[2] == 0)
        def _():
            o_ref[...] = jnp.zeros_like(o_ref)

        o_ref[...] += _dot_tn(a_ref[...], b_ref[...])

        if ns:
            @pl.when((ids[0] == grid[0] - 1) & (ids[1] == grid[1] - 1) & (ids[2] == grid[2] - 1))
            def _():
                for cp in _scatter_copies(*comm):
                    cp.wait()

    outs = pl.pallas_call(
        body, grid=grid,
        in_specs=[pl.BlockSpec((tk, tm), lambda i, j, k: (k, i)), pl.BlockSpec((tk, tn), lambda i, j, k: (k, j))] + [HBM] * ns,
        out_specs=[pl.BlockSpec((tm, tn), lambda i, j, k: (i, j))] + [HBM] * ns,
        out_shape=[SDS((rows or M, N), F32)] + [SDS((3,) + h.shape[1:], h.dtype) for h in (scatter or [])],
        scratch_shapes=[pltpu.SemaphoreType.DMA((ns, 3)), pltpu.SemaphoreType.DMA((ns, 3))] if ns else [],
        compiler_params=_cp("arbitrary", "arbitrary", "arbitrary"), name=name,
    )(a, b, *(scatter or []))
    return outs[0] if scatter is None else (outs[0], outs[1:])


def matmul_tn_cols(a, bs, tk, name):
    T, M = a.shape
    n = bs[0].shape[1]
    ng = len(bs)

    def body(*refs):
        a_ref, b_refs, o_ref = refs[0], refs[1:1 + ng], refs[1 + ng]

        @pl.when(pl.program_id(0) == 0)
        def _():
            o_ref[...] = jnp.zeros_like(o_ref)

        av = a_ref[...]
        for g, b_ref in enumerate(b_refs):
            o_ref[:, g * n:(g + 1) * n] += _dot_tn(av, b_ref[...])

    return pl.pallas_call(
        body, grid=(T // tk,),
        in_specs=[pl.BlockSpec((tk, M), lambda k: (k, 0))] + [pl.BlockSpec((tk, n), lambda k: (k, 0))] * ng,
        out_specs=pl.BlockSpec((M, ng * n), lambda k: (0, 0)), out_shape=SDS((M, ng * n), F32),
        compiler_params=_cp("arbitrary"), name=name,
    )(a, *bs)


def rope_tables(pos_col, name):
    T = pos_col.shape[0]
    tm = 1024

    def body(p_ref, c_ref, s1_ref, s2_ref):
        lane = lax.broadcasted_iota(jnp.int32, (1, LANE), 1)
        l64 = lane % HD
        inv_freq = jnp.exp((l64 % 8).astype(F32) * (-math.log(ROPE_THETA) / 8.0))
        ang = p_ref[...].astype(F32) * inv_freq
        cs = jnp.cos(ang)
        sn = jnp.sin(ang)
        c_ref[...] = jnp.where(l64 < 16, cs, 1.0)
        s1_ref[...] = jnp.where(l64 < 8, -sn, 0.0)
        s2_ref[...] = jnp.where((l64 >= 8) & (l64 < 16), sn, 0.0)

    tab = pl.BlockSpec((tm, LANE), lambda i: (i, 0))
    return pl.pallas_call(
        body, grid=(T // tm,), in_specs=[pl.BlockSpec((tm, 1), lambda i: (i, 0))], out_specs=[tab, tab, tab],
        out_shape=[SDS((T, LANE), F32)] * 3, compiler_params=_cp("arbitrary"), name=name,
    )(pos_col)


def mixer_proj(x, mod3, g_pre, w_main, w_f, rc, rs1, rs2, name):
    T = x.shape[0]

    def body(x_ref, mod_ref, g_ref, w_ref, wf_ref, c_ref, s1_ref, s2_ref, h_ref, pa_ref, pb_ref, f_ref):
        xf = x_ref[...]
        h = (xf * _rms(xf) * g_ref[...]) * (1.0 + mod_ref[0, 1:2, :]) + mod_ref[0, 0:1, :]
        hb = h.astype(BF16)
        h_ref[...] = hb
        f_ref[...] = _dot(hb, wf_ref[...])
        c, s1, s2 = c_ref[...], s1_ref[...], s2_ref[...]
        for grp in range(2):
            pr = _dot(hb, w_ref[:, grp * WG:(grp + 1) * WG])
            for k in range(WG // LANE):
                t = pr[:, k * LANE:(k + 1) * LANE]
                pa_ref[:, grp * WG + k * LANE:grp * WG + (k + 1) * LANE] = (
                    t * c + pltpu.roll(t, LANE - 8, 1) * s1 + pltpu.roll(t, 8, 1) * s2)
        pa_ref[:, 2 * WG:3 * WG] = _dot(hb, w_ref[:, 2 * WG:3 * WG])
        for grp in range(3):
            pb_ref[:, grp * WG:(grp + 1) * WG] = _dot(hb, w_ref[:, (3 + grp) * WG:(4 + grp) * WG]).astype(BF16)

    tok = pl.BlockSpec((TM, D), lambda i: (i, 0))
    vec = pl.BlockSpec((1, D), lambda i: (0, 0))
    tab = pl.BlockSpec((TM, LANE), lambda i: (i, 0))
    grp3 = pl.BlockSpec((TM, 3 * WG), lambda i: (i, 0))
    return pl.pallas_call(
        body, grid=(T // TM,),
        in_specs=[tok, pl.BlockSpec((1, 3, D), _mod_map), vec, pl.BlockSpec((D, IN_MAIN), lambda i: (0, 0)),
                  pl.BlockSpec((D, LANE), lambda i: (0, 0)), tab, tab, tab],
        out_specs=[tok, grp3, grp3, tab],
        out_shape=[SDS((T, D), BF16), SDS((T, 3 * WG), F32), SDS((T, 3 * WG), BF16), SDS((T, LANE), F32)],
        compiler_params=_cp("arbitrary"), name=name,
    )(x, mod3, g_pre, w_main, w_f, rc, rs1, rs2)


def _head_lanes():
    return lax.broadcasted_iota(jnp.int32, (1, LANE), 1) < HD


def _pair(m0, a, b):
    return jnp.where(m0, a, b)


def _band_rows(i, d, nbc):
    if nbc == 1:
        return i, i, 0
    r, mb = i // nbc, i % nbc
    return r + mb * (QB * d), r + jnp.maximum(mb - 1, 0) * (QB * d), jnp.where(mb > 0, QB, 0)


def _rows(start, size, d):
    return pl.ds(pl.multiple_of(start, QB), size) if d == 1 else pl.ds(start, size, stride=d)


def _band_valid(span, off):
    rq = lax.broadcasted_iota(jnp.int32, (QB, span), 0)
    rel = lax.broadcasted_iota(jnp.int32, (QB, span), 1) - off
    return (rel <= rq) & (rel >= rq - QB)


def band_fwd(pa, name):
    T = pa.shape[0]
    B = T // SEQ
    NP = WG // LANE

    def body(q_ref, k_ref, v_ref, out_ref, lse_ref, o_s, l_s):
        m0 = _head_lanes()
        for pidx, (d, nbc) in enumerate(PATTERNS):
            span = QB if nbc == 1 else 2 * QB

            def blk(it, carry, pidx=pidx, d=d, nbc=nbc, span=span):
                ld = []
                for u in range(BAND_UNROLL):
                    qs, ks, off = _band_rows(it * BAND_UNROLL + u, d, nbc)
                    q = q_ref[_rows(qs, QB, d), :] * ATTN_SCALE
                    ld.append((qs, q, k_ref[_rows(ks, span, d), :].astype(BF16), v_ref[_rows(ks, span, d), :].astype(BF16),
                               _band_valid(span, off)))
                ss = [[jnp.where(valid, _dot_nt(jnp.where(mh, q, 0.0).astype(BF16), k), NEG) for mh in (m0, jnp.logical_not(m0))]
                      for _, q, k, _, valid in ld]
                ps = []
                for pair in ss:
                    row = []
                    for s in pair:
                        m = jnp.max(s, axis=-1, keepdims=True)
                        p = jnp.exp(s - m)
                        row.append((p.astype(BF16), jnp.sum(p, axis=-1, keepdims=True), m))
                    ps.append(row)
                pv = [[_dot(p, ld[u][3]) for p, _, _ in ps[u]] for u in range(BAND_UNROLL)]
                for u in range(BAND_UNROLL):
                    rows = _rows(ld[u][0], QB, d)
                    (_, l0, mx0), (_, l1, mx1) = ps[u]
                    o_s[pidx, rows, :] = _pair(m0, pv[u][0] / l0, pv[u][1] / l1)
                    l_s[pidx, rows, :] = _pair(m0, mx0 + jnp.log(l0), mx1 + jnp.log(l1))
                return carry

            lax.fori_loop(0, SEQ // QB // BAND_UNROLL, blk, 0)
        for c in range(SEQ // ROWS):
            sl = slice(c * ROWS, (c + 1) * ROWS)
            a, b, e = l_s[0, sl, :], l_s[1, sl, :], l_s[2, sl, :]
            m = jnp.maximum(jnp.maximum(a, b), e)
            L = m + jnp.log(jnp.exp(a - m) + jnp.exp(b - m) + jnp.exp(e - m))
            out_ref[sl, :] = jnp.exp(a - L) * o_s[0, sl, :] + jnp.exp(b - L) * o_s[1, sl, :] + jnp.exp(e - L) * o_s[2, sl, :]
            lse_ref[sl, :] = L

    blk_of = lambda g: pl.BlockSpec((SEQ, LANE), lambda b, hp, g=g: (b, g * NP + hp))
    return pl.pallas_call(
        body, grid=(B, NP), in_specs=[blk_of(0), blk_of(1), blk_of(2)], out_specs=[blk_of(0), blk_of(0)],
        out_shape=[SDS((T, WG), F32), SDS((T, WG), F32)],
        scratch_shapes=[pltpu.VMEM((3, SEQ, LANE), F32), pltpu.VMEM((3, SEQ, LANE), F32)],
        compiler_params=_cp("arbitrary", "arbitrary"), name=name,
    )(pa, pa, pa)


def _pair_rowsum(m0, prod):
    s0 = jnp.sum(jnp.where(m0, prod, 0.0), axis=-1, keepdims=True)
    return _pair(m0, s0, jnp.sum(prod, axis=-1, keepdims=True) - s0)


def band_bwd(pa, do, out, lse, rc, rs1, rs2, name):
    T = pa.shape[0]
    B = T // SEQ
    NP = WG // LANE

    def body(q_ref, k_ref, v_ref, do_ref, out_ref, l_ref, c_ref, s1_ref, s2_ref, dqo_ref, dko_ref, dvo_ref, d_s, dq_ref, dk_ref,
             dv_ref):
        m0 = _head_lanes()
        dq_ref[...] = jnp.zeros_like(dq_ref)
        dk_ref[...] = jnp.zeros_like(dk_ref)
        dv_ref[...] = jnp.zeros_like(dv_ref)
        for c in range(SEQ // ROWS):
            sl = slice(c * ROWS, (c + 1) * ROWS)
            d_s[sl, :] = _pair_rowsum(m0, do_ref[sl, :] * out_ref[sl, :])
        for d, nbc in PATTERNS:
            span = QB if nbc == 1 else 2 * QB

            def blk(it, carry, d=d, nbc=nbc, span=span):
                masks = (m0, jnp.logical_not(m0))
                ld = []
                for u in range(BAND_UNROLL_BWD):
                    qs, ks, off = _band_rows(it * BAND_UNROLL_BWD + u, d, nbc)
                    qrow, krow = _rows(qs, QB, d), _rows(ks, span, d)
                    ld.append(dict(qrow=qrow, krow=krow, q=q_ref[qrow, :] * ATTN_SCALE, k=k_ref[krow, :].astype(BF16),
                                   v=v_ref[krow, :].astype(BF16), do=do_ref[qrow, :], l=l_ref[qrow, :], dv=d_s[qrow, :],
                                   valid=_band_valid(span, off)))
                for t in ld:
                    t["qm"] = [jnp.where(mh, t["q"], 0.0).astype(BF16) for mh in masks]
                    t["dom"] = [jnp.where(mh, t["do"], 0.0).astype(BF16) for mh in masks]
                sd = [[(jnp.where(t["valid"], _dot_nt(t["qm"][h], t["k"]), NEG), _dot_nt(t["dom"][h], t["v"])) for h in range(2)]
                      for t in ld]
                pd = []
                for t, pair in zip(ld, sd):
                    row = []
                    for h, (s, dp) in enumerate(pair):
                        col = slice(h * HD, h * HD + 1)
                        p = jnp.exp(s - t["l"][:, col])
                        row.append((p.astype(BF16), (p * (dp - t["dv"][:, col])).astype(BF16)))
                    pd.append(row)
                gr = [(_dot(row[0][1], t["k"]), _dot(row[1][1], t["k"]),
                       _dot_tn(jnp.concatenate([row[0][1], row[1][1]], axis=0), jnp.concatenate(t["qm"], axis=0)),
                       _dot_tn(jnp.concatenate([row[0][0], row[1][0]], axis=0), jnp.concatenate(t["dom"], axis=0)))
                      for t, row in zip(ld, pd)]
                for t, (dq0, dq1, dk, dv) in zip(ld, gr):
                    dq_ref[t["qrow"], :] += _pair(m0, dq0, dq1) * ATTN_SCALE
                    dk_ref[t["krow"], :] += dk
                    dv_ref[t["krow"], :] += dv
                return carry

            lax.fori_loop(0, SEQ // QB // BAND_UNROLL_BWD, blk, 0)
        for c in range(SEQ // ROWS):
            sl = slice(c * ROWS, (c + 1) * ROWS)
            cc, s1, s2 = c_ref[sl, :], s1_ref[sl, :], s2_ref[sl, :]
            for acc, o_ref in ((dq_ref, dqo_ref), (dk_ref, dko_ref)):
                d = acc[sl, :]
                o_ref[sl, :] = (d * cc + pltpu.roll(d * s1, 8, 1) + pltpu.roll(d * s2, LANE - 8, 1)).astype(BF16)
            dvo_ref[sl, :] = dv_ref[sl, :].astype(BF16)

    blk_of = lambda g: pl.BlockSpec((SEQ, LANE), lambda b, hp, g=g: (b, g * NP + hp))
    tab = pl.BlockSpec((SEQ, LANE), lambda b, hp: (b, 0))
    return pl.pallas_call(
        body, grid=(B, NP), in_specs=[blk_of(0), blk_of(1), blk_of(2), blk_of(0), blk_of(0), blk_of(0), tab, tab, tab],
        out_specs=[blk_of(0)] * 3, out_shape=[SDS((T, WG), BF16)] * 3,
        scratch_shapes=[pltpu.VMEM((SEQ, LANE), F32)] * 4,
        compiler_params=_cp("arbitrary", "arbitrary"), name=name,
    )(pa, pa, pa, do, out, lse, rc, rs1, rs2)


def _tile_causal(nq, nk, q0, k0):
    r = lax.broadcasted_iota(jnp.int32, (nq, nk), 0)
    c = lax.broadcasted_iota(jnp.int32, (nq, nk), 1)
    return r + (q0 - k0) >= c


def _row_to_col(row):
    n = row.shape[1]
    return jnp.transpose(jnp.broadcast_to(row, (LANE, n)))[:, 0:1]


def _col_to_row(col):
    n = col.shape[0]
    return jnp.transpose(jnp.broadcast_to(col, (n, LANE)))[0:1, :]


def fox_fwd(pb, fblk, frow, name, gather=None):
    FQ = FOX_QB
    T = pb.shape[0]
    B = T // SEQ
    NG = WG // (LANE * FOX_PAIRS)
    NHS = 2 * FOX_PAIRS
    W = LANE * FOX_PAIRS
    n = SEQ // FQ
    ng = 0 if gather is None else len(gather[0])
    plan = None if gather is None else ShardGather([w.shape for w in gather[0]], gather[1])

    def body(*refs):
        q_ref, k_ref, v_ref, fc_ref, fr_ref = refs[:5]
        o_ref, lse_ref = refs[5 + ng:7 + ng]
        if plan is not None:
            comm = (refs[5:5 + ng], refs[7 + ng:7 + 2 * ng], refs[7 + 2 * ng:])
            ids = [pl.program_id(ax) for ax in range(3)]
            pl.when((ids[0] == 0) & (ids[1] == 0) & (ids[2] == 0))(lambda: plan.start(*comm))
        i = pl.program_id(2)
        m0 = _head_lanes()
        masks = (m0, jnp.logical_not(m0))
        heads = [(hh, slice((hh // 2) * LANE, (hh // 2 + 1) * LANE), masks[hh % 2]) for hh in range(NHS)]
        qh, fq = [], []
        for hh, lanes, mh in heads:
            q = q_ref[:, lanes] * ATTN_SCALE
            qh.append(jnp.where(mh, q, jnp.zeros_like(q)))
            fq.append(_row_to_col(fc_ref[0, hh, 0]))

        def step(t, carry, masked):
            rows = pl.ds(pl.multiple_of(t * FT, FT), FT)
            ss = [_dot_nt(qh[hh], k_ref[rows, lanes]) + fq[hh] - fr_ref[0, hh, t] for hh, lanes, _ in heads]
            if masked:
                ok = _tile_causal(FQ, FT, i * FQ, t * FT)
                ss = [jnp.where(ok, s, NEG) for s in ss]
            st = []
            for hh, _, _ in heads:
                m2 = jnp.maximum(carry[hh][0], jnp.max(ss[hh], axis=-1, keepdims=True))
                st.append((m2, jnp.exp(carry[hh][0] - m2), jnp.exp(ss[hh] - m2).astype(BF16)))
            pv = []
            for hh, lanes, mh in heads:
                vt = v_ref[rows, lanes]
                pv.append(_dot(st[hh][2], jnp.where(mh, vt, jnp.ones_like(vt))))
            return tuple((st[hh][0], st[hh][1] * carry[hh][1] + pv[hh]) for hh in range(NHS))

        one = (jnp.full((FQ, 1), NEG, F32), jnp.zeros((FQ, LANE), F32))
        last = (i * FQ) // FT
        carry = lax.fori_loop(0, last, lambda t, cr: step(t, cr, False), (one,) * NHS)
        carry = step(last, carry, True)
        for pr in range(FOX_PAIRS):
            (ma, acca), (mb, accb) = carry[2 * pr], carry[2 * pr + 1]
            la, lb = acca[:, HD:HD + 1], accb[:, 0:1]
            o_ref[:, pr * LANE:(pr + 1) * LANE] = _pair(m0, acca / la, accb / lb)
            lse_ref[0, 2 * pr, 0] = _col_to_row(ma + jnp.log(la))
            lse_ref[0, 2 * pr + 1, 0] = _col_to_row(mb + jnp.log(lb))
        if plan is not None:
            pl.when((ids[0] == B - 1) & (ids[1] == NG - 1) & (ids[2] == n - 1))(lambda: plan.finish(*comm))

    qblk = pl.BlockSpec((FQ, W), lambda b, g, i: (b * n + i, g))
    full = lambda grp: pl.BlockSpec((SEQ, W), lambda b, g, i, grp=grp: (b, grp * NG + g))
    rowb = pl.BlockSpec((1, NHS, 1, 1, FQ), lambda b, g, i: (b, g, i, 0, 0))
    outs = pl.pallas_call(
        body, grid=(B, NG, n),
        in_specs=[qblk, full(1), full(2), rowb, pl.BlockSpec((1, NHS, SEQ // FT, 1, FT), lambda b, g, i: (b, g, 0, 0, 0))]
        + [HBM] * ng,
        out_specs=[qblk, rowb] + [HBM] * ng,
        out_shape=[SDS((T, WG), F32), SDS((B, NH, n, 1, FQ), F32)] + ([] if plan is None else plan.out_shapes(BF16)),
        scratch_shapes=[] if plan is None else plan.scratch(),
        compiler_params=_cp("arbitrary", "arbitrary", "arbitrary"), name=name,
    )(pb, pb, pb, fblk, frow, *([] if gather is None else gather[0]))
    return outs[:2], outs[2:]


def fox_bwd(pb, do, lrow, drow, fblk, frow, name):
    T = pb.shape[0]
    B = T // SEQ
    PAIRS = FOX_PAIRS_BWD
    NG = WG // (LANE * PAIRS)
    NHS = 2 * PAIRS
    W = LANE * PAIRS
    n = SEQ // FB

    def body(q_ref, k_ref, v_ref, do_ref, l_ref, d_ref, fc_ref, fr_ref, dqo_ref, dk_ref, dv_ref, dfq_ref, dfk_ref, dq_ref):
        j = pl.program_id(2)
        m0 = _head_lanes()
        masks = (m0, jnp.logical_not(m0))
        heads = [(hh, slice((hh // 2) * LANE, (hh // 2 + 1) * LANE), masks[hh % 2]) for hh in range(NHS)]

        @pl.when(j == 0)
        def _():
            dq_ref[...] = jnp.zeros_like(dq_ref)
            dfq_ref[...] = jnp.zeros_like(dfq_ref)

        kj = [k_ref[:, lanes] for _, lanes, _ in heads]
        vj = [v_ref[:, lanes] for _, lanes, _ in heads]
        fk = [_row_to_col(fc_ref[0, hh, 0]) for hh in range(NHS)]

        def step(t, carry, masked):
            rows = pl.ds(pl.multiple_of(t * FT, FT), FT)
            qm, dom = [], []
            for _, lanes, mh in heads:
                qt = q_ref[rows, lanes] * ATTN_SCALE
                qm.append(jnp.where(mh, qt, jnp.zeros_like(qt)))
                dom.append(jnp.where(mh, do_ref[rows, lanes], 0.0).astype(BF16))
            ss = [_dot_nt(kj[hh], qm[hh]) + fr_ref[0, hh, t] - fk[hh] for hh in range(NHS)]
            dps = [_dot_nt(vj[hh], dom[hh]) for hh in range(NHS)]
            if masked:
                key = lax.broadcasted_iota(jnp.int32, (FB, FT), 0)
                qry = lax.broadcasted_iota(jnp.int32, (FB, FT), 1)
                ok = qry + (t * FT - j * FB) >= key
                ss = [jnp.where(ok, s, NEG) for s in ss]
            pds = []
            for hh in range(NHS):
                p = jnp.exp(ss[hh] - l_ref[0, hh, t])
                ds = p * (dps[hh] - d_ref[0, hh, t])
                dfq_ref[0, hh, t] += jnp.sum(ds, axis=0, keepdims=True)
                pds.append((p.astype(BF16), ds.astype(BF16), jnp.sum(ds, axis=-1, keepdims=True)))
            dks = [_dot(pds[hh][1], qm[hh]) for hh in range(NHS)]
            dvs = [_dot(pds[hh][0], dom[hh]) for hh in range(NHS)]
            dqs = [_dot_tn(pds[hh][1], kj[hh]) for hh in range(NHS)]
            for pr in range(PAIRS):
                dq_ref[rows, pr * LANE:(pr + 1) * LANE] += _pair(m0, dqs[2 * pr], dqs[2 * pr + 1]) * ATTN_SCALE
            return tuple((carry[hh][0] + dks[hh], carry[hh][1] + dvs[hh], carry[hh][2] - pds[hh][2]) for hh in range(NHS))

        one = (jnp.zeros((FB, LANE), F32), jnp.zeros((FB, LANE), F32), jnp.zeros((FB, 1), F32))
        first = (j * FB) // FT
        carry = step(first, (one,) * NHS, True)
        carry = lax.fori_loop(first + 1, SEQ // FT, lambda t, cr: step(t, cr, False), carry)
        for pr in range(PAIRS):
            (dka, dva, dfka), (dkb, dvb, dfkb) = carry[2 * pr], carry[2 * pr + 1]
            dk_ref[:, pr * LANE:(pr + 1) * LANE] = _pair(m0, dka, dkb).astype(BF16)
            dv_ref[:, pr * LANE:(pr + 1) * LANE] = _pair(m0, dva, dvb).astype(BF16)
            dfk_ref[0, 2 * pr, 0] = _col_to_row(dfka)
            dfk_ref[0, 2 * pr + 1, 0] = _col_to_row(dfkb)

        @pl.when(j == n - 1)
        def _():
            dqo_ref[...] = dq_ref[...].astype(BF16)

    kblk = lambda grp: pl.BlockSpec((FB, W), lambda b, g, j, grp=grp: (b * n + j, grp * NG + g))
    full = pl.BlockSpec((SEQ, W), lambda b, g, j: (b, g))
    rowf = pl.BlockSpec((1, NHS, SEQ // FT, 1, FT), lambda b, g, j: (b, g, 0, 0, 0))
    rowb = pl.BlockSpec((1, NHS, 1, 1, FB), lambda b, g, j: (b, g, j, 0, 0))
    return pl.pallas_call(
        body, grid=(B, NG, n), in_specs=[full, kblk(1), kblk(2), full, rowf, rowf, rowb, rowf],
        out_specs=[full, kblk(0), kblk(0), rowf, rowb],
        out_shape=[SDS((T, WG), BF16), SDS((T, WG), BF16), SDS((T, WG), BF16), SDS((B, NH, SEQ // FT, 1, FT), F32),
                   SDS((B, NH, n, 1, FB), F32)],
        scratch_shapes=[pltpu.VMEM((SEQ, W), F32)],
        compiler_params=_cp("arbitrary", "arbitrary", "arbitrary"), name=name,
    )(pb, pb, pb, do, lrow, drow, fblk, frow)


def _tri(lower):
    r = lax.broadcasted_iota(jnp.int32, (LANE, LANE), 0)
    c = lax.broadcasted_iota(jnp.int32, (LANE, LANE), 1)
    return ((r >= c) if lower else (r <= c)).astype(F32)


def _tri_dot(t, xblk):
    return jnp.dot(t, xblk, precision=lax.Precision.HIGHEST, preferred_element_type=F32)


def forget_cumsum(flog, bias, name):
    B, S, _ = flog.shape

    def body(f_ref, b_ref, o_ref):
        t = _tri(True)
        carry = jnp.zeros((1, LANE), F32)
        for blk in range(S // LANE):
            z = f_ref[0, blk * LANE:(blk + 1) * LANE, :] + b_ref[...]
            lf = jnp.minimum(z, 0.0) - jnp.log(1.0 + jnp.exp(-jnp.abs(z)))
            cs = _tri_dot(t, lf) + carry
            o_ref[0, blk * LANE:(blk + 1) * LANE, :] = cs
            carry = cs[LANE - 1:LANE, :]

    spec = pl.BlockSpec((1, S, LANE), lambda b: (b, 0, 0))
    return pl.pallas_call(
        body, grid=(B,), in_specs=[spec, pl.BlockSpec((1, LANE), lambda b: (0, 0))], out_specs=spec,
        out_shape=SDS((B, S, LANE), F32), compiler_params=_cp("arbitrary"), name=name,
    )(flog, bias)


def forget_cumsum_bwd(dF, flog, bias, name):
    B, S, _ = flog.shape

    def body(d_ref, f_ref, b_ref, o_ref, db_ref):
        @pl.when(pl.program_id(0) == 0)
        def _():
            db_ref[...] = jnp.zeros_like(db_ref)

        t = _tri(False)
        carry = jnp.zeros((1, LANE), F32)
        tot = jnp.zeros((1, LANE), F32)
        for blk in reversed(range(S // LANE)):
            sl = slice(blk * LANE, (blk + 1) * LANE)
            rc = _tri_dot(t, d_ref[0, sl, :]) + carry
            carry = rc[0:1, :]
            z = f_ref[0, sl, :] + b_ref[...]
            dz = rc * jax.nn.sigmoid(-z)
            o_ref[0, sl, :] = dz
            tot = tot + jnp.sum(dz, axis=0, keepdims=True)
        db_ref[...] += tot

    spec = pl.BlockSpec((1, S, LANE), lambda b: (b, 0, 0))
    vec = pl.BlockSpec((1, LANE), lambda b: (0, 0))
    return pl.pallas_call(
        body, grid=(B,), in_specs=[spec, spec, vec], out_specs=[spec, vec],
        out_shape=[SDS((B, S, LANE), F32), SDS((1, LANE), F32)], compiler_params=_cp("arbitrary"), name=name,
    )(dF, flog, bias)


def mixer_out_fwd(oa, ob, goa, gob, w_out, g_post, x, mod3, name):
    T = x.shape[0]

    def body(oa_ref, ob_ref, goa_ref, gob_ref, w_ref, gp_ref, x_ref, mod_ref, xo_ref, mg_ref, y0_ref):
        a = oa_ref[...]
        b = ob_ref[...]
        mg = jnp.concatenate([a * _rms(a) * goa_ref[...], b * _rms(b) * gob_ref[...]], axis=-1).astype(BF16)
        mg_ref[...] = mg
        y0 = _dot(mg, w_ref[...])
        y0_ref[...] = y0
        xo_ref[...] = x_ref[...] + mod_ref[0, 2:3, :] * (y0 * _rms(y0) * gp_ref[...])

    tok = pl.BlockSpec((TM, D), lambda i: (i, 0))
    half = pl.BlockSpec((TM, WG), lambda i: (i, 0))
    hv = pl.BlockSpec((1, WG), lambda i: (0, 0))
    return pl.pallas_call(
        body, grid=(T // TM,),
        in_specs=[half, half, hv, hv, pl.BlockSpec((D, D), lambda i: (0, 0)), pl.BlockSpec((1, D), lambda i: (0, 0)), tok,
                  pl.BlockSpec((1, 3, D), _mod_map)],
        out_specs=[tok, tok, tok], out_shape=[SDS((T, D), F32), SDS((T, D), BF16), SDS((T, D), F32)],
        compiler_params=_cp("arbitrary"), name=name,
    )(oa, ob, goa, gob, w_out, g_post, x, mod3)


def mixer_out_bwd(dxo, y0, mod3, g_post, w_out, oa, ob, goa, gob, name):
    T = dxo.shape[0]
    nb = T // SEQ
    tiles_per_seq = SEQ // TM

    def body(dxo_ref, y0_ref, mod_ref, gp_ref, w_ref, oa_ref, ob_ref, goa_ref, gob_ref,
             dy0_ref, doa_ref, dob_ref, dmg_ref, dgp_ref, dgoa_ref, dgob_ref, dvb_ref):
        i = pl.program_id(0)

        @pl.when(i == 0)
        def _():
            dgp_ref[...] = jnp.zeros_like(dgp_ref)
            dgoa_ref[...] = jnp.zeros_like(dgoa_ref)
            dgob_ref[...] = jnp.zeros_like(dgob_ref)

        @pl.when(i % tiles_per_seq == 0)
        def _():
            dmg_ref[...] = jnp.zeros_like(dmg_ref)

        dy0, dmg, dg = _post_bwd(dxo_ref[...], y0_ref[...], gp_ref[...], mod_ref[0, 2:3, :], 1.0)
        dmg_ref[0] += dmg
        dgp_ref[...] += dg
        db = dy0.astype(BF16)
        dy0_ref[...] = db
        dm = _dot_nt(db, w_ref[...])
        for o_ref, g_ref, do_ref, dg_ref, sl in ((oa_ref, goa_ref, doa_ref, dgoa_ref, slice(0, WG)),
                                                  (ob_ref, gob_ref, dob_ref, dgob_ref, slice(WG, 2 * WG))):
            o = o_ref[...]
            r = _rms(o)
            oh = o * r
            d = dm[:, sl]
            dg_ref[...] += jnp.sum(d * oh, axis=0, keepdims=True)
            dh = d * g_ref[...]
            do = r * (dh - oh * jnp.mean(dh * oh, axis=-1, keepdims=True))
            do_ref[...] = do
        ind = (lax.broadcasted_iota(jnp.int32, (WG, LANE), 0) // HD == lax.broadcasted_iota(jnp.int32, (WG, LANE), 1)).astype(BF16)
        prod = do * o
        hi = prod.astype(BF16)
        dvb_ref[...] = _dot(hi, ind) + _dot((prod - hi.astype(F32)).astype(BF16), ind)

    tok = pl.BlockSpec((TM, D), lambda i: (i, 0))
    half = pl.BlockSpec((TM, WG), lambda i: (i, 0))
    hv = pl.BlockSpec((1, WG), lambda i: (0, 0))
    vec = pl.BlockSpec((1, D), lambda i: (0, 0))
    return pl.pallas_call(
        body, grid=(T // TM,),
        in_specs=[tok, tok, pl.BlockSpec((1, 3, D), _mod_map), vec, pl.BlockSpec((D, D), lambda i: (0, 0)), half, half, hv, hv],
        out_specs=[tok, half, half, pl.BlockSpec((1, 1, D), _mod_map), vec, hv, hv, pl.BlockSpec((TM, LANE), lambda i: (i, 0))],
        out_shape=[SDS((T, D), BF16), SDS((T, WG), F32), SDS((T, WG), F32), SDS((nb, 1, D), F32), SDS((1, D), F32),
                   SDS((1, WG), F32), SDS((1, WG), F32), SDS((T, LANE), F32)],
        compiler_params=_cp("arbitrary"), name=name,
    )(dxo, y0, mod3, g_post, w_out, oa, ob, goa, gob)


def mixer_proj_bwd(dps, dflog, dxo, x, mod3, g_pre, w_main, w_f, name):
    T = x.shape[0]
    nb = T // SEQ
    tiles_per_seq = SEQ // TM
    ngrp = len(dps)

    def body(*refs):
        dp_refs = refs[:ngrp]
        df_ref, dxo_ref, x_ref, mod_ref, g_ref, w_ref, wf_ref, dx_ref, dmod_ref, dg_ref = refs[ngrp:]
        i = pl.program_id(0)

        @pl.when(i == 0)
        def _():
            dg_ref[...] = jnp.zeros_like(dg_ref)

        @pl.when(i % tiles_per_seq == 0)
        def _():
            dmod_ref[...] = jnp.zeros_like(dmod_ref)

        dh = _dot_nt(df_ref[...].astype(BF16), wf_ref[...])
        for g, dp_ref in enumerate(dp_refs):
            dh = dh + _dot_nt(dp_ref[...], w_ref[:, g * WG:(g + 1) * WG])
        dx, dsh, dsc, dg = _norm_mod_bwd(dh, x_ref[...], g_ref[...], mod_ref[0, 1:2, :])
        dx_ref[...] = dxo_ref[...] + dx
        dmod_ref[0, 0:1, :] += dsh
        dmod_ref[0, 1:2, :] += dsc
        dg_ref[...] += dg

    tok = pl.BlockSpec((TM, D), lambda i: (i, 0))
    vec = pl.BlockSpec((1, D), lambda i: (0, 0))
    return pl.pallas_call(
        body, grid=(T // TM,),
        in_specs=[pl.BlockSpec((TM, WG), lambda i: (i, 0))] * ngrp
        + [pl.BlockSpec((TM, LANE), lambda i: (i, 0)), tok, tok, pl.BlockSpec((1, 3, D), _mod_map), vec,
           pl.BlockSpec((D, IN_MAIN), lambda i: (0, 0)), pl.BlockSpec((D, LANE), lambda i: (0, 0))],
        out_specs=[tok, pl.BlockSpec((1, 2, D), _mod_map), vec],
        out_shape=[SDS((T, D), F32), SDS((nb, 2, D), F32), SDS((1, D), F32)],
        compiler_params=_cp("arbitrary"), name=name,
    )(*dps, dflog, dxo, x, mod3, g_pre, w_main, w_f)


def ada_fwd(c_all, w, b, name):
    n = w.shape[1]
    tn = n // 2

    def body(c_ref, w_ref, b_ref, o_ref):
        cv = c_ref[...]
        o_ref[...] = _dot((cv * jax.nn.sigmoid(cv)).astype(BF16), w_ref[...].astype(BF16)) + b_ref[...]

    R = c_all.shape[0]
    return pl.pallas_call(
        body, grid=(2,),
        in_specs=[pl.BlockSpec((R, D), lambda j: (0, 0)), pl.BlockSpec((D, tn), lambda j: (0, j)), pl.BlockSpec((1, tn), lambda j: (0, j))],
        out_specs=pl.BlockSpec((R, tn), lambda j: (0, j)), out_shape=SDS((R, n), F32),
        compiler_params=_cp("arbitrary"), name=name,
    )(c_all, w, b)


def ada_bwd(c_all, dmod, name):
    R, n = dmod.shape
    tn = n // 2

    def body(c_ref, d_ref, o_ref):
        cv = c_ref[...]
        o_ref[...] = _dot_tn((cv * jax.nn.sigmoid(cv)).astype(BF16), d_ref[...].astype(BF16))

    return pl.pallas_call(
        body, grid=(2,), in_specs=[pl.BlockSpec((R, D), lambda j: (0, 0)), pl.BlockSpec((R, tn), lambda j: (0, j))],
        out_specs=pl.BlockSpec((D, tn), lambda j: (0, j)), out_shape=SDS((D, n), F32),
        compiler_params=_cp("arbitrary"), name=name,
    )(c_all, dmod)


def _adam_math(w, g, m, v):
    m2 = ADAM_B1 * m + (1.0 - ADAM_B1) * g
    v2 = ADAM_B2 * v + (1.0 - ADAM_B2) * (g * g)
    m_hat = m2 / (1.0 - ADAM_B1 ** ADAM_STEP)
    v_hat = v2 / (1.0 - ADAM_B2 ** ADAM_STEP)
    delta = -ADAM_LR * (m_hat / (jnp.sqrt(v_hat) + ADAM_EPS) + ADAM_WD * w)
    return delta, m2, v2


def adam_update(w, g, m, v, tr, name):
    _, R, C = w.shape

    def body(w_ref, g_ref, m_ref, v_ref, d_ref, mo_ref, vo_ref):
        d_ref[0], mo_ref[0], vo_ref[0] = _adam_math(w_ref[0], g_ref[...], m_ref[0], v_ref[0])

    spec = pl.BlockSpec((1, tr, C), lambda i: (0, i, 0))
    gspec = pl.BlockSpec((tr, C), lambda i: (i, 0))
    return pl.pallas_call(
        body, grid=(R // tr,), in_specs=[spec, gspec, spec, spec], out_specs=[spec] * 3, out_shape=[SDS((1, R, C), F32)] * 3,
        compiler_params=_cp("arbitrary"), name=name,
    )(w, g, m, v)


def adam_update_halves(w, mine, other, m, v, cidx, tr, name):
    _, R, C = w.shape
    nh = R // 2 // tr

    def body(c_ref, w_ref, a_ref, b_ref, m_ref, v_ref, g_ref, d_ref, mo_ref, vo_ref):
        first_half = pl.program_id(0) < nh
        g = jnp.where(first_half == (c_ref[0] == 0), a_ref[...], b_ref[...])
        g_ref[0] = g
        d_ref[0], mo_ref[0], vo_ref[0] = _adam_math(w_ref[0], g, m_ref[0], v_ref[0])

    spec = pl.BlockSpec((1, tr, C), lambda i, c_ref: (0, i, 0))
    hspec = pl.BlockSpec((tr, C), lambda i, c_ref: (i % nh, 0))
    return pl.pallas_call(
        body,
        grid_spec=pltpu.PrefetchScalarGridSpec(num_scalar_prefetch=1, grid=(R // tr,), in_specs=[spec, hspec, hspec, spec, spec],
                                               out_specs=[spec] * 4),
        out_shape=[SDS((1, R, C), F32)] * 4, compiler_params=_cp("arbitrary"), name=name,
    )(cidx, w, mine, other, m, v)


def vec_adam(parts, w, m, v, name):
    P, C = parts.shape

    def body(p_ref, w_ref, m_ref, v_ref, g_ref, d_ref, mo_ref, vo_ref):
        g = jnp.sum(p_ref[...], axis=0, keepdims=True)
        g_ref[...] = g
        d_ref[...], mo_ref[...], vo_ref[...] = _adam_math(w_ref[...], g, m_ref[...], v_ref[...])

    return pl.pallas_call(body, out_shape=[SDS((1, C), F32)] * 4, compiler_params=_cp(), name=name)(parts, w, m, v)


def small_adam(parts, layout, ws, ms, vs, name):
    P, C = parts.shape
    k = len(layout)

    def body(*refs):
        p_ref = refs[0]
        w_refs, m_refs, v_refs = refs[1:1 + k], refs[1 + k:1 + 2 * k], refs[1 + 2 * k:1 + 3 * k]
        outs = refs[1 + 3 * k:]
        g_all = jnp.sum(p_ref[...], axis=0, keepdims=True)
        outs[4 * k][...] = g_all
        for n, (off, width) in enumerate(layout):
            g = g_all[:, off:off + width]
            outs[4 * n][...] = g
            outs[4 * n + 1][...], outs[4 * n + 2][...], outs[4 * n + 3][...] = _adam_math(
                w_refs[n][...], g, m_refs[n][...], v_refs[n][...])

    shapes = [SDS((1, width), F32) for _, width in layout for _ in range(4)] + [SDS((1, C), F32)]
    res = pl.pallas_call(body, out_shape=shapes, compiler_params=_cp(), name=name)(parts, *ws, *ms, *vs)
    return [tuple(res[4 * n:4 * n + 4]) for n in range(k)], res[4 * k]


HBM = pl.BlockSpec(memory_space=pltpu.HBM)
VMEM = pl.BlockSpec(memory_space=pltpu.VMEM)


def _place():
    x, y, c = lax.axis_index("x"), lax.axis_index("y"), lax.axis_index("c")
    return x, y, c, [(1 - x, y), (x, 1 - y), (1 - x, 1 - y)]


def all_gather_chips(xs, name):
    R, C = xs.shape

    def body(x_ref, out_ref, send_sems, recv_sems, local_sem):
        x, y, c, chips = _place()
        mine = pltpu.make_async_copy(x_ref, out_ref.at[2 * x + y], local_sem)
        mine.start()
        cps = [pltpu.make_async_remote_copy(src_ref=x_ref, dst_ref=out_ref.at[2 * x + y], send_sem=send_sems.at[j],
                                            recv_sem=recv_sems.at[j], device_id=(*chip, c), device_id_type=MESH)
               for j, chip in enumerate(chips)]
        for cp in cps:
            cp.start()
        for cp in cps:
            cp.wait()
        mine.wait()

    return pl.pallas_call(
        body, out_shape=SDS((N_SHARD, R, C), xs.dtype), in_specs=[VMEM], out_specs=VMEM,
        scratch_shapes=[pltpu.SemaphoreType.DMA((3,)), pltpu.SemaphoreType.DMA((3,)), pltpu.SemaphoreType.DMA],
        compiler_params=pltpu.CompilerParams(vmem_limit_bytes=VMEM_LIMIT), name=name,
    )(xs)


def all_gather8(xs, name):
    R, C = xs.shape

    def body(x_ref, out_ref, send_sems, recv_sems, local_sem):
        x, y, c, chips = _place()
        me, sibling = (x, y, c), (x, y, 1 - c)

        def slot(px, py, pc):
            return out_ref.at[4 * px + 2 * py + pc]

        def copy(k, block, to, src=None):
            return pltpu.make_async_remote_copy(
                src_ref=slot(*block) if src is None else src, dst_ref=slot(*block),
                send_sem=send_sems.at[k], recv_sem=recv_sems.at[k], device_id=to, device_id_type=MESH)

        mine = pltpu.make_async_copy(x_ref, slot(*me), local_sem)
        mine.start()
        first = [copy(0, me, sibling, src=x_ref)]
        first += [copy(1 + j, me, (*chip, c), src=x_ref) for j, chip in enumerate(chips)]
        for cp in first:
            cp.start()
        passed = [copy(4 + j, (*chip, c), sibling) for j, chip in enumerate(chips)]
        for j, chip in enumerate(chips):
            copy(1 + j, (*chip, c), me).wait_recv()
            passed[j].start()
        copy(0, sibling, me).wait_recv()
        for j, chip in enumerate(chips):
            copy(4 + j, (*chip, 1 - c), me).wait_recv()
        for cp in first + passed:
            cp.wait_send()
        mine.wait()

    return pl.pallas_call(
        body, out_shape=SDS((N_DEV, R, C), xs.dtype), in_specs=[VMEM], out_specs=VMEM,
        scratch_shapes=[pltpu.SemaphoreType.DMA((7,)), pltpu.SemaphoreType.DMA((7,)), pltpu.SemaphoreType.DMA],
        compiler_params=pltpu.CompilerParams(vmem_limit_bytes=VMEM_LIMIT), name=name,
    )(xs)


class ShardGather:
    def __init__(self, shapes, splits):
        self.shapes, self.splits, self.n = shapes, splits, len(shapes)

    def scratch(self):
        n = self.n
        return [pltpu.SemaphoreType.DMA((n, 6)), pltpu.SemaphoreType.DMA((n, 6)), pltpu.SemaphoreType.DMA((n,))]

    def out_shapes(self, dtype):
        return [SDS((N_SHARD,) + tuple(s), dtype) for s in self.shapes]

    def _half(self, ref, k, cc):
        lo, hi = (0, self.splits[k]) if cc == 0 else (self.splits[k], self.shapes[k][0])
        return ref.at[pl.ds(lo, hi - lo)]

    def _phase(self, w_refs, o_refs, sems, finish):
        send_sems, recv_sems, local_sems = sems
        x, y, c, chips = _place()
        sibling = (x, y, 1 - c)
        me_s = 2 * x + y

        def rcopy(src, dst, k, s, to):
            return pltpu.make_async_remote_copy(src_ref=src, dst_ref=dst, send_sem=send_sems.at[k, s],
                                                recv_sem=recv_sems.at[k, s], device_id=to, device_id_type=MESH)

        for cc in (0, 1):
            @pl.when(c == cc)
            def _():
                local = [pltpu.make_async_copy(w_refs[k], o_refs[k].at[me_s], local_sems.at[k]) for k in range(self.n)]
                first = [rcopy(self._half(w_refs[k], k, cc), self._half(o_refs[k].at[me_s], k, cc), k, j, (*chip, c))
                         for k in range(self.n) for j, chip in enumerate(chips)]
                if not finish:
                    for cp in local + first:
                        cp.start()
                    return
                passed = []
                for k in range(self.n):
                    for j, chip in enumerate(chips):
                        land = self._half(o_refs[k].at[2 * chip[0] + chip[1]], k, cc)
                        rcopy(land, land, k, j, (*chip, c)).wait_recv()
                        f = rcopy(land, land, k, 3 + j, sibling)
                        f.start()
                        passed.append(f)
                for k in range(self.n):
                    for j, chip in enumerate(chips):
                        other = self._half(o_refs[k].at[2 * chip[0] + chip[1]], k, 1 - cc)
                        rcopy(other, other, k, 3 + j, sibling).wait_recv()
                for s in first + passed:
                    s.wait_send()
                for cp in local:
                    cp.wait()

    def start(self, w_refs, o_refs, sems):
        self._phase(w_refs, o_refs, sems, False)

    def finish(self, w_refs, o_refs, sems):
        self._phase(w_refs, o_refs, sems, True)


def all_gather_shards(ws, splits, name):
    n = len(ws)
    plan = ShardGather([w.shape for w in ws], splits)

    def body(*refs):
        plan.start(refs[:n], refs[n:2 * n], refs[2 * n:])
        plan.finish(refs[:n], refs[n:2 * n], refs[2 * n:])

    return pl.pallas_call(
        body, out_shape=plan.out_shapes(ws[0].dtype), in_specs=[HBM] * n, out_specs=[HBM] * n,
        scratch_shapes=plan.scratch(), name=name,
    )(*ws)


def sibling_send_half(gs, name):
    n = len(gs)

    def body(*refs):
        g_refs, o_refs = refs[:n], refs[n:2 * n]
        send_sems, recv_sems = refs[2 * n:]
        x, y, c, _ = _place()
        cps = []
        for k in range(n):
            hr = gs[k].shape[1] // 2
            src = g_refs[k].at[:, pl.ds(pl.multiple_of((1 - c) * hr, 8), hr)]
            cp = pltpu.make_async_remote_copy(src_ref=src, dst_ref=o_refs[k], send_sem=send_sems.at[k], recv_sem=recv_sems.at[k],
                                              device_id=(x, y, 1 - c), device_id_type=MESH)
            cp.start()
            cps.append(cp)
        for cp in cps:
            cp.wait()

    return pl.pallas_call(
        body, out_shape=[SDS((N_SHARD, g.shape[1] // 2, g.shape[2]), g.dtype) for g in gs], in_specs=[HBM] * n, out_specs=[HBM] * n,
        scratch_shapes=[pltpu.SemaphoreType.DMA((n,)), pltpu.SemaphoreType.DMA((n,))], name=name,
    )(*gs)


def _scatter_copies(h_refs, o_refs, send_sems, recv_sems):
    _, _, c, chips = _place()
    return [pltpu.make_async_remote_copy(
        src_ref=h_refs[k].at[2 * chip[0] + chip[1]], dst_ref=o_refs[k].at[j], send_sem=send_sems.at[k, j],
        recv_sem=recv_sems.at[k, j], device_id=(*chip, c), device_id_type=MESH)
        for k in range(len(h_refs)) for j, chip in enumerate(chips)]


def chip_scatter(hs, name):
    n = len(hs)

    def body(*refs):
        cps = _scatter_copies(refs[:n], refs[n:2 * n], *refs[2 * n:])
        for cp in cps:
            cp.start()
        for cp in cps:
            cp.wait()

    return pl.pallas_call(
        body, out_shape=[SDS((3,) + h.shape[1:], h.dtype) for h in hs], in_specs=[HBM] * n, out_specs=[HBM] * n,
        scratch_shapes=[pltpu.SemaphoreType.DMA((n, 3)), pltpu.SemaphoreType.DMA((n, 3))], name=name,
    )(*hs)


def sibling_swap(ghs, name):
    n = len(ghs)

    def body(*refs):
        g_refs, o_refs = refs[:n], refs[n:2 * n]
        send_sems, recv_sems = refs[2 * n:]
        x, y, c, _ = _place()
        cps = []
        for k in range(n):
            cp = pltpu.make_async_remote_copy(src_ref=g_refs[k], dst_ref=o_refs[k], send_sem=send_sems.at[k],
                                              recv_sem=recv_sems.at[k], device_id=(x, y, 1 - c), device_id_type=MESH)
            cp.start()
            cps.append(cp)
        for cp in cps:
            cp.wait()

    return pl.pallas_call(
        body, out_shape=[SDS(g.shape, g.dtype) for g in ghs], in_specs=[HBM] * n, out_specs=[HBM] * n,
        scratch_shapes=[pltpu.SemaphoreType.DMA((n,)), pltpu.SemaphoreType.DMA((n,))], name=name,
    )(*ghs)


def pair_sums(gs, ras, cidx, name):
    n = len(gs)
    halves = [(g.shape[1] // 2, g.shape[2]) for g in gs]

    def body(c_ref, *refs):
        for g_ref, a_ref, o_ref in zip(refs[:n], refs[n:2 * n], refs[2 * n:]):
            o_ref[...] = (g_ref[...] + a_ref[...]).astype(BF16)

    mine = [pl.BlockSpec((1, hr, cols), lambda s, c_ref: (s, c_ref[0], 0)) for hr, cols in halves]
    whole = [pl.BlockSpec((1, hr, cols), lambda s, c_ref: (s, 0, 0)) for hr, cols in halves]
    return pl.pallas_call(
        body,
        grid_spec=pltpu.PrefetchScalarGridSpec(num_scalar_prefetch=1, grid=(N_SHARD,), in_specs=mine + whole, out_specs=whole),
        out_shape=[SDS((N_SHARD, hr, cols), BF16) for hr, cols in halves], compiler_params=_cp("arbitrary"), name=name,
    )(cidx, *gs, *ras)


def chip_sums_total(hs, rbs, sidx, name):
    n = len(hs)
    halves = [h.shape[1:] for h in hs]

    def body(s_ref, *refs):
        for h_ref, r_ref, o_ref in zip(refs[:n], refs[n:2 * n], refs[2 * n:]):
            o_ref[...] = ((h_ref[0].astype(F32) + r_ref[0].astype(F32)) + r_ref[1].astype(F32)) + r_ref[2].astype(F32)

    return pl.pallas_call(
        body,
        grid_spec=pltpu.PrefetchScalarGridSpec(
            num_scalar_prefetch=1, grid=(1,),
            in_specs=[pl.BlockSpec((1, hr, cols), lambda i, s_ref: (s_ref[0], 0, 0)) for hr, cols in halves]
            + [pl.BlockSpec((3, hr, cols), lambda i, s_ref: (0, 0, 0)) for hr, cols in halves],
            out_specs=[pl.BlockSpec((hr, cols), lambda i, s_ref: (0, 0)) for hr, cols in halves]),
        out_shape=[SDS((hr, cols), F32) for hr, cols in halves], compiler_params=_cp("arbitrary"), name=name,
    )(sidx, *hs, *rbs)


def _shard_cols(g, n_valid):
    r = g.shape[0]
    return g[:, :n_valid].reshape(r, N_SHARD, n_valid // N_SHARD).transpose(1, 0, 2)


def _unshard_cols(o, pad_to):
    _, r, n = o.shape
    full = o.transpose(1, 0, 2).reshape(r, N_SHARD * n)
    return jnp.pad(full, ((0, 0), (0, pad_to - N_SHARD * n)))


def _rows_of_tiles(t):
    B, H, S = t.shape
    return t.reshape(B, H, S // FT, 1, FT)


def mixer_fwd(x1, mod3, g_pre, w_main, w_f, b_forget_pad, goa, gob, w_out, g_post, tabs, nb, gather=None):
    hmix, pa, pb, flog = mixer_proj(x1, mod3, g_pre, w_main, w_f, *tabs, name="mixer_proj")
    out_a, lse_a = band_fwd(pa, name="band_fwd")
    F = forget_cumsum(flog.reshape(nb, SEQ, LANE), b_forget_pad, name="forget_cumsum")
    Fh = F[:, :, :NH].transpose(0, 2, 1)
    fblk = Fh.reshape(nb, NH, SEQ // FB, 1, FB)
    frow = _rows_of_tiles(Fh)
    (out_b, lse_b), gathered = fox_fwd(pb, Fh.reshape(nb, NH, SEQ // FOX_QB, 1, FOX_QB), frow, name="fox_fwd", gather=gather)
    x2, merged, y0m = mixer_out_fwd(out_a, out_b, goa, gob, w_out, g_post, x1, mod3, name="mixer_out_fwd")
    res = dict(hmix=hmix, flog=flog, pa=pa, pb=pb, out_a=out_a, lse_a=lse_a, fblk=fblk, frow=frow, out_b=out_b,
               lrow=_rows_of_tiles(lse_b.reshape(nb, NH, SEQ)), merged=merged, y0m=y0m)
    return x2, res, gathered


def mixer_bwd(dx2, x1, mod3, g_pre, w_main, w_f, b_forget_pad, goa, gob, w_out, g_post, tabs, res, nb):
    T = nb * SEQ
    dy0m, doa, dob, dmgate, dg_post, dgoa, dgob, dvec_b = mixer_out_bwd(
        dx2, res["y0m"], mod3, g_post, w_out, res["out_a"], res["out_b"], goa, gob, name="mixer_out_bwd")
    dqa, dka, dva = band_bwd(res["pa"], doa, res["out_a"], res["lse_a"], *tabs, name="band_bwd")
    drow = _rows_of_tiles(dvec_b[:, :NH].reshape(nb, SEQ, NH).transpose(0, 2, 1))
    dqb, dkb, dvb, dfq, dfk = fox_bwd(res["pb"], dob, res["lrow"], drow, res["fblk"], res["frow"], name="fox_bwd")
    dF = (dfq.reshape(nb, NH, SEQ) + dfk.reshape(nb, NH, SEQ)).transpose(0, 2, 1)
    dF = jnp.pad(dF, ((0, 0), (0, 0), (0, LANE - NH)))
    dflog, dbf = forget_cumsum_bwd(dF, res["flog"].reshape(nb, SEQ, LANE), b_forget_pad, name="forget_cumsum_bwd")
    dflog = dflog.reshape(T, LANE)
    dps = (dqa, dka, dva, dqb, dkb, dvb)
    dx1, dmod2, dg_pre = mixer_proj_bwd(dps, dflog, dx2, x1, mod3, g_pre, w_main, w_f, name="mixer_proj_bwd")
    g_main = matmul_tn_cols(res["hmix"], dps, 1024, name="grad_w_in")
    g_f = matmul_tn(res["hmix"], dflog.astype(BF16), D, LANE, 1024, name="grad_w_forget")
    g_out = matmul_tn(res["merged"], dy0m, D, D, 1024, name="grad_w_out")
    dmod3 = jnp.concatenate([dmod2, dmgate], axis=1)
    return dx1, dmod3, dict(g_pre=dg_pre, g_post=dg_post, goa=dgoa, gob=dgob, b_forget=dbf[:, :NH],
                            w_in=jnp.concatenate([g_main, g_f[:, :NH]], axis=1), w_out=g_out)


def ffn_grads(h, dy0, act, dgate, dup, pre, reduce=None):
    g_gate = matmul_tn(h, dgate, D, DFF_PAD, 1024, name=pre + "_grad_gate")
    if reduce is None:
        g_up = matmul_tn(h, dup, D, DFF_PAD, 1024, name=pre + "_grad_up")
        g_down = matmul_tn(act, dy0, FF_TN, D, 1024, name=pre + "_grad_down", rows=DFF)
        return (g_gate, g_up, g_down), {}
    hs_gate = reduce("gate", g_gate)
    g_up, rb_gate = matmul_tn(h, dup, D, DFF_PAD, 1024, name=pre + "_grad_up", scatter=hs_gate)
    hs_up = reduce("up", g_up)
    g_down, rb_up = matmul_tn(act, dy0, FF_TN, D, 1024, name=pre + "_grad_down", scatter=hs_up, rows=DFF)
    return (g_gate, g_up, g_down), {"gate": (hs_gate[0], rb_gate[0]), "up": (hs_up[0], rb_up[0])}


def local_step(x0, tgt, pos_col, mod, wfull, p, late_weights=None, last_weights=None, early_grads=None, last_reduce=None):
    T = x0.shape[0]
    nb = T // SEQ
    mod_ff1, mod_mix, mod_ff2 = mod[:, 0:3], mod[:, 3:6], mod[:, 6:9]
    tabs = rope_tables(pos_col, name="rope_tables")
    bf_pad = jnp.pad(p["b_forget"], ((0, 0), (0, LANE - NH)))

    (x1, h1, gate1, up1, y01), gathered = ffn_fwd(
        x0, mod_ff1, p["g_pre_ff1"], p["g_post_ff1"], wfull["w_ff1_gate"], wfull["w_ff1_up"], wfull["w_ff1_down"], 0.5,
        name="ff1_fwd", gather=None if late_weights is None else late_weights[:2])
    if late_weights is not None:
        wfull = {**wfull, **late_weights[2](gathered)}
    x2, res, gathered = mixer_fwd(x1, mod_mix, p["g_pre_mix"], wfull["w_main"], wfull["w_f"], bf_pad, p["g_out_a"],
                                  p["g_out_b"], wfull["w_out"], p["g_post_mix"], tabs, nb,
                                  gather=None if last_weights is None else last_weights[:2])
    if last_weights is not None:
        wfull = {**wfull, **last_weights[2](gathered)}
    (x3, h2, gate2, up2, y02), _ = ffn_fwd(x2, mod_ff2, p["g_pre_ff2"], p["g_post_ff2"], wfull["w_ff2_gate"],
                                           wfull["w_ff2_up"], wfull["w_ff2_down"], 0.5, name="ff2_fwd")

    (dx2, dy02, act2, dgate2, dup2, dmod_ff2, dgpre2, dgpost2), (loss_part,) = ffn_bwd(
        x3, x2, y02, mod_ff2, p["g_pre_ff2"], p["g_post_ff2"], gate2, up2, wfull["w_ff2_gate"], wfull["w_ff2_up"],
        wfull["w_ff2_down"], 0.5, name="ff2_bwd", target=tgt)
    gw = {}
    (gw["w_ff2_gate"], gw["w_ff2_up"], gw["w_ff2_down"]), _ = ffn_grads(h2, dy02, act2, dgate2, dup2, "ff2")
    dx1, dmod_mix, gmix = mixer_bwd(dx2, x1, mod_mix, p["g_pre_mix"], wfull["w_main"], wfull["w_f"], bf_pad, p["g_out_a"],
                                    p["g_out_b"], wfull["w_out"], p["g_post_mix"], tabs, res, nb)
    gw["w_in"], gw["w_out"] = gmix["w_in"], gmix["w_out"]
    (dx0, dy01, act1, dgate1, dup1, dmod_ff1, dgpre1, dgpost1), scattered = ffn_bwd(
        dx1, x0, y01, mod_ff1, p["g_pre_ff1"], p["g_post_ff1"], gate1, up1, wfull["w_ff1_gate"], wfull["w_ff1_up"],
        wfull["w_ff1_down"], 0.5, name="ff1_bwd", scatter=None if early_grads is None else early_grads(gw))
    (gw["w_ff1_gate"], gw["w_ff1_up"], gw["w_ff1_down"]), chained = ffn_grads(h1, dy01, act1, dgate1, dup1, "ff1", last_reduce)
    dmod = jnp.concatenate([dmod_ff1, dmod_mix, dmod_ff2], axis=1).reshape(nb, 9 * D)
    small = dict(g_pre_ff1=dgpre1, g_post_ff1=dgpost1, g_pre_mix=gmix["g_pre"], g_post_mix=gmix["g_post"], g_pre_ff2=dgpre2,
                 g_post_ff2=dgpost2, g_out_a=gmix["goa"], g_out_b=gmix["gob"], b_forget=gmix["b_forget"])
    return loss_part, dx0, dmod, gw, small, scattered, chained


def kernel(x, c, positions, w_ada, b_ada, g_pre_ff1, g_post_ff1, w_ff1_gate, w_ff1_up, w_ff1_down, g_pre_mix, g_post_mix, w_in, b_forget, g_out_a, g_out_b, w_out, g_pre_ff2, g_post_ff2, w_ff2_gate, w_ff2_up, w_ff2_down, loss_target, m_w_ada, m_b_ada, m_g_pre_ff1, m_g_post_ff1, m_w_ff1_gate, m_w_ff1_up, m_w_ff1_down, m_g_pre_mix, m_g_post_mix, m_w_in, m_b_forget, m_g_out_a, m_g_out_b, m_w_out, m_g_pre_ff2, m_g_post_ff2, m_w_ff2_gate, m_w_ff2_up, m_w_ff2_down, v_w_ada, v_b_ada, v_g_pre_ff1, v_g_post_ff1, v_w_ff1_gate, v_w_ff1_up, v_w_ff1_down, v_g_pre_mix, v_g_post_mix, v_w_in, v_b_forget, v_g_out_a, v_g_out_b, v_w_out, v_g_pre_ff2, v_g_post_ff2, v_w_ff2_gate, v_w_ff2_up, v_w_ff2_down):
    args = dict(locals())
    nb = x.shape[0]
    T = nb * SEQ
    ax, ay, ac = lax.axis_index("x"), lax.axis_index("y"), lax.axis_index("c")
    shard = 2 * ax + ay
    cidx = jnp.reshape(ac, (1,)).astype(jnp.int32)
    sidx = jnp.reshape(shard, (1,)).astype(jnp.int32)

    big = ["w_ff1_gate", "w_ff1_up", "w_ff1_down", "w_in", "w_out", "w_ff2_gate", "w_ff2_up", "w_ff2_down"]
    vecs = ["g_pre_ff1", "g_post_ff1", "g_pre_mix", "g_post_mix", "g_pre_ff2", "g_post_ff2"]

    first, late = big[:3], big[3:]
    splits = {n: -(-(args[n].shape[1] // 2) // BF16_ROW_TILE) * BF16_ROW_TILE for n in big}

    def assemble(names, gathered):
        out = {}
        for n, o in zip(names, gathered):
            if n.endswith("gate") or n.endswith("up"):
                out[n] = _unshard_cols(o, DFF_PAD)
            elif n.endswith("down"):
                out[n] = jnp.pad(o.reshape(DFF, D), ((0, DFF_PAD - DFF), (0, 0)))
            elif n == "w_in":
                full = _unshard_cols(o, IN_COLS)
                out["w_main"] = full[:, :IN_MAIN]
                out["w_f"] = jnp.pad(full[:, IN_MAIN:], ((0, 0), (0, LANE - NH)))
            else:
                out[n] = o.reshape(D, D)
        return out

    wfull = assemble(first, all_gather_shards([args[n][0].astype(BF16) for n in first], [splits[n] for n in first],
                                              name="all_gather_weights"))
    def gather_plan(names):
        return ([args[n][0].astype(BF16) for n in names], [splits[n] for n in names], functools.partial(assemble, names))

    late_weights, last_weights = gather_plan(late[:2]), gather_plan(late[2:])

    ncol = w_ada.shape[2]
    c_all = all_gather8(c, name="all_gather_c").reshape(N_DEV * nb, D)
    b_loc = lax.dynamic_slice(b_ada, (0, shard * ncol), (1, ncol))
    mod_loc = ada_fwd(c_all, w_ada[0], b_loc, name="ada_fwd")
    mod_g = all_gather_chips(mod_loc, name="all_gather_mod")
    row0 = (4 * ax + 2 * ay + ac) * nb
    mod_rows = lax.dynamic_slice(mod_g, (0, row0, 0), (N_SHARD, nb, ncol))
    mod = jnp.concatenate([mod_rows[s] for s in range(N_SHARD)], axis=-1).reshape(nb, 9, D)

    small_in = dict(g_pre_ff1=g_pre_ff1, g_post_ff1=g_post_ff1, g_pre_mix=g_pre_mix, g_post_mix=g_post_mix, g_pre_ff2=g_pre_ff2,
                    g_post_ff2=g_post_ff2, g_out_a=g_out_a, g_out_b=g_out_b, b_forget=b_forget)
    def shard_blocked(n, g):
        if n.endswith("gate") or n.endswith("up"):
            return _shard_cols(g, DFF)
        if n.endswith("down"):
            return g.reshape(N_SHARD, DFF // N_SHARD, D)
        if n == "w_in":
            return _shard_cols(g, IN_COLS)
        return g.reshape(N_SHARD, D // N_SHARD, D)

    def chip_sums(names, gw, tag):
        gsb = [shard_blocked(n, gw[n]) for n in names]
        ras = sibling_send_half(gsb, name="grad_sibling_send_" + tag)
        return pair_sums(gsb, ras, cidx, name="grad_pair_sum_" + tag)

    hs = {}

    def early_grads(gw):
        hs.update(zip(late, chip_sums(late, gw, "late")))
        return [hs[n] for n in late]

    def last_reduce(which, g):
        return chip_sums(["w_ff1_" + which], {"w_ff1_" + which: g}, which)

    loss_part, dx0, dmod, gw, small, rbs_late, chained = local_step(
        x.reshape(T, D), loss_target.reshape(T, D), positions.reshape(T, 1), mod, wfull, small_in, late_weights, last_weights,
        early_grads, last_reduce)

    dmod_all = all_gather8(dmod, name="all_gather_dmod").reshape(N_DEV * nb, 9 * D)
    dmod_loc = lax.dynamic_slice(dmod_all, (0, shard * ncol), (N_DEV * nb, ncol))
    g_w_ada = ada_bwd(c_all, dmod_loc, name="ada_bwd")

    rbs = dict(zip(late, rbs_late))
    for which, (h, rb) in chained.items():
        hs["w_ff1_" + which], rbs["w_ff1_" + which] = h, rb
    hs["w_ff1_down"] = chip_sums(["w_ff1_down"], gw, "down")[0]
    rbs["w_ff1_down"] = chip_scatter([hs["w_ff1_down"]], name="grad_chip_scatter")[0]
    ghs = []
    for part, names in enumerate((big[:4], big[4:])):
        ghs += chip_sums_total([hs[n] for n in names], [rbs[n] for n in names], sidx, name=f"grad_chip_sum_{part}")
    theirs = sibling_swap(ghs, name="grad_sibling_swap")

    row6 = jnp.concatenate([small["g_out_a"], small["g_out_b"]], axis=1)
    row7 = jnp.concatenate([small["b_forget"], loss_part[0:1, 0:1], jnp.zeros((1, D - NH - 1), F32)], axis=1)
    pack = jnp.concatenate([small[n] for n in vecs] + [row6, row7], axis=0)
    packed = all_gather8(pack, name="all_gather_small").reshape(N_DEV, 8 * D)

    names = vecs + ["g_out_a", "g_out_b", "b_forget"]
    layout = [(i * D, D) for i in range(len(vecs))] + [(6 * D, WG), (6 * D + WG, WG), (7 * D, NH)]
    per_param, packed_sum = small_adam(packed, layout, [args[n] for n in names], [args["m_" + n] for n in names],
                                       [args["v_" + n] for n in names], name="adam_small")
    outs = dict(grad={}, delta={}, new_m={}, new_v={})
    for n, (g, d, m2, v2) in zip(names, per_param):
        outs["grad"][n], outs["delta"][n], outs["new_m"][n], outs["new_v"][n] = g, d, m2, v2
    loss = packed_sum[0, 7 * D + NH]
    outs["grad"]["b_ada"], outs["delta"]["b_ada"], outs["new_m"]["b_ada"], outs["new_v"]["b_ada"] = vec_adam(
        dmod_all, b_ada, m_b_ada, v_b_ada, name="adam_b_ada")

    for n, mine, other in zip(big, ghs, theirs):
        tr = 128 if mine.shape[0] % 128 == 0 else mine.shape[0]
        outs["grad"][n], outs["delta"][n], outs["new_m"][n], outs["new_v"][n] = adam_update_halves(
            args[n], mine, other, args["m_" + n], args["v_" + n], cidx, tr, name="adam_" + n)
    outs["delta"]["w_ada"], outs["new_m"]["w_ada"], outs["new_v"]["w_ada"] = adam_update(
        w_ada, g_w_ada, m_w_ada, v_w_ada, 128, name="adam_w_ada")
    outs["grad"]["w_ada"] = g_w_ada[None]

    order = ["w_ada", "b_ada", "g_pre_ff1", "g_post_ff1", "w_ff1_gate", "w_ff1_up", "w_ff1_down", "g_pre_mix", "g_post_mix", "w_in",
             "b_forget", "g_out_a", "g_out_b", "w_out", "g_pre_ff2", "g_post_ff2", "w_ff2_gate", "w_ff2_up", "w_ff2_down"]
    result = [loss, dx0.reshape(nb, SEQ, D)]
    for kind in ("grad", "delta", "new_m", "new_v"):
        result += [outs[kind][n] for n in order]
    return tuple(result)
```

```python
import functools
import math

import jax
import jax.numpy as jnp
from jax import lax
from jax.experimental import pallas as pl
from jax.experimental.pallas import tpu as pltpu

D = 1024
SEQ = 2048
HD = 64
NH = 8
WG = NH * HD
DFF = 2752
DFF_PAD = 2816
IN_MAIN = 6 * WG
IN_COLS = IN_MAIN + NH
N_SHARD = 4
N_DEV = 8
LANE = 128
BF16_ROW_TILE = 16
QB = 128
ROWS = 256
FB = 512
FT = 512
FOX_QB = 512
FOX_PAIRS = 4
FOX_PAIRS_BWD = 2
BAND_UNROLL = 8
BAND_UNROLL_BWD = 8
PATTERNS = ((1, 16), (4, 4), (16, 1))
ROPE_THETA = 500000.0
EPS = 1e-6
NEG = -1e30
ATTN_SCALE = HD ** -0.5
TM = 512
TM_FFN = 512
TM_BWD = 256
VMEM_LIMIT = 56 * 1024 * 1024

ADAM_LR, ADAM_B1, ADAM_B2, ADAM_EPS, ADAM_WD, ADAM_STEP = 0.001, 0.9, 0.999, 1e-08, 0.01, 10

F32 = jnp.float32
BF16 = jnp.bfloat16
MESH = pl.DeviceIdType.MESH
SDS = jax.ShapeDtypeStruct


def _cp(*sem):
    return pltpu.CompilerParams(dimension_semantics=sem, vmem_limit_bytes=VMEM_LIMIT)


def _dot(a, b):
    return jnp.dot(a, b, preferred_element_type=F32)


def _dot_nt(a, b):
    return lax.dot_general(a, b, (((1,), (1,)), ((), ())), preferred_element_type=F32)


def _dot_tn(a, b):
    return lax.dot_general(a, b, (((0,), (0,)), ((), ())), preferred_element_type=F32)


def _rms(xf):
    return lax.rsqrt(jnp.mean(xf * xf, axis=-1, keepdims=True) + EPS)


def _norm_mod_bwd(dh, xf, g, scale):
    r = _rms(xf)
    xh = xf * r
    dsh = jnp.sum(dh, axis=0, keepdims=True)
    dsc = jnp.sum(dh * (xh * g), axis=0, keepdims=True)
    dn = dh * (1.0 + scale)
    dg = jnp.sum(dn * xh, axis=0, keepdims=True)
    dxh = dn * g
    dx = r * (dxh - xh * jnp.mean(dxh * xh, axis=-1, keepdims=True))
    return dx, dsh, dsc, dg


def _post_bwd(dxo, y0, g, mgate, gs):
    r = _rms(y0)
    yh = y0 * r
    dmg = gs * jnp.sum(dxo * (yh * g), axis=0, keepdims=True)
    dy = (gs * mgate) * dxo
    dg = jnp.sum(dy * yh, axis=0, keepdims=True)
    dyh = dy * g
    dy0 = r * (dyh - yh * jnp.mean(dyh * yh, axis=-1, keepdims=True))
    return dy0, dmg, dg


def _mod_map(i, *_):
    return ((i * TM) // SEQ, 0, 0)


FF_TN = 1408
FF_TILES = ((0, 768), (768, 1536), (1536, 2304), (2304, 2816))


def _resident_scratch():
    return [pltpu.VMEM((D, DFF_PAD), BF16), pltpu.VMEM((D, DFF_PAD), BF16), pltpu.VMEM((DFF_PAD, D), BF16),
            pltpu.SemaphoreType.DMA((3,))]


def _load_resident(first_step, srcs, dsts, sems):
    @pl.when(first_step)
    def _():
        cps = [pltpu.make_async_copy(s, d, sems.at[k]) for k, (s, d) in enumerate(zip(srcs, dsts))]
        for cp in cps:
            cp.start()
        for cp in cps:
            cp.wait()


def ffn_fwd(x, mod3, g_pre, g_post, wg, wu, wd, gs, name, gather=None):
    T = x.shape[0]
    tm = TM_FFN
    ng = 0 if gather is None else len(gather[0])
    plan = None if gather is None else ShardGather([w.shape for w in gather[0]], gather[1])

    def body(*refs):
        x_ref, mod_ref, gpre_ref, gpost_ref = refs[:4]
        xo_ref, h_ref, gate_ref, up_ref, y0_ref = refs[7 + ng:12 + ng]
        wg_ref, wu_ref, wd_ref, wsem = refs[12 + 2 * ng:16 + 2 * ng]
        i = pl.program_id(0)
        if plan is not None:
            comm = (refs[7:7 + ng], refs[12 + ng:12 + 2 * ng], refs[16 + 2 * ng:])
            pl.when(i == 0)(lambda: plan.start(*comm))
        _load_resident(i == 0, refs[4:7], (wg_ref, wu_ref, wd_ref), wsem)

        xf = x_ref[...]
        hb = ((xf * _rms(xf) * gpre_ref[...]) * (1.0 + mod_ref[0, 1:2, :]) + mod_ref[0, 0:1, :]).astype(BF16)
        h_ref[...] = hb
        y0 = None
        for lo, hi in FF_TILES:
            gate = _dot(hb, wg_ref[:, lo:hi])
            up = _dot(hb, wu_ref[:, lo:hi])
            gate_ref[:, lo:hi] = gate.astype(BF16)
            up_ref[:, lo:hi] = up.astype(BF16)
            part = _dot((gate * jax.nn.sigmoid(gate) * up).astype(BF16), wd_ref[lo:hi, :])
            y0 = part if y0 is None else y0 + part
        y0_ref[...] = y0
        xo_ref[...] = xf + (gs * mod_ref[0, 2:3, :]) * (y0 * _rms(y0) * gpost_ref[...])

        if plan is not None:
            pl.when(i == T // tm - 1)(lambda: plan.finish(*comm))

    tok = pl.BlockSpec((tm, D), lambda i: (i, 0))
    vec = pl.BlockSpec((1, D), lambda i: (0, 0))
    hid = pl.BlockSpec((tm, DFF_PAD), lambda i: (i, 0))
    outs = pl.pallas_call(
        body, grid=(T // tm,),
        in_specs=[tok, pl.BlockSpec((1, 3, D), lambda i: ((i * tm) // SEQ, 0, 0)), vec, vec, HBM, HBM, HBM] + [HBM] * ng,
        out_specs=[tok, tok, hid, hid, tok] + [HBM] * ng,
        out_shape=[SDS((T, D), F32), SDS((T, D), BF16), SDS((T, DFF_PAD), BF16), SDS((T, DFF_PAD), BF16), SDS((T, D), F32)]
        + ([] if plan is None else plan.out_shapes(BF16)),
        scratch_shapes=_resident_scratch() + ([] if plan is None else plan.scratch()),
        compiler_params=_cp("arbitrary"), name=name,
    )(x, mod3, g_pre, g_post, wg, wu, wd, *([] if gather is None else gather[0]))
    return outs[:5], outs[5:]


def ffn_bwd(dxo, x, y0, mod3, g_pre, g_post, gate, up, wg, wu, wd, gs, name, scatter=None, target=None):
    assert scatter is None or target is None
    T = x.shape[0]
    nb = T // SEQ
    tm = TM_BWD
    tiles_per_seq = SEQ // tm
    ns = 0 if scatter is None else len(scatter)
    ne = ns + (target is not None)

    def body(*refs):
        dxo_ref, x_ref, y0_ref, mod_ref, gpre_ref, gpost_ref, gate_ref, up_ref = refs[:8]
        dx_ref, dy0_ref, act_ref, dgate_ref, dup_ref, dmod_ref, dgpre_ref, dgpost_ref = refs[11 + ne:19 + ne]
        wg_ref, wu_ref, wd_ref, wsem = refs[19 + 2 * ne:23 + 2 * ne]
        i = pl.program_id(0)
        _load_resident(i == 0, refs[8:11], (wg_ref, wu_ref, wd_ref), wsem)
        if ns:
            comm = (refs[11:11 + ns], refs[19 + ns:19 + 2 * ns], *refs[23 + 2 * ns:])

            @pl.when(i == 0)
            def _():
                for cp in _scatter_copies(*comm):
                    cp.start()

        @pl.when(i == 0)
        def _():
            dgpre_ref[...] = jnp.zeros_like(dgpre_ref)
            dgpost_ref[...] = jnp.zeros_like(dgpost_ref)

        @pl.when(i % tiles_per_seq == 0)
        def _():
            dmod_ref[...] = jnp.zeros_like(dmod_ref)

        dxo = dxo_ref[...]
        if target is not None:
            loss_ref = refs[19 + ne]

            @pl.when(i == 0)
            def _():
                loss_ref[...] = jnp.zeros_like(loss_ref)

            err = dxo - refs[11][...]
            loss_ref[...] += jnp.sum(err * err) * (0.5 / D)
            dxo = err * (1.0 / D)
        dy0, dmg, dg = _post_bwd(dxo, y0_ref[...], gpost_ref[...], mod_ref[0, 2:3, :], gs)
        dmod_ref[0, 2:3, :] += dmg
        dgpost_ref[...] += dg
        db = dy0.astype(BF16)
        dy0_ref[...] = db
        dh = None
        for lo, hi in FF_TILES:
            dact = _dot_nt(db, wd_ref[lo:hi, :])
            g = gate_ref[:, lo:hi].astype(F32)
            u = up_ref[:, lo:hi].astype(F32)
            sig = jax.nn.sigmoid(g)
            sl = g * sig
            dgate = (dact * u * (sig * (1.0 + g * (1.0 - sig)))).astype(BF16)
            dup = (dact * sl).astype(BF16)
            act_ref[:, lo:hi] = (sl * u).astype(BF16)
            dgate_ref[:, lo:hi] = dgate
            dup_ref[:, lo:hi] = dup
            part = _dot_nt(dgate, wg_ref[:, lo:hi]) + _dot_nt(dup, wu_ref[:, lo:hi])
            dh = part if dh is None else dh + part
        dx, dsh, dsc, dg = _norm_mod_bwd(dh, x_ref[...], gpre_ref[...], mod_ref[0, 1:2, :])
        dx_ref[...] = dxo + dx
        dmod_ref[0, 0:1, :] += dsh
        dmod_ref[0, 1:2, :] += dsc
        dgpre_ref[...] += dg

        if ns:
            @pl.when(i == T // tm - 1)
            def _():
                for cp in _scatter_copies(*comm):
                    cp.wait()

    tok = pl.BlockSpec((tm, D), lambda i: (i, 0))
    vec = pl.BlockSpec((1, D), lambda i: (0, 0))
    hid = pl.BlockSpec((tm, DFF_PAD), lambda i: (i, 0))
    modspec = pl.BlockSpec((1, 3, D), lambda i: ((i * tm) // SEQ, 0, 0))
    outs = pl.pallas_call(
        body, grid=(T // tm,),
        in_specs=[tok, tok, tok, modspec, vec, vec, hid, hid, HBM, HBM, HBM] + [HBM] * ns + [tok] * (ne - ns),
        out_specs=[tok, tok, hid, hid, hid, modspec, vec, vec] + [HBM] * ns
        + [pl.BlockSpec((8, LANE), lambda i: (0, 0))] * (ne - ns),
        out_shape=[SDS((T, D), F32), SDS((T, D), BF16), SDS((T, DFF_PAD), BF16), SDS((T, DFF_PAD), BF16),
                   SDS((T, DFF_PAD), BF16), SDS((nb, 3, D), F32), SDS((1, D), F32), SDS((1, D), F32)]
        + [SDS((3,) + h.shape[1:], h.dtype) for h in (scatter or [])] + [SDS((8, LANE), F32)] * (ne - ns),
        scratch_shapes=_resident_scratch()
        + ([pltpu.SemaphoreType.DMA((ns, 3)), pltpu.SemaphoreType.DMA((ns, 3))] if ns else []),
        compiler_params=_cp("arbitrary"), name=name,
    )(dxo, x, y0, mod3, g_pre, g_post, gate, up, wg, wu, wd, *(scatter or []), *([] if target is None else [target]))
    return outs[:8], outs[8:]


def matmul_tn(a, b, tm, tn, tk, name, scatter=None, rows=None):
    T, M = a.shape
    N = b.shape[1]
    grid = (M // tm, N // tn, T // tk)
    ns = 0 if scatter is None else len(scatter)

    def body(*refs):
        a_ref, b_ref = refs[:2]
        o_ref = refs[2 + ns]
        ids = [pl.program_id(ax) for ax in range(3)]
        if ns:
            comm = (refs[2:2 + ns], refs[3 + ns:3 + 2 * ns], *refs[3 + 2 * ns:])

            @pl.when((ids[0] == 0) & (ids[1] == 0) & (ids[2] == 0))
            def _():
                for cp in _scatter_copies(*comm):
                    cp.start()

        @pl.when(ids[2] == 0)
        def _():
            o_ref[...] = jnp.zeros_like(o_ref)

        o_ref[...] += _dot_tn(a_ref[...], b_ref[...])

        if ns:
            @pl.when((ids[0] == grid[0] - 1) & (ids[1] == grid[1] - 1) & (ids[2] == grid[2] - 1))
            def _():
                for cp in _scatter_copies(*comm):
                    cp.wait()

    outs = pl.pallas_call(
        body, grid=grid,
        in_specs=[pl.BlockSpec((tk, tm), lambda i, j, k: (k, i)), pl.BlockSpec((tk, tn), lambda i, j, k: (k, j))] + [HBM] * ns,
        out_specs=[pl.BlockSpec((tm, tn), lambda i, j, k: (i, j))] + [HBM] * ns,
        out_shape=[SDS((rows or M, N), F32)] + [SDS((3,) + h.shape[1:], h.dtype) for h in (scatter or [])],
        scratch_shapes=[pltpu.SemaphoreType.DMA((ns, 3)), pltpu.SemaphoreType.DMA((ns, 3))] if ns else [],
        compiler_params=_cp("arbitrary", "arbitrary", "arbitrary"), name=name,
    )(a, b, *(scatter or []))
    return outs[0] if scatter is None else (outs[0], outs[1:])


def matmul_tn_cols(a, bs, tk, name):
    T, M = a.shape
    n = bs[0].shape[1]
    ng = len(bs)

    def body(*refs):
        a_ref, b_refs, o_ref = refs[0], refs[1:1 + ng], refs[1 + ng]

        @pl.when(pl.program_id(0) == 0)
        def _():
            o_ref[...] = jnp.zeros_like(o_ref)

        av = a_ref[...]
        for g, b_ref in enumerate(b_refs):
            o_ref[:, g * n:(g + 1) * n] += _dot_tn(av, b_ref[...])

    return pl.pallas_call(
        body, grid=(T // tk,),
        in_specs=[pl.BlockSpec((tk, M), lambda k: (k, 0))] + [pl.BlockSpec((tk, n), lambda k: (k, 0))] * ng,
        out_specs=pl.BlockSpec((M, ng * n), lambda k: (0, 0)), out_shape=SDS((M, ng * n), F32),
        compiler_params=_cp("arbitrary"), name=name,
    )(a, *bs)


def rope_tables(pos_col, name):
    T = pos_col.shape[0]
    tm = 1024

    def body(p_ref, c_ref, s1_ref, s2_ref):
        lane = lax.broadcasted_iota(jnp.int32, (1, LANE), 1)
        l64 = lane % HD
        inv_freq = jnp.exp((l64 % 8).astype(F32) * (-math.log(ROPE_THETA) / 8.0))
        ang = p_ref[...].astype(F32) * inv_freq
        cs = jnp.cos(ang)
        sn = jnp.sin(ang)
        c_ref[...] = jnp.where(l64 < 16, cs, 1.0)
        s1_ref[...] = jnp.where(l64 < 8, -sn, 0.0)
        s2_ref[...] = jnp.where((l64 >= 8) & (l64 < 16), sn, 0.0)

    tab = pl.BlockSpec((tm, LANE), lambda i: (i, 0))
    return pl.pallas_call(
        body, grid=(T // tm,), in_specs=[pl.BlockSpec((tm, 1), lambda i: (i, 0))], out_specs=[tab, tab, tab],
        out_shape=[SDS((T, LANE), F32)] * 3, compiler_params=_cp("arbitrary"), name=name,
    )(pos_col)


def mixer_proj(x, mod3, g_pre, w_main, w_f, rc, rs1, rs2, name):
    T = x.shape[0]

    def body(x_ref, mod_ref, g_ref, w_ref, wf_ref, c_ref, s1_ref, s2_ref, h_ref, pa_ref, pb_ref, f_ref):
        xf = x_ref[...]
        h = (xf * _rms(xf) * g_ref[...]) * (1.0 + mod_ref[0, 1:2, :]) + mod_ref[0, 0:1, :]
        hb = h.astype(BF16)
        h_ref[...] = hb
        f_ref[...] = _dot(hb, wf_ref[...])
        c, s1, s2 = c_ref[...], s1_ref[...], s2_ref[...]
        for grp in range(2):
            pr = _dot(hb, w_ref[:, grp * WG:(grp + 1) * WG])
            for k in range(WG // LANE):
                t = pr[:, k * LANE:(k + 1) * LANE]
                pa_ref[:, grp * WG + k * LANE:grp * WG + (k + 1) * LANE] = (
                    t * c + pltpu.roll(t, LANE - 8, 1) * s1 + pltpu.roll(t, 8, 1) * s2)
        pa_ref[:, 2 * WG:3 * WG] = _dot(hb, w_ref[:, 2 * WG:3 * WG])
        for grp in range(3):
            pb_ref[:, grp * WG:(grp + 1) * WG] = _dot(hb, w_ref[:, (3 + grp) * WG:(4 + grp) * WG]).astype(BF16)

    tok = pl.BlockSpec((TM, D), lambda i: (i, 0))
    vec = pl.BlockSpec((1, D), lambda i: (0, 0))
    tab = pl.BlockSpec((TM, LANE), lambda i: (i, 0))
    grp3 = pl.BlockSpec((TM, 3 * WG), lambda i: (i, 0))
    return pl.pallas_call(
        body, grid=(T // TM,),
        in_specs=[tok, pl.BlockSpec((1, 3, D), _mod_map), vec, pl.BlockSpec((D, IN_MAIN), lambda i: (0, 0)),
                  pl.BlockSpec((D, LANE), lambda i: (0, 0)), tab, tab, tab],
        out_specs=[tok, grp3, grp3, tab],
        out_shape=[SDS((T, D), BF16), SDS((T, 3 * WG), F32), SDS((T, 3 * WG), BF16), SDS((T, LANE), F32)],
        compiler_params=_cp("arbitrary"), name=name,
    )(x, mod3, g_pre, w_main, w_f, rc, rs1, rs2)


def _head_lanes():
    return lax.broadcasted_iota(jnp.int32, (1, LANE), 1) < HD


def _pair(m0, a, b):
    return jnp.where(m0, a, b)


def _band_rows(i, d, nbc):
    if nbc == 1:
        return i, i, 0
    r, mb = i // nbc, i % nbc
    return r + mb * (QB * d), r + jnp.maximum(mb - 1, 0) * (QB * d), jnp.where(mb > 0, QB, 0)


def _rows(start, size, d):
    return pl.ds(pl.multiple_of(start, QB), size) if d == 1 else pl.ds(start, size, stride=d)


def _band_valid(span, off):
    rq = lax.broadcasted_iota(jnp.int32, (QB, span), 0)
    rel = lax.broadcasted_iota(jnp.int32, (QB, span), 1) - off
    return (rel <= rq) & (rel >= rq - QB)


def band_fwd(pa, name):
    T = pa.shape[0]
    B = T // SEQ
    NP = WG // LANE

    def body(q_ref, k_ref, v_ref, out_ref, lse_ref, o_s, l_s):
        m0 = _head_lanes()
        for pidx, (d, nbc) in enumerate(PATTERNS):
            span = QB if nbc == 1 else 2 * QB

            def blk(it, carry, pidx=pidx, d=d, nbc=nbc, span=span):
                ld = []
                for u in range(BAND_UNROLL):
                    qs, ks, off = _band_rows(it * BAND_UNROLL + u, d, nbc)
                    q = q_ref[_rows(qs, QB, d), :] * ATTN_SCALE
                    ld.append((qs, q, k_ref[_rows(ks, span, d), :].astype(BF16), v_ref[_rows(ks, span, d), :].astype(BF16),
                               _band_valid(span, off)))
                ss = [[jnp.where(valid, _dot_nt(jnp.where(mh, q, 0.0).astype(BF16), k), NEG) for mh in (m0, jnp.logical_not(m0))]
                      for _, q, k, _, valid in ld]
                ps = []
                for pair in ss:
                    row = []
                    for s in pair:
                        m = jnp.max(s, axis=-1, keepdims=True)
                        p = jnp.exp(s - m)
                        row.append((p.astype(BF16), jnp.sum(p, axis=-1, keepdims=True), m))
                    ps.append(row)
                pv = [[_dot(p, ld[u][3]) for p, _, _ in ps[u]] for u in range(BAND_UNROLL)]
                for u in range(BAND_UNROLL):
                    rows = _rows(ld[u][0], QB, d)
                    (_, l0, mx0), (_, l1, mx1) = ps[u]
                    o_s[pidx, rows, :] = _pair(m0, pv[u][0] / l0, pv[u][1] / l1)
                    l_s[pidx, rows, :] = _pair(m0, mx0 + jnp.log(l0), mx1 + jnp.log(l1))
                return carry

            lax.fori_loop(0, SEQ // QB // BAND_UNROLL, blk, 0)
        for c in range(SEQ // ROWS):
            sl = slice(c * ROWS, (c + 1) * ROWS)
            a, b, e = l_s[0, sl, :], l_s[1, sl, :], l_s[2, sl, :]
            m = jnp.maximum(jnp.maximum(a, b), e)
            L = m + jnp.log(jnp.exp(a - m) + jnp.exp(b - m) + jnp.exp(e - m))
            out_ref[sl, :] = jnp.exp(a - L) * o_s[0, sl, :] + jnp.exp(b - L) * o_s[1, sl, :] + jnp.exp(e - L) * o_s[2, sl, :]
            lse_ref[sl, :] = L

    blk_of = lambda g: pl.BlockSpec((SEQ, LANE), lambda b, hp, g=g: (b, g * NP + hp))
    return pl.pallas_call(
        body, grid=(B, NP), in_specs=[blk_of(0), blk_of(1), blk_of(2)], out_specs=[blk_of(0), blk_of(0)],
        out_shape=[SDS((T, WG), F32), SDS((T, WG), F32)],
        scratch_shapes=[pltpu.VMEM((3, SEQ, LANE), F32), pltpu.VMEM((3, SEQ, LANE), F32)],
        compiler_params=_cp("arbitrary", "arbitrary"), name=name,
    )(pa, pa, pa)


def _pair_rowsum(m0, prod):
    s0 = jnp.sum(jnp.where(m0, prod, 0.0), axis=-1, keepdims=True)
    return _pair(m0, s0, jnp.sum(prod, axis=-1, keepdims=True) - s0)


def band_bwd(pa, do, out, lse, rc, rs1, rs2, name):
    T = pa.shape[0]
    B = T // SEQ
    NP = WG // LANE

    def body(q_ref, k_ref, v_ref, do_ref, out_ref, l_ref, c_ref, s1_ref, s2_ref, dqo_ref, dko_ref, dvo_ref, d_s, dq_ref, dk_ref,
             dv_ref):
        m0 = _head_lanes()
        dq_ref[...] = jnp.zeros_like(dq_ref)
        dk_ref[...] = jnp.zeros_like(dk_ref)
        dv_ref[...] = jnp.zeros_like(dv_ref)
        for c in range(SEQ // ROWS):
            sl = slice(c * ROWS, (c + 1) * ROWS)
            d_s[sl, :] = _pair_rowsum(m0, do_ref[sl, :] * out_ref[sl, :])
        for d, nbc in PATTERNS:
            span = QB if nbc == 1 else 2 * QB

            def blk(it, carry, d=d, nbc=nbc, span=span):
                masks = (m0, jnp.logical_not(m0))
                ld = []
                for u in range(BAND_UNROLL_BWD):
                    qs, ks, off = _band_rows(it * BAND_UNROLL_BWD + u, d, nbc)
                    qrow, krow = _rows(qs, QB, d), _rows(ks, span, d)
                    ld.append(dict(qrow=qrow, krow=krow, q=q_ref[qrow, :] * ATTN_SCALE, k=k_ref[krow, :].astype(BF16),
                                   v=v_ref[krow, :].astype(BF16), do=do_ref[qrow, :], l=l_ref[qrow, :], dv=d_s[qrow, :],
                                   valid=_band_valid(span, off)))
                for t in ld:
                    t["qm"] = [jnp.where(mh, t["q"], 0.0).astype(BF16) for mh in masks]
                    t["dom"] = [jnp.where(mh, t["do"], 0.0).astype(BF16) for mh in masks]
                sd = [[(jnp.where(t["valid"], _dot_nt(t["qm"][h], t["k"]), NEG), _dot_nt(t["dom"][h], t["v"])) for h in range(2)]
                      for t in ld]
                pd = []
                for t, pair in zip(ld, sd):
                    row = []
                    for h, (s, dp) in enumerate(pair):
                        col = slice(h * HD, h * HD + 1)
                        p = jnp.exp(s - t["l"][:, col])
                        row.append((p.astype(BF16), (p * (dp - t["dv"][:, col])).astype(BF16)))
                    pd.append(row)
                gr = [(_dot(row[0][1], t["k"]), _dot(row[1][1], t["k"]),
                       _dot_tn(jnp.concatenate([row[0][1], row[1][1]], axis=0), jnp.concatenate(t["qm"], axis=0)),
                       _dot_tn(jnp.concatenate([row[0][0], row[1][0]], axis=0), jnp.concatenate(t["dom"], axis=0)))
                      for t, row in zip(ld, pd)]
                for t, (dq0, dq1, dk, dv) in zip(ld, gr):
                    dq_ref[t["qrow"], :] += _pair(m0, dq0, dq1) * ATTN_SCALE
                    dk_ref[t["krow"], :] += dk
                    dv_ref[t["krow"], :] += dv
                return carry

            lax.fori_loop(0, SEQ // QB // BAND_UNROLL_BWD, blk, 0)
        for c in range(SEQ // ROWS):
            sl = slice(c * ROWS, (c + 1) * ROWS)
            cc, s1, s2 = c_ref[sl, :], s1_ref[sl, :], s2_ref[sl, :]
            for acc, o_ref in ((dq_ref, dqo_ref), (dk_ref, dko_ref)):
                d = acc[sl, :]
                o_ref[sl, :] = (d * cc + pltpu.roll(d * s1, 8, 1) + pltpu.roll(d * s2, LANE - 8, 1)).astype(BF16)
            dvo_ref[sl, :] = dv_ref[sl, :].astype(BF16)

    blk_of = lambda g: pl.BlockSpec((SEQ, LANE), lambda b, hp, g=g: (b, g * NP + hp))
    tab = pl.BlockSpec((SEQ, LANE), lambda b, hp: (b, 0))
    return pl.pallas_call(
        body, grid=(B, NP), in_specs=[blk_of(0), blk_of(1), blk_of(2), blk_of(0), blk_of(0), blk_of(0), tab, tab, tab],
        out_specs=[blk_of(0)] * 3, out_shape=[SDS((T, WG), BF16)] * 3,
        scratch_shapes=[pltpu.VMEM((SEQ, LANE), F32)] * 4,
        compiler_params=_cp("arbitrary", "arbitrary"), name=name,
    )(pa, pa, pa, do, out, lse, rc, rs1, rs2)


def _tile_causal(nq, nk, q0, k0):
    r = lax.broadcasted_iota(jnp.int32, (nq, nk), 0)
    c = lax.broadcasted_iota(jnp.int32, (nq, nk), 1)
    return r + (q0 - k0) >= c


def _row_to_col(row):
    n = row.shape[1]
    return jnp.transpose(jnp.broadcast_to(row, (LANE, n)))[:, 0:1]


def _col_to_row(col):
    n = col.shape[0]
    return jnp.transpose(jnp.broadcast_to(col, (n, LANE)))[0:1, :]


def fox_fwd(pb, fblk, frow, name, gather=None):
    FQ = FOX_QB
    T = pb.shape[0]
    B = T // SEQ
    NG = WG // (LANE * FOX_PAIRS)
    NHS = 2 * FOX_PAIRS
    W = LANE * FOX_PAIRS
    n = SEQ // FQ
    ng = 0 if gather is None else len(gather[0])
    plan = None if gather is None else ShardGather([w.shape for w in gather[0]], gather[1])

    def body(*refs):
        q_ref, k_ref, v_ref, fc_ref, fr_ref = refs[:5]
        o_ref, lse_ref = refs[5 + ng:7 + ng]
        if plan is not None:
            comm = (refs[5:5 + ng], refs[7 + ng:7 + 2 * ng], refs[7 + 2 * ng:])
            ids = [pl.program_id(ax) for ax in range(3)]
            pl.when((ids[0] == 0) & (ids[1] == 0) & (ids[2] == 0))(lambda: plan.start(*comm))
        i = pl.program_id(2)
        m0 = _head_lanes()
        masks = (m0, jnp.logical_not(m0))
        heads = [(hh, slice((hh // 2) * LANE, (hh // 2 + 1) * LANE), masks[hh % 2]) for hh in range(NHS)]
        qh, fq = [], []
        for hh, lanes, mh in heads:
            q = q_ref[:, lanes] * ATTN_SCALE
            qh.append(jnp.where(mh, q, jnp.zeros_like(q)))
            fq.append(_row_to_col(fc_ref[0, hh, 0]))

        def step(t, carry, masked):
            rows = pl.ds(pl.multiple_of(t * FT, FT), FT)
            ss = [_dot_nt(qh[hh], k_ref[rows, lanes]) + fq[hh] - fr_ref[0, hh, t] for hh, lanes, _ in heads]
            if masked:
                ok = _tile_causal(FQ, FT, i * FQ, t * FT)
                ss = [jnp.where(ok, s, NEG) for s in ss]
            st = []
            for hh, _, _ in heads:
                m2 = jnp.maximum(carry[hh][0], jnp.max(ss[hh], axis=-1, keepdims=True))
                st.append((m2, jnp.exp(carry[hh][0] - m2), jnp.exp(ss[hh] - m2).astype(BF16)))
            pv = []
            for hh, lanes, mh in heads:
                vt = v_ref[rows, lanes]
                pv.append(_dot(st[hh][2], jnp.where(mh, vt, jnp.ones_like(vt))))
            return tuple((st[hh][0], st[hh][1] * carry[hh][1] + pv[hh]) for hh in range(NHS))

        one = (jnp.full((FQ, 1), NEG, F32), jnp.zeros((FQ, LANE), F32))
        last = (i * FQ) // FT
        carry = lax.fori_loop(0, last, lambda t, cr: step(t, cr, False), (one,) * NHS)
        carry = step(last, carry, True)
        for pr in range(FOX_PAIRS):
            (ma, acca), (mb, accb) = carry[2 * pr], carry[2 * pr + 1]
            la, lb = acca[:, HD:HD + 1], accb[:, 0:1]
            o_ref[:, pr * LANE:(pr + 1) * LANE] = _pair(m0, acca / la, accb / lb)
            lse_ref[0, 2 * pr, 0] = _col_to_row(ma + jnp.log(la))
            lse_ref[0, 2 * pr + 1, 0] = _col_to_row(mb + jnp.log(lb))
        if plan is not None:
            pl.when((ids[0] == B - 1) & (ids[1] == NG - 1) & (ids[2] == n - 1))(lambda: plan.finish(*comm))

    qblk = pl.BlockSpec((FQ, W), lambda b, g, i: (b * n + i, g))
    full = lambda grp: pl.BlockSpec((SEQ, W), lambda b, g, i, grp=grp: (b, grp * NG + g))
    rowb = pl.BlockSpec((1, NHS, 1, 1, FQ), lambda b, g, i: (b, g, i, 0, 0))
    outs = pl.pallas_call(
        body, grid=(B, NG, n),
        in_specs=[qblk, full(1), full(2), rowb, pl.BlockSpec((1, NHS, SEQ // FT, 1, FT), lambda b, g, i: (b, g, 0, 0, 0))]
        + [HBM] * ng,
        out_specs=[qblk, rowb] + [HBM] * ng,
        out_shape=[SDS((T, WG), F32), SDS((B, NH, n, 1, FQ), F32)] + ([] if plan is None else plan.out_shapes(BF16)),
        scratch_shapes=[] if plan is None else plan.scratch(),
        compiler_params=_cp("arbitrary", "arbitrary", "arbitrary"), name=name,
    )(pb, pb, pb, fblk, frow, *([] if gather is None else gather[0]))
    return outs[:2], outs[2:]


def fox_bwd(pb, do, lrow, drow, fblk, frow, name):
    T = pb.shape[0]
    B = T // SEQ
    PAIRS = FOX_PAIRS_BWD
    NG = WG // (LANE * PAIRS)
    NHS = 2 * PAIRS
    W = LANE * PAIRS
    n = SEQ // FB

    def body(q_ref, k_ref, v_ref, do_ref, l_ref, d_ref, fc_ref, fr_ref, dqo_ref, dk_ref, dv_ref, dfq_ref, dfk_ref, dq_ref):
        j = pl.program_id(2)
        m0 = _head_lanes()
        masks = (m0, jnp.logical_not(m0))
        heads = [(hh, slice((hh // 2) * LANE, (hh // 2 + 1) * LANE), masks[hh % 2]) for hh in range(NHS)]

        @pl.when(j == 0)
        def _():
            dq_ref[...] = jnp.zeros_like(dq_ref)
            dfq_ref[...] = jnp.zeros_like(dfq_ref)

        kj = [k_ref[:, lanes] for _, lanes, _ in heads]
        vj = [v_ref[:, lanes] for _, lanes, _ in heads]
        fk = [_row_to_col(fc_ref[0, hh, 0]) for hh in range(NHS)]

        def step(t, carry, masked):
            rows = pl.ds(pl.multiple_of(t * FT, FT), FT)
            qm, dom = [], []
            for _, lanes, mh in heads:
                qt = q_ref[rows, lanes] * ATTN_SCALE
                qm.append(jnp.where(mh, qt, jnp.zeros_like(qt)))
                dom.append(jnp.where(mh, do_ref[rows, lanes], 0.0).astype(BF16))
            ss = [_dot_nt(kj[hh], qm[hh]) + fr_ref[0, hh, t] - fk[hh] for hh in range(NHS)]
            dps = [_dot_nt(vj[hh], dom[hh]) for hh in range(NHS)]
            if masked:
                key = lax.broadcasted_iota(jnp.int32, (FB, FT), 0)
                qry = lax.broadcasted_iota(jnp.int32, (FB, FT), 1)
                ok = qry + (t * FT - j * FB) >= key
                ss = [jnp.where(ok, s, NEG) for s in ss]
            pds = []
            for hh in range(NHS):
                p = jnp.exp(ss[hh] - l_ref[0, hh, t])
                ds = p * (dps[hh] - d_ref[0, hh, t])
                dfq_ref[0, hh, t] += jnp.sum(ds, axis=0, keepdims=True)
                pds.append((p.astype(BF16), ds.astype(BF16), jnp.sum(ds, axis=-1, keepdims=True)))
            dks = [_dot(pds[hh][1], qm[hh]) for hh in range(NHS)]
            dvs = [_dot(pds[hh][0], dom[hh]) for hh in range(NHS)]
            dqs = [_dot_tn(pds[hh][1], kj[hh]) for hh in range(NHS)]
            for pr in range(PAIRS):
                dq_ref[rows, pr * LANE:(pr + 1) * LANE] += _pair(m0, dqs[2 * pr], dqs[2 * pr + 1]) * ATTN_SCALE
            return tuple((carry[hh][0] + dks[hh], carry[hh][1] + dvs[hh], carry[hh][2] - pds[hh][2]) for hh in range(NHS))

        one = (jnp.zeros((FB, LANE), F32), jnp.zeros((FB, LANE), F32), jnp.zeros((FB, 1), F32))
        first = (j * FB) // FT
        carry = step(first, (one,) * NHS, True)
        carry = lax.fori_loop(first + 1, SEQ // FT, lambda t, cr: step(t, cr, False), carry)
        for pr in range(PAIRS):
            (dka, dva, dfka), (dkb, dvb, dfkb) = carry[2 * pr], carry[2 * pr + 1]
            dk_ref[:, pr * LANE:(pr + 1) * LANE] = _pair(m0, dka, dkb).astype(BF16)
            dv_ref[:, pr * LANE:(pr + 1) * LANE] = _pair(m0, dva, dvb).astype(BF16)
            dfk_ref[0, 2 * pr, 0] = _col_to_row(dfka)
            dfk_ref[0, 2 * pr + 1, 0] = _col_to_row(dfkb)

        @pl.when(j == n - 1)
        def _():
            dqo_ref[...] = dq_ref[...].astype(BF16)

    kblk = lambda grp: pl.BlockSpec((FB, W), lambda b, g, j, grp=grp: (b * n + j, grp * NG + g))
    full = pl.BlockSpec((SEQ, W), lambda b, g, j: (b, g))
    rowf = pl.BlockSpec((1, NHS, SEQ // FT, 1, FT), lambda b, g, j: (b, g, 0, 0, 0))
    rowb = pl.BlockSpec((1, NHS, 1, 1, FB), lambda b, g, j: (b, g, j, 0, 0))
    return pl.pallas_call(
        body, grid=(B, NG, n), in_specs=[full, kblk(1), kblk(2), full, rowf, rowf, rowb, rowf],
        out_specs=[full, kblk(0), kblk(0), rowf, rowb],
        out_shape=[SDS((T, WG), BF16), SDS((T, WG), BF16), SDS((T, WG), BF16), SDS((B, NH, SEQ // FT, 1, FT), F32),
                   SDS((B, NH, n, 1, FB), F32)],
        scratch_shapes=[pltpu.VMEM((SEQ, W), F32)],
        compiler_params=_cp("arbitrary", "arbitrary", "arbitrary"), name=name,
    )(pb, pb, pb, do, lrow, drow, fblk, frow)


def _tri(lower):
    r = lax.broadcasted_iota(jnp.int32, (LANE, LANE), 0)
    c = lax.broadcasted_iota(jnp.int32, (LANE, LANE), 1)
    return ((r >= c) if lower else (r <= c)).astype(F32)


def _tri_dot(t, xblk):
    return jnp.dot(t, xblk, precision=lax.Precision.HIGHEST, preferred_element_type=F32)


def forget_cumsum(flog, bias, name):
    B, S, _ = flog.shape

    def body(f_ref, b_ref, o_ref):
        t = _tri(True)
        carry = jnp.zeros((1, LANE), F32)
        for blk in range(S // LANE):
            z = f_ref[0, blk * LANE:(blk + 1) * LANE, :] + b_ref[...]
            lf = jnp.minimum(z, 0.0) - jnp.log(1.0 + jnp.exp(-jnp.abs(z)))
            cs = _tri_dot(t, lf) + carry
            o_ref[0, blk * LANE:(blk + 1) * LANE, :] = cs
            carry = cs[LANE - 1:LANE, :]

    spec = pl.BlockSpec((1, S, LANE), lambda b: (b, 0, 0))
    return pl.pallas_call(
        body, grid=(B,), in_specs=[spec, pl.BlockSpec((1, LANE), lambda b: (0, 0))], out_specs=spec,
        out_shape=SDS((B, S, LANE), F32), compiler_params=_cp("arbitrary"), name=name,
    )(flog, bias)


def forget_cumsum_bwd(dF, flog, bias, name):
    B, S, _ = flog.shape

    def body(d_ref, f_ref, b_ref, o_ref, db_ref):
        @pl.when(pl.program_id(0) == 0)
        def _():
            db_ref[...] = jnp.zeros_like(db_ref)

        t = _tri(False)
        carry = jnp.zeros((1, LANE), F32)
        tot = jnp.zeros((1, LANE), F32)
        for blk in reversed(range(S // LANE)):
            sl = slice(blk * LANE, (blk + 1) * LANE)
            rc = _tri_dot(t, d_ref[0, sl, :]) + carry
            carry = rc[0:1, :]
            z = f_ref[0, sl, :] + b_ref[...]
            dz = rc * jax.nn.sigmoid(-z)
            o_ref[0, sl, :] = dz
            tot = tot + jnp.sum(dz, axis=0, keepdims=True)
        db_ref[...] += tot

    spec = pl.BlockSpec((1, S, LANE), lambda b: (b, 0, 0))
    vec = pl.BlockSpec((1, LANE), lambda b: (0, 0))
    return pl.pallas_call(
        body, grid=(B,), in_specs=[spec, spec, vec], out_specs=[spec, vec],
        out_shape=[SDS((B, S, LANE), F32), SDS((1, LANE), F32)], compiler_params=_cp("arbitrary"), name=name,
    )(dF, flog, bias)


def mixer_out_fwd(oa, ob, goa, gob, w_out, g_post, x, mod3, name):
    T = x.shape[0]

    def body(oa_ref, ob_ref, goa_ref, gob_ref, w_ref, gp_ref, x_ref, mod_ref, xo_ref, mg_ref, y0_ref):
        a = oa_ref[...]
        b = ob_ref[...]
        mg = jnp.concatenate([a * _rms(a) * goa_ref[...], b * _rms(b) * gob_ref[...]], axis=-1).astype(BF16)
        mg_ref[...] = mg
        y0 = _dot(mg, w_ref[...])
        y0_ref[...] = y0
        xo_ref[...] = x_ref[...] + mod_ref[0, 2:3, :] * (y0 * _rms(y0) * gp_ref[...])

    tok = pl.BlockSpec((TM, D), lambda i: (i, 0))
    half = pl.BlockSpec((TM, WG), lambda i: (i, 0))
    hv = pl.BlockSpec((1, WG), lambda i: (0, 0))
    return pl.pallas_call(
        body, grid=(T // TM,),
        in_specs=[half, half, hv, hv, pl.BlockSpec((D, D), lambda i: (0, 0)), pl.BlockSpec((1, D), lambda i: (0, 0)), tok,
                  pl.BlockSpec((1, 3, D), _mod_map)],
        out_specs=[tok, tok, tok], out_shape=[SDS((T, D), F32), SDS((T, D), BF16), SDS((T, D), F32)],
        compiler_params=_cp("arbitrary"), name=name,
    )(oa, ob, goa, gob, w_out, g_post, x, mod3)


def mixer_out_bwd(dxo, y0, mod3, g_post, w_out, oa, ob, goa, gob, name):
    T = dxo.shape[0]
    nb = T // SEQ
    tiles_per_seq = SEQ // TM

    def body(dxo_ref, y0_ref, mod_ref, gp_ref, w_ref, oa_ref, ob_ref, goa_ref, gob_ref,
             dy0_ref, doa_ref, dob_ref, dmg_ref, dgp_ref, dgoa_ref, dgob_ref, dvb_ref):
        i = pl.program_id(0)

        @pl.when(i == 0)
        def _():
            dgp_ref[...] = jnp.zeros_like(dgp_ref)
            dgoa_ref[...] = jnp.zeros_like(dgoa_ref)
            dgob_ref[...] = jnp.zeros_like(dgob_ref)

        @pl.when(i % tiles_per_seq == 0)
        def _():
            dmg_ref[...] = jnp.zeros_like(dmg_ref)

        dy0, dmg, dg = _post_bwd(dxo_ref[...], y0_ref[...], gp_ref[...], mod_ref[0, 2:3, :], 1.0)
        dmg_ref[0] += dmg
        dgp_ref[...] += dg
        db = dy0.astype(BF16)
        dy0_ref[...] = db
        dm = _dot_nt(db, w_ref[...])
        for o_ref, g_ref, do_ref, dg_ref, sl in ((oa_ref, goa_ref, doa_ref, dgoa_ref, slice(0, WG)),
                                                  (ob_ref, gob_ref, dob_ref, dgob_ref, slice(WG, 2 * WG))):
            o = o_ref[...]
            r = _rms(o)
            oh = o * r
            d = dm[:, sl]
            dg_ref[...] += jnp.sum(d * oh, axis=0, keepdims=True)
            dh = d * g_ref[...]
            do = r * (dh - oh * jnp.mean(dh * oh, axis=-1, keepdims=True))
            do_ref[...] = do
        ind = (lax.broadcasted_iota(jnp.int32, (WG, LANE), 0) // HD == lax.broadcasted_iota(jnp.int32, (WG, LANE), 1)).astype(BF16)
        prod = do * o
        hi = prod.astype(BF16)
        dvb_ref[...] = _dot(hi, ind) + _dot((prod - hi.astype(F32)).astype(BF16), ind)

    tok = pl.BlockSpec((TM, D), lambda i: (i, 0))
    half = pl.BlockSpec((TM, WG), lambda i: (i, 0))
    hv = pl.BlockSpec((1, WG), lambda i: (0, 0))
    vec = pl.BlockSpec((1, D), lambda i: (0, 0))
    return pl.pallas_call(
        body, grid=(T // TM,),
        in_specs=[tok, tok, pl.BlockSpec((1, 3, D), _mod_map), vec, pl.BlockSpec((D, D), lambda i: (0, 0)), half, half, hv, hv],
        out_specs=[tok, half, half, pl.BlockSpec((1, 1, D), _mod_map), vec, hv, hv, pl.BlockSpec((TM, LANE), lambda i: (i, 0))],
        out_shape=[SDS((T, D), BF16), SDS((T, WG), F32), SDS((T, WG), F32), SDS((nb, 1, D), F32), SDS((1, D), F32),
                   SDS((1, WG), F32), SDS((1, WG), F32), SDS((T, LANE), F32)],
        compiler_params=_cp("arbitrary"), name=name,
    )(dxo, y0, mod3, g_post, w_out, oa, ob, goa, gob)


def mixer_proj_bwd(dps, dflog, dxo, x, mod3, g_pre, w_main, w_f, name):
    T = x.shape[0]
    nb = T // SEQ
    tiles_per_seq = SEQ // TM
    ngrp = len(dps)

    def body(*refs):
        dp_refs = refs[:ngrp]
        df_ref, dxo_ref, x_ref, mod_ref, g_ref, w_ref, wf_ref, dx_ref, dmod_ref, dg_ref = refs[ngrp:]
        i = pl.program_id(0)

        @pl.when(i == 0)
        def _():
            dg_ref[...] = jnp.zeros_like(dg_ref)

        @pl.when(i % tiles_per_seq == 0)
        def _():
            dmod_ref[...] = jnp.zeros_like(dmod_ref)

        dh = _dot_nt(df_ref[...].astype(BF16), wf_ref[...])
        for g, dp_ref in enumerate(dp_refs):
            dh = dh + _dot_nt(dp_ref[...], w_ref[:, g * WG:(g + 1) * WG])
        dx, dsh, dsc, dg = _norm_mod_bwd(dh, x_ref[...], g_ref[...], mod_ref[0, 1:2, :])
        dx_ref[...] = dxo_ref[...] + dx
        dmod_ref[0, 0:1, :] += dsh
        dmod_ref[0, 1:2, :] += dsc
        dg_ref[...] += dg

    tok = pl.BlockSpec((TM, D), lambda i: (i, 0))
    vec = pl.BlockSpec((1, D), lambda i: (0, 0))
    return pl.pallas_call(
        body, grid=(T // TM,),
        in_specs=[pl.BlockSpec((TM, WG), lambda i: (i, 0))] * ngrp
        + [pl.BlockSpec((TM, LANE), lambda i: (i, 0)), tok, tok, pl.BlockSpec((1, 3, D), _mod_map), vec,
           pl.BlockSpec((D, IN_MAIN), lambda i: (0, 0)), pl.BlockSpec((D, LANE), lambda i: (0, 0))],
        out_specs=[tok, pl.BlockSpec((1, 2, D), _mod_map), vec],
        out_shape=[SDS((T, D), F32), SDS((nb, 2, D), F32), SDS((1, D), F32)],
        compiler_params=_cp("arbitrary"), name=name,
    )(*dps, dflog, dxo, x, mod3, g_pre, w_main, w_f)


def ada_fwd(c_all, w, b, name):
    n = w.shape[1]
    tn = n // 2

    def body(c_ref, w_ref, b_ref, o_ref):
        cv = c_ref[...]
        o_ref[...] = _dot((cv * jax.nn.sigmoid(cv)).astype(BF16), w_ref[...].astype(BF16)) + b_ref[...]

    R = c_all.shape[0]
    return pl.pallas_call(
        body, grid=(2,),
        in_specs=[pl.BlockSpec((R, D), lambda j: (0, 0)), pl.BlockSpec((D, tn), lambda j: (0, j)), pl.BlockSpec((1, tn), lambda j: (0, j))],
        out_specs=pl.BlockSpec((R, tn), lambda j: (0, j)), out_shape=SDS((R, n), F32),
        compiler_params=_cp("arbitrary"), name=name,
    )(c_all, w, b)


def ada_bwd(c_all, dmod, name):
    R, n = dmod.shape
    tn = n // 2

    def body(c_ref, d_ref, o_ref):
        cv = c_ref[...]
        o_ref[...] = _dot_tn((cv * jax.nn.sigmoid(cv)).astype(BF16), d_ref[...].astype(BF16))

    return pl.pallas_call(
        body, grid=(2,), in_specs=[pl.BlockSpec((R, D), lambda j: (0, 0)), pl.BlockSpec((R, tn), lambda j: (0, j))],
        out_specs=pl.BlockSpec((D, tn), lambda j: (0, j)), out_shape=SDS((D, n), F32),
        compiler_params=_cp("arbitrary"), name=name,
    )(c_all, dmod)


def _adam_math(w, g, m, v):
    m2 = ADAM_B1 * m + (1.0 - ADAM_B1) * g
    v2 = ADAM_B2 * v + (1.0 - ADAM_B2) * (g * g)
    m_hat = m2 / (1.0 - ADAM_B1 ** ADAM_STEP)
    v_hat = v2 / (1.0 - ADAM_B2 ** ADAM_STEP)
    delta = -ADAM_LR * (m_hat / (jnp.sqrt(v_hat) + ADAM_EPS) + ADAM_WD * w)
    return delta, m2, v2


def adam_update(w, g, m, v, tr, name):
    _, R, C = w.shape

    def body(w_ref, g_ref, m_ref, v_ref, d_ref, mo_ref, vo_ref):
        d_ref[0], mo_ref[0], vo_ref[0] = _adam_math(w_ref[0], g_ref[...], m_ref[0], v_ref[0])

    spec = pl.BlockSpec((1, tr, C), lambda i: (0, i, 0))
    gspec = pl.BlockSpec((tr, C), lambda i: (i, 0))
    return pl.pallas_call(
        body, grid=(R // tr,), in_specs=[spec, gspec, spec, spec], out_specs=[spec] * 3, out_shape=[SDS((1, R, C), F32)] * 3,
        compiler_params=_cp("arbitrary"), name=name,
    )(w, g, m, v)


def adam_update_halves(w, mine, other, m, v, cidx, tr, name):
    _, R, C = w.shape
    nh = R // 2 // tr

    def body(c_ref, w_ref, a_ref, b_ref, m_ref, v_ref, g_ref, d_ref, mo_ref, vo_ref):
        first_half = pl.program_id(0) < nh
        g = jnp.where(first_half == (c_ref[0] == 0), a_ref[...], b_ref[...])
        g_ref[0] = g
        d_ref[0], mo_ref[0], vo_ref[0] = _adam_math(w_ref[0], g, m_ref[0], v_ref[0])

    spec = pl.BlockSpec((1, tr, C), lambda i, c_ref: (0, i, 0))
    hspec = pl.BlockSpec((tr, C), lambda i, c_ref: (i % nh, 0))
    return pl.pallas_call(
        body,
        grid_spec=pltpu.PrefetchScalarGridSpec(num_scalar_prefetch=1, grid=(R // tr,), in_specs=[spec, hspec, hspec, spec, spec],
                                               out_specs=[spec] * 4),
        out_shape=[SDS((1, R, C), F32)] * 4, compiler_params=_cp("arbitrary"), name=name,
    )(cidx, w, mine, other, m, v)


def vec_adam(parts, w, m, v, name):
    P, C = parts.shape

    def body(p_ref, w_ref, m_ref, v_ref, g_ref, d_ref, mo_ref, vo_ref):
        g = jnp.sum(p_ref[...], axis=0, keepdims=True)
        g_ref[...] = g
        d_ref[...], mo_ref[...], vo_ref[...] = _adam_math(w_ref[...], g, m_ref[...], v_ref[...])

    return pl.pallas_call(body, out_shape=[SDS((1, C), F32)] * 4, compiler_params=_cp(), name=name)(parts, w, m, v)


def small_adam(parts, layout, ws, ms, vs, name):
    P, C = parts.shape
    k = len(layout)

    def body(*refs):
        p_ref = refs[0]
        w_refs, m_refs, v_refs = refs[1:1 + k], refs[1 + k:1 + 2 * k], refs[1 + 2 * k:1 + 3 * k]
        outs = refs[1 + 3 * k:]
        g_all = jnp.sum(p_ref[...], axis=0, keepdims=True)
        outs[4 * k][...] = g_all
        for n, (off, width) in enumerate(layout):
            g = g_all[:, off:off + width]
            outs[4 * n][...] = g
            outs[4 * n + 1][...], outs[4 * n + 2][...], outs[4 * n + 3][...] = _adam_math(
                w_refs[n][...], g, m_refs[n][...], v_refs[n][...])

    shapes = [SDS((1, width), F32) for _, width in layout for _ in range(4)] + [SDS((1, C), F32)]
    res = pl.pallas_call(body, out_shape=shapes, compiler_params=_cp(), name=name)(parts, *ws, *ms, *vs)
    return [tuple(res[4 * n:4 * n + 4]) for n in range(k)], res[4 * k]


HBM = pl.BlockSpec(memory_space=pltpu.HBM)
VMEM = pl.BlockSpec(memory_space=pltpu.VMEM)


def _place():
    x, y, c = lax.axis_index("x"), lax.axis_index("y"), lax.axis_index("c")
    return x, y, c, [(1 - x, y), (x, 1 - y), (1 - x, 1 - y)]


def all_gather8(xs, name):
    R, C = xs.shape

    def body(x_ref, out_ref, send_sems, recv_sems, local_sem):
        x, y, c, chips = _place()
        me, sibling = (x, y, c), (x, y, 1 - c)

        def slot(px, py, pc):
            return out_ref.at[4 * px + 2 * py + pc]

        def copy(k, block, to, src=None):
            return pltpu.make_async_remote_copy(
                src_ref=slot(*block) if src is None else src, dst_ref=slot(*block),
                send_sem=send_sems.at[k], recv_sem=recv_sems.at[k], device_id=to, device_id_type=MESH)

        mine = pltpu.make_async_copy(x_ref, slot(*me), local_sem)
        mine.start()
        first = [copy(0, me, sibling, src=x_ref)]
        first += [copy(1 + j, me, (*chip, c), src=x_ref) for j, chip in enumerate(chips)]
        for cp in first:
            cp.start()
        passed = [copy(4 + j, (*chip, c), sibling) for j, chip in enumerate(chips)]
        for j, chip in enumerate(chips):
            copy(1 + j, (*chip, c), me).wait_recv()
            passed[j].start()
        copy(0, sibling, me).wait_recv()
        for j, chip in enumerate(chips):
            copy(4 + j, (*chip, 1 - c), me).wait_recv()
        for cp in first + passed:
            cp.wait_send()
        mine.wait()

    return pl.pallas_call(
        body, out_shape=SDS((N_DEV, R, C), xs.dtype), in_specs=[VMEM], out_specs=VMEM,
        scratch_shapes=[pltpu.SemaphoreType.DMA((7,)), pltpu.SemaphoreType.DMA((7,)), pltpu.SemaphoreType.DMA],
        compiler_params=pltpu.CompilerParams(vmem_limit_bytes=VMEM_LIMIT), name=name,
    )(xs)


class ShardGather:
    def __init__(self, shapes, splits):
        self.shapes, self.splits, self.n = shapes, splits, len(shapes)

    def scratch(self):
        n = self.n
        return [pltpu.SemaphoreType.DMA((n, 6)), pltpu.SemaphoreType.DMA((n, 6)), pltpu.SemaphoreType.DMA((n,))]

    def out_shapes(self, dtype):
        return [SDS((N_SHARD,) + tuple(s), dtype) for s in self.shapes]

    def _half(self, ref, k, cc):
        lo, hi = (0, self.splits[k]) if cc == 0 else (self.splits[k], self.shapes[k][0])
        return ref.at[pl.ds(lo, hi - lo)]

    def _phase(self, w_refs, o_refs, sems, finish):
        send_sems, recv_sems, local_sems = sems
        x, y, c, chips = _place()
        sibling = (x, y, 1 - c)
        me_s = 2 * x + y

        def rcopy(src, dst, k, s, to):
            return pltpu.make_async_remote_copy(src_ref=src, dst_ref=dst, send_sem=send_sems.at[k, s],
                                                recv_sem=recv_sems.at[k, s], device_id=to, device_id_type=MESH)

        for cc in (0, 1):
            @pl.when(c == cc)
            def _():
                local = [pltpu.make_async_copy(w_refs[k], o_refs[k].at[me_s], local_sems.at[k]) for k in range(self.n)]
                first = [rcopy(self._half(w_refs[k], k, cc), self._half(o_refs[k].at[me_s], k, cc), k, j, (*chip, c))
                         for k in range(self.n) for j, chip in enumerate(chips)]
                if not finish:
                    for cp in local + first:
                        cp.start()
                    return
                passed = []
                for k in range(self.n):
                    for j, chip in enumerate(chips):
                        land = self._half(o_refs[k].at[2 * chip[0] + chip[1]], k, cc)
                        rcopy(land, land, k, j, (*chip, c)).wait_recv()
                        f = rcopy(land, land, k, 3 + j, sibling)
                        f.start()
                        passed.append(f)
                for k in range(self.n):
                    for j, chip in enumerate(chips):
                        other = self._half(o_refs[k].at[2 * chip[0] + chip[1]], k, 1 - cc)
                        rcopy(other, other, k, 3 + j, sibling).wait_recv()
                for s in first + passed:
                    s.wait_send()
                for cp in local:
                    cp.wait()

    def start(self, w_refs, o_refs, sems):
        self._phase(w_refs, o_refs, sems, False)

    def finish(self, w_refs, o_refs, sems):
        self._phase(w_refs, o_refs, sems, True)


def all_gather_shards(ws, splits, name):
    n = len(ws)
    plan = ShardGather([w.shape for w in ws], splits)

    def body(*refs):
        plan.start(refs[:n], refs[n:2 * n], refs[2 * n:])
        plan.finish(refs[:n], refs[n:2 * n], refs[2 * n:])

    return pl.pallas_call(
        body, out_shape=plan.out_shapes(ws[0].dtype), in_specs=[HBM] * n, out_specs=[HBM] * n,
        scratch_shapes=plan.scratch(), name=name,
    )(*ws)


def sibling_send_half(gs, name):
    n = len(gs)

    def body(*refs):
        g_refs, o_refs = refs[:n], refs[n:2 * n]
        send_sems, recv_sems = refs[2 * n:]
        x, y, c, _ = _place()
        cps = []
        for k in range(n):
            hr = gs[k].shape[1] // 2
            src = g_refs[k].at[:, pl.ds(pl.multiple_of((1 - c) * hr, 8), hr)]
            cp = pltpu.make_async_remote_copy(src_ref=src, dst_ref=o_refs[k], send_sem=send_sems.at[k], recv_sem=recv_sems.at[k],
                                              device_id=(x, y, 1 - c), device_id_type=MESH)
            cp.start()
            cps.append(cp)
        for cp in cps:
            cp.wait()

    return pl.pallas_call(
        body, out_shape=[SDS((N_SHARD, g.shape[1] // 2, g.shape[2]), g.dtype) for g in gs], in_specs=[HBM] * n, out_specs=[HBM] * n,
        scratch_shapes=[pltpu.SemaphoreType.DMA((n,)), pltpu.SemaphoreType.DMA((n,))], name=name,
    )(*gs)


def _scatter_copies(h_refs, o_refs, send_sems, recv_sems):
    _, _, c, chips = _place()
    return [pltpu.make_async_remote_copy(
        src_ref=h_refs[k].at[2 * chip[0] + chip[1]], dst_ref=o_refs[k].at[j], send_sem=send_sems.at[k, j],
        recv_sem=recv_sems.at[k, j], device_id=(*chip, c), device_id_type=MESH)
        for k in range(len(h_refs)) for j, chip in enumerate(chips)]


def chip_scatter(hs, name):
    n = len(hs)

    def body(*refs):
        cps = _scatter_copies(refs[:n], refs[n:2 * n], *refs[2 * n:])
        for cp in cps:
            cp.start()
        for cp in cps:
            cp.wait()

    return pl.pallas_call(
        body, out_shape=[SDS((3,) + h.shape[1:], h.dtype) for h in hs], in_specs=[HBM] * n, out_specs=[HBM] * n,
        scratch_shapes=[pltpu.SemaphoreType.DMA((n, 3)), pltpu.SemaphoreType.DMA((n, 3))], name=name,
    )(*hs)


def sibling_swap(ghs, name):
    n = len(ghs)

    def body(*refs):
        g_refs, o_refs = refs[:n], refs[n:2 * n]
        send_sems, recv_sems = refs[2 * n:]
        x, y, c, _ = _place()
        cps = []
        for k in range(n):
            cp = pltpu.make_async_remote_copy(src_ref=g_refs[k], dst_ref=o_refs[k], send_sem=send_sems.at[k],
                                              recv_sem=recv_sems.at[k], device_id=(x, y, 1 - c), device_id_type=MESH)
            cp.start()
            cps.append(cp)
        for cp in cps:
            cp.wait()

    return pl.pallas_call(
        body, out_shape=[SDS(g.shape, g.dtype) for g in ghs], in_specs=[HBM] * n, out_specs=[HBM] * n,
        scratch_shapes=[pltpu.SemaphoreType.DMA((n,)), pltpu.SemaphoreType.DMA((n,))], name=name,
    )(*ghs)


def pair_sums(gs, ras, cidx, name):
    n = len(gs)
    halves = [(g.shape[1] // 2, g.shape[2]) for g in gs]

    def body(c_ref, *refs):
        for g_ref, a_ref, o_ref in zip(refs[:n], refs[n:2 * n], refs[2 * n:]):
            o_ref[...] = (g_ref[...] + a_ref[...]).astype(BF16)

    mine = [pl.BlockSpec((1, hr, cols), lambda s, c_ref: (s, c_ref[0], 0)) for hr, cols in halves]
    whole = [pl.BlockSpec((1, hr, cols), lambda s, c_ref: (s, 0, 0)) for hr, cols in halves]
    return pl.pallas_call(
        body,
        grid_spec=pltpu.PrefetchScalarGridSpec(num_scalar_prefetch=1, grid=(N_SHARD,), in_specs=mine + whole, out_specs=whole),
        out_shape=[SDS((N_SHARD, hr, cols), BF16) for hr, cols in halves], compiler_params=_cp("arbitrary"), name=name,
    )(cidx, *gs, *ras)


def chip_sums_total(hs, rbs, sidx, name):
    n = len(hs)
    halves = [h.shape[1:] for h in hs]

    def body(s_ref, *refs):
        for h_ref, r_ref, o_ref in zip(refs[:n], refs[n:2 * n], refs[2 * n:]):
            o_ref[...] = ((h_ref[0].astype(F32) + r_ref[0].astype(F32)) + r_ref[1].astype(F32)) + r_ref[2].astype(F32)

    return pl.pallas_call(
        body,
        grid_spec=pltpu.PrefetchScalarGridSpec(
            num_scalar_prefetch=1, grid=(1,),
            in_specs=[pl.BlockSpec((1, hr, cols), lambda i, s_ref: (s_ref[0], 0, 0)) for hr, cols in halves]
            + [pl.BlockSpec((3, hr, cols), lambda i, s_ref: (0, 0, 0)) for hr, cols in halves],
            out_specs=[pl.BlockSpec((hr, cols), lambda i, s_ref: (0, 0)) for hr, cols in halves]),
        out_shape=[SDS((hr, cols), F32) for hr, cols in halves], compiler_params=_cp("arbitrary"), name=name,
    )(sidx, *hs, *rbs)


def _shard_cols(g, n_valid):
    r = g.shape[0]
    return g[:, :n_valid].reshape(r, N_SHARD, n_valid // N_SHARD).transpose(1, 0, 2)


def _unshard_cols(o, pad_to):
    _, r, n = o.shape
    full = o.transpose(1, 0, 2).reshape(r, N_SHARD * n)
    return jnp.pad(full, ((0, 0), (0, pad_to - N_SHARD * n)))


def _rows_of_tiles(t):
    B, H, S = t.shape
    return t.reshape(B, H, S // FT, 1, FT)


def mixer_fwd(x1, mod3, g_pre, w_main, w_f, b_forget_pad, goa, gob, w_out, g_post, tabs, nb, gather=None):
    hmix, pa, pb, flog = mixer_proj(x1, mod3, g_pre, w_main, w_f, *tabs, name="mixer_proj")
    out_a, lse_a = band_fwd(pa, name="band_fwd")
    F = forget_cumsum(flog.reshape(nb, SEQ, LANE), b_forget_pad, name="forget_cumsum")
    Fh = F[:, :, :NH].transpose(0, 2, 1)
    fblk = Fh.reshape(nb, NH, SEQ // FB, 1, FB)
    frow = _rows_of_tiles(Fh)
    (out_b, lse_b), gathered = fox_fwd(pb, Fh.reshape(nb, NH, SEQ // FOX_QB, 1, FOX_QB), frow, name="fox_fwd", gather=gather)
    x2, merged, y0m = mixer_out_fwd(out_a, out_b, goa, gob, w_out, g_post, x1, mod3, name="mixer_out_fwd")
    res = dict(hmix=hmix, flog=flog, pa=pa, pb=pb, out_a=out_a, lse_a=lse_a, fblk=fblk, frow=frow, out_b=out_b,
               lrow=_rows_of_tiles(lse_b.reshape(nb, NH, SEQ)), merged=merged, y0m=y0m)
    return x2, res, gathered


def mixer_bwd(dx2, x1, mod3, g_pre, w_main, w_f, b_forget_pad, goa, gob, w_out, g_post, tabs, res, nb):
    T = nb * SEQ
    dy0m, doa, dob, dmgate, dg_post, dgoa, dgob, dvec_b = mixer_out_bwd(
        dx2, res["y0m"], mod3, g_post, w_out, res["out_a"], res["out_b"], goa, gob, name="mixer_out_bwd")
    dqa, dka, dva = band_bwd(res["pa"], doa, res["out_a"], res["lse_a"], *tabs, name="band_bwd")
    drow = _rows_of_tiles(dvec_b[:, :NH].reshape(nb, SEQ, NH).transpose(0, 2, 1))
    dqb, dkb, dvb, dfq, dfk = fox_bwd(res["pb"], dob, res["lrow"], drow, res["fblk"], res["frow"], name="fox_bwd")
    dF = (dfq.reshape(nb, NH, SEQ) + dfk.reshape(nb, NH, SEQ)).transpose(0, 2, 1)
    dF = jnp.pad(dF, ((0, 0), (0, 0), (0, LANE - NH)))
    dflog, dbf = forget_cumsum_bwd(dF, res["flog"].reshape(nb, SEQ, LANE), b_forget_pad, name="forget_cumsum_bwd")
    dflog = dflog.reshape(T, LANE)
    dps = (dqa, dka, dva, dqb, dkb, dvb)
    dx1, dmod2, dg_pre = mixer_proj_bwd(dps, dflog, dx2, x1, mod3, g_pre, w_main, w_f, name="mixer_proj_bwd")
    g_main = matmul_tn_cols(res["hmix"], dps, 1024, name="grad_w_in")
    g_f = matmul_tn(res["hmix"], dflog.astype(BF16), D, LANE, 1024, name="grad_w_forget")
    g_out = matmul_tn(res["merged"], dy0m, D, D, 1024, name="grad_w_out")
    dmod3 = jnp.concatenate([dmod2, dmgate], axis=1)
    return dx1, dmod3, dict(g_pre=dg_pre, g_post=dg_post, goa=dgoa, gob=dgob, b_forget=dbf[:, :NH],
                            w_in=jnp.concatenate([g_main, g_f[:, :NH]], axis=1), w_out=g_out)


def ffn_grads(h, dy0, act, dgate, dup, pre, reduce=None):
    g_gate = matmul_tn(h, dgate, D, DFF_PAD, 1024, name=pre + "_grad_gate")
    if reduce is None:
        g_up = matmul_tn(h, dup, D, DFF_PAD, 1024, name=pre + "_grad_up")
        g_down = matmul_tn(act, dy0, FF_TN, D, 1024, name=pre + "_grad_down", rows=DFF)
        return (g_gate, g_up, g_down), {}
    hs_gate = reduce("gate", g_gate)
    g_up, rb_gate = matmul_tn(h, dup, D, DFF_PAD, 1024, name=pre + "_grad_up", scatter=hs_gate)
    hs_up = reduce("up", g_up)
    g_down, rb_up = matmul_tn(act, dy0, FF_TN, D, 1024, name=pre + "_grad_down", scatter=hs_up, rows=DFF)
    return (g_gate, g_up, g_down), {"gate": (hs_gate[0], rb_gate[0]), "up": (hs_up[0], rb_up[0])}


def local_step(x0, tgt, pos_col, mod, wfull, p, late_weights=None, last_weights=None, early_grads=None, last_reduce=None):
    T = x0.shape[0]
    nb = T // SEQ
    mod_ff1, mod_mix, mod_ff2 = mod[:, 0:3], mod[:, 3:6], mod[:, 6:9]
    tabs = rope_tables(pos_col, name="rope_tables")
    bf_pad = jnp.pad(p["b_forget"], ((0, 0), (0, LANE - NH)))

    (x1, h1, gate1, up1, y01), gathered = ffn_fwd(
        x0, mod_ff1, p["g_pre_ff1"], p["g_post_ff1"], wfull["w_ff1_gate"], wfull["w_ff1_up"], wfull["w_ff1_down"], 0.5,
        name="ff1_fwd", gather=None if late_weights is None else late_weights[:2])
    if late_weights is not None:
        wfull = {**wfull, **late_weights[2](gathered)}
    x2, res, gathered = mixer_fwd(x1, mod_mix, p["g_pre_mix"], wfull["w_main"], wfull["w_f"], bf_pad, p["g_out_a"],
                                  p["g_out_b"], wfull["w_out"], p["g_post_mix"], tabs, nb,
                                  gather=None if last_weights is None else last_weights[:2])
    if last_weights is not None:
        wfull = {**wfull, **last_weights[2](gathered)}
    (x3, h2, gate2, up2, y02), _ = ffn_fwd(x2, mod_ff2, p["g_pre_ff2"], p["g_post_ff2"], wfull["w_ff2_gate"],
                                           wfull["w_ff2_up"], wfull["w_ff2_down"], 0.5, name="ff2_fwd")

    (dx2, dy02, act2, dgate2, dup2, dmod_ff2, dgpre2, dgpost2), (loss_part,) = ffn_bwd(
        x3, x2, y02, mod_ff2, p["g_pre_ff2"], p["g_post_ff2"], gate2, up2, wfull["w_ff2_gate"], wfull["w_ff2_up"],
        wfull["w_ff2_down"], 0.5, name="ff2_bwd", target=tgt)
    gw = {}
    (gw["w_ff2_gate"], gw["w_ff2_up"], gw["w_ff2_down"]), _ = ffn_grads(h2, dy02, act2, dgate2, dup2, "ff2")
    dx1, dmod_mix, gmix = mixer_bwd(dx2, x1, mod_mix, p["g_pre_mix"], wfull["w_main"], wfull["w_f"], bf_pad, p["g_out_a"],
                                    p["g_out_b"], wfull["w_out"], p["g_post_mix"], tabs, res, nb)
    gw["w_in"], gw["w_out"] = gmix["w_in"], gmix["w_out"]
    (dx0, dy01, act1, dgate1, dup1, dmod_ff1, dgpre1, dgpost1), scattered = ffn_bwd(
        dx1, x0, y01, mod_ff1, p["g_pre_ff1"], p["g_post_ff1"], gate1, up1, wfull["w_ff1_gate"], wfull["w_ff1_up"],
        wfull["w_ff1_down"], 0.5, name="ff1_bwd", scatter=None if early_grads is None else early_grads(gw))
    (gw["w_ff1_gate"], gw["w_ff1_up"], gw["w_ff1_down"]), chained = ffn_grads(h1, dy01, act1, dgate1, dup1, "ff1", last_reduce)
    dmod = jnp.concatenate([dmod_ff1, dmod_mix, dmod_ff2], axis=1).reshape(nb, 9 * D)
    small = dict(g_pre_ff1=dgpre1, g_post_ff1=dgpost1, g_pre_mix=gmix["g_pre"], g_post_mix=gmix["g_post"], g_pre_ff2=dgpre2,
                 g_post_ff2=dgpost2, g_out_a=gmix["goa"], g_out_b=gmix["gob"], b_forget=gmix["b_forget"])
    return loss_part, dx0, dmod, gw, small, scattered, chained


def kernel(x, c, positions, w_ada, b_ada, g_pre_ff1, g_post_ff1, w_ff1_gate, w_ff1_up, w_ff1_down, g_pre_mix, g_post_mix, w_in, b_forget, g_out_a, g_out_b, w_out, g_pre_ff2, g_post_ff2, w_ff2_gate, w_ff2_up, w_ff2_down, loss_target, m_w_ada, m_b_ada, m_g_pre_ff1, m_g_post_ff1, m_w_ff1_gate, m_w_ff1_up, m_w_ff1_down, m_g_pre_mix, m_g_post_mix, m_w_in, m_b_forget, m_g_out_a, m_g_out_b, m_w_out, m_g_pre_ff2, m_g_post_ff2, m_w_ff2_gate, m_w_ff2_up, m_w_ff2_down, v_w_ada, v_b_ada, v_g_pre_ff1, v_g_post_ff1, v_w_ff1_gate, v_w_ff1_up, v_w_ff1_down, v_g_pre_mix, v_g_post_mix, v_w_in, v_b_forget, v_g_out_a, v_g_out_b, v_w_out, v_g_pre_ff2, v_g_post_ff2, v_w_ff2_gate, v_w_ff2_up, v_w_ff2_down):
    args = dict(locals())
    nb = x.shape[0]
    T = nb * SEQ
    ax, ay, ac = lax.axis_index("x"), lax.axis_index("y"), lax.axis_index("c")
    shard = 2 * ax + ay
    cidx = jnp.reshape(ac, (1,)).astype(jnp.int32)
    sidx = jnp.reshape(shard, (1,)).astype(jnp.int32)

    big = ["w_ff1_gate", "w_ff1_up", "w_ff1_down", "w_in", "w_out", "w_ff2_gate", "w_ff2_up", "w_ff2_down"]
    vecs = ["g_pre_ff1", "g_post_ff1", "g_pre_mix", "g_post_mix", "g_pre_ff2", "g_post_ff2"]

    first, late = big[:3], big[3:]
    splits = {n: -(-(args[n].shape[1] // 2) // BF16_ROW_TILE) * BF16_ROW_TILE for n in big}

    def assemble(names, gathered):
        out = {}
        for n, o in zip(names, gathered):
            if n.endswith("gate") or n.endswith("up"):
                out[n] = _unshard_cols(o, DFF_PAD)
            elif n.endswith("down"):
                out[n] = jnp.pad(o.reshape(DFF, D), ((0, DFF_PAD - DFF), (0, 0)))
            elif n == "w_in":
                full = _unshard_cols(o, IN_COLS)
                out["w_main"] = full[:, :IN_MAIN]
                out["w_f"] = jnp.pad(full[:, IN_MAIN:], ((0, 0), (0, LANE - NH)))
            else:
                out[n] = o.reshape(D, D)
        return out

    wfull = assemble(first, all_gather_shards([args[n][0].astype(BF16) for n in first], [splits[n] for n in first],
                                              name="all_gather_weights"))
    def gather_plan(names):
        return ([args[n][0].astype(BF16) for n in names], [splits[n] for n in names], functools.partial(assemble, names))

    late_weights, last_weights = gather_plan(late[:2]), gather_plan(late[2:])

    ncol = w_ada.shape[2]
    c_all = all_gather8(c, name="all_gather_c").reshape(N_DEV * nb, D)
    b_loc = lax.dynamic_slice(b_ada, (0, shard * ncol), (1, ncol))
    mod_loc = ada_fwd(c_all, w_ada[0], b_loc, name="ada_fwd")
    mod_g = all_gather8(mod_loc, name="all_gather_mod")
    row0 = (4 * ax + 2 * ay + ac) * nb
    mod_rows = lax.dynamic_slice(mod_g, (0, row0, 0), (N_DEV, nb, ncol))
    mod = jnp.concatenate([mod_rows[2 * s] for s in range(N_SHARD)], axis=-1).reshape(nb, 9, D)

    small_in = dict(g_pre_ff1=g_pre_ff1, g_post_ff1=g_post_ff1, g_pre_mix=g_pre_mix, g_post_mix=g_post_mix, g_pre_ff2=g_pre_ff2,
                    g_post_ff2=g_post_ff2, g_out_a=g_out_a, g_out_b=g_out_b, b_forget=b_forget)
    def shard_blocked(n, g):
        if n.endswith("gate") or n.endswith("up"):
            return _shard_cols(g, DFF)
        if n.endswith("down"):
            return g.reshape(N_SHARD, DFF // N_SHARD, D)
        if n == "w_in":
            return _shard_cols(g, IN_COLS)
        return g.reshape(N_SHARD, D // N_SHARD, D)

    def chip_sums(names, gw, tag):
        gsb = [shard_blocked(n, gw[n]) for n in names]
        ras = sibling_send_half(gsb, name="grad_sibling_send_" + tag)
        return pair_sums(gsb, ras, cidx, name="grad_pair_sum_" + tag)

    hs = {}

    def early_grads(gw):
        hs.update(zip(late, chip_sums(late, gw, "late")))
        return [hs[n] for n in late]

    def last_reduce(which, g):
        return chip_sums(["w_ff1_" + which], {"w_ff1_" + which: g}, which)

    loss_part, dx0, dmod, gw, small, rbs_late, chained = local_step(
        x.reshape(T, D), loss_target.reshape(T, D), positions.reshape(T, 1), mod, wfull, small_in, late_weights, last_weights,
        early_grads, last_reduce)

    row6 = jnp.concatenate([small["g_out_a"], small["g_out_b"]], axis=1)
    row7 = jnp.concatenate([small["b_forget"], loss_part[0:1, 0:1], jnp.zeros((1, D - NH - 1), F32)], axis=1)
    pack = jnp.concatenate([small[n] for n in vecs] + [row6, row7], axis=0)
    tail = jnp.concatenate([dmod, jnp.pad(pack.reshape(1, 8 * D), ((0, 0), (0, D)))], axis=0)
    tail_all = all_gather8(tail, name="all_gather_tail")
    dmod_all = tail_all[:, :nb].reshape(N_DEV * nb, 9 * D)
    packed = tail_all[:, nb, :8 * D]
    dmod_loc = lax.dynamic_slice(dmod_all, (0, shard * ncol), (N_DEV * nb, ncol))
    g_w_ada = ada_bwd(c_all, dmod_loc, name="ada_bwd")

    rbs = dict(zip(late, rbs_late))
    for which, (h, rb) in chained.items():
        hs["w_ff1_" + which], rbs["w_ff1_" + which] = h, rb
    hs["w_ff1_down"] = chip_sums(["w_ff1_down"], gw, "down")[0]
    rbs["w_ff1_down"] = chip_scatter([hs["w_ff1_down"]], name="grad_chip_scatter")[0]
    ghs = []
    for part, names in enumerate((big[:4], big[4:])):
        ghs += chip_sums_total([hs[n] for n in names], [rbs[n] for n in names], sidx, name=f"grad_chip_sum_{part}")
    theirs = sibling_swap(ghs, name="grad_sibling_swap")


    names = vecs + ["g_out_a", "g_out_b", "b_forget"]
    layout = [(i * D, D) for i in range(len(vecs))] + [(6 * D, WG), (6 * D + WG, WG), (7 * D, NH)]
    per_param, packed_sum = small_adam(packed, layout, [args[n] for n in names], [args["m_" + n] for n in names],
                                       [args["v_" + n] for n in names], name="adam_small")
    outs = dict(grad={}, delta={}, new_m={}, new_v={})
    for n, (g, d, m2, v2) in zip(names, per_param):
        outs["grad"][n], outs["delta"][n], outs["new_m"][n], outs["new_v"][n] = g, d, m2, v2
    loss = packed_sum[0, 7 * D + NH]
    outs["grad"]["b_ada"], outs["delta"]["b_ada"], outs["new_m"]["b_ada"], outs["new_v"]["b_ada"] = vec_adam(
        dmod_all, b_ada, m_b_ada, v_b_ada, name="adam_b_ada")

    for n, mine, other in zip(big, ghs, theirs):
        tr = 128 if mine.shape[0] % 128 == 0 else mine.shape[0]
        outs["grad"][n], outs["delta"][n], outs["new_m"][n], outs["new_v"][n] = adam_update_halves(
            args[n], mine, other, args["m_" + n], args["v_" + n], cidx, tr, name="adam_" + n)
    outs["delta"]["w_ada"], outs["new_m"]["w_ada"], outs["new_v"]["w_ada"] = adam_update(
        w_ada, g_w_ada, m_w_ada, v_w_ada, 128, name="adam_w_ada")
    outs["grad"]["w_ada"] = g_w_ada[None]

    order = ["w_ada", "b_ada", "g_pre_ff1", "g_post_ff1", "w_ff1_gate", "w_ff1_up", "w_ff1_down", "g_pre_mix", "g_post_mix", "w_in",
             "b_forget", "g_out_a", "g_out_b", "w_out", "g_pre_ff2", "g_post_ff2", "w_ff2_gate", "w_ff2_up", "w_ff2_down"]
    result = [loss, dx0.reshape(nb, SEQ, D)]
    for kind in ("grad", "delta", "new_m", "new_v"):
        result += [outs[kind][n] for n in order]
    return tuple(result)
```
